```python
import math
import jax, jax.numpy as jnp
from jax import lax
import numpy as np

D_MODEL = 1024
BATCH = 8
SEQ = 8192
DEPTH = 2

N_EVEN = (DEPTH + 1) // 2
N_ODD = DEPTH // 2
HEAD_DIM = 64
CONV_GROUPS = 8
CONV_WIDTH = CONV_GROUPS * HEAD_DIM
ATTN_HEADS = 8
ATTN_WIDTH = ATTN_HEADS * HEAD_DIM
MIX_IN = 3 * CONV_WIDTH + 3 * ATTN_WIDTH
MIX_OUT = CONV_WIDTH + ATTN_WIDTH
SHORT_CONV_K = 3
DILATED_PAIRS = ((128, 1), (512, 4), (2048, 16))
REL_BUCKETS = 32
REL_MAX_DIST = 2048
LRU_WIDTH = D_MODEL
LRU_BLOCKS = 4
LRU_BLOCK = LRU_WIDTH // LRU_BLOCKS
REC_CONV_K = 4
LRU_C = 8.0
D_FF = 2816
PLE_DIM = 256
EPS = 1e-6

kernel_name = "hybrid_conv_dilattn_rglru_macaron"


def rms_norm(x, gain):
    xf = x.astype(jnp.float32)
    y = xf * lax.rsqrt(jnp.mean(xf * xf, axis=-1, keepdims=True) + EPS)
    return (y * gain.astype(jnp.float32)).astype(x.dtype)


def swiglu(h, w_gate, w_up, w_down):
    return (jax.nn.silu(h @ w_gate) * (h @ w_up)) @ w_down


def causal_depthwise_conv(x, w):
    k_taps = w.shape[0]
    s = x.shape[1]
    xp = jnp.pad(x, ((0, 0), (k_taps - 1, 0), (0, 0)))
    y = xp[:, 0:s] * w[0]
    for j in range(1, k_taps):
        y = y + xp[:, j:j + s] * w[j]
    return y


def rel_bucket(dist):
    max_exact = REL_BUCKETS // 2
    n = jnp.maximum(dist, 1).astype(jnp.float32)
    large = max_exact + (jnp.log(n / max_exact) / math.log(REL_MAX_DIST / max_exact)
                         * (REL_BUCKETS - max_exact)).astype(jnp.int32)
    large = jnp.minimum(large, REL_BUCKETS - 1)
    return jnp.where(dist < max_exact, dist, large)


def dilated_branch(q, k, v, rel_bias, window, dilation):
    b_, s_, h_, dh = q.shape
    d = dilation
    nw = window // d
    sub_len = s_ // d
    nb = -(-sub_len // nw)
    lp = nb * nw

    def strided(t, front):
        t = t.reshape(b_, sub_len, d, h_, dh)
        return jnp.pad(t, ((0, 0), (front, lp - sub_len), (0, 0), (0, 0), (0, 0)))

    q_b = strided(q, 0).reshape(b_, nb, nw, d, h_, dh)
    k_p = strided(k, nw).reshape(b_, nb + 1, nw, d, h_, dh)
    v_p = strided(v, nw).reshape(b_, nb + 1, nw, d, h_, dh)
    k_b = jnp.concatenate([k_p[:, :-1], k_p[:, 1:]], axis=2)
    v_b = jnp.concatenate([v_p[:, :-1], v_p[:, 1:]], axis=2)

    scores = jnp.einsum('bnqrhe,bnkrhe->bnrhqk', q_b, k_b).astype(jnp.float32) * (dh ** -0.5)
    qi = jnp.arange(nw)[:, None]
    kj = jnp.arange(2 * nw)[None, :]
    dist = qi + nw - kj
    key_pos = jnp.arange(nb)[:, None, None] * nw + kj[None] - nw
    valid = (dist >= 0)[None] & (dist <= nw)[None] & (key_pos >= 0)
    bucket = rel_bucket(jnp.clip(dist, 0, nw) * d)
    bias = jnp.transpose(rel_bias[bucket].astype(jnp.float32), (2, 0, 1))
    logits = jnp.where(valid[None, :, None, None], scores + bias, -jnp.inf)
    lse = jax.nn.logsumexp(logits, axis=-1)
    probs = jnp.exp(logits - lse[..., None])
    out = jnp.einsum('bnrhqk,bnkrhe->bnqrhe', probs.astype(v.dtype), v_b)
    out = out.reshape(b_, lp, d, h_, dh)[:, :sub_len].reshape(b_, s_, h_, dh)
    lse = jnp.transpose(lse, (0, 1, 4, 2, 3)).reshape(b_, lp, d, h_)[:, :sub_len].reshape(b_, s_, h_)
    return out, lse


def hybrid_mixer(h, w_in, conv_w, q_gain, k_gain, rel_bias, w_out):
    b_, s_, _ = h.shape
    z = h @ w_in
    cuts = np.cumsum([CONV_WIDTH, CONV_WIDTH, CONV_WIDTH, ATTN_WIDTH, ATTN_WIDTH])
    g_b, g_c, c_x, q, k, v = jnp.split(z, cuts, axis=-1)
    y_conv = g_b * causal_depthwise_conv(g_c * c_x, conv_w)
    qh = rms_norm(q.reshape(b_, s_, ATTN_HEADS, HEAD_DIM), q_gain)
    kh = rms_norm(k.reshape(b_, s_, ATTN_HEADS, HEAD_DIM), k_gain)
    vh = v.reshape(b_, s_, ATTN_HEADS, HEAD_DIM)
    outs = []
    lses = []
    for window, dil in DILATED_PAIRS:
        o, l = dilated_branch(qh, kh, vh, rel_bias, window, dil)
        outs.append(o)
        lses.append(l)
    wts = jax.nn.softmax(jnp.stack(lses), axis=0)
    y_attn = jnp.sum(wts[..., None] * jnp.stack(outs).astype(jnp.float32), axis=0)
    y_attn = y_attn.astype(h.dtype).reshape(b_, s_, ATTN_WIDTH)
    return jnp.concatenate([y_conv, y_attn], axis=-1) @ w_out


def rg_lru(xb, wa, ba, wx, bx, lam):
    b_, s_, _ = xb.shape
    xf = xb.astype(jnp.float32)
    xr = xf.reshape(b_, s_, LRU_BLOCKS, LRU_BLOCK)
    gate_a = jnp.einsum('bsgi,gij->bsgj', xr, wa.astype(jnp.float32)).reshape(b_, s_, LRU_WIDTH) + ba.astype(jnp.float32)
    gate_x = jnp.einsum('bsgi,gij->bsgj', xr, wx.astype(jnp.float32)).reshape(b_, s_, LRU_WIDTH) + bx.astype(jnp.float32)
    log_a = -LRU_C * jax.nn.sigmoid(gate_a) * jax.nn.softplus(-lam.astype(jnp.float32))
    a = jnp.exp(log_a)
    u = jnp.sqrt(-jnp.expm1(2.0 * log_a)) * (jax.nn.sigmoid(gate_x) * xf)

    def combine(left, right):
        a1, b1 = left
        a2, b2 = right
        return a1 * a2, a2 * b1 + b2

    _, hs = lax.associative_scan(combine, (a, u), axis=1)
    return hs.astype(xb.dtype)


def recurrent_mixer(h, w_in, conv_w, conv_b, wa, ba, wx, bx, lam, w_out):
    z = h @ w_in
    xb, yb = jnp.split(z, 2, axis=-1)
    xb = causal_depthwise_conv(xb, conv_w) + conv_b
    return (rg_lru(xb, wa, ba, wx, bx, lam) * jax.nn.gelu(yb)) @ w_out


def _fwd_setup_inputs(seed: int = 0) -> dict:
    key = jax.random.key(seed)
    ks = iter(jax.random.split(key, 40))
    f32 = jnp.float32

    def dense(shape, fan_in):
        return jax.random.normal(next(ks), shape, f32) * (fan_in ** -0.5)

    def gain(shape):
        return 1.0 + 0.02 * jax.random.normal(next(ks), shape, f32)

    def small(shape, scale=0.02):
        return scale * jax.random.normal(next(ks), shape, f32)

    u = jax.random.uniform(next(ks), (N_ODD, LRU_WIDTH), f32, 0.9, 0.999)
    s_base = u ** (1.0 / LRU_C)
    lru_lambda = jnp.log(s_base) - jnp.log1p(-s_base)

    return {
        "x": jax.random.normal(next(ks), (BATCH, SEQ, D_MODEL), f32),
        "p": jax.random.normal(next(ks), (DEPTH, BATCH, SEQ, PLE_DIM), f32),
        "rel_bias": small((REL_BUCKETS, ATTN_HEADS), 0.1),
        "ffn1_norm": gain((DEPTH, D_MODEL)),
        "ffn1_w_gate": dense((DEPTH, D_MODEL, D_FF), D_MODEL),
        "ffn1_w_up": dense((DEPTH, D_MODEL, D_FF), D_MODEL),
        "ffn1_w_down": dense((DEPTH, D_FF, D_MODEL), D_FF),
        "mix_norm": gain((DEPTH, D_MODEL)),
        "hyb_w_in": dense((N_EVEN, D_MODEL, MIX_IN), D_MODEL),
        "hyb_conv_w": dense((N_EVEN, SHORT_CONV_K, CONV_WIDTH), SHORT_CONV_K),
        "hyb_q_gain": gain((N_EVEN, HEAD_DIM)),
        "hyb_k_gain": gain((N_EVEN, HEAD_DIM)),
        "hyb_w_out": dense((N_EVEN, MIX_OUT, D_MODEL), MIX_OUT),
        "rec_w_in": dense((N_ODD, D_MODEL, 2 * LRU_WIDTH), D_MODEL),
        "rec_conv_w": dense((N_ODD, REC_CONV_K, LRU_WIDTH), REC_CONV_K),
        "rec_conv_b": small((N_ODD, LRU_WIDTH)),
        "lru_wa": dense((N_ODD, LRU_BLOCKS, LRU_BLOCK, LRU_BLOCK), LRU_BLOCK),
        "lru_ba": small((N_ODD, LRU_WIDTH)),
        "lru_wx": dense((N_ODD, LRU_BLOCKS, LRU_BLOCK, LRU_BLOCK), LRU_BLOCK),
        "lru_bx": small((N_ODD, LRU_WIDTH)),
        "lru_lambda": lru_lambda,
        "rec_w_out": dense((N_ODD, LRU_WIDTH, D_MODEL), LRU_WIDTH),
        "ffn2_norm": gain((DEPTH, D_MODEL)),
        "ffn2_w_gate": dense((DEPTH, D_MODEL, D_FF), D_MODEL),
        "ffn2_w_up": dense((DEPTH, D_MODEL, D_FF), D_MODEL),
        "ffn2_w_down": dense((DEPTH, D_FF, D_MODEL), D_FF),
        "ple_norm": gain((DEPTH, D_MODEL)),
        "ple_w_gate": dense((DEPTH, D_MODEL, D_MODEL), D_MODEL),
        "ple_w_proj": dense((DEPTH, PLE_DIM, D_MODEL), PLE_DIM),
    }


def _fwd_reference(x, p, rel_bias, ffn1_norm, ffn1_w_gate, ffn1_w_up, ffn1_w_down, mix_norm,
              hyb_w_in, hyb_conv_w, hyb_q_gain, hyb_k_gain, hyb_w_out,
              rec_w_in, rec_conv_w, rec_conv_b, lru_wa, lru_ba, lru_wx, lru_bx, lru_lambda, rec_w_out,
              ffn2_norm, ffn2_w_gate, ffn2_w_up, ffn2_w_down, ple_norm, ple_w_gate, ple_w_proj):
    h = x
    for i in range(DEPTH):
        h = h + 0.5 * swiglu(rms_norm(h, ffn1_norm[i]), ffn1_w_gate[i], ffn1_w_up[i], ffn1_w_down[i])
        hn = rms_norm(h, mix_norm[i])
        if i % 2 == 0:
            e = i // 2
            h = h + hybrid_mixer(hn, hyb_w_in[e], hyb_conv_w[e], hyb_q_gain[e], hyb_k_gain[e],
                                 rel_bias, hyb_w_out[e])
        else:
            o = i // 2
            h = h + recurrent_mixer(hn, rec_w_in[o], rec_conv_w[o], rec_conv_b[o], lru_wa[o], lru_ba[o],
                                    lru_wx[o], lru_bx[o], lru_lambda[o], rec_w_out[o])
        h = h + 0.5 * swiglu(rms_norm(h, ffn2_norm[i]), ffn2_w_gate[i], ffn2_w_up[i], ffn2_w_down[i])
        gate = jax.nn.sigmoid(rms_norm(h, ple_norm[i]) @ ple_w_gate[i])
        h = h + gate * (p[i] @ ple_w_proj[i])
    return h


import jax as _jax
import jax.numpy as _jnp

TWIN_FORMAT = 'train_step'
FWD_PARAMS = ['x', 'p', 'rel_bias', 'ffn1_norm', 'ffn1_w_gate', 'ffn1_w_up', 'ffn1_w_down', 'mix_norm', 'hyb_w_in', 'hyb_conv_w', 'hyb_q_gain', 'hyb_k_gain', 'hyb_w_out', 'rec_w_in', 'rec_conv_w', 'rec_conv_b', 'lru_wa', 'lru_ba', 'lru_wx', 'lru_bx', 'lru_lambda', 'rec_w_out', 'ffn2_norm', 'ffn2_w_gate', 'ffn2_w_up', 'ffn2_w_down', 'ple_norm', 'ple_w_gate', 'ple_w_proj']
TWIN_WEIGHTS = ['rel_bias', 'ffn1_norm', 'ffn1_w_gate', 'ffn1_w_up', 'ffn1_w_down', 'mix_norm', 'hyb_w_in', 'hyb_conv_w', 'hyb_q_gain', 'hyb_k_gain', 'hyb_w_out', 'rec_w_in', 'rec_conv_w', 'rec_conv_b', 'lru_wa', 'lru_ba', 'lru_wx', 'lru_bx', 'lru_lambda', 'rec_w_out', 'ffn2_norm', 'ffn2_w_gate', 'ffn2_w_up', 'ffn2_w_down', 'ple_norm', 'ple_w_gate', 'ple_w_proj']
TWIN_DIFF_INPUT = 'x'
TWIN_INPUTS = ['x', 'p', 'rel_bias', 'ffn1_norm', 'ffn1_w_gate', 'ffn1_w_up', 'ffn1_w_down', 'mix_norm', 'hyb_w_in', 'hyb_conv_w', 'hyb_q_gain', 'hyb_k_gain', 'hyb_w_out', 'rec_w_in', 'rec_conv_w', 'rec_conv_b', 'lru_wa', 'lru_ba', 'lru_wx', 'lru_bx', 'lru_lambda', 'rec_w_out', 'ffn2_norm', 'ffn2_w_gate', 'ffn2_w_up', 'ffn2_w_down', 'ple_norm', 'ple_w_gate', 'ple_w_proj', 'loss_target', 'm_rel_bias', 'm_ffn1_norm', 'm_ffn1_w_gate', 'm_ffn1_w_up', 'm_ffn1_w_down', 'm_mix_norm', 'm_hyb_w_in', 'm_hyb_conv_w', 'm_hyb_q_gain', 'm_hyb_k_gain', 'm_hyb_w_out', 'm_rec_w_in', 'm_rec_conv_w', 'm_rec_conv_b', 'm_lru_wa', 'm_lru_ba', 'm_lru_wx', 'm_lru_bx', 'm_lru_lambda', 'm_rec_w_out', 'm_ffn2_norm', 'm_ffn2_w_gate', 'm_ffn2_w_up', 'm_ffn2_w_down', 'm_ple_norm', 'm_ple_w_gate', 'm_ple_w_proj', 'v_rel_bias', 'v_ffn1_norm', 'v_ffn1_w_gate', 'v_ffn1_w_up', 'v_ffn1_w_down', 'v_mix_norm', 'v_hyb_w_in', 'v_hyb_conv_w', 'v_hyb_q_gain', 'v_hyb_k_gain', 'v_hyb_w_out', 'v_rec_w_in', 'v_rec_conv_w', 'v_rec_conv_b', 'v_lru_wa', 'v_lru_ba', 'v_lru_wx', 'v_lru_bx', 'v_lru_lambda', 'v_rec_w_out', 'v_ffn2_norm', 'v_ffn2_w_gate', 'v_ffn2_w_up', 'v_ffn2_w_down', 'v_ple_norm', 'v_ple_w_gate', 'v_ple_w_proj']
TWIN_OUTPUTS = ['loss', 'grad_x', 'grad_rel_bias', 'grad_ffn1_norm', 'grad_ffn1_w_gate', 'grad_ffn1_w_up', 'grad_ffn1_w_down', 'grad_mix_norm', 'grad_hyb_w_in', 'grad_hyb_conv_w', 'grad_hyb_q_gain', 'grad_hyb_k_gain', 'grad_hyb_w_out', 'grad_rec_w_in', 'grad_rec_conv_w', 'grad_rec_conv_b', 'grad_lru_wa', 'grad_lru_ba', 'grad_lru_wx', 'grad_lru_bx', 'grad_lru_lambda', 'grad_rec_w_out', 'grad_ffn2_norm', 'grad_ffn2_w_gate', 'grad_ffn2_w_up', 'grad_ffn2_w_down', 'grad_ple_norm', 'grad_ple_w_gate', 'grad_ple_w_proj', 'delta_rel_bias', 'delta_ffn1_norm', 'delta_ffn1_w_gate', 'delta_ffn1_w_up', 'delta_ffn1_w_down', 'delta_mix_norm', 'delta_hyb_w_in', 'delta_hyb_conv_w', 'delta_hyb_q_gain', 'delta_hyb_k_gain', 'delta_hyb_w_out', 'delta_rec_w_in', 'delta_rec_conv_w', 'delta_rec_conv_b', 'delta_lru_wa', 'delta_lru_ba', 'delta_lru_wx', 'delta_lru_bx', 'delta_lru_lambda', 'delta_rec_w_out', 'delta_ffn2_norm', 'delta_ffn2_w_gate', 'delta_ffn2_w_up', 'delta_ffn2_w_down', 'delta_ple_norm', 'delta_ple_w_gate', 'delta_ple_w_proj', 'new_m_rel_bias', 'new_m_ffn1_norm', 'new_m_ffn1_w_gate', 'new_m_ffn1_w_up', 'new_m_ffn1_w_down', 'new_m_mix_norm', 'new_m_hyb_w_in', 'new_m_hyb_conv_w', 'new_m_hyb_q_gain', 'new_m_hyb_k_gain', 'new_m_hyb_w_out', 'new_m_rec_w_in', 'new_m_rec_conv_w', 'new_m_rec_conv_b', 'new_m_lru_wa', 'new_m_lru_ba', 'new_m_lru_wx', 'new_m_lru_bx', 'new_m_lru_lambda', 'new_m_rec_w_out', 'new_m_ffn2_norm', 'new_m_ffn2_w_gate', 'new_m_ffn2_w_up', 'new_m_ffn2_w_down', 'new_m_ple_norm', 'new_m_ple_w_gate', 'new_m_ple_w_proj', 'new_v_rel_bias', 'new_v_ffn1_norm', 'new_v_ffn1_w_gate', 'new_v_ffn1_w_up', 'new_v_ffn1_w_down', 'new_v_mix_norm', 'new_v_hyb_w_in', 'new_v_hyb_conv_w', 'new_v_hyb_q_gain', 'new_v_hyb_k_gain', 'new_v_hyb_w_out', 'new_v_rec_w_in', 'new_v_rec_conv_w', 'new_v_rec_conv_b', 'new_v_lru_wa', 'new_v_lru_ba', 'new_v_lru_wx', 'new_v_lru_bx', 'new_v_lru_lambda', 'new_v_rec_w_out', 'new_v_ffn2_norm', 'new_v_ffn2_w_gate', 'new_v_ffn2_w_up', 'new_v_ffn2_w_down', 'new_v_ple_norm', 'new_v_ple_w_gate', 'new_v_ple_w_proj']
TWIN_LEAF_KINDS = {'loss': 'loss', 'grad_x': 'grad_x', 'grad_rel_bias': 'grad_w', 'grad_ffn1_norm': 'grad_w', 'grad_ffn1_w_gate': 'grad_w', 'grad_ffn1_w_up': 'grad_w', 'grad_ffn1_w_down': 'grad_w', 'grad_mix_norm': 'grad_w', 'grad_hyb_w_in': 'grad_w', 'grad_hyb_conv_w': 'grad_w', 'grad_hyb_q_gain': 'grad_w', 'grad_hyb_k_gain': 'grad_w', 'grad_hyb_w_out': 'grad_w', 'grad_rec_w_in': 'grad_w', 'grad_rec_conv_w': 'grad_w', 'grad_rec_conv_b': 'grad_w', 'grad_lru_wa': 'grad_w', 'grad_lru_ba': 'grad_w', 'grad_lru_wx': 'grad_w', 'grad_lru_bx': 'grad_w', 'grad_lru_lambda': 'grad_w', 'grad_rec_w_out': 'grad_w', 'grad_ffn2_norm': 'grad_w', 'grad_ffn2_w_gate': 'grad_w', 'grad_ffn2_w_up': 'grad_w', 'grad_ffn2_w_down': 'grad_w', 'grad_ple_norm': 'grad_w', 'grad_ple_w_gate': 'grad_w', 'grad_ple_w_proj': 'grad_w', 'delta_rel_bias': 'delta_w', 'delta_ffn1_norm': 'delta_w', 'delta_ffn1_w_gate': 'delta_w', 'delta_ffn1_w_up': 'delta_w', 'delta_ffn1_w_down': 'delta_w', 'delta_mix_norm': 'delta_w', 'delta_hyb_w_in': 'delta_w', 'delta_hyb_conv_w': 'delta_w', 'delta_hyb_q_gain': 'delta_w', 'delta_hyb_k_gain': 'delta_w', 'delta_hyb_w_out': 'delta_w', 'delta_rec_w_in': 'delta_w', 'delta_rec_conv_w': 'delta_w', 'delta_rec_conv_b': 'delta_w', 'delta_lru_wa': 'delta_w', 'delta_lru_ba': 'delta_w', 'delta_lru_wx': 'delta_w', 'delta_lru_bx': 'delta_w', 'delta_lru_lambda': 'delta_w', 'delta_rec_w_out': 'delta_w', 'delta_ffn2_norm': 'delta_w', 'delta_ffn2_w_gate': 'delta_w', 'delta_ffn2_w_up': 'delta_w', 'delta_ffn2_w_down': 'delta_w', 'delta_ple_norm': 'delta_w', 'delta_ple_w_gate': 'delta_w', 'delta_ple_w_proj': 'delta_w', 'new_m_rel_bias': 'new_m', 'new_m_ffn1_norm': 'new_m', 'new_m_ffn1_w_gate': 'new_m', 'new_m_ffn1_w_up': 'new_m', 'new_m_ffn1_w_down': 'new_m', 'new_m_mix_norm': 'new_m', 'new_m_hyb_w_in': 'new_m', 'new_m_hyb_conv_w': 'new_m', 'new_m_hyb_q_gain': 'new_m', 'new_m_hyb_k_gain': 'new_m', 'new_m_hyb_w_out': 'new_m', 'new_m_rec_w_in': 'new_m', 'new_m_rec_conv_w': 'new_m', 'new_m_rec_conv_b': 'new_m', 'new_m_lru_wa': 'new_m', 'new_m_lru_ba': 'new_m', 'new_m_lru_wx': 'new_m', 'new_m_lru_bx': 'new_m', 'new_m_lru_lambda': 'new_m', 'new_m_rec_w_out': 'new_m', 'new_m_ffn2_norm': 'new_m', 'new_m_ffn2_w_gate': 'new_m', 'new_m_ffn2_w_up': 'new_m', 'new_m_ffn2_w_down': 'new_m', 'new_m_ple_norm': 'new_m', 'new_m_ple_w_gate': 'new_m', 'new_m_ple_w_proj': 'new_m', 'new_v_rel_bias': 'new_v', 'new_v_ffn1_norm': 'new_v', 'new_v_ffn1_w_gate': 'new_v', 'new_v_ffn1_w_up': 'new_v', 'new_v_ffn1_w_down': 'new_v', 'new_v_mix_norm': 'new_v', 'new_v_hyb_w_in': 'new_v', 'new_v_hyb_conv_w': 'new_v', 'new_v_hyb_q_gain': 'new_v', 'new_v_hyb_k_gain': 'new_v', 'new_v_hyb_w_out': 'new_v', 'new_v_rec_w_in': 'new_v', 'new_v_rec_conv_w': 'new_v', 'new_v_rec_conv_b': 'new_v', 'new_v_lru_wa': 'new_v', 'new_v_lru_ba': 'new_v', 'new_v_lru_wx': 'new_v', 'new_v_lru_bx': 'new_v', 'new_v_lru_lambda': 'new_v', 'new_v_rec_w_out': 'new_v', 'new_v_ffn2_norm': 'new_v', 'new_v_ffn2_w_gate': 'new_v', 'new_v_ffn2_w_up': 'new_v', 'new_v_ffn2_w_down': 'new_v', 'new_v_ple_norm': 'new_v', 'new_v_ple_w_gate': 'new_v', 'new_v_ple_w_proj': 'new_v'}


def _forward(args):
    return _fwd_reference(*[args[k] for k in FWD_PARAMS])


def _output_shape():
    def fwd():
        inp = _fwd_setup_inputs(0)
        return _fwd_reference(*[inp[k] for k in FWD_PARAMS])
    out = _jax.eval_shape(fwd)
    return out.shape, out.dtype

N_MICROBATCH = 1
ADAM_LR = 0.001
ADAM_B1 = 0.9
ADAM_B2 = 0.999
ADAM_EPS = 1e-08
ADAM_WD = 0.01
ADAM_STEP = 10
PER_EXAMPLE_BATCH_AXIS = {'x': 0, 'p': 1, 'loss_target': 0}
SHARED_INPUTS = []
_WEIGHT_DTYPES = {'rel_bias': _jnp.float32, 'ffn1_norm': _jnp.float32, 'ffn1_w_gate': _jnp.float32, 'ffn1_w_up': _jnp.float32, 'ffn1_w_down': _jnp.float32, 'mix_norm': _jnp.float32, 'hyb_w_in': _jnp.float32, 'hyb_conv_w': _jnp.float32, 'hyb_q_gain': _jnp.float32, 'hyb_k_gain': _jnp.float32, 'hyb_w_out': _jnp.float32, 'rec_w_in': _jnp.float32, 'rec_conv_w': _jnp.float32, 'rec_conv_b': _jnp.float32, 'lru_wa': _jnp.float32, 'lru_ba': _jnp.float32, 'lru_wx': _jnp.float32, 'lru_bx': _jnp.float32, 'lru_lambda': _jnp.float32, 'rec_w_out': _jnp.float32, 'ffn2_norm': _jnp.float32, 'ffn2_w_gate': _jnp.float32, 'ffn2_w_up': _jnp.float32, 'ffn2_w_down': _jnp.float32, 'ple_norm': _jnp.float32, 'ple_w_gate': _jnp.float32, 'ple_w_proj': _jnp.float32}
MOMENT_SCALE = {'rel_bias': 5.466902e-01, 'ffn1_norm': 1.214147e+01, 'ffn1_w_gate': 2.794866e-01, 'ffn1_w_up': 3.008227e-01, 'ffn1_w_down': 5.031459e-01, 'mix_norm': 6.922022e+01, 'hyb_w_in': 1.399327e+00, 'hyb_conv_w': 3.670110e+01, 'hyb_q_gain': 3.543811e+00, 'hyb_k_gain': 3.534488e+00, 'hyb_w_out': 1.457819e+00, 'rec_w_in': 9.520716e-01, 'rec_conv_w': 7.560208e+00, 'rec_conv_b': 2.412847e+01, 'lru_wa': 6.765440e-01, 'lru_ba': 6.341342e-01, 'lru_wx': 1.240521e+00, 'lru_bx': 4.926284e+00, 'lru_lambda': 1.253907e+00, 'rec_w_out': 5.693951e-01, 'ffn2_norm': 1.232499e+01, 'ffn2_w_gate': 1.695013e-01, 'ffn2_w_up': 2.105592e-01, 'ffn2_w_down': 3.498910e-01, 'ple_norm': 1.921358e+00, 'ple_w_gate': 1.901342e-01, 'ple_w_proj': 9.144453e-01}


def _to_microbatches(a, axis):
    t = _jnp.moveaxis(a, axis, 0)
    t = t.reshape((N_MICROBATCH, t.shape[0] // N_MICROBATCH) + t.shape[1:])
    return _jnp.moveaxis(t, 1, axis + 1)


def setup_inputs(seed: int = 0) -> dict:
    inp = _fwd_setup_inputs(seed)
    key = _jax.random.fold_in(_jax.random.key(seed), 7919)
    shape, _ = _output_shape()
    out = dict(inp)
    out["loss_target"] = _jax.random.normal(_jax.random.fold_in(key, 0), shape, _jnp.float32)
    for i, name in enumerate(TWIN_WEIGHTS):
        w = inp[name].astype(_jnp.float32)
        if MOMENT_SCALE is None:
            s = _jnp.sqrt(_jnp.mean(_jnp.square(w)) + 1e-30)
        else:
            s = MOMENT_SCALE[name]
        km, kv = _jax.random.split(_jax.random.fold_in(key, i + 1))
        out[name] = w
        out["m_" + name] = s * _jax.random.normal(km, w.shape, _jnp.float32)
        out["v_" + name] = (s * s) * _jax.random.uniform(kv, w.shape, _jnp.float32, 0.5, 1.5)
    if N_MICROBATCH > 1:
        for name, axis in PER_EXAMPLE_BATCH_AXIS.items():
            out[name] = _to_microbatches(out[name], axis)
    return {'x': out['x'], 'p': out['p'], 'rel_bias': out['rel_bias'], 'ffn1_norm': out['ffn1_norm'], 'ffn1_w_gate': out['ffn1_w_gate'], 'ffn1_w_up': out['ffn1_w_up'], 'ffn1_w_down': out['ffn1_w_down'], 'mix_norm': out['mix_norm'], 'hyb_w_in': out['hyb_w_in'], 'hyb_conv_w': out['hyb_conv_w'], 'hyb_q_gain': out['hyb_q_gain'], 'hyb_k_gain': out['hyb_k_gain'], 'hyb_w_out': out['hyb_w_out'], 'rec_w_in': out['rec_w_in'], 'rec_conv_w': out['rec_conv_w'], 'rec_conv_b': out['rec_conv_b'], 'lru_wa': out['lru_wa'], 'lru_ba': out['lru_ba'], 'lru_wx': out['lru_wx'], 'lru_bx': out['lru_bx'], 'lru_lambda': out['lru_lambda'], 'rec_w_out': out['rec_w_out'], 'ffn2_norm': out['ffn2_norm'], 'ffn2_w_gate': out['ffn2_w_gate'], 'ffn2_w_up': out['ffn2_w_up'], 'ffn2_w_down': out['ffn2_w_down'], 'ple_norm': out['ple_norm'], 'ple_w_gate': out['ple_w_gate'], 'ple_w_proj': out['ple_w_proj'], 'loss_target': out['loss_target'], 'm_rel_bias': out['m_rel_bias'], 'm_ffn1_norm': out['m_ffn1_norm'], 'm_ffn1_w_gate': out['m_ffn1_w_gate'], 'm_ffn1_w_up': out['m_ffn1_w_up'], 'm_ffn1_w_down': out['m_ffn1_w_down'], 'm_mix_norm': out['m_mix_norm'], 'm_hyb_w_in': out['m_hyb_w_in'], 'm_hyb_conv_w': out['m_hyb_conv_w'], 'm_hyb_q_gain': out['m_hyb_q_gain'], 'm_hyb_k_gain': out['m_hyb_k_gain'], 'm_hyb_w_out': out['m_hyb_w_out'], 'm_rec_w_in': out['m_rec_w_in'], 'm_rec_conv_w': out['m_rec_conv_w'], 'm_rec_conv_b': out['m_rec_conv_b'], 'm_lru_wa': out['m_lru_wa'], 'm_lru_ba': out['m_lru_ba'], 'm_lru_wx': out['m_lru_wx'], 'm_lru_bx': out['m_lru_bx'], 'm_lru_lambda': out['m_lru_lambda'], 'm_rec_w_out': out['m_rec_w_out'], 'm_ffn2_norm': out['m_ffn2_norm'], 'm_ffn2_w_gate': out['m_ffn2_w_gate'], 'm_ffn2_w_up': out['m_ffn2_w_up'], 'm_ffn2_w_down': out['m_ffn2_w_down'], 'm_ple_norm': out['m_ple_norm'], 'm_ple_w_gate': out['m_ple_w_gate'], 'm_ple_w_proj': out['m_ple_w_proj'], 'v_rel_bias': out['v_rel_bias'], 'v_ffn1_norm': out['v_ffn1_norm'], 'v_ffn1_w_gate': out['v_ffn1_w_gate'], 'v_ffn1_w_up': out['v_ffn1_w_up'], 'v_ffn1_w_down': out['v_ffn1_w_down'], 'v_mix_norm': out['v_mix_norm'], 'v_hyb_w_in': out['v_hyb_w_in'], 'v_hyb_conv_w': out['v_hyb_conv_w'], 'v_hyb_q_gain': out['v_hyb_q_gain'], 'v_hyb_k_gain': out['v_hyb_k_gain'], 'v_hyb_w_out': out['v_hyb_w_out'], 'v_rec_w_in': out['v_rec_w_in'], 'v_rec_conv_w': out['v_rec_conv_w'], 'v_rec_conv_b': out['v_rec_conv_b'], 'v_lru_wa': out['v_lru_wa'], 'v_lru_ba': out['v_lru_ba'], 'v_lru_wx': out['v_lru_wx'], 'v_lru_bx': out['v_lru_bx'], 'v_lru_lambda': out['v_lru_lambda'], 'v_rec_w_out': out['v_rec_w_out'], 'v_ffn2_norm': out['v_ffn2_norm'], 'v_ffn2_w_gate': out['v_ffn2_w_gate'], 'v_ffn2_w_up': out['v_ffn2_w_up'], 'v_ffn2_w_down': out['v_ffn2_w_down'], 'v_ple_norm': out['v_ple_norm'], 'v_ple_w_gate': out['v_ple_w_gate'], 'v_ple_w_proj': out['v_ple_w_proj']}


def _loss(weights, diff, rest, loss_target):
    with _jax.named_scope("forward"):
        args = {**rest, TWIN_DIFF_INPUT: diff, **{k: w.astype(_WEIGHT_DTYPES[k]) for k, w in weights.items()}}
        y = _forward(args)
    with _jax.named_scope("loss_head"):
        err = _jnp.square(y.astype(_jnp.float32) - loss_target)
        return 0.5 * _jnp.sum(_jnp.mean(err, axis=-1)) if err.ndim else 0.5 * err


def _adamw(w, g, m, v):
    m = ADAM_B1 * m + (1.0 - ADAM_B1) * g
    v = ADAM_B2 * v + (1.0 - ADAM_B2) * _jnp.square(g)
    m_hat = m / (1.0 - ADAM_B1 ** ADAM_STEP)
    v_hat = v / (1.0 - ADAM_B2 ** ADAM_STEP)
    delta = -ADAM_LR * (m_hat / (_jnp.sqrt(v_hat) + ADAM_EPS) + ADAM_WD * w)
    return delta, m, v


def reference(x, p, rel_bias, ffn1_norm, ffn1_w_gate, ffn1_w_up, ffn1_w_down, mix_norm, hyb_w_in, hyb_conv_w, hyb_q_gain, hyb_k_gain, hyb_w_out, rec_w_in, rec_conv_w, rec_conv_b, lru_wa, lru_ba, lru_wx, lru_bx, lru_lambda, rec_w_out, ffn2_norm, ffn2_w_gate, ffn2_w_up, ffn2_w_down, ple_norm, ple_w_gate, ple_w_proj, loss_target, m_rel_bias, m_ffn1_norm, m_ffn1_w_gate, m_ffn1_w_up, m_ffn1_w_down, m_mix_norm, m_hyb_w_in, m_hyb_conv_w, m_hyb_q_gain, m_hyb_k_gain, m_hyb_w_out, m_rec_w_in, m_rec_conv_w, m_rec_conv_b, m_lru_wa, m_lru_ba, m_lru_wx, m_lru_bx, m_lru_lambda, m_rec_w_out, m_ffn2_norm, m_ffn2_w_gate, m_ffn2_w_up, m_ffn2_w_down, m_ple_norm, m_ple_w_gate, m_ple_w_proj, v_rel_bias, v_ffn1_norm, v_ffn1_w_gate, v_ffn1_w_up, v_ffn1_w_down, v_mix_norm, v_hyb_w_in, v_hyb_conv_w, v_hyb_q_gain, v_hyb_k_gain, v_hyb_w_out, v_rec_w_in, v_rec_conv_w, v_rec_conv_b, v_lru_wa, v_lru_ba, v_lru_wx, v_lru_bx, v_lru_lambda, v_rec_w_out, v_ffn2_norm, v_ffn2_w_gate, v_ffn2_w_up, v_ffn2_w_down, v_ple_norm, v_ple_w_gate, v_ple_w_proj):
    given = dict(x=x, p=p, rel_bias=rel_bias, ffn1_norm=ffn1_norm, ffn1_w_gate=ffn1_w_gate, ffn1_w_up=ffn1_w_up, ffn1_w_down=ffn1_w_down, mix_norm=mix_norm, hyb_w_in=hyb_w_in, hyb_conv_w=hyb_conv_w, hyb_q_gain=hyb_q_gain, hyb_k_gain=hyb_k_gain, hyb_w_out=hyb_w_out, rec_w_in=rec_w_in, rec_conv_w=rec_conv_w, rec_conv_b=rec_conv_b, lru_wa=lru_wa, lru_ba=lru_ba, lru_wx=lru_wx, lru_bx=lru_bx, lru_lambda=lru_lambda, rec_w_out=rec_w_out, ffn2_norm=ffn2_norm, ffn2_w_gate=ffn2_w_gate, ffn2_w_up=ffn2_w_up, ffn2_w_down=ffn2_w_down, ple_norm=ple_norm, ple_w_gate=ple_w_gate, ple_w_proj=ple_w_proj, loss_target=loss_target, m_rel_bias=m_rel_bias, m_ffn1_norm=m_ffn1_norm, m_ffn1_w_gate=m_ffn1_w_gate, m_ffn1_w_up=m_ffn1_w_up, m_ffn1_w_down=m_ffn1_w_down, m_mix_norm=m_mix_norm, m_hyb_w_in=m_hyb_w_in, m_hyb_conv_w=m_hyb_conv_w, m_hyb_q_gain=m_hyb_q_gain, m_hyb_k_gain=m_hyb_k_gain, m_hyb_w_out=m_hyb_w_out, m_rec_w_in=m_rec_w_in, m_rec_conv_w=m_rec_conv_w, m_rec_conv_b=m_rec_conv_b, m_lru_wa=m_lru_wa, m_lru_ba=m_lru_ba, m_lru_wx=m_lru_wx, m_lru_bx=m_lru_bx, m_lru_lambda=m_lru_lambda, m_rec_w_out=m_rec_w_out, m_ffn2_norm=m_ffn2_norm, m_ffn2_w_gate=m_ffn2_w_gate, m_ffn2_w_up=m_ffn2_w_up, m_ffn2_w_down=m_ffn2_w_down, m_ple_norm=m_ple_norm, m_ple_w_gate=m_ple_w_gate, m_ple_w_proj=m_ple_w_proj, v_rel_bias=v_rel_bias, v_ffn1_norm=v_ffn1_norm, v_ffn1_w_gate=v_ffn1_w_gate, v_ffn1_w_up=v_ffn1_w_up, v_ffn1_w_down=v_ffn1_w_down, v_mix_norm=v_mix_norm, v_hyb_w_in=v_hyb_w_in, v_hyb_conv_w=v_hyb_conv_w, v_hyb_q_gain=v_hyb_q_gain, v_hyb_k_gain=v_hyb_k_gain, v_hyb_w_out=v_hyb_w_out, v_rec_w_in=v_rec_w_in, v_rec_conv_w=v_rec_conv_w, v_rec_conv_b=v_rec_conv_b, v_lru_wa=v_lru_wa, v_lru_ba=v_lru_ba, v_lru_wx=v_lru_wx, v_lru_bx=v_lru_bx, v_lru_lambda=v_lru_lambda, v_rec_w_out=v_rec_w_out, v_ffn2_norm=v_ffn2_norm, v_ffn2_w_gate=v_ffn2_w_gate, v_ffn2_w_up=v_ffn2_w_up, v_ffn2_w_down=v_ffn2_w_down, v_ple_norm=v_ple_norm, v_ple_w_gate=v_ple_w_gate, v_ple_w_proj=v_ple_w_proj)
    weights = {n: given[n] for n in TWIN_WEIGHTS}
    shared = {n: given[n] for n in SHARED_INPUTS}
    per_example = {n: given[n] for n in ['x', 'p']}
    grad_fn = _jax.value_and_grad(_loss, argnums=(0, 1))

    def one_microbatch(ex, loss_target):
        ex = dict(ex)
        diff = ex.pop(TWIN_DIFF_INPUT)
        return grad_fn(weights, diff, {**shared, **ex}, loss_target)

    if N_MICROBATCH == 1:
        loss, (grad_w, grad_x) = one_microbatch(per_example, given["loss_target"])
    else:
        def body(carry, xs):
            loss_sum, grad_sum = carry
            l_k, (gw_k, gx_k) = one_microbatch(xs[0], xs[1])
            with _jax.named_scope("update"):
                return (loss_sum + l_k, _jax.tree.map(_jnp.add, grad_sum, gw_k)), gx_k

        init = (_jnp.zeros((), _jnp.float32), _jax.tree.map(_jnp.zeros_like, weights))
        (loss, grad_w), grad_x = _jax.lax.scan(body, init, (per_example, given["loss_target"]))
    with _jax.named_scope("update"):
        delta_w, new_m, new_v = {}, {}, {}
        for n in TWIN_WEIGHTS:
            delta_w[n], new_m[n], new_v[n] = _adamw(weights[n], grad_w[n], given["m_" + n], given["v_" + n])
    return (loss, grad_x, *[grad_w[n] for n in TWIN_WEIGHTS], *[delta_w[n] for n in TWIN_WEIGHTS],
            *[new_m[n] for n in TWIN_WEIGHTS], *[new_v[n] for n in TWIN_WEIGHTS])
```

```python
import functools
import math

import numpy as np
import jax
import jax.numpy as jnp
from jax import lax
from jax.experimental import pallas as pl
from jax.experimental.pallas import tpu as pltpu

F32, BF16 = jnp.float32, jnp.bfloat16
S = jax.ShapeDtypeStruct
MESH = pl.DeviceIdType.MESH

D_MODEL = 1024
N_SHARD = 4
N_DEV = 8
HEAD_DIM = 64
N_HEADS = 8
ATTN_W = N_HEADS * HEAD_DIM
CONV_W = 512
BAND = 128
DILATIONS = (1, 4, 16)
REL_BUCKETS = 32
REL_MAX_DIST = 2048
LRU_BLOCKS = 4
LRU_BLOCK = 256
LRU_C = 8.0
EPS = 1e-6
NEG = -1e30
VMEM_LIMIT = 56 * 1024 * 1024

ADAM_LR, ADAM_B1, ADAM_B2, ADAM_EPS, ADAM_WD, ADAM_STEP = 0.001, 0.9, 0.999, 1e-08, 0.01, 10


def _cp(*sem):
    return pltpu.CompilerParams(dimension_semantics=sem, vmem_limit_bytes=VMEM_LIMIT)


def _bs(shape, imap):
    return pl.BlockSpec(shape, imap)


def _row_tile(t, want):
    for cand in range(min(want, t) // 8 * 8, 0, -8):
        if t % cand == 0:
            return cand
    return t


def _rstd(x):
    return lax.rsqrt(jnp.mean(x * x, axis=-1, keepdims=True) + EPS)


def _sigmoid(x):
    return 1.0 / (1.0 + jnp.exp(-x))


def _dot(a, b):
    return jnp.dot(a, b, preferred_element_type=F32)


def _dot_nt(a, b):
    return lax.dot_general(a, b, (((1,), (1,)), ((), ())), preferred_element_type=F32)


def _dot_tn(a, b):
    return lax.dot_general(a, b, (((0,), (0,)), ((), ())), preferred_element_type=F32)


def _seg_dot(x, seg_bf16):
    hi = x.astype(BF16)
    lo = (x - hi.astype(F32)).astype(BF16)
    return _dot(hi, seg_bf16) + _dot(lo, seg_bf16)


def _shift_down(x, prev8, s):
    if s == 0:
        return x
    tm = x.shape[0]
    row = lax.broadcasted_iota(jnp.int32, x.shape, 0)
    main = jnp.where(row >= s, pltpu.roll(x, s, axis=0), 0.0)
    row8 = lax.broadcasted_iota(jnp.int32, prev8.shape, 0)
    head = jnp.where(row8 < s, pltpu.roll(prev8, s, axis=0), 0.0)
    if tm == 8:
        return main + head
    return main + jnp.concatenate([head, jnp.zeros((tm - 8, x.shape[1]), x.dtype)], axis=0)


def _shift_up(x, next8, s):
    if s == 0:
        return x
    tm = x.shape[0]
    row = lax.broadcasted_iota(jnp.int32, x.shape, 0)
    main = jnp.where(row < tm - s, pltpu.roll(x, tm - s, axis=0), 0.0)
    row8 = lax.broadcasted_iota(jnp.int32, next8.shape, 0)
    tail = jnp.where(row8 >= 8 - s, pltpu.roll(next8, 8 - s, axis=0), 0.0)
    if tm == 8:
        return main + tail
    return main + jnp.concatenate([jnp.zeros((tm - 8, x.shape[1]), x.dtype), tail], axis=0)


def _roll_fill(x, s, fill, up):
    tm = x.shape[0]
    row = lax.broadcasted_iota(jnp.int32, x.shape, 0)
    if up:
        return jnp.where(row < tm - s, pltpu.roll(x, tm - s, axis=0), fill)
    return jnp.where(row >= s, pltpu.roll(x, s, axis=0), fill)


def _log1p(y):
    u = 1.0 + y
    return jnp.where(u == 1.0, y, jnp.log(u) * (y / jnp.where(u == 1.0, 1.0, u - 1.0)))


def _softplus(x):
    return jnp.maximum(x, 0.0) + _log1p(jnp.exp(-jnp.abs(x)))


def _neg_expm1(y):
    series = -y * (1.0 + y * (0.5 + y * (1.0 / 6.0 + y * (1.0 / 24.0 + y * (1.0 / 120.0)))))
    return jnp.where(jnp.abs(y) < 0.03, series, 1.0 - jnp.exp(y))


_GELU_C = math.sqrt(2.0 / math.pi)


def _gelu_and_grad(x):
    inner = _GELU_C * (x + 0.044715 * x * x * x)
    t = jnp.tanh(inner)
    g = 0.5 * x * (1.0 + t)
    dg = 0.5 * (1.0 + t) + 0.5 * x * (1.0 - t * t) * _GELU_C * (1.0 + 3.0 * 0.044715 * x * x)
    return g, dg


def _rmsnorm_bwd(x, gain, dy):
    r = _rstd(x)
    xhat = x * r
    dxhat = dy * gain
    dx = r * (dxhat - xhat * jnp.mean(dxhat * xhat, axis=-1, keepdims=True))
    return dx, jnp.sum(dy * xhat, axis=0, keepdims=True)


def ffn_up(h, gain, wg, wu, layer, name):
    t, d = h.shape
    nk, _, _, f = wg.shape
    tm = _row_tile(t, 512)

    def body(h_ref, g_ref, wg_ref, wu_ref, hn_ref, gg_ref, uu_ref, aa_ref, hn_scr):
        @pl.when(pl.program_id(1) == 0)
        def _():
            x = h_ref[...]
            hn = (x * _rstd(x) * g_ref[...]).astype(BF16)
            hn_scr[...] = hn
            hn_ref[...] = hn
        hn = hn_scr[...]
        g = _dot(hn, wg_ref[...])
        u = _dot(hn, wu_ref[...])
        gg_ref[...] = g.astype(BF16)
        uu_ref[...] = u.astype(BF16)
        aa_ref[...] = (g * _sigmoid(g) * u).astype(BF16)

    wspec = _bs((None, None, d, f), lambda i, k: (k, layer, 0, 0))
    aspec = _bs((None, tm, f), lambda i, k: (k, i, 0))
    return pl.pallas_call(
        body, name=name, grid=(t // tm, nk),
        in_specs=[_bs((tm, d), lambda i, k: (i, 0)), _bs((1, d), lambda i, k: (0, 0)), wspec, wspec],
        out_specs=[_bs((tm, d), lambda i, k: (i, 0)), aspec, aspec, aspec],
        out_shape=[S((t, d), BF16), S((nk, t, f), BF16), S((nk, t, f), BF16), S((nk, t, f), BF16)],
        scratch_shapes=[pltpu.VMEM((tm, d), BF16)],
        compiler_params=_cp("parallel", "arbitrary"),
    )(h, gain, wg, wu)


def mm_acc(a, a_spec, b, b_spec, res, scale, nk, t, n, name, tm):
    def body(a_ref, b_ref, r_ref, o_ref, acc):
        k = pl.program_id(1)

        @pl.when(k == 0)
        def _():
            acc[...] = jnp.zeros_like(acc)
        acc[...] += _dot(a_ref[...].astype(BF16), b_ref[...])

        @pl.when(k == nk - 1)
        def _():
            o_ref[...] = r_ref[...] + scale * acc[...]

    return pl.pallas_call(
        body, name=name, grid=(t // tm, nk),
        in_specs=[a_spec, b_spec, _bs((tm, n), lambda i, k: (i, 0))],
        out_specs=_bs((tm, n), lambda i, k: (i, 0)),
        out_shape=S((t, n), F32),
        scratch_shapes=[pltpu.VMEM((tm, n), F32)],
        compiler_params=_cp("parallel", "arbitrary"),
    )(a, b, res)


def ffn_down(a, wd, layer, h, name):
    nk, t, f = a.shape
    d = h.shape[1]
    tm = _row_tile(t, 512)
    return mm_acc(a, _bs((None, tm, f), lambda i, k: (k, i, 0)),
                  wd, _bs((None, None, f, d), lambda i, k: (k, layer, 0, 0)),
                  h, 0.5, nk, t, d, name, tm)


def ffn_bwd_act(dh, wd, layer, gg, uu, name):
    nk, t, f = gg.shape
    d = dh.shape[1]
    tm = _row_tile(t, 512)

    def body(dh_ref, wd_ref, g_ref, u_ref, dg_ref, du_ref):
        da = 0.5 * _dot_nt(dh_ref[...].astype(BF16), wd_ref[...])
        g = g_ref[...].astype(F32)
        u = u_ref[...].astype(F32)
        s = _sigmoid(g)
        dg_ref[...] = (da * u * (s * (1.0 + g * (1.0 - s)))).astype(BF16)
        du_ref[...] = (da * (g * s)).astype(BF16)

    aspec = _bs((None, tm, f), lambda i, k: (k, i, 0))
    return pl.pallas_call(
        body, name=name, grid=(t // tm, nk),
        in_specs=[_bs((tm, d), lambda i, k: (i, 0)), _bs((None, None, f, d), lambda i, k: (k, layer, 0, 0)), aspec, aspec],
        out_specs=[aspec, aspec],
        out_shape=[S((nk, t, f), BF16), S((nk, t, f), BF16)],
        compiler_params=_cp("parallel", "arbitrary"),
    )(dh, wd, gg, uu)


def nt_acc_normbwd(terms, nk, h, gain, dh, name):
    t, d = h.shape
    tm = _row_tile(t, 512)
    nterm = len(terms)

    def body(*refs):
        xs = refs[:2 * nterm]
        h_ref, g_ref, dh_ref, o_ref, dg_ref, acc = refs[2 * nterm:]
        i, k = pl.program_id(0), pl.program_id(1)

        @pl.when(k == 0)
        def _():
            acc[...] = jnp.zeros_like(acc)

        @pl.when(jnp.logical_and(i == 0, k == 0))
        def _():
            dg_ref[...] = jnp.zeros_like(dg_ref)
        for j in range(nterm):
            acc[...] += _dot_nt(xs[2 * j][...], xs[2 * j + 1][...])

        @pl.when(k == nk - 1)
        def _():
            dx, dgain = _rmsnorm_bwd(h_ref[...], g_ref[...], acc[...])
            o_ref[...] = dh_ref[...] + dx
            dg_ref[...] += dgain

    in_specs, args = [], []
    for x, xs_, w, ws_ in terms:
        in_specs += [xs_, ws_]
        args += [x, w]
    row = _bs((tm, d), lambda i, k: (i, 0))
    vec = _bs((1, d), lambda i, k: (0, 0))
    return pl.pallas_call(
        body, name=name, grid=(t // tm, nk),
        in_specs=in_specs + [row, vec, row],
        out_specs=[row, vec],
        out_shape=[S((t, d), F32), S((1, d), F32)],
        scratch_shapes=[pltpu.VMEM((tm, d), F32)],
        compiler_params=_cp("arbitrary", "arbitrary"),
    )(*args, h, gain, dh)


def ffn_bwd_in(dg, du, wg, wu, layer, h, gain, dh, name):
    nk, t, f = dg.shape
    d = h.shape[1]
    tm = _row_tile(t, 512)
    aspec = _bs((None, tm, f), lambda i, k: (k, i, 0))
    wspec = _bs((None, None, d, f), lambda i, k: (k, layer, 0, 0))
    return nt_acc_normbwd([(dg, aspec, wg, wspec), (du, aspec, wu, wspec)], nk, h, gain, dh, name)


def tn_mm(x, x_spec, y, y_spec, nblk, t, ka, nb, out_shape, out_spec, scale, name, prev=None, tk=512):
    tk = _row_tile(t, tk)

    def body(*refs):
        if prev is None:
            x_ref, y_ref, o_ref, acc = refs
        else:
            x_ref, y_ref, _, o_ref, acc = refs
        j = pl.program_id(1)

        @pl.when(j == 0)
        def _():
            acc[...] = jnp.zeros_like(acc)
        acc[...] += _dot_tn(x_ref[...].astype(BF16), y_ref[...].astype(BF16))

        @pl.when(j == t // tk - 1)
        def _():
            o_ref[...] = (scale * acc[...]).astype(o_ref.dtype)

    in_specs = [x_spec(tk), y_spec(tk)]
    args = [x, y]
    aliases = {}
    if prev is not None:
        in_specs.append(pl.BlockSpec(memory_space=pl.ANY))
        args.append(prev)
        aliases = {2: 0}
    return pl.pallas_call(
        body, name=name, grid=(nblk, t // tk),
        in_specs=in_specs, out_specs=out_spec, out_shape=out_shape,
        scratch_shapes=[pltpu.VMEM((ka, nb), F32)],
        input_output_aliases=aliases,
        compiler_params=_cp("parallel", "arbitrary"),
    )(*args)


def ffn_wgrads(hn, dh, aa, dg, du, layer, prev, name):
    nk, t, f = aa.shape
    d = hn.shape[1]
    pg, pu, pd = prev if prev is not None else (None, None, None)
    hn_spec = lambda tk: _bs((tk, d), lambda k, j: (j, 0))
    a_spec = lambda tk: _bs((None, tk, f), lambda k, j: (k, j, 0))
    shape_gu, spec_gu = S((nk, 2, d, f), BF16), _bs((None, None, d, f), lambda k, j: (k, layer, 0, 0))
    shape_d, spec_d = S((nk, 2, f, d), BF16), _bs((None, None, f, d), lambda k, j: (k, layer, 0, 0))
    gwg = tn_mm(hn, hn_spec, dg, a_spec, nk, t, d, f, shape_gu, spec_gu, 1.0, name + "_g", pg)
    gwu = tn_mm(hn, hn_spec, du, a_spec, nk, t, d, f, shape_gu, spec_gu, 1.0, name + "_u", pu)
    gwd = tn_mm(aa, a_spec, dh, hn_spec, nk, t, f, d, shape_d, spec_d, 0.5, name + "_d", pd)
    return gwg, gwu, gwd


def norm_mm(h, gain, w, name):
    t, d = h.shape
    nb, _, bw = w.shape
    tm = _row_tile(t, 512)

    def body(h_ref, g_ref, w_ref, hn_ref, z_ref, hn_scr):
        @pl.when(pl.program_id(1) == 0)
        def _():
            x = h_ref[...]
            hn = (x * _rstd(x) * g_ref[...]).astype(BF16)
            hn_scr[...] = hn
            hn_ref[...] = hn
        z_ref[...] = _dot(hn_scr[...], w_ref[...])

    return pl.pallas_call(
        body, name=name, grid=(t // tm, nb),
        in_specs=[_bs((tm, d), lambda i, k: (i, 0)), _bs((1, d), lambda i, k: (0, 0)), _bs((None, d, bw), lambda i, k: (k, 0, 0))],
        out_specs=[_bs((tm, d), lambda i, k: (i, 0)), _bs((tm, bw), lambda i, k: (i, k))],
        out_shape=[S((t, d), BF16), S((t, nb * bw), F32)],
        scratch_shapes=[pltpu.VMEM((tm, d), BF16)],
        compiler_params=_cp("parallel", "arbitrary"),
    )(h, gain, w)


def nt_mm(a, w, name):
    t, k = a.shape
    n = w.shape[0]
    tm = _row_tile(t, 512)

    def body(a_ref, w_ref, o_ref):
        o_ref[...] = _dot_nt(a_ref[...].astype(BF16), w_ref[...])

    return pl.pallas_call(
        body, name=name, grid=(t // tm,),
        in_specs=[_bs((tm, k), lambda i: (i, 0)), _bs((n, k), lambda i: (0, 0))],
        out_specs=_bs((tm, n), lambda i: (i, 0)),
        out_shape=S((t, n), F32),
        compiler_params=_cp("parallel"),
    )(a, w)


def _head_mean_matrix():
    m = np.kron(np.eye(N_HEADS, dtype=np.float32), np.full((HEAD_DIM, HEAD_DIM), 1.0 / HEAD_DIM, np.float32))
    return jnp.asarray(m, BF16)


def _head_sum_matrix():
    m = np.kron(np.eye(N_HEADS, dtype=np.float32), np.ones((HEAD_DIM, HEAD_DIM), np.float32))
    return jnp.asarray(m, BF16)


def _rel_bucket_np(dist):
    max_exact = REL_BUCKETS // 2
    n = np.maximum(dist, 1).astype(np.float32)
    large = max_exact + (np.log(n / np.float32(max_exact)) / np.float32(math.log(REL_MAX_DIST / max_exact))
                         * np.float32(REL_BUCKETS - max_exact)).astype(np.int32)
    large = np.minimum(large, REL_BUCKETS - 1)
    return np.where(dist < max_exact, dist, large)


def _band_tables():
    qi = np.arange(BAND)[:, None]
    kj = np.arange(2 * BAND)[None, :]
    dist_q = qi + BAND - kj
    qq = np.arange(2 * BAND)[:, None]
    kk = np.arange(BAND)[None, :]
    dist_k = qq - kk
    out = []
    for dist in (dist_q, dist_k):
        valid = (dist >= 0) & (dist <= BAND)
        bucket = np.stack([_rel_bucket_np(np.clip(dist, 0, BAND) * d) for d in DILATIONS])
        out.append((bucket, valid))
    return out


def band_bias(rel_bias):
    (bq, vq), (bk, vk) = _band_tables()
    tq = jnp.where(vq[None, :, :, None], rel_bias[bq], NEG)
    tk = jnp.where(vk[None, :, :, None], rel_bias[bk], NEG)
    return jnp.transpose(tq, (0, 3, 1, 2)), jnp.transpose(tk, (0, 3, 1, 2))


def hyb_prep(z, q_gain, k_gain, name):
    t = z.shape[0]
    tm = _row_tile(t, 512)
    seg = _head_mean_matrix()

    def body(q_ref, k_ref, v_ref, qg_ref, kg_ref, seg_ref, qo_ref, ko_ref, vo_ref):
        q = q_ref[...]
        k = k_ref[...]
        qo_ref[...] = (q * lax.rsqrt(_seg_dot(q * q, seg_ref[...]) + EPS) * qg_ref[...]).astype(BF16)
        ko_ref[...] = (k * lax.rsqrt(_seg_dot(k * k, seg_ref[...]) + EPS) * kg_ref[...]).astype(BF16)
        vo_ref[...] = v_ref[...].astype(BF16)

    col = lambda c: _bs((tm, ATTN_W), lambda i: (i, c))
    vec = _bs((1, ATTN_W), lambda i: (0, 0))
    out = _bs((tm, ATTN_W), lambda i: (i, 0))
    return pl.pallas_call(
        body, name=name, grid=(t // tm,),
        in_specs=[col(3), col(4), col(5), vec, vec, _bs((ATTN_W, ATTN_W), lambda i: (0, 0))],
        out_specs=[out, out, out],
        out_shape=[S((t, ATTN_W), BF16)] * 3,
        compiler_params=_cp("parallel"),
    )(z, z, z, q_gain, k_gain, seg)


def _lane_lo(shape):
    return lax.broadcasted_iota(jnp.int32, shape, 1) < HEAD_DIM


def attn_fwd(q, k, v, bias, dil, name):
    t = q.shape[0]
    sub = t // dil
    nb = sub // BAND
    qv, kv, vv = (a.reshape(sub, dil * ATTN_W) for a in (q, k, v))

    def body(q_ref, kp_ref, kc_ref, vp_ref, vc_ref, b_ref, o_ref, l_ref):
        n = pl.program_id(1)
        first = n == 0
        colk = lax.broadcasted_iota(jnp.int32, (BAND, 2 * BAND), 1)
        for j in range(N_HEADS // 2):
            sl = slice(2 * HEAD_DIM * j, 2 * HEAD_DIM * (j + 1))
            qp = q_ref[:, sl]
            kk = jnp.concatenate([kp_ref[:, sl], kc_ref[:, sl]], axis=0)
            vv_ = jnp.concatenate([vp_ref[:, sl], vc_ref[:, sl]], axis=0)
            lo = _lane_lo(qp.shape)
            outs, lses = [], []
            for hh in range(2):
                qm = jnp.where(lo if hh == 0 else jnp.logical_not(lo), qp, jnp.zeros_like(qp))
                s = _dot_nt(qm, kk) * (HEAD_DIM ** -0.5) + b_ref[2 * j + hh]
                s = jnp.where(jnp.logical_and(first, colk < BAND), NEG, s)
                m = jnp.max(s, axis=-1, keepdims=True)
                p = jnp.exp(s - m)
                l = jnp.sum(p, axis=-1, keepdims=True)
                outs.append(_dot(p.astype(BF16), vv_) / l)
                lses.append(m + jnp.log(l))
            o_ref[:, sl] = jnp.where(lo, outs[0], outs[1])
            l_ref[:, sl] = jnp.where(lo, lses[0], lses[1])

    cur = _bs((BAND, ATTN_W), lambda r, n: (n, r))
    prv = _bs((BAND, ATTN_W), lambda r, n: (jnp.maximum(n - 1, 0), r))
    o, lse = pl.pallas_call(
        body, name=name, grid=(dil, nb),
        in_specs=[cur, prv, cur, prv, cur, _bs((N_HEADS, BAND, 2 * BAND), lambda r, n: (0, 0, 0))],
        out_specs=[cur, cur],
        out_shape=[S((sub, dil * ATTN_W), F32)] * 2,
        compiler_params=_cp("parallel", "arbitrary"),
    )(qv, kv, kv, vv, vv, bias)
    return o.reshape(t, ATTN_W), lse.reshape(t, ATTN_W)


def hyb_post(z, conv_w, os_, lses, name):
    t = z.shape[0]
    tm = _row_tile(t, 512)

    def body(gb_ref, gc_ref, cx_ref, gch_ref, cxh_ref, w_ref, o1, o2, o3, l1, l2, l3, y_ref, lt_ref, ya_ref):
        i = pl.program_id(0)
        m = gc_ref[...] * cx_ref[...]
        mh = jnp.where(i == 0, 0.0, gch_ref[...] * cxh_ref[...])
        conv = w_ref[0:1, :] * _shift_down(m, mh, 2) + w_ref[1:2, :] * _shift_down(m, mh, 1) + w_ref[2:3, :] * m
        y_ref[0] = (gb_ref[...] * conv).astype(BF16)
        la, lb, lc = l1[...], l2[...], l3[...]
        mx = jnp.maximum(jnp.maximum(la, lb), lc)
        ea, eb, ec = jnp.exp(la - mx), jnp.exp(lb - mx), jnp.exp(lc - mx)
        den = ea + eb + ec
        ya = (ea * o1[...] + eb * o2[...] + ec * o3[...]) / den
        y_ref[1] = ya.astype(BF16)
        ya_ref[...] = ya
        lt_ref[...] = mx + jnp.log(den)

    hb = tm // 8
    col = lambda c: _bs((tm, CONV_W), lambda i: (i, c))
    halo = lambda c: _bs((8, CONV_W), lambda i: (jnp.maximum(i * hb - 1, 0), c))
    row = _bs((tm, ATTN_W), lambda i: (i, 0))
    return pl.pallas_call(
        body, name=name, grid=(t // tm,),
        in_specs=[col(0), col(1), col(2), halo(1), halo(2), _bs((3, CONV_W), lambda i: (0, 0))] + [row] * 6,
        out_specs=[_bs((2, tm, ATTN_W), lambda i: (0, i, 0)), row, row],
        out_shape=[S((2, t, ATTN_W), BF16), S((t, ATTN_W), F32), S((t, ATTN_W), F32)],
        compiler_params=_cp("parallel"),
    )(z, z, z, z, z, conv_w, *os_, *lses)


def attn_delta(dy, ya, name):
    t = ya.shape[0]
    tm = _row_tile(t, 512)
    seg = _head_sum_matrix()

    def body(dy_ref, ya_ref, seg_ref, dl_ref, db_ref):
        dya = dy_ref[...]
        dl_ref[...] = _seg_dot(dya * ya_ref[...], seg_ref[...])
        db_ref[...] = dya.astype(BF16)

    row = _bs((tm, ATTN_W), lambda i: (i, 0))
    return pl.pallas_call(
        body, name=name, grid=(t // tm,),
        in_specs=[_bs((tm, ATTN_W), lambda i: (i, 1)), row, _bs((ATTN_W, ATTN_W), lambda i: (0, 0))],
        out_specs=[row, row],
        out_shape=[S((t, ATTN_W), F32), S((t, ATTN_W), BF16)],
        compiler_params=_cp("parallel"),
    )(dy, ya, seg)


def attn_bwd_dq(q, k, v, dya, lt, delta, bias, dil, name):
    t = q.shape[0]
    sub = t // dil
    nb = sub // BAND
    qv, kv, vv, dv_, lv, ev = (a.reshape(sub, dil * ATTN_W) for a in (q, k, v, dya, lt, delta))

    def body(q_ref, kp_ref, kc_ref, vp_ref, vc_ref, do_ref, l_ref, e_ref, b_ref, dq_ref, db_ref):
        r, n = pl.program_id(0), pl.program_id(1)

        @pl.when(jnp.logical_and(r == 0, n == 0))
        def _():
            db_ref[...] = jnp.zeros_like(db_ref)
        first = n == 0
        colk = lax.broadcasted_iota(jnp.int32, (BAND, 2 * BAND), 1)
        for j in range(N_HEADS // 2):
            sl = slice(2 * HEAD_DIM * j, 2 * HEAD_DIM * (j + 1))
            qp, dop = q_ref[:, sl], do_ref[:, sl]
            kk = jnp.concatenate([kp_ref[:, sl], kc_ref[:, sl]], axis=0)
            vv_ = jnp.concatenate([vp_ref[:, sl], vc_ref[:, sl]], axis=0)
            lo = _lane_lo(qp.shape)
            dqs = []
            for hh in range(2):
                msk = lo if hh == 0 else jnp.logical_not(lo)
                c0 = 2 * HEAD_DIM * j + HEAD_DIM * hh
                qm = jnp.where(msk, qp, jnp.zeros_like(qp))
                dom = jnp.where(msk, dop, jnp.zeros_like(dop))
                s = _dot_nt(qm, kk) * (HEAD_DIM ** -0.5) + b_ref[2 * j + hh]
                s = jnp.where(jnp.logical_and(first, colk < BAND), NEG, s)
                p = jnp.exp(s - l_ref[:, c0:c0 + 1])
                ds = p * (_dot_nt(dom, vv_) - e_ref[:, c0:c0 + 1])
                db_ref[2 * j + hh] += ds
                dqs.append(_dot(ds.astype(BF16), kk) * (HEAD_DIM ** -0.5))
            dq_ref[:, sl] = jnp.where(lo, dqs[0], dqs[1])

    cur = _bs((BAND, ATTN_W), lambda r, n: (n, r))
    prv = _bs((BAND, ATTN_W), lambda r, n: (jnp.maximum(n - 1, 0), r))
    tab = _bs((N_HEADS, BAND, 2 * BAND), lambda r, n: (0, 0, 0))
    dq, db = pl.pallas_call(
        body, name=name, grid=(dil, nb),
        in_specs=[cur, prv, cur, prv, cur, cur, cur, cur, tab],
        out_specs=[cur, tab],
        out_shape=[S((sub, dil * ATTN_W), F32), S((N_HEADS, BAND, 2 * BAND), F32)],
        compiler_params=_cp("arbitrary", "arbitrary"),
    )(qv, kv, kv, vv, vv, dv_, lv, ev, bias)
    return dq.reshape(t, ATTN_W), db


def attn_bwd_dkv(q, k, v, dya, lt, delta, bias_k, dil, name):
    t = q.shape[0]
    sub = t // dil
    nb = sub // BAND
    qv, kv, vv, dv_, lv, ev = (a.reshape(sub, dil * ATTN_W) for a in (q, k, v, dya, lt, delta))

    def body(k_ref, v_ref, qc_ref, qn_ref, dc_ref, dn_ref, lc_ref, ln_ref, ec_ref, en_ref, b_ref, dk_ref, dv_ref):
        n = pl.program_id(1)
        last = n == nb - 1
        rowq = lax.broadcasted_iota(jnp.int32, (2 * BAND, BAND), 0)
        for j in range(N_HEADS // 2):
            sl = slice(2 * HEAD_DIM * j, 2 * HEAD_DIM * (j + 1))
            kp, vp = k_ref[:, sl], v_ref[:, sl]
            qq = jnp.concatenate([qc_ref[:, sl], qn_ref[:, sl]], axis=0)
            do = jnp.concatenate([dc_ref[:, sl], dn_ref[:, sl]], axis=0)
            lo = _lane_lo(qq.shape)
            dks, dvs = [], []
            for hh in range(2):
                msk = lo if hh == 0 else jnp.logical_not(lo)
                c0 = 2 * HEAD_DIM * j + HEAD_DIM * hh
                qm = jnp.where(msk, qq, jnp.zeros_like(qq))
                dom = jnp.where(msk, do, jnp.zeros_like(do))
                ll = jnp.concatenate([lc_ref[:, c0:c0 + 1], ln_ref[:, c0:c0 + 1]], axis=0)
                ee = jnp.concatenate([ec_ref[:, c0:c0 + 1], en_ref[:, c0:c0 + 1]], axis=0)
                s = _dot_nt(qm, kp) * (HEAD_DIM ** -0.5) + b_ref[2 * j + hh]
                s = jnp.where(jnp.logical_and(last, rowq >= BAND), NEG, s)
                p = jnp.exp(s - ll)
                ds = p * (_dot_nt(dom, vp) - ee)
                dvs.append(_dot_tn(p.astype(BF16), dom))
                dks.append(_dot_tn(ds.astype(BF16), qm) * (HEAD_DIM ** -0.5))
            lo_k = _lane_lo(kp.shape)
            dk_ref[:, sl] = jnp.where(lo_k, dks[0], dks[1])
            dv_ref[:, sl] = jnp.where(lo_k, dvs[0], dvs[1])

    cur = _bs((BAND, ATTN_W), lambda r, n: (n, r))
    nxt = _bs((BAND, ATTN_W), lambda r, n: (jnp.minimum(n + 1, nb - 1), r))
    tab = _bs((N_HEADS, 2 * BAND, BAND), lambda r, n: (0, 0, 0))
    dk, dv = pl.pallas_call(
        body, name=name, grid=(dil, nb),
        in_specs=[cur, cur, cur, nxt, cur, nxt, cur, nxt, cur, nxt, tab],
        out_specs=[cur, cur],
        out_shape=[S((sub, dil * ATTN_W), F32)] * 2,
        compiler_params=_cp("parallel", "arbitrary"),
    )(kv, vv, qv, qv, dv_, dv_, lv, lv, ev, ev, bias_k)
    return dk.reshape(t, ATTN_W), dv.reshape(t, ATTN_W)


def hyb_dz(z, dy, conv_w, q_gain, k_gain, dqs, dks, dvs, name):
    t = z.shape[0]
    tm = _row_tile(t, 256)
    nt = t // tm
    seg = _head_mean_matrix()

    def body(gb_ref, gc_ref, cx_ref, q_ref, k_ref, gch_ref, cxh_ref, gbn_ref, dyc_ref, dyn_ref, w_ref, qg_ref, kg_ref, seg_ref,
             dq1, dq2, dq3, dk1, dk2, dk3, dv1, dv2, dv3, dz_ref, dw_ref, dqg_ref, dkg_ref):
        i = pl.program_id(0)

        @pl.when(i == 0)
        def _():
            dw_ref[...] = jnp.zeros_like(dw_ref)
            dqg_ref[...] = jnp.zeros_like(dqg_ref)
            dkg_ref[...] = jnp.zeros_like(dkg_ref)
        gb, gc, cx, dyc = gb_ref[...], gc_ref[...], cx_ref[...], dyc_ref[...]
        m = gc * cx
        mh = jnp.where(i == 0, 0.0, gch_ref[...] * cxh_ref[...])
        m1, m2 = _shift_down(m, mh, 1), _shift_down(m, mh, 2)
        conv = w_ref[0:1, :] * m2 + w_ref[1:2, :] * m1 + w_ref[2:3, :] * m
        dconv = dyc * gb
        dcn = jnp.where(i == nt - 1, 0.0, dyn_ref[...] * gbn_ref[...])
        dm = w_ref[2:3, :] * dconv + w_ref[1:2, :] * _shift_up(dconv, dcn, 1) + w_ref[0:1, :] * _shift_up(dconv, dcn, 2)
        dz_ref[:, 0:CONV_W] = (dyc * conv).astype(BF16)
        dz_ref[:, CONV_W:2 * CONV_W] = (dm * cx).astype(BF16)
        dz_ref[:, 2 * CONV_W:3 * CONV_W] = (dm * gc).astype(BF16)
        dw_ref[0:1, :] += jnp.sum(dconv * m2, axis=0, keepdims=True)
        dw_ref[1:2, :] += jnp.sum(dconv * m1, axis=0, keepdims=True)
        dw_ref[2:3, :] += jnp.sum(dconv * m, axis=0, keepdims=True)
        base = 3 * CONV_W
        for idx, (x_ref, g_ref, parts, dgain_ref) in enumerate(((q_ref, qg_ref, (dq1, dq2, dq3), dqg_ref),
                                                                  (k_ref, kg_ref, (dk1, dk2, dk3), dkg_ref))):
            x = x_ref[...]
            dxh = parts[0][...] + parts[1][...] + parts[2][...]
            r = lax.rsqrt(_seg_dot(x * x, seg_ref[...]) + EPS)
            xhat = x * r
            tt = dxh * g_ref[...]
            dx = r * (tt - xhat * _seg_dot(tt * xhat, seg_ref[...]))
            dz_ref[:, base + idx * ATTN_W:base + (idx + 1) * ATTN_W] = dx.astype(BF16)
            dgain_ref[...] += jnp.sum(dxh * xhat, axis=0, keepdims=True)
        dz_ref[:, base + 2 * ATTN_W:base + 3 * ATTN_W] = (dv1[...] + dv2[...] + dv3[...]).astype(BF16)

    hb = tm // 8
    col = lambda c: _bs((tm, CONV_W), lambda i: (i, c))
    prev = lambda c: _bs((8, CONV_W), lambda i: (jnp.maximum(i * hb - 1, 0), c))
    nxt = lambda c: _bs((8, CONV_W), lambda i: (jnp.minimum((i + 1) * hb, t // 8 - 1), c))
    row = _bs((tm, ATTN_W), lambda i: (i, 0))
    vec = _bs((1, ATTN_W), lambda i: (0, 0))
    return pl.pallas_call(
        body, name=name, grid=(nt,),
        in_specs=[col(0), col(1), col(2), col(3), col(4), prev(1), prev(2), nxt(0), col(0), nxt(0),
                  _bs((3, CONV_W), lambda i: (0, 0)), vec, vec, _bs((ATTN_W, ATTN_W), lambda i: (0, 0))] + [row] * 9,
        out_specs=[_bs((tm, 6 * CONV_W), lambda i: (i, 0)), _bs((3, CONV_W), lambda i: (0, 0)), vec, vec],
        out_shape=[S((t, 6 * CONV_W), BF16), S((3, CONV_W), F32), S((1, ATTN_W), F32), S((1, ATTN_W), F32)],
        compiler_params=_cp("arbitrary"),
    )(z, z, z, z, z, z, z, z, dy, dy, conv_w, q_gain, k_gain, seg, *dqs, *dks, *dvs)


def rel_bias_grad(dbs, name):
    (bq, vq), _ = _band_tables()
    onehot = np.zeros((len(DILATIONS), REL_BUCKETS, BAND * 2 * BAND), np.float32)
    for g in range(len(DILATIONS)):
        idx = bq[g].reshape(-1)
        ok = vq.reshape(-1)
        onehot[g, idx[ok], np.nonzero(ok)[0]] = 1.0
    onehot = jnp.asarray(onehot, BF16)
    flat = [d.reshape(N_HEADS, BAND * 2 * BAND) for d in dbs]

    def body(oh_ref, d1, d2, d3, o_ref):
        acc = jnp.zeros((REL_BUCKETS, N_HEADS), F32)
        for g, d in enumerate((d1, d2, d3)):
            x = d[...]
            hi = x.astype(BF16)
            lo = (x - hi.astype(F32)).astype(BF16)
            acc += _dot_nt(oh_ref[g], hi) + _dot_nt(oh_ref[g], lo)
        o_ref[...] = acc

    full = lambda shp: _bs(shp, lambda: tuple(0 for _ in shp))
    return pl.pallas_call(
        body, name=name,
        in_specs=[full(onehot.shape)] + [full(flat[0].shape)] * 3,
        out_specs=full((REL_BUCKETS, N_HEADS)),
        out_shape=S((REL_BUCKETS, N_HEADS), F32),
        compiler_params=pltpu.CompilerParams(vmem_limit_bytes=VMEM_LIMIT),
    )(onehot, *flat)


def _lru_gates(xb, wa_ref, wx_ref, ba, bx):
    xb16 = xb.astype(BF16)
    ga = jnp.concatenate([_dot(xb16[:, LRU_BLOCK * g:LRU_BLOCK * (g + 1)], wa_ref[g]) for g in range(LRU_BLOCKS)], axis=1) + ba
    gx = jnp.concatenate([_dot(xb16[:, LRU_BLOCK * g:LRU_BLOCK * (g + 1)], wx_ref[g]) for g in range(LRU_BLOCKS)], axis=1) + bx
    return ga, gx


def _lru_coeffs(ga, gx, lam):
    sga = _sigmoid(ga)
    sp = _softplus(-lam)
    log_a = -LRU_C * sga * sp
    a = jnp.exp(log_a)
    one_m_a2 = _neg_expm1(2.0 * log_a)
    return sga, sp, a, one_m_a2, jnp.sqrt(one_m_a2), _sigmoid(gx)


def rec_fwd(z, conv_w, conv_b, wa, wx, ba, bx, lam, name):
    t = z.shape[0]
    w = z.shape[1] // 2
    tm = _row_tile(t, 256)

    def body(xp_ref, xh_ref, yb_ref, cw_ref, cb_ref, wa_ref, wx_ref, ba_ref, bx_ref, lam_ref,
             xb_ref, ga_ref, gx_ref, hs_ref, out_ref, carry):
        i = pl.program_id(0)

        @pl.when(i == 0)
        def _():
            carry[...] = jnp.zeros_like(carry)
        xp = xp_ref[...]
        xh = jnp.where(i == 0, 0.0, xh_ref[...])
        xb = cb_ref[...] + cw_ref[3:4, :] * xp
        for j in range(3):
            xb = xb + cw_ref[j:j + 1, :] * _shift_down(xp, xh, 3 - j)
        ga, gx = _lru_gates(xb, wa_ref, wx_ref, ba_ref[...], bx_ref[...])
        _, _, a, _, sq, sgx = _lru_coeffs(ga, gx, lam_ref[...])
        aa, bb = a, sq * sgx * xb
        s = 1
        while s < tm:
            bb = aa * _roll_fill(bb, s, 0.0, False) + bb
            aa = aa * _roll_fill(aa, s, 1.0, False)
            s *= 2
        hs = aa * carry[0:1, :] + bb
        xb_ref[...] = xb
        ga_ref[...] = ga
        gx_ref[...] = gx
        hs_ref[...] = hs
        carry[0:1, :] = hs_ref[tm - 1:tm, :]
        gy, _ = _gelu_and_grad(yb_ref[...])
        out_ref[...] = (hs * gy).astype(BF16)

    hb = tm // 8
    row = _bs((tm, w), lambda i: (i, 0))
    vec = _bs((1, w), lambda i: (0, 0))
    wsp = _bs((LRU_BLOCKS, LRU_BLOCK, LRU_BLOCK), lambda i: (0, 0, 0))
    return pl.pallas_call(
        body, name=name, grid=(t // tm,),
        in_specs=[row, _bs((8, w), lambda i: (jnp.maximum(i * hb - 1, 0), 0)), _bs((tm, w), lambda i: (i, 1)),
                  _bs((4, w), lambda i: (0, 0)), vec, wsp, wsp, vec, vec, vec],
        out_specs=[row] * 5,
        out_shape=[S((t, w), F32)] * 4 + [S((t, w), BF16)],
        scratch_shapes=[pltpu.VMEM((8, w), F32)],
        compiler_params=_cp("arbitrary"),
    )(z, z, z, conv_w, conv_b, wa, wx, ba, bx, lam)


def rec_bwd(d_out, z, xb, ga, gx, hs, conv_w, wa, wx, lam, name):
    t = z.shape[0]
    w = z.shape[1] // 2
    tm = _row_tile(t, 256)
    nt = t // tm

    def body(do_ref, xp_ref, xph_ref, yb_ref, xb_ref, ga_ref, gx_ref, hs_ref, hsh_ref, cw_ref, wa_ref, wx_ref, lam_ref,
             dz_ref, dga_ref, dgx_ref, sm_ref, c_lam, c_a, c_dxb):
        i = pl.program_id(0)

        @pl.when(i == 0)
        def _():
            sm_ref[...] = jnp.zeros_like(sm_ref)
            c_lam[...] = jnp.zeros_like(c_lam)
            c_a[...] = jnp.zeros_like(c_a)
            c_dxb[...] = jnp.zeros_like(c_dxb)
        d_o, yb, xb, hs = do_ref[...], yb_ref[...], xb_ref[...], hs_ref[...]
        lam = lam_ref[...]
        gy, dgy = _gelu_and_grad(yb)
        dz_ref[:, w:2 * w] = (d_o * hs * dgy).astype(BF16)
        sga, sp, a, one_m_a2, sq, sgx = _lru_coeffs(ga_ref[...], gx_ref[...], lam)
        aa = _shift_up(a, c_a[...], 1)
        bb = d_o * gy
        s = 1
        while s < tm:
            bb = aa * _roll_fill(bb, s, 0.0, True) + bb
            aa = aa * _roll_fill(aa, s, 1.0, True)
            s *= 2
        lmb = aa * c_lam[0:1, :] + bb
        c_a[...] = a[0:8, :]
        c_lam[...] = lmb[0:8, :]
        hprev = _shift_down(hs, jnp.where(i == nt - 1, 0.0, hsh_ref[...]), 1)
        d_sq = lmb * sgx * xb
        d_sgx = lmb * sq * xb
        d_log_a = lmb * hprev * a - d_sq * (1.0 - one_m_a2) / sq
        dga = d_log_a * (-LRU_C * sp) * sga * (1.0 - sga)
        dgx = d_sgx * sgx * (1.0 - sgx)
        dga16, dgx16 = dga.astype(BF16), dgx.astype(BF16)
        dga_ref[...] = dga16
        dgx_ref[...] = dgx16
        dxb = lmb * sq * sgx + jnp.concatenate(
            [_dot_nt(dga16[:, LRU_BLOCK * g:LRU_BLOCK * (g + 1)], wa_ref[g]) + _dot_nt(dgx16[:, LRU_BLOCK * g:LRU_BLOCK * (g + 1)], wx_ref[g])
             for g in range(LRU_BLOCKS)], axis=1)
        nxt = c_dxb[...]
        dxp = cw_ref[3:4, :] * dxb
        for j in range(3):
            dxp = dxp + cw_ref[j:j + 1, :] * _shift_up(dxb, nxt, 3 - j)
        c_dxb[...] = dxb[0:8, :]
        dz_ref[:, 0:w] = dxp.astype(BF16)
        xp = xp_ref[...]
        xph = jnp.where(i == nt - 1, 0.0, xph_ref[...])
        sm_ref[0:1, :] += jnp.sum(dga, axis=0, keepdims=True)
        sm_ref[1:2, :] += jnp.sum(dgx, axis=0, keepdims=True)
        sm_ref[2:3, :] += jnp.sum(d_log_a * (-LRU_C * sga), axis=0, keepdims=True) * (-_sigmoid(-lam))
        sm_ref[3:4, :] += jnp.sum(dxb, axis=0, keepdims=True)
        for j in range(4):
            sm_ref[4 + j:5 + j, :] += jnp.sum(dxb * _shift_down(xp, xph, 3 - j), axis=0, keepdims=True)

    hb = tm // 8
    rev = lambda c: _bs((tm, w), lambda i: (nt - 1 - i, c))
    halo = lambda c: _bs((8, w), lambda i: (jnp.maximum((nt - 1 - i) * hb - 1, 0), c))
    vec = _bs((1, w), lambda i: (0, 0))
    wsp = _bs((LRU_BLOCKS, LRU_BLOCK, LRU_BLOCK), lambda i: (0, 0, 0))
    return pl.pallas_call(
        body, name=name, grid=(nt,),
        in_specs=[rev(0), rev(0), halo(0), rev(1), rev(0), rev(0), rev(0), rev(0), halo(0),
                  _bs((4, w), lambda i: (0, 0)), wsp, wsp, vec],
        out_specs=[_bs((tm, 2 * w), lambda i: (nt - 1 - i, 0)), rev(0), rev(0), _bs((8, w), lambda i: (0, 0))],
        out_shape=[S((t, 2 * w), BF16), S((t, w), BF16), S((t, w), BF16), S((8, w), F32)],
        scratch_shapes=[pltpu.VMEM((8, w), F32)] * 3,
        compiler_params=_cp("arbitrary"),
    )(d_out, z, z, z, xb, ga, gx, hs, hs, conv_w, wa, wx, lam)


def ple_fwd(h, gain, wpg, layer, p, wpp, name):
    t, d = h.shape
    pd = p.shape[1]
    nk, _, rb, _ = wpg.shape
    cb = wpp.shape[3]
    tm = _row_tile(t, 512)

    def body(h_ref, g_ref, wg_ref, p_ref, wp_ref, o_ref, hn_ref, gp_ref, pp_ref):
        x = h_ref[...]
        hn = (x * _rstd(x) * g_ref[...]).astype(BF16)
        gp = _dot(hn[:, 0:rb], wg_ref[0])
        for k in range(1, nk):
            gp = gp + _dot(hn[:, rb * k:rb * (k + 1)], wg_ref[k])
        p16 = p_ref[...].astype(BF16)
        pp = jnp.concatenate([_dot(p16, wp_ref[k]) for k in range(nk)], axis=1)
        hn_ref[...] = hn
        gp_ref[...] = gp
        pp_ref[...] = pp
        o_ref[...] = x + _sigmoid(gp) * pp

    row = _bs((tm, d), lambda i: (i, 0))
    return pl.pallas_call(
        body, name=name, grid=(t // tm,),
        in_specs=[row, _bs((1, d), lambda i: (0, 0)), _bs((nk, None, rb, d), lambda i: (0, layer, 0, 0)),
                  _bs((tm, pd), lambda i: (i, 0)), _bs((nk, None, pd, cb), lambda i: (0, layer, 0, 0))],
        out_specs=[row] * 4,
        out_shape=[S((t, d), F32), S((t, d), BF16), S((t, d), F32), S((t, d), F32)],
        compiler_params=_cp("parallel"),
    )(h, gain, wpg, p, wpp)


def ple_bwd(dh, h, gain, wpg, layer, gp, pp, name):
    t, d = h.shape
    nk, _, rb, _ = wpg.shape
    tm = _row_tile(t, 512)

    def body(dh_ref, h_ref, g_ref, wg_ref, gp_ref, pp_ref, o_ref, dgp_ref, dpp_ref, dg_ref):
        @pl.when(pl.program_id(0) == 0)
        def _():
            dg_ref[...] = jnp.zeros_like(dg_ref)
        d_h = dh_ref[...]
        gate = _sigmoid(gp_ref[...])
        dgp = (d_h * pp_ref[...] * gate * (1.0 - gate)).astype(BF16)
        dgp_ref[...] = dgp
        dpp_ref[...] = (d_h * gate).astype(BF16)
        dhn = jnp.concatenate([_dot_nt(dgp, wg_ref[k]) for k in range(nk)], axis=1)
        dx, dgain = _rmsnorm_bwd(h_ref[...], g_ref[...], dhn)
        o_ref[...] = d_h + dx
        dg_ref[...] += dgain

    row = _bs((tm, d), lambda i: (i, 0))
    vec = _bs((1, d), lambda i: (0, 0))
    return pl.pallas_call(
        body, name=name, grid=(t // tm,),
        in_specs=[row, row, vec, _bs((nk, None, rb, d), lambda i: (0, layer, 0, 0)), row, row],
        out_specs=[row, row, row, vec],
        out_shape=[S((t, d), F32), S((t, d), BF16), S((t, d), BF16), S((1, d), F32)],
        compiler_params=_cp("arbitrary"),
    )(dh, h, gain, wpg, gp, pp)


def loss_and_grad(y, target, name):
    t, d = y.shape
    tm = _row_tile(t, 512)

    def body(y_ref, t_ref, l_ref, g_ref):
        @pl.when(pl.program_id(0) == 0)
        def _():
            l_ref[...] = jnp.zeros_like(l_ref)
        err = y_ref[...] - t_ref[...]
        g_ref[...] = err * (1.0 / d)
        l_ref[...] += jnp.sum(jnp.sum(err * err, axis=1, keepdims=True), axis=0, keepdims=True) * (0.5 / d)

    row = _bs((tm, d), lambda i: (i, 0))
    return pl.pallas_call(
        body, name=name, grid=(t // tm,),
        in_specs=[row, row],
        out_specs=[_bs((1, 1), lambda i: (0, 0)), row],
        out_shape=[S((1, 1), F32), S((t, d), F32)],
        compiler_params=_cp("arbitrary"),
    )(y, target)


def _vec(a, i):
    return a[i:i + 1]


def local_step(x, p, target, w):
    t = x.shape[0]
    tm = _row_tile(t, 512)
    grads = {}
    saved = []
    h = x
    bias_q, bias_k = band_bias(w["rel_bias"])
    qg = jnp.tile(w["hyb_q_gain"], (1, N_HEADS))
    kg = jnp.tile(w["hyb_k_gain"], (1, N_HEADS))
    w_hyb_in = w["hyb_w_in"].reshape(N_SHARD, D_MODEL, -1)
    w_rec_in = w["rec_w_in"].reshape(N_SHARD, D_MODEL, -1)
    w_hyb_out = w["hyb_w_out"].reshape(D_MODEL, D_MODEL)
    w_rec_out = w["rec_w_out"].reshape(D_MODEL, D_MODEL)
    lru_wa, lru_wx = (jnp.transpose(w[n].reshape(N_SHARD, LRU_BLOCKS, 64, LRU_BLOCK), (1, 0, 2, 3)).reshape(LRU_BLOCKS, LRU_BLOCK, LRU_BLOCK)
                      for n in ("lru_wa", "lru_wx"))
    for i in range(2):
        s = {}
        s["h0"] = h
        s["hn1"], s["g1"], s["u1"], s["a1"] = ffn_up(h, _vec(w["ffn1_norm"], i), w["ffn1_w_gate"], w["ffn1_w_up"], i, f"ffn1_up_{i}")
        h = ffn_down(s["a1"], w["ffn1_w_down"], i, h, f"ffn1_down_{i}")
        s["h1"] = h
        if i == 0:
            s["hnm"], s["z"] = norm_mm(h, _vec(w["mix_norm"], i), w_hyb_in, "hyb_in")
            s["q"], s["k"], s["v"] = hyb_prep(s["z"], qg, kg, "hyb_prep")
            os_, lses = [], []
            for g, dil in enumerate(DILATIONS):
                o, l = attn_fwd(s["q"], s["k"], s["v"], bias_q[g], dil, f"attn_fwd_{dil}")
                os_.append(o)
                lses.append(l)
            s["y2"], s["lt"], s["ya"] = hyb_post(s["z"], w["hyb_conv_w"], os_, lses, "hyb_post")
            h = mm_acc(s["y2"], _bs((None, tm, ATTN_W), lambda r, k: (k, r, 0)),
                       w_hyb_out.reshape(2, ATTN_W, D_MODEL), _bs((None, ATTN_W, D_MODEL), lambda r, k: (k, 0, 0)),
                       h, 1.0, 2, t, D_MODEL, "hyb_out", tm)
        else:
            s["hnm"], s["z"] = norm_mm(h, _vec(w["mix_norm"], i), w_rec_in, "rec_in")
            s["xb"], s["ga"], s["gx"], s["hs"], s["ro"] = rec_fwd(
                s["z"], w["rec_conv_w"], w["rec_conv_b"], lru_wa, lru_wx, w["lru_ba"], w["lru_bx"], w["lru_lambda"], "rec_fwd")
            h = mm_acc(s["ro"], _bs((tm, D_MODEL), lambda r, k: (r, 0)), w_rec_out, _bs((D_MODEL, D_MODEL), lambda r, k: (0, 0)),
                       h, 1.0, 1, t, D_MODEL, "rec_out", tm)
        s["h2"] = h
        s["hn2"], s["g2"], s["u2"], s["a2"] = ffn_up(h, _vec(w["ffn2_norm"], i), w["ffn2_w_gate"], w["ffn2_w_up"], i, f"ffn2_up_{i}")
        h = ffn_down(s["a2"], w["ffn2_w_down"], i, h, f"ffn2_down_{i}")
        s["h3"] = h
        h, s["hnp"], s["gp"], s["pp"] = ple_fwd(h, _vec(w["ple_norm"], i), w["ple_w_gate"], i, p[i], w["ple_w_proj"], f"ple_fwd_{i}")
        saved.append(s)
    loss, dh = loss_and_grad(h, target, "loss")

    norm_g = {n: [None, None] for n in ("ffn1_norm", "mix_norm", "ffn2_norm", "ple_norm")}
    prev_f1 = prev_f2 = None
    prev_pg = prev_pp = None
    for i in (1, 0):
        s = saved[i]
        dh_out = dh
        dh, dgp, dpp, norm_g["ple_norm"][i] = ple_bwd(dh_out, s["h3"], _vec(w["ple_norm"], i), w["ple_w_gate"], i, s["gp"], s["pp"], f"ple_bwd_{i}")
        prev_pg = tn_mm(s["hnp"], lambda tk: _bs((tk, 256), lambda k, j: (j, k)), dgp, lambda tk: _bs((tk, D_MODEL), lambda k, j: (j, 0)),
                        N_SHARD, t, 256, D_MODEL, S((N_SHARD, 2, 256, D_MODEL), BF16),
                        _bs((None, None, 256, D_MODEL), lambda k, j, i=i: (k, i, 0, 0)), 1.0, f"ple_gw_gate_{i}", prev_pg)
        prev_pp = tn_mm(p[i], lambda tk: _bs((tk, 256), lambda k, j: (j, 0)), dpp, lambda tk: _bs((tk, 256), lambda k, j: (j, k)),
                        N_SHARD, t, 256, 256, S((N_SHARD, 2, 256, 256), BF16),
                        _bs((None, None, 256, 256), lambda k, j, i=i: (k, i, 0, 0)), 1.0, f"ple_gw_proj_{i}", prev_pp)
        dh_out = dh
        dg, du = ffn_bwd_act(dh_out, w["ffn2_w_down"], i, s["g2"], s["u2"], f"ffn2_bwd_act_{i}")
        dh, norm_g["ffn2_norm"][i] = ffn_bwd_in(dg, du, w["ffn2_w_gate"], w["ffn2_w_up"], i, s["h2"], _vec(w["ffn2_norm"], i), dh_out, f"ffn2_bwd_in_{i}")
        prev_f2 = ffn_wgrads(s["hn2"], dh_out, s["a2"], dg, du, i, prev_f2, f"ffn2_gw_{i}")
        dh_out = dh
        if i == 1:
            d_o = nt_mm(dh_out, w_rec_out, "rec_bwd_out")
            grads["rec_w_out"] = tn_mm(s["ro"], lambda tk: _bs((tk, 256), lambda k, j: (j, k)), dh_out, lambda tk: _bs((tk, D_MODEL), lambda k, j: (j, 0)),
                                       N_SHARD, t, 256, D_MODEL, S((N_SHARD, 256, D_MODEL), BF16), _bs((None, 256, D_MODEL), lambda k, j: (k, 0, 0)),
                                       1.0, "rec_gw_out").reshape(N_SHARD, 1, 256, D_MODEL)
            dz, dga, dgx, small = rec_bwd(d_o, s["z"], s["xb"], s["ga"], s["gx"], s["hs"], w["rec_conv_w"], lru_wa, lru_wx, w["lru_lambda"], "rec_bwd")
            blk = lambda tk: _bs((tk, LRU_BLOCK), lambda k, j: (j, k))
            for nm, dgt in (("lru_wa", dga), ("lru_wx", dgx)):
                gw = tn_mm(s["xb"], blk, dgt, blk, LRU_BLOCKS, t, LRU_BLOCK, LRU_BLOCK, S((LRU_BLOCKS, LRU_BLOCK, LRU_BLOCK), BF16),
                           _bs((None, LRU_BLOCK, LRU_BLOCK), lambda k, j: (k, 0, 0)), 1.0, "rec_gw_" + nm)
                grads[nm] = jnp.transpose(gw.reshape(LRU_BLOCKS, N_SHARD, 64, LRU_BLOCK), (1, 0, 2, 3)).reshape(N_SHARD, 1, LRU_BLOCKS, 64, LRU_BLOCK)
            grads["lru_ba"], grads["lru_bx"], grads["lru_lambda"], grads["rec_conv_b"] = (small[r:r + 1] for r in range(4))
            grads["rec_conv_w"] = small[4:8]
            nb_, bw = N_SHARD, 512
            w_in, nm_in = w_rec_in, "rec_w_in"
        else:
            dy = nt_mm(dh_out, w_hyb_out, "hyb_bwd_out")
            grads["hyb_w_out"] = tn_mm(s["y2"], lambda tk: _bs((None, tk, 256), lambda k, j: (k // 2, j, k % 2)), dh_out,
                                       lambda tk: _bs((tk, D_MODEL), lambda k, j: (j, 0)),
                                       N_SHARD, t, 256, D_MODEL, S((N_SHARD, 256, D_MODEL), BF16), _bs((None, 256, D_MODEL), lambda k, j: (k, 0, 0)),
                                       1.0, "hyb_gw_out").reshape(N_SHARD, 1, 256, D_MODEL)
            delta, dya = attn_delta(dy, s["ya"], "attn_delta")
            dqs, dks, dvs, dbs = [], [], [], []
            for g, dil in enumerate(DILATIONS):
                dq, db = attn_bwd_dq(s["q"], s["k"], s["v"], dya, s["lt"], delta, bias_q[g], dil, f"attn_bwd_dq_{dil}")
                dk, dv = attn_bwd_dkv(s["q"], s["k"], s["v"], dya, s["lt"], delta, bias_k[g], dil, f"attn_bwd_dkv_{dil}")
                dqs.append(dq); dks.append(dk); dvs.append(dv); dbs.append(db)
            grads["rel_bias"] = rel_bias_grad(dbs, "rel_bias_grad")
            dz, grads["hyb_conv_w"], dqg, dkg = hyb_dz(s["z"], dy, w["hyb_conv_w"], qg, kg, dqs, dks, dvs, "hyb_dz")
            grads["hyb_q_gain"] = jnp.sum(dqg.reshape(N_HEADS, HEAD_DIM), axis=0, keepdims=True)
            grads["hyb_k_gain"] = jnp.sum(dkg.reshape(N_HEADS, HEAD_DIM), axis=0, keepdims=True)
            nb_, bw = N_SHARD, 768
            w_in, nm_in = w_hyb_in, "hyb_w_in"
        dh, norm_g["mix_norm"][i] = nt_acc_normbwd(
            [(dz, _bs((tm, bw), lambda r, k: (r, k)), w_in, _bs((None, D_MODEL, bw), lambda r, k: (k, 0, 0)))],
            nb_, s["h1"], _vec(w["mix_norm"], i), dh_out, f"mix_bwd_in_{i}")
        grads[nm_in] = tn_mm(s["hnm"], lambda tk: _bs((tk, D_MODEL), lambda k, j: (j, 0)), dz, lambda tk, bw=bw: _bs((tk, bw), lambda k, j: (j, k)),
                             nb_, t, D_MODEL, bw, S((nb_, D_MODEL, bw), BF16), _bs((None, D_MODEL, bw), lambda k, j: (k, 0, 0)),
                             1.0, f"mix_gw_in_{i}").reshape(nb_, 1, D_MODEL, bw)
        dh_out = dh
        dg, du = ffn_bwd_act(dh_out, w["ffn1_w_down"], i, s["g1"], s["u1"], f"ffn1_bwd_act_{i}")
        dh, norm_g["ffn1_norm"][i] = ffn_bwd_in(dg, du, w["ffn1_w_gate"], w["ffn1_w_up"], i, s["h0"], _vec(w["ffn1_norm"], i), dh_out, f"ffn1_bwd_in_{i}")
        prev_f1 = ffn_wgrads(s["hn1"], dh_out, s["a1"], dg, du, i, prev_f1, f"ffn1_gw_{i}")
    grads["ffn1_w_gate"], grads["ffn1_w_up"], grads["ffn1_w_down"] = prev_f1
    grads["ffn2_w_gate"], grads["ffn2_w_up"], grads["ffn2_w_down"] = prev_f2
    grads["ple_w_gate"], grads["ple_w_proj"] = prev_pg, prev_pp
    for n, (g0, g1) in norm_g.items():
        grads[n] = jnp.concatenate([g0, g1], axis=0)
    return loss, dh, grads


_ANY = pl.BlockSpec(memory_space=pl.ANY)


def _place():
    x, y, c = lax.axis_index("x"), lax.axis_index("y"), lax.axis_index("c")
    chips = [(1 - x, y), (x, 1 - y), (1 - x, 1 - y)]
    return x, y, c, 2 * x + y, chips, [2 * cx + cy for cx, cy in chips]


def gather_weights(shards, name):
    n = len(shards)

    def body(*refs):
        ins, outs = refs[:n], refs[n:2 * n]
        send1, recv1, send2, recv2, lsem = refs[2 * n:]
        x, y, c, k, chips, kk = _place()
        sib = (x, y, 1 - c)

        def remote(src, dst, ssem, rsem, to):
            return pltpu.make_async_remote_copy(src_ref=src, dst_ref=dst, send_sem=ssem, recv_sem=rsem, device_id=to, device_id_type=MESH)

        local = [pltpu.make_async_copy(ins[a], outs[a].at[k], lsem.at[a]) for a in range(n)]
        for cp in local:
            cp.start()
        sends = []
        for a in range(n):
            for j, chip in enumerate(chips):
                cp = remote(ins[a].at[c], outs[a].at[k, c], send1.at[3 * a + j], recv1.at[3 * a + j], (*chip, c))
                cp.start()
                sends.append(cp)
        for a in range(n):
            for j, chip in enumerate(chips):
                remote(ins[a].at[c], outs[a].at[kk[j], c], send1.at[3 * a + j], recv1.at[3 * a + j], (*chip, c)).wait_recv()
                cp = remote(outs[a].at[kk[j], c], outs[a].at[kk[j], c], send2.at[3 * a + j], recv2.at[3 * a + j], sib)
                cp.start()
                sends.append(cp)
        for a in range(n):
            for j in range(3):
                remote(outs[a].at[kk[j], 1 - c], outs[a].at[kk[j], 1 - c], send2.at[3 * a + j], recv2.at[3 * a + j], sib).wait_recv()
        for cp in sends:
            cp.wait_send()
        for cp in local:
            cp.wait()

    return pl.pallas_call(
        body, name=name,
        in_specs=[_ANY] * n, out_specs=[_ANY] * n,
        out_shape=[S((N_SHARD,) + s.shape, s.dtype) for s in shards],
        scratch_shapes=[pltpu.SemaphoreType.DMA((3 * n,))] * 4 + [pltpu.SemaphoreType.DMA((n,))],
    )(*shards)


def scatter_grads(gs, name):
    n = len(gs)

    def body(*refs):
        ins, outs = refs[:n], refs[n:2 * n]
        send1, recv1, send2, recv2, send3, recv3, lsem = refs[2 * n:]
        x, y, c, k, chips, kk = _place()
        sib = (x, y, 1 - c)

        def remote(src, dst, ssem, rsem, to):
            return pltpu.make_async_remote_copy(src_ref=src, dst_ref=dst, send_sem=ssem, recv_sem=rsem, device_id=to, device_id_type=MESH)

        local = [pltpu.make_async_copy(ins[a].at[k], outs[a].at[2 * k + c], lsem.at[a]) for a in range(n)]
        for cp in local:
            cp.start()
        sends = []
        for a in range(n):
            for j, chip in enumerate(chips):
                cp = remote(ins[a].at[kk[j]], outs[a].at[2 * k + c], send1.at[3 * a + j], recv1.at[3 * a + j], (*chip, c))
                cp.start()
                sends.append(cp)
        for a in range(n):
            cp = remote(ins[a].at[k], outs[a].at[2 * k + c], send3.at[a], recv3.at[a], sib)
            cp.start()
            sends.append(cp)
        for a in range(n):
            for j, chip in enumerate(chips):
                slot = outs[a].at[2 * kk[j] + c]
                remote(ins[a].at[k], slot, send1.at[3 * a + j], recv1.at[3 * a + j], (*chip, c)).wait_recv()
                cp = remote(slot, slot, send2.at[3 * a + j], recv2.at[3 * a + j], sib)
                cp.start()
                sends.append(cp)
        for a in range(n):
            slot = outs[a].at[2 * k + 1 - c]
            remote(slot, slot, send3.at[a], recv3.at[a], sib).wait_recv()
            for j in range(3):
                slot = outs[a].at[2 * kk[j] + 1 - c]
                remote(slot, slot, send2.at[3 * a + j], recv2.at[3 * a + j], sib).wait_recv()
        for cp in sends:
            cp.wait_send()
        for cp in local:
            cp.wait()

    return pl.pallas_call(
        body, name=name,
        in_specs=[_ANY] * n, out_specs=[_ANY] * n,
        out_shape=[S((N_DEV,) + g.shape[1:], g.dtype) for g in gs],
        scratch_shapes=[pltpu.SemaphoreType.DMA((3 * n,))] * 4 + [pltpu.SemaphoreType.DMA((n,))] * 3,
    )(*gs)


def allgather8(a, name):
    def body(a_ref, o_ref, send, recv, lsem):
        x, y, c = lax.axis_index("x"), lax.axis_index("y"), lax.axis_index("c")
        me = 4 * x + 2 * y + c
        local = pltpu.make_async_copy(a_ref, o_ref.at[me], lsem)
        local.start()
        cps = []
        for f in range(1, N_DEV):
            fx, fy, fc = (f >> 2) & 1, (f >> 1) & 1, f & 1
            peer = (1 - x if fx else x, 1 - y if fy else y, 1 - c if fc else c)
            cp = pltpu.make_async_remote_copy(src_ref=a_ref, dst_ref=o_ref.at[me], send_sem=send.at[f - 1], recv_sem=recv.at[f - 1],
                                              device_id=peer, device_id_type=MESH)
            cp.start()
            cps.append((cp, 4 * peer[0] + 2 * peer[1] + peer[2], f))
        for cp, pidx, f in cps:
            pltpu.make_async_remote_copy(src_ref=a_ref, dst_ref=o_ref.at[pidx], send_sem=send.at[f - 1], recv_sem=recv.at[f - 1],
                                         device_id=(x, y, c), device_id_type=MESH).wait_recv()
        for cp, _, _ in cps:
            cp.wait_send()
        local.wait()

    return pl.pallas_call(
        body, name=name, in_specs=[_ANY], out_specs=_ANY,
        out_shape=S((N_DEV,) + a.shape, a.dtype),
        scratch_shapes=[pltpu.SemaphoreType.DMA((N_DEV - 1,)), pltpu.SemaphoreType.DMA((N_DEV - 1,)), pltpu.SemaphoreType.DMA],
    )(a)


def sum8(a, name):
    _, r, c = a.shape

    def body(a_ref, o_ref):
        acc = a_ref[0]
        for j in range(1, N_DEV):
            acc = acc + a_ref[j]
        o_ref[...] = acc

    return pl.pallas_call(
        body, name=name, in_specs=[_bs((N_DEV, r, c), lambda: (0, 0, 0))], out_specs=_bs((r, c), lambda: (0, 0)),
        out_shape=S((r, c), F32),
    )(a)


def adamw(w, m, v, g, name):
    r, c = w.shape
    tr = _row_tile(r, 256)
    summed = g.ndim == 3

    def body(w_ref, m_ref, v_ref, g_ref, go_ref, d_ref, mo_ref, vo_ref):
        if summed:
            gr = g_ref[0].astype(F32)
            for j in range(1, N_DEV):
                gr = gr + g_ref[j].astype(F32)
        else:
            gr = g_ref[...]
        m_new = ADAM_B1 * m_ref[...] + (1.0 - ADAM_B1) * gr
        v_new = ADAM_B2 * v_ref[...] + (1.0 - ADAM_B2) * (gr * gr)
        m_hat = m_new / (1.0 - ADAM_B1 ** ADAM_STEP)
        v_hat = v_new / (1.0 - ADAM_B2 ** ADAM_STEP)
        go_ref[...] = gr
        d_ref[...] = -ADAM_LR * (m_hat / (jnp.sqrt(v_hat) + ADAM_EPS) + ADAM_WD * w_ref[...])
        mo_ref[...] = m_new
        vo_ref[...] = v_new

    row = _bs((tr, c), lambda i: (i, 0))
    gspec = _bs((N_DEV, tr, c), lambda i: (0, i, 0)) if summed else row
    return pl.pallas_call(
        body, name=name, grid=(r // tr,),
        in_specs=[row, row, row, gspec], out_specs=[row] * 4, out_shape=[S((r, c), F32)] * 4,
        compiler_params=_cp("parallel"),
    )(w, m, v, g)


WEIGHTS = ["rel_bias", "ffn1_norm", "ffn1_w_gate", "ffn1_w_up", "ffn1_w_down", "mix_norm", "hyb_w_in", "hyb_conv_w", "hyb_q_gain",
           "hyb_k_gain", "hyb_w_out", "rec_w_in", "rec_conv_w", "rec_conv_b", "lru_wa", "lru_ba", "lru_wx", "lru_bx", "lru_lambda",
           "rec_w_out", "ffn2_norm", "ffn2_w_gate", "ffn2_w_up", "ffn2_w_down", "ple_norm", "ple_w_gate", "ple_w_proj"]
BIG = ["ffn1_w_gate", "ffn1_w_up", "ffn1_w_down", "hyb_w_in", "hyb_w_out", "rec_w_in", "lru_wa", "lru_wx", "rec_w_out",
       "ffn2_w_gate", "ffn2_w_up", "ffn2_w_down", "ple_w_gate", "ple_w_proj"]
SMALL_SHARDED = ["hyb_conv_w", "rec_conv_w", "rec_conv_b", "lru_ba", "lru_bx", "lru_lambda"]
SMALL = ["rel_bias", "ffn1_norm", "mix_norm", "ffn2_norm", "ple_norm", "hyb_q_gain", "hyb_k_gain"] + SMALL_SHARDED
PACK_W = 1024


def _halves(a):
    if a.shape[0] == 2:
        return a
    return a.reshape((2, a.shape[1] // 2) + a.shape[2:])


def _pack_rows(arrs, width):
    rows, offs, r0 = [], [], 0
    for a in arrs:
        if a.shape[1] > width:
            a = a.reshape(-1, width)
        rows.append(jnp.pad(a, ((0, 0), (0, width - a.shape[1]))))
        offs.append(r0)
        r0 += a.shape[0]
    pad = (-r0) % 8
    if pad:
        rows.append(jnp.zeros((pad, width), F32))
    return jnp.concatenate(rows, axis=0), offs


def kernel(x, p, rel_bias, ffn1_norm, ffn1_w_gate, ffn1_w_up, ffn1_w_down, mix_norm, hyb_w_in, hyb_conv_w, hyb_q_gain, hyb_k_gain, hyb_w_out, rec_w_in, rec_conv_w, rec_conv_b, lru_wa, lru_ba, lru_wx, lru_bx, lru_lambda, rec_w_out, ffn2_norm, ffn2_w_gate, ffn2_w_up, ffn2_w_down, ple_norm, ple_w_gate, ple_w_proj, loss_target, m_rel_bias, m_ffn1_norm, m_ffn1_w_gate, m_ffn1_w_up, m_ffn1_w_down, m_mix_norm, m_hyb_w_in, m_hyb_conv_w, m_hyb_q_gain, m_hyb_k_gain, m_hyb_w_out, m_rec_w_in, m_rec_conv_w, m_rec_conv_b, m_lru_wa, m_lru_ba, m_lru_wx, m_lru_bx, m_lru_lambda, m_rec_w_out, m_ffn2_norm, m_ffn2_w_gate, m_ffn2_w_up, m_ffn2_w_down, m_ple_norm, m_ple_w_gate, m_ple_w_proj, v_rel_bias, v_ffn1_norm, v_ffn1_w_gate, v_ffn1_w_up, v_ffn1_w_down, v_mix_norm, v_hyb_w_in, v_hyb_conv_w, v_hyb_q_gain, v_hyb_k_gain, v_hyb_w_out, v_rec_w_in, v_rec_conv_w, v_rec_conv_b, v_lru_wa, v_lru_ba, v_lru_wx, v_lru_bx, v_lru_lambda, v_rec_w_out, v_ffn2_norm, v_ffn2_w_gate, v_ffn2_w_up, v_ffn2_w_down, v_ple_norm, v_ple_w_gate, v_ple_w_proj):
    given = dict(locals())
    wts = {n: given[n] for n in WEIGHTS}
    k_chip = 2 * lax.axis_index("x") + lax.axis_index("y")

    gathered = gather_weights([_halves(wts[n].astype(BF16)) for n in BIG], "gather_weights")
    w = {n: g.reshape((N_SHARD,) + wts[n].shape) for n, g in zip(BIG, gathered)}
    sm2d = {n: wts[n].reshape(-1, wts[n].shape[-1]) for n in SMALL_SHARDED}
    slab, offs = _pack_rows([sm2d[n] for n in SMALL_SHARDED], 256)
    slabs = allgather8(slab, "gather_small")[0::2]
    for n, o in zip(SMALL_SHARDED, offs):
        r, cw = sm2d[n].shape
        w[n] = jnp.concatenate([slabs[kc, o:o + r, :cw] for kc in range(N_SHARD)], axis=1)
    for n in SMALL:
        if n not in SMALL_SHARDED:
            w[n] = wts[n]

    loss, dx, grads = local_step(x[0], p[:, 0], loss_target[0], w)
    loss = lax.psum(loss[0, 0], ("x", "y", "c"))

    landed = scatter_grads([grads[n].reshape((N_SHARD,) + wts[n].shape) for n in BIG], "scatter_grads")
    out = {}
    for n, r8 in zip(BIG, landed):
        shp = wts[n].shape
        c2 = shp[-1]
        two = lambda a: a.reshape(-1, c2)
        res = adamw(two(wts[n]), two(given["m_" + n]), two(given["v_" + n]), r8.reshape(N_DEV, -1, c2), "adamw_" + n)
        out[n] = [a.reshape(shp) for a in res]
    g2d = [grads[n].reshape(-1, grads[n].shape[-1]) if n != "rel_bias" else grads[n].reshape(1, -1) for n in SMALL]
    gslab, goffs = _pack_rows(g2d, PACK_W)
    gsum = sum8(allgather8(gslab, "gather_small_grads"), "sum_small_grads")
    for n, o, g in zip(SMALL, goffs, g2d):
        shp = wts[n].shape
        r, cw = g.shape
        gs = gsum[o:o + r, :cw]
        if n in SMALL_SHARDED:
            sw = shp[-1]
            gs = lax.dynamic_slice_in_dim(gs, k_chip * sw, sw, axis=1)
        gs = gs.reshape(shp)
        two = lambda a: a.reshape(-1, shp[-1])
        res = adamw(two(wts[n]), two(given["m_" + n]), two(given["v_" + n]), two(gs), "adamw_" + n)
        out[n] = [a.reshape(shp) for a in res]
    return (loss, dx[None], *[out[n][0] for n in WEIGHTS], *[out[n][1] for n in WEIGHTS],
            *[out[n][2] for n in WEIGHTS], *[out[n][3] for n in WEIGHTS])
```

```python
import functools
import math

import numpy as np
import jax
import jax.numpy as jnp
from jax import lax
from jax.experimental import pallas as pl
from jax.experimental.pallas import tpu as pltpu

F32, BF16 = jnp.float32, jnp.bfloat16
S = jax.ShapeDtypeStruct
MESH = pl.DeviceIdType.MESH

D_MODEL = 1024
N_SHARD = 4
N_DEV = 8
HEAD_DIM = 64
N_HEADS = 8
ATTN_W = N_HEADS * HEAD_DIM
CONV_W = 512
BAND = 128
DILATIONS = (1, 4, 16)
REL_BUCKETS = 32
REL_MAX_DIST = 2048
LRU_BLOCKS = 4
LRU_BLOCK = 256
LRU_C = 8.0
EPS = 1e-6
NEG = -1e30
VMEM_LIMIT = 56 * 1024 * 1024
FFN_ROWS = 1024

ADAM_LR, ADAM_B1, ADAM_B2, ADAM_EPS, ADAM_WD, ADAM_STEP = 0.001, 0.9, 0.999, 1e-08, 0.01, 10


def _cp(*sem):
    return pltpu.CompilerParams(dimension_semantics=sem, vmem_limit_bytes=VMEM_LIMIT)


def _bs(shape, imap):
    return pl.BlockSpec(shape, imap)


def _row_tile(t, want):
    for cand in range(min(want, t) // 8 * 8, 0, -8):
        if t % cand == 0:
            return cand
    return t


_ANY = pl.BlockSpec(memory_space=pl.ANY)


def _place():
    x, y, c = lax.axis_index("x"), lax.axis_index("y"), lax.axis_index("c")
    chips = [(1 - x, y), (x, 1 - y), (1 - x, 1 - y)]
    return x, y, c, 2 * x + y, chips, [2 * cx + cy for cx, cy in chips]


def _remote(src, dst, ssem, rsem, to):
    return pltpu.make_async_remote_copy(src_ref=src, dst_ref=dst, send_sem=ssem, recv_sem=rsem, device_id=to, device_id_type=MESH)


class Host:
    def __init__(self, kind, items):
        self.kind, self.items, self.outs = kind, items, None

    def operands(self):
        if self.kind == "gather":
            return list(self.items), [S((N_SHARD,) + s.shape, s.dtype) for s in self.items], {}
        xin, shapes, alias = [], [], {}
        for a, (g, _, r_prev) in enumerate(self.items):
            xin.append(g)
            if r_prev is not None:
                alias[len(xin)] = a
                xin.append(r_prev)
            shapes.append(S((N_DEV,) + g.shape[1:], g.dtype))
        return xin, shapes, alias

    def copies(self, xi, xo, send, recv, lsem):
        x, y, c, k, chips, kk = _place()
        starts, waits = [], []
        pos = 0
        for a, item in enumerate(self.items):
            if self.kind == "gather":
                src_of = lambda chip_idx, s=xi[a]: s
                dst_of = lambda chip_idx, o=xo[a]: o.at[chip_idx]
                mine, theirs = k, kk
            else:
                g_ref = xi[pos]
                pos += 1 if item[2] is None else 2
                lay = item[1]
                src_of = lambda chip_idx, g=g_ref, lay=lay: g.at[chip_idx, lay]
                dst_of = lambda slot, o=xo[a], lay=lay: o.at[slot, lay]
                mine, theirs = 2 * k + c, [2 * kj + c for kj in kk]
            own_src = src_of(k)
            local = pltpu.make_async_copy(own_src, dst_of(mine), lsem.at[a])
            starts.append((local, "start"))
            waits.append((local, "wait"))
            for j, chip in enumerate(chips):
                src = own_src if self.kind == "gather" else src_of(kk[j])
                cp = _remote(src, dst_of(mine), send.at[3 * a + j], recv.at[3 * a + j], (*chip, c))
                starts.append((cp, "start"))
                waits.append((cp, "wait_send"))
                waits.append((_remote(own_src, dst_of(theirs[j]), send.at[3 * a + j], recv.at[3 * a + j], (*chip, c)), "wait_recv"))
        return starts, waits


def _call(host, body, *, name, grid, in_specs, out_specs, out_shape, scratch_shapes=(), compiler_params=None, args, aliases=None):
    aliases = dict(aliases or {})
    if host is None:
        return pl.pallas_call(body, name=name, grid=grid, in_specs=in_specs, out_specs=out_specs, out_shape=out_shape,
                              scratch_shapes=list(scratch_shapes), input_output_aliases=aliases, compiler_params=compiler_params)(*args)
    single = not isinstance(out_shape, (list, tuple))
    out_specs_l = [out_specs] if single else list(out_specs)
    out_shape_l = [out_shape] if single else list(out_shape)
    n_in, n_out, n_scr = len(in_specs), len(out_shape_l), len(scratch_shapes)
    xin, xshapes, xalias = host.operands()
    n_items = len(host.items)
    for i_in, i_out in xalias.items():
        aliases[n_in + i_in] = n_out + i_out
    nd = len(grid)

    def hosted(*refs):
        ins, xi = refs[:n_in], refs[n_in:n_in + len(xin)]
        o0 = n_in + len(xin)
        outs, xo = refs[o0:o0 + n_out], refs[o0 + n_out:o0 + n_out + n_items]
        s0 = o0 + n_out + n_items
        scr = refs[s0:s0 + n_scr]
        send, recv, lsem = refs[s0 + n_scr:]
        first = functools.reduce(jnp.logical_and, [pl.program_id(d) == 0 for d in range(nd)])
        last = functools.reduce(jnp.logical_and, [pl.program_id(d) == grid[d] - 1 for d in range(nd)])
        starts, waits = host.copies(xi, xo, send, recv, lsem)

        @pl.when(first)
        def _():
            for cp, how in starts:
                getattr(cp, how)()
        body(*ins, *outs, *scr)

        @pl.when(last)
        def _():
            for cp, how in waits:
                getattr(cp, how)()

    res = pl.pallas_call(
        hosted, name=name, grid=grid,
        in_specs=list(in_specs) + [_ANY] * len(xin),
        out_specs=out_specs_l + [_ANY] * n_items,
        out_shape=out_shape_l + xshapes,
        scratch_shapes=list(scratch_shapes) + [pltpu.SemaphoreType.DMA((3 * n_items,)), pltpu.SemaphoreType.DMA((3 * n_items,)),
                                               pltpu.SemaphoreType.DMA((n_items,))],
        input_output_aliases=aliases,
        compiler_params=pltpu.CompilerParams(dimension_semantics=("arbitrary",) * nd, vmem_limit_bytes=VMEM_LIMIT),
    )(*args, *xin)
    host.outs = list(res[n_out:])
    return res[0] if single else list(res[:n_out])


def _rstd(x):
    return lax.rsqrt(jnp.mean(x * x, axis=-1, keepdims=True) + EPS)


def _sigmoid(x):
    return 1.0 / (1.0 + jnp.exp(-x))


def _dot(a, b):
    return jnp.dot(a, b, preferred_element_type=F32)


def _dot_nt(a, b):
    return lax.dot_general(a, b, (((1,), (1,)), ((), ())), preferred_element_type=F32)


def _dot_tn(a, b):
    return lax.dot_general(a, b, (((0,), (0,)), ((), ())), preferred_element_type=F32)


def _seg_dot(x, seg_bf16):
    hi = x.astype(BF16)
    lo = (x - hi.astype(F32)).astype(BF16)
    return _dot(hi, seg_bf16) + _dot(lo, seg_bf16)


def _shift_down(x, prev8, s):
    if s == 0:
        return x
    tm = x.shape[0]
    row = lax.broadcasted_iota(jnp.int32, x.shape, 0)
    main = jnp.where(row >= s, pltpu.roll(x, s, axis=0), 0.0)
    row8 = lax.broadcasted_iota(jnp.int32, prev8.shape, 0)
    head = jnp.where(row8 < s, pltpu.roll(prev8, s, axis=0), 0.0)
    if tm == 8:
        return main + head
    return main + jnp.concatenate([head, jnp.zeros((tm - 8, x.shape[1]), x.dtype)], axis=0)


def _shift_up(x, next8, s):
    if s == 0:
        return x
    tm = x.shape[0]
    row = lax.broadcasted_iota(jnp.int32, x.shape, 0)
    main = jnp.where(row < tm - s, pltpu.roll(x, tm - s, axis=0), 0.0)
    row8 = lax.broadcasted_iota(jnp.int32, next8.shape, 0)
    tail = jnp.where(row8 >= 8 - s, pltpu.roll(next8, 8 - s, axis=0), 0.0)
    if tm == 8:
        return main + tail
    return main + jnp.concatenate([jnp.zeros((tm - 8, x.shape[1]), x.dtype), tail], axis=0)


def _roll_fill(x, s, fill, up):
    tm = x.shape[0]
    row = lax.broadcasted_iota(jnp.int32, x.shape, 0)
    if up:
        return jnp.where(row < tm - s, pltpu.roll(x, tm - s, axis=0), fill)
    return jnp.where(row >= s, pltpu.roll(x, s, axis=0), fill)


def _log1p(y):
    u = 1.0 + y
    return jnp.where(u == 1.0, y, jnp.log(u) * (y / jnp.where(u == 1.0, 1.0, u - 1.0)))


def _softplus(x):
    return jnp.maximum(x, 0.0) + _log1p(jnp.exp(-jnp.abs(x)))


def _neg_expm1(y):
    series = -y * (1.0 + y * (0.5 + y * (1.0 / 6.0 + y * (1.0 / 24.0 + y * (1.0 / 120.0)))))
    return jnp.where(jnp.abs(y) < 0.03, series, 1.0 - jnp.exp(y))


_GELU_C = math.sqrt(2.0 / math.pi)


def _gelu_and_grad(x):
    inner = _GELU_C * (x + 0.044715 * x * x * x)
    t = jnp.tanh(inner)
    g = 0.5 * x * (1.0 + t)
    dg = 0.5 * (1.0 + t) + 0.5 * x * (1.0 - t * t) * _GELU_C * (1.0 + 3.0 * 0.044715 * x * x)
    return g, dg


def _rmsnorm_bwd(x, gain, dy):
    r = _rstd(x)
    xhat = x * r
    dxhat = dy * gain
    dx = r * (dxhat - xhat * jnp.mean(dxhat * xhat, axis=-1, keepdims=True))
    return dx, jnp.sum(dy * xhat, axis=0, keepdims=True)


def ffn_up(h, gain, wg, wu, layer, name, host=None):
    t, d = h.shape
    nk, _, _, f = wg.shape
    tm = _row_tile(t, FFN_ROWS)

    def body(h_ref, g_ref, wg_ref, wu_ref, hn_ref, gg_ref, uu_ref, aa_ref, hn_scr):
        @pl.when(pl.program_id(1) == 0)
        def _():
            x = h_ref[...]
            hn = (x * _rstd(x) * g_ref[...]).astype(BF16)
            hn_scr[...] = hn
            hn_ref[...] = hn
        hn = hn_scr[...]
        g = _dot(hn, wg_ref[...])
        u = _dot(hn, wu_ref[...])
        gg_ref[...] = g.astype(BF16)
        uu_ref[...] = u.astype(BF16)
        aa_ref[...] = (g * _sigmoid(g) * u).astype(BF16)

    wspec = _bs((None, None, d, f), lambda i, k: (k, layer, 0, 0))
    aspec = _bs((None, tm, f), lambda i, k: (k, i, 0))
    return _call(
        host, body, name=name, grid=(t // tm, nk),
        in_specs=[_bs((tm, d), lambda i, k: (i, 0)), _bs((1, d), lambda i, k: (0, 0)), wspec, wspec],
        out_specs=[_bs((tm, d), lambda i, k: (i, 0)), aspec, aspec, aspec],
        out_shape=[S((t, d), BF16), S((nk, t, f), BF16), S((nk, t, f), BF16), S((nk, t, f), BF16)],
        scratch_shapes=[pltpu.VMEM((tm, d), BF16)],
        compiler_params=_cp("parallel", "arbitrary"),
        args=(h, gain, wg, wu))


def mm_acc(a, a_spec, b, b_spec, res, scale, nk, t, n, name, tm, host=None):
    def body(a_ref, b_ref, r_ref, o_ref, acc):
        k = pl.program_id(1)

        @pl.when(k == 0)
        def _():
            acc[...] = jnp.zeros_like(acc)
        acc[...] += _dot(a_ref[...].astype(BF16), b_ref[...])

        @pl.when(k == nk - 1)
        def _():
            o_ref[...] = r_ref[...] + scale * acc[...]

    return _call(
        host, body, name=name, grid=(t // tm, nk),
        in_specs=[a_spec, b_spec, _bs((tm, n), lambda i, k: (i, 0))],
        out_specs=_bs((tm, n), lambda i, k: (i, 0)),
        out_shape=S((t, n), F32),
        scratch_shapes=[pltpu.VMEM((tm, n), F32)],
        compiler_params=_cp("parallel", "arbitrary"),
        args=(a, b, res))


def ffn_down(a, wd, layer, h, name, host=None):
    nk, t, f = a.shape
    d = h.shape[1]
    tm = _row_tile(t, FFN_ROWS)
    return mm_acc(a, _bs((None, tm, f), lambda i, k: (k, i, 0)),
                  wd, _bs((None, None, f, d), lambda i, k: (k, layer, 0, 0)),
                  h, 0.5, nk, t, d, name, tm, host)


def ffn_bwd_act(dh, wd, layer, gg, uu, name, host=None):
    nk, t, f = gg.shape
    d = dh.shape[1]
    tm = _row_tile(t, FFN_ROWS)

    def body(dh_ref, wd_ref, g_ref, u_ref, dg_ref, du_ref):
        da = 0.5 * _dot_nt(dh_ref[...].astype(BF16), wd_ref[...])
        g = g_ref[...].astype(F32)
        u = u_ref[...].astype(F32)
        s = _sigmoid(g)
        dg_ref[...] = (da * u * (s * (1.0 + g * (1.0 - s)))).astype(BF16)
        du_ref[...] = (da * (g * s)).astype(BF16)

    aspec = _bs((None, tm, f), lambda i, k: (k, i, 0))
    return _call(
        host, body, name=name, grid=(t // tm, nk),
        in_specs=[_bs((tm, d), lambda i, k: (i, 0)), _bs((None, None, f, d), lambda i, k: (k, layer, 0, 0)), aspec, aspec],
        out_specs=[aspec, aspec],
        out_shape=[S((nk, t, f), BF16), S((nk, t, f), BF16)],
        compiler_params=_cp("parallel", "arbitrary"),
        args=(dh, wd, gg, uu))


def nt_acc_normbwd(terms, nk, h, gain, dh, name, host=None):
    t, d = h.shape
    tm = _row_tile(t, 512)
    nterm = len(terms)

    def body(*refs):
        xs = refs[:2 * nterm]
        h_ref, g_ref, dh_ref, o_ref, dg_ref, acc = refs[2 * nterm:]
        i, k = pl.program_id(0), pl.program_id(1)

        @pl.when(k == 0)
        def _():
            acc[...] = jnp.zeros_like(acc)

        @pl.when(jnp.logical_and(i == 0, k == 0))
        def _():
            dg_ref[...] = jnp.zeros_like(dg_ref)
        for j in range(nterm):
            acc[...] += _dot_nt(xs[2 * j][...], xs[2 * j + 1][...])

        @pl.when(k == nk - 1)
        def _():
            dx, dgain = _rmsnorm_bwd(h_ref[...], g_ref[...], acc[...])
            o_ref[...] = dh_ref[...] + dx
            dg_ref[...] += dgain

    in_specs, args = [], []
    for x, xs_, w, ws_ in terms:
        in_specs += [xs_, ws_]
        args += [x, w]
    row = _bs((tm, d), lambda i, k: (i, 0))
    vec = _bs((1, d), lambda i, k: (0, 0))
    return _call(
        host, body, name=name, grid=(t // tm, nk),
        in_specs=in_specs + [row, vec, row],
        out_specs=[row, vec],
        out_shape=[S((t, d), F32), S((1, d), F32)],
        scratch_shapes=[pltpu.VMEM((tm, d), F32)],
        compiler_params=_cp("arbitrary", "arbitrary"),
        args=(*args, h, gain, dh))


def ffn_bwd_in(dg, du, wg, wu, layer, h, gain, dh, name, host=None):
    nk, t, f = dg.shape
    d = h.shape[1]
    tm = _row_tile(t, 512)
    aspec = _bs((None, tm, f), lambda i, k: (k, i, 0))
    wspec = _bs((None, None, d, f), lambda i, k: (k, layer, 0, 0))
    return nt_acc_normbwd([(dg, aspec, wg, wspec), (du, aspec, wu, wspec)], nk, h, gain, dh, name, host)


def tn_mm(x, x_spec, y, y_spec, nblk, t, ka, nb, out_shape, out_spec, scale, name, prev=None, tk=512):
    tk = _row_tile(t, tk)

    def body(*refs):
        if prev is None:
            x_ref, y_ref, o_ref, acc = refs
        else:
            x_ref, y_ref, _, o_ref, acc = refs
        j = pl.program_id(1)

        @pl.when(j == 0)
        def _():
            acc[...] = jnp.zeros_like(acc)
        acc[...] += _dot_tn(x_ref[...].astype(BF16), y_ref[...].astype(BF16))

        @pl.when(j == t // tk - 1)
        def _():
            o_ref[...] = (scale * acc[...]).astype(o_ref.dtype)

    in_specs = [x_spec(tk), y_spec(tk)]
    args = [x, y]
    aliases = {}
    if prev is not None:
        in_specs.append(pl.BlockSpec(memory_space=pl.ANY))
        args.append(prev)
        aliases = {2: 0}
    return pl.pallas_call(
        body, name=name, grid=(nblk, t // tk),
        in_specs=in_specs, out_specs=out_spec, out_shape=out_shape,
        scratch_shapes=[pltpu.VMEM((ka, nb), F32)],
        input_output_aliases=aliases,
        compiler_params=_cp("parallel", "arbitrary"),
    )(*args)


def ffn_wgrads(hn, dh, aa, dg, du, layer, prev, name):
    nk, t, f = aa.shape
    d = hn.shape[1]
    pg, pu, pd = prev if prev is not None else (None, None, None)
    hn_spec = lambda tk: _bs((tk, d), lambda k, j: (j, 0))
    a_spec = lambda tk: _bs((None, tk, f), lambda k, j: (k, j, 0))
    shape_gu, spec_gu = S((nk, 2, d, f), BF16), _bs((None, None, d, f), lambda k, j: (k, layer, 0, 0))
    shape_d, spec_d = S((nk, 2, f, d), BF16), _bs((None, None, f, d), lambda k, j: (k, layer, 0, 0))
    gwg = tn_mm(hn, hn_spec, dg, a_spec, nk, t, d, f, shape_gu, spec_gu, 1.0, name + "_g", pg)
    gwu = tn_mm(hn, hn_spec, du, a_spec, nk, t, d, f, shape_gu, spec_gu, 1.0, name + "_u", pu)
    gwd = tn_mm(aa, a_spec, dh, hn_spec, nk, t, f, d, shape_d, spec_d, 0.5, name + "_d", pd)
    return gwg, gwu, gwd


def norm_mm(h, gain, w, name, host=None):
    t, d = h.shape
    nb, _, bw = w.shape
    tm = _row_tile(t, 512)

    def body(h_ref, g_ref, w_ref, hn_ref, z_ref, hn_scr):
        @pl.when(pl.program_id(1) == 0)
        def _():
            x = h_ref[...]
            hn = (x * _rstd(x) * g_ref[...]).astype(BF16)
            hn_scr[...] = hn
            hn_ref[...] = hn
        z_ref[...] = _dot(hn_scr[...], w_ref[...])

    return _call(
        host, body, name=name, grid=(t // tm, nb),
        in_specs=[_bs((tm, d), lambda i, k: (i, 0)), _bs((1, d), lambda i, k: (0, 0)), _bs((None, d, bw), lambda i, k: (k, 0, 0))],
        out_specs=[_bs((tm, d), lambda i, k: (i, 0)), _bs((tm, bw), lambda i, k: (i, k))],
        out_shape=[S((t, d), BF16), S((t, nb * bw), F32)],
        scratch_shapes=[pltpu.VMEM((tm, d), BF16)],
        compiler_params=_cp("parallel", "arbitrary"),
        args=(h, gain, w))


def nt_mm(a, w, name):
    t, k = a.shape
    n = w.shape[0]
    tm = _row_tile(t, 512)

    def body(a_ref, w_ref, o_ref):
        o_ref[...] = _dot_nt(a_ref[...].astype(BF16), w_ref[...])

    return pl.pallas_call(
        body, name=name, grid=(t // tm,),
        in_specs=[_bs((tm, k), lambda i: (i, 0)), _bs((n, k), lambda i: (0, 0))],
        out_specs=_bs((tm, n), lambda i: (i, 0)),
        out_shape=S((t, n), F32),
        compiler_params=_cp("parallel"),
    )(a, w)


def _head_mean_matrix():
    m = np.kron(np.eye(N_HEADS, dtype=np.float32), np.full((HEAD_DIM, HEAD_DIM), 1.0 / HEAD_DIM, np.float32))
    return jnp.asarray(m, BF16)


def _head_sum_matrix():
    m = np.kron(np.eye(N_HEADS, dtype=np.float32), np.ones((HEAD_DIM, HEAD_DIM), np.float32))
    return jnp.asarray(m, BF16)


def _rel_bucket_np(dist):
    max_exact = REL_BUCKETS // 2
    n = np.maximum(dist, 1).astype(np.float32)
    large = max_exact + (np.log(n / np.float32(max_exact)) / np.float32(math.log(REL_MAX_DIST / max_exact))
                         * np.float32(REL_BUCKETS - max_exact)).astype(np.int32)
    large = np.minimum(large, REL_BUCKETS - 1)
    return np.where(dist < max_exact, dist, large)


def _band_tables():
    qi = np.arange(BAND)[:, None]
    kj = np.arange(2 * BAND)[None, :]
    dist_q = qi + BAND - kj
    qq = np.arange(2 * BAND)[:, None]
    kk = np.arange(BAND)[None, :]
    dist_k = qq - kk
    out = []
    for dist in (dist_q, dist_k):
        valid = (dist >= 0) & (dist <= BAND)
        bucket = np.stack([_rel_bucket_np(np.clip(dist, 0, BAND) * d) for d in DILATIONS])
        out.append((bucket, valid))
    return out


def band_bias(rel_bias):
    out = []
    for bucket, valid in _band_tables():
        bucket = np.where(valid[None], bucket, -1)[:, None]
        tab = jnp.full((len(DILATIONS), N_HEADS) + bucket.shape[2:], NEG, F32)
        for b in range(REL_BUCKETS):
            if (bucket == b).any():
                tab = jnp.where(bucket == b, rel_bias[b][None, :, None, None], tab)
        out.append(tab)
    return out


def hyb_prep(z, q_gain, k_gain, name):
    t = z.shape[0]
    tm = _row_tile(t, 512)
    seg = _head_mean_matrix()

    def body(q_ref, k_ref, v_ref, qg_ref, kg_ref, seg_ref, qo_ref, ko_ref, vo_ref):
        q = q_ref[...]
        k = k_ref[...]
        qo_ref[...] = (q * lax.rsqrt(_seg_dot(q * q, seg_ref[...]) + EPS) * qg_ref[...]).astype(BF16)
        ko_ref[...] = (k * lax.rsqrt(_seg_dot(k * k, seg_ref[...]) + EPS) * kg_ref[...]).astype(BF16)
        vo_ref[...] = v_ref[...].astype(BF16)

    col = lambda c: _bs((tm, ATTN_W), lambda i: (i, c))
    vec = _bs((1, ATTN_W), lambda i: (0, 0))
    out = _bs((tm, ATTN_W), lambda i: (i, 0))
    return pl.pallas_call(
        body, name=name, grid=(t // tm,),
        in_specs=[col(3), col(4), col(5), vec, vec, _bs((ATTN_W, ATTN_W), lambda i: (0, 0))],
        out_specs=[out, out, out],
        out_shape=[S((t, ATTN_W), BF16)] * 3,
        compiler_params=_cp("parallel"),
    )(z, z, z, q_gain, k_gain, seg)


def _lane_lo(shape):
    return lax.broadcasted_iota(jnp.int32, shape, 1) < HEAD_DIM


def attn_fwd(q, k, v, bias, dil, name):
    t = q.shape[0]
    sub = t // dil
    nb = sub // BAND
    qv, kv, vv = (a.reshape(sub, dil * ATTN_W) for a in (q, k, v))

    def body(q_ref, kp_ref, kc_ref, vp_ref, vc_ref, b_ref, o_ref, l_ref):
        n = pl.program_id(1)
        first = n == 0
        colk = lax.broadcasted_iota(jnp.int32, (BAND, 2 * BAND), 1)
        for j in range(N_HEADS // 2):
            sl = slice(2 * HEAD_DIM * j, 2 * HEAD_DIM * (j + 1))
            qp = q_ref[:, sl]
            kk = jnp.concatenate([kp_ref[:, sl], kc_ref[:, sl]], axis=0)
            vv_ = jnp.concatenate([vp_ref[:, sl], vc_ref[:, sl]], axis=0)
            lo = _lane_lo(qp.shape)
            outs, lses = [], []
            for hh in range(2):
                qm = jnp.where(lo if hh == 0 else jnp.logical_not(lo), qp, jnp.zeros_like(qp))
                s = _dot_nt(qm, kk) * (HEAD_DIM ** -0.5) + b_ref[2 * j + hh]
                s = jnp.where(jnp.logical_and(first, colk < BAND), NEG, s)
                m = jnp.max(s, axis=-1, keepdims=True)
                p = jnp.exp(s - m)
                l = jnp.sum(p, axis=-1, keepdims=True)
                outs.append(_dot(p.astype(BF16), vv_) / l)
                lses.append(m + jnp.log(l))
            o_ref[:, sl] = jnp.where(lo, outs[0], outs[1])
            l_ref[:, sl] = jnp.where(lo, lses[0], lses[1])

    cur = _bs((BAND, ATTN_W), lambda r, n: (n, r))
    prv = _bs((BAND, ATTN_W), lambda r, n: (jnp.maximum(n - 1, 0), r))
    o, lse = pl.pallas_call(
        body, name=name, grid=(dil, nb),
        in_specs=[cur, prv, cur, prv, cur, _bs((N_HEADS, BAND, 2 * BAND), lambda r, n: (0, 0, 0))],
        out_specs=[cur, cur],
        out_shape=[S((sub, dil * ATTN_W), F32)] * 2,
        compiler_params=_cp("parallel", "arbitrary"),
    )(qv, kv, kv, vv, vv, bias)
    return o.reshape(t, ATTN_W), lse.reshape(t, ATTN_W)


def hyb_post(z, conv_w, os_, lses, name):
    t = z.shape[0]
    tm = _row_tile(t, 512)

    def body(gb_ref, gc_ref, cx_ref, gch_ref, cxh_ref, w_ref, o1, o2, o3, l1, l2, l3, y_ref, lt_ref, ya_ref):
        i = pl.program_id(0)
        m = gc_ref[...] * cx_ref[...]
        mh = jnp.where(i == 0, 0.0, gch_ref[...] * cxh_ref[...])
        conv = w_ref[0:1, :] * _shift_down(m, mh, 2) + w_ref[1:2, :] * _shift_down(m, mh, 1) + w_ref[2:3, :] * m
        y_ref[0] = (gb_ref[...] * conv).astype(BF16)
        la, lb, lc = l1[...], l2[...], l3[...]
        mx = jnp.maximum(jnp.maximum(la, lb), lc)
        ea, eb, ec = jnp.exp(la - mx), jnp.exp(lb - mx), jnp.exp(lc - mx)
        den = ea + eb + ec
        ya = (ea * o1[...] + eb * o2[...] + ec * o3[...]) / den
        y_ref[1] = ya.astype(BF16)
        ya_ref[...] = ya
        lt_ref[...] = mx + jnp.log(den)

    hb = tm // 8
    col = lambda c: _bs((tm, CONV_W), lambda i: (i, c))
    halo = lambda c: _bs((8, CONV_W), lambda i: (jnp.maximum(i * hb - 1, 0), c))
    row = _bs((tm, ATTN_W), lambda i: (i, 0))
    return pl.pallas_call(
        body, name=name, grid=(t // tm,),
        in_specs=[col(0), col(1), col(2), halo(1), halo(2), _bs((3, CONV_W), lambda i: (0, 0))] + [row] * 6,
        out_specs=[_bs((2, tm, ATTN_W), lambda i: (0, i, 0)), row, row],
        out_shape=[S((2, t, ATTN_W), BF16), S((t, ATTN_W), F32), S((t, ATTN_W), F32)],
        compiler_params=_cp("parallel"),
    )(z, z, z, z, z, conv_w, *os_, *lses)


def attn_delta(dy, ya, name):
    t = ya.shape[0]
    tm = _row_tile(t, 512)
    seg = _head_sum_matrix()

    def body(dy_ref, ya_ref, seg_ref, dl_ref, db_ref):
        dya = dy_ref[...]
        dl_ref[...] = _seg_dot(dya * ya_ref[...], seg_ref[...])
        db_ref[...] = dya.astype(BF16)

    row = _bs((tm, ATTN_W), lambda i: (i, 0))
    return pl.pallas_call(
        body, name=name, grid=(t // tm,),
        in_specs=[_bs((tm, ATTN_W), lambda i: (i, 1)), row, _bs((ATTN_W, ATTN_W), lambda i: (0, 0))],
        out_specs=[row, row],
        out_shape=[S((t, ATTN_W), F32), S((t, ATTN_W), BF16)],
        compiler_params=_cp("parallel"),
    )(dy, ya, seg)


def attn_bwd_dq(q, k, v, dya, lt, delta, bias, dil, name):
    t = q.shape[0]
    sub = t // dil
    nb = sub // BAND
    qv, kv, vv, dv_, lv, ev = (a.reshape(sub, dil * ATTN_W) for a in (q, k, v, dya, lt, delta))

    def body(q_ref, kp_ref, kc_ref, vp_ref, vc_ref, do_ref, l_ref, e_ref, b_ref, dq_ref, db_ref):
        r, n = pl.program_id(0), pl.program_id(1)

        @pl.when(jnp.logical_and(r == 0, n == 0))
        def _():
            db_ref[...] = jnp.zeros_like(db_ref)
        first = n == 0
        colk = lax.broadcasted_iota(jnp.int32, (BAND, 2 * BAND), 1)
        for j in range(N_HEADS // 2):
            sl = slice(2 * HEAD_DIM * j, 2 * HEAD_DIM * (j + 1))
            qp, dop = q_ref[:, sl], do_ref[:, sl]
            kk = jnp.concatenate([kp_ref[:, sl], kc_ref[:, sl]], axis=0)
            vv_ = jnp.concatenate([vp_ref[:, sl], vc_ref[:, sl]], axis=0)
            lo = _lane_lo(qp.shape)
            dqs = []
            for hh in range(2):
                msk = lo if hh == 0 else jnp.logical_not(lo)
                c0 = 2 * HEAD_DIM * j + HEAD_DIM * hh
                qm = jnp.where(msk, qp, jnp.zeros_like(qp))
                dom = jnp.where(msk, dop, jnp.zeros_like(dop))
                s = _dot_nt(qm, kk) * (HEAD_DIM ** -0.5) + b_ref[2 * j + hh]
                s = jnp.where(jnp.logical_and(first, colk < BAND), NEG, s)
                p = jnp.exp(s - l_ref[:, c0:c0 + 1])
                ds = p * (_dot_nt(dom, vv_) - e_ref[:, c0:c0 + 1])
                db_ref[2 * j + hh] += ds
                dqs.append(_dot(ds.astype(BF16), kk) * (HEAD_DIM ** -0.5))
            dq_ref[:, sl] = jnp.where(lo, dqs[0], dqs[1])

    cur = _bs((BAND, ATTN_W), lambda r, n: (n, r))
    prv = _bs((BAND, ATTN_W), lambda r, n: (jnp.maximum(n - 1, 0), r))
    tab = _bs((N_HEADS, BAND, 2 * BAND), lambda r, n: (0, 0, 0))
    dq, db = pl.pallas_call(
        body, name=name, grid=(dil, nb),
        in_specs=[cur, prv, cur, prv, cur, cur, cur, cur, tab],
        out_specs=[cur, tab],
        out_shape=[S((sub, dil * ATTN_W), F32), S((N_HEADS, BAND, 2 * BAND), F32)],
        compiler_params=_cp("arbitrary", "arbitrary"),
    )(qv, kv, kv, vv, vv, dv_, lv, ev, bias)
    return dq.reshape(t, ATTN_W), db


def attn_bwd_dkv(q, k, v, dya, lt, delta, bias_k, dil, name):
    t = q.shape[0]
    sub = t // dil
    nb = sub // BAND
    qv, kv, vv, dv_, lv, ev = (a.reshape(sub, dil * ATTN_W) for a in (q, k, v, dya, lt, delta))

    def body(k_ref, v_ref, qc_ref, qn_ref, dc_ref, dn_ref, lc_ref, ln_ref, ec_ref, en_ref, b_ref, dk_ref, dv_ref):
        n = pl.program_id(1)
        last = n == nb - 1
        rowq = lax.broadcasted_iota(jnp.int32, (2 * BAND, BAND), 0)
        for j in range(N_HEADS // 2):
            sl = slice(2 * HEAD_DIM * j, 2 * HEAD_DIM * (j + 1))
            kp, vp = k_ref[:, sl], v_ref[:, sl]
            qq = jnp.concatenate([qc_ref[:, sl], qn_ref[:, sl]], axis=0)
            do = jnp.concatenate([dc_ref[:, sl], dn_ref[:, sl]], axis=0)
            lo = _lane_lo(qq.shape)
            dks, dvs = [], []
            for hh in range(2):
                msk = lo if hh == 0 else jnp.logical_not(lo)
                c0 = 2 * HEAD_DIM * j + HEAD_DIM * hh
                qm = jnp.where(msk, qq, jnp.zeros_like(qq))
                dom = jnp.where(msk, do, jnp.zeros_like(do))
                ll = jnp.concatenate([lc_ref[:, c0:c0 + 1], ln_ref[:, c0:c0 + 1]], axis=0)
                ee = jnp.concatenate([ec_ref[:, c0:c0 + 1], en_ref[:, c0:c0 + 1]], axis=0)
                s = _dot_nt(qm, kp) * (HEAD_DIM ** -0.5) + b_ref[2 * j + hh]
                s = jnp.where(jnp.logical_and(last, rowq >= BAND), NEG, s)
                p = jnp.exp(s - ll)
                ds = p * (_dot_nt(dom, vp) - ee)
                dvs.append(_dot_tn(p.astype(BF16), dom))
                dks.append(_dot_tn(ds.astype(BF16), qm) * (HEAD_DIM ** -0.5))
            lo_k = _lane_lo(kp.shape)
            dk_ref[:, sl] = jnp.where(lo_k, dks[0], dks[1])
            dv_ref[:, sl] = jnp.where(lo_k, dvs[0], dvs[1])

    cur = _bs((BAND, ATTN_W), lambda r, n: (n, r))
    nxt = _bs((BAND, ATTN_W), lambda r, n: (jnp.minimum(n + 1, nb - 1), r))
    tab = _bs((N_HEADS, 2 * BAND, BAND), lambda r, n: (0, 0, 0))
    dk, dv = pl.pallas_call(
        body, name=name, grid=(dil, nb),
        in_specs=[cur, cur, cur, nxt, cur, nxt, cur, nxt, cur, nxt, tab],
        out_specs=[cur, cur],
        out_shape=[S((sub, dil * ATTN_W), F32)] * 2,
        compiler_params=_cp("parallel", "arbitrary"),
    )(kv, vv, qv, qv, dv_, dv_, lv, lv, ev, ev, bias_k)
    return dk.reshape(t, ATTN_W), dv.reshape(t, ATTN_W)


def hyb_dz(z, dy, conv_w, q_gain, k_gain, dqs, dks, dvs, name):
    t = z.shape[0]
    tm = _row_tile(t, 256)
    nt = t // tm
    seg = _head_mean_matrix()

    def body(gb_ref, gc_ref, cx_ref, q_ref, k_ref, gch_ref, cxh_ref, gbn_ref, dyc_ref, dyn_ref, w_ref, qg_ref, kg_ref, seg_ref,
             dq1, dq2, dq3, dk1, dk2, dk3, dv1, dv2, dv3, dz_ref, dw_ref, dqg_ref, dkg_ref):
        i = pl.program_id(0)

        @pl.when(i == 0)
        def _():
            dw_ref[...] = jnp.zeros_like(dw_ref)
            dqg_ref[...] = jnp.zeros_like(dqg_ref)
            dkg_ref[...] = jnp.zeros_like(dkg_ref)
        gb, gc, cx, dyc = gb_ref[...], gc_ref[...], cx_ref[...], dyc_ref[...]
        m = gc * cx
        mh = jnp.where(i == 0, 0.0, gch_ref[...] * cxh_ref[...])
        m1, m2 = _shift_down(m, mh, 1), _shift_down(m, mh, 2)
        conv = w_ref[0:1, :] * m2 + w_ref[1:2, :] * m1 + w_ref[2:3, :] * m
        dconv = dyc * gb
        dcn = jnp.where(i == nt - 1, 0.0, dyn_ref[...] * gbn_ref[...])
        dm = w_ref[2:3, :] * dconv + w_ref[1:2, :] * _shift_up(dconv, dcn, 1) + w_ref[0:1, :] * _shift_up(dconv, dcn, 2)
        dz_ref[:, 0:CONV_W] = (dyc * conv).astype(BF16)
        dz_ref[:, CONV_W:2 * CONV_W] = (dm * cx).astype(BF16)
        dz_ref[:, 2 * CONV_W:3 * CONV_W] = (dm * gc).astype(BF16)
        dw_ref[0:1, :] += jnp.sum(dconv * m2, axis=0, keepdims=True)
        dw_ref[1:2, :] += jnp.sum(dconv * m1, axis=0, keepdims=True)
        dw_ref[2:3, :] += jnp.sum(dconv * m, axis=0, keepdims=True)
        base = 3 * CONV_W
        for idx, (x_ref, g_ref, parts, dgain_ref) in enumerate(((q_ref, qg_ref, (dq1, dq2, dq3), dqg_ref),
                                                                  (k_ref, kg_ref, (dk1, dk2, dk3), dkg_ref))):
            x = x_ref[...]
            dxh = parts[0][...] + parts[1][...] + parts[2][...]
            r = lax.rsqrt(_seg_dot(x * x, seg_ref[...]) + EPS)
            xhat = x * r
            tt = dxh * g_ref[...]
            dx = r * (tt - xhat * _seg_dot(tt * xhat, seg_ref[...]))
            dz_ref[:, base + idx * ATTN_W:base + (idx + 1) * ATTN_W] = dx.astype(BF16)
            dgain_ref[...] += jnp.sum(dxh * xhat, axis=0, keepdims=True)
        dz_ref[:, base + 2 * ATTN_W:base + 3 * ATTN_W] = (dv1[...] + dv2[...] + dv3[...]).astype(BF16)

    hb = tm // 8
    col = lambda c: _bs((tm, CONV_W), lambda i: (i, c))
    prev = lambda c: _bs((8, CONV_W), lambda i: (jnp.maximum(i * hb - 1, 0), c))
    nxt = lambda c: _bs((8, CONV_W), lambda i: (jnp.minimum((i + 1) * hb, t // 8 - 1), c))
    row = _bs((tm, ATTN_W), lambda i: (i, 0))
    vec = _bs((1, ATTN_W), lambda i: (0, 0))
    return pl.pallas_call(
        body, name=name, grid=(nt,),
        in_specs=[col(0), col(1), col(2), col(3), col(4), prev(1), prev(2), nxt(0), col(0), nxt(0),
                  _bs((3, CONV_W), lambda i: (0, 0)), vec, vec, _bs((ATTN_W, ATTN_W), lambda i: (0, 0))] + [row] * 9,
        out_specs=[_bs((tm, 6 * CONV_W), lambda i: (i, 0)), _bs((3, CONV_W), lambda i: (0, 0)), vec, vec],
        out_shape=[S((t, 6 * CONV_W), BF16), S((3, CONV_W), F32), S((1, ATTN_W), F32), S((1, ATTN_W), F32)],
        compiler_params=_cp("arbitrary"),
    )(z, z, z, z, z, z, z, z, dy, dy, conv_w, q_gain, k_gain, seg, *dqs, *dks, *dvs)


def rel_bias_grad(dbs, name):
    (bq, vq), _ = _band_tables()
    onehot = np.zeros((len(DILATIONS), REL_BUCKETS, BAND * 2 * BAND), np.float32)
    for g in range(len(DILATIONS)):
        idx = bq[g].reshape(-1)
        ok = vq.reshape(-1)
        onehot[g, idx[ok], np.nonzero(ok)[0]] = 1.0
    onehot = jnp.asarray(onehot, BF16)
    flat = [d.reshape(N_HEADS, BAND * 2 * BAND) for d in dbs]

    def body(oh_ref, d1, d2, d3, o_ref):
        acc = jnp.zeros((REL_BUCKETS, N_HEADS), F32)
        for g, d in enumerate((d1, d2, d3)):
            x = d[...]
            hi = x.astype(BF16)
            lo = (x - hi.astype(F32)).astype(BF16)
            acc += _dot_nt(oh_ref[g], hi) + _dot_nt(oh_ref[g], lo)
        o_ref[...] = acc

    full = lambda shp: _bs(shp, lambda: tuple(0 for _ in shp))
    return pl.pallas_call(
        body, name=name,
        in_specs=[full(onehot.shape)] + [full(flat[0].shape)] * 3,
        out_specs=full((REL_BUCKETS, N_HEADS)),
        out_shape=S((REL_BUCKETS, N_HEADS), F32),
        compiler_params=pltpu.CompilerParams(vmem_limit_bytes=VMEM_LIMIT),
    )(onehot, *flat)


def _lru_gates(xb, wa_ref, wx_ref, ba, bx):
    xb16 = xb.astype(BF16)
    ga = jnp.concatenate([_dot(xb16[:, LRU_BLOCK * g:LRU_BLOCK * (g + 1)], wa_ref[g]) for g in range(LRU_BLOCKS)], axis=1) + ba
    gx = jnp.concatenate([_dot(xb16[:, LRU_BLOCK * g:LRU_BLOCK * (g + 1)], wx_ref[g]) for g in range(LRU_BLOCKS)], axis=1) + bx
    return ga, gx


def _lru_coeffs(ga, gx, lam):
    sga = _sigmoid(ga)
    sp = _softplus(-lam)
    log_a = -LRU_C * sga * sp
    a = jnp.exp(log_a)
    one_m_a2 = _neg_expm1(2.0 * log_a)
    return sga, sp, a, one_m_a2, jnp.sqrt(one_m_a2), _sigmoid(gx)


def rec_fwd(z, conv_w, conv_b, wa, wx, ba, bx, lam, name):
    t = z.shape[0]
    w = z.shape[1] // 2
    tm = _row_tile(t, 256)

    def body(xp_ref, xh_ref, yb_ref, cw_ref, cb_ref, wa_ref, wx_ref, ba_ref, bx_ref, lam_ref,
             xb_ref, ga_ref, gx_ref, hs_ref, out_ref, carry):
        i = pl.program_id(0)

        @pl.when(i == 0)
        def _():
            carry[...] = jnp.zeros_like(carry)
        xp = xp_ref[...]
        xh = jnp.where(i == 0, 0.0, xh_ref[...])
        xb = cb_ref[...] + cw_ref[3:4, :] * xp
        for j in range(3):
            xb = xb + cw_ref[j:j + 1, :] * _shift_down(xp, xh, 3 - j)
        ga, gx = _lru_gates(xb, wa_ref, wx_ref, ba_ref[...], bx_ref[...])
        _, _, a, _, sq, sgx = _lru_coeffs(ga, gx, lam_ref[...])
        aa, bb = a, sq * sgx * xb
        s = 1
        while s < tm:
            bb = aa * _roll_fill(bb, s, 0.0, False) + bb
            aa = aa * _roll_fill(aa, s, 1.0, False)
            s *= 2
        hs = aa * carry[0:1, :] + bb
        xb_ref[...] = xb
        ga_ref[...] = ga
        gx_ref[...] = gx
        hs_ref[...] = hs
        carry[0:1, :] = hs_ref[tm - 1:tm, :]
        gy, _ = _gelu_and_grad(yb_ref[...])
        out_ref[...] = (hs * gy).astype(BF16)

    hb = tm // 8
    row = _bs((tm, w), lambda i: (i, 0))
    vec = _bs((1, w), lambda i: (0, 0))
    wsp = _bs((LRU_BLOCKS, LRU_BLOCK, LRU_BLOCK), lambda i: (0, 0, 0))
    return pl.pallas_call(
        body, name=name, grid=(t // tm,),
        in_specs=[row, _bs((8, w), lambda i: (jnp.maximum(i * hb - 1, 0), 0)), _bs((tm, w), lambda i: (i, 1)),
                  _bs((4, w), lambda i: (0, 0)), vec, wsp, wsp, vec, vec, vec],
        out_specs=[row] * 5,
        out_shape=[S((t, w), F32)] * 4 + [S((t, w), BF16)],
        scratch_shapes=[pltpu.VMEM((8, w), F32)],
        compiler_params=_cp("arbitrary"),
    )(z, z, z, conv_w, conv_b, wa, wx, ba, bx, lam)


def rec_bwd(d_out, z, xb, ga, gx, hs, conv_w, wa, wx, lam, name):
    t = z.shape[0]
    w = z.shape[1] // 2
    tm = _row_tile(t, 256)
    nt = t // tm

    def body(do_ref, xp_ref, xph_ref, yb_ref, xb_ref, ga_ref, gx_ref, hs_ref, hsh_ref, cw_ref, wa_ref, wx_ref, lam_ref,
             dz_ref, dga_ref, dgx_ref, sm_ref, c_lam, c_a, c_dxb):
        i = pl.program_id(0)

        @pl.when(i == 0)
        def _():
            sm_ref[...] = jnp.zeros_like(sm_ref)
            c_lam[...] = jnp.zeros_like(c_lam)
            c_a[...] = jnp.zeros_like(c_a)
            c_dxb[...] = jnp.zeros_like(c_dxb)
        d_o, yb, xb, hs = do_ref[...], yb_ref[...], xb_ref[...], hs_ref[...]
        lam = lam_ref[...]
        gy, dgy = _gelu_and_grad(yb)
        dz_ref[:, w:2 * w] = (d_o * hs * dgy).astype(BF16)
        sga, sp, a, one_m_a2, sq, sgx = _lru_coeffs(ga_ref[...], gx_ref[...], lam)
        aa = _shift_up(a, c_a[...], 1)
        bb = d_o * gy
        s = 1
        while s < tm:
            bb = aa * _roll_fill(bb, s, 0.0, True) + bb
            aa = aa * _roll_fill(aa, s, 1.0, True)
            s *= 2
        lmb = aa * c_lam[0:1, :] + bb
        c_a[...] = a[0:8, :]
        c_lam[...] = lmb[0:8, :]
        hprev = _shift_down(hs, jnp.where(i == nt - 1, 0.0, hsh_ref[...]), 1)
        d_sq = lmb * sgx * xb
        d_sgx = lmb * sq * xb
        d_log_a = lmb * hprev * a - d_sq * (1.0 - one_m_a2) / sq
        dga = d_log_a * (-LRU_C * sp) * sga * (1.0 - sga)
        dgx = d_sgx * sgx * (1.0 - sgx)
        dga16, dgx16 = dga.astype(BF16), dgx.astype(BF16)
        dga_ref[...] = dga16
        dgx_ref[...] = dgx16
        dxb = lmb * sq * sgx + jnp.concatenate(
            [_dot_nt(dga16[:, LRU_BLOCK * g:LRU_BLOCK * (g + 1)], wa_ref[g]) + _dot_nt(dgx16[:, LRU_BLOCK * g:LRU_BLOCK * (g + 1)], wx_ref[g])
             for g in range(LRU_BLOCKS)], axis=1)
        nxt = c_dxb[...]
        dxp = cw_ref[3:4, :] * dxb
        for j in range(3):
            dxp = dxp + cw_ref[j:j + 1, :] * _shift_up(dxb, nxt, 3 - j)
        c_dxb[...] = dxb[0:8, :]
        dz_ref[:, 0:w] = dxp.astype(BF16)
        xp = xp_ref[...]
        xph = jnp.where(i == nt - 1, 0.0, xph_ref[...])
        sm_ref[0:1, :] += jnp.sum(dga, axis=0, keepdims=True)
        sm_ref[1:2, :] += jnp.sum(dgx, axis=0, keepdims=True)
        sm_ref[2:3, :] += jnp.sum(d_log_a * (-LRU_C * sga), axis=0, keepdims=True) * (-_sigmoid(-lam))
        sm_ref[3:4, :] += jnp.sum(dxb, axis=0, keepdims=True)
        for j in range(4):
            sm_ref[4 + j:5 + j, :] += jnp.sum(dxb * _shift_down(xp, xph, 3 - j), axis=0, keepdims=True)

    hb = tm // 8
    rev = lambda c: _bs((tm, w), lambda i: (nt - 1 - i, c))
    halo = lambda c: _bs((8, w), lambda i: (jnp.maximum((nt - 1 - i) * hb - 1, 0), c))
    vec = _bs((1, w), lambda i: (0, 0))
    wsp = _bs((LRU_BLOCKS, LRU_BLOCK, LRU_BLOCK), lambda i: (0, 0, 0))
    return pl.pallas_call(
        body, name=name, grid=(nt,),
        in_specs=[rev(0), rev(0), halo(0), rev(1), rev(0), rev(0), rev(0), rev(0), halo(0),
                  _bs((4, w), lambda i: (0, 0)), wsp, wsp, vec],
        out_specs=[_bs((tm, 2 * w), lambda i: (nt - 1 - i, 0)), rev(0), rev(0), _bs((8, w), lambda i: (0, 0))],
        out_shape=[S((t, 2 * w), BF16), S((t, w), BF16), S((t, w), BF16), S((8, w), F32)],
        scratch_shapes=[pltpu.VMEM((8, w), F32)] * 3,
        compiler_params=_cp("arbitrary"),
    )(d_out, z, z, z, xb, ga, gx, hs, hs, conv_w, wa, wx, lam)


def ple_fwd(h, gain, wpg, layer, p, wpp, name):
    t, d = h.shape
    pd = p.shape[1]
    nk, _, rb, _ = wpg.shape
    cb = wpp.shape[3]
    tm = _row_tile(t, 512)

    def body(h_ref, g_ref, wg_ref, p_ref, wp_ref, o_ref, hn_ref, gp_ref, pp_ref):
        x = h_ref[...]
        hn = (x * _rstd(x) * g_ref[...]).astype(BF16)
        gp = _dot(hn[:, 0:rb], wg_ref[0])
        for k in range(1, nk):
            gp = gp + _dot(hn[:, rb * k:rb * (k + 1)], wg_ref[k])
        p16 = p_ref[...].astype(BF16)
        pp = jnp.concatenate([_dot(p16, wp_ref[k]) for k in range(nk)], axis=1)
        hn_ref[...] = hn
        gp_ref[...] = gp
        pp_ref[...] = pp
        o_ref[...] = x + _sigmoid(gp) * pp

    row = _bs((tm, d), lambda i: (i, 0))
    return pl.pallas_call(
        body, name=name, grid=(t // tm,),
        in_specs=[row, _bs((1, d), lambda i: (0, 0)), _bs((nk, None, rb, d), lambda i: (0, layer, 0, 0)),
                  _bs((tm, pd), lambda i: (i, 0)), _bs((nk, None, pd, cb), lambda i: (0, layer, 0, 0))],
        out_specs=[row] * 4,
        out_shape=[S((t, d), F32), S((t, d), BF16), S((t, d), F32), S((t, d), F32)],
        compiler_params=_cp("parallel"),
    )(h, gain, wpg, p, wpp)


def ple_bwd(dh, h, gain, wpg, layer, gp, pp, name):
    t, d = h.shape
    nk, _, rb, _ = wpg.shape
    tm = _row_tile(t, 512)

    def body(dh_ref, h_ref, g_ref, wg_ref, gp_ref, pp_ref, o_ref, dgp_ref, dpp_ref, dg_ref):
        @pl.when(pl.program_id(0) == 0)
        def _():
            dg_ref[...] = jnp.zeros_like(dg_ref)
        d_h = dh_ref[...]
        gate = _sigmoid(gp_ref[...])
        dgp = (d_h * pp_ref[...] * gate * (1.0 - gate)).astype(BF16)
        dgp_ref[...] = dgp
        dpp_ref[...] = (d_h * gate).astype(BF16)
        dhn = jnp.concatenate([_dot_nt(dgp, wg_ref[k]) for k in range(nk)], axis=1)
        dx, dgain = _rmsnorm_bwd(h_ref[...], g_ref[...], dhn)
        o_ref[...] = d_h + dx
        dg_ref[...] += dgain

    row = _bs((tm, d), lambda i: (i, 0))
    vec = _bs((1, d), lambda i: (0, 0))
    return pl.pallas_call(
        body, name=name, grid=(t // tm,),
        in_specs=[row, row, vec, _bs((nk, None, rb, d), lambda i: (0, layer, 0, 0)), row, row],
        out_specs=[row, row, row, vec],
        out_shape=[S((t, d), F32), S((t, d), BF16), S((t, d), BF16), S((1, d), F32)],
        compiler_params=_cp("arbitrary"),
    )(dh, h, gain, wpg, gp, pp)


def loss_and_grad(y, target, name):
    t, d = y.shape
    tm = _row_tile(t, 512)

    def body(y_ref, t_ref, l_ref, g_ref):
        @pl.when(pl.program_id(0) == 0)
        def _():
            l_ref[...] = jnp.zeros_like(l_ref)
        err = y_ref[...] - t_ref[...]
        g_ref[...] = err * (1.0 / d)
        l_ref[...] += jnp.sum(jnp.sum(err * err, axis=1, keepdims=True), axis=0, keepdims=True) * (0.5 / d)

    row = _bs((tm, d), lambda i: (i, 0))
    return pl.pallas_call(
        body, name=name, grid=(t // tm,),
        in_specs=[row, row],
        out_specs=[_bs((1, 1), lambda i: (0, 0)), row],
        out_shape=[S((1, 1), F32), S((t, d), F32)],
        compiler_params=_cp("arbitrary"),
    )(y, target)


def _vec(a, i):
    return a[i:i + 1]


def local_step(x, p, target, w, plan=None):
    t = x.shape[0]
    tm = _row_tile(t, 512)
    grads = {}
    if plan is not None:
        plan.grads = grads
    saved = []
    h = x
    bias_q, bias_k = band_bias(w["rel_bias"])
    qg = jnp.tile(w["hyb_q_gain"], (1, N_HEADS))
    kg = jnp.tile(w["hyb_k_gain"], (1, N_HEADS))

    def run(fn, *a, name):
        hst = plan.host(name) if plan is not None else None
        out = fn(*a, name, hst)
        if hst is not None:
            plan.done(hst)
        return out

    def lru_blocks(n):
        return jnp.transpose(w[n].reshape(N_SHARD, LRU_BLOCKS, 64, LRU_BLOCK), (1, 0, 2, 3)).reshape(LRU_BLOCKS, LRU_BLOCK, LRU_BLOCK)

    for i in range(2):
        s = {}
        s["h0"] = h
        s["hn1"], s["g1"], s["u1"], s["a1"] = run(ffn_up, h, _vec(w["ffn1_norm"], i), w[f"ffn1_w_gate/{i}"], w[f"ffn1_w_up/{i}"], 0, name=f"ffn1_up_{i}")
        h = run(ffn_down, s["a1"], w[f"ffn1_w_down/{i}"], 0, h, name=f"ffn1_down_{i}")
        s["h1"] = h
        if i == 0:
            w_hyb_in = w["hyb_w_in"].reshape(N_SHARD, D_MODEL, -1)
            w_hyb_out = w["hyb_w_out"].reshape(D_MODEL, D_MODEL)
            s["hnm"], s["z"] = run(norm_mm, h, _vec(w["mix_norm"], i), w_hyb_in, name="hyb_in")
            s["q"], s["k"], s["v"] = hyb_prep(s["z"], qg, kg, "hyb_prep")
            os_, lses = [], []
            for g, dil in enumerate(DILATIONS):
                o, l = attn_fwd(s["q"], s["k"], s["v"], bias_q[g], dil, f"attn_fwd_{dil}")
                os_.append(o)
                lses.append(l)
            s["y2"], s["lt"], s["ya"] = hyb_post(s["z"], w["hyb_conv_w"], os_, lses, "hyb_post")
            h = mm_acc(s["y2"], _bs((None, tm, ATTN_W), lambda r, k: (k, r, 0)),
                       w_hyb_out.reshape(2, ATTN_W, D_MODEL), _bs((None, ATTN_W, D_MODEL), lambda r, k: (k, 0, 0)),
                       h, 1.0, 2, t, D_MODEL, "hyb_out", tm)
        else:
            w_rec_in = w["rec_w_in"].reshape(N_SHARD, D_MODEL, -1)
            s["hnm"], s["z"] = run(norm_mm, h, _vec(w["mix_norm"], i), w_rec_in, name="rec_in")
            w_rec_out = w["rec_w_out"].reshape(D_MODEL, D_MODEL)
            lru_wa, lru_wx = lru_blocks("lru_wa"), lru_blocks("lru_wx")
            s["xb"], s["ga"], s["gx"], s["hs"], s["ro"] = rec_fwd(
                s["z"], w["rec_conv_w"], w["rec_conv_b"], lru_wa, lru_wx, w["lru_ba"], w["lru_bx"], w["lru_lambda"], "rec_fwd")
            h = mm_acc(s["ro"], _bs((tm, D_MODEL), lambda r, k: (r, 0)), w_rec_out, _bs((D_MODEL, D_MODEL), lambda r, k: (0, 0)),
                       h, 1.0, 1, t, D_MODEL, "rec_out", tm)
        s["h2"] = h
        s["hn2"], s["g2"], s["u2"], s["a2"] = run(ffn_up, h, _vec(w["ffn2_norm"], i), w[f"ffn2_w_gate/{i}"], w[f"ffn2_w_up/{i}"], 0, name=f"ffn2_up_{i}")
        h = run(ffn_down, s["a2"], w[f"ffn2_w_down/{i}"], 0, h, name=f"ffn2_down_{i}")
        s["h3"] = h
        h, s["hnp"], s["gp"], s["pp"] = ple_fwd(h, _vec(w["ple_norm"], i), w[f"ple_w_gate/{i}"], 0, p[i], w[f"ple_w_proj/{i}"], f"ple_fwd_{i}")
        saved.append(s)
    loss, dh = loss_and_grad(h, target, "loss")

    norm_g = {n: [None, None] for n in ("ffn1_norm", "mix_norm", "ffn2_norm", "ple_norm")}
    prev_f1 = prev_f2 = None
    prev_pg = prev_pp = None
    for i in (1, 0):
        s = saved[i]
        dh_out = dh
        dh, dgp, dpp, norm_g["ple_norm"][i] = ple_bwd(dh_out, s["h3"], _vec(w["ple_norm"], i), w[f"ple_w_gate/{i}"], 0, s["gp"], s["pp"], f"ple_bwd_{i}")
        prev_pg = tn_mm(s["hnp"], lambda tk: _bs((tk, 256), lambda k, j: (j, k)), dgp, lambda tk: _bs((tk, D_MODEL), lambda k, j: (j, 0)),
                        N_SHARD, t, 256, D_MODEL, S((N_SHARD, 2, 256, D_MODEL), BF16),
                        _bs((None, None, 256, D_MODEL), lambda k, j, i=i: (k, i, 0, 0)), 1.0, f"ple_gw_gate_{i}", prev_pg)
        prev_pp = tn_mm(p[i], lambda tk: _bs((tk, 256), lambda k, j: (j, 0)), dpp, lambda tk: _bs((tk, 256), lambda k, j: (j, k)),
                        N_SHARD, t, 256, 256, S((N_SHARD, 2, 256, 256), BF16),
                        _bs((None, None, 256, 256), lambda k, j, i=i: (k, i, 0, 0)), 1.0, f"ple_gw_proj_{i}", prev_pp)
        grads["ple_w_gate"], grads["ple_w_proj"] = prev_pg, prev_pp
        dh_out = dh
        dg, du = run(ffn_bwd_act, dh_out, w[f"ffn2_w_down/{i}"], 0, s["g2"], s["u2"], name=f"ffn2_bwd_act_{i}")
        prev_f2 = ffn_wgrads(s["hn2"], dh_out, s["a2"], dg, du, i, prev_f2, f"ffn2_gw_{i}")
        grads["ffn2_w_gate"], grads["ffn2_w_up"], grads["ffn2_w_down"] = prev_f2
        dh, norm_g["ffn2_norm"][i] = run(ffn_bwd_in, dg, du, w[f"ffn2_w_gate/{i}"], w[f"ffn2_w_up/{i}"], 0, s["h2"], _vec(w["ffn2_norm"], i), dh_out,
                                         name=f"ffn2_bwd_in_{i}")
        dh_out = dh
        if i == 1:
            d_o = nt_mm(dh_out, w_rec_out, "rec_bwd_out")
            grads["rec_w_out"] = tn_mm(s["ro"], lambda tk: _bs((tk, 256), lambda k, j: (j, k)), dh_out, lambda tk: _bs((tk, D_MODEL), lambda k, j: (j, 0)),
                                       N_SHARD, t, 256, D_MODEL, S((N_SHARD, 256, D_MODEL), BF16), _bs((None, 256, D_MODEL), lambda k, j: (k, 0, 0)),
                                       1.0, "rec_gw_out").reshape(N_SHARD, 1, 256, D_MODEL)
            dz, dga, dgx, small = rec_bwd(d_o, s["z"], s["xb"], s["ga"], s["gx"], s["hs"], w["rec_conv_w"], lru_wa, lru_wx, w["lru_lambda"], "rec_bwd")
            blk = lambda tk: _bs((tk, LRU_BLOCK), lambda k, j: (j, k))
            for nm, dgt in (("lru_wa", dga), ("lru_wx", dgx)):
                gw = tn_mm(s["xb"], blk, dgt, blk, LRU_BLOCKS, t, LRU_BLOCK, LRU_BLOCK, S((LRU_BLOCKS, LRU_BLOCK, LRU_BLOCK), BF16),
                           _bs((None, LRU_BLOCK, LRU_BLOCK), lambda k, j: (k, 0, 0)), 1.0, "rec_gw_" + nm)
                grads[nm] = jnp.transpose(gw.reshape(LRU_BLOCKS, N_SHARD, 64, LRU_BLOCK), (1, 0, 2, 3)).reshape(N_SHARD, 1, LRU_BLOCKS, 64, LRU_BLOCK)
            grads["lru_ba"], grads["lru_bx"], grads["lru_lambda"], grads["rec_conv_b"] = (small[r:r + 1] for r in range(4))
            grads["rec_conv_w"] = small[4:8]
            nb_, bw = N_SHARD, 512
            w_in, nm_in = w_rec_in, "rec_w_in"
        else:
            dy = nt_mm(dh_out, w_hyb_out, "hyb_bwd_out")
            grads["hyb_w_out"] = tn_mm(s["y2"], lambda tk: _bs((None, tk, 256), lambda k, j: (k // 2, j, k % 2)), dh_out,
                                       lambda tk: _bs((tk, D_MODEL), lambda k, j: (j, 0)),
                                       N_SHARD, t, 256, D_MODEL, S((N_SHARD, 256, D_MODEL), BF16), _bs((None, 256, D_MODEL), lambda k, j: (k, 0, 0)),
                                       1.0, "hyb_gw_out").reshape(N_SHARD, 1, 256, D_MODEL)
            delta, dya = attn_delta(dy, s["ya"], "attn_delta")
            dqs, dks, dvs, dbs = [], [], [], []
            for g, dil in enumerate(DILATIONS):
                dq, db = attn_bwd_dq(s["q"], s["k"], s["v"], dya, s["lt"], delta, bias_q[g], dil, f"attn_bwd_dq_{dil}")
                dk, dv = attn_bwd_dkv(s["q"], s["k"], s["v"], dya, s["lt"], delta, bias_k[g], dil, f"attn_bwd_dkv_{dil}")
                dqs.append(dq); dks.append(dk); dvs.append(dv); dbs.append(db)
            grads["rel_bias"] = rel_bias_grad(dbs, "rel_bias_grad")
            dz, grads["hyb_conv_w"], dqg, dkg = hyb_dz(s["z"], dy, w["hyb_conv_w"], qg, kg, dqs, dks, dvs, "hyb_dz")
            grads["hyb_q_gain"] = jnp.sum(dqg.reshape(N_HEADS, HEAD_DIM), axis=0, keepdims=True)
            grads["hyb_k_gain"] = jnp.sum(dkg.reshape(N_HEADS, HEAD_DIM), axis=0, keepdims=True)
            nb_, bw = N_SHARD, 768
            w_in, nm_in = w_hyb_in, "hyb_w_in"
        grads[nm_in] = tn_mm(s["hnm"], lambda tk: _bs((tk, D_MODEL), lambda k, j: (j, 0)), dz, lambda tk, bw=bw: _bs((tk, bw), lambda k, j: (j, k)),
                             nb_, t, D_MODEL, bw, S((nb_, D_MODEL, bw), BF16), _bs((None, D_MODEL, bw), lambda k, j: (k, 0, 0)),
                             1.0, f"mix_gw_in_{i}").reshape(nb_, 1, D_MODEL, bw)
        dh, norm_g["mix_norm"][i] = run(
            nt_acc_normbwd, [(dz, _bs((tm, bw), lambda r, k: (r, k)), w_in, _bs((None, D_MODEL, bw), lambda r, k: (k, 0, 0)))],
            nb_, s["h1"], _vec(w["mix_norm"], i), dh_out, name=f"mix_bwd_in_{i}")
        dh_out = dh
        dg, du = run(ffn_bwd_act, dh_out, w[f"ffn1_w_down/{i}"], 0, s["g1"], s["u1"], name=f"ffn1_bwd_act_{i}")
        prev_f1 = ffn_wgrads(s["hn1"], dh_out, s["a1"], dg, du, i, prev_f1, f"ffn1_gw_{i}")
        grads["ffn1_w_gate"], grads["ffn1_w_up"], grads["ffn1_w_down"] = prev_f1
        dh, norm_g["ffn1_norm"][i] = run(ffn_bwd_in, dg, du, w[f"ffn1_w_gate/{i}"], w[f"ffn1_w_up/{i}"], 0, s["h0"], _vec(w["ffn1_norm"], i), dh_out,
                                         name=f"ffn1_bwd_in_{i}")
    for n, (g0, g1) in norm_g.items():
        grads[n] = jnp.concatenate([g0, g1], axis=0)
    return loss, dh, grads


def gather_weights(shards, name):
    n = len(shards)

    def body(*refs):
        ins, outs = refs[:n], refs[n:2 * n]
        send1, recv1, send2, recv2, lsem = refs[2 * n:]
        x, y, c, k, chips, kk = _place()
        sib = (x, y, 1 - c)

        def remote(src, dst, ssem, rsem, to):
            return pltpu.make_async_remote_copy(src_ref=src, dst_ref=dst, send_sem=ssem, recv_sem=rsem, device_id=to, device_id_type=MESH)

        local = [pltpu.make_async_copy(ins[a], outs[a].at[k], lsem.at[a]) for a in range(n)]
        for cp in local:
            cp.start()
        sends = []
        for a in range(n):
            for j, chip in enumerate(chips):
                cp = remote(ins[a].at[c], outs[a].at[k, c], send1.at[3 * a + j], recv1.at[3 * a + j], (*chip, c))
                cp.start()
                sends.append(cp)
        for a in range(n):
            for j, chip in enumerate(chips):
                remote(ins[a].at[c], outs[a].at[kk[j], c], send1.at[3 * a + j], recv1.at[3 * a + j], (*chip, c)).wait_recv()
                cp = remote(outs[a].at[kk[j], c], outs[a].at[kk[j], c], send2.at[3 * a + j], recv2.at[3 * a + j], sib)
                cp.start()
                sends.append(cp)
        for a in range(n):
            for j in range(3):
                remote(outs[a].at[kk[j], 1 - c], outs[a].at[kk[j], 1 - c], send2.at[3 * a + j], recv2.at[3 * a + j], sib).wait_recv()
        for cp in sends:
            cp.wait_send()
        for cp in local:
            cp.wait()

    return pl.pallas_call(
        body, name=name,
        in_specs=[_ANY] * n, out_specs=[_ANY] * n,
        out_shape=[S((N_SHARD,) + s.shape, s.dtype) for s in shards],
        scratch_shapes=[pltpu.SemaphoreType.DMA((3 * n,))] * 4 + [pltpu.SemaphoreType.DMA((n,))],
    )(*shards)


def exchange_cores(rs, name):
    n = len(rs)

    def body(*refs):
        outs = refs[n:2 * n]
        send, recv = refs[2 * n:]
        x, y, c = lax.axis_index("x"), lax.axis_index("y"), lax.axis_index("c")
        sends = []
        for a in range(n):
            for k in range(N_SHARD):
                slot = outs[a].at[2 * k + c]
                cp = _remote(slot, slot, send.at[N_SHARD * a + k], recv.at[N_SHARD * a + k], (x, y, 1 - c))
                cp.start()
                sends.append(cp)
        for a in range(n):
            for k in range(N_SHARD):
                slot = outs[a].at[2 * k + 1 - c]
                _remote(slot, slot, send.at[N_SHARD * a + k], recv.at[N_SHARD * a + k], (x, y, 1 - c)).wait_recv()
        for cp in sends:
            cp.wait_send()

    return pl.pallas_call(
        body, name=name,
        in_specs=[_ANY] * n, out_specs=[_ANY] * n,
        out_shape=[S(r.shape, r.dtype) for r in rs],
        input_output_aliases={a: a for a in range(n)},
        scratch_shapes=[pltpu.SemaphoreType.DMA((N_SHARD * n,))] * 2,
    )(*rs)


def allgather8(a, name):
    def body(a_ref, o_ref, send, recv, lsem):
        x, y, c = lax.axis_index("x"), lax.axis_index("y"), lax.axis_index("c")
        me = 4 * x + 2 * y + c
        local = pltpu.make_async_copy(a_ref, o_ref.at[me], lsem)
        local.start()
        cps = []
        for f in range(1, N_DEV):
            fx, fy, fc = (f >> 2) & 1, (f >> 1) & 1, f & 1
            peer = (1 - x if fx else x, 1 - y if fy else y, 1 - c if fc else c)
            cp = pltpu.make_async_remote_copy(src_ref=a_ref, dst_ref=o_ref.at[me], send_sem=send.at[f - 1], recv_sem=recv.at[f - 1],
                                              device_id=peer, device_id_type=MESH)
            cp.start()
            cps.append((cp, 4 * peer[0] + 2 * peer[1] + peer[2], f))
        for cp, pidx, f in cps:
            pltpu.make_async_remote_copy(src_ref=a_ref, dst_ref=o_ref.at[pidx], send_sem=send.at[f - 1], recv_sem=recv.at[f - 1],
                                         device_id=(x, y, c), device_id_type=MESH).wait_recv()
        for cp, _, _ in cps:
            cp.wait_send()
        local.wait()

    return pl.pallas_call(
        body, name=name, in_specs=[_ANY], out_specs=_ANY,
        out_shape=S((N_DEV,) + a.shape, a.dtype),
        scratch_shapes=[pltpu.SemaphoreType.DMA((N_DEV - 1,)), pltpu.SemaphoreType.DMA((N_DEV - 1,)), pltpu.SemaphoreType.DMA],
    )(a)


def sum8(a, name):
    _, r, c = a.shape

    def body(a_ref, o_ref):
        acc = a_ref[0]
        for j in range(1, N_DEV):
            acc = acc + a_ref[j]
        o_ref[...] = acc

    return pl.pallas_call(
        body, name=name, in_specs=[_bs((N_DEV, r, c), lambda: (0, 0, 0))], out_specs=_bs((r, c), lambda: (0, 0)),
        out_shape=S((r, c), F32),
    )(a)


def adamw(w, m, v, g, name):
    r, c = w.shape
    tr = _row_tile(r, 256)
    summed = g.ndim == 3

    def body(w_ref, m_ref, v_ref, g_ref, go_ref, d_ref, mo_ref, vo_ref):
        if summed:
            gr = g_ref[0].astype(F32)
            for j in range(1, N_DEV):
                gr = gr + g_ref[j].astype(F32)
        else:
            gr = g_ref[...]
        m_new = ADAM_B1 * m_ref[...] + (1.0 - ADAM_B1) * gr
        v_new = ADAM_B2 * v_ref[...] + (1.0 - ADAM_B2) * (gr * gr)
        m_hat = m_new / (1.0 - ADAM_B1 ** ADAM_STEP)
        v_hat = v_new / (1.0 - ADAM_B2 ** ADAM_STEP)
        go_ref[...] = gr
        d_ref[...] = -ADAM_LR * (m_hat / (jnp.sqrt(v_hat) + ADAM_EPS) + ADAM_WD * w_ref[...])
        mo_ref[...] = m_new
        vo_ref[...] = v_new

    row = _bs((tr, c), lambda i: (i, 0))
    gspec = _bs((N_DEV, tr, c), lambda i: (0, i, 0)) if summed else row
    return pl.pallas_call(
        body, name=name, grid=(r // tr,),
        in_specs=[row, row, row, gspec], out_specs=[row] * 4, out_shape=[S((r, c), F32)] * 4,
        compiler_params=_cp("parallel"),
    )(w, m, v, g)


WEIGHTS = ["rel_bias", "ffn1_norm", "ffn1_w_gate", "ffn1_w_up", "ffn1_w_down", "mix_norm", "hyb_w_in", "hyb_conv_w", "hyb_q_gain",
           "hyb_k_gain", "hyb_w_out", "rec_w_in", "rec_conv_w", "rec_conv_b", "lru_wa", "lru_ba", "lru_wx", "lru_bx", "lru_lambda",
           "rec_w_out", "ffn2_norm", "ffn2_w_gate", "ffn2_w_up", "ffn2_w_down", "ple_norm", "ple_w_gate", "ple_w_proj"]
BIG = ["ffn1_w_gate", "ffn1_w_up", "ffn1_w_down", "hyb_w_in", "hyb_w_out", "rec_w_in", "lru_wa", "lru_wx", "rec_w_out",
       "ffn2_w_gate", "ffn2_w_up", "ffn2_w_down", "ple_w_gate", "ple_w_proj"]
SMALL_SHARDED = ["hyb_conv_w", "rec_conv_w", "rec_conv_b", "lru_ba", "lru_bx", "lru_lambda"]
SMALL = ["rel_bias", "ffn1_norm", "mix_norm", "ffn2_norm", "ple_norm", "hyb_q_gain", "hyb_k_gain"] + SMALL_SHARDED
PACK_W = 1024
PER_LAYER = ["ffn1_w_gate", "ffn1_w_up", "ffn1_w_down", "ffn2_w_gate", "ffn2_w_up", "ffn2_w_down", "ple_w_gate", "ple_w_proj"]
FIRST = ["ffn1_w_gate/0", "ffn1_w_up/0", "ffn1_w_down/0"]
GATHER_PLAN = {
    "ffn1_up_0": ["hyb_w_in", "hyb_w_out", "ple_w_gate/0", "ple_w_proj/0"],
    "ffn1_down_0": ["ffn2_w_gate/0"],
    "hyb_in": ["ffn2_w_up/0", "ffn2_w_down/0"],
    "ffn2_up_0": ["ffn1_w_gate/1", "ffn1_w_up/1"],
    "ffn2_down_0": ["ffn1_w_down/1"],
    "ffn1_up_1": ["rec_w_in", "lru_wa", "lru_wx", "rec_w_out", "ffn2_w_gate/1"],
    "ffn1_down_1": ["ffn2_w_up/1"],
    "rec_in": ["ffn2_w_down/1", "ple_w_gate/1", "ple_w_proj/1"],
}
SCATTER_PLAN = {
    "ffn2_bwd_act_1": [("ple_w_gate", 1), ("ple_w_proj", 1)],
    "ffn2_bwd_in_1": [("ffn2_w_gate", 1), ("ffn2_w_up", 1), ("ffn2_w_down", 1)],
    "mix_bwd_in_1": [("rec_w_in", 0), ("rec_w_out", 0), ("lru_wa", 0), ("lru_wx", 0)],
    "ffn1_bwd_in_1": [("ffn1_w_gate", 1), ("ffn1_w_up", 1), ("ffn1_w_down", 1)],
    "ffn2_bwd_act_0": [("ple_w_gate", 0), ("ple_w_proj", 0)],
    "ffn2_bwd_in_0": [("ffn2_w_gate", 0), ("ffn2_w_up", 0), ("ffn2_w_down", 0)],
    "mix_bwd_in_0": [("hyb_w_in", 0), ("hyb_w_out", 0)],
    "ffn1_bwd_in_0": [("ffn1_w_gate", 0), ("ffn1_w_up", 0), ("ffn1_w_down", 0)],
}


class Plan:
    def __init__(self, shards, w):
        self.shards, self.w, self.grads, self.landed = shards, w, None, {}

    def host(self, kname):
        if kname in GATHER_PLAN:
            h = Host("gather", [self.shards[n] for n in GATHER_PLAN[kname]])
            h.names = GATHER_PLAN[kname]
            return h
        if kname in SCATTER_PLAN:
            h = Host("scatter", [(self.grads[n], lay, self.landed.get(n)) for n, lay in SCATTER_PLAN[kname]])
            h.names = [n for n, _ in SCATTER_PLAN[kname]]
            return h
        return None

    def done(self, h):
        for n, o in zip(h.names, h.outs):
            if h.kind == "gather":
                self.w[n] = o
            else:
                self.landed[n] = o


def _halves(a):
    if a.shape[0] == 2:
        return a
    return a.reshape((2, a.shape[1] // 2) + a.shape[2:])


def _pack_rows(arrs, width):
    rows, offs, r0 = [], [], 0
    for a in arrs:
        if a.shape[1] > width:
            a = a.reshape(-1, width)
        rows.append(jnp.pad(a, ((0, 0), (0, width - a.shape[1]))))
        offs.append(r0)
        r0 += a.shape[0]
    pad = (-r0) % 8
    if pad:
        rows.append(jnp.zeros((pad, width), F32))
    return jnp.concatenate(rows, axis=0), offs


def kernel(x, p, rel_bias, ffn1_norm, ffn1_w_gate, ffn1_w_up, ffn1_w_down, mix_norm, hyb_w_in, hyb_conv_w, hyb_q_gain, hyb_k_gain, hyb_w_out, rec_w_in, rec_conv_w, rec_conv_b, lru_wa, lru_ba, lru_wx, lru_bx, lru_lambda, rec_w_out, ffn2_norm, ffn2_w_gate, ffn2_w_up, ffn2_w_down, ple_norm, ple_w_gate, ple_w_proj, loss_target, m_rel_bias, m_ffn1_norm, m_ffn1_w_gate, m_ffn1_w_up, m_ffn1_w_down, m_mix_norm, m_hyb_w_in, m_hyb_conv_w, m_hyb_q_gain, m_hyb_k_gain, m_hyb_w_out, m_rec_w_in, m_rec_conv_w, m_rec_conv_b, m_lru_wa, m_lru_ba, m_lru_wx, m_lru_bx, m_lru_lambda, m_rec_w_out, m_ffn2_norm, m_ffn2_w_gate, m_ffn2_w_up, m_ffn2_w_down, m_ple_norm, m_ple_w_gate, m_ple_w_proj, v_rel_bias, v_ffn1_norm, v_ffn1_w_gate, v_ffn1_w_up, v_ffn1_w_down, v_mix_norm, v_hyb_w_in, v_hyb_conv_w, v_hyb_q_gain, v_hyb_k_gain, v_hyb_w_out, v_rec_w_in, v_rec_conv_w, v_rec_conv_b, v_lru_wa, v_lru_ba, v_lru_wx, v_lru_bx, v_lru_lambda, v_rec_w_out, v_ffn2_norm, v_ffn2_w_gate, v_ffn2_w_up, v_ffn2_w_down, v_ple_norm, v_ple_w_gate, v_ple_w_proj):
    given = dict(locals())
    wts = {n: given[n] for n in WEIGHTS}
    k_chip = 2 * lax.axis_index("x") + lax.axis_index("y")

    shards = {}
    for n in BIG:
        b16 = wts[n].astype(BF16)
        if n in PER_LAYER:
            shards[n + "/0"], shards[n + "/1"] = b16[0:1], b16[1:2]
        else:
            shards[n] = b16
    first = gather_weights([_halves(shards[n]) for n in FIRST], "gather_first")
    w = {n: g.reshape((N_SHARD,) + shards[n].shape) for n, g in zip(FIRST, first)}
    plan = Plan(shards, w)
    sm2d = {n: wts[n].reshape(-1, wts[n].shape[-1]) for n in SMALL_SHARDED}
    slab, offs = _pack_rows([sm2d[n] for n in SMALL_SHARDED], 256)
    slabs = allgather8(slab, "gather_small")[0::2]
    for n, o in zip(SMALL_SHARDED, offs):
        r, cw = sm2d[n].shape
        w[n] = jnp.concatenate([slabs[kc, o:o + r, :cw] for kc in range(N_SHARD)], axis=1)
    for n in SMALL:
        if n not in SMALL_SHARDED:
            w[n] = wts[n]

    loss, dx, grads = local_step(x[0], p[:, 0], loss_target[0], w, plan)
    loss = lax.psum(loss[0, 0], ("x", "y", "c"))

    landed = exchange_cores([plan.landed[n] for n in BIG], "exchange_cores")
    out = {}
    for n, r8 in zip(BIG, landed):
        shp = wts[n].shape
        c2 = shp[-1]
        two = lambda a: a.reshape(-1, c2)
        res = adamw(two(wts[n]), two(given["m_" + n]), two(given["v_" + n]), r8.reshape(N_DEV, -1, c2), "adamw_" + n)
        out[n] = [a.reshape(shp) for a in res]
    g2d = [grads[n].reshape(-1, grads[n].shape[-1]) if n != "rel_bias" else grads[n].reshape(1, -1) for n in SMALL]
    gslab, goffs = _pack_rows(g2d, PACK_W)
    gsum = sum8(allgather8(gslab, "gather_small_grads"), "sum_small_grads")
    for n, o, g in zip(SMALL, goffs, g2d):
        shp = wts[n].shape
        r, cw = g.shape
        gs = gsum[o:o + r, :cw]
        if n in SMALL_SHARDED:
            sw = shp[-1]
            gs = lax.dynamic_slice_in_dim(gs, k_chip * sw, sw, axis=1)
        gs = gs.reshape(shp)
        two = lambda a: a.reshape(-1, shp[-1])
        res = adamw(two(wts[n]), two(given["m_" + n]), two(given["v_" + n]), two(gs), "adamw_" + n)
        out[n] = [a.reshape(shp) for a in res]
    return (loss, dx[None], *[out[n][0] for n in WEIGHTS], *[out[n][1] for n in WEIGHTS],
            *[out[n][2] for n in WEIGHTS], *[out[n][3] for n in WEIGHTS])
```

```python
import functools
import math

import numpy as np
import jax
import jax.numpy as jnp
from jax import lax
from jax.experimental import pallas as pl
from jax.experimental.pallas import tpu as pltpu

F32, BF16 = jnp.float32, jnp.bfloat16
S = jax.ShapeDtypeStruct
MESH = pl.DeviceIdType.MESH

D_MODEL = 1024
N_SHARD = 4
N_DEV = 8
HEAD_DIM = 64
N_HEADS = 8
ATTN_W = N_HEADS * HEAD_DIM
CONV_W = 512
BAND = 128
DILATIONS = (1, 4, 16)
REL_BUCKETS = 32
REL_MAX_DIST = 2048
LRU_BLOCKS = 4
LRU_BLOCK = 256
LRU_C = 8.0
EPS = 1e-6
NEG = -1e30
VMEM_LIMIT = 56 * 1024 * 1024
FFN_ROWS = 1024
TN_ROWS = 2048

ADAM_LR, ADAM_B1, ADAM_B2, ADAM_EPS, ADAM_WD, ADAM_STEP = 0.001, 0.9, 0.999, 1e-08, 0.01, 10


def _cp(*sem):
    return pltpu.CompilerParams(dimension_semantics=sem, vmem_limit_bytes=VMEM_LIMIT)


def _bs(shape, imap):
    return pl.BlockSpec(shape, imap)


def _row_tile(t, want):
    for cand in range(min(want, t) // 8 * 8, 0, -8):
        if t % cand == 0:
            return cand
    return t


_ANY = pl.BlockSpec(memory_space=pl.ANY)


def _place():
    x, y, c = lax.axis_index("x"), lax.axis_index("y"), lax.axis_index("c")
    chips = [(1 - x, y), (x, 1 - y), (1 - x, 1 - y)]
    return x, y, c, 2 * x + y, chips, [2 * cx + cy for cx, cy in chips]


def _remote(src, dst, ssem, rsem, to):
    return pltpu.make_async_remote_copy(src_ref=src, dst_ref=dst, send_sem=ssem, recv_sem=rsem, device_id=to, device_id_type=MESH)


class Host:
    def __init__(self, kind, items, forwards=()):
        self.kind, self.items, self.forwards, self.outs = kind, items, list(forwards), None

    def n_sems(self):
        return 3 * len(self.items) + N_SHARD * len(self.forwards), len(self.items)

    def operands(self):
        if self.kind == "gather":
            return list(self.items), [S((N_SHARD,) + s.shape, s.dtype) for s in self.items], {}
        xin, shapes, alias = [], [], {}
        for a, (g, _, r_prev) in enumerate(self.items):
            xin.append(g)
            if r_prev is not None:
                alias[len(xin)] = a
                xin.append(r_prev)
            shapes.append(S((N_DEV,) + g.shape[1:], g.dtype))
        for f, r in enumerate(self.forwards):
            alias[len(xin)] = len(self.items) + f
            xin.append(r)
            shapes.append(S(r.shape, r.dtype))
        return xin, shapes, alias

    def copies(self, xi, xo, send, recv, lsem):
        x, y, c, k, chips, kk = _place()
        starts, waits = [], []
        pos = 0
        for f in range(len(self.forwards)):
            arr = xo[len(self.items) + f]
            for kq in range(N_SHARD):
                sem = 3 * len(self.items) + N_SHARD * f + kq
                cp = _remote(arr.at[2 * kq + c], arr.at[2 * kq + c], send.at[sem], recv.at[sem], (x, y, 1 - c))
                starts.append((cp, "start"))
                waits.append((cp, "wait_send"))
                other = arr.at[2 * kq + 1 - c]
                waits.append((_remote(other, other, send.at[sem], recv.at[sem], (x, y, 1 - c)), "wait_recv"))
        for a, item in enumerate(self.items):
            if self.kind == "gather":
                src_of = lambda chip_idx, s=xi[a]: s
                dst_of = lambda chip_idx, o=xo[a]: o.at[chip_idx]
                mine, theirs = k, kk
            else:
                g_ref = xi[pos]
                pos += 1 if item[2] is None else 2
                lay = item[1]
                src_of = lambda chip_idx, g=g_ref, lay=lay: g.at[chip_idx, lay]
                dst_of = lambda slot, o=xo[a], lay=lay: o.at[slot, lay]
                mine, theirs = 2 * k + c, [2 * kj + c for kj in kk]
            own_src = src_of(k)
            local = pltpu.make_async_copy(own_src, dst_of(mine), lsem.at[a])
            starts.append((local, "start"))
            waits.append((local, "wait"))
            for j, chip in enumerate(chips):
                src = own_src if self.kind == "gather" else src_of(kk[j])
                cp = _remote(src, dst_of(mine), send.at[3 * a + j], recv.at[3 * a + j], (*chip, c))
                starts.append((cp, "start"))
                waits.append((cp, "wait_send"))
                waits.append((_remote(own_src, dst_of(theirs[j]), send.at[3 * a + j], recv.at[3 * a + j], (*chip, c)), "wait_recv"))
        return starts, waits


def _call(host, body, *, name, grid, in_specs, out_specs, out_shape, scratch_shapes=(), compiler_params=None, args, aliases=None):
    aliases = dict(aliases or {})
    if host is None:
        return pl.pallas_call(body, name=name, grid=grid, in_specs=in_specs, out_specs=out_specs, out_shape=out_shape,
                              scratch_shapes=list(scratch_shapes), input_output_aliases=aliases, compiler_params=compiler_params)(*args)
    single = not isinstance(out_shape, (list, tuple))
    out_specs_l = [out_specs] if single else list(out_specs)
    out_shape_l = [out_shape] if single else list(out_shape)
    n_in, n_out, n_scr = len(in_specs), len(out_shape_l), len(scratch_shapes)
    xin, xshapes, xalias = host.operands()
    n_items = len(xshapes)
    n_rsem, n_lsem = host.n_sems()
    for i_in, i_out in xalias.items():
        aliases[n_in + i_in] = n_out + i_out
    nd = len(grid)

    def hosted(*refs):
        ins, xi = refs[:n_in], refs[n_in:n_in + len(xin)]
        o0 = n_in + len(xin)
        outs, xo = refs[o0:o0 + n_out], refs[o0 + n_out:o0 + n_out + n_items]
        s0 = o0 + n_out + n_items
        scr = refs[s0:s0 + n_scr]
        send, recv, lsem = refs[s0 + n_scr:]
        first = functools.reduce(jnp.logical_and, [pl.program_id(d) == 0 for d in range(nd)])
        last = functools.reduce(jnp.logical_and, [pl.program_id(d) == grid[d] - 1 for d in range(nd)])
        starts, waits = host.copies(xi, xo, send, recv, lsem)

        @pl.when(first)
        def _():
            for cp, how in starts:
                getattr(cp, how)()
        body(*ins, *outs, *scr)

        @pl.when(last)
        def _():
            for cp, how in waits:
                getattr(cp, how)()

    res = pl.pallas_call(
        hosted, name=name, grid=grid,
        in_specs=list(in_specs) + [_ANY] * len(xin),
        out_specs=out_specs_l + [_ANY] * n_items,
        out_shape=out_shape_l + xshapes,
        scratch_shapes=list(scratch_shapes) + [pltpu.SemaphoreType.DMA((n_rsem,)), pltpu.SemaphoreType.DMA((n_rsem,)),
                                               pltpu.SemaphoreType.DMA((max(n_lsem, 1),))],
        input_output_aliases=aliases,
        compiler_params=pltpu.CompilerParams(dimension_semantics=("arbitrary",) * nd, vmem_limit_bytes=VMEM_LIMIT),
    )(*args, *xin)
    host.outs = list(res[n_out:])
    return res[0] if single else list(res[:n_out])


def _rstd(x):
    return lax.rsqrt(jnp.mean(x * x, axis=-1, keepdims=True) + EPS)


def _sigmoid(x):
    return 1.0 / (1.0 + jnp.exp(-x))


def _dot(a, b):
    return jnp.dot(a, b, preferred_element_type=F32)


def _dot_nt(a, b):
    return lax.dot_general(a, b, (((1,), (1,)), ((), ())), preferred_element_type=F32)


def _dot_tn(a, b):
    return lax.dot_general(a, b, (((0,), (0,)), ((), ())), preferred_element_type=F32)


def _seg_dot(x, seg_bf16):
    hi = x.astype(BF16)
    lo = (x - hi.astype(F32)).astype(BF16)
    return _dot(hi, seg_bf16) + _dot(lo, seg_bf16)


def _shift_down(x, prev8, s):
    if s == 0:
        return x
    tm = x.shape[0]
    row = lax.broadcasted_iota(jnp.int32, x.shape, 0)
    main = jnp.where(row >= s, pltpu.roll(x, s, axis=0), 0.0)
    row8 = lax.broadcasted_iota(jnp.int32, prev8.shape, 0)
    head = jnp.where(row8 < s, pltpu.roll(prev8, s, axis=0), 0.0)
    if tm == 8:
        return main + head
    return main + jnp.concatenate([head, jnp.zeros((tm - 8, x.shape[1]), x.dtype)], axis=0)


def _shift_up(x, next8, s):
    if s == 0:
        return x
    tm = x.shape[0]
    row = lax.broadcasted_iota(jnp.int32, x.shape, 0)
    main = jnp.where(row < tm - s, pltpu.roll(x, tm - s, axis=0), 0.0)
    row8 = lax.broadcasted_iota(jnp.int32, next8.shape, 0)
    tail = jnp.where(row8 >= 8 - s, pltpu.roll(next8, 8 - s, axis=0), 0.0)
    if tm == 8:
        return main + tail
    return main + jnp.concatenate([jnp.zeros((tm - 8, x.shape[1]), x.dtype), tail], axis=0)


def _roll_fill(x, s, fill, up):
    tm = x.shape[0]
    row = lax.broadcasted_iota(jnp.int32, x.shape, 0)
    if up:
        return jnp.where(row < tm - s, pltpu.roll(x, tm - s, axis=0), fill)
    return jnp.where(row >= s, pltpu.roll(x, s, axis=0), fill)


def _log1p(y):
    u = 1.0 + y
    return jnp.where(u == 1.0, y, jnp.log(u) * (y / jnp.where(u == 1.0, 1.0, u - 1.0)))


def _softplus(x):
    return jnp.maximum(x, 0.0) + _log1p(jnp.exp(-jnp.abs(x)))


def _neg_expm1(y):
    series = -y * (1.0 + y * (0.5 + y * (1.0 / 6.0 + y * (1.0 / 24.0 + y * (1.0 / 120.0)))))
    return jnp.where(jnp.abs(y) < 0.03, series, 1.0 - jnp.exp(y))


_GELU_C = math.sqrt(2.0 / math.pi)


def _gelu_and_grad(x):
    inner = _GELU_C * (x + 0.044715 * x * x * x)
    t = jnp.tanh(inner)
    g = 0.5 * x * (1.0 + t)
    dg = 0.5 * (1.0 + t) + 0.5 * x * (1.0 - t * t) * _GELU_C * (1.0 + 3.0 * 0.044715 * x * x)
    return g, dg


def _rmsnorm_bwd(x, gain, dy):
    r = _rstd(x)
    xhat = x * r
    dxhat = dy * gain
    dx = r * (dxhat - xhat * jnp.mean(dxhat * xhat, axis=-1, keepdims=True))
    return dx, jnp.sum(dy * xhat, axis=0, keepdims=True)


def ffn_up(h, gain, wg, wu, layer, name, host=None):
    t, d = h.shape
    nk, _, _, f = wg.shape
    tm = _row_tile(t, FFN_ROWS)

    def body(h_ref, g_ref, wg_ref, wu_ref, hn_ref, gg_ref, uu_ref, aa_ref, hn_scr):
        @pl.when(pl.program_id(1) == 0)
        def _():
            x = h_ref[...]
            hn = (x * _rstd(x) * g_ref[...]).astype(BF16)
            hn_scr[...] = hn
            hn_ref[...] = hn
        hn = hn_scr[...]
        g = _dot(hn, wg_ref[...])
        u = _dot(hn, wu_ref[...])
        gg_ref[...] = g.astype(BF16)
        uu_ref[...] = u.astype(BF16)
        aa_ref[...] = (g * _sigmoid(g) * u).astype(BF16)

    wspec = _bs((None, None, d, f), lambda i, k: (k, layer, 0, 0))
    aspec = _bs((None, tm, f), lambda i, k: (k, i, 0))
    return _call(
        host, body, name=name, grid=(t // tm, nk),
        in_specs=[_bs((tm, d), lambda i, k: (i, 0)), _bs((1, d), lambda i, k: (0, 0)), wspec, wspec],
        out_specs=[_bs((tm, d), lambda i, k: (i, 0)), aspec, aspec, aspec],
        out_shape=[S((t, d), BF16), S((nk, t, f), BF16), S((nk, t, f), BF16), S((nk, t, f), BF16)],
        scratch_shapes=[pltpu.VMEM((tm, d), BF16)],
        compiler_params=_cp("parallel", "arbitrary"),
        args=(h, gain, wg, wu))


def mm_acc(a, a_spec, b, b_spec, res, scale, nk, t, n, tm, name, host=None):
    def body(a_ref, b_ref, r_ref, o_ref, acc):
        k = pl.program_id(1)

        @pl.when(k == 0)
        def _():
            acc[...] = jnp.zeros_like(acc)
        acc[...] += _dot(a_ref[...].astype(BF16), b_ref[...])

        @pl.when(k == nk - 1)
        def _():
            o_ref[...] = r_ref[...] + scale * acc[...]

    return _call(
        host, body, name=name, grid=(t // tm, nk),
        in_specs=[a_spec, b_spec, _bs((tm, n), lambda i, k: (i, 0))],
        out_specs=_bs((tm, n), lambda i, k: (i, 0)),
        out_shape=S((t, n), F32),
        scratch_shapes=[pltpu.VMEM((tm, n), F32)],
        compiler_params=_cp("parallel", "arbitrary"),
        args=(a, b, res))


def ffn_down(a, wd, layer, h, name, host=None):
    nk, t, f = a.shape
    d = h.shape[1]
    tm = _row_tile(t, FFN_ROWS)
    return mm_acc(a, _bs((None, tm, f), lambda i, k: (k, i, 0)),
                  wd, _bs((None, None, f, d), lambda i, k: (k, layer, 0, 0)),
                  h, 0.5, nk, t, d, tm, name, host)


def ffn_bwd_act(dh, wd, layer, gg, uu, name, host=None):
    nk, t, f = gg.shape
    d = dh.shape[1]
    tm = _row_tile(t, FFN_ROWS)

    def body(dh_ref, wd_ref, g_ref, u_ref, dg_ref, du_ref):
        da = 0.5 * _dot_nt(dh_ref[...].astype(BF16), wd_ref[...])
        g = g_ref[...].astype(F32)
        u = u_ref[...].astype(F32)
        s = _sigmoid(g)
        dg_ref[...] = (da * u * (s * (1.0 + g * (1.0 - s)))).astype(BF16)
        du_ref[...] = (da * (g * s)).astype(BF16)

    aspec = _bs((None, tm, f), lambda i, k: (k, i, 0))
    return _call(
        host, body, name=name, grid=(t // tm, nk),
        in_specs=[_bs((tm, d), lambda i, k: (i, 0)), _bs((None, None, f, d), lambda i, k: (k, layer, 0, 0)), aspec, aspec],
        out_specs=[aspec, aspec],
        out_shape=[S((nk, t, f), BF16), S((nk, t, f), BF16)],
        compiler_params=_cp("parallel", "arbitrary"),
        args=(dh, wd, gg, uu))


def nt_acc_normbwd(terms, nk, h, gain, dh, name, host=None):
    t, d = h.shape
    tm = _row_tile(t, 512)
    nterm = len(terms)

    def body(*refs):
        xs = refs[:2 * nterm]
        h_ref, g_ref, dh_ref, o_ref, dg_ref, acc = refs[2 * nterm:]
        i, k = pl.program_id(0), pl.program_id(1)

        @pl.when(k == 0)
        def _():
            acc[...] = jnp.zeros_like(acc)

        @pl.when(jnp.logical_and(i == 0, k == 0))
        def _():
            dg_ref[...] = jnp.zeros_like(dg_ref)
        tot = _dot_nt(xs[0][...], xs[1][...])
        for j in range(1, nterm):
            tot = tot + _dot_nt(xs[2 * j][...], xs[2 * j + 1][...])
        acc[...] += tot

        @pl.when(k == nk - 1)
        def _():
            dx, dgain = _rmsnorm_bwd(h_ref[...], g_ref[...], acc[...])
            o_ref[...] = dh_ref[...] + dx
            dg_ref[...] += dgain

    in_specs, args = [], []
    for x, xs_, w, ws_ in terms:
        in_specs += [xs_, ws_]
        args += [x, w]
    row = _bs((tm, d), lambda i, k: (i, 0))
    vec = _bs((1, d), lambda i, k: (0, 0))
    return _call(
        host, body, name=name, grid=(t // tm, nk),
        in_specs=in_specs + [row, vec, row],
        out_specs=[row, vec],
        out_shape=[S((t, d), F32), S((1, d), F32)],
        scratch_shapes=[pltpu.VMEM((tm, d), F32)],
        compiler_params=_cp("arbitrary", "arbitrary"),
        args=(*args, h, gain, dh))


def ffn_bwd_in(dg, du, wg, wu, layer, h, gain, dh, name, host=None):
    nk, t, f = dg.shape
    d = h.shape[1]
    tm = _row_tile(t, 512)
    aspec = _bs((None, tm, f), lambda i, k: (k, i, 0))
    wspec = _bs((None, None, d, f), lambda i, k: (k, layer, 0, 0))
    return nt_acc_normbwd([(dg, aspec, wg, wspec), (du, aspec, wu, wspec)], nk, h, gain, dh, name, host)


def tn_mm(x, x_spec, y, y_spec, nblk, t, ka, nb, out_shape, out_spec, scale, prev, name, host=None):
    tk = _row_tile(t, TN_ROWS)

    def body(*refs):
        if prev is None:
            x_ref, y_ref, o_ref, acc = refs
        else:
            x_ref, y_ref, _, o_ref, acc = refs
        j = pl.program_id(1)

        @pl.when(j == 0)
        def _():
            acc[...] = jnp.zeros_like(acc)
        acc[...] += _dot_tn(x_ref[...].astype(BF16), y_ref[...].astype(BF16))

        @pl.when(j == t // tk - 1)
        def _():
            o_ref[...] = (scale * acc[...]).astype(o_ref.dtype)

    in_specs = [x_spec(tk), y_spec(tk)]
    args = [x, y]
    aliases = {}
    if prev is not None:
        in_specs.append(pl.BlockSpec(memory_space=pl.ANY))
        args.append(prev)
        aliases = {2: 0}
    return _call(
        host, body, name=name, grid=(nblk, t // tk),
        in_specs=in_specs, out_specs=out_spec, out_shape=out_shape,
        scratch_shapes=[pltpu.VMEM((ka, nb), F32)],
        aliases=aliases,
        compiler_params=_cp("parallel", "arbitrary"),
        args=tuple(args))


def ffn_wgrads(which, hn, dh, aa, dg, du, layer, grads, run):
    nk, t, f = aa.shape
    d = hn.shape[1]
    hn_spec = lambda tk: _bs((tk, d), lambda k, j: (j, 0))
    a_spec = lambda tk: _bs((None, tk, f), lambda k, j: (k, j, 0))
    shape_gu, spec_gu = S((nk, 2, d, f), BF16), _bs((None, None, d, f), lambda k, j: (k, layer, 0, 0))
    shape_d, spec_d = S((nk, 2, f, d), BF16), _bs((None, None, f, d), lambda k, j: (k, layer, 0, 0))
    for suffix, x, xs, y, ys, ka, nb, shp, spec, scale in (
            ("gate", hn, hn_spec, dg, a_spec, d, f, shape_gu, spec_gu, 1.0),
            ("up", hn, hn_spec, du, a_spec, d, f, shape_gu, spec_gu, 1.0),
            ("down", aa, a_spec, dh, hn_spec, f, d, shape_d, spec_d, 0.5)):
        key = f"{which}_w_{suffix}"
        grads[key] = run(tn_mm, x, xs, y, ys, nk, t, ka, nb, shp, spec, scale, grads.get(key), name=f"{which}_gw_{layer}_{suffix}")


def norm_mm(h, gain, w, name, host=None):
    t, d = h.shape
    nb, _, bw = w.shape
    tm = _row_tile(t, 512)

    def body(h_ref, g_ref, w_ref, hn_ref, z_ref, hn_scr):
        @pl.when(pl.program_id(1) == 0)
        def _():
            x = h_ref[...]
            hn = (x * _rstd(x) * g_ref[...]).astype(BF16)
            hn_scr[...] = hn
            hn_ref[...] = hn
        z_ref[...] = _dot(hn_scr[...], w_ref[...])

    return _call(
        host, body, name=name, grid=(t // tm, nb),
        in_specs=[_bs((tm, d), lambda i, k: (i, 0)), _bs((1, d), lambda i, k: (0, 0)), _bs((None, d, bw), lambda i, k: (k, 0, 0))],
        out_specs=[_bs((tm, d), lambda i, k: (i, 0)), _bs((tm, bw), lambda i, k: (i, k))],
        out_shape=[S((t, d), BF16), S((t, nb * bw), F32)],
        scratch_shapes=[pltpu.VMEM((tm, d), BF16)],
        compiler_params=_cp("parallel", "arbitrary"),
        args=(h, gain, w))


def nt_mm(a, w, name):
    t, k = a.shape
    n = w.shape[0]
    tm = _row_tile(t, 512)

    def body(a_ref, w_ref, o_ref):
        o_ref[...] = _dot_nt(a_ref[...].astype(BF16), w_ref[...])

    return pl.pallas_call(
        body, name=name, grid=(t // tm,),
        in_specs=[_bs((tm, k), lambda i: (i, 0)), _bs((n, k), lambda i: (0, 0))],
        out_specs=_bs((tm, n), lambda i: (i, 0)),
        out_shape=S((t, n), F32),
        compiler_params=_cp("parallel"),
    )(a, w)


def _head_mean_matrix():
    m = np.kron(np.eye(N_HEADS, dtype=np.float32), np.full((HEAD_DIM, HEAD_DIM), 1.0 / HEAD_DIM, np.float32))
    return jnp.asarray(m, BF16)


def _head_sum_matrix():
    m = np.kron(np.eye(N_HEADS, dtype=np.float32), np.ones((HEAD_DIM, HEAD_DIM), np.float32))
    return jnp.asarray(m, BF16)


def _rel_bucket_np(dist):
    max_exact = REL_BUCKETS // 2
    n = np.maximum(dist, 1).astype(np.float32)
    large = max_exact + (np.log(n / np.float32(max_exact)) / np.float32(math.log(REL_MAX_DIST / max_exact))
                         * np.float32(REL_BUCKETS - max_exact)).astype(np.int32)
    large = np.minimum(large, REL_BUCKETS - 1)
    return np.where(dist < max_exact, dist, large)


def _band_tables():
    qi = np.arange(BAND)[:, None]
    kj = np.arange(2 * BAND)[None, :]
    dist_q = qi + BAND - kj
    qq = np.arange(2 * BAND)[:, None]
    kk = np.arange(BAND)[None, :]
    dist_k = qq - kk
    out = []
    for dist in (dist_q, dist_k):
        valid = (dist >= 0) & (dist <= BAND)
        bucket = np.stack([_rel_bucket_np(np.clip(dist, 0, BAND) * d) for d in DILATIONS])
        out.append((bucket, valid))
    return out


def band_bias(rel_bias):
    out = []
    for bucket, valid in _band_tables():
        bucket = np.where(valid[None], bucket, -1)[:, None]
        tab = jnp.full((len(DILATIONS), N_HEADS) + bucket.shape[2:], NEG, F32)
        for b in range(REL_BUCKETS):
            if (bucket == b).any():
                tab = jnp.where(bucket == b, rel_bias[b][None, :, None, None], tab)
        out.append(tab)
    return out


def hyb_prep(z, q_gain, k_gain, name):
    t = z.shape[0]
    tm = _row_tile(t, 512)
    seg = _head_mean_matrix()

    def body(q_ref, k_ref, v_ref, qg_ref, kg_ref, seg_ref, qo_ref, ko_ref, vo_ref):
        q = q_ref[...]
        k = k_ref[...]
        qo_ref[...] = (q * lax.rsqrt(_seg_dot(q * q, seg_ref[...]) + EPS) * qg_ref[...]).astype(BF16)
        ko_ref[...] = (k * lax.rsqrt(_seg_dot(k * k, seg_ref[...]) + EPS) * kg_ref[...]).astype(BF16)
        vo_ref[...] = v_ref[...].astype(BF16)

    col = lambda c: _bs((tm, ATTN_W), lambda i: (i, c))
    vec = _bs((1, ATTN_W), lambda i: (0, 0))
    out = _bs((tm, ATTN_W), lambda i: (i, 0))
    return pl.pallas_call(
        body, name=name, grid=(t // tm,),
        in_specs=[col(3), col(4), col(5), vec, vec, _bs((ATTN_W, ATTN_W), lambda i: (0, 0))],
        out_specs=[out, out, out],
        out_shape=[S((t, ATTN_W), BF16)] * 3,
        compiler_params=_cp("parallel"),
    )(z, z, z, q_gain, k_gain, seg)


def _lane_lo(shape):
    return lax.broadcasted_iota(jnp.int32, shape, 1) < HEAD_DIM


def attn_fwd(q, k, v, bias, dil, name):
    t = q.shape[0]
    sub = t // dil
    nb = sub // BAND
    qv, kv, vv = (a.reshape(sub, dil * ATTN_W) for a in (q, k, v))

    def body(q_ref, kp_ref, kc_ref, vp_ref, vc_ref, b_ref, o_ref, l_ref):
        n = pl.program_id(1)
        first = n == 0
        colk = lax.broadcasted_iota(jnp.int32, (BAND, 2 * BAND), 1)
        for j in range(N_HEADS // 2):
            sl = slice(2 * HEAD_DIM * j, 2 * HEAD_DIM * (j + 1))
            qp = q_ref[:, sl]
            kk = jnp.concatenate([kp_ref[:, sl], kc_ref[:, sl]], axis=0)
            vv_ = jnp.concatenate([vp_ref[:, sl], vc_ref[:, sl]], axis=0)
            lo = _lane_lo(qp.shape)
            outs, lses = [], []
            for hh in range(2):
                qm = jnp.where(lo if hh == 0 else jnp.logical_not(lo), qp, jnp.zeros_like(qp))
                s = _dot_nt(qm, kk) * (HEAD_DIM ** -0.5) + b_ref[2 * j + hh]
                s = jnp.where(jnp.logical_and(first, colk < BAND), NEG, s)
                m = jnp.max(s, axis=-1, keepdims=True)
                p = jnp.exp(s - m)
                l = jnp.sum(p, axis=-1, keepdims=True)
                outs.append(_dot(p.astype(BF16), vv_) / l)
                lses.append(m + jnp.log(l))
            o_ref[:, sl] = jnp.where(lo, outs[0], outs[1])
            l_ref[:, sl] = jnp.where(lo, lses[0], lses[1])

    cur = _bs((BAND, ATTN_W), lambda r, n: (n, r))
    prv = _bs((BAND, ATTN_W), lambda r, n: (jnp.maximum(n - 1, 0), r))
    o, lse = pl.pallas_call(
        body, name=name, grid=(dil, nb),
        in_specs=[cur, prv, cur, prv, cur, _bs((N_HEADS, BAND, 2 * BAND), lambda r, n: (0, 0, 0))],
        out_specs=[cur, cur],
        out_shape=[S((sub, dil * ATTN_W), F32)] * 2,
        compiler_params=_cp("parallel", "arbitrary"),
    )(qv, kv, kv, vv, vv, bias)
    return o.reshape(t, ATTN_W), lse.reshape(t, ATTN_W)


def hyb_post(z, conv_w, os_, lses, name):
    t = z.shape[0]
    tm = _row_tile(t, 512)

    def body(gb_ref, gc_ref, cx_ref, gch_ref, cxh_ref, w_ref, o1, o2, o3, l1, l2, l3, y_ref, lt_ref, ya_ref):
        i = pl.program_id(0)
        m = gc_ref[...] * cx_ref[...]
        mh = jnp.where(i == 0, 0.0, gch_ref[...] * cxh_ref[...])
        conv = w_ref[0:1, :] * _shift_down(m, mh, 2) + w_ref[1:2, :] * _shift_down(m, mh, 1) + w_ref[2:3, :] * m
        y_ref[0] = (gb_ref[...] * conv).astype(BF16)
        la, lb, lc = l1[...], l2[...], l3[...]
        mx = jnp.maximum(jnp.maximum(la, lb), lc)
        ea, eb, ec = jnp.exp(la - mx), jnp.exp(lb - mx), jnp.exp(lc - mx)
        den = ea + eb + ec
        ya = (ea * o1[...] + eb * o2[...] + ec * o3[...]) / den
        y_ref[1] = ya.astype(BF16)
        ya_ref[...] = ya
        lt_ref[...] = mx + jnp.log(den)

    hb = tm // 8
    col = lambda c: _bs((tm, CONV_W), lambda i: (i, c))
    halo = lambda c: _bs((8, CONV_W), lambda i: (jnp.maximum(i * hb - 1, 0), c))
    row = _bs((tm, ATTN_W), lambda i: (i, 0))
    return pl.pallas_call(
        body, name=name, grid=(t // tm,),
        in_specs=[col(0), col(1), col(2), halo(1), halo(2), _bs((3, CONV_W), lambda i: (0, 0))] + [row] * 6,
        out_specs=[_bs((2, tm, ATTN_W), lambda i: (0, i, 0)), row, row],
        out_shape=[S((2, t, ATTN_W), BF16), S((t, ATTN_W), F32), S((t, ATTN_W), F32)],
        compiler_params=_cp("parallel"),
    )(z, z, z, z, z, conv_w, *os_, *lses)


def attn_delta(dy, ya, name):
    t = ya.shape[0]
    tm = _row_tile(t, 512)
    seg = _head_sum_matrix()

    def body(dy_ref, ya_ref, seg_ref, dl_ref, db_ref):
        dya = dy_ref[...]
        dl_ref[...] = _seg_dot(dya * ya_ref[...], seg_ref[...])
        db_ref[...] = dya.astype(BF16)

    row = _bs((tm, ATTN_W), lambda i: (i, 0))
    return pl.pallas_call(
        body, name=name, grid=(t // tm,),
        in_specs=[_bs((tm, ATTN_W), lambda i: (i, 1)), row, _bs((ATTN_W, ATTN_W), lambda i: (0, 0))],
        out_specs=[row, row],
        out_shape=[S((t, ATTN_W), F32), S((t, ATTN_W), BF16)],
        compiler_params=_cp("parallel"),
    )(dy, ya, seg)


def attn_bwd_dq(q, k, v, dya, lt, delta, bias, dil, name):
    t = q.shape[0]
    sub = t // dil
    nb = sub // BAND
    qv, kv, vv, dv_, lv, ev = (a.reshape(sub, dil * ATTN_W) for a in (q, k, v, dya, lt, delta))

    def body(q_ref, kp_ref, kc_ref, vp_ref, vc_ref, do_ref, l_ref, e_ref, b_ref, dq_ref, db_ref):
        r, n = pl.program_id(0), pl.program_id(1)

        @pl.when(jnp.logical_and(r == 0, n == 0))
        def _():
            db_ref[...] = jnp.zeros_like(db_ref)
        first = n == 0
        colk = lax.broadcasted_iota(jnp.int32, (BAND, 2 * BAND), 1)
        for j in range(N_HEADS // 2):
            sl = slice(2 * HEAD_DIM * j, 2 * HEAD_DIM * (j + 1))
            qp, dop = q_ref[:, sl], do_ref[:, sl]
            kk = jnp.concatenate([kp_ref[:, sl], kc_ref[:, sl]], axis=0)
            vv_ = jnp.concatenate([vp_ref[:, sl], vc_ref[:, sl]], axis=0)
            lo = _lane_lo(qp.shape)
            dqs = []
            for hh in range(2):
                msk = lo if hh == 0 else jnp.logical_not(lo)
                c0 = 2 * HEAD_DIM * j + HEAD_DIM * hh
                qm = jnp.where(msk, qp, jnp.zeros_like(qp))
                dom = jnp.where(msk, dop, jnp.zeros_like(dop))
                s = _dot_nt(qm, kk) * (HEAD_DIM ** -0.5) + b_ref[2 * j + hh]
                s = jnp.where(jnp.logical_and(first, colk < BAND), NEG, s)
                p = jnp.exp(s - l_ref[:, c0:c0 + 1])
                ds = p * (_dot_nt(dom, vv_) - e_ref[:, c0:c0 + 1])
                db_ref[2 * j + hh] += ds
                dqs.append(_dot(ds.astype(BF16), kk) * (HEAD_DIM ** -0.5))
            dq_ref[:, sl] = jnp.where(lo, dqs[0], dqs[1])

    cur = _bs((BAND, ATTN_W), lambda r, n: (n, r))
    prv = _bs((BAND, ATTN_W), lambda r, n: (jnp.maximum(n - 1, 0), r))
    tab = _bs((N_HEADS, BAND, 2 * BAND), lambda r, n: (0, 0, 0))
    dq, db = pl.pallas_call(
        body, name=name, grid=(dil, nb),
        in_specs=[cur, prv, cur, prv, cur, cur, cur, cur, tab],
        out_specs=[cur, tab],
        out_shape=[S((sub, dil * ATTN_W), F32), S((N_HEADS, BAND, 2 * BAND), F32)],
        compiler_params=_cp("arbitrary", "arbitrary"),
    )(qv, kv, kv, vv, vv, dv_, lv, ev, bias)
    return dq.reshape(t, ATTN_W), db


def attn_bwd_dkv(q, k, v, dya, lt, delta, bias_k, dil, name):
    t = q.shape[0]
    sub = t // dil
    nb = sub // BAND
    qv, kv, vv, dv_, lv, ev = (a.reshape(sub, dil * ATTN_W) for a in (q, k, v, dya, lt, delta))

    def body(k_ref, v_ref, qc_ref, qn_ref, dc_ref, dn_ref, lc_ref, ln_ref, ec_ref, en_ref, b_ref, dk_ref, dv_ref):
        n = pl.program_id(1)
        last = n == nb - 1
        rowq = lax.broadcasted_iota(jnp.int32, (2 * BAND, BAND), 0)
        for j in range(N_HEADS // 2):
            sl = slice(2 * HEAD_DIM * j, 2 * HEAD_DIM * (j + 1))
            kp, vp = k_ref[:, sl], v_ref[:, sl]
            qq = jnp.concatenate([qc_ref[:, sl], qn_ref[:, sl]], axis=0)
            do = jnp.concatenate([dc_ref[:, sl], dn_ref[:, sl]], axis=0)
            lo = _lane_lo(qq.shape)
            dks, dvs = [], []
            for hh in range(2):
                msk = lo if hh == 0 else jnp.logical_not(lo)
                c0 = 2 * HEAD_DIM * j + HEAD_DIM * hh
                qm = jnp.where(msk, qq, jnp.zeros_like(qq))
                dom = jnp.where(msk, do, jnp.zeros_like(do))
                ll = jnp.concatenate([lc_ref[:, c0:c0 + 1], ln_ref[:, c0:c0 + 1]], axis=0)
                ee = jnp.concatenate([ec_ref[:, c0:c0 + 1], en_ref[:, c0:c0 + 1]], axis=0)
                s = _dot_nt(qm, kp) * (HEAD_DIM ** -0.5) + b_ref[2 * j + hh]
                s = jnp.where(jnp.logical_and(last, rowq >= BAND), NEG, s)
                p = jnp.exp(s - ll)
                ds = p * (_dot_nt(dom, vp) - ee)
                dvs.append(_dot_tn(p.astype(BF16), dom))
                dks.append(_dot_tn(ds.astype(BF16), qm) * (HEAD_DIM ** -0.5))
            lo_k = _lane_lo(kp.shape)
            dk_ref[:, sl] = jnp.where(lo_k, dks[0], dks[1])
            dv_ref[:, sl] = jnp.where(lo_k, dvs[0], dvs[1])

    cur = _bs((BAND, ATTN_W), lambda r, n: (n, r))
    nxt = _bs((BAND, ATTN_W), lambda r, n: (jnp.minimum(n + 1, nb - 1), r))
    tab = _bs((N_HEADS, 2 * BAND, BAND), lambda r, n: (0, 0, 0))
    dk, dv = pl.pallas_call(
        body, name=name, grid=(dil, nb),
        in_specs=[cur, cur, cur, nxt, cur, nxt, cur, nxt, cur, nxt, tab],
        out_specs=[cur, cur],
        out_shape=[S((sub, dil * ATTN_W), F32)] * 2,
        compiler_params=_cp("parallel", "arbitrary"),
    )(kv, vv, qv, qv, dv_, dv_, lv, lv, ev, ev, bias_k)
    return dk.reshape(t, ATTN_W), dv.reshape(t, ATTN_W)


def hyb_dz(z, dy, conv_w, q_gain, k_gain, dqs, dks, dvs, name):
    t = z.shape[0]
    tm = _row_tile(t, 256)
    nt = t // tm
    seg = _head_mean_matrix()

    def body(gb_ref, gc_ref, cx_ref, q_ref, k_ref, gch_ref, cxh_ref, gbn_ref, dyc_ref, dyn_ref, w_ref, qg_ref, kg_ref, seg_ref,
             dq1, dq2, dq3, dk1, dk2, dk3, dv1, dv2, dv3, dz_ref, dw_ref, dqg_ref, dkg_ref):
        i = pl.program_id(0)

        @pl.when(i == 0)
        def _():
            dw_ref[...] = jnp.zeros_like(dw_ref)
            dqg_ref[...] = jnp.zeros_like(dqg_ref)
            dkg_ref[...] = jnp.zeros_like(dkg_ref)
        gb, gc, cx, dyc = gb_ref[...], gc_ref[...], cx_ref[...], dyc_ref[...]
        m = gc * cx
        mh = jnp.where(i == 0, 0.0, gch_ref[...] * cxh_ref[...])
        m1, m2 = _shift_down(m, mh, 1), _shift_down(m, mh, 2)
        conv = w_ref[0:1, :] * m2 + w_ref[1:2, :] * m1 + w_ref[2:3, :] * m
        dconv = dyc * gb
        dcn = jnp.where(i == nt - 1, 0.0, dyn_ref[...] * gbn_ref[...])
        dm = w_ref[2:3, :] * dconv + w_ref[1:2, :] * _shift_up(dconv, dcn, 1) + w_ref[0:1, :] * _shift_up(dconv, dcn, 2)
        dz_ref[:, 0:CONV_W] = (dyc * conv).astype(BF16)
        dz_ref[:, CONV_W:2 * CONV_W] = (dm * cx).astype(BF16)
        dz_ref[:, 2 * CONV_W:3 * CONV_W] = (dm * gc).astype(BF16)
        dw_ref[0:1, :] += jnp.sum(dconv * m2, axis=0, keepdims=True)
        dw_ref[1:2, :] += jnp.sum(dconv * m1, axis=0, keepdims=True)
        dw_ref[2:3, :] += jnp.sum(dconv * m, axis=0, keepdims=True)
        base = 3 * CONV_W
        for idx, (x_ref, g_ref, parts, dgain_ref) in enumerate(((q_ref, qg_ref, (dq1, dq2, dq3), dqg_ref),
                                                                  (k_ref, kg_ref, (dk1, dk2, dk3), dkg_ref))):
            x = x_ref[...]
            dxh = parts[0][...] + parts[1][...] + parts[2][...]
            r = lax.rsqrt(_seg_dot(x * x, seg_ref[...]) + EPS)
            xhat = x * r
            tt = dxh * g_ref[...]
            dx = r * (tt - xhat * _seg_dot(tt * xhat, seg_ref[...]))
            dz_ref[:, base + idx * ATTN_W:base + (idx + 1) * ATTN_W] = dx.astype(BF16)
            dgain_ref[...] += jnp.sum(dxh * xhat, axis=0, keepdims=True)
        dz_ref[:, base + 2 * ATTN_W:base + 3 * ATTN_W] = (dv1[...] + dv2[...] + dv3[...]).astype(BF16)

    hb = tm // 8
    col = lambda c: _bs((tm, CONV_W), lambda i: (i, c))
    prev = lambda c: _bs((8, CONV_W), lambda i: (jnp.maximum(i * hb - 1, 0), c))
    nxt = lambda c: _bs((8, CONV_W), lambda i: (jnp.minimum((i + 1) * hb, t // 8 - 1), c))
    row = _bs((tm, ATTN_W), lambda i: (i, 0))
    vec = _bs((1, ATTN_W), lambda i: (0, 0))
    return pl.pallas_call(
        body, name=name, grid=(nt,),
        in_specs=[col(0), col(1), col(2), col(3), col(4), prev(1), prev(2), nxt(0), col(0), nxt(0),
                  _bs((3, CONV_W), lambda i: (0, 0)), vec, vec, _bs((ATTN_W, ATTN_W), lambda i: (0, 0))] + [row] * 9,
        out_specs=[_bs((tm, 6 * CONV_W), lambda i: (i, 0)), _bs((3, CONV_W), lambda i: (0, 0)), vec, vec],
        out_shape=[S((t, 6 * CONV_W), BF16), S((3, CONV_W), F32), S((1, ATTN_W), F32), S((1, ATTN_W), F32)],
        compiler_params=_cp("arbitrary"),
    )(z, z, z, z, z, z, z, z, dy, dy, conv_w, q_gain, k_gain, seg, *dqs, *dks, *dvs)


def rel_bias_grad(dbs, name):
    (bq, vq), _ = _band_tables()
    onehot = np.zeros((len(DILATIONS), REL_BUCKETS, BAND * 2 * BAND), np.float32)
    for g in range(len(DILATIONS)):
        idx = bq[g].reshape(-1)
        ok = vq.reshape(-1)
        onehot[g, idx[ok], np.nonzero(ok)[0]] = 1.0
    onehot = jnp.asarray(onehot, BF16)
    flat = [d.reshape(N_HEADS, BAND * 2 * BAND) for d in dbs]

    def body(oh_ref, d1, d2, d3, o_ref):
        acc = jnp.zeros((REL_BUCKETS, N_HEADS), F32)
        for g, d in enumerate((d1, d2, d3)):
            x = d[...]
            hi = x.astype(BF16)
            lo = (x - hi.astype(F32)).astype(BF16)
            acc += _dot_nt(oh_ref[g], hi) + _dot_nt(oh_ref[g], lo)
        o_ref[...] = acc

    full = lambda shp: _bs(shp, lambda: tuple(0 for _ in shp))
    return pl.pallas_call(
        body, name=name,
        in_specs=[full(onehot.shape)] + [full(flat[0].shape)] * 3,
        out_specs=full((REL_BUCKETS, N_HEADS)),
        out_shape=S((REL_BUCKETS, N_HEADS), F32),
        compiler_params=pltpu.CompilerParams(vmem_limit_bytes=VMEM_LIMIT),
    )(onehot, *flat)


def _lru_gates(xb, wa_ref, wx_ref, ba, bx):
    xb16 = xb.astype(BF16)
    ga = jnp.concatenate([_dot(xb16[:, LRU_BLOCK * g:LRU_BLOCK * (g + 1)], wa_ref[g]) for g in range(LRU_BLOCKS)], axis=1) + ba
    gx = jnp.concatenate([_dot(xb16[:, LRU_BLOCK * g:LRU_BLOCK * (g + 1)], wx_ref[g]) for g in range(LRU_BLOCKS)], axis=1) + bx
    return ga, gx


def _lru_coeffs(ga, gx, lam):
    sga = _sigmoid(ga)
    sp = _softplus(-lam)
    log_a = -LRU_C * sga * sp
    a = jnp.exp(log_a)
    one_m_a2 = _neg_expm1(2.0 * log_a)
    return sga, sp, a, one_m_a2, jnp.sqrt(one_m_a2), _sigmoid(gx)


def rec_fwd(z, conv_w, conv_b, wa, wx, ba, bx, lam, name):
    t = z.shape[0]
    w = z.shape[1] // 2
    tm = _row_tile(t, 256)

    def body(xp_ref, xh_ref, yb_ref, cw_ref, cb_ref, wa_ref, wx_ref, ba_ref, bx_ref, lam_ref,
             xb_ref, ga_ref, gx_ref, hs_ref, out_ref, carry):
        i = pl.program_id(0)

        @pl.when(i == 0)
        def _():
            carry[...] = jnp.zeros_like(carry)
        xp = xp_ref[...]
        xh = jnp.where(i == 0, 0.0, xh_ref[...])
        xb = cb_ref[...] + cw_ref[3:4, :] * xp
        for j in range(3):
            xb = xb + cw_ref[j:j + 1, :] * _shift_down(xp, xh, 3 - j)
        ga, gx = _lru_gates(xb, wa_ref, wx_ref, ba_ref[...], bx_ref[...])
        _, _, a, _, sq, sgx = _lru_coeffs(ga, gx, lam_ref[...])
        aa, bb = a, sq * sgx * xb
        s = 1
        while s < tm:
            bb = aa * _roll_fill(bb, s, 0.0, False) + bb
            aa = aa * _roll_fill(aa, s, 1.0, False)
            s *= 2
        hs = aa * carry[0:1, :] + bb
        xb_ref[...] = xb
        ga_ref[...] = ga
        gx_ref[...] = gx
        hs_ref[...] = hs
        carry[0:1, :] = hs_ref[tm - 1:tm, :]
        gy, _ = _gelu_and_grad(yb_ref[...])
        out_ref[...] = (hs * gy).astype(BF16)

    hb = tm // 8
    row = _bs((tm, w), lambda i: (i, 0))
    vec = _bs((1, w), lambda i: (0, 0))
    wsp = _bs((LRU_BLOCKS, LRU_BLOCK, LRU_BLOCK), lambda i: (0, 0, 0))
    return pl.pallas_call(
        body, name=name, grid=(t // tm,),
        in_specs=[row, _bs((8, w), lambda i: (jnp.maximum(i * hb - 1, 0), 0)), _bs((tm, w), lambda i: (i, 1)),
                  _bs((4, w), lambda i: (0, 0)), vec, wsp, wsp, vec, vec, vec],
        out_specs=[row] * 5,
        out_shape=[S((t, w), F32)] * 4 + [S((t, w), BF16)],
        scratch_shapes=[pltpu.VMEM((8, w), F32)],
        compiler_params=_cp("arbitrary"),
    )(z, z, z, conv_w, conv_b, wa, wx, ba, bx, lam)


def rec_bwd(d_out, z, xb, ga, gx, hs, conv_w, wa, wx, lam, name):
    t = z.shape[0]
    w = z.shape[1] // 2
    tm = _row_tile(t, 256)
    nt = t // tm

    def body(do_ref, xp_ref, xph_ref, yb_ref, xb_ref, ga_ref, gx_ref, hs_ref, hsh_ref, cw_ref, wa_ref, wx_ref, lam_ref,
             dz_ref, dga_ref, dgx_ref, sm_ref, c_lam, c_a, c_dxb):
        i = pl.program_id(0)

        @pl.when(i == 0)
        def _():
            sm_ref[...] = jnp.zeros_like(sm_ref)
            c_lam[...] = jnp.zeros_like(c_lam)
            c_a[...] = jnp.zeros_like(c_a)
            c_dxb[...] = jnp.zeros_like(c_dxb)
        d_o, yb, xb, hs = do_ref[...], yb_ref[...], xb_ref[...], hs_ref[...]
        lam = lam_ref[...]
        gy, dgy = _gelu_and_grad(yb)
        dz_ref[:, w:2 * w] = (d_o * hs * dgy).astype(BF16)
        sga, sp, a, one_m_a2, sq, sgx = _lru_coeffs(ga_ref[...], gx_ref[...], lam)
        aa = _shift_up(a, c_a[...], 1)
        bb = d_o * gy
        s = 1
        while s < tm:
            bb = aa * _roll_fill(bb, s, 0.0, True) + bb
            aa = aa * _roll_fill(aa, s, 1.0, True)
            s *= 2
        lmb = aa * c_lam[0:1, :] + bb
        c_a[...] = a[0:8, :]
        c_lam[...] = lmb[0:8, :]
        hprev = _shift_down(hs, jnp.where(i == nt - 1, 0.0, hsh_ref[...]), 1)
        d_sq = lmb * sgx * xb
        d_sgx = lmb * sq * xb
        d_log_a = lmb * hprev * a - d_sq * (1.0 - one_m_a2) / sq
        dga = d_log_a * (-LRU_C * sp) * sga * (1.0 - sga)
        dgx = d_sgx * sgx * (1.0 - sgx)
        dga16, dgx16 = dga.astype(BF16), dgx.astype(BF16)
        dga_ref[...] = dga16
        dgx_ref[...] = dgx16
        dxb = lmb * sq * sgx + jnp.concatenate(
            [_dot_nt(dga16[:, LRU_BLOCK * g:LRU_BLOCK * (g + 1)], wa_ref[g]) + _dot_nt(dgx16[:, LRU_BLOCK * g:LRU_BLOCK * (g + 1)], wx_ref[g])
             for g in range(LRU_BLOCKS)], axis=1)
        nxt = c_dxb[...]
        dxp = cw_ref[3:4, :] * dxb
        for j in range(3):
            dxp = dxp + cw_ref[j:j + 1, :] * _shift_up(dxb, nxt, 3 - j)
        c_dxb[...] = dxb[0:8, :]
        dz_ref[:, 0:w] = dxp.astype(BF16)
        xp = xp_ref[...]
        xph = jnp.where(i == nt - 1, 0.0, xph_ref[...])
        sm_ref[0:1, :] += jnp.sum(dga, axis=0, keepdims=True)
        sm_ref[1:2, :] += jnp.sum(dgx, axis=0, keepdims=True)
        sm_ref[2:3, :] += jnp.sum(d_log_a * (-LRU_C * sga), axis=0, keepdims=True) * (-_sigmoid(-lam))
        sm_ref[3:4, :] += jnp.sum(dxb, axis=0, keepdims=True)
        for j in range(4):
            sm_ref[4 + j:5 + j, :] += jnp.sum(dxb * _shift_down(xp, xph, 3 - j), axis=0, keepdims=True)

    hb = tm // 8
    rev = lambda c: _bs((tm, w), lambda i: (nt - 1 - i, c))
    halo = lambda c: _bs((8, w), lambda i: (jnp.maximum((nt - 1 - i) * hb - 1, 0), c))
    vec = _bs((1, w), lambda i: (0, 0))
    wsp = _bs((LRU_BLOCKS, LRU_BLOCK, LRU_BLOCK), lambda i: (0, 0, 0))
    return pl.pallas_call(
        body, name=name, grid=(nt,),
        in_specs=[rev(0), rev(0), halo(0), rev(1), rev(0), rev(0), rev(0), rev(0), halo(0),
                  _bs((4, w), lambda i: (0, 0)), wsp, wsp, vec],
        out_specs=[_bs((tm, 2 * w), lambda i: (nt - 1 - i, 0)), rev(0), rev(0), _bs((8, w), lambda i: (0, 0))],
        out_shape=[S((t, 2 * w), BF16), S((t, w), BF16), S((t, w), BF16), S((8, w), F32)],
        scratch_shapes=[pltpu.VMEM((8, w), F32)] * 3,
        compiler_params=_cp("arbitrary"),
    )(d_out, z, z, z, xb, ga, gx, hs, hs, conv_w, wa, wx, lam)


def ple_fwd(h, gain, wpg, layer, p, wpp, name):
    t, d = h.shape
    pd = p.shape[1]
    nk, _, rb, _ = wpg.shape
    cb = wpp.shape[3]
    tm = _row_tile(t, 512)

    def body(h_ref, g_ref, wg_ref, p_ref, wp_ref, o_ref, hn_ref, gp_ref, pp_ref):
        x = h_ref[...]
        hn = (x * _rstd(x) * g_ref[...]).astype(BF16)
        gp = _dot(hn[:, 0:rb], wg_ref[0])
        for k in range(1, nk):
            gp = gp + _dot(hn[:, rb * k:rb * (k + 1)], wg_ref[k])
        p16 = p_ref[...].astype(BF16)
        pp = jnp.concatenate([_dot(p16, wp_ref[k]) for k in range(nk)], axis=1)
        hn_ref[...] = hn
        gp_ref[...] = gp
        pp_ref[...] = pp
        o_ref[...] = x + _sigmoid(gp) * pp

    row = _bs((tm, d), lambda i: (i, 0))
    return pl.pallas_call(
        body, name=name, grid=(t // tm,),
        in_specs=[row, _bs((1, d), lambda i: (0, 0)), _bs((nk, None, rb, d), lambda i: (0, layer, 0, 0)),
                  _bs((tm, pd), lambda i: (i, 0)), _bs((nk, None, pd, cb), lambda i: (0, layer, 0, 0))],
        out_specs=[row] * 4,
        out_shape=[S((t, d), F32), S((t, d), BF16), S((t, d), F32), S((t, d), F32)],
        compiler_params=_cp("parallel"),
    )(h, gain, wpg, p, wpp)


def ple_bwd(dh, h, gain, wpg, layer, gp, pp, name):
    t, d = h.shape
    nk, _, rb, _ = wpg.shape
    tm = _row_tile(t, 512)

    def body(dh_ref, h_ref, g_ref, wg_ref, gp_ref, pp_ref, o_ref, dgp_ref, dpp_ref, dg_ref):
        @pl.when(pl.program_id(0) == 0)
        def _():
            dg_ref[...] = jnp.zeros_like(dg_ref)
        d_h = dh_ref[...]
        gate = _sigmoid(gp_ref[...])
        dgp = (d_h * pp_ref[...] * gate * (1.0 - gate)).astype(BF16)
        dgp_ref[...] = dgp
        dpp_ref[...] = (d_h * gate).astype(BF16)
        dhn = jnp.concatenate([_dot_nt(dgp, wg_ref[k]) for k in range(nk)], axis=1)
        dx, dgain = _rmsnorm_bwd(h_ref[...], g_ref[...], dhn)
        o_ref[...] = d_h + dx
        dg_ref[...] += dgain

    row = _bs((tm, d), lambda i: (i, 0))
    vec = _bs((1, d), lambda i: (0, 0))
    return pl.pallas_call(
        body, name=name, grid=(t // tm,),
        in_specs=[row, row, vec, _bs((nk, None, rb, d), lambda i: (0, layer, 0, 0)), row, row],
        out_specs=[row, row, row, vec],
        out_shape=[S((t, d), F32), S((t, d), BF16), S((t, d), BF16), S((1, d), F32)],
        compiler_params=_cp("arbitrary"),
    )(dh, h, gain, wpg, gp, pp)


def loss_and_grad(y, target, name):
    t, d = y.shape
    tm = _row_tile(t, 512)

    def body(y_ref, t_ref, l_ref, g_ref):
        @pl.when(pl.program_id(0) == 0)
        def _():
            l_ref[...] = jnp.zeros_like(l_ref)
        err = y_ref[...] - t_ref[...]
        g_ref[...] = err * (1.0 / d)
        l_ref[...] += jnp.sum(jnp.sum(err * err, axis=1, keepdims=True), axis=0, keepdims=True) * (0.5 / d)

    row = _bs((tm, d), lambda i: (i, 0))
    return pl.pallas_call(
        body, name=name, grid=(t // tm,),
        in_specs=[row, row],
        out_specs=[_bs((1, 1), lambda i: (0, 0)), row],
        out_shape=[S((1, 1), F32), S((t, d), F32)],
        compiler_params=_cp("arbitrary"),
    )(y, target)


def _vec(a, i):
    return a[i:i + 1]


def local_step(x, p, target, w, plan=None):
    t = x.shape[0]
    tm = _row_tile(t, 512)
    grads = {}
    if plan is not None:
        plan.grads = grads
    saved = []
    h = x
    bias_q, bias_k = band_bias(w["rel_bias"])
    qg = jnp.tile(w["hyb_q_gain"], (1, N_HEADS))
    kg = jnp.tile(w["hyb_k_gain"], (1, N_HEADS))

    def run(fn, *a, name):
        hst = plan.host(name) if plan is not None else None
        out = fn(*a, name, hst)
        if hst is not None:
            plan.done(hst)
        return out

    def lru_blocks(n):
        return jnp.transpose(w[n].reshape(N_SHARD, LRU_BLOCKS, 64, LRU_BLOCK), (1, 0, 2, 3)).reshape(LRU_BLOCKS, LRU_BLOCK, LRU_BLOCK)

    for i in range(2):
        s = {}
        s["h0"] = h
        s["hn1"], s["g1"], s["u1"], s["a1"] = run(ffn_up, h, _vec(w["ffn1_norm"], i), w[f"ffn1_w_gate/{i}"], w[f"ffn1_w_up/{i}"], 0, name=f"ffn1_up_{i}")
        h = run(ffn_down, s["a1"], w[f"ffn1_w_down/{i}"], 0, h, name=f"ffn1_down_{i}")
        s["h1"] = h
        if i == 0:
            w_hyb_in = w["hyb_w_in"].reshape(N_SHARD, D_MODEL, -1)
            w_hyb_out = w["hyb_w_out"].reshape(D_MODEL, D_MODEL)
            s["hnm"], s["z"] = run(norm_mm, h, _vec(w["mix_norm"], i), w_hyb_in, name="hyb_in")
            s["q"], s["k"], s["v"] = hyb_prep(s["z"], qg, kg, "hyb_prep")
            os_, lses = [], []
            for g, dil in enumerate(DILATIONS):
                o, l = attn_fwd(s["q"], s["k"], s["v"], bias_q[g], dil, f"attn_fwd_{dil}")
                os_.append(o)
                lses.append(l)
            s["y2"], s["lt"], s["ya"] = hyb_post(s["z"], w["hyb_conv_w"], os_, lses, "hyb_post")
            h = run(mm_acc, s["y2"], _bs((None, tm, ATTN_W), lambda r, k: (k, r, 0)),
                    w_hyb_out.reshape(2, ATTN_W, D_MODEL), _bs((None, ATTN_W, D_MODEL), lambda r, k: (k, 0, 0)),
                    h, 1.0, 2, t, D_MODEL, tm, name="hyb_out")
        else:
            w_rec_in = w["rec_w_in"].reshape(N_SHARD, D_MODEL, -1)
            s["hnm"], s["z"] = run(norm_mm, h, _vec(w["mix_norm"], i), w_rec_in, name="rec_in")
            w_rec_out = w["rec_w_out"].reshape(D_MODEL, D_MODEL)
            lru_wa, lru_wx = lru_blocks("lru_wa"), lru_blocks("lru_wx")
            s["xb"], s["ga"], s["gx"], s["hs"], s["ro"] = rec_fwd(
                s["z"], w["rec_conv_w"], w["rec_conv_b"], lru_wa, lru_wx, w["lru_ba"], w["lru_bx"], w["lru_lambda"], "rec_fwd")
            h = run(mm_acc, s["ro"], _bs((tm, D_MODEL), lambda r, k: (r, 0)), w_rec_out, _bs((D_MODEL, D_MODEL), lambda r, k: (0, 0)),
                    h, 1.0, 1, t, D_MODEL, tm, name="rec_out")
        s["h2"] = h
        s["hn2"], s["g2"], s["u2"], s["a2"] = run(ffn_up, h, _vec(w["ffn2_norm"], i), w[f"ffn2_w_gate/{i}"], w[f"ffn2_w_up/{i}"], 0, name=f"ffn2_up_{i}")
        h = run(ffn_down, s["a2"], w[f"ffn2_w_down/{i}"], 0, h, name=f"ffn2_down_{i}")
        s["h3"] = h
        h, s["hnp"], s["gp"], s["pp"] = ple_fwd(h, _vec(w["ple_norm"], i), w[f"ple_w_gate/{i}"], 0, p[i], w[f"ple_w_proj/{i}"], f"ple_fwd_{i}")
        saved.append(s)
    loss, dh = loss_and_grad(h, target, "loss")

    norm_g = {n: [None, None] for n in ("ffn1_norm", "mix_norm", "ffn2_norm", "ple_norm")}
    for i in (1, 0):
        s = saved[i]
        dh_out = dh
        dh, dgp, dpp, norm_g["ple_norm"][i] = ple_bwd(dh_out, s["h3"], _vec(w["ple_norm"], i), w[f"ple_w_gate/{i}"], 0, s["gp"], s["pp"], f"ple_bwd_{i}")
        grads["ple_w_gate"] = run(tn_mm, s["hnp"], lambda tk: _bs((tk, 256), lambda k, j: (j, k)), dgp, lambda tk: _bs((tk, D_MODEL), lambda k, j: (j, 0)),
                                  N_SHARD, t, 256, D_MODEL, S((N_SHARD, 2, 256, D_MODEL), BF16),
                                  _bs((None, None, 256, D_MODEL), lambda k, j, i=i: (k, i, 0, 0)), 1.0, grads.get("ple_w_gate"), name=f"ple_gw_gate_{i}")
        grads["ple_w_proj"] = run(tn_mm, p[i], lambda tk: _bs((tk, 256), lambda k, j: (j, 0)), dpp, lambda tk: _bs((tk, 256), lambda k, j: (j, k)),
                                  N_SHARD, t, 256, 256, S((N_SHARD, 2, 256, 256), BF16),
                                  _bs((None, None, 256, 256), lambda k, j, i=i: (k, i, 0, 0)), 1.0, grads.get("ple_w_proj"), name=f"ple_gw_proj_{i}")
        dh_out = dh
        dg, du = run(ffn_bwd_act, dh_out, w[f"ffn2_w_down/{i}"], 0, s["g2"], s["u2"], name=f"ffn2_bwd_act_{i}")
        ffn_wgrads("ffn2", s["hn2"], dh_out, s["a2"], dg, du, i, grads, run)
        dh, norm_g["ffn2_norm"][i] = run(ffn_bwd_in, dg, du, w[f"ffn2_w_gate/{i}"], w[f"ffn2_w_up/{i}"], 0, s["h2"], _vec(w["ffn2_norm"], i), dh_out,
                                         name=f"ffn2_bwd_in_{i}")
        dh_out = dh
        if i == 1:
            d_o = nt_mm(dh_out, w_rec_out, "rec_bwd_out")
            grads["rec_w_out"] = run(tn_mm, s["ro"], lambda tk: _bs((tk, 256), lambda k, j: (j, k)), dh_out, lambda tk: _bs((tk, D_MODEL), lambda k, j: (j, 0)),
                                     N_SHARD, t, 256, D_MODEL, S((N_SHARD, 256, D_MODEL), BF16), _bs((None, 256, D_MODEL), lambda k, j: (k, 0, 0)),
                                     1.0, None, name="rec_gw_out").reshape(N_SHARD, 1, 256, D_MODEL)
            dz, dga, dgx, small = rec_bwd(d_o, s["z"], s["xb"], s["ga"], s["gx"], s["hs"], w["rec_conv_w"], lru_wa, lru_wx, w["lru_lambda"], "rec_bwd")
            blk = lambda tk: _bs((tk, LRU_BLOCK), lambda k, j: (j, k))
            for nm, dgt in (("lru_wa", dga), ("lru_wx", dgx)):
                gw = run(tn_mm, s["xb"], blk, dgt, blk, LRU_BLOCKS, t, LRU_BLOCK, LRU_BLOCK, S((LRU_BLOCKS, LRU_BLOCK, LRU_BLOCK), BF16),
                         _bs((None, LRU_BLOCK, LRU_BLOCK), lambda k, j: (k, 0, 0)), 1.0, None, name="rec_gw_" + nm)
                grads[nm] = jnp.transpose(gw.reshape(LRU_BLOCKS, N_SHARD, 64, LRU_BLOCK), (1, 0, 2, 3)).reshape(N_SHARD, 1, LRU_BLOCKS, 64, LRU_BLOCK)
            grads["lru_ba"], grads["lru_bx"], grads["lru_lambda"], grads["rec_conv_b"] = (small[r:r + 1] for r in range(4))
            grads["rec_conv_w"] = small[4:8]
            nb_, bw = N_SHARD, 512
            w_in, nm_in = w_rec_in, "rec_w_in"
        else:
            dy = nt_mm(dh_out, w_hyb_out, "hyb_bwd_out")
            grads["hyb_w_out"] = run(tn_mm, s["y2"], lambda tk: _bs((None, tk, 256), lambda k, j: (k // 2, j, k % 2)), dh_out,
                                     lambda tk: _bs((tk, D_MODEL), lambda k, j: (j, 0)),
                                     N_SHARD, t, 256, D_MODEL, S((N_SHARD, 256, D_MODEL), BF16), _bs((None, 256, D_MODEL), lambda k, j: (k, 0, 0)),
                                     1.0, None, name="hyb_gw_out").reshape(N_SHARD, 1, 256, D_MODEL)
            delta, dya = attn_delta(dy, s["ya"], "attn_delta")
            dqs, dks, dvs, dbs = [], [], [], []
            for g, dil in enumerate(DILATIONS):
                dq, db = attn_bwd_dq(s["q"], s["k"], s["v"], dya, s["lt"], delta, bias_q[g], dil, f"attn_bwd_dq_{dil}")
                dk, dv = attn_bwd_dkv(s["q"], s["k"], s["v"], dya, s["lt"], delta, bias_k[g], dil, f"attn_bwd_dkv_{dil}")
                dqs.append(dq); dks.append(dk); dvs.append(dv); dbs.append(db)
            grads["rel_bias"] = rel_bias_grad(dbs, "rel_bias_grad")
            dz, grads["hyb_conv_w"], dqg, dkg = hyb_dz(s["z"], dy, w["hyb_conv_w"], qg, kg, dqs, dks, dvs, "hyb_dz")
            grads["hyb_q_gain"] = jnp.sum(dqg.reshape(N_HEADS, HEAD_DIM), axis=0, keepdims=True)
            grads["hyb_k_gain"] = jnp.sum(dkg.reshape(N_HEADS, HEAD_DIM), axis=0, keepdims=True)
            nb_, bw = N_SHARD, 768
            w_in, nm_in = w_hyb_in, "hyb_w_in"
        grads[nm_in] = run(tn_mm, s["hnm"], lambda tk: _bs((tk, D_MODEL), lambda k, j: (j, 0)), dz, lambda tk, bw=bw: _bs((tk, bw), lambda k, j: (j, k)),
                           nb_, t, D_MODEL, bw, S((nb_, D_MODEL, bw), BF16), _bs((None, D_MODEL, bw), lambda k, j: (k, 0, 0)),
                           1.0, None, name=f"mix_gw_in_{i}").reshape(nb_, 1, D_MODEL, bw)
        dh, norm_g["mix_norm"][i] = run(
            nt_acc_normbwd, [(dz, _bs((tm, bw), lambda r, k: (r, k)), w_in, _bs((None, D_MODEL, bw), lambda r, k: (k, 0, 0)))],
            nb_, s["h1"], _vec(w["mix_norm"], i), dh_out, name=f"mix_bwd_in_{i}")
        dh_out = dh
        dg, du = run(ffn_bwd_act, dh_out, w[f"ffn1_w_down/{i}"], 0, s["g1"], s["u1"], name=f"ffn1_bwd_act_{i}")
        ffn_wgrads("ffn1", s["hn1"], dh_out, s["a1"], dg, du, i, grads, run)
        dh, norm_g["ffn1_norm"][i] = run(ffn_bwd_in, dg, du, w[f"ffn1_w_gate/{i}"], w[f"ffn1_w_up/{i}"], 0, s["h0"], _vec(w["ffn1_norm"], i), dh_out,
                                         name=f"ffn1_bwd_in_{i}")
    for n, (g0, g1) in norm_g.items():
        grads[n] = jnp.concatenate([g0, g1], axis=0)
    return loss, dh, grads


def gather_weights(shards, name):
    n = len(shards)

    def body(*refs):
        ins, outs = refs[:n], refs[n:2 * n]
        send1, recv1, send2, recv2, lsem = refs[2 * n:]
        x, y, c, k, chips, kk = _place()
        sib = (x, y, 1 - c)

        def remote(src, dst, ssem, rsem, to):
            return pltpu.make_async_remote_copy(src_ref=src, dst_ref=dst, send_sem=ssem, recv_sem=rsem, device_id=to, device_id_type=MESH)

        local = [pltpu.make_async_copy(ins[a], outs[a].at[k], lsem.at[a]) for a in range(n)]
        for cp in local:
            cp.start()
        sends = []
        for a in range(n):
            for j, chip in enumerate(chips):
                cp = remote(ins[a].at[c], outs[a].at[k, c], send1.at[3 * a + j], recv1.at[3 * a + j], (*chip, c))
                cp.start()
                sends.append(cp)
        for a in range(n):
            for j, chip in enumerate(chips):
                remote(ins[a].at[c], outs[a].at[kk[j], c], send1.at[3 * a + j], recv1.at[3 * a + j], (*chip, c)).wait_recv()
                cp = remote(outs[a].at[kk[j], c], outs[a].at[kk[j], c], send2.at[3 * a + j], recv2.at[3 * a + j], sib)
                cp.start()
                sends.append(cp)
        for a in range(n):
            for j in range(3):
                remote(outs[a].at[kk[j], 1 - c], outs[a].at[kk[j], 1 - c], send2.at[3 * a + j], recv2.at[3 * a + j], sib).wait_recv()
        for cp in sends:
            cp.wait_send()
        for cp in local:
            cp.wait()

    return pl.pallas_call(
        body, name=name,
        in_specs=[_ANY] * n, out_specs=[_ANY] * n,
        out_shape=[S((N_SHARD,) + s.shape, s.dtype) for s in shards],
        scratch_shapes=[pltpu.SemaphoreType.DMA((3 * n,))] * 4 + [pltpu.SemaphoreType.DMA((n,))],
    )(*shards)


def exchange_cores(rs, name):
    n = len(rs)

    def body(*refs):
        outs = refs[n:2 * n]
        send, recv = refs[2 * n:]
        x, y, c = lax.axis_index("x"), lax.axis_index("y"), lax.axis_index("c")
        sends = []
        for a in range(n):
            for k in range(N_SHARD):
                slot = outs[a].at[2 * k + c]
                cp = _remote(slot, slot, send.at[N_SHARD * a + k], recv.at[N_SHARD * a + k], (x, y, 1 - c))
                cp.start()
                sends.append(cp)
        for a in range(n):
            for k in range(N_SHARD):
                slot = outs[a].at[2 * k + 1 - c]
                _remote(slot, slot, send.at[N_SHARD * a + k], recv.at[N_SHARD * a + k], (x, y, 1 - c)).wait_recv()
        for cp in sends:
            cp.wait_send()

    return pl.pallas_call(
        body, name=name,
        in_specs=[_ANY] * n, out_specs=[_ANY] * n,
        out_shape=[S(r.shape, r.dtype) for r in rs],
        input_output_aliases={a: a for a in range(n)},
        scratch_shapes=[pltpu.SemaphoreType.DMA((N_SHARD * n,))] * 2,
    )(*rs)


def allgather8(a, name):
    def body(a_ref, o_ref, send, recv, lsem):
        x, y, c = lax.axis_index("x"), lax.axis_index("y"), lax.axis_index("c")
        me = 4 * x + 2 * y + c
        local = pltpu.make_async_copy(a_ref, o_ref.at[me], lsem)
        local.start()
        cps = []
        for f in range(1, N_DEV):
            fx, fy, fc = (f >> 2) & 1, (f >> 1) & 1, f & 1
            peer = (1 - x if fx else x, 1 - y if fy else y, 1 - c if fc else c)
            cp = pltpu.make_async_remote_copy(src_ref=a_ref, dst_ref=o_ref.at[me], send_sem=send.at[f - 1], recv_sem=recv.at[f - 1],
                                              device_id=peer, device_id_type=MESH)
            cp.start()
            cps.append((cp, 4 * peer[0] + 2 * peer[1] + peer[2], f))
        for cp, pidx, f in cps:
            pltpu.make_async_remote_copy(src_ref=a_ref, dst_ref=o_ref.at[pidx], send_sem=send.at[f - 1], recv_sem=recv.at[f - 1],
                                         device_id=(x, y, c), device_id_type=MESH).wait_recv()
        for cp, _, _ in cps:
            cp.wait_send()
        local.wait()

    return pl.pallas_call(
        body, name=name, in_specs=[_ANY], out_specs=_ANY,
        out_shape=S((N_DEV,) + a.shape, a.dtype),
        scratch_shapes=[pltpu.SemaphoreType.DMA((N_DEV - 1,)), pltpu.SemaphoreType.DMA((N_DEV - 1,)), pltpu.SemaphoreType.DMA],
    )(a)


def sum8(a, name):
    _, r, c = a.shape

    def body(a_ref, o_ref):
        acc = a_ref[0]
        for j in range(1, N_DEV):
            acc = acc + a_ref[j]
        o_ref[...] = acc

    return pl.pallas_call(
        body, name=name, in_specs=[_bs((N_DEV, r, c), lambda: (0, 0, 0))], out_specs=_bs((r, c), lambda: (0, 0)),
        out_shape=S((r, c), F32),
    )(a)


def adamw(w, m, v, g, name):
    r, c = w.shape
    tr = _row_tile(r, 256)
    summed = g.ndim == 3

    def body(w_ref, m_ref, v_ref, g_ref, go_ref, d_ref, mo_ref, vo_ref):
        if summed:
            gr = g_ref[0].astype(F32)
            for j in range(1, N_DEV):
                gr = gr + g_ref[j].astype(F32)
        else:
            gr = g_ref[...]
        m_new = ADAM_B1 * m_ref[...] + (1.0 - ADAM_B1) * gr
        v_new = ADAM_B2 * v_ref[...] + (1.0 - ADAM_B2) * (gr * gr)
        m_hat = m_new / (1.0 - ADAM_B1 ** ADAM_STEP)
        v_hat = v_new / (1.0 - ADAM_B2 ** ADAM_STEP)
        go_ref[...] = gr
        d_ref[...] = -ADAM_LR * (m_hat / (jnp.sqrt(v_hat) + ADAM_EPS) + ADAM_WD * w_ref[...])
        mo_ref[...] = m_new
        vo_ref[...] = v_new

    row = _bs((tr, c), lambda i: (i, 0))
    gspec = _bs((N_DEV, tr, c), lambda i: (0, i, 0)) if summed else row
    return pl.pallas_call(
        body, name=name, grid=(r // tr,),
        in_specs=[row, row, row, gspec], out_specs=[row] * 4, out_shape=[S((r, c), F32)] * 4,
        compiler_params=_cp("parallel"),
    )(w, m, v, g)


WEIGHTS = ["rel_bias", "ffn1_norm", "ffn1_w_gate", "ffn1_w_up", "ffn1_w_down", "mix_norm", "hyb_w_in", "hyb_conv_w", "hyb_q_gain",
           "hyb_k_gain", "hyb_w_out", "rec_w_in", "rec_conv_w", "rec_conv_b", "lru_wa", "lru_ba", "lru_wx", "lru_bx", "lru_lambda",
           "rec_w_out", "ffn2_norm", "ffn2_w_gate", "ffn2_w_up", "ffn2_w_down", "ple_norm", "ple_w_gate", "ple_w_proj"]
BIG = ["ffn1_w_gate", "ffn1_w_up", "ffn1_w_down", "hyb_w_in", "hyb_w_out", "rec_w_in", "lru_wa", "lru_wx", "rec_w_out",
       "ffn2_w_gate", "ffn2_w_up", "ffn2_w_down", "ple_w_gate", "ple_w_proj"]
SMALL_SHARDED = ["hyb_conv_w", "rec_conv_w", "rec_conv_b", "lru_ba", "lru_bx", "lru_lambda"]
SMALL = ["rel_bias", "ffn1_norm", "mix_norm", "ffn2_norm", "ple_norm", "hyb_q_gain", "hyb_k_gain"] + SMALL_SHARDED
PACK_W = 1024
PER_LAYER = ["ffn1_w_gate", "ffn1_w_up", "ffn1_w_down", "ffn2_w_gate", "ffn2_w_up", "ffn2_w_down", "ple_w_gate", "ple_w_proj"]
FIRST = ["ffn1_w_gate/0", "ffn1_w_up/0"]
LAST = ["ffn1_w_down"]
GATHER_PLAN = {
    "ffn1_up_0": ["ffn1_w_down/0", "hyb_w_in"],
    "ffn1_down_0": ["hyb_w_out", "ple_w_gate/0", "ple_w_proj/0"],
    "hyb_in": ["ffn2_w_gate/0", "ffn2_w_up/0"],
    "hyb_out": ["ffn2_w_down/0"],
    "ffn2_up_0": ["ffn1_w_gate/1", "ffn1_w_up/1"],
    "ffn2_down_0": ["ffn1_w_down/1"],
    "ffn1_up_1": ["rec_w_in", "lru_wa", "lru_wx", "rec_w_out", "ffn2_w_gate/1"],
    "ffn1_down_1": ["ffn2_w_up/1"],
    "rec_in": ["ffn2_w_down/1", "ple_w_gate/1", "ple_w_proj/1"],
}
SCATTER_PLAN = {
    "ple_gw_proj_1": [("ple_w_gate", 1)],
    "ffn2_bwd_act_1": [("ple_w_proj", 1)],
    "ffn2_gw_1_up": [("ffn2_w_gate", 1)],
    "ffn2_gw_1_down": [("ffn2_w_up", 1)],
    "ffn2_bwd_in_1": [("ffn2_w_down", 1)],
    "mix_bwd_in_1": [("rec_w_in", 0), ("rec_w_out", 0), ("lru_wa", 0), ("lru_wx", 0)],
    "ffn1_gw_1_up": [("ffn1_w_gate", 1)],
    "ffn1_gw_1_down": [("ffn1_w_up", 1)],
    "ffn1_bwd_in_1": [("ffn1_w_down", 1)],
    "ple_gw_proj_0": [("ple_w_gate", 0)],
    "ffn2_bwd_act_0": [("ple_w_proj", 0)],
    "ffn2_gw_0_up": [("ffn2_w_gate", 0)],
    "ffn2_gw_0_down": [("ffn2_w_up", 0)],
    "ffn2_bwd_in_0": [("ffn2_w_down", 0)],
    "mix_bwd_in_0": [("hyb_w_in", 0), ("hyb_w_out", 0)],
    "ffn1_gw_0_up": [("ffn1_w_gate", 0)],
    "ffn1_gw_0_down": [("ffn1_w_up", 0)],
    "ffn1_bwd_in_0": [("ffn1_w_down", 0)],
}
FORWARD_PLAN = {
    "ffn1_gw_1_up": ["rec_w_in", "rec_w_out", "lru_wa", "lru_wx"],
    "ffn2_bwd_in_0": ["ple_w_gate", "ple_w_proj", "ffn2_w_gate", "ffn2_w_up"],
    "mix_bwd_in_0": ["ffn2_w_down"],
    "ffn1_gw_0_up": ["hyb_w_in", "hyb_w_out"],
    "ffn1_bwd_in_0": ["ffn1_w_gate", "ffn1_w_up"],
}


class Plan:
    def __init__(self, shards, w):
        self.shards, self.w, self.grads, self.landed = shards, w, None, {}

    def host(self, kname):
        if kname in GATHER_PLAN:
            h = Host("gather", [self.shards[n] for n in GATHER_PLAN[kname]])
            h.names = GATHER_PLAN[kname]
            return h
        if kname in SCATTER_PLAN or kname in FORWARD_PLAN:
            items = SCATTER_PLAN.get(kname, [])
            fwd = FORWARD_PLAN.get(kname, [])
            h = Host("scatter", [(self.grads[n], lay, self.landed.get(n)) for n, lay in items], [self.landed[n] for n in fwd])
            h.names = [n for n, _ in items] + fwd
            return h
        return None

    def done(self, h):
        for n, o in zip(h.names, h.outs):
            if h.kind == "gather":
                self.w[n] = o
            else:
                self.landed[n] = o


def _halves(a):
    if a.shape[0] == 2:
        return a
    return a.reshape((2, a.shape[1] // 2) + a.shape[2:])


def _pack_rows(arrs, width):
    rows, offs, r0 = [], [], 0
    for a in arrs:
        if a.shape[1] > width:
            a = a.reshape(-1, width)
        rows.append(jnp.pad(a, ((0, 0), (0, width - a.shape[1]))))
        offs.append(r0)
        r0 += a.shape[0]
    pad = (-r0) % 8
    if pad:
        rows.append(jnp.zeros((pad, width), F32))
    return jnp.concatenate(rows, axis=0), offs


def kernel(x, p, rel_bias, ffn1_norm, ffn1_w_gate, ffn1_w_up, ffn1_w_down, mix_norm, hyb_w_in, hyb_conv_w, hyb_q_gain, hyb_k_gain, hyb_w_out, rec_w_in, rec_conv_w, rec_conv_b, lru_wa, lru_ba, lru_wx, lru_bx, lru_lambda, rec_w_out, ffn2_norm, ffn2_w_gate, ffn2_w_up, ffn2_w_down, ple_norm, ple_w_gate, ple_w_proj, loss_target, m_rel_bias, m_ffn1_norm, m_ffn1_w_gate, m_ffn1_w_up, m_ffn1_w_down, m_mix_norm, m_hyb_w_in, m_hyb_conv_w, m_hyb_q_gain, m_hyb_k_gain, m_hyb_w_out, m_rec_w_in, m_rec_conv_w, m_rec_conv_b, m_lru_wa, m_lru_ba, m_lru_wx, m_lru_bx, m_lru_lambda, m_rec_w_out, m_ffn2_norm, m_ffn2_w_gate, m_ffn2_w_up, m_ffn2_w_down, m_ple_norm, m_ple_w_gate, m_ple_w_proj, v_rel_bias, v_ffn1_norm, v_ffn1_w_gate, v_ffn1_w_up, v_ffn1_w_down, v_mix_norm, v_hyb_w_in, v_hyb_conv_w, v_hyb_q_gain, v_hyb_k_gain, v_hyb_w_out, v_rec_w_in, v_rec_conv_w, v_rec_conv_b, v_lru_wa, v_lru_ba, v_lru_wx, v_lru_bx, v_lru_lambda, v_rec_w_out, v_ffn2_norm, v_ffn2_w_gate, v_ffn2_w_up, v_ffn2_w_down, v_ple_norm, v_ple_w_gate, v_ple_w_proj):
    given = dict(locals())
    wts = {n: given[n] for n in WEIGHTS}
    k_chip = 2 * lax.axis_index("x") + lax.axis_index("y")

    shards = {}
    for n in BIG:
        b16 = wts[n].astype(BF16)
        if n in PER_LAYER:
            shards[n + "/0"], shards[n + "/1"] = b16[0:1], b16[1:2]
        else:
            shards[n] = b16
    first = gather_weights([_halves(shards[n]) for n in FIRST], "gather_first")
    w = {n: g.reshape((N_SHARD,) + shards[n].shape) for n, g in zip(FIRST, first)}
    plan = Plan(shards, w)
    sm2d = {n: wts[n].reshape(-1, wts[n].shape[-1]) for n in SMALL_SHARDED}
    slab, offs = _pack_rows([sm2d[n] for n in SMALL_SHARDED], 256)
    slabs = allgather8(slab, "gather_small")[0::2]
    for n, o in zip(SMALL_SHARDED, offs):
        r, cw = sm2d[n].shape
        w[n] = jnp.concatenate([slabs[kc, o:o + r, :cw] for kc in range(N_SHARD)], axis=1)
    for n in SMALL:
        if n not in SMALL_SHARDED:
            w[n] = wts[n]

    loss, dx, grads = local_step(x[0], p[:, 0], loss_target[0], w, plan)
    loss = lax.psum(loss[0, 0], ("x", "y", "c"))

    for n, r8 in zip(LAST, exchange_cores([plan.landed[n] for n in LAST], "exchange_cores")):
        plan.landed[n] = r8
    out = {}
    for n in BIG:
        r8 = plan.landed[n]
        shp = wts[n].shape
        c2 = shp[-1]
        two = lambda a: a.reshape(-1, c2)
        res = adamw(two(wts[n]), two(given["m_" + n]), two(given["v_" + n]), r8.reshape(N_DEV, -1, c2), "adamw_" + n)
        out[n] = [a.reshape(shp) for a in res]
    g2d = [grads[n].reshape(-1, grads[n].shape[-1]) if n != "rel_bias" else grads[n].reshape(1, -1) for n in SMALL]
    gslab, goffs = _pack_rows(g2d, PACK_W)
    gsum = sum8(allgather8(gslab, "gather_small_grads"), "sum_small_grads")
    for n, o, g in zip(SMALL, goffs, g2d):
        shp = wts[n].shape
        r, cw = g.shape
        gs = gsum[o:o + r, :cw]
        if n in SMALL_SHARDED:
            sw = shp[-1]
            gs = lax.dynamic_slice_in_dim(gs, k_chip * sw, sw, axis=1)
        gs = gs.reshape(shp)
        two = lambda a: a.reshape(-1, shp[-1])
        res = adamw(two(wts[n]), two(given["m_" + n]), two(given["v_" + n]), two(gs), "adamw_" + n)
        out[n] = [a.reshape(shp) for a in res]
    return (loss, dx[None], *[out[n][0] for n in WEIGHTS], *[out[n][1] for n in WEIGHTS],
            *[out[n][2] for n in WEIGHTS], *[out[n][3] for n in WEIGHTS])
```

```python
import functools
import math

import numpy as np
import jax
import jax.numpy as jnp
from jax import lax
from jax.experimental import pallas as pl
from jax.experimental.pallas import tpu as pltpu

F32, BF16 = jnp.float32, jnp.bfloat16
S = jax.ShapeDtypeStruct
MESH = pl.DeviceIdType.MESH

D_MODEL = 1024
N_SHARD = 4
N_DEV = 8
HEAD_DIM = 64
N_HEADS = 8
ATTN_W = N_HEADS * HEAD_DIM
CONV_W = 512
BAND = 128
DILATIONS = (1, 4, 16)
REL_BUCKETS = 32
REL_MAX_DIST = 2048
LRU_BLOCKS = 4
LRU_BLOCK = 256
LRU_C = 8.0
EPS = 1e-6
NEG = -1e30
VMEM_LIMIT = 56 * 1024 * 1024
FFN_ROWS = 1024
TN_ROWS = 2048
BWD_IN_ROWS = 1024

ADAM_LR, ADAM_B1, ADAM_B2, ADAM_EPS, ADAM_WD, ADAM_STEP = 0.001, 0.9, 0.999, 1e-08, 0.01, 10


def _cp(*sem):
    return pltpu.CompilerParams(dimension_semantics=sem, vmem_limit_bytes=VMEM_LIMIT)


def _bs(shape, imap):
    return pl.BlockSpec(shape, imap)


def _row_tile(t, want):
    for cand in range(min(want, t) // 8 * 8, 0, -8):
        if t % cand == 0:
            return cand
    return t


_ANY = pl.BlockSpec(memory_space=pl.ANY)


def _place():
    x, y, c = lax.axis_index("x"), lax.axis_index("y"), lax.axis_index("c")
    chips = [(1 - x, y), (x, 1 - y), (1 - x, 1 - y)]
    return x, y, c, 2 * x + y, chips, [2 * cx + cy for cx, cy in chips]


def _remote(src, dst, ssem, rsem, to):
    return pltpu.make_async_remote_copy(src_ref=src, dst_ref=dst, send_sem=ssem, recv_sem=rsem, device_id=to, device_id_type=MESH)


class Host:
    def __init__(self, kind, items, forwards=()):
        self.kind, self.items, self.forwards, self.outs = kind, items, list(forwards), None

    def n_sems(self):
        return 3 * len(self.items) + N_SHARD * len(self.forwards), len(self.items)

    def operands(self):
        if self.kind == "gather":
            return list(self.items), [S((N_SHARD,) + s.shape, s.dtype) for s in self.items], {}
        xin, shapes, alias = [], [], {}
        for a, (g, _, r_prev) in enumerate(self.items):
            xin.append(g)
            if r_prev is not None:
                alias[len(xin)] = a
                xin.append(r_prev)
            shapes.append(S((N_DEV,) + g.shape[1:], g.dtype))
        for f, r in enumerate(self.forwards):
            alias[len(xin)] = len(self.items) + f
            xin.append(r)
            shapes.append(S(r.shape, r.dtype))
        return xin, shapes, alias

    def copies(self, xi, xo, send, recv, lsem):
        x, y, c, k, chips, kk = _place()
        starts, waits = [], []
        pos = 0
        for f in range(len(self.forwards)):
            arr = xo[len(self.items) + f]
            for kq in range(N_SHARD):
                sem = 3 * len(self.items) + N_SHARD * f + kq
                cp = _remote(arr.at[2 * kq + c], arr.at[2 * kq + c], send.at[sem], recv.at[sem], (x, y, 1 - c))
                starts.append((cp, "start"))
                waits.append((cp, "wait_send"))
                other = arr.at[2 * kq + 1 - c]
                waits.append((_remote(other, other, send.at[sem], recv.at[sem], (x, y, 1 - c)), "wait_recv"))
        for a, item in enumerate(self.items):
            if self.kind == "gather":
                src_of = lambda chip_idx, s=xi[a]: s
                dst_of = lambda chip_idx, o=xo[a]: o.at[chip_idx]
                mine, theirs = k, kk
            else:
                g_ref = xi[pos]
                pos += 1 if item[2] is None else 2
                lay = item[1]
                src_of = lambda chip_idx, g=g_ref, lay=lay: g.at[chip_idx, lay]
                dst_of = lambda slot, o=xo[a], lay=lay: o.at[slot, lay]
                mine, theirs = 2 * k + c, [2 * kj + c for kj in kk]
            own_src = src_of(k)
            local = pltpu.make_async_copy(own_src, dst_of(mine), lsem.at[a])
            starts.append((local, "start"))
            waits.append((local, "wait"))
            for j, chip in enumerate(chips):
                src = own_src if self.kind == "gather" else src_of(kk[j])
                cp = _remote(src, dst_of(mine), send.at[3 * a + j], recv.at[3 * a + j], (*chip, c))
                starts.append((cp, "start"))
                waits.append((cp, "wait_send"))
                waits.append((_remote(own_src, dst_of(theirs[j]), send.at[3 * a + j], recv.at[3 * a + j], (*chip, c)), "wait_recv"))
        return starts, waits


def _call(host, body, *, name, grid, in_specs, out_specs, out_shape, scratch_shapes=(), compiler_params=None, args, aliases=None):
    aliases = dict(aliases or {})
    if host is None:
        return pl.pallas_call(body, name=name, grid=grid, in_specs=in_specs, out_specs=out_specs, out_shape=out_shape,
                              scratch_shapes=list(scratch_shapes), input_output_aliases=aliases, compiler_params=compiler_params)(*args)
    single = not isinstance(out_shape, (list, tuple))
    out_specs_l = [out_specs] if single else list(out_specs)
    out_shape_l = [out_shape] if single else list(out_shape)
    n_in, n_out, n_scr = len(in_specs), len(out_shape_l), len(scratch_shapes)
    xin, xshapes, xalias = host.operands()
    n_items = len(xshapes)
    n_rsem, n_lsem = host.n_sems()
    for i_in, i_out in xalias.items():
        aliases[n_in + i_in] = n_out + i_out
    nd = len(grid)

    def hosted(*refs):
        ins, xi = refs[:n_in], refs[n_in:n_in + len(xin)]
        o0 = n_in + len(xin)
        outs, xo = refs[o0:o0 + n_out], refs[o0 + n_out:o0 + n_out + n_items]
        s0 = o0 + n_out + n_items
        scr = refs[s0:s0 + n_scr]
        send, recv, lsem = refs[s0 + n_scr:]
        first = functools.reduce(jnp.logical_and, [pl.program_id(d) == 0 for d in range(nd)])
        last = functools.reduce(jnp.logical_and, [pl.program_id(d) == grid[d] - 1 for d in range(nd)])
        starts, waits = host.copies(xi, xo, send, recv, lsem)

        @pl.when(first)
        def _():
            for cp, how in starts:
                getattr(cp, how)()
        body(*ins, *outs, *scr)

        @pl.when(last)
        def _():
            for cp, how in waits:
                getattr(cp, how)()

    res = pl.pallas_call(
        hosted, name=name, grid=grid,
        in_specs=list(in_specs) + [_ANY] * len(xin),
        out_specs=out_specs_l + [_ANY] * n_items,
        out_shape=out_shape_l + xshapes,
        scratch_shapes=list(scratch_shapes) + [pltpu.SemaphoreType.DMA((n_rsem,)), pltpu.SemaphoreType.DMA((n_rsem,)),
                                               pltpu.SemaphoreType.DMA((max(n_lsem, 1),))],
        input_output_aliases=aliases,
        compiler_params=pltpu.CompilerParams(dimension_semantics=("arbitrary",) * nd, vmem_limit_bytes=VMEM_LIMIT),
    )(*args, *xin)
    host.outs = list(res[n_out:])
    return res[0] if single else list(res[:n_out])


def _rstd(x):
    return lax.rsqrt(jnp.mean(x * x, axis=-1, keepdims=True) + EPS)


def _sigmoid(x):
    return 1.0 / (1.0 + jnp.exp(-x))


def _dot(a, b):
    return jnp.dot(a, b, preferred_element_type=F32)


def _dot_nt(a, b):
    return lax.dot_general(a, b, (((1,), (1,)), ((), ())), preferred_element_type=F32)


def _dot_tn(a, b):
    return lax.dot_general(a, b, (((0,), (0,)), ((), ())), preferred_element_type=F32)


def _seg_dot(x, seg_bf16):
    hi = x.astype(BF16)
    lo = (x - hi.astype(F32)).astype(BF16)
    return _dot(hi, seg_bf16) + _dot(lo, seg_bf16)


def _shift_down(x, prev8, s):
    if s == 0:
        return x
    tm = x.shape[0]
    row = lax.broadcasted_iota(jnp.int32, x.shape, 0)
    main = jnp.where(row >= s, pltpu.roll(x, s, axis=0), 0.0)
    row8 = lax.broadcasted_iota(jnp.int32, prev8.shape, 0)
    head = jnp.where(row8 < s, pltpu.roll(prev8, s, axis=0), 0.0)
    if tm == 8:
        return main + head
    return main + jnp.concatenate([head, jnp.zeros((tm - 8, x.shape[1]), x.dtype)], axis=0)


def _shift_up(x, next8, s):
    if s == 0:
        return x
    tm = x.shape[0]
    row = lax.broadcasted_iota(jnp.int32, x.shape, 0)
    main = jnp.where(row < tm - s, pltpu.roll(x, tm - s, axis=0), 0.0)
    row8 = lax.broadcasted_iota(jnp.int32, next8.shape, 0)
    tail = jnp.where(row8 >= 8 - s, pltpu.roll(next8, 8 - s, axis=0), 0.0)
    if tm == 8:
        return main + tail
    return main + jnp.concatenate([jnp.zeros((tm - 8, x.shape[1]), x.dtype), tail], axis=0)


def _roll_fill(x, s, fill, up):
    tm = x.shape[0]
    row = lax.broadcasted_iota(jnp.int32, x.shape, 0)
    if up:
        return jnp.where(row < tm - s, pltpu.roll(x, tm - s, axis=0), fill)
    return jnp.where(row >= s, pltpu.roll(x, s, axis=0), fill)


def _log1p(y):
    u = 1.0 + y
    return jnp.where(u == 1.0, y, jnp.log(u) * (y / jnp.where(u == 1.0, 1.0, u - 1.0)))


def _softplus(x):
    return jnp.maximum(x, 0.0) + _log1p(jnp.exp(-jnp.abs(x)))


def _neg_expm1(y):
    series = -y * (1.0 + y * (0.5 + y * (1.0 / 6.0 + y * (1.0 / 24.0 + y * (1.0 / 120.0)))))
    return jnp.where(jnp.abs(y) < 0.03, series, 1.0 - jnp.exp(y))


_GELU_C = math.sqrt(2.0 / math.pi)


def _gelu_and_grad(x):
    inner = _GELU_C * (x + 0.044715 * x * x * x)
    t = jnp.tanh(inner)
    g = 0.5 * x * (1.0 + t)
    dg = 0.5 * (1.0 + t) + 0.5 * x * (1.0 - t * t) * _GELU_C * (1.0 + 3.0 * 0.044715 * x * x)
    return g, dg


def _rmsnorm_bwd(x, gain, dy):
    r = _rstd(x)
    xhat = x * r
    dxhat = dy * gain
    dx = r * (dxhat - xhat * jnp.mean(dxhat * xhat, axis=-1, keepdims=True))
    return dx, jnp.sum(dy * xhat, axis=0, keepdims=True)


def ffn_up(h, gain, wg, wu, layer, name, host=None):
    t, d = h.shape
    nk, _, _, f = wg.shape
    tm = _row_tile(t, FFN_ROWS)

    def body(h_ref, g_ref, wg_ref, wu_ref, hn_ref, gg_ref, uu_ref, aa_ref, hn_scr):
        @pl.when(pl.program_id(1) == 0)
        def _():
            x = h_ref[...]
            hn = (x * _rstd(x) * g_ref[...]).astype(BF16)
            hn_scr[...] = hn
            hn_ref[...] = hn
        hn = hn_scr[...]
        g = _dot(hn, wg_ref[...])
        u = _dot(hn, wu_ref[...])
        s = _sigmoid(g)
        silu = g * s
        gg_ref[...] = (u * (s * (1.0 + g * (1.0 - s)))).astype(BF16)
        uu_ref[...] = silu.astype(BF16)
        aa_ref[...] = (silu * u).astype(BF16)

    wspec = _bs((None, None, d, f), lambda i, k: (k, layer, 0, 0))
    aspec = _bs((None, tm, f), lambda i, k: (k, i, 0))
    return _call(
        host, body, name=name, grid=(t // tm, nk),
        in_specs=[_bs((tm, d), lambda i, k: (i, 0)), _bs((1, d), lambda i, k: (0, 0)), wspec, wspec],
        out_specs=[_bs((tm, d), lambda i, k: (i, 0)), aspec, aspec, aspec],
        out_shape=[S((t, d), BF16), S((nk, t, f), BF16), S((nk, t, f), BF16), S((nk, t, f), BF16)],
        scratch_shapes=[pltpu.VMEM((tm, d), BF16)],
        compiler_params=_cp("parallel", "arbitrary"),
        args=(h, gain, wg, wu))


def mm_acc(a, a_spec, b, b_spec, res, scale, nk, t, n, tm, name, host=None):
    def body(a_ref, b_ref, r_ref, o_ref, acc):
        k = pl.program_id(1)

        @pl.when(k == 0)
        def _():
            acc[...] = jnp.zeros_like(acc)
        acc[...] += _dot(a_ref[...].astype(BF16), b_ref[...])

        @pl.when(k == nk - 1)
        def _():
            o_ref[...] = r_ref[...] + scale * acc[...]

    return _call(
        host, body, name=name, grid=(t // tm, nk),
        in_specs=[a_spec, b_spec, _bs((tm, n), lambda i, k: (i, 0))],
        out_specs=_bs((tm, n), lambda i, k: (i, 0)),
        out_shape=S((t, n), F32),
        scratch_shapes=[pltpu.VMEM((tm, n), F32)],
        compiler_params=_cp("parallel", "arbitrary"),
        args=(a, b, res))


def ffn_down(a, wd, layer, h, name, host=None):
    nk, t, f = a.shape
    d = h.shape[1]
    tm = _row_tile(t, FFN_ROWS)
    return mm_acc(a, _bs((None, tm, f), lambda i, k: (k, i, 0)),
                  wd, _bs((None, None, f, d), lambda i, k: (k, layer, 0, 0)),
                  h, 0.5, nk, t, d, tm, name, host)


def ffn_bwd_act(dh, wd, layer, gg, uu, name, host=None):
    nk, t, f = gg.shape
    d = dh.shape[1]
    tm = _row_tile(t, FFN_ROWS)

    def body(dh_ref, wd_ref, g_ref, u_ref, dg_ref, du_ref):
        da = 0.5 * _dot_nt(dh_ref[...].astype(BF16), wd_ref[...])
        dg_ref[...] = (da * g_ref[...].astype(F32)).astype(BF16)
        du_ref[...] = (da * u_ref[...].astype(F32)).astype(BF16)

    aspec = _bs((None, tm, f), lambda i, k: (k, i, 0))
    return _call(
        host, body, name=name, grid=(t // tm, nk),
        in_specs=[_bs((tm, d), lambda i, k: (i, 0)), _bs((None, None, f, d), lambda i, k: (k, layer, 0, 0)), aspec, aspec],
        out_specs=[aspec, aspec],
        out_shape=[S((nk, t, f), BF16), S((nk, t, f), BF16)],
        compiler_params=_cp("parallel", "arbitrary"),
        args=(dh, wd, gg, uu))


def nt_acc_normbwd(terms, nk, h, gain, dh, name, host=None):
    t, d = h.shape
    tm = _row_tile(t, BWD_IN_ROWS)
    sub = _row_tile(tm, 256)
    nterm = len(terms)

    def body(*refs):
        xs = refs[:2 * nterm]
        h_ref, g_ref, dh_ref, o_ref, dg_ref, acc = refs[2 * nterm:]
        i, k = pl.program_id(0), pl.program_id(1)

        @pl.when(k == 0)
        def _():
            acc[...] = jnp.zeros_like(acc)

        @pl.when(jnp.logical_and(i == 0, k == 0))
        def _():
            dg_ref[...] = jnp.zeros_like(dg_ref)
        tot = _dot_nt(xs[0][...], xs[1][...])
        for j in range(1, nterm):
            tot = tot + _dot_nt(xs[2 * j][...], xs[2 * j + 1][...])
        acc[...] += tot

        @pl.when(k == nk - 1)
        def _():
            def rows_of(cidx, dgain):
                rows = pl.ds(pl.multiple_of(cidx * sub, sub), sub)
                dx, dgc = _rmsnorm_bwd(h_ref[rows, :], g_ref[...], acc[rows, :])
                o_ref[rows, :] = dh_ref[rows, :] + dx
                return dgain + dgc
            dg_ref[...] += lax.fori_loop(0, tm // sub, rows_of, jnp.zeros((1, d), F32))

    in_specs, args = [], []
    for x, xs_, w, ws_ in terms:
        in_specs += [xs_, ws_]
        args += [x, w]
    row = _bs((tm, d), lambda i, k: (i, 0))
    vec = _bs((1, d), lambda i, k: (0, 0))
    return _call(
        host, body, name=name, grid=(t // tm, nk),
        in_specs=in_specs + [row, vec, row],
        out_specs=[row, vec],
        out_shape=[S((t, d), F32), S((1, d), F32)],
        scratch_shapes=[pltpu.VMEM((tm, d), F32)],
        compiler_params=_cp("arbitrary", "arbitrary"),
        args=(*args, h, gain, dh))


def ffn_bwd_in(dg, du, wg, wu, layer, h, gain, dh, name, host=None):
    nk, t, f = dg.shape
    d = h.shape[1]
    tm = _row_tile(t, BWD_IN_ROWS)
    aspec = _bs((None, tm, f), lambda i, k: (k, i, 0))
    wspec = _bs((None, None, d, f), lambda i, k: (k, layer, 0, 0))
    return nt_acc_normbwd([(dg, aspec, wg, wspec), (du, aspec, wu, wspec)], nk, h, gain, dh, name, host)


def tn_mm(x, x_spec, y, y_spec, nblk, t, ka, nb, out_shape, out_spec, scale, prev, name, host=None):
    tk = _row_tile(t, TN_ROWS)

    def body(*refs):
        if prev is None:
            x_ref, y_ref, o_ref, acc = refs
        else:
            x_ref, y_ref, _, o_ref, acc = refs
        j = pl.program_id(1)

        @pl.when(j == 0)
        def _():
            acc[...] = jnp.zeros_like(acc)
        acc[...] += _dot_tn(x_ref[...].astype(BF16), y_ref[...].astype(BF16))

        @pl.when(j == t // tk - 1)
        def _():
            o_ref[...] = (scale * acc[...]).astype(o_ref.dtype)

    in_specs = [x_spec(tk), y_spec(tk)]
    args = [x, y]
    aliases = {}
    if prev is not None:
        in_specs.append(pl.BlockSpec(memory_space=pl.ANY))
        args.append(prev)
        aliases = {2: 0}
    return _call(
        host, body, name=name, grid=(nblk, t // tk),
        in_specs=in_specs, out_specs=out_spec, out_shape=out_shape,
        scratch_shapes=[pltpu.VMEM((ka, nb), F32)],
        aliases=aliases,
        compiler_params=_cp("parallel", "arbitrary"),
        args=tuple(args))


def ffn_wgrads(which, hn, dh, aa, dg, du, layer, grads, run):
    nk, t, f = aa.shape
    d = hn.shape[1]
    hn_spec = lambda tk: _bs((tk, d), lambda k, j: (j, 0))
    a_spec = lambda tk: _bs((None, tk, f), lambda k, j: (k, j, 0))
    shape_gu, spec_gu = S((nk, 2, d, f), BF16), _bs((None, None, d, f), lambda k, j: (k, layer, 0, 0))
    shape_d, spec_d = S((nk, 2, f, d), BF16), _bs((None, None, f, d), lambda k, j: (k, layer, 0, 0))
    for suffix, x, xs, y, ys, ka, nb, shp, spec, scale in (
            ("gate", hn, hn_spec, dg, a_spec, d, f, shape_gu, spec_gu, 1.0),
            ("up", hn, hn_spec, du, a_spec, d, f, shape_gu, spec_gu, 1.0),
            ("down", aa, a_spec, dh, hn_spec, f, d, shape_d, spec_d, 0.5)):
        key = f"{which}_w_{suffix}"
        grads[key] = run(tn_mm, x, xs, y, ys, nk, t, ka, nb, shp, spec, scale, grads.get(key), name=f"{which}_gw_{layer}_{suffix}")


def norm_mm(h, gain, w, name, host=None):
    t, d = h.shape
    nb, _, bw = w.shape
    tm = _row_tile(t, 512)

    def body(h_ref, g_ref, w_ref, hn_ref, z_ref, hn_scr):
        @pl.when(pl.program_id(1) == 0)
        def _():
            x = h_ref[...]
            hn = (x * _rstd(x) * g_ref[...]).astype(BF16)
            hn_scr[...] = hn
            hn_ref[...] = hn
        z_ref[...] = _dot(hn_scr[...], w_ref[...])

    return _call(
        host, body, name=name, grid=(t // tm, nb),
        in_specs=[_bs((tm, d), lambda i, k: (i, 0)), _bs((1, d), lambda i, k: (0, 0)), _bs((None, d, bw), lambda i, k: (k, 0, 0))],
        out_specs=[_bs((tm, d), lambda i, k: (i, 0)), _bs((tm, bw), lambda i, k: (i, k))],
        out_shape=[S((t, d), BF16), S((t, nb * bw), F32)],
        scratch_shapes=[pltpu.VMEM((tm, d), BF16)],
        compiler_params=_cp("parallel", "arbitrary"),
        args=(h, gain, w))


def nt_mm(a, w, name):
    t, k = a.shape
    n = w.shape[0]
    tm = _row_tile(t, 512)

    def body(a_ref, w_ref, o_ref):
        o_ref[...] = _dot_nt(a_ref[...].astype(BF16), w_ref[...])

    return pl.pallas_call(
        body, name=name, grid=(t // tm,),
        in_specs=[_bs((tm, k), lambda i: (i, 0)), _bs((n, k), lambda i: (0, 0))],
        out_specs=_bs((tm, n), lambda i: (i, 0)),
        out_shape=S((t, n), F32),
        compiler_params=_cp("parallel"),
    )(a, w)


def _head_mean_matrix():
    m = np.kron(np.eye(N_HEADS, dtype=np.float32), np.full((HEAD_DIM, HEAD_DIM), 1.0 / HEAD_DIM, np.float32))
    return jnp.asarray(m, BF16)


def _head_sum_matrix():
    m = np.kron(np.eye(N_HEADS, dtype=np.float32), np.ones((HEAD_DIM, HEAD_DIM), np.float32))
    return jnp.asarray(m, BF16)


def _rel_bucket_np(dist):
    max_exact = REL_BUCKETS // 2
    n = np.maximum(dist, 1).astype(np.float32)
    large = max_exact + (np.log(n / np.float32(max_exact)) / np.float32(math.log(REL_MAX_DIST / max_exact))
                         * np.float32(REL_BUCKETS - max_exact)).astype(np.int32)
    large = np.minimum(large, REL_BUCKETS - 1)
    return np.where(dist < max_exact, dist, large)


def _band_tables():
    qi = np.arange(BAND)[:, None]
    kj = np.arange(2 * BAND)[None, :]
    dist_q = qi + BAND - kj
    qq = np.arange(2 * BAND)[:, None]
    kk = np.arange(BAND)[None, :]
    dist_k = qq - kk
    out = []
    for dist in (dist_q, dist_k):
        valid = (dist >= 0) & (dist <= BAND)
        bucket = np.stack([_rel_bucket_np(np.clip(dist, 0, BAND) * d) for d in DILATIONS])
        out.append((bucket, valid))
    return out


def band_bias(rel_bias):
    out = []
    for bucket, valid in _band_tables():
        bucket = np.where(valid[None], bucket, -1)[:, None]
        tab = jnp.full((len(DILATIONS), N_HEADS) + bucket.shape[2:], NEG, F32)
        for b in range(REL_BUCKETS):
            if (bucket == b).any():
                tab = jnp.where(bucket == b, rel_bias[b][None, :, None, None], tab)
        out.append(tab)
    return out


LANE_TILE = 128
N_LANE_TILES = ATTN_W // LANE_TILE


def _view_shape(t, dil):
    return (t // dil, dil * ATTN_W)


def _view_spec(tm, dil):
    return _bs((tm // dil, dil * ATTN_W), lambda i: (i, 0))


def _cols_to(scr, val):
    for cc in range(N_LANE_TILES):
        scr[cc] = val[:, LANE_TILE * cc:LANE_TILE * (cc + 1)]


def _cols_from(scr):
    return jnp.concatenate([scr[cc] for cc in range(N_LANE_TILES)], axis=1)


def _write_view(scr, out_ref, dil):
    if dil == 1:
        out_ref[...] = _cols_from(scr).astype(out_ref.dtype)
        return
    rows = scr.shape[1] // dil
    for r in range(dil):
        for cc in range(N_LANE_TILES):
            c0 = r * ATTN_W + LANE_TILE * cc
            out_ref[:, c0:c0 + LANE_TILE] = scr[cc, pl.ds(r, rows, stride=dil), :].astype(out_ref.dtype)


def _read_view(scr, in_ref, dil):
    if dil == 1:
        return in_ref[...].astype(F32)
    rows = scr.shape[1] // dil
    for r in range(dil):
        for cc in range(N_LANE_TILES):
            c0 = r * ATTN_W + LANE_TILE * cc
            scr[cc, pl.ds(r, rows, stride=dil), :] = in_ref[:, c0:c0 + LANE_TILE].astype(F32)
    return _cols_from(scr)


def hyb_prep(z, q_gain, k_gain, name):
    t = z.shape[0]
    tm = _row_tile(t, 512)
    seg = _head_mean_matrix()
    nd = len(DILATIONS)

    def body(q_ref, k_ref, v_ref, qg_ref, kg_ref, seg_ref, *rest):
        outs, scr = rest[:3 * nd], rest[3 * nd]
        q = q_ref[...]
        k = k_ref[...]
        vals = (q * lax.rsqrt(_seg_dot(q * q, seg_ref[...]) + EPS) * qg_ref[...],
                k * lax.rsqrt(_seg_dot(k * k, seg_ref[...]) + EPS) * kg_ref[...],
                v_ref[...])
        for j, val in enumerate(vals):
            _cols_to(scr, val)
            for g, dil in enumerate(DILATIONS):
                _write_view(scr, outs[3 * g + j], dil)

    col = lambda c: _bs((tm, ATTN_W), lambda i: (i, c))
    vec = _bs((1, ATTN_W), lambda i: (0, 0))
    res = pl.pallas_call(
        body, name=name, grid=(t // tm,),
        in_specs=[col(3), col(4), col(5), vec, vec, _bs((ATTN_W, ATTN_W), lambda i: (0, 0))],
        out_specs=[_view_spec(tm, dil) for dil in DILATIONS for _ in range(3)],
        out_shape=[S(_view_shape(t, dil), BF16) for dil in DILATIONS for _ in range(3)],
        scratch_shapes=[pltpu.VMEM((N_LANE_TILES, tm, LANE_TILE), F32)],
        compiler_params=_cp("parallel"),
    )(z, z, z, q_gain, k_gain, seg)
    return {dil: tuple(res[3 * g:3 * g + 3]) for g, dil in enumerate(DILATIONS)}


def _lane_lo(shape):
    return lax.broadcasted_iota(jnp.int32, shape, 1) < HEAD_DIM


def attn_fwd(q, k, v, bias, dil, name):
    qv, kv, vv = q, k, v
    sub = q.shape[0]
    nb = sub // BAND

    def body(q_ref, kp_ref, kc_ref, vp_ref, vc_ref, b_ref, o_ref, l_ref):
        n = pl.program_id(1)
        first = n == 0
        colk = lax.broadcasted_iota(jnp.int32, (BAND, 2 * BAND), 1)
        for j in range(N_HEADS // 2):
            sl = slice(2 * HEAD_DIM * j, 2 * HEAD_DIM * (j + 1))
            qp = q_ref[:, sl]
            kk = jnp.concatenate([kp_ref[:, sl], kc_ref[:, sl]], axis=0)
            vv_ = jnp.concatenate([vp_ref[:, sl], vc_ref[:, sl]], axis=0)
            lo = _lane_lo(qp.shape)
            outs, lses = [], []
            for hh in range(2):
                qm = jnp.where(lo if hh == 0 else jnp.logical_not(lo), qp, jnp.zeros_like(qp))
                s = _dot_nt(qm, kk) * (HEAD_DIM ** -0.5) + b_ref[2 * j + hh]
                s = jnp.where(jnp.logical_and(first, colk < BAND), NEG, s)
                m = jnp.max(s, axis=-1, keepdims=True)
                p = jnp.exp(s - m)
                l = jnp.sum(p, axis=-1, keepdims=True)
                outs.append(_dot(p.astype(BF16), vv_) / l)
                lses.append(m + jnp.log(l))
            o_ref[:, sl] = jnp.where(lo, outs[0], outs[1])
            l_ref[:, sl] = jnp.where(lo, lses[0], lses[1])

    cur = _bs((BAND, ATTN_W), lambda r, n: (n, r))
    prv = _bs((BAND, ATTN_W), lambda r, n: (jnp.maximum(n - 1, 0), r))
    o, lse = pl.pallas_call(
        body, name=name, grid=(dil, nb),
        in_specs=[cur, prv, cur, prv, cur, _bs((N_HEADS, BAND, 2 * BAND), lambda r, n: (0, 0, 0))],
        out_specs=[cur, cur],
        out_shape=[S((sub, dil * ATTN_W), F32)] * 2,
        compiler_params=_cp("parallel", "arbitrary"),
    )(qv, kv, kv, vv, vv, bias)
    return o, lse


def hyb_post(z, conv_w, os_, lses, name):
    t = z.shape[0]
    tm = _row_tile(t, 512)
    nd = len(DILATIONS)

    def body(gb_ref, gc_ref, cx_ref, gch_ref, cxh_ref, w_ref, *rest):
        o_refs, l_refs = rest[:nd], rest[nd:2 * nd]
        y_ref, ya_ref = rest[2 * nd:2 * nd + 2]
        lt_refs, scr = rest[2 * nd + 2:3 * nd + 2], rest[3 * nd + 2]
        i = pl.program_id(0)
        m = gc_ref[...] * cx_ref[...]
        mh = jnp.where(i == 0, 0.0, gch_ref[...] * cxh_ref[...])
        conv = w_ref[0:1, :] * _shift_down(m, mh, 2) + w_ref[1:2, :] * _shift_down(m, mh, 1) + w_ref[2:3, :] * m
        y_ref[0] = (gb_ref[...] * conv).astype(BF16)
        ls = [_read_view(scr, l_refs[g], dil) for g, dil in enumerate(DILATIONS)]
        mx = functools.reduce(jnp.maximum, ls)
        es = [jnp.exp(l - mx) for l in ls]
        den = functools.reduce(lambda a, b: a + b, es)
        num = es[0] * _read_view(scr, o_refs[0], DILATIONS[0])
        for g in range(1, nd):
            num = num + es[g] * _read_view(scr, o_refs[g], DILATIONS[g])
        ya = num / den
        y_ref[1] = ya.astype(BF16)
        ya_ref[...] = ya
        _cols_to(scr, mx + jnp.log(den))
        for g, dil in enumerate(DILATIONS):
            _write_view(scr, lt_refs[g], dil)

    hb = tm // 8
    col = lambda c: _bs((tm, CONV_W), lambda i: (i, c))
    halo = lambda c: _bs((8, CONV_W), lambda i: (jnp.maximum(i * hb - 1, 0), c))
    row = _bs((tm, ATTN_W), lambda i: (i, 0))
    views = [_view_spec(tm, dil) for dil in DILATIONS]
    res = pl.pallas_call(
        body, name=name, grid=(t // tm,),
        in_specs=[col(0), col(1), col(2), halo(1), halo(2), _bs((3, CONV_W), lambda i: (0, 0))] + views * 2,
        out_specs=[_bs((2, tm, ATTN_W), lambda i: (0, i, 0)), row] + views,
        out_shape=[S((2, t, ATTN_W), BF16), S((t, ATTN_W), F32)] + [S(_view_shape(t, dil), F32) for dil in DILATIONS],
        scratch_shapes=[pltpu.VMEM((N_LANE_TILES, tm, LANE_TILE), F32)],
        compiler_params=_cp("parallel"),
    )(z, z, z, z, z, conv_w, *os_, *lses)
    return res[0], res[1], dict(zip(DILATIONS, res[2:]))


def attn_delta(dy, ya, name):
    t = ya.shape[0]
    tm = _row_tile(t, 512)
    seg = _head_sum_matrix()
    nd = len(DILATIONS)

    def body(dy_ref, ya_ref, seg_ref, *rest):
        dl_refs, db_refs, scr = rest[:nd], rest[nd:2 * nd], rest[2 * nd]
        dya = dy_ref[...]
        _cols_to(scr, _seg_dot(dya * ya_ref[...], seg_ref[...]))
        for g, dil in enumerate(DILATIONS):
            _write_view(scr, dl_refs[g], dil)
        _cols_to(scr, dya)
        for g, dil in enumerate(DILATIONS):
            _write_view(scr, db_refs[g], dil)

    row = _bs((tm, ATTN_W), lambda i: (i, 0))
    views = [_view_spec(tm, dil) for dil in DILATIONS]
    res = pl.pallas_call(
        body, name=name, grid=(t // tm,),
        in_specs=[_bs((tm, ATTN_W), lambda i: (i, 1)), row, _bs((ATTN_W, ATTN_W), lambda i: (0, 0))],
        out_specs=views * 2,
        out_shape=[S(_view_shape(t, dil), F32) for dil in DILATIONS] + [S(_view_shape(t, dil), BF16) for dil in DILATIONS],
        scratch_shapes=[pltpu.VMEM((N_LANE_TILES, tm, LANE_TILE), F32)],
        compiler_params=_cp("parallel"),
    )(dy, ya, seg)
    return dict(zip(DILATIONS, res[:nd])), dict(zip(DILATIONS, res[nd:]))


def attn_bwd_dq(q, k, v, dya, lt, delta, bias, dil, name):
    qv, kv, vv, dv_, lv, ev = q, k, v, dya, lt, delta
    sub = q.shape[0]
    nb = sub // BAND

    def body(q_ref, kp_ref, kc_ref, vp_ref, vc_ref, do_ref, l_ref, e_ref, b_ref, dq_ref, db_ref):
        r, n = pl.program_id(0), pl.program_id(1)

        @pl.when(jnp.logical_and(r == 0, n == 0))
        def _():
            db_ref[...] = jnp.zeros_like(db_ref)
        first = n == 0
        colk = lax.broadcasted_iota(jnp.int32, (BAND, 2 * BAND), 1)
        for j in range(N_HEADS // 2):
            sl = slice(2 * HEAD_DIM * j, 2 * HEAD_DIM * (j + 1))
            qp, dop = q_ref[:, sl], do_ref[:, sl]
            kk = jnp.concatenate([kp_ref[:, sl], kc_ref[:, sl]], axis=0)
            vv_ = jnp.concatenate([vp_ref[:, sl], vc_ref[:, sl]], axis=0)
            lo = _lane_lo(qp.shape)
            dqs = []
            for hh in range(2):
                msk = lo if hh == 0 else jnp.logical_not(lo)
                c0 = 2 * HEAD_DIM * j + HEAD_DIM * hh
                qm = jnp.where(msk, qp, jnp.zeros_like(qp))
                dom = jnp.where(msk, dop, jnp.zeros_like(dop))
                s = _dot_nt(qm, kk) * (HEAD_DIM ** -0.5) + b_ref[2 * j + hh]
                s = jnp.where(jnp.logical_and(first, colk < BAND), NEG, s)
                p = jnp.exp(s - l_ref[:, c0:c0 + 1])
                ds = p * (_dot_nt(dom, vv_) - e_ref[:, c0:c0 + 1])
                db_ref[2 * j + hh] += ds
                dqs.append(_dot(ds.astype(BF16), kk) * (HEAD_DIM ** -0.5))
            dq_ref[:, sl] = jnp.where(lo, dqs[0], dqs[1])

    cur = _bs((BAND, ATTN_W), lambda r, n: (n, r))
    prv = _bs((BAND, ATTN_W), lambda r, n: (jnp.maximum(n - 1, 0), r))
    tab = _bs((N_HEADS, BAND, 2 * BAND), lambda r, n: (0, 0, 0))
    dq, db = pl.pallas_call(
        body, name=name, grid=(dil, nb),
        in_specs=[cur, prv, cur, prv, cur, cur, cur, cur, tab],
        out_specs=[cur, tab],
        out_shape=[S((sub, dil * ATTN_W), F32), S((N_HEADS, BAND, 2 * BAND), F32)],
        compiler_params=_cp("arbitrary", "arbitrary"),
    )(qv, kv, kv, vv, vv, dv_, lv, ev, bias)
    return dq, db


def attn_bwd_dkv(q, k, v, dya, lt, delta, bias_k, dil, name):
    qv, kv, vv, dv_, lv, ev = q, k, v, dya, lt, delta
    sub = q.shape[0]
    nb = sub // BAND

    def body(k_ref, v_ref, qc_ref, qn_ref, dc_ref, dn_ref, lc_ref, ln_ref, ec_ref, en_ref, b_ref, dk_ref, dv_ref):
        n = pl.program_id(1)
        last = n == nb - 1
        rowq = lax.broadcasted_iota(jnp.int32, (2 * BAND, BAND), 0)
        for j in range(N_HEADS // 2):
            sl = slice(2 * HEAD_DIM * j, 2 * HEAD_DIM * (j + 1))
            kp, vp = k_ref[:, sl], v_ref[:, sl]
            qq = jnp.concatenate([qc_ref[:, sl], qn_ref[:, sl]], axis=0)
            do = jnp.concatenate([dc_ref[:, sl], dn_ref[:, sl]], axis=0)
            lo = _lane_lo(qq.shape)
            dks, dvs = [], []
            for hh in range(2):
                msk = lo if hh == 0 else jnp.logical_not(lo)
                c0 = 2 * HEAD_DIM * j + HEAD_DIM * hh
                qm = jnp.where(msk, qq, jnp.zeros_like(qq))
                dom = jnp.where(msk, do, jnp.zeros_like(do))
                ll = jnp.concatenate([lc_ref[:, c0:c0 + 1], ln_ref[:, c0:c0 + 1]], axis=0)
                ee = jnp.concatenate([ec_ref[:, c0:c0 + 1], en_ref[:, c0:c0 + 1]], axis=0)
                s = _dot_nt(qm, kp) * (HEAD_DIM ** -0.5) + b_ref[2 * j + hh]
                s = jnp.where(jnp.logical_and(last, rowq >= BAND), NEG, s)
                p = jnp.exp(s - ll)
                ds = p * (_dot_nt(dom, vp) - ee)
                dvs.append(_dot_tn(p.astype(BF16), dom))
                dks.append(_dot_tn(ds.astype(BF16), qm) * (HEAD_DIM ** -0.5))
            lo_k = _lane_lo(kp.shape)
            dk_ref[:, sl] = jnp.where(lo_k, dks[0], dks[1])
            dv_ref[:, sl] = jnp.where(lo_k, dvs[0], dvs[1])

    cur = _bs((BAND, ATTN_W), lambda r, n: (n, r))
    nxt = _bs((BAND, ATTN_W), lambda r, n: (jnp.minimum(n + 1, nb - 1), r))
    tab = _bs((N_HEADS, 2 * BAND, BAND), lambda r, n: (0, 0, 0))
    dk, dv = pl.pallas_call(
        body, name=name, grid=(dil, nb),
        in_specs=[cur, cur, cur, nxt, cur, nxt, cur, nxt, cur, nxt, tab],
        out_specs=[cur, cur],
        out_shape=[S((sub, dil * ATTN_W), F32)] * 2,
        compiler_params=_cp("parallel", "arbitrary"),
    )(kv, vv, qv, qv, dv_, dv_, lv, lv, ev, ev, bias_k)
    return dk, dv


def hyb_dz(z, dy, conv_w, q_gain, k_gain, dqs, dks, dvs, name):
    t = z.shape[0]
    tm = _row_tile(t, 256)
    nt = t // tm
    seg = _head_mean_matrix()

    def body(gb_ref, gc_ref, cx_ref, q_ref, k_ref, gch_ref, cxh_ref, gbn_ref, dyc_ref, dyn_ref, w_ref, qg_ref, kg_ref, seg_ref,
             dq1, dq2, dq3, dk1, dk2, dk3, dv1, dv2, dv3, dz_ref, dw_ref, dqg_ref, dkg_ref, scr):
        i = pl.program_id(0)

        def total(parts):
            acc = _read_view(scr, parts[0], DILATIONS[0])
            for g in range(1, len(DILATIONS)):
                acc = acc + _read_view(scr, parts[g], DILATIONS[g])
            return acc

        @pl.when(i == 0)
        def _():
            dw_ref[...] = jnp.zeros_like(dw_ref)
            dqg_ref[...] = jnp.zeros_like(dqg_ref)
            dkg_ref[...] = jnp.zeros_like(dkg_ref)
        gb, gc, cx, dyc = gb_ref[...], gc_ref[...], cx_ref[...], dyc_ref[...]
        m = gc * cx
        mh = jnp.where(i == 0, 0.0, gch_ref[...] * cxh_ref[...])
        m1, m2 = _shift_down(m, mh, 1), _shift_down(m, mh, 2)
        conv = w_ref[0:1, :] * m2 + w_ref[1:2, :] * m1 + w_ref[2:3, :] * m
        dconv = dyc * gb
        dcn = jnp.where(i == nt - 1, 0.0, dyn_ref[...] * gbn_ref[...])
        dm = w_ref[2:3, :] * dconv + w_ref[1:2, :] * _shift_up(dconv, dcn, 1) + w_ref[0:1, :] * _shift_up(dconv, dcn, 2)
        dz_ref[:, 0:CONV_W] = (dyc * conv).astype(BF16)
        dz_ref[:, CONV_W:2 * CONV_W] = (dm * cx).astype(BF16)
        dz_ref[:, 2 * CONV_W:3 * CONV_W] = (dm * gc).astype(BF16)
        dw_ref[0:1, :] += jnp.sum(dconv * m2, axis=0, keepdims=True)
        dw_ref[1:2, :] += jnp.sum(dconv * m1, axis=0, keepdims=True)
        dw_ref[2:3, :] += jnp.sum(dconv * m, axis=0, keepdims=True)
        base = 3 * CONV_W
        for idx, (x_ref, g_ref, parts, dgain_ref) in enumerate(((q_ref, qg_ref, (dq1, dq2, dq3), dqg_ref),
                                                                  (k_ref, kg_ref, (dk1, dk2, dk3), dkg_ref))):
            x = x_ref[...]
            dxh = total(parts)
            r = lax.rsqrt(_seg_dot(x * x, seg_ref[...]) + EPS)
            xhat = x * r
            tt = dxh * g_ref[...]
            dx = r * (tt - xhat * _seg_dot(tt * xhat, seg_ref[...]))
            dz_ref[:, base + idx * ATTN_W:base + (idx + 1) * ATTN_W] = dx.astype(BF16)
            dgain_ref[...] += jnp.sum(dxh * xhat, axis=0, keepdims=True)
        dz_ref[:, base + 2 * ATTN_W:base + 3 * ATTN_W] = total((dv1, dv2, dv3)).astype(BF16)

    hb = tm // 8
    col = lambda c: _bs((tm, CONV_W), lambda i: (i, c))
    prev = lambda c: _bs((8, CONV_W), lambda i: (jnp.maximum(i * hb - 1, 0), c))
    nxt = lambda c: _bs((8, CONV_W), lambda i: (jnp.minimum((i + 1) * hb, t // 8 - 1), c))
    row = _bs((tm, ATTN_W), lambda i: (i, 0))
    vec = _bs((1, ATTN_W), lambda i: (0, 0))
    return pl.pallas_call(
        body, name=name, grid=(nt,),
        in_specs=[col(0), col(1), col(2), col(3), col(4), prev(1), prev(2), nxt(0), col(0), nxt(0),
                  _bs((3, CONV_W), lambda i: (0, 0)), vec, vec, _bs((ATTN_W, ATTN_W), lambda i: (0, 0))]
                 + [_view_spec(tm, dil) for dil in DILATIONS] * 3,
        out_specs=[_bs((tm, 6 * CONV_W), lambda i: (i, 0)), _bs((3, CONV_W), lambda i: (0, 0)), vec, vec],
        out_shape=[S((t, 6 * CONV_W), BF16), S((3, CONV_W), F32), S((1, ATTN_W), F32), S((1, ATTN_W), F32)],
        scratch_shapes=[pltpu.VMEM((N_LANE_TILES, tm, LANE_TILE), F32)],
        compiler_params=_cp("arbitrary"),
    )(z, z, z, z, z, z, z, z, dy, dy, conv_w, q_gain, k_gain, seg, *dqs, *dks, *dvs)


def rel_bias_grad(dbs, name):
    (bq, vq), _ = _band_tables()
    onehot = np.zeros((len(DILATIONS), REL_BUCKETS, BAND * 2 * BAND), np.float32)
    for g in range(len(DILATIONS)):
        idx = bq[g].reshape(-1)
        ok = vq.reshape(-1)
        onehot[g, idx[ok], np.nonzero(ok)[0]] = 1.0
    onehot = jnp.asarray(onehot, BF16)
    flat = [d.reshape(N_HEADS, BAND * 2 * BAND) for d in dbs]

    def body(oh_ref, d1, d2, d3, o_ref):
        acc = jnp.zeros((REL_BUCKETS, N_HEADS), F32)
        for g, d in enumerate((d1, d2, d3)):
            x = d[...]
            hi = x.astype(BF16)
            lo = (x - hi.astype(F32)).astype(BF16)
            acc += _dot_nt(oh_ref[g], hi) + _dot_nt(oh_ref[g], lo)
        o_ref[...] = acc

    full = lambda shp: _bs(shp, lambda: tuple(0 for _ in shp))
    return pl.pallas_call(
        body, name=name,
        in_specs=[full(onehot.shape)] + [full(flat[0].shape)] * 3,
        out_specs=full((REL_BUCKETS, N_HEADS)),
        out_shape=S((REL_BUCKETS, N_HEADS), F32),
        compiler_params=pltpu.CompilerParams(vmem_limit_bytes=VMEM_LIMIT),
    )(onehot, *flat)


def _lru_gates(xb, wa_ref, wx_ref, ba, bx):
    xb16 = xb.astype(BF16)
    ga = jnp.concatenate([_dot(xb16[:, LRU_BLOCK * g:LRU_BLOCK * (g + 1)], wa_ref[g]) for g in range(LRU_BLOCKS)], axis=1) + ba
    gx = jnp.concatenate([_dot(xb16[:, LRU_BLOCK * g:LRU_BLOCK * (g + 1)], wx_ref[g]) for g in range(LRU_BLOCKS)], axis=1) + bx
    return ga, gx


def _lru_coeffs(ga, gx, lam):
    sga = _sigmoid(ga)
    sp = _softplus(-lam)
    log_a = -LRU_C * sga * sp
    a = jnp.exp(log_a)
    one_m_a2 = _neg_expm1(2.0 * log_a)
    return sga, sp, a, one_m_a2, jnp.sqrt(one_m_a2), _sigmoid(gx)


def rec_fwd(z, conv_w, conv_b, wa, wx, ba, bx, lam, name):
    t = z.shape[0]
    w = z.shape[1] // 2
    tm = _row_tile(t, 256)

    def body(xp_ref, xh_ref, yb_ref, cw_ref, cb_ref, wa_ref, wx_ref, ba_ref, bx_ref, lam_ref,
             xb_ref, ga_ref, gx_ref, hs_ref, out_ref, carry):
        i = pl.program_id(0)

        @pl.when(i == 0)
        def _():
            carry[...] = jnp.zeros_like(carry)
        xp = xp_ref[...]
        xh = jnp.where(i == 0, 0.0, xh_ref[...])
        xb = cb_ref[...] + cw_ref[3:4, :] * xp
        for j in range(3):
            xb = xb + cw_ref[j:j + 1, :] * _shift_down(xp, xh, 3 - j)
        ga, gx = _lru_gates(xb, wa_ref, wx_ref, ba_ref[...], bx_ref[...])
        _, _, a, _, sq, sgx = _lru_coeffs(ga, gx, lam_ref[...])
        aa, bb = a, sq * sgx * xb
        s = 1
        while s < tm:
            bb = aa * _roll_fill(bb, s, 0.0, False) + bb
            aa = aa * _roll_fill(aa, s, 1.0, False)
            s *= 2
        hs = aa * carry[0:1, :] + bb
        xb_ref[...] = xb
        ga_ref[...] = ga
        gx_ref[...] = gx
        hs_ref[...] = hs
        carry[0:1, :] = hs_ref[tm - 1:tm, :]
        gy, _ = _gelu_and_grad(yb_ref[...])
        out_ref[...] = (hs * gy).astype(BF16)

    hb = tm // 8
    row = _bs((tm, w), lambda i: (i, 0))
    vec = _bs((1, w), lambda i: (0, 0))
    wsp = _bs((LRU_BLOCKS, LRU_BLOCK, LRU_BLOCK), lambda i: (0, 0, 0))
    return pl.pallas_call(
        body, name=name, grid=(t // tm,),
        in_specs=[row, _bs((8, w), lambda i: (jnp.maximum(i * hb - 1, 0), 0)), _bs((tm, w), lambda i: (i, 1)),
                  _bs((4, w), lambda i: (0, 0)), vec, wsp, wsp, vec, vec, vec],
        out_specs=[row] * 5,
        out_shape=[S((t, w), F32)] * 4 + [S((t, w), BF16)],
        scratch_shapes=[pltpu.VMEM((8, w), F32)],
        compiler_params=_cp("arbitrary"),
    )(z, z, z, conv_w, conv_b, wa, wx, ba, bx, lam)


def rec_bwd(d_out, z, xb, ga, gx, hs, conv_w, wa, wx, lam, name):
    t = z.shape[0]
    w = z.shape[1] // 2
    tm = _row_tile(t, 256)
    nt = t // tm

    def body(do_ref, xp_ref, xph_ref, yb_ref, xb_ref, ga_ref, gx_ref, hs_ref, hsh_ref, cw_ref, wa_ref, wx_ref, lam_ref,
             dz_ref, dga_ref, dgx_ref, sm_ref, c_lam, c_a, c_dxb):
        i = pl.program_id(0)

        @pl.when(i == 0)
        def _():
            sm_ref[...] = jnp.zeros_like(sm_ref)
            c_lam[...] = jnp.zeros_like(c_lam)
            c_a[...] = jnp.zeros_like(c_a)
            c_dxb[...] = jnp.zeros_like(c_dxb)
        d_o, yb, xb, hs = do_ref[...], yb_ref[...], xb_ref[...], hs_ref[...]
        lam = lam_ref[...]
        gy, dgy = _gelu_and_grad(yb)
        dz_ref[:, w:2 * w] = (d_o * hs * dgy).astype(BF16)
        sga, sp, a, one_m_a2, sq, sgx = _lru_coeffs(ga_ref[...], gx_ref[...], lam)
        aa = _shift_up(a, c_a[...], 1)
        bb = d_o * gy
        s = 1
        while s < tm:
            bb = aa * _roll_fill(bb, s, 0.0, True) + bb
            aa = aa * _roll_fill(aa, s, 1.0, True)
            s *= 2
        lmb = aa * c_lam[0:1, :] + bb
        c_a[...] = a[0:8, :]
        c_lam[...] = lmb[0:8, :]
        hprev = _shift_down(hs, jnp.where(i == nt - 1, 0.0, hsh_ref[...]), 1)
        d_sq = lmb * sgx * xb
        d_sgx = lmb * sq * xb
        d_log_a = lmb * hprev * a - d_sq * (1.0 - one_m_a2) / sq
        dga = d_log_a * (-LRU_C * sp) * sga * (1.0 - sga)
        dgx = d_sgx * sgx * (1.0 - sgx)
        dga16, dgx16 = dga.astype(BF16), dgx.astype(BF16)
        dga_ref[...] = dga16
        dgx_ref[...] = dgx16
        dxb = lmb * sq * sgx + jnp.concatenate(
            [_dot_nt(dga16[:, LRU_BLOCK * g:LRU_BLOCK * (g + 1)], wa_ref[g]) + _dot_nt(dgx16[:, LRU_BLOCK * g:LRU_BLOCK * (g + 1)], wx_ref[g])
             for g in range(LRU_BLOCKS)], axis=1)
        nxt = c_dxb[...]
        dxp = cw_ref[3:4, :] * dxb
        for j in range(3):
            dxp = dxp + cw_ref[j:j + 1, :] * _shift_up(dxb, nxt, 3 - j)
        c_dxb[...] = dxb[0:8, :]
        dz_ref[:, 0:w] = dxp.astype(BF16)
        xp = xp_ref[...]
        xph = jnp.where(i == nt - 1, 0.0, xph_ref[...])
        sm_ref[0:1, :] += jnp.sum(dga, axis=0, keepdims=True)
        sm_ref[1:2, :] += jnp.sum(dgx, axis=0, keepdims=True)
        sm_ref[2:3, :] += jnp.sum(d_log_a * (-LRU_C * sga), axis=0, keepdims=True) * (-_sigmoid(-lam))
        sm_ref[3:4, :] += jnp.sum(dxb, axis=0, keepdims=True)
        for j in range(4):
            sm_ref[4 + j:5 + j, :] += jnp.sum(dxb * _shift_down(xp, xph, 3 - j), axis=0, keepdims=True)

    hb = tm // 8
    rev = lambda c: _bs((tm, w), lambda i: (nt - 1 - i, c))
    halo = lambda c: _bs((8, w), lambda i: (jnp.maximum((nt - 1 - i) * hb - 1, 0), c))
    vec = _bs((1, w), lambda i: (0, 0))
    wsp = _bs((LRU_BLOCKS, LRU_BLOCK, LRU_BLOCK), lambda i: (0, 0, 0))
    return pl.pallas_call(
        body, name=name, grid=(nt,),
        in_specs=[rev(0), rev(0), halo(0), rev(1), rev(0), rev(0), rev(0), rev(0), halo(0),
                  _bs((4, w), lambda i: (0, 0)), wsp, wsp, vec],
        out_specs=[_bs((tm, 2 * w), lambda i: (nt - 1 - i, 0)), rev(0), rev(0), _bs((8, w), lambda i: (0, 0))],
        out_shape=[S((t, 2 * w), BF16), S((t, w), BF16), S((t, w), BF16), S((8, w), F32)],
        scratch_shapes=[pltpu.VMEM((8, w), F32)] * 3,
        compiler_params=_cp("arbitrary"),
    )(d_out, z, z, z, xb, ga, gx, hs, hs, conv_w, wa, wx, lam)


def ple_fwd(h, gain, wpg, layer, p, wpp, name):
    t, d = h.shape
    pd = p.shape[1]
    nk, _, rb, _ = wpg.shape
    cb = wpp.shape[3]
    tm = _row_tile(t, 512)

    def body(h_ref, g_ref, wg_ref, p_ref, wp_ref, o_ref, hn_ref, gp_ref, pp_ref):
        x = h_ref[...]
        hn = (x * _rstd(x) * g_ref[...]).astype(BF16)
        gp = _dot(hn[:, 0:rb], wg_ref[0])
        for k in range(1, nk):
            gp = gp + _dot(hn[:, rb * k:rb * (k + 1)], wg_ref[k])
        p16 = p_ref[...].astype(BF16)
        pp = jnp.concatenate([_dot(p16, wp_ref[k]) for k in range(nk)], axis=1)
        hn_ref[...] = hn
        gp_ref[...] = gp
        pp_ref[...] = pp
        o_ref[...] = x + _sigmoid(gp) * pp

    row = _bs((tm, d), lambda i: (i, 0))
    return pl.pallas_call(
        body, name=name, grid=(t // tm,),
        in_specs=[row, _bs((1, d), lambda i: (0, 0)), _bs((nk, None, rb, d), lambda i: (0, layer, 0, 0)),
                  _bs((tm, pd), lambda i: (i, 0)), _bs((nk, None, pd, cb), lambda i: (0, layer, 0, 0))],
        out_specs=[row] * 4,
        out_shape=[S((t, d), F32), S((t, d), BF16), S((t, d), F32), S((t, d), F32)],
        compiler_params=_cp("parallel"),
    )(h, gain, wpg, p, wpp)


def ple_bwd(dh, h, gain, wpg, layer, gp, pp, name):
    t, d = h.shape
    nk, _, rb, _ = wpg.shape
    tm = _row_tile(t, 512)

    def body(dh_ref, h_ref, g_ref, wg_ref, gp_ref, pp_ref, o_ref, dgp_ref, dpp_ref, dg_ref):
        @pl.when(pl.program_id(0) == 0)
        def _():
            dg_ref[...] = jnp.zeros_like(dg_ref)
        d_h = dh_ref[...]
        gate = _sigmoid(gp_ref[...])
        dgp = (d_h * pp_ref[...] * gate * (1.0 - gate)).astype(BF16)
        dgp_ref[...] = dgp
        dpp_ref[...] = (d_h * gate).astype(BF16)
        dhn = jnp.concatenate([_dot_nt(dgp, wg_ref[k]) for k in range(nk)], axis=1)
        dx, dgain = _rmsnorm_bwd(h_ref[...], g_ref[...], dhn)
        o_ref[...] = d_h + dx
        dg_ref[...] += dgain

    row = _bs((tm, d), lambda i: (i, 0))
    vec = _bs((1, d), lambda i: (0, 0))
    return pl.pallas_call(
        body, name=name, grid=(t // tm,),
        in_specs=[row, row, vec, _bs((nk, None, rb, d), lambda i: (0, layer, 0, 0)), row, row],
        out_specs=[row, row, row, vec],
        out_shape=[S((t, d), F32), S((t, d), BF16), S((t, d), BF16), S((1, d), F32)],
        compiler_params=_cp("arbitrary"),
    )(dh, h, gain, wpg, gp, pp)


def loss_and_grad(y, target, name):
    t, d = y.shape
    tm = _row_tile(t, 512)

    def body(y_ref, t_ref, l_ref, g_ref):
        @pl.when(pl.program_id(0) == 0)
        def _():
            l_ref[...] = jnp.zeros_like(l_ref)
        err = y_ref[...] - t_ref[...]
        g_ref[...] = err * (1.0 / d)
        l_ref[...] += jnp.sum(jnp.sum(err * err, axis=1, keepdims=True), axis=0, keepdims=True) * (0.5 / d)

    row = _bs((tm, d), lambda i: (i, 0))
    return pl.pallas_call(
        body, name=name, grid=(t // tm,),
        in_specs=[row, row],
        out_specs=[_bs((1, 1), lambda i: (0, 0)), row],
        out_shape=[S((1, 1), F32), S((t, d), F32)],
        compiler_params=_cp("arbitrary"),
    )(y, target)


def _vec(a, i):
    return a[i:i + 1]


def local_step(x, p, target, w, plan=None):
    t = x.shape[0]
    tm = _row_tile(t, 512)
    grads = {}
    if plan is not None:
        plan.grads = grads
    saved = []
    h = x
    bias_q, bias_k = band_bias(w["rel_bias"])
    qg = jnp.tile(w["hyb_q_gain"], (1, N_HEADS))
    kg = jnp.tile(w["hyb_k_gain"], (1, N_HEADS))

    def run(fn, *a, name):
        hst = plan.host(name) if plan is not None else None
        out = fn(*a, name, hst)
        if hst is not None:
            plan.done(hst)
        return out

    def lru_blocks(n):
        return jnp.transpose(w[n].reshape(N_SHARD, LRU_BLOCKS, 64, LRU_BLOCK), (1, 0, 2, 3)).reshape(LRU_BLOCKS, LRU_BLOCK, LRU_BLOCK)

    for i in range(2):
        s = {}
        s["h0"] = h
        s["hn1"], s["g1"], s["u1"], s["a1"] = run(ffn_up, h, _vec(w["ffn1_norm"], i), w[f"ffn1_w_gate/{i}"], w[f"ffn1_w_up/{i}"], 0, name=f"ffn1_up_{i}")
        h = run(ffn_down, s["a1"], w[f"ffn1_w_down/{i}"], 0, h, name=f"ffn1_down_{i}")
        s["h1"] = h
        if i == 0:
            w_hyb_in = w["hyb_w_in"].reshape(N_SHARD, D_MODEL, -1)
            w_hyb_out = w["hyb_w_out"].reshape(D_MODEL, D_MODEL)
            s["hnm"], s["z"] = run(norm_mm, h, _vec(w["mix_norm"], i), w_hyb_in, name="hyb_in")
            s["qkv"] = hyb_prep(s["z"], qg, kg, "hyb_prep")
            os_, lses = [], []
            for g, dil in enumerate(DILATIONS):
                o, l = attn_fwd(*s["qkv"][dil], bias_q[g], dil, f"attn_fwd_{dil}")
                os_.append(o)
                lses.append(l)
            s["y2"], s["ya"], s["lt"] = hyb_post(s["z"], w["hyb_conv_w"], os_, lses, "hyb_post")
            h = run(mm_acc, s["y2"], _bs((None, tm, ATTN_W), lambda r, k: (k, r, 0)),
                    w_hyb_out.reshape(2, ATTN_W, D_MODEL), _bs((None, ATTN_W, D_MODEL), lambda r, k: (k, 0, 0)),
                    h, 1.0, 2, t, D_MODEL, tm, name="hyb_out")
        else:
            w_rec_in = w["rec_w_in"].reshape(N_SHARD, D_MODEL, -1)
            s["hnm"], s["z"] = run(norm_mm, h, _vec(w["mix_norm"], i), w_rec_in, name="rec_in")
            w_rec_out = w["rec_w_out"].reshape(D_MODEL, D_MODEL)
            lru_wa, lru_wx = lru_blocks("lru_wa"), lru_blocks("lru_wx")
            s["xb"], s["ga"], s["gx"], s["hs"], s["ro"] = rec_fwd(
                s["z"], w["rec_conv_w"], w["rec_conv_b"], lru_wa, lru_wx, w["lru_ba"], w["lru_bx"], w["lru_lambda"], "rec_fwd")
            h = run(mm_acc, s["ro"], _bs((tm, D_MODEL), lambda r, k: (r, 0)), w_rec_out, _bs((D_MODEL, D_MODEL), lambda r, k: (0, 0)),
                    h, 1.0, 1, t, D_MODEL, tm, name="rec_out")
        s["h2"] = h
        s["hn2"], s["g2"], s["u2"], s["a2"] = run(ffn_up, h, _vec(w["ffn2_norm"], i), w[f"ffn2_w_gate/{i}"], w[f"ffn2_w_up/{i}"], 0, name=f"ffn2_up_{i}")
        h = run(ffn_down, s["a2"], w[f"ffn2_w_down/{i}"], 0, h, name=f"ffn2_down_{i}")
        s["h3"] = h
        h, s["hnp"], s["gp"], s["pp"] = ple_fwd(h, _vec(w["ple_norm"], i), w[f"ple_w_gate/{i}"], 0, p[i], w[f"ple_w_proj/{i}"], f"ple_fwd_{i}")
        saved.append(s)
    loss, dh = loss_and_grad(h, target, "loss")

    norm_g = {n: [None, None] for n in ("ffn1_norm", "mix_norm", "ffn2_norm", "ple_norm")}
    for i in (1, 0):
        s = saved[i]
        dh_out = dh
        dh, dgp, dpp, norm_g["ple_norm"][i] = ple_bwd(dh_out, s["h3"], _vec(w["ple_norm"], i), w[f"ple_w_gate/{i}"], 0, s["gp"], s["pp"], f"ple_bwd_{i}")
        grads["ple_w_gate"] = run(tn_mm, s["hnp"], lambda tk: _bs((tk, 256), lambda k, j: (j, k)), dgp, lambda tk: _bs((tk, D_MODEL), lambda k, j: (j, 0)),
                                  N_SHARD, t, 256, D_MODEL, S((N_SHARD, 2, 256, D_MODEL), BF16),
                                  _bs((None, None, 256, D_MODEL), lambda k, j, i=i: (k, i, 0, 0)), 1.0, grads.get("ple_w_gate"), name=f"ple_gw_gate_{i}")
        grads["ple_w_proj"] = run(tn_mm, p[i], lambda tk: _bs((tk, 256), lambda k, j: (j, 0)), dpp, lambda tk: _bs((tk, 256), lambda k, j: (j, k)),
                                  N_SHARD, t, 256, 256, S((N_SHARD, 2, 256, 256), BF16),
                                  _bs((None, None, 256, 256), lambda k, j, i=i: (k, i, 0, 0)), 1.0, grads.get("ple_w_proj"), name=f"ple_gw_proj_{i}")
        dh_out = dh
        dg, du = run(ffn_bwd_act, dh_out, w[f"ffn2_w_down/{i}"], 0, s["g2"], s["u2"], name=f"ffn2_bwd_act_{i}")
        ffn_wgrads("ffn2", s["hn2"], dh_out, s["a2"], dg, du, i, grads, run)
        dh, norm_g["ffn2_norm"][i] = run(ffn_bwd_in, dg, du, w[f"ffn2_w_gate/{i}"], w[f"ffn2_w_up/{i}"], 0, s["h2"], _vec(w["ffn2_norm"], i), dh_out,
                                         name=f"ffn2_bwd_in_{i}")
        dh_out = dh
        if i == 1:
            d_o = nt_mm(dh_out, w_rec_out, "rec_bwd_out")
            grads["rec_w_out"] = run(tn_mm, s["ro"], lambda tk: _bs((tk, 256), lambda k, j: (j, k)), dh_out, lambda tk: _bs((tk, D_MODEL), lambda k, j: (j, 0)),
                                     N_SHARD, t, 256, D_MODEL, S((N_SHARD, 256, D_MODEL), BF16), _bs((None, 256, D_MODEL), lambda k, j: (k, 0, 0)),
                                     1.0, None, name="rec_gw_out").reshape(N_SHARD, 1, 256, D_MODEL)
            dz, dga, dgx, small = rec_bwd(d_o, s["z"], s["xb"], s["ga"], s["gx"], s["hs"], w["rec_conv_w"], lru_wa, lru_wx, w["lru_lambda"], "rec_bwd")
            blk = lambda tk: _bs((tk, LRU_BLOCK), lambda k, j: (j, k))
            for nm, dgt in (("lru_wa", dga), ("lru_wx", dgx)):
                gw = run(tn_mm, s["xb"], blk, dgt, blk, LRU_BLOCKS, t, LRU_BLOCK, LRU_BLOCK, S((LRU_BLOCKS, LRU_BLOCK, LRU_BLOCK), BF16),
                         _bs((None, LRU_BLOCK, LRU_BLOCK), lambda k, j: (k, 0, 0)), 1.0, None, name="rec_gw_" + nm)
                grads[nm] = jnp.transpose(gw.reshape(LRU_BLOCKS, N_SHARD, 64, LRU_BLOCK), (1, 0, 2, 3)).reshape(N_SHARD, 1, LRU_BLOCKS, 64, LRU_BLOCK)
            grads["lru_ba"], grads["lru_bx"], grads["lru_lambda"], grads["rec_conv_b"] = (small[r:r + 1] for r in range(4))
            grads["rec_conv_w"] = small[4:8]
            nb_, bw = N_SHARD, 512
            w_in, nm_in = w_rec_in, "rec_w_in"
        else:
            dy = nt_mm(dh_out, w_hyb_out, "hyb_bwd_out")
            grads["hyb_w_out"] = run(tn_mm, s["y2"], lambda tk: _bs((None, tk, 256), lambda k, j: (k // 2, j, k % 2)), dh_out,
                                     lambda tk: _bs((tk, D_MODEL), lambda k, j: (j, 0)),
                                     N_SHARD, t, 256, D_MODEL, S((N_SHARD, 256, D_MODEL), BF16), _bs((None, 256, D_MODEL), lambda k, j: (k, 0, 0)),
                                     1.0, None, name="hyb_gw_out").reshape(N_SHARD, 1, 256, D_MODEL)
            delta, dya = attn_delta(dy, s["ya"], "attn_delta")
            dqs, dks, dvs, dbs = [], [], [], []
            for g, dil in enumerate(DILATIONS):
                dq, db = attn_bwd_dq(*s["qkv"][dil], dya[dil], s["lt"][dil], delta[dil], bias_q[g], dil, f"attn_bwd_dq_{dil}")
                dk, dv = attn_bwd_dkv(*s["qkv"][dil], dya[dil], s["lt"][dil], delta[dil], bias_k[g], dil, f"attn_bwd_dkv_{dil}")
                dqs.append(dq); dks.append(dk); dvs.append(dv); dbs.append(db)
            grads["rel_bias"] = rel_bias_grad(dbs, "rel_bias_grad")
            dz, grads["hyb_conv_w"], dqg, dkg = hyb_dz(s["z"], dy, w["hyb_conv_w"], qg, kg, dqs, dks, dvs, "hyb_dz")
            grads["hyb_q_gain"] = jnp.sum(dqg.reshape(N_HEADS, HEAD_DIM), axis=0, keepdims=True)
            grads["hyb_k_gain"] = jnp.sum(dkg.reshape(N_HEADS, HEAD_DIM), axis=0, keepdims=True)
            nb_, bw = N_SHARD, 768
            w_in, nm_in = w_hyb_in, "hyb_w_in"
        grads[nm_in] = run(tn_mm, s["hnm"], lambda tk: _bs((tk, D_MODEL), lambda k, j: (j, 0)), dz, lambda tk, bw=bw: _bs((tk, bw), lambda k, j: (j, k)),
                           nb_, t, D_MODEL, bw, S((nb_, D_MODEL, bw), BF16), _bs((None, D_MODEL, bw), lambda k, j: (k, 0, 0)),
                           1.0, None, name=f"mix_gw_in_{i}").reshape(nb_, 1, D_MODEL, bw)
        dh, norm_g["mix_norm"][i] = run(
            nt_acc_normbwd, [(dz, _bs((_row_tile(t, BWD_IN_ROWS), bw), lambda r, k: (r, k)), w_in, _bs((None, D_MODEL, bw), lambda r, k: (k, 0, 0)))],
            nb_, s["h1"], _vec(w["mix_norm"], i), dh_out, name=f"mix_bwd_in_{i}")
        dh_out = dh
        dg, du = run(ffn_bwd_act, dh_out, w[f"ffn1_w_down/{i}"], 0, s["g1"], s["u1"], name=f"ffn1_bwd_act_{i}")
        ffn_wgrads("ffn1", s["hn1"], dh_out, s["a1"], dg, du, i, grads, run)
        dh, norm_g["ffn1_norm"][i] = run(ffn_bwd_in, dg, du, w[f"ffn1_w_gate/{i}"], w[f"ffn1_w_up/{i}"], 0, s["h0"], _vec(w["ffn1_norm"], i), dh_out,
                                         name=f"ffn1_bwd_in_{i}")
    for n, (g0, g1) in norm_g.items():
        grads[n] = jnp.concatenate([g0, g1], axis=0)
    return loss, dh, grads


def gather_weights(shards, name):
    n = len(shards)

    def body(*refs):
        ins, outs = refs[:n], refs[n:2 * n]
        send1, recv1, send2, recv2, lsem = refs[2 * n:]
        x, y, c, k, chips, kk = _place()
        sib = (x, y, 1 - c)

        def remote(src, dst, ssem, rsem, to):
            return pltpu.make_async_remote_copy(src_ref=src, dst_ref=dst, send_sem=ssem, recv_sem=rsem, device_id=to, device_id_type=MESH)

        local = [pltpu.make_async_copy(ins[a], outs[a].at[k], lsem.at[a]) for a in range(n)]
        for cp in local:
            cp.start()
        sends = []
        for a in range(n):
            for j, chip in enumerate(chips):
                cp = remote(ins[a].at[c], outs[a].at[k, c], send1.at[3 * a + j], recv1.at[3 * a + j], (*chip, c))
                cp.start()
                sends.append(cp)
        for a in range(n):
            for j, chip in enumerate(chips):
                remote(ins[a].at[c], outs[a].at[kk[j], c], send1.at[3 * a + j], recv1.at[3 * a + j], (*chip, c)).wait_recv()
                cp = remote(outs[a].at[kk[j], c], outs[a].at[kk[j], c], send2.at[3 * a + j], recv2.at[3 * a + j], sib)
                cp.start()
                sends.append(cp)
        for a in range(n):
            for j in range(3):
                remote(outs[a].at[kk[j], 1 - c], outs[a].at[kk[j], 1 - c], send2.at[3 * a + j], recv2.at[3 * a + j], sib).wait_recv()
        for cp in sends:
            cp.wait_send()
        for cp in local:
            cp.wait()

    return pl.pallas_call(
        body, name=name,
        in_specs=[_ANY] * n, out_specs=[_ANY] * n,
        out_shape=[S((N_SHARD,) + s.shape, s.dtype) for s in shards],
        scratch_shapes=[pltpu.SemaphoreType.DMA((3 * n,))] * 4 + [pltpu.SemaphoreType.DMA((n,))],
    )(*shards)


def exchange_cores(rs, name):
    n = len(rs)

    def body(*refs):
        outs = refs[n:2 * n]
        send, recv = refs[2 * n:]
        x, y, c = lax.axis_index("x"), lax.axis_index("y"), lax.axis_index("c")
        sends = []
        for a in range(n):
            for k in range(N_SHARD):
                slot = outs[a].at[2 * k + c]
                cp = _remote(slot, slot, send.at[N_SHARD * a + k], recv.at[N_SHARD * a + k], (x, y, 1 - c))
                cp.start()
                sends.append(cp)
        for a in range(n):
            for k in range(N_SHARD):
                slot = outs[a].at[2 * k + 1 - c]
                _remote(slot, slot, send.at[N_SHARD * a + k], recv.at[N_SHARD * a + k], (x, y, 1 - c)).wait_recv()
        for cp in sends:
            cp.wait_send()

    return pl.pallas_call(
        body, name=name,
        in_specs=[_ANY] * n, out_specs=[_ANY] * n,
        out_shape=[S(r.shape, r.dtype) for r in rs],
        input_output_aliases={a: a for a in range(n)},
        scratch_shapes=[pltpu.SemaphoreType.DMA((N_SHARD * n,))] * 2,
    )(*rs)


def allgather8(a, name):
    def body(a_ref, o_ref, send, recv, lsem):
        x, y, c = lax.axis_index("x"), lax.axis_index("y"), lax.axis_index("c")
        me = 4 * x + 2 * y + c
        local = pltpu.make_async_copy(a_ref, o_ref.at[me], lsem)
        local.start()
        cps = []
        for f in range(1, N_DEV):
            fx, fy, fc = (f >> 2) & 1, (f >> 1) & 1, f & 1
            peer = (1 - x if fx else x, 1 - y if fy else y, 1 - c if fc else c)
            cp = pltpu.make_async_remote_copy(src_ref=a_ref, dst_ref=o_ref.at[me], send_sem=send.at[f - 1], recv_sem=recv.at[f - 1],
                                              device_id=peer, device_id_type=MESH)
            cp.start()
            cps.append((cp, 4 * peer[0] + 2 * peer[1] + peer[2], f))
        for cp, pidx, f in cps:
            pltpu.make_async_remote_copy(src_ref=a_ref, dst_ref=o_ref.at[pidx], send_sem=send.at[f - 1], recv_sem=recv.at[f - 1],
                                         device_id=(x, y, c), device_id_type=MESH).wait_recv()
        for cp, _, _ in cps:
            cp.wait_send()
        local.wait()

    return pl.pallas_call(
        body, name=name, in_specs=[_ANY], out_specs=_ANY,
        out_shape=S((N_DEV,) + a.shape, a.dtype),
        scratch_shapes=[pltpu.SemaphoreType.DMA((N_DEV - 1,)), pltpu.SemaphoreType.DMA((N_DEV - 1,)), pltpu.SemaphoreType.DMA],
    )(a)


def sum8(a, name):
    _, r, c = a.shape

    def body(a_ref, o_ref):
        acc = a_ref[0]
        for j in range(1, N_DEV):
            acc = acc + a_ref[j]
        o_ref[...] = acc

    return pl.pallas_call(
        body, name=name, in_specs=[_bs((N_DEV, r, c), lambda: (0, 0, 0))], out_specs=_bs((r, c), lambda: (0, 0)),
        out_shape=S((r, c), F32),
    )(a)


def adamw(w, m, v, g, name):
    r, c = w.shape
    tr = _row_tile(r, 256)
    summed = g.ndim == 3

    def body(w_ref, m_ref, v_ref, g_ref, go_ref, d_ref, mo_ref, vo_ref):
        if summed:
            gr = g_ref[0].astype(F32)
            for j in range(1, N_DEV):
                gr = gr + g_ref[j].astype(F32)
        else:
            gr = g_ref[...]
        m_new = ADAM_B1 * m_ref[...] + (1.0 - ADAM_B1) * gr
        v_new = ADAM_B2 * v_ref[...] + (1.0 - ADAM_B2) * (gr * gr)
        m_hat = m_new / (1.0 - ADAM_B1 ** ADAM_STEP)
        v_hat = v_new / (1.0 - ADAM_B2 ** ADAM_STEP)
        go_ref[...] = gr
        d_ref[...] = -ADAM_LR * (m_hat / (jnp.sqrt(v_hat) + ADAM_EPS) + ADAM_WD * w_ref[...])
        mo_ref[...] = m_new
        vo_ref[...] = v_new

    row = _bs((tr, c), lambda i: (i, 0))
    gspec = _bs((N_DEV, tr, c), lambda i: (0, i, 0)) if summed else row
    return pl.pallas_call(
        body, name=name, grid=(r // tr,),
        in_specs=[row, row, row, gspec], out_specs=[row] * 4, out_shape=[S((r, c), F32)] * 4,
        compiler_params=_cp("parallel"),
    )(w, m, v, g)


WEIGHTS = ["rel_bias", "ffn1_norm", "ffn1_w_gate", "ffn1_w_up", "ffn1_w_down", "mix_norm", "hyb_w_in", "hyb_conv_w", "hyb_q_gain",
           "hyb_k_gain", "hyb_w_out", "rec_w_in", "rec_conv_w", "rec_conv_b", "lru_wa", "lru_ba", "lru_wx", "lru_bx", "lru_lambda",
           "rec_w_out", "ffn2_norm", "ffn2_w_gate", "ffn2_w_up", "ffn2_w_down", "ple_norm", "ple_w_gate", "ple_w_proj"]
BIG = ["ffn1_w_gate", "ffn1_w_up", "ffn1_w_down", "hyb_w_in", "hyb_w_out", "rec_w_in", "lru_wa", "lru_wx", "rec_w_out",
       "ffn2_w_gate", "ffn2_w_up", "ffn2_w_down", "ple_w_gate", "ple_w_proj"]
SMALL_SHARDED = ["hyb_conv_w", "rec_conv_w", "rec_conv_b", "lru_ba", "lru_bx", "lru_lambda"]
SMALL = ["rel_bias", "ffn1_norm", "mix_norm", "ffn2_norm", "ple_norm", "hyb_q_gain", "hyb_k_gain"] + SMALL_SHARDED
PACK_W = 1024
PER_LAYER = ["ffn1_w_gate", "ffn1_w_up", "ffn1_w_down", "ffn2_w_gate", "ffn2_w_up", "ffn2_w_down", "ple_w_gate", "ple_w_proj"]
FIRST = ["ffn1_w_gate/0", "ffn1_w_up/0"]
LAST = ["ffn1_w_down"]
GATHER_PLAN = {
    "ffn1_up_0": ["ffn1_w_down/0", "hyb_w_in"],
    "ffn1_down_0": ["hyb_w_out", "ple_w_gate/0", "ple_w_proj/0"],
    "hyb_in": ["ffn2_w_gate/0", "ffn2_w_up/0"],
    "hyb_out": ["ffn2_w_down/0"],
    "ffn2_up_0": ["ffn1_w_gate/1", "ffn1_w_up/1"],
    "ffn2_down_0": ["ffn1_w_down/1"],
    "ffn1_up_1": ["rec_w_in", "lru_wa", "lru_wx", "rec_w_out", "ffn2_w_gate/1"],
    "ffn1_down_1": ["ffn2_w_up/1"],
    "rec_in": ["ffn2_w_down/1", "ple_w_gate/1", "ple_w_proj/1"],
}
SCATTER_PLAN = {
    "ple_gw_proj_1": [("ple_w_gate", 1)],
    "ffn2_bwd_act_1": [("ple_w_proj", 1)],
    "ffn2_gw_1_up": [("ffn2_w_gate", 1)],
    "ffn2_gw_1_down": [("ffn2_w_up", 1)],
    "ffn2_bwd_in_1": [("ffn2_w_down", 1)],
    "mix_bwd_in_1": [("rec_w_in", 0), ("rec_w_out", 0), ("lru_wa", 0), ("lru_wx", 0)],
    "ffn1_gw_1_up": [("ffn1_w_gate", 1)],
    "ffn1_gw_1_down": [("ffn1_w_up", 1)],
    "ffn1_bwd_in_1": [("ffn1_w_down", 1)],
    "ple_gw_proj_0": [("ple_w_gate", 0)],
    "ffn2_bwd_act_0": [("ple_w_proj", 0)],
    "ffn2_gw_0_up": [("ffn2_w_gate", 0)],
    "ffn2_gw_0_down": [("ffn2_w_up", 0)],
    "ffn2_bwd_in_0": [("ffn2_w_down", 0)],
    "mix_bwd_in_0": [("hyb_w_in", 0), ("hyb_w_out", 0)],
    "ffn1_gw_0_up": [("ffn1_w_gate", 0)],
    "ffn1_gw_0_down": [("ffn1_w_up", 0)],
    "ffn1_bwd_in_0": [("ffn1_w_down", 0)],
}
FORWARD_PLAN = {
    "ffn1_gw_1_up": ["rec_w_in", "rec_w_out", "lru_wa", "lru_wx"],
    "ffn2_bwd_in_0": ["ple_w_gate", "ple_w_proj", "ffn2_w_gate", "ffn2_w_up"],
    "mix_bwd_in_0": ["ffn2_w_down"],
    "ffn1_gw_0_up": ["hyb_w_in", "hyb_w_out"],
    "ffn1_bwd_in_0": ["ffn1_w_gate", "ffn1_w_up"],
}


class Plan:
    def __init__(self, shards, w):
        self.shards, self.w, self.grads, self.landed = shards, w, None, {}

    def host(self, kname):
        if kname in GATHER_PLAN:
            h = Host("gather", [self.shards[n] for n in GATHER_PLAN[kname]])
            h.names = GATHER_PLAN[kname]
            return h
        if kname in SCATTER_PLAN or kname in FORWARD_PLAN:
            items = SCATTER_PLAN.get(kname, [])
            fwd = FORWARD_PLAN.get(kname, [])
            h = Host("scatter", [(self.grads[n], lay, self.landed.get(n)) for n, lay in items], [self.landed[n] for n in fwd])
            h.names = [n for n, _ in items] + fwd
            return h
        return None

    def done(self, h):
        for n, o in zip(h.names, h.outs):
            if h.kind == "gather":
                self.w[n] = o
            else:
                self.landed[n] = o


def _halves(a):
    if a.shape[0] == 2:
        return a
    return a.reshape((2, a.shape[1] // 2) + a.shape[2:])


def _pack_rows(arrs, width):
    rows, offs, r0 = [], [], 0
    for a in arrs:
        if a.shape[1] > width:
            a = a.reshape(-1, width)
        rows.append(jnp.pad(a, ((0, 0), (0, width - a.shape[1]))))
        offs.append(r0)
        r0 += a.shape[0]
    pad = (-r0) % 8
    if pad:
        rows.append(jnp.zeros((pad, width), F32))
    return jnp.concatenate(rows, axis=0), offs


def kernel(x, p, rel_bias, ffn1_norm, ffn1_w_gate, ffn1_w_up, ffn1_w_down, mix_norm, hyb_w_in, hyb_conv_w, hyb_q_gain, hyb_k_gain, hyb_w_out, rec_w_in, rec_conv_w, rec_conv_b, lru_wa, lru_ba, lru_wx, lru_bx, lru_lambda, rec_w_out, ffn2_norm, ffn2_w_gate, ffn2_w_up, ffn2_w_down, ple_norm, ple_w_gate, ple_w_proj, loss_target, m_rel_bias, m_ffn1_norm, m_ffn1_w_gate, m_ffn1_w_up, m_ffn1_w_down, m_mix_norm, m_hyb_w_in, m_hyb_conv_w, m_hyb_q_gain, m_hyb_k_gain, m_hyb_w_out, m_rec_w_in, m_rec_conv_w, m_rec_conv_b, m_lru_wa, m_lru_ba, m_lru_wx, m_lru_bx, m_lru_lambda, m_rec_w_out, m_ffn2_norm, m_ffn2_w_gate, m_ffn2_w_up, m_ffn2_w_down, m_ple_norm, m_ple_w_gate, m_ple_w_proj, v_rel_bias, v_ffn1_norm, v_ffn1_w_gate, v_ffn1_w_up, v_ffn1_w_down, v_mix_norm, v_hyb_w_in, v_hyb_conv_w, v_hyb_q_gain, v_hyb_k_gain, v_hyb_w_out, v_rec_w_in, v_rec_conv_w, v_rec_conv_b, v_lru_wa, v_lru_ba, v_lru_wx, v_lru_bx, v_lru_lambda, v_rec_w_out, v_ffn2_norm, v_ffn2_w_gate, v_ffn2_w_up, v_ffn2_w_down, v_ple_norm, v_ple_w_gate, v_ple_w_proj):
    given = dict(locals())
    wts = {n: given[n] for n in WEIGHTS}
    k_chip = 2 * lax.axis_index("x") + lax.axis_index("y")

    shards = {}
    for n in BIG:
        b16 = wts[n].astype(BF16)
        if n in PER_LAYER:
            shards[n + "/0"], shards[n + "/1"] = b16[0:1], b16[1:2]
        else:
            shards[n] = b16
    first = gather_weights([_halves(shards[n]) for n in FIRST], "gather_first")
    w = {n: g.reshape((N_SHARD,) + shards[n].shape) for n, g in zip(FIRST, first)}
    plan = Plan(shards, w)
    sm2d = {n: wts[n].reshape(-1, wts[n].shape[-1]) for n in SMALL_SHARDED}
    slab, offs = _pack_rows([sm2d[n] for n in SMALL_SHARDED], 256)
    slabs = allgather8(slab, "gather_small")[0::2]
    for n, o in zip(SMALL_SHARDED, offs):
        r, cw = sm2d[n].shape
        w[n] = jnp.concatenate([slabs[kc, o:o + r, :cw] for kc in range(N_SHARD)], axis=1)
    for n in SMALL:
        if n not in SMALL_SHARDED:
            w[n] = wts[n]

    loss, dx, grads = local_step(x[0], p[:, 0], loss_target[0], w, plan)
    loss = lax.psum(loss[0, 0], ("x", "y", "c"))

    for n, r8 in zip(LAST, exchange_cores([plan.landed[n] for n in LAST], "exchange_cores")):
        plan.landed[n] = r8
    out = {}
    for n in BIG:
        r8 = plan.landed[n]
        shp = wts[n].shape
        c2 = shp[-1]
        two = lambda a: a.reshape(-1, c2)
        res = adamw(two(wts[n]), two(given["m_" + n]), two(given["v_" + n]), r8.reshape(N_DEV, -1, c2), "adamw_" + n)
        out[n] = [a.reshape(shp) for a in res]
    g2d = [grads[n].reshape(-1, grads[n].shape[-1]) if n != "rel_bias" else grads[n].reshape(1, -1) for n in SMALL]
    gslab, goffs = _pack_rows(g2d, PACK_W)
    gsum = sum8(allgather8(gslab, "gather_small_grads"), "sum_small_grads")
    for n, o, g in zip(SMALL, goffs, g2d):
        shp = wts[n].shape
        r, cw = g.shape
        gs = gsum[o:o + r, :cw]
        if n in SMALL_SHARDED:
            sw = shp[-1]
            gs = lax.dynamic_slice_in_dim(gs, k_chip * sw, sw, axis=1)
        gs = gs.reshape(shp)
        two = lambda a: a.reshape(-1, shp[-1])
        res = adamw(two(wts[n]), two(given["m_" + n]), two(given["v_" + n]), two(gs), "adamw_" + n)
        out[n] = [a.reshape(shp) for a in res]
    return (loss, dx[None], *[out[n][0] for n in WEIGHTS], *[out[n][1] for n in WEIGHTS],
            *[out[n][2] for n in WEIGHTS], *[out[n][3] for n in WEIGHTS])
```

```python
import functools
import math

import numpy as np
import jax
import jax.numpy as jnp
from jax import lax
from jax.experimental import pallas as pl
from jax.experimental.pallas import tpu as pltpu

F32, BF16 = jnp.float32, jnp.bfloat16
S = jax.ShapeDtypeStruct
MESH = pl.DeviceIdType.MESH

D_MODEL = 1024
N_SHARD = 4
N_DEV = 8
HEAD_DIM = 64
N_HEADS = 8
ATTN_W = N_HEADS * HEAD_DIM
CONV_W = 512
BAND = 128
DILATIONS = (1, 4, 16)
REL_BUCKETS = 32
REL_MAX_DIST = 2048
LRU_BLOCKS = 4
LRU_BLOCK = 256
LRU_C = 8.0
EPS = 1e-6
NEG = -1e30
VMEM_LIMIT = 56 * 1024 * 1024
FFN_ROWS = 1024
TN_ROWS = 2048
BWD_IN_ROWS = 1024

ADAM_LR, ADAM_B1, ADAM_B2, ADAM_EPS, ADAM_WD, ADAM_STEP = 0.001, 0.9, 0.999, 1e-08, 0.01, 10


def _cp(*sem):
    return pltpu.CompilerParams(dimension_semantics=sem, vmem_limit_bytes=VMEM_LIMIT)


def _bs(shape, imap):
    return pl.BlockSpec(shape, imap)


def _row_tile(t, want):
    for cand in range(min(want, t) // 8 * 8, 0, -8):
        if t % cand == 0:
            return cand
    return t


_ANY = pl.BlockSpec(memory_space=pl.ANY)


def _place():
    x, y, c = lax.axis_index("x"), lax.axis_index("y"), lax.axis_index("c")
    chips = [(1 - x, y), (x, 1 - y), (1 - x, 1 - y)]
    return x, y, c, 2 * x + y, chips, [2 * cx + cy for cx, cy in chips]


def _remote(src, dst, ssem, rsem, to):
    return pltpu.make_async_remote_copy(src_ref=src, dst_ref=dst, send_sem=ssem, recv_sem=rsem, device_id=to, device_id_type=MESH)


class Host:
    def __init__(self, kind, items, forwards=()):
        self.kind, self.items, self.forwards, self.outs = kind, items, list(forwards), None

    def n_sems(self):
        return 3 * len(self.items) + N_SHARD * len(self.forwards), len(self.items)

    def operands(self):
        if self.kind == "gather":
            return list(self.items), [S((N_SHARD,) + s.shape, s.dtype) for s in self.items], {}
        xin, shapes, alias = [], [], {}
        for a, (g, _, r_prev) in enumerate(self.items):
            xin.append(g)
            if r_prev is not None:
                alias[len(xin)] = a
                xin.append(r_prev)
            shapes.append(S((N_DEV,) + g.shape[1:], g.dtype))
        for f, r in enumerate(self.forwards):
            alias[len(xin)] = len(self.items) + f
            xin.append(r)
            shapes.append(S(r.shape, r.dtype))
        return xin, shapes, alias

    def copies(self, xi, xo, send, recv, lsem):
        x, y, c, k, chips, kk = _place()
        starts, waits = [], []
        pos = 0
        for f in range(len(self.forwards)):
            arr = xo[len(self.items) + f]
            for kq in range(N_SHARD):
                sem = 3 * len(self.items) + N_SHARD * f + kq
                cp = _remote(arr.at[2 * kq + c], arr.at[2 * kq + c], send.at[sem], recv.at[sem], (x, y, 1 - c))
                starts.append((cp, "start"))
                waits.append((cp, "wait_send"))
                other = arr.at[2 * kq + 1 - c]
                waits.append((_remote(other, other, send.at[sem], recv.at[sem], (x, y, 1 - c)), "wait_recv"))
        for a, item in enumerate(self.items):
            if self.kind == "gather":
                src_of = lambda chip_idx, s=xi[a]: s
                dst_of = lambda chip_idx, o=xo[a]: o.at[chip_idx]
                mine, theirs = k, kk
            else:
                g_ref = xi[pos]
                pos += 1 if item[2] is None else 2
                lay = item[1]
                src_of = lambda chip_idx, g=g_ref, lay=lay: g.at[chip_idx, lay]
                dst_of = lambda slot, o=xo[a], lay=lay: o.at[slot, lay]
                mine, theirs = 2 * k + c, [2 * kj + c for kj in kk]
            own_src = src_of(k)
            local = pltpu.make_async_copy(own_src, dst_of(mine), lsem.at[a])
            starts.append((local, "start"))
            waits.append((local, "wait"))
            for j, chip in enumerate(chips):
                src = own_src if self.kind == "gather" else src_of(kk[j])
                cp = _remote(src, dst_of(mine), send.at[3 * a + j], recv.at[3 * a + j], (*chip, c))
                starts.append((cp, "start"))
                waits.append((cp, "wait_send"))
                waits.append((_remote(own_src, dst_of(theirs[j]), send.at[3 * a + j], recv.at[3 * a + j], (*chip, c)), "wait_recv"))
        return starts, waits


def _call(host, body, *, name, grid, in_specs, out_specs, out_shape, scratch_shapes=(), compiler_params=None, args, aliases=None):
    aliases = dict(aliases or {})
    if host is None:
        return pl.pallas_call(body, name=name, grid=grid, in_specs=in_specs, out_specs=out_specs, out_shape=out_shape,
                              scratch_shapes=list(scratch_shapes), input_output_aliases=aliases, compiler_params=compiler_params)(*args)
    single = not isinstance(out_shape, (list, tuple))
    out_specs_l = [out_specs] if single else list(out_specs)
    out_shape_l = [out_shape] if single else list(out_shape)
    n_in, n_out, n_scr = len(in_specs), len(out_shape_l), len(scratch_shapes)
    xin, xshapes, xalias = host.operands()
    n_items = len(xshapes)
    n_rsem, n_lsem = host.n_sems()
    for i_in, i_out in xalias.items():
        aliases[n_in + i_in] = n_out + i_out
    nd = len(grid)

    def hosted(*refs):
        ins, xi = refs[:n_in], refs[n_in:n_in + len(xin)]
        o0 = n_in + len(xin)
        outs, xo = refs[o0:o0 + n_out], refs[o0 + n_out:o0 + n_out + n_items]
        s0 = o0 + n_out + n_items
        scr = refs[s0:s0 + n_scr]
        send, recv, lsem = refs[s0 + n_scr:]
        first = functools.reduce(jnp.logical_and, [pl.program_id(d) == 0 for d in range(nd)])
        last = functools.reduce(jnp.logical_and, [pl.program_id(d) == grid[d] - 1 for d in range(nd)])
        starts, waits = host.copies(xi, xo, send, recv, lsem)

        @pl.when(first)
        def _():
            for cp, how in starts:
                getattr(cp, how)()
        body(*ins, *outs, *scr)

        @pl.when(last)
        def _():
            for cp, how in waits:
                getattr(cp, how)()

    res = pl.pallas_call(
        hosted, name=name, grid=grid,
        in_specs=list(in_specs) + [_ANY] * len(xin),
        out_specs=out_specs_l + [_ANY] * n_items,
        out_shape=out_shape_l + xshapes,
        scratch_shapes=list(scratch_shapes) + [pltpu.SemaphoreType.DMA((n_rsem,)), pltpu.SemaphoreType.DMA((n_rsem,)),
                                               pltpu.SemaphoreType.DMA((max(n_lsem, 1),))],
        input_output_aliases=aliases,
        compiler_params=pltpu.CompilerParams(dimension_semantics=("arbitrary",) * nd, vmem_limit_bytes=VMEM_LIMIT),
    )(*args, *xin)
    host.outs = list(res[n_out:])
    return res[0] if single else list(res[:n_out])


def _rstd(x):
    return lax.rsqrt(jnp.mean(x * x, axis=-1, keepdims=True) + EPS)


def _sigmoid(x):
    return 1.0 / (1.0 + jnp.exp(-x))


def _dot(a, b):
    return jnp.dot(a, b, preferred_element_type=F32)


def _dot_nt(a, b):
    return lax.dot_general(a, b, (((1,), (1,)), ((), ())), preferred_element_type=F32)


def _dot_tn(a, b):
    return lax.dot_general(a, b, (((0,), (0,)), ((), ())), preferred_element_type=F32)


def _seg_dot(x, seg_bf16):
    hi = x.astype(BF16)
    lo = (x - hi.astype(F32)).astype(BF16)
    return _dot(hi, seg_bf16) + _dot(lo, seg_bf16)


def _shift_down(x, prev8, s):
    if s == 0:
        return x
    tm = x.shape[0]
    row = lax.broadcasted_iota(jnp.int32, x.shape, 0)
    main = jnp.where(row >= s, pltpu.roll(x, s, axis=0), 0.0)
    row8 = lax.broadcasted_iota(jnp.int32, prev8.shape, 0)
    head = jnp.where(row8 < s, pltpu.roll(prev8, s, axis=0), 0.0)
    if tm == 8:
        return main + head
    return main + jnp.concatenate([head, jnp.zeros((tm - 8, x.shape[1]), x.dtype)], axis=0)


def _shift_up(x, next8, s):
    if s == 0:
        return x
    tm = x.shape[0]
    row = lax.broadcasted_iota(jnp.int32, x.shape, 0)
    main = jnp.where(row < tm - s, pltpu.roll(x, tm - s, axis=0), 0.0)
    row8 = lax.broadcasted_iota(jnp.int32, next8.shape, 0)
    tail = jnp.where(row8 >= 8 - s, pltpu.roll(next8, 8 - s, axis=0), 0.0)
    if tm == 8:
        return main + tail
    return main + jnp.concatenate([jnp.zeros((tm - 8, x.shape[1]), x.dtype), tail], axis=0)


def _roll_fill(x, s, fill, up):
    tm = x.shape[0]
    row = lax.broadcasted_iota(jnp.int32, x.shape, 0)
    if up:
        return jnp.where(row < tm - s, pltpu.roll(x, tm - s, axis=0), fill)
    return jnp.where(row >= s, pltpu.roll(x, s, axis=0), fill)


def _log1p(y):
    u = 1.0 + y
    return jnp.where(u == 1.0, y, jnp.log(u) * (y / jnp.where(u == 1.0, 1.0, u - 1.0)))


def _softplus(x):
    return jnp.maximum(x, 0.0) + _log1p(jnp.exp(-jnp.abs(x)))


def _neg_expm1(y):
    series = -y * (1.0 + y * (0.5 + y * (1.0 / 6.0 + y * (1.0 / 24.0 + y * (1.0 / 120.0)))))
    return jnp.where(jnp.abs(y) < 0.03, series, 1.0 - jnp.exp(y))


_GELU_C = math.sqrt(2.0 / math.pi)


def _gelu_and_grad(x):
    inner = _GELU_C * (x + 0.044715 * x * x * x)
    t = jnp.tanh(inner)
    g = 0.5 * x * (1.0 + t)
    dg = 0.5 * (1.0 + t) + 0.5 * x * (1.0 - t * t) * _GELU_C * (1.0 + 3.0 * 0.044715 * x * x)
    return g, dg


def _rmsnorm_bwd(x, gain, dy):
    r = _rstd(x)
    xhat = x * r
    dxhat = dy * gain
    dx = r * (dxhat - xhat * jnp.mean(dxhat * xhat, axis=-1, keepdims=True))
    return dx, jnp.sum(dy * xhat, axis=0, keepdims=True)


def ffn_up(h, gain, wg, wu, layer, name, host=None):
    t, d = h.shape
    nk, _, _, f = wg.shape
    tm = _row_tile(t, FFN_ROWS)

    def body(h_ref, g_ref, wg_ref, wu_ref, hn_ref, gg_ref, uu_ref, aa_ref, hn_scr):
        @pl.when(pl.program_id(1) == 0)
        def _():
            x = h_ref[...]
            hn = (x * _rstd(x) * g_ref[...]).astype(BF16)
            hn_scr[...] = hn
            hn_ref[...] = hn
        hn = hn_scr[...]
        g = _dot(hn, wg_ref[...])
        u = _dot(hn, wu_ref[...])
        s = _sigmoid(g)
        silu = g * s
        gg_ref[...] = (u * (s * (1.0 + g * (1.0 - s)))).astype(BF16)
        uu_ref[...] = silu.astype(BF16)
        aa_ref[...] = (silu * u).astype(BF16)

    wspec = _bs((None, None, d, f), lambda i, k: (k, layer, 0, 0))
    aspec = _bs((None, tm, f), lambda i, k: (k, i, 0))
    return _call(
        host, body, name=name, grid=(t // tm, nk),
        in_specs=[_bs((tm, d), lambda i, k: (i, 0)), _bs((1, d), lambda i, k: (0, 0)), wspec, wspec],
        out_specs=[_bs((tm, d), lambda i, k: (i, 0)), aspec, aspec, aspec],
        out_shape=[S((t, d), BF16), S((nk, t, f), BF16), S((nk, t, f), BF16), S((nk, t, f), BF16)],
        scratch_shapes=[pltpu.VMEM((tm, d), BF16)],
        compiler_params=_cp("parallel", "arbitrary"),
        args=(h, gain, wg, wu))


def mm_acc(a, a_spec, b, b_spec, res, scale, nk, t, n, tm, name, host=None):
    def body(a_ref, b_ref, r_ref, o_ref, acc):
        k = pl.program_id(1)

        @pl.when(k == 0)
        def _():
            acc[...] = jnp.zeros_like(acc)
        acc[...] += _dot(a_ref[...].astype(BF16), b_ref[...])

        @pl.when(k == nk - 1)
        def _():
            o_ref[...] = r_ref[...] + scale * acc[...]

    return _call(
        host, body, name=name, grid=(t // tm, nk),
        in_specs=[a_spec, b_spec, _bs((tm, n), lambda i, k: (i, 0))],
        out_specs=_bs((tm, n), lambda i, k: (i, 0)),
        out_shape=S((t, n), F32),
        scratch_shapes=[pltpu.VMEM((tm, n), F32)],
        compiler_params=_cp("parallel", "arbitrary"),
        args=(a, b, res))


def ffn_down(a, wd, layer, h, name, host=None):
    nk, t, f = a.shape
    d = h.shape[1]
    tm = _row_tile(t, FFN_ROWS)
    return mm_acc(a, _bs((None, tm, f), lambda i, k: (k, i, 0)),
                  wd, _bs((None, None, f, d), lambda i, k: (k, layer, 0, 0)),
                  h, 0.5, nk, t, d, tm, name, host)


def ffn_bwd_act(dh, wd, layer, gg, uu, name, host=None):
    nk, t, f = gg.shape
    d = dh.shape[1]
    tm = _row_tile(t, FFN_ROWS)

    def body(dh_ref, wd_ref, g_ref, u_ref, dg_ref, du_ref):
        da = 0.5 * _dot_nt(dh_ref[...].astype(BF16), wd_ref[...])
        dg_ref[...] = (da * g_ref[...].astype(F32)).astype(BF16)
        du_ref[...] = (da * u_ref[...].astype(F32)).astype(BF16)

    aspec = _bs((None, tm, f), lambda i, k: (k, i, 0))
    return _call(
        host, body, name=name, grid=(t // tm, nk),
        in_specs=[_bs((tm, d), lambda i, k: (i, 0)), _bs((None, None, f, d), lambda i, k: (k, layer, 0, 0)), aspec, aspec],
        out_specs=[aspec, aspec],
        out_shape=[S((nk, t, f), BF16), S((nk, t, f), BF16)],
        compiler_params=_cp("parallel", "arbitrary"),
        args=(dh, wd, gg, uu))


def nt_acc_normbwd(terms, nk, h, gain, dh, name, host=None):
    t, d = h.shape
    tm = _row_tile(t, BWD_IN_ROWS)
    sub = _row_tile(tm, 256)
    nterm = len(terms)

    def body(*refs):
        xs = refs[:2 * nterm]
        h_ref, g_ref, dh_ref, o_ref, dg_ref, acc = refs[2 * nterm:]
        i, k = pl.program_id(0), pl.program_id(1)

        @pl.when(k == 0)
        def _():
            acc[...] = jnp.zeros_like(acc)

        @pl.when(jnp.logical_and(i == 0, k == 0))
        def _():
            dg_ref[...] = jnp.zeros_like(dg_ref)
        tot = _dot_nt(xs[0][...], xs[1][...])
        for j in range(1, nterm):
            tot = tot + _dot_nt(xs[2 * j][...], xs[2 * j + 1][...])
        acc[...] += tot

        @pl.when(k == nk - 1)
        def _():
            def rows_of(cidx, dgain):
                rows = pl.ds(pl.multiple_of(cidx * sub, sub), sub)
                dx, dgc = _rmsnorm_bwd(h_ref[rows, :], g_ref[...], acc[rows, :])
                o_ref[rows, :] = dh_ref[rows, :] + dx
                return dgain + dgc
            dg_ref[...] += lax.fori_loop(0, tm // sub, rows_of, jnp.zeros((1, d), F32))

    in_specs, args = [], []
    for x, xs_, w, ws_ in terms:
        in_specs += [xs_, ws_]
        args += [x, w]
    row = _bs((tm, d), lambda i, k: (i, 0))
    vec = _bs((1, d), lambda i, k: (0, 0))
    return _call(
        host, body, name=name, grid=(t // tm, nk),
        in_specs=in_specs + [row, vec, row],
        out_specs=[row, vec],
        out_shape=[S((t, d), F32), S((1, d), F32)],
        scratch_shapes=[pltpu.VMEM((tm, d), F32)],
        compiler_params=_cp("arbitrary", "arbitrary"),
        args=(*args, h, gain, dh))


def ffn_bwd_in(dg, du, wg, wu, layer, h, gain, dh, name, host=None):
    nk, t, f = dg.shape
    d = h.shape[1]
    tm = _row_tile(t, BWD_IN_ROWS)
    aspec = _bs((None, tm, f), lambda i, k: (k, i, 0))
    wspec = _bs((None, None, d, f), lambda i, k: (k, layer, 0, 0))
    return nt_acc_normbwd([(dg, aspec, wg, wspec), (du, aspec, wu, wspec)], nk, h, gain, dh, name, host)


def tn_mm(x, x_spec, y, y_spec, nblk, t, ka, nb, out_shape, out_spec, scale, prev, name, host=None):
    tk = _row_tile(t, TN_ROWS)

    def body(*refs):
        if prev is None:
            x_ref, y_ref, o_ref, acc = refs
        else:
            x_ref, y_ref, _, o_ref, acc = refs
        j = pl.program_id(1)

        @pl.when(j == 0)
        def _():
            acc[...] = jnp.zeros_like(acc)
        acc[...] += _dot_tn(x_ref[...].astype(BF16), y_ref[...].astype(BF16))

        @pl.when(j == t // tk - 1)
        def _():
            o_ref[...] = (scale * acc[...]).astype(o_ref.dtype)

    in_specs = [x_spec(tk), y_spec(tk)]
    args = [x, y]
    aliases = {}
    if prev is not None:
        in_specs.append(pl.BlockSpec(memory_space=pl.ANY))
        args.append(prev)
        aliases = {2: 0}
    return _call(
        host, body, name=name, grid=(nblk, t // tk),
        in_specs=in_specs, out_specs=out_spec, out_shape=out_shape,
        scratch_shapes=[pltpu.VMEM((ka, nb), F32)],
        aliases=aliases,
        compiler_params=_cp("parallel", "arbitrary"),
        args=tuple(args))


def ffn_wgrads(which, hn, dh, aa, dg, du, layer, grads, run):
    nk, t, f = aa.shape
    d = hn.shape[1]
    hn_spec = lambda tk: _bs((tk, d), lambda k, j: (j, 0))
    a_spec = lambda tk: _bs((None, tk, f), lambda k, j: (k, j, 0))
    shape_gu, spec_gu = S((nk, 2, d, f), BF16), _bs((None, None, d, f), lambda k, j: (k, layer, 0, 0))
    shape_d, spec_d = S((nk, 2, f, d), BF16), _bs((None, None, f, d), lambda k, j: (k, layer, 0, 0))
    for suffix, x, xs, y, ys, ka, nb, shp, spec, scale in (
            ("gate", hn, hn_spec, dg, a_spec, d, f, shape_gu, spec_gu, 1.0),
            ("up", hn, hn_spec, du, a_spec, d, f, shape_gu, spec_gu, 1.0),
            ("down", aa, a_spec, dh, hn_spec, f, d, shape_d, spec_d, 0.5)):
        key = f"{which}_w_{suffix}"
        grads[key] = run(tn_mm, x, xs, y, ys, nk, t, ka, nb, shp, spec, scale, grads.get(key), name=f"{which}_gw_{layer}_{suffix}")


def norm_mm(h, gain, w, name, host=None):
    t, d = h.shape
    nb, _, bw = w.shape
    tm = _row_tile(t, FFN_ROWS)

    def body(h_ref, g_ref, w_ref, hn_ref, z_ref, hn_scr):
        @pl.when(pl.program_id(1) == 0)
        def _():
            x = h_ref[...]
            hn = (x * _rstd(x) * g_ref[...]).astype(BF16)
            hn_scr[...] = hn
            hn_ref[...] = hn
        z_ref[...] = _dot(hn_scr[...], w_ref[...])

    return _call(
        host, body, name=name, grid=(t // tm, nb),
        in_specs=[_bs((tm, d), lambda i, k: (i, 0)), _bs((1, d), lambda i, k: (0, 0)), _bs((None, d, bw), lambda i, k: (k, 0, 0))],
        out_specs=[_bs((tm, d), lambda i, k: (i, 0)), _bs((tm, bw), lambda i, k: (i, k))],
        out_shape=[S((t, d), BF16), S((t, nb * bw), F32)],
        scratch_shapes=[pltpu.VMEM((tm, d), BF16)],
        compiler_params=_cp("parallel", "arbitrary"),
        args=(h, gain, w))


def nt_mm(a, w, name):
    t, k = a.shape
    n = w.shape[0]
    tm = _row_tile(t, 512)

    def body(a_ref, w_ref, o_ref):
        o_ref[...] = _dot_nt(a_ref[...].astype(BF16), w_ref[...])

    return pl.pallas_call(
        body, name=name, grid=(t // tm,),
        in_specs=[_bs((tm, k), lambda i: (i, 0)), _bs((n, k), lambda i: (0, 0))],
        out_specs=_bs((tm, n), lambda i: (i, 0)),
        out_shape=S((t, n), F32),
        compiler_params=_cp("parallel"),
    )(a, w)


def _head_mean_matrix():
    m = np.kron(np.eye(N_HEADS, dtype=np.float32), np.full((HEAD_DIM, HEAD_DIM), 1.0 / HEAD_DIM, np.float32))
    return jnp.asarray(m, BF16)


def _head_sum_matrix():
    m = np.kron(np.eye(N_HEADS, dtype=np.float32), np.ones((HEAD_DIM, HEAD_DIM), np.float32))
    return jnp.asarray(m, BF16)


def _rel_bucket_np(dist):
    max_exact = REL_BUCKETS // 2
    n = np.maximum(dist, 1).astype(np.float32)
    large = max_exact + (np.log(n / np.float32(max_exact)) / np.float32(math.log(REL_MAX_DIST / max_exact))
                         * np.float32(REL_BUCKETS - max_exact)).astype(np.int32)
    large = np.minimum(large, REL_BUCKETS - 1)
    return np.where(dist < max_exact, dist, large)


def _band_tables():
    qi = np.arange(BAND)[:, None]
    kj = np.arange(2 * BAND)[None, :]
    dist_q = qi + BAND - kj
    qq = np.arange(2 * BAND)[:, None]
    kk = np.arange(BAND)[None, :]
    dist_k = qq - kk
    out = []
    for dist in (dist_q, dist_k):
        valid = (dist >= 0) & (dist <= BAND)
        bucket = np.stack([_rel_bucket_np(np.clip(dist, 0, BAND) * d) for d in DILATIONS])
        out.append((bucket, valid))
    return out


def band_bias(rel_bias):
    out = []
    for bucket, valid in _band_tables():
        bucket = np.where(valid[None], bucket, -1)[:, None]
        tab = jnp.full((len(DILATIONS), N_HEADS) + bucket.shape[2:], NEG, F32)
        for b in range(REL_BUCKETS):
            if (bucket == b).any():
                tab = jnp.where(bucket == b, rel_bias[b][None, :, None, None], tab)
        out.append(tab.reshape(len(DILATIONS), N_HEADS // 2, 2 * tab.shape[2], tab.shape[3]))
    return out


LANE_TILE = 128
N_LANE_TILES = ATTN_W // LANE_TILE


def _view_shape(t, dil):
    return (t // dil, dil * ATTN_W)


def _view_spec(tm, dil):
    return _bs((tm // dil, dil * ATTN_W), lambda i: (i, 0))


def _cols_to(scr, val):
    for cc in range(N_LANE_TILES):
        scr[cc] = val[:, LANE_TILE * cc:LANE_TILE * (cc + 1)]


def _cols_from(scr):
    return jnp.concatenate([scr[cc] for cc in range(N_LANE_TILES)], axis=1)


def _write_view(scr, out_ref, dil):
    if dil == 1:
        out_ref[...] = _cols_from(scr).astype(out_ref.dtype)
        return
    rows = scr.shape[1] // dil
    for r in range(dil):
        for cc in range(N_LANE_TILES):
            c0 = r * ATTN_W + LANE_TILE * cc
            out_ref[:, c0:c0 + LANE_TILE] = scr[cc, pl.ds(r, rows, stride=dil), :].astype(out_ref.dtype)


def _read_view(scr, in_ref, dil):
    if dil == 1:
        return in_ref[...].astype(F32)
    rows = scr.shape[1] // dil
    for r in range(dil):
        for cc in range(N_LANE_TILES):
            c0 = r * ATTN_W + LANE_TILE * cc
            scr[cc, pl.ds(r, rows, stride=dil), :] = in_ref[:, c0:c0 + LANE_TILE].astype(F32)
    return _cols_from(scr)


def hyb_prep(z, q_gain, k_gain, name):
    t = z.shape[0]
    tm = _row_tile(t, 512)
    seg = _head_mean_matrix()
    nd = len(DILATIONS)

    def body(q_ref, k_ref, v_ref, qg_ref, kg_ref, seg_ref, *rest):
        outs, scr = rest[:3 * nd], rest[3 * nd]
        q = q_ref[...]
        k = k_ref[...]
        vals = (q * lax.rsqrt(_seg_dot(q * q, seg_ref[...]) + EPS) * qg_ref[...],
                k * lax.rsqrt(_seg_dot(k * k, seg_ref[...]) + EPS) * kg_ref[...],
                v_ref[...])
        for j, val in enumerate(vals):
            _cols_to(scr, val)
            for g, dil in enumerate(DILATIONS):
                _write_view(scr, outs[3 * g + j], dil)

    col = lambda c: _bs((tm, ATTN_W), lambda i: (i, c))
    vec = _bs((1, ATTN_W), lambda i: (0, 0))
    res = pl.pallas_call(
        body, name=name, grid=(t // tm,),
        in_specs=[col(3), col(4), col(5), vec, vec, _bs((ATTN_W, ATTN_W), lambda i: (0, 0))],
        out_specs=[_view_spec(tm, dil) for dil in DILATIONS for _ in range(3)],
        out_shape=[S(_view_shape(t, dil), BF16) for dil in DILATIONS for _ in range(3)],
        scratch_shapes=[pltpu.VMEM((N_LANE_TILES, tm, LANE_TILE), F32)],
        compiler_params=_cp("parallel"),
    )(z, z, z, q_gain, k_gain, seg)
    return {dil: tuple(res[3 * g:3 * g + 3]) for g, dil in enumerate(DILATIONS)}


def _lane_lo(shape):
    return lax.broadcasted_iota(jnp.int32, shape, 1) < HEAD_DIM


def _stack_heads(pair):
    lo = _lane_lo(pair.shape)
    zero = jnp.zeros_like(pair)
    return jnp.concatenate([jnp.where(lo, pair, zero), jnp.where(lo, zero, pair)], axis=0)


def _unstack_heads(st):
    rows = st.shape[0] // 2
    return jnp.where(_lane_lo((rows, st.shape[1])), st[:rows], st[rows:])


def attn_fwd(q, k, v, bias, dil, name, host=None):
    qv, kv, vv = q, k, v
    sub = q.shape[0]
    nb = sub // BAND

    def body(q_ref, kp_ref, kc_ref, vp_ref, vc_ref, b_ref, o_ref, l_ref):
        first = pl.program_id(1) == 0
        colk = lax.broadcasted_iota(jnp.int32, (2 * BAND, 2 * BAND), 1)
        for j in range(N_HEADS // 2):
            sl = slice(2 * HEAD_DIM * j, 2 * HEAD_DIM * (j + 1))
            kk = jnp.concatenate([kp_ref[:, sl], kc_ref[:, sl]], axis=0)
            vv_ = jnp.concatenate([vp_ref[:, sl], vc_ref[:, sl]], axis=0)
            s = _dot_nt(_stack_heads(q_ref[:, sl]), kk) * (HEAD_DIM ** -0.5) + b_ref[j]
            s = jnp.where(jnp.logical_and(first, colk < BAND), NEG, s)
            m = jnp.max(s, axis=-1, keepdims=True)
            p = jnp.exp(s - m)
            l = jnp.sum(p, axis=-1, keepdims=True)
            o_ref[:, sl] = _unstack_heads(_dot(p.astype(BF16), vv_) / l)
            l_ref[:, sl] = _unstack_heads(jnp.broadcast_to(m + jnp.log(l), (2 * BAND, 2 * HEAD_DIM)))

    cur = _bs((BAND, ATTN_W), lambda r, n: (n, r))
    prv = _bs((BAND, ATTN_W), lambda r, n: (jnp.maximum(n - 1, 0), r))
    return _call(
        host, body, name=name, grid=(dil, nb),
        in_specs=[cur, prv, cur, prv, cur, _bs((N_HEADS // 2, 2 * BAND, 2 * BAND), lambda r, n: (0, 0, 0))],
        out_specs=[cur, cur],
        out_shape=[S((sub, dil * ATTN_W), F32)] * 2,
        compiler_params=_cp("parallel", "arbitrary"),
        args=(qv, kv, kv, vv, vv, bias))


def hyb_post(z, conv_w, os_, lses, name):
    t = z.shape[0]
    tm = _row_tile(t, 512)
    nd = len(DILATIONS)

    def body(gb_ref, gc_ref, cx_ref, gch_ref, cxh_ref, w_ref, *rest):
        o_refs, l_refs = rest[:nd], rest[nd:2 * nd]
        y_ref, ya_ref = rest[2 * nd:2 * nd + 2]
        lt_refs, scr = rest[2 * nd + 2:3 * nd + 2], rest[3 * nd + 2]
        i = pl.program_id(0)
        m = gc_ref[...] * cx_ref[...]
        mh = jnp.where(i == 0, 0.0, gch_ref[...] * cxh_ref[...])
        conv = w_ref[0:1, :] * _shift_down(m, mh, 2) + w_ref[1:2, :] * _shift_down(m, mh, 1) + w_ref[2:3, :] * m
        y_ref[0] = (gb_ref[...] * conv).astype(BF16)
        ls = [_read_view(scr, l_refs[g], dil) for g, dil in enumerate(DILATIONS)]
        mx = functools.reduce(jnp.maximum, ls)
        es = [jnp.exp(l - mx) for l in ls]
        den = functools.reduce(lambda a, b: a + b, es)
        num = es[0] * _read_view(scr, o_refs[0], DILATIONS[0])
        for g in range(1, nd):
            num = num + es[g] * _read_view(scr, o_refs[g], DILATIONS[g])
        ya = num / den
        y_ref[1] = ya.astype(BF16)
        ya_ref[...] = ya
        _cols_to(scr, mx + jnp.log(den))
        for g, dil in enumerate(DILATIONS):
            _write_view(scr, lt_refs[g], dil)

    hb = tm // 8
    col = lambda c: _bs((tm, CONV_W), lambda i: (i, c))
    halo = lambda c: _bs((8, CONV_W), lambda i: (jnp.maximum(i * hb - 1, 0), c))
    row = _bs((tm, ATTN_W), lambda i: (i, 0))
    views = [_view_spec(tm, dil) for dil in DILATIONS]
    res = pl.pallas_call(
        body, name=name, grid=(t // tm,),
        in_specs=[col(0), col(1), col(2), halo(1), halo(2), _bs((3, CONV_W), lambda i: (0, 0))] + views * 2,
        out_specs=[_bs((2, tm, ATTN_W), lambda i: (0, i, 0)), row] + views,
        out_shape=[S((2, t, ATTN_W), BF16), S((t, ATTN_W), F32)] + [S(_view_shape(t, dil), F32) for dil in DILATIONS],
        scratch_shapes=[pltpu.VMEM((N_LANE_TILES, tm, LANE_TILE), F32)],
        compiler_params=_cp("parallel"),
    )(z, z, z, z, z, conv_w, *os_, *lses)
    return res[0], res[1], dict(zip(DILATIONS, res[2:]))


def attn_delta(dy, ya, name):
    t = ya.shape[0]
    tm = _row_tile(t, 512)
    seg = _head_sum_matrix()
    nd = len(DILATIONS)

    def body(dy_ref, ya_ref, seg_ref, *rest):
        dl_refs, db_refs, scr = rest[:nd], rest[nd:2 * nd], rest[2 * nd]
        dya = dy_ref[...]
        _cols_to(scr, _seg_dot(dya * ya_ref[...], seg_ref[...]))
        for g, dil in enumerate(DILATIONS):
            _write_view(scr, dl_refs[g], dil)
        _cols_to(scr, dya)
        for g, dil in enumerate(DILATIONS):
            _write_view(scr, db_refs[g], dil)

    row = _bs((tm, ATTN_W), lambda i: (i, 0))
    views = [_view_spec(tm, dil) for dil in DILATIONS]
    res = pl.pallas_call(
        body, name=name, grid=(t // tm,),
        in_specs=[_bs((tm, ATTN_W), lambda i: (i, 1)), row, _bs((ATTN_W, ATTN_W), lambda i: (0, 0))],
        out_specs=views * 2,
        out_shape=[S(_view_shape(t, dil), F32) for dil in DILATIONS] + [S(_view_shape(t, dil), BF16) for dil in DILATIONS],
        scratch_shapes=[pltpu.VMEM((N_LANE_TILES, tm, LANE_TILE), F32)],
        compiler_params=_cp("parallel"),
    )(dy, ya, seg)
    return dict(zip(DILATIONS, res[:nd])), dict(zip(DILATIONS, res[nd:]))


def attn_bwd_dq(q, k, v, dya, lt, delta, bias, dil, name):
    qv, kv, vv, dv_, lv, ev = q, k, v, dya, lt, delta
    sub = q.shape[0]
    nb = sub // BAND

    def body(q_ref, kp_ref, kc_ref, vp_ref, vc_ref, do_ref, l_ref, e_ref, b_ref, dq_ref, db_ref):
        r, n = pl.program_id(0), pl.program_id(1)

        @pl.when(jnp.logical_and(r == 0, n == 0))
        def _():
            db_ref[...] = jnp.zeros_like(db_ref)
        first = n == 0
        colk = lax.broadcasted_iota(jnp.int32, (2 * BAND, 2 * BAND), 1)
        for j in range(N_HEADS // 2):
            c0 = 2 * HEAD_DIM * j
            sl = slice(c0, c0 + 2 * HEAD_DIM)
            kk = jnp.concatenate([kp_ref[:, sl], kc_ref[:, sl]], axis=0)
            vv_ = jnp.concatenate([vp_ref[:, sl], vc_ref[:, sl]], axis=0)
            lse = jnp.concatenate([l_ref[:, c0:c0 + 1], l_ref[:, c0 + HEAD_DIM:c0 + HEAD_DIM + 1]], axis=0)
            dlt = jnp.concatenate([e_ref[:, c0:c0 + 1], e_ref[:, c0 + HEAD_DIM:c0 + HEAD_DIM + 1]], axis=0)
            s = _dot_nt(_stack_heads(q_ref[:, sl]), kk) * (HEAD_DIM ** -0.5) + b_ref[j]
            s = jnp.where(jnp.logical_and(first, colk < BAND), NEG, s)
            p = jnp.exp(s - lse)
            ds = p * (_dot_nt(_stack_heads(do_ref[:, sl]), vv_) - dlt)
            db_ref[j] += ds
            dq_ref[:, sl] = _unstack_heads(_dot(ds.astype(BF16), kk)) * (HEAD_DIM ** -0.5)

    cur = _bs((BAND, ATTN_W), lambda r, n: (n, r))
    prv = _bs((BAND, ATTN_W), lambda r, n: (jnp.maximum(n - 1, 0), r))
    tab = _bs((N_HEADS // 2, 2 * BAND, 2 * BAND), lambda r, n: (0, 0, 0))
    dq, db = pl.pallas_call(
        body, name=name, grid=(dil, nb),
        in_specs=[cur, prv, cur, prv, cur, cur, cur, cur, tab],
        out_specs=[cur, tab],
        out_shape=[S((sub, dil * ATTN_W), F32), S((N_HEADS // 2, 2 * BAND, 2 * BAND), F32)],
        compiler_params=_cp("arbitrary", "arbitrary"),
    )(qv, kv, kv, vv, vv, dv_, lv, ev, bias)
    return dq, db.reshape(N_HEADS, BAND, 2 * BAND)


def attn_bwd_dkv(q, k, v, dya, lt, delta, bias_k, dil, name):
    qv, kv, vv, dv_, lv, ev = q, k, v, dya, lt, delta
    sub = q.shape[0]
    nb = sub // BAND

    def body(k_ref, v_ref, qc_ref, qn_ref, dc_ref, dn_ref, lc_ref, ln_ref, ec_ref, en_ref, b_ref, dk_ref, dv_ref):
        last = pl.program_id(1) == nb - 1
        rowq = lax.broadcasted_iota(jnp.int32, (4 * BAND, BAND), 0)
        from_next = (rowq & BAND) != 0
        for j in range(N_HEADS // 2):
            c0 = 2 * HEAD_DIM * j
            sl = slice(c0, c0 + 2 * HEAD_DIM)
            kp, vp = k_ref[:, sl], v_ref[:, sl]
            q4 = _stack_heads(jnp.concatenate([qc_ref[:, sl], qn_ref[:, sl]], axis=0))
            do4 = _stack_heads(jnp.concatenate([dc_ref[:, sl], dn_ref[:, sl]], axis=0))
            lse = jnp.concatenate([ref[:, c:c + 1] for c in (c0, c0 + HEAD_DIM) for ref in (lc_ref, ln_ref)], axis=0)
            dlt = jnp.concatenate([ref[:, c:c + 1] for c in (c0, c0 + HEAD_DIM) for ref in (ec_ref, en_ref)], axis=0)
            s = _dot_nt(q4, kp) * (HEAD_DIM ** -0.5) + b_ref[j]
            s = jnp.where(jnp.logical_and(last, from_next), NEG, s)
            p = jnp.exp(s - lse)
            ds = p * (_dot_nt(do4, vp) - dlt)
            dv_ref[:, sl] = _dot_tn(p.astype(BF16), do4)
            dk_ref[:, sl] = _dot_tn(ds.astype(BF16), q4) * (HEAD_DIM ** -0.5)

    cur = _bs((BAND, ATTN_W), lambda r, n: (n, r))
    nxt = _bs((BAND, ATTN_W), lambda r, n: (jnp.minimum(n + 1, nb - 1), r))
    tab = _bs((N_HEADS // 2, 4 * BAND, BAND), lambda r, n: (0, 0, 0))
    dk, dv = pl.pallas_call(
        body, name=name, grid=(dil, nb),
        in_specs=[cur, cur, cur, nxt, cur, nxt, cur, nxt, cur, nxt, tab],
        out_specs=[cur, cur],
        out_shape=[S((sub, dil * ATTN_W), F32)] * 2,
        compiler_params=_cp("parallel", "arbitrary"),
    )(kv, vv, qv, qv, dv_, dv_, lv, lv, ev, ev, bias_k)
    return dk, dv


def hyb_dz(z, dy, conv_w, q_gain, k_gain, dqs, dks, dvs, name):
    t = z.shape[0]
    tm = _row_tile(t, 256)
    nt = t // tm
    seg = _head_mean_matrix()

    def body(gb_ref, gc_ref, cx_ref, q_ref, k_ref, gch_ref, cxh_ref, gbn_ref, dyc_ref, dyn_ref, w_ref, qg_ref, kg_ref, seg_ref,
             dq1, dq2, dq3, dk1, dk2, dk3, dv1, dv2, dv3, dz_ref, dw_ref, dqg_ref, dkg_ref, scr):
        i = pl.program_id(0)

        def total(parts):
            acc = _read_view(scr, parts[0], DILATIONS[0])
            for g in range(1, len(DILATIONS)):
                acc = acc + _read_view(scr, parts[g], DILATIONS[g])
            return acc

        @pl.when(i == 0)
        def _():
            dw_ref[...] = jnp.zeros_like(dw_ref)
            dqg_ref[...] = jnp.zeros_like(dqg_ref)
            dkg_ref[...] = jnp.zeros_like(dkg_ref)
        gb, gc, cx, dyc = gb_ref[...], gc_ref[...], cx_ref[...], dyc_ref[...]
        m = gc * cx
        mh = jnp.where(i == 0, 0.0, gch_ref[...] * cxh_ref[...])
        m1, m2 = _shift_down(m, mh, 1), _shift_down(m, mh, 2)
        conv = w_ref[0:1, :] * m2 + w_ref[1:2, :] * m1 + w_ref[2:3, :] * m
        dconv = dyc * gb
        dcn = jnp.where(i == nt - 1, 0.0, dyn_ref[...] * gbn_ref[...])
        dm = w_ref[2:3, :] * dconv + w_ref[1:2, :] * _shift_up(dconv, dcn, 1) + w_ref[0:1, :] * _shift_up(dconv, dcn, 2)
        dz_ref[:, 0:CONV_W] = (dyc * conv).astype(BF16)
        dz_ref[:, CONV_W:2 * CONV_W] = (dm * cx).astype(BF16)
        dz_ref[:, 2 * CONV_W:3 * CONV_W] = (dm * gc).astype(BF16)
        dw_ref[0:1, :] += jnp.sum(dconv * m2, axis=0, keepdims=True)
        dw_ref[1:2, :] += jnp.sum(dconv * m1, axis=0, keepdims=True)
        dw_ref[2:3, :] += jnp.sum(dconv * m, axis=0, keepdims=True)
        base = 3 * CONV_W
        for idx, (x_ref, g_ref, parts, dgain_ref) in enumerate(((q_ref, qg_ref, (dq1, dq2, dq3), dqg_ref),
                                                                  (k_ref, kg_ref, (dk1, dk2, dk3), dkg_ref))):
            x = x_ref[...]
            dxh = total(parts)
            r = lax.rsqrt(_seg_dot(x * x, seg_ref[...]) + EPS)
            xhat = x * r
            tt = dxh * g_ref[...]
            dx = r * (tt - xhat * _seg_dot(tt * xhat, seg_ref[...]))
            dz_ref[:, base + idx * ATTN_W:base + (idx + 1) * ATTN_W] = dx.astype(BF16)
            dgain_ref[...] += jnp.sum(dxh * xhat, axis=0, keepdims=True)
        dz_ref[:, base + 2 * ATTN_W:base + 3 * ATTN_W] = total((dv1, dv2, dv3)).astype(BF16)

    hb = tm // 8
    col = lambda c: _bs((tm, CONV_W), lambda i: (i, c))
    prev = lambda c: _bs((8, CONV_W), lambda i: (jnp.maximum(i * hb - 1, 0), c))
    nxt = lambda c: _bs((8, CONV_W), lambda i: (jnp.minimum((i + 1) * hb, t // 8 - 1), c))
    row = _bs((tm, ATTN_W), lambda i: (i, 0))
    vec = _bs((1, ATTN_W), lambda i: (0, 0))
    return pl.pallas_call(
        body, name=name, grid=(nt,),
        in_specs=[col(0), col(1), col(2), col(3), col(4), prev(1), prev(2), nxt(0), col(0), nxt(0),
                  _bs((3, CONV_W), lambda i: (0, 0)), vec, vec, _bs((ATTN_W, ATTN_W), lambda i: (0, 0))]
                 + [_view_spec(tm, dil) for dil in DILATIONS] * 3,
        out_specs=[_bs((tm, 6 * CONV_W), lambda i: (i, 0)), _bs((3, CONV_W), lambda i: (0, 0)), vec, vec],
        out_shape=[S((t, 6 * CONV_W), BF16), S((3, CONV_W), F32), S((1, ATTN_W), F32), S((1, ATTN_W), F32)],
        scratch_shapes=[pltpu.VMEM((N_LANE_TILES, tm, LANE_TILE), F32)],
        compiler_params=_cp("arbitrary"),
    )(z, z, z, z, z, z, z, z, dy, dy, conv_w, q_gain, k_gain, seg, *dqs, *dks, *dvs)


def rel_bias_grad(dbs, name):
    (bq, vq), _ = _band_tables()
    onehot = np.zeros((len(DILATIONS), REL_BUCKETS, BAND * 2 * BAND), np.float32)
    for g in range(len(DILATIONS)):
        idx = bq[g].reshape(-1)
        ok = vq.reshape(-1)
        onehot[g, idx[ok], np.nonzero(ok)[0]] = 1.0
    onehot = jnp.asarray(onehot, BF16)
    flat = [d.reshape(N_HEADS, BAND * 2 * BAND) for d in dbs]

    def body(oh_ref, d1, d2, d3, o_ref):
        acc = jnp.zeros((REL_BUCKETS, N_HEADS), F32)
        for g, d in enumerate((d1, d2, d3)):
            x = d[...]
            hi = x.astype(BF16)
            lo = (x - hi.astype(F32)).astype(BF16)
            acc += _dot_nt(oh_ref[g], hi) + _dot_nt(oh_ref[g], lo)
        o_ref[...] = acc

    full = lambda shp: _bs(shp, lambda: tuple(0 for _ in shp))
    return pl.pallas_call(
        body, name=name,
        in_specs=[full(onehot.shape)] + [full(flat[0].shape)] * 3,
        out_specs=full((REL_BUCKETS, N_HEADS)),
        out_shape=S((REL_BUCKETS, N_HEADS), F32),
        compiler_params=pltpu.CompilerParams(vmem_limit_bytes=VMEM_LIMIT),
    )(onehot, *flat)


def _lru_gates(xb, wa_ref, wx_ref, ba, bx):
    xb16 = xb.astype(BF16)
    ga = jnp.concatenate([_dot(xb16[:, LRU_BLOCK * g:LRU_BLOCK * (g + 1)], wa_ref[g]) for g in range(LRU_BLOCKS)], axis=1) + ba
    gx = jnp.concatenate([_dot(xb16[:, LRU_BLOCK * g:LRU_BLOCK * (g + 1)], wx_ref[g]) for g in range(LRU_BLOCKS)], axis=1) + bx
    return ga, gx


def _lru_coeffs(ga, gx, lam):
    sga = _sigmoid(ga)
    sp = _softplus(-lam)
    log_a = -LRU_C * sga * sp
    a = jnp.exp(log_a)
    one_m_a2 = _neg_expm1(2.0 * log_a)
    return sga, sp, a, one_m_a2, jnp.sqrt(one_m_a2), _sigmoid(gx)


def rec_fwd(z, conv_w, conv_b, wa, wx, ba, bx, lam, name):
    t = z.shape[0]
    w = z.shape[1] // 2
    tm = _row_tile(t, 256)

    def body(xp_ref, xh_ref, yb_ref, cw_ref, cb_ref, wa_ref, wx_ref, ba_ref, bx_ref, lam_ref,
             xb_ref, ga_ref, gx_ref, hs_ref, out_ref, carry):
        i = pl.program_id(0)

        @pl.when(i == 0)
        def _():
            carry[...] = jnp.zeros_like(carry)
        xp = xp_ref[...]
        xh = jnp.where(i == 0, 0.0, xh_ref[...])
        xb = cb_ref[...] + cw_ref[3:4, :] * xp
        for j in range(3):
            xb = xb + cw_ref[j:j + 1, :] * _shift_down(xp, xh, 3 - j)
        ga, gx = _lru_gates(xb, wa_ref, wx_ref, ba_ref[...], bx_ref[...])
        _, _, a, _, sq, sgx = _lru_coeffs(ga, gx, lam_ref[...])
        aa, bb = a, sq * sgx * xb
        s = 1
        while s < tm:
            bb = aa * _roll_fill(bb, s, 0.0, False) + bb
            aa = aa * _roll_fill(aa, s, 1.0, False)
            s *= 2
        hs = aa * carry[0:1, :] + bb
        xb_ref[...] = xb
        ga_ref[...] = ga
        gx_ref[...] = gx
        hs_ref[...] = hs
        carry[0:1, :] = hs_ref[tm - 1:tm, :]
        gy, _ = _gelu_and_grad(yb_ref[...])
        out_ref[...] = (hs * gy).astype(BF16)

    hb = tm // 8
    row = _bs((tm, w), lambda i: (i, 0))
    vec = _bs((1, w), lambda i: (0, 0))
    wsp = _bs((LRU_BLOCKS, LRU_BLOCK, LRU_BLOCK), lambda i: (0, 0, 0))
    return pl.pallas_call(
        body, name=name, grid=(t // tm,),
        in_specs=[row, _bs((8, w), lambda i: (jnp.maximum(i * hb - 1, 0), 0)), _bs((tm, w), lambda i: (i, 1)),
                  _bs((4, w), lambda i: (0, 0)), vec, wsp, wsp, vec, vec, vec],
        out_specs=[row] * 5,
        out_shape=[S((t, w), F32)] * 4 + [S((t, w), BF16)],
        scratch_shapes=[pltpu.VMEM((8, w), F32)],
        compiler_params=_cp("arbitrary"),
    )(z, z, z, conv_w, conv_b, wa, wx, ba, bx, lam)


def rec_bwd(d_out, z, xb, ga, gx, hs, conv_w, wa, wx, lam, name):
    t = z.shape[0]
    w = z.shape[1] // 2
    tm = _row_tile(t, 256)
    nt = t // tm

    def body(do_ref, xp_ref, xph_ref, yb_ref, xb_ref, ga_ref, gx_ref, hs_ref, hsh_ref, cw_ref, wa_ref, wx_ref, lam_ref,
             dz_ref, dga_ref, dgx_ref, sm_ref, c_lam, c_a, c_dxb):
        i = pl.program_id(0)

        @pl.when(i == 0)
        def _():
            sm_ref[...] = jnp.zeros_like(sm_ref)
            c_lam[...] = jnp.zeros_like(c_lam)
            c_a[...] = jnp.zeros_like(c_a)
            c_dxb[...] = jnp.zeros_like(c_dxb)
        d_o, yb, xb, hs = do_ref[...], yb_ref[...], xb_ref[...], hs_ref[...]
        lam = lam_ref[...]
        gy, dgy = _gelu_and_grad(yb)
        dz_ref[:, w:2 * w] = (d_o * hs * dgy).astype(BF16)
        sga, sp, a, one_m_a2, sq, sgx = _lru_coeffs(ga_ref[...], gx_ref[...], lam)
        aa = _shift_up(a, c_a[...], 1)
        bb = d_o * gy
        s = 1
        while s < tm:
            bb = aa * _roll_fill(bb, s, 0.0, True) + bb
            aa = aa * _roll_fill(aa, s, 1.0, True)
            s *= 2
        lmb = aa * c_lam[0:1, :] + bb
        c_a[...] = a[0:8, :]
        c_lam[...] = lmb[0:8, :]
        hprev = _shift_down(hs, jnp.where(i == nt - 1, 0.0, hsh_ref[...]), 1)
        d_sq = lmb * sgx * xb
        d_sgx = lmb * sq * xb
        d_log_a = lmb * hprev * a - d_sq * (1.0 - one_m_a2) / sq
        dga = d_log_a * (-LRU_C * sp) * sga * (1.0 - sga)
        dgx = d_sgx * sgx * (1.0 - sgx)
        dga16, dgx16 = dga.astype(BF16), dgx.astype(BF16)
        dga_ref[...] = dga16
        dgx_ref[...] = dgx16
        dxb = lmb * sq * sgx + jnp.concatenate(
            [_dot_nt(dga16[:, LRU_BLOCK * g:LRU_BLOCK * (g + 1)], wa_ref[g]) + _dot_nt(dgx16[:, LRU_BLOCK * g:LRU_BLOCK * (g + 1)], wx_ref[g])
             for g in range(LRU_BLOCKS)], axis=1)
        nxt = c_dxb[...]
        dxp = cw_ref[3:4, :] * dxb
        for j in range(3):
            dxp = dxp + cw_ref[j:j + 1, :] * _shift_up(dxb, nxt, 3 - j)
        c_dxb[...] = dxb[0:8, :]
        dz_ref[:, 0:w] = dxp.astype(BF16)
        xp = xp_ref[...]
        xph = jnp.where(i == nt - 1, 0.0, xph_ref[...])
        sm_ref[0:1, :] += jnp.sum(dga, axis=0, keepdims=True)
        sm_ref[1:2, :] += jnp.sum(dgx, axis=0, keepdims=True)
        sm_ref[2:3, :] += jnp.sum(d_log_a * (-LRU_C * sga), axis=0, keepdims=True) * (-_sigmoid(-lam))
        sm_ref[3:4, :] += jnp.sum(dxb, axis=0, keepdims=True)
        for j in range(4):
            sm_ref[4 + j:5 + j, :] += jnp.sum(dxb * _shift_down(xp, xph, 3 - j), axis=0, keepdims=True)

    hb = tm // 8
    rev = lambda c: _bs((tm, w), lambda i: (nt - 1 - i, c))
    halo = lambda c: _bs((8, w), lambda i: (jnp.maximum((nt - 1 - i) * hb - 1, 0), c))
    vec = _bs((1, w), lambda i: (0, 0))
    wsp = _bs((LRU_BLOCKS, LRU_BLOCK, LRU_BLOCK), lambda i: (0, 0, 0))
    return pl.pallas_call(
        body, name=name, grid=(nt,),
        in_specs=[rev(0), rev(0), halo(0), rev(1), rev(0), rev(0), rev(0), rev(0), halo(0),
                  _bs((4, w), lambda i: (0, 0)), wsp, wsp, vec],
        out_specs=[_bs((tm, 2 * w), lambda i: (nt - 1 - i, 0)), rev(0), rev(0), _bs((8, w), lambda i: (0, 0))],
        out_shape=[S((t, 2 * w), BF16), S((t, w), BF16), S((t, w), BF16), S((8, w), F32)],
        scratch_shapes=[pltpu.VMEM((8, w), F32)] * 3,
        compiler_params=_cp("arbitrary"),
    )(d_out, z, z, z, xb, ga, gx, hs, hs, conv_w, wa, wx, lam)


def ple_fwd(h, gain, wpg, layer, p, wpp, name):
    t, d = h.shape
    pd = p.shape[1]
    nk, _, rb, _ = wpg.shape
    cb = wpp.shape[3]
    tm = _row_tile(t, 512)

    def body(h_ref, g_ref, wg_ref, p_ref, wp_ref, o_ref, hn_ref, gp_ref, pp_ref):
        x = h_ref[...]
        hn = (x * _rstd(x) * g_ref[...]).astype(BF16)
        gp = _dot(hn[:, 0:rb], wg_ref[0])
        for k in range(1, nk):
            gp = gp + _dot(hn[:, rb * k:rb * (k + 1)], wg_ref[k])
        p16 = p_ref[...].astype(BF16)
        pp = jnp.concatenate([_dot(p16, wp_ref[k]) for k in range(nk)], axis=1)
        hn_ref[...] = hn
        gp_ref[...] = gp
        pp_ref[...] = pp
        o_ref[...] = x + _sigmoid(gp) * pp

    row = _bs((tm, d), lambda i: (i, 0))
    return pl.pallas_call(
        body, name=name, grid=(t // tm,),
        in_specs=[row, _bs((1, d), lambda i: (0, 0)), _bs((nk, None, rb, d), lambda i: (0, layer, 0, 0)),
                  _bs((tm, pd), lambda i: (i, 0)), _bs((nk, None, pd, cb), lambda i: (0, layer, 0, 0))],
        out_specs=[row] * 4,
        out_shape=[S((t, d), F32), S((t, d), BF16), S((t, d), F32), S((t, d), F32)],
        compiler_params=_cp("parallel"),
    )(h, gain, wpg, p, wpp)


def ple_bwd(dh, h, gain, wpg, layer, gp, pp, name):
    t, d = h.shape
    nk, _, rb, _ = wpg.shape
    tm = _row_tile(t, 512)

    def body(dh_ref, h_ref, g_ref, wg_ref, gp_ref, pp_ref, o_ref, dgp_ref, dpp_ref, dg_ref):
        @pl.when(pl.program_id(0) == 0)
        def _():
            dg_ref[...] = jnp.zeros_like(dg_ref)
        d_h = dh_ref[...]
        gate = _sigmoid(gp_ref[...])
        dgp = (d_h * pp_ref[...] * gate * (1.0 - gate)).astype(BF16)
        dgp_ref[...] = dgp
        dpp_ref[...] = (d_h * gate).astype(BF16)
        dhn = jnp.concatenate([_dot_nt(dgp, wg_ref[k]) for k in range(nk)], axis=1)
        dx, dgain = _rmsnorm_bwd(h_ref[...], g_ref[...], dhn)
        o_ref[...] = d_h + dx
        dg_ref[...] += dgain

    row = _bs((tm, d), lambda i: (i, 0))
    vec = _bs((1, d), lambda i: (0, 0))
    return pl.pallas_call(
        body, name=name, grid=(t // tm,),
        in_specs=[row, row, vec, _bs((nk, None, rb, d), lambda i: (0, layer, 0, 0)), row, row],
        out_specs=[row, row, row, vec],
        out_shape=[S((t, d), F32), S((t, d), BF16), S((t, d), BF16), S((1, d), F32)],
        compiler_params=_cp("arbitrary"),
    )(dh, h, gain, wpg, gp, pp)


def loss_and_grad(y, target, name):
    t, d = y.shape
    tm = _row_tile(t, 512)

    def body(y_ref, t_ref, l_ref, g_ref):
        @pl.when(pl.program_id(0) == 0)
        def _():
            l_ref[...] = jnp.zeros_like(l_ref)
        err = y_ref[...] - t_ref[...]
        g_ref[...] = err * (1.0 / d)
        l_ref[...] += jnp.sum(jnp.sum(err * err, axis=1, keepdims=True), axis=0, keepdims=True) * (0.5 / d)

    row = _bs((tm, d), lambda i: (i, 0))
    return pl.pallas_call(
        body, name=name, grid=(t // tm,),
        in_specs=[row, row],
        out_specs=[_bs((1, 1), lambda i: (0, 0)), row],
        out_shape=[S((1, 1), F32), S((t, d), F32)],
        compiler_params=_cp("arbitrary"),
    )(y, target)


def _vec(a, i):
    return a[i:i + 1]


def local_step(x, p, target, w, plan=None):
    t = x.shape[0]
    tm = _row_tile(t, 512)
    grads = {}
    if plan is not None:
        plan.grads = grads
    saved = []
    h = x
    bias_q, bias_k = band_bias(w["rel_bias"])
    qg = jnp.tile(w["hyb_q_gain"], (1, N_HEADS))
    kg = jnp.tile(w["hyb_k_gain"], (1, N_HEADS))

    def run(fn, *a, name):
        hst = plan.host(name) if plan is not None else None
        out = fn(*a, name, hst)
        if hst is not None:
            plan.done(hst)
        return out

    def lru_blocks(n):
        return jnp.transpose(w[n].reshape(N_SHARD, LRU_BLOCKS, 64, LRU_BLOCK), (1, 0, 2, 3)).reshape(LRU_BLOCKS, LRU_BLOCK, LRU_BLOCK)

    for i in range(2):
        s = {}
        s["h0"] = h
        s["hn1"], s["g1"], s["u1"], s["a1"] = run(ffn_up, h, _vec(w["ffn1_norm"], i), w[f"ffn1_w_gate/{i}"], w[f"ffn1_w_up/{i}"], 0, name=f"ffn1_up_{i}")
        h = run(ffn_down, s["a1"], w[f"ffn1_w_down/{i}"], 0, h, name=f"ffn1_down_{i}")
        s["h1"] = h
        if i == 0:
            w_hyb_in = w["hyb_w_in"].reshape(N_SHARD, D_MODEL, -1)
            w_hyb_out = w["hyb_w_out"].reshape(D_MODEL, D_MODEL)
            s["hnm"], s["z"] = run(norm_mm, h, _vec(w["mix_norm"], i), w_hyb_in, name="hyb_in")
            s["qkv"] = hyb_prep(s["z"], qg, kg, "hyb_prep")
            os_, lses = [], []
            for g, dil in enumerate(DILATIONS):
                o, l = run(attn_fwd, *s["qkv"][dil], bias_q[g], dil, name=f"attn_fwd_{dil}")
                os_.append(o)
                lses.append(l)
            s["y2"], s["ya"], s["lt"] = hyb_post(s["z"], w["hyb_conv_w"], os_, lses, "hyb_post")
            h = run(mm_acc, s["y2"], _bs((None, tm, ATTN_W), lambda r, k: (k, r, 0)),
                    w_hyb_out.reshape(2, ATTN_W, D_MODEL), _bs((None, ATTN_W, D_MODEL), lambda r, k: (k, 0, 0)),
                    h, 1.0, 2, t, D_MODEL, tm, name="hyb_out")
        else:
            w_rec_in = w["rec_w_in"].reshape(N_SHARD, D_MODEL, -1)
            s["hnm"], s["z"] = run(norm_mm, h, _vec(w["mix_norm"], i), w_rec_in, name="rec_in")
            w_rec_out = w["rec_w_out"].reshape(D_MODEL, D_MODEL)
            lru_wa, lru_wx = lru_blocks("lru_wa"), lru_blocks("lru_wx")
            s["xb"], s["ga"], s["gx"], s["hs"], s["ro"] = rec_fwd(
                s["z"], w["rec_conv_w"], w["rec_conv_b"], lru_wa, lru_wx, w["lru_ba"], w["lru_bx"], w["lru_lambda"], "rec_fwd")
            h = run(mm_acc, s["ro"], _bs((tm, D_MODEL), lambda r, k: (r, 0)), w_rec_out, _bs((D_MODEL, D_MODEL), lambda r, k: (0, 0)),
                    h, 1.0, 1, t, D_MODEL, tm, name="rec_out")
        s["h2"] = h
        s["hn2"], s["g2"], s["u2"], s["a2"] = run(ffn_up, h, _vec(w["ffn2_norm"], i), w[f"ffn2_w_gate/{i}"], w[f"ffn2_w_up/{i}"], 0, name=f"ffn2_up_{i}")
        h = run(ffn_down, s["a2"], w[f"ffn2_w_down/{i}"], 0, h, name=f"ffn2_down_{i}")
        s["h3"] = h
        h, s["hnp"], s["gp"], s["pp"] = ple_fwd(h, _vec(w["ple_norm"], i), w[f"ple_w_gate/{i}"], 0, p[i], w[f"ple_w_proj/{i}"], f"ple_fwd_{i}")
        saved.append(s)
    loss, dh = loss_and_grad(h, target, "loss")

    norm_g = {n: [None, None] for n in ("ffn1_norm", "mix_norm", "ffn2_norm", "ple_norm")}
    for i in (1, 0):
        s = saved[i]
        dh_out = dh
        dh, dgp, dpp, norm_g["ple_norm"][i] = ple_bwd(dh_out, s["h3"], _vec(w["ple_norm"], i), w[f"ple_w_gate/{i}"], 0, s["gp"], s["pp"], f"ple_bwd_{i}")
        grads["ple_w_gate"] = run(tn_mm, s["hnp"], lambda tk: _bs((tk, 256), lambda k, j: (j, k)), dgp, lambda tk: _bs((tk, D_MODEL), lambda k, j: (j, 0)),
                                  N_SHARD, t, 256, D_MODEL, S((N_SHARD, 2, 256, D_MODEL), BF16),
                                  _bs((None, None, 256, D_MODEL), lambda k, j, i=i: (k, i, 0, 0)), 1.0, grads.get("ple_w_gate"), name=f"ple_gw_gate_{i}")
        grads["ple_w_proj"] = run(tn_mm, p[i], lambda tk: _bs((tk, 256), lambda k, j: (j, 0)), dpp, lambda tk: _bs((tk, 256), lambda k, j: (j, k)),
                                  N_SHARD, t, 256, 256, S((N_SHARD, 2, 256, 256), BF16),
                                  _bs((None, None, 256, 256), lambda k, j, i=i: (k, i, 0, 0)), 1.0, grads.get("ple_w_proj"), name=f"ple_gw_proj_{i}")
        dh_out = dh
        dg, du = run(ffn_bwd_act, dh_out, w[f"ffn2_w_down/{i}"], 0, s["g2"], s["u2"], name=f"ffn2_bwd_act_{i}")
        ffn_wgrads("ffn2", s["hn2"], dh_out, s["a2"], dg, du, i, grads, run)
        dh, norm_g["ffn2_norm"][i] = run(ffn_bwd_in, dg, du, w[f"ffn2_w_gate/{i}"], w[f"ffn2_w_up/{i}"], 0, s["h2"], _vec(w["ffn2_norm"], i), dh_out,
                                         name=f"ffn2_bwd_in_{i}")
        dh_out = dh
        if i == 1:
            d_o = nt_mm(dh_out, w_rec_out, "rec_bwd_out")
            grads["rec_w_out"] = run(tn_mm, s["ro"], lambda tk: _bs((tk, 256), lambda k, j: (j, k)), dh_out, lambda tk: _bs((tk, D_MODEL), lambda k, j: (j, 0)),
                                     N_SHARD, t, 256, D_MODEL, S((N_SHARD, 256, D_MODEL), BF16), _bs((None, 256, D_MODEL), lambda k, j: (k, 0, 0)),
                                     1.0, None, name="rec_gw_out").reshape(N_SHARD, 1, 256, D_MODEL)
            dz, dga, dgx, small = rec_bwd(d_o, s["z"], s["xb"], s["ga"], s["gx"], s["hs"], w["rec_conv_w"], lru_wa, lru_wx, w["lru_lambda"], "rec_bwd")
            blk = lambda tk: _bs((tk, LRU_BLOCK), lambda k, j: (j, k))
            for nm, dgt in (("lru_wa", dga), ("lru_wx", dgx)):
                gw = run(tn_mm, s["xb"], blk, dgt, blk, LRU_BLOCKS, t, LRU_BLOCK, LRU_BLOCK, S((LRU_BLOCKS, LRU_BLOCK, LRU_BLOCK), BF16),
                         _bs((None, LRU_BLOCK, LRU_BLOCK), lambda k, j: (k, 0, 0)), 1.0, None, name="rec_gw_" + nm)
                grads[nm] = jnp.transpose(gw.reshape(LRU_BLOCKS, N_SHARD, 64, LRU_BLOCK), (1, 0, 2, 3)).reshape(N_SHARD, 1, LRU_BLOCKS, 64, LRU_BLOCK)
            grads["lru_ba"], grads["lru_bx"], grads["lru_lambda"], grads["rec_conv_b"] = (small[r:r + 1] for r in range(4))
            grads["rec_conv_w"] = small[4:8]
            nb_, bw = N_SHARD, 512
            w_in, nm_in = w_rec_in, "rec_w_in"
        else:
            dy = nt_mm(dh_out, w_hyb_out, "hyb_bwd_out")
            grads["hyb_w_out"] = run(tn_mm, s["y2"], lambda tk: _bs((None, tk, 256), lambda k, j: (k // 2, j, k % 2)), dh_out,
                                     lambda tk: _bs((tk, D_MODEL), lambda k, j: (j, 0)),
                                     N_SHARD, t, 256, D_MODEL, S((N_SHARD, 256, D_MODEL), BF16), _bs((None, 256, D_MODEL), lambda k, j: (k, 0, 0)),
                                     1.0, None, name="hyb_gw_out").reshape(N_SHARD, 1, 256, D_MODEL)
            delta, dya = attn_delta(dy, s["ya"], "attn_delta")
            dqs, dks, dvs, dbs = [], [], [], []
            for g, dil in enumerate(DILATIONS):
                dq, db = attn_bwd_dq(*s["qkv"][dil], dya[dil], s["lt"][dil], delta[dil], bias_q[g], dil, f"attn_bwd_dq_{dil}")
                dk, dv = attn_bwd_dkv(*s["qkv"][dil], dya[dil], s["lt"][dil], delta[dil], bias_k[g], dil, f"attn_bwd_dkv_{dil}")
                dqs.append(dq); dks.append(dk); dvs.append(dv); dbs.append(db)
            grads["rel_bias"] = rel_bias_grad(dbs, "rel_bias_grad")
            dz, grads["hyb_conv_w"], dqg, dkg = hyb_dz(s["z"], dy, w["hyb_conv_w"], qg, kg, dqs, dks, dvs, "hyb_dz")
            grads["hyb_q_gain"] = jnp.sum(dqg.reshape(N_HEADS, HEAD_DIM), axis=0, keepdims=True)
            grads["hyb_k_gain"] = jnp.sum(dkg.reshape(N_HEADS, HEAD_DIM), axis=0, keepdims=True)
            nb_, bw = N_SHARD, 768
            w_in, nm_in = w_hyb_in, "hyb_w_in"
        grads[nm_in] = run(tn_mm, s["hnm"], lambda tk: _bs((tk, D_MODEL), lambda k, j: (j, 0)), dz, lambda tk, bw=bw: _bs((tk, bw), lambda k, j: (j, k)),
                           nb_, t, D_MODEL, bw, S((nb_, D_MODEL, bw), BF16), _bs((None, D_MODEL, bw), lambda k, j: (k, 0, 0)),
                           1.0, None, name=f"mix_gw_in_{i}").reshape(nb_, 1, D_MODEL, bw)
        dh, norm_g["mix_norm"][i] = run(
            nt_acc_normbwd, [(dz, _bs((_row_tile(t, BWD_IN_ROWS), bw), lambda r, k: (r, k)), w_in, _bs((None, D_MODEL, bw), lambda r, k: (k, 0, 0)))],
            nb_, s["h1"], _vec(w["mix_norm"], i), dh_out, name=f"mix_bwd_in_{i}")
        dh_out = dh
        dg, du = run(ffn_bwd_act, dh_out, w[f"ffn1_w_down/{i}"], 0, s["g1"], s["u1"], name=f"ffn1_bwd_act_{i}")
        ffn_wgrads("ffn1", s["hn1"], dh_out, s["a1"], dg, du, i, grads, run)
        dh, norm_g["ffn1_norm"][i] = run(ffn_bwd_in, dg, du, w[f"ffn1_w_gate/{i}"], w[f"ffn1_w_up/{i}"], 0, s["h0"], _vec(w["ffn1_norm"], i), dh_out,
                                         name=f"ffn1_bwd_in_{i}")
    for n, (g0, g1) in norm_g.items():
        grads[n] = jnp.concatenate([g0, g1], axis=0)
    return loss, dh, grads


def gather_weights(shards, name):
    n = len(shards)

    def body(*refs):
        ins, outs = refs[:n], refs[n:2 * n]
        send1, recv1, send2, recv2, lsem = refs[2 * n:]
        x, y, c, k, chips, kk = _place()
        sib = (x, y, 1 - c)

        def remote(src, dst, ssem, rsem, to):
            return pltpu.make_async_remote_copy(src_ref=src, dst_ref=dst, send_sem=ssem, recv_sem=rsem, device_id=to, device_id_type=MESH)

        local = [pltpu.make_async_copy(ins[a], outs[a].at[k], lsem.at[a]) for a in range(n)]
        for cp in local:
            cp.start()
        sends = []
        for a in range(n):
            for j, chip in enumerate(chips):
                cp = remote(ins[a].at[c], outs[a].at[k, c], send1.at[3 * a + j], recv1.at[3 * a + j], (*chip, c))
                cp.start()
                sends.append(cp)
        for a in range(n):
            for j, chip in enumerate(chips):
                remote(ins[a].at[c], outs[a].at[kk[j], c], send1.at[3 * a + j], recv1.at[3 * a + j], (*chip, c)).wait_recv()
                cp = remote(outs[a].at[kk[j], c], outs[a].at[kk[j], c], send2.at[3 * a + j], recv2.at[3 * a + j], sib)
                cp.start()
                sends.append(cp)
        for a in range(n):
            for j in range(3):
                remote(outs[a].at[kk[j], 1 - c], outs[a].at[kk[j], 1 - c], send2.at[3 * a + j], recv2.at[3 * a + j], sib).wait_recv()
        for cp in sends:
            cp.wait_send()
        for cp in local:
            cp.wait()

    return pl.pallas_call(
        body, name=name,
        in_specs=[_ANY] * n, out_specs=[_ANY] * n,
        out_shape=[S((N_SHARD,) + s.shape, s.dtype) for s in shards],
        scratch_shapes=[pltpu.SemaphoreType.DMA((3 * n,))] * 4 + [pltpu.SemaphoreType.DMA((n,))],
    )(*shards)


def exchange_cores(rs, name):
    n = len(rs)

    def body(*refs):
        outs = refs[n:2 * n]
        send, recv = refs[2 * n:]
        x, y, c = lax.axis_index("x"), lax.axis_index("y"), lax.axis_index("c")
        sends = []
        for a in range(n):
            for k in range(N_SHARD):
                slot = outs[a].at[2 * k + c]
                cp = _remote(slot, slot, send.at[N_SHARD * a + k], recv.at[N_SHARD * a + k], (x, y, 1 - c))
                cp.start()
                sends.append(cp)
        for a in range(n):
            for k in range(N_SHARD):
                slot = outs[a].at[2 * k + 1 - c]
                _remote(slot, slot, send.at[N_SHARD * a + k], recv.at[N_SHARD * a + k], (x, y, 1 - c)).wait_recv()
        for cp in sends:
            cp.wait_send()

    return pl.pallas_call(
        body, name=name,
        in_specs=[_ANY] * n, out_specs=[_ANY] * n,
        out_shape=[S(r.shape, r.dtype) for r in rs],
        input_output_aliases={a: a for a in range(n)},
        scratch_shapes=[pltpu.SemaphoreType.DMA((N_SHARD * n,))] * 2,
    )(*rs)


def allgather8(a, name):
    def body(a_ref, o_ref, send, recv, lsem):
        x, y, c = lax.axis_index("x"), lax.axis_index("y"), lax.axis_index("c")
        me = 4 * x + 2 * y + c
        local = pltpu.make_async_copy(a_ref, o_ref.at[me], lsem)
        local.start()
        cps = []
        for f in range(1, N_DEV):
            fx, fy, fc = (f >> 2) & 1, (f >> 1) & 1, f & 1
            peer = (1 - x if fx else x, 1 - y if fy else y, 1 - c if fc else c)
            cp = pltpu.make_async_remote_copy(src_ref=a_ref, dst_ref=o_ref.at[me], send_sem=send.at[f - 1], recv_sem=recv.at[f - 1],
                                              device_id=peer, device_id_type=MESH)
            cp.start()
            cps.append((cp, 4 * peer[0] + 2 * peer[1] + peer[2], f))
        for cp, pidx, f in cps:
            pltpu.make_async_remote_copy(src_ref=a_ref, dst_ref=o_ref.at[pidx], send_sem=send.at[f - 1], recv_sem=recv.at[f - 1],
                                         device_id=(x, y, c), device_id_type=MESH).wait_recv()
        for cp, _, _ in cps:
            cp.wait_send()
        local.wait()

    return pl.pallas_call(
        body, name=name, in_specs=[_ANY], out_specs=_ANY,
        out_shape=S((N_DEV,) + a.shape, a.dtype),
        scratch_shapes=[pltpu.SemaphoreType.DMA((N_DEV - 1,)), pltpu.SemaphoreType.DMA((N_DEV - 1,)), pltpu.SemaphoreType.DMA],
    )(a)


def sum8(a, name):
    _, r, c = a.shape

    def body(a_ref, o_ref):
        acc = a_ref[0]
        for j in range(1, N_DEV):
            acc = acc + a_ref[j]
        o_ref[...] = acc

    return pl.pallas_call(
        body, name=name, in_specs=[_bs((N_DEV, r, c), lambda: (0, 0, 0))], out_specs=_bs((r, c), lambda: (0, 0)),
        out_shape=S((r, c), F32),
    )(a)


def adamw(w, m, v, g, name):
    nl, r, c = w.shape
    tr = _row_tile(r, 256)
    summed = g.ndim == 4

    def body(w_ref, m_ref, v_ref, g_ref, go_ref, d_ref, mo_ref, vo_ref):
        if summed:
            gr = g_ref[0].astype(F32)
            for j in range(1, N_DEV):
                gr = gr + g_ref[j].astype(F32)
        else:
            gr = g_ref[...]
        m_new = ADAM_B1 * m_ref[...] + (1.0 - ADAM_B1) * gr
        v_new = ADAM_B2 * v_ref[...] + (1.0 - ADAM_B2) * (gr * gr)
        m_hat = m_new / (1.0 - ADAM_B1 ** ADAM_STEP)
        v_hat = v_new / (1.0 - ADAM_B2 ** ADAM_STEP)
        go_ref[...] = gr
        d_ref[...] = -ADAM_LR * (m_hat / (jnp.sqrt(v_hat) + ADAM_EPS) + ADAM_WD * w_ref[...])
        mo_ref[...] = m_new
        vo_ref[...] = v_new

    row = _bs((None, tr, c), lambda l, i: (l, i, 0))
    gspec = _bs((N_DEV, None, tr, c), lambda l, i: (0, l, i, 0)) if summed else row
    return pl.pallas_call(
        body, name=name, grid=(nl, r // tr),
        in_specs=[row, row, row, gspec], out_specs=[row] * 4, out_shape=[S((nl, r, c), F32)] * 4,
        compiler_params=_cp("parallel", "parallel"),
    )(w, m, v, g)


WEIGHTS = ["rel_bias", "ffn1_norm", "ffn1_w_gate", "ffn1_w_up", "ffn1_w_down", "mix_norm", "hyb_w_in", "hyb_conv_w", "hyb_q_gain",
           "hyb_k_gain", "hyb_w_out", "rec_w_in", "rec_conv_w", "rec_conv_b", "lru_wa", "lru_ba", "lru_wx", "lru_bx", "lru_lambda",
           "rec_w_out", "ffn2_norm", "ffn2_w_gate", "ffn2_w_up", "ffn2_w_down", "ple_norm", "ple_w_gate", "ple_w_proj"]
BIG = ["ffn1_w_gate", "ffn1_w_up", "ffn1_w_down", "hyb_w_in", "hyb_w_out", "rec_w_in", "lru_wa", "lru_wx", "rec_w_out",
       "ffn2_w_gate", "ffn2_w_up", "ffn2_w_down", "ple_w_gate", "ple_w_proj"]
SMALL_SHARDED = ["hyb_conv_w", "rec_conv_w", "rec_conv_b", "lru_ba", "lru_bx", "lru_lambda"]
SMALL = ["rel_bias", "ffn1_norm", "mix_norm", "ffn2_norm", "ple_norm", "hyb_q_gain", "hyb_k_gain"] + SMALL_SHARDED
PACK_W = 1024
PER_LAYER = ["ffn1_w_gate", "ffn1_w_up", "ffn1_w_down", "ffn2_w_gate", "ffn2_w_up", "ffn2_w_down", "ple_w_gate", "ple_w_proj"]
FIRST = ["ffn1_w_gate/0", "ffn1_w_up/0"]
LAST = ["ffn1_w_down"]
GATHER_PLAN = {
    "ffn1_up_0": ["ffn1_w_down/0", "hyb_w_in"],
    "ffn1_down_0": ["hyb_w_out", "ple_w_gate/0", "ple_w_proj/0"],
    "hyb_in": ["ffn2_w_gate/0"],
    "attn_fwd_1": ["ffn2_w_up/0"],
    "attn_fwd_4": ["ffn2_w_down/0"],
    "ffn2_up_0": ["ffn1_w_gate/1", "ffn1_w_up/1"],
    "ffn2_down_0": ["ffn1_w_down/1"],
    "ffn1_up_1": ["rec_w_in", "lru_wa", "lru_wx", "rec_w_out", "ffn2_w_gate/1"],
    "ffn1_down_1": ["ffn2_w_up/1"],
    "rec_in": ["ffn2_w_down/1", "ple_w_gate/1", "ple_w_proj/1"],
}
SCATTER_PLAN = {
    "ple_gw_proj_1": [("ple_w_gate", 1)],
    "ffn2_bwd_act_1": [("ple_w_proj", 1)],
    "ffn2_gw_1_up": [("ffn2_w_gate", 1)],
    "ffn2_gw_1_down": [("ffn2_w_up", 1)],
    "ffn2_bwd_in_1": [("ffn2_w_down", 1)],
    "mix_bwd_in_1": [("rec_w_in", 0), ("rec_w_out", 0), ("lru_wa", 0), ("lru_wx", 0)],
    "ffn1_gw_1_up": [("ffn1_w_gate", 1)],
    "ffn1_gw_1_down": [("ffn1_w_up", 1)],
    "ffn1_bwd_in_1": [("ffn1_w_down", 1)],
    "ple_gw_proj_0": [("ple_w_gate", 0)],
    "ffn2_bwd_act_0": [("ple_w_proj", 0)],
    "ffn2_gw_0_up": [("ffn2_w_gate", 0)],
    "ffn2_gw_0_down": [("ffn2_w_up", 0)],
    "ffn2_bwd_in_0": [("ffn2_w_down", 0)],
    "mix_bwd_in_0": [("hyb_w_in", 0), ("hyb_w_out", 0)],
    "ffn1_gw_0_up": [("ffn1_w_gate", 0)],
    "ffn1_gw_0_down": [("ffn1_w_up", 0)],
    "ffn1_bwd_in_0": [("ffn1_w_down", 0)],
}
FORWARD_PLAN = {
    "ffn1_gw_1_up": ["rec_w_in", "rec_w_out", "lru_wa", "lru_wx"],
    "ffn2_bwd_in_0": ["ple_w_gate", "ple_w_proj", "ffn2_w_gate", "ffn2_w_up"],
    "mix_bwd_in_0": ["ffn2_w_down"],
    "ffn1_gw_0_up": ["hyb_w_in", "hyb_w_out"],
    "ffn1_bwd_in_0": ["ffn1_w_gate", "ffn1_w_up"],
}


class Plan:
    def __init__(self, shards, w):
        self.shards, self.w, self.grads, self.landed = shards, w, None, {}

    def host(self, kname):
        if kname in GATHER_PLAN:
            h = Host("gather", [self.shards[n] for n in GATHER_PLAN[kname]])
            h.names = GATHER_PLAN[kname]
            return h
        if kname in SCATTER_PLAN or kname in FORWARD_PLAN:
            items = SCATTER_PLAN.get(kname, [])
            fwd = FORWARD_PLAN.get(kname, [])
            h = Host("scatter", [(self.grads[n], lay, self.landed.get(n)) for n, lay in items], [self.landed[n] for n in fwd])
            h.names = [n for n, _ in items] + fwd
            return h
        return None

    def done(self, h):
        for n, o in zip(h.names, h.outs):
            if h.kind == "gather":
                self.w[n] = o
            else:
                self.landed[n] = o


def _halves(a):
    if a.shape[0] == 2:
        return a
    return a.reshape((2, a.shape[1] // 2) + a.shape[2:])


def _pack_rows(arrs, width):
    rows, offs, r0 = [], [], 0
    for a in arrs:
        if a.shape[1] > width:
            a = a.reshape(-1, width)
        rows.append(jnp.pad(a, ((0, 0), (0, width - a.shape[1]))))
        offs.append(r0)
        r0 += a.shape[0]
    pad = (-r0) % 8
    if pad:
        rows.append(jnp.zeros((pad, width), F32))
    return jnp.concatenate(rows, axis=0), offs


def kernel(x, p, rel_bias, ffn1_norm, ffn1_w_gate, ffn1_w_up, ffn1_w_down, mix_norm, hyb_w_in, hyb_conv_w, hyb_q_gain, hyb_k_gain, hyb_w_out, rec_w_in, rec_conv_w, rec_conv_b, lru_wa, lru_ba, lru_wx, lru_bx, lru_lambda, rec_w_out, ffn2_norm, ffn2_w_gate, ffn2_w_up, ffn2_w_down, ple_norm, ple_w_gate, ple_w_proj, loss_target, m_rel_bias, m_ffn1_norm, m_ffn1_w_gate, m_ffn1_w_up, m_ffn1_w_down, m_mix_norm, m_hyb_w_in, m_hyb_conv_w, m_hyb_q_gain, m_hyb_k_gain, m_hyb_w_out, m_rec_w_in, m_rec_conv_w, m_rec_conv_b, m_lru_wa, m_lru_ba, m_lru_wx, m_lru_bx, m_lru_lambda, m_rec_w_out, m_ffn2_norm, m_ffn2_w_gate, m_ffn2_w_up, m_ffn2_w_down, m_ple_norm, m_ple_w_gate, m_ple_w_proj, v_rel_bias, v_ffn1_norm, v_ffn1_w_gate, v_ffn1_w_up, v_ffn1_w_down, v_mix_norm, v_hyb_w_in, v_hyb_conv_w, v_hyb_q_gain, v_hyb_k_gain, v_hyb_w_out, v_rec_w_in, v_rec_conv_w, v_rec_conv_b, v_lru_wa, v_lru_ba, v_lru_wx, v_lru_bx, v_lru_lambda, v_rec_w_out, v_ffn2_norm, v_ffn2_w_gate, v_ffn2_w_up, v_ffn2_w_down, v_ple_norm, v_ple_w_gate, v_ple_w_proj):
    given = dict(locals())
    wts = {n: given[n] for n in WEIGHTS}
    k_chip = 2 * lax.axis_index("x") + lax.axis_index("y")

    shards = {}
    for n in BIG:
        b16 = wts[n].astype(BF16)
        if n in PER_LAYER:
            shards[n + "/0"], shards[n + "/1"] = b16[0:1], b16[1:2]
        else:
            shards[n] = b16
    first = gather_weights([_halves(shards[n]) for n in FIRST], "gather_first")
    w = {n: g.reshape((N_SHARD,) + shards[n].shape) for n, g in zip(FIRST, first)}
    plan = Plan(shards, w)
    sm2d = {n: wts[n].reshape(-1, wts[n].shape[-1]) for n in SMALL_SHARDED}
    slab, offs = _pack_rows([sm2d[n] for n in SMALL_SHARDED], 256)
    slabs = allgather8(slab, "gather_small")[0::2]
    for n, o in zip(SMALL_SHARDED, offs):
        r, cw = sm2d[n].shape
        w[n] = jnp.concatenate([slabs[kc, o:o + r, :cw] for kc in range(N_SHARD)], axis=1)
    for n in SMALL:
        if n not in SMALL_SHARDED:
            w[n] = wts[n]

    loss, dx, grads = local_step(x[0], p[:, 0], loss_target[0], w, plan)
    loss = lax.psum(loss[0, 0], ("x", "y", "c"))

    for n, r8 in zip(LAST, exchange_cores([plan.landed[n] for n in LAST], "exchange_cores")):
        plan.landed[n] = r8
    out = {}
    for n in BIG:
        r8 = plan.landed[n]
        shp = wts[n].shape
        shp3 = shp if len(shp) == 3 else (shp[0], -1, shp[-1])
        three = lambda a: a.reshape(shp3)
        res = adamw(three(wts[n]), three(given["m_" + n]), three(given["v_" + n]), r8.reshape((N_DEV,) + three(wts[n]).shape), "adamw_" + n)
        out[n] = [a.reshape(shp) for a in res]
    g2d = [grads[n].reshape(-1, grads[n].shape[-1]) if n != "rel_bias" else grads[n].reshape(1, -1) for n in SMALL]
    gslab, goffs = _pack_rows(g2d, PACK_W)
    gsum = sum8(allgather8(gslab, "gather_small_grads"), "sum_small_grads")
    for n, o, g in zip(SMALL, goffs, g2d):
        shp = wts[n].shape
        r, cw = g.shape
        gs = gsum[o:o + r, :cw]
        if n in SMALL_SHARDED:
            sw = shp[-1]
            gs = lax.dynamic_slice_in_dim(gs, k_chip * sw, sw, axis=1)
        three = lambda a: a.reshape((1, -1, shp[-1]))
        res = adamw(three(wts[n]), three(given["m_" + n]), three(given["v_" + n]), three(gs), "adamw_" + n)
        out[n] = [a.reshape(shp) for a in res]
    return (loss, dx[None], *[out[n][0] for n in WEIGHTS], *[out[n][1] for n in WEIGHTS],
            *[out[n][2] for n in WEIGHTS], *[out[n][3] for n in WEIGHTS])
```

```python
import functools
import math

import numpy as np
import jax
import jax.numpy as jnp
from jax import lax
from jax.experimental import pallas as pl
from jax.experimental.pallas import tpu as pltpu

F32, BF16 = jnp.float32, jnp.bfloat16
S = jax.ShapeDtypeStruct
MESH = pl.DeviceIdType.MESH

D_MODEL = 1024
N_SHARD = 4
N_DEV = 8
HEAD_DIM = 64
N_HEADS = 8
ATTN_W = N_HEADS * HEAD_DIM
CONV_W = 512
BAND = 128
DILATIONS = (1, 4, 16)
REL_BUCKETS = 32
REL_MAX_DIST = 2048
LRU_BLOCKS = 4
LRU_BLOCK = 256
LRU_C = 8.0
EPS = 1e-6
NEG = -1e30
VMEM_LIMIT = 56 * 1024 * 1024
FFN_ROWS = 1024
TN_ROWS = 2048
BWD_IN_ROWS = 1024

ADAM_LR, ADAM_B1, ADAM_B2, ADAM_EPS, ADAM_WD, ADAM_STEP = 0.001, 0.9, 0.999, 1e-08, 0.01, 10


def _cp(*sem):
    return pltpu.CompilerParams(dimension_semantics=sem, vmem_limit_bytes=VMEM_LIMIT)


def _bs(shape, imap):
    return pl.BlockSpec(shape, imap)


def _row_tile(t, want):
    for cand in range(min(want, t) // 8 * 8, 0, -8):
        if t % cand == 0:
            return cand
    return t


_ANY = pl.BlockSpec(memory_space=pl.ANY)


def _place():
    x, y, c = lax.axis_index("x"), lax.axis_index("y"), lax.axis_index("c")
    chips = [(1 - x, y), (x, 1 - y), (1 - x, 1 - y)]
    return x, y, c, 2 * x + y, chips, [2 * cx + cy for cx, cy in chips]


def _remote(src, dst, ssem, rsem, to):
    return pltpu.make_async_remote_copy(src_ref=src, dst_ref=dst, send_sem=ssem, recv_sem=rsem, device_id=to, device_id_type=MESH)


class Host:
    def __init__(self, kind, items, forwards=()):
        self.kind, self.items, self.forwards, self.outs = kind, items, list(forwards), None

    def n_sems(self):
        return 3 * len(self.items) + N_SHARD * len(self.forwards), len(self.items)

    def operands(self):
        if self.kind == "gather":
            return list(self.items), [S((N_SHARD,) + s.shape, s.dtype) for s in self.items], {}
        xin, shapes, alias = [], [], {}
        for a, (g, _, r_prev) in enumerate(self.items):
            xin.append(g)
            if r_prev is not None:
                alias[len(xin)] = a
                xin.append(r_prev)
            shapes.append(S((N_DEV,) + g.shape[1:], g.dtype))
        for f, r in enumerate(self.forwards):
            alias[len(xin)] = len(self.items) + f
            xin.append(r)
            shapes.append(S(r.shape, r.dtype))
        return xin, shapes, alias

    def copies(self, xi, xo, send, recv, lsem):
        x, y, c, k, chips, kk = _place()
        starts, waits = [], []
        pos = 0
        for f in range(len(self.forwards)):
            arr = xo[len(self.items) + f]
            for kq in range(N_SHARD):
                sem = 3 * len(self.items) + N_SHARD * f + kq
                cp = _remote(arr.at[2 * kq + c], arr.at[2 * kq + c], send.at[sem], recv.at[sem], (x, y, 1 - c))
                starts.append((cp, "start"))
                waits.append((cp, "wait_send"))
                other = arr.at[2 * kq + 1 - c]
                waits.append((_remote(other, other, send.at[sem], recv.at[sem], (x, y, 1 - c)), "wait_recv"))
        for a, item in enumerate(self.items):
            if self.kind == "gather":
                src_of = lambda chip_idx, s=xi[a]: s
                dst_of = lambda chip_idx, o=xo[a]: o.at[chip_idx]
                mine, theirs = k, kk
            else:
                g_ref = xi[pos]
                pos += 1 if item[2] is None else 2
                lay = item[1]
                src_of = lambda chip_idx, g=g_ref, lay=lay: g.at[chip_idx, lay]
                dst_of = lambda slot, o=xo[a], lay=lay: o.at[slot, lay]
                mine, theirs = 2 * k + c, [2 * kj + c for kj in kk]
            own_src = src_of(k)
            local = pltpu.make_async_copy(own_src, dst_of(mine), lsem.at[a])
            starts.append((local, "start"))
            waits.append((local, "wait"))
            for j, chip in enumerate(chips):
                src = own_src if self.kind == "gather" else src_of(kk[j])
                cp = _remote(src, dst_of(mine), send.at[3 * a + j], recv.at[3 * a + j], (*chip, c))
                starts.append((cp, "start"))
                waits.append((cp, "wait_send"))
                waits.append((_remote(own_src, dst_of(theirs[j]), send.at[3 * a + j], recv.at[3 * a + j], (*chip, c)), "wait_recv"))
        return starts, waits


def _call(host, body, *, name, grid, in_specs, out_specs, out_shape, scratch_shapes=(), compiler_params=None, args, aliases=None):
    aliases = dict(aliases or {})
    if host is None:
        return pl.pallas_call(body, name=name, grid=grid, in_specs=in_specs, out_specs=out_specs, out_shape=out_shape,
                              scratch_shapes=list(scratch_shapes), input_output_aliases=aliases, compiler_params=compiler_params)(*args)
    single = not isinstance(out_shape, (list, tuple))
    out_specs_l = [out_specs] if single else list(out_specs)
    out_shape_l = [out_shape] if single else list(out_shape)
    n_in, n_out, n_scr = len(in_specs), len(out_shape_l), len(scratch_shapes)
    xin, xshapes, xalias = host.operands()
    n_items = len(xshapes)
    n_rsem, n_lsem = host.n_sems()
    for i_in, i_out in xalias.items():
        aliases[n_in + i_in] = n_out + i_out
    nd = len(grid)

    def hosted(*refs):
        ins, xi = refs[:n_in], refs[n_in:n_in + len(xin)]
        o0 = n_in + len(xin)
        outs, xo = refs[o0:o0 + n_out], refs[o0 + n_out:o0 + n_out + n_items]
        s0 = o0 + n_out + n_items
        scr = refs[s0:s0 + n_scr]
        send, recv, lsem = refs[s0 + n_scr:]
        first = functools.reduce(jnp.logical_and, [pl.program_id(d) == 0 for d in range(nd)])
        last = functools.reduce(jnp.logical_and, [pl.program_id(d) == grid[d] - 1 for d in range(nd)])
        starts, waits = host.copies(xi, xo, send, recv, lsem)

        @pl.when(first)
        def _():
            for cp, how in starts:
                getattr(cp, how)()
        body(*ins, *outs, *scr)

        @pl.when(last)
        def _():
            for cp, how in waits:
                getattr(cp, how)()

    res = pl.pallas_call(
        hosted, name=name, grid=grid,
        in_specs=list(in_specs) + [_ANY] * len(xin),
        out_specs=out_specs_l + [_ANY] * n_items,
        out_shape=out_shape_l + xshapes,
        scratch_shapes=list(scratch_shapes) + [pltpu.SemaphoreType.DMA((n_rsem,)), pltpu.SemaphoreType.DMA((n_rsem,)),
                                               pltpu.SemaphoreType.DMA((max(n_lsem, 1),))],
        input_output_aliases=aliases,
        compiler_params=pltpu.CompilerParams(dimension_semantics=("arbitrary",) * nd, vmem_limit_bytes=VMEM_LIMIT),
    )(*args, *xin)
    host.outs = list(res[n_out:])
    return res[0] if single else list(res[:n_out])


def _rstd(x):
    return lax.rsqrt(jnp.mean(x * x, axis=-1, keepdims=True) + EPS)


def _sigmoid(x):
    return 1.0 / (1.0 + jnp.exp(-x))


def _dot(a, b):
    return jnp.dot(a, b, preferred_element_type=F32)


def _dot_nt(a, b):
    return lax.dot_general(a, b, (((1,), (1,)), ((), ())), preferred_element_type=F32)


def _dot_tn(a, b):
    return lax.dot_general(a, b, (((0,), (0,)), ((), ())), preferred_element_type=F32)


def _seg_dot(x, seg_bf16):
    hi = x.astype(BF16)
    lo = (x - hi.astype(F32)).astype(BF16)
    return _dot(hi, seg_bf16) + _dot(lo, seg_bf16)


def _shift_down(x, prev8, s):
    if s == 0:
        return x
    tm = x.shape[0]
    row = lax.broadcasted_iota(jnp.int32, x.shape, 0)
    main = jnp.where(row >= s, pltpu.roll(x, s, axis=0), 0.0)
    row8 = lax.broadcasted_iota(jnp.int32, prev8.shape, 0)
    head = jnp.where(row8 < s, pltpu.roll(prev8, s, axis=0), 0.0)
    if tm == 8:
        return main + head
    return main + jnp.concatenate([head, jnp.zeros((tm - 8, x.shape[1]), x.dtype)], axis=0)


def _shift_up(x, next8, s):
    if s == 0:
        return x
    tm = x.shape[0]
    row = lax.broadcasted_iota(jnp.int32, x.shape, 0)
    main = jnp.where(row < tm - s, pltpu.roll(x, tm - s, axis=0), 0.0)
    row8 = lax.broadcasted_iota(jnp.int32, next8.shape, 0)
    tail = jnp.where(row8 >= 8 - s, pltpu.roll(next8, 8 - s, axis=0), 0.0)
    if tm == 8:
        return main + tail
    return main + jnp.concatenate([jnp.zeros((tm - 8, x.shape[1]), x.dtype), tail], axis=0)


def _roll_fill(x, s, fill, up):
    tm = x.shape[0]
    row = lax.broadcasted_iota(jnp.int32, x.shape, 0)
    if up:
        return jnp.where(row < tm - s, pltpu.roll(x, tm - s, axis=0), fill)
    return jnp.where(row >= s, pltpu.roll(x, s, axis=0), fill)


def _log1p(y):
    u = 1.0 + y
    return jnp.where(u == 1.0, y, jnp.log(u) * (y / jnp.where(u == 1.0, 1.0, u - 1.0)))


def _softplus(x):
    return jnp.maximum(x, 0.0) + _log1p(jnp.exp(-jnp.abs(x)))


def _neg_expm1(y):
    series = -y * (1.0 + y * (0.5 + y * (1.0 / 6.0 + y * (1.0 / 24.0 + y * (1.0 / 120.0)))))
    return jnp.where(jnp.abs(y) < 0.03, series, 1.0 - jnp.exp(y))


_GELU_C = math.sqrt(2.0 / math.pi)


def _gelu_and_grad(x):
    inner = _GELU_C * (x + 0.044715 * x * x * x)
    t = jnp.tanh(inner)
    g = 0.5 * x * (1.0 + t)
    dg = 0.5 * (1.0 + t) + 0.5 * x * (1.0 - t * t) * _GELU_C * (1.0 + 3.0 * 0.044715 * x * x)
    return g, dg


def _rmsnorm_bwd(x, gain, dy):
    r = _rstd(x)
    xhat = x * r
    dxhat = dy * gain
    dx = r * (dxhat - xhat * jnp.mean(dxhat * xhat, axis=-1, keepdims=True))
    return dx, jnp.sum(dy * xhat, axis=0, keepdims=True)


def ffn_up(h, gain, wg, wu, layer, name, host=None):
    t, d = h.shape
    nk, _, _, f = wg.shape
    tm = _row_tile(t, FFN_ROWS)

    def body(h_ref, g_ref, wg_ref, wu_ref, hn_ref, gg_ref, uu_ref, aa_ref, hn_scr):
        @pl.when(pl.program_id(1) == 0)
        def _():
            x = h_ref[...]
            hn = (x * _rstd(x) * g_ref[...]).astype(BF16)
            hn_scr[...] = hn
            hn_ref[...] = hn
        hn = hn_scr[...]
        g = _dot(hn, wg_ref[...])
        u = _dot(hn, wu_ref[...])
        s = _sigmoid(g)
        silu = g * s
        gg_ref[...] = (u * (s * (1.0 + g * (1.0 - s)))).astype(BF16)
        uu_ref[...] = silu.astype(BF16)
        aa_ref[...] = (silu * u).astype(BF16)

    wspec = _bs((None, None, d, f), lambda i, k: (k, layer, 0, 0))
    aspec = _bs((None, tm, f), lambda i, k: (k, i, 0))
    return _call(
        host, body, name=name, grid=(t // tm, nk),
        in_specs=[_bs((tm, d), lambda i, k: (i, 0)), _bs((1, d), lambda i, k: (0, 0)), wspec, wspec],
        out_specs=[_bs((tm, d), lambda i, k: (i, 0)), aspec, aspec, aspec],
        out_shape=[S((t, d), BF16), S((nk, t, f), BF16), S((nk, t, f), BF16), S((nk, t, f), BF16)],
        scratch_shapes=[pltpu.VMEM((tm, d), BF16)],
        compiler_params=_cp("parallel", "arbitrary"),
        args=(h, gain, wg, wu))


def mm_acc(a, a_spec, b, b_spec, res, scale, nk, t, n, tm, name, host=None):
    def body(a_ref, b_ref, r_ref, o_ref, acc):
        k = pl.program_id(1)

        @pl.when(k == 0)
        def _():
            acc[...] = jnp.zeros_like(acc)
        acc[...] += _dot(a_ref[...].astype(BF16), b_ref[...])

        @pl.when(k == nk - 1)
        def _():
            o_ref[...] = r_ref[...] + scale * acc[...]

    return _call(
        host, body, name=name, grid=(t // tm, nk),
        in_specs=[a_spec, b_spec, _bs((tm, n), lambda i, k: (i, 0))],
        out_specs=_bs((tm, n), lambda i, k: (i, 0)),
        out_shape=S((t, n), F32),
        scratch_shapes=[pltpu.VMEM((tm, n), F32)],
        compiler_params=_cp("parallel", "arbitrary"),
        args=(a, b, res))


def ffn_down(a, wd, layer, h, name, host=None):
    nk, t, f = a.shape
    d = h.shape[1]
    tm = _row_tile(t, FFN_ROWS)
    return mm_acc(a, _bs((None, tm, f), lambda i, k: (k, i, 0)),
                  wd, _bs((None, None, f, d), lambda i, k: (k, layer, 0, 0)),
                  h, 0.5, nk, t, d, tm, name, host)


def ffn_bwd_act(dh, wd, layer, gg, uu, name, host=None):
    nk, t, f = gg.shape
    d = dh.shape[1]
    tm = _row_tile(t, FFN_ROWS)

    def body(dh_ref, wd_ref, g_ref, u_ref, dg_ref, du_ref):
        da = 0.5 * _dot_nt(dh_ref[...].astype(BF16), wd_ref[...])
        dg_ref[...] = (da * g_ref[...].astype(F32)).astype(BF16)
        du_ref[...] = (da * u_ref[...].astype(F32)).astype(BF16)

    aspec = _bs((None, tm, f), lambda i, k: (k, i, 0))
    return _call(
        host, body, name=name, grid=(t // tm, nk),
        in_specs=[_bs((tm, d), lambda i, k: (i, 0)), _bs((None, None, f, d), lambda i, k: (k, layer, 0, 0)), aspec, aspec],
        out_specs=[aspec, aspec],
        out_shape=[S((nk, t, f), BF16), S((nk, t, f), BF16)],
        compiler_params=_cp("parallel", "arbitrary"),
        args=(dh, wd, gg, uu))


def nt_acc_normbwd(terms, nk, h, gain, dh, name, host=None):
    t, d = h.shape
    tm = _row_tile(t, BWD_IN_ROWS)
    sub = _row_tile(tm, 256)
    nterm = len(terms)

    def body(*refs):
        xs = refs[:2 * nterm]
        h_ref, g_ref, dh_ref, o_ref, dg_ref, acc = refs[2 * nterm:]
        i, k = pl.program_id(0), pl.program_id(1)

        @pl.when(k == 0)
        def _():
            acc[...] = jnp.zeros_like(acc)

        @pl.when(jnp.logical_and(i == 0, k == 0))
        def _():
            dg_ref[...] = jnp.zeros_like(dg_ref)
        tot = _dot_nt(xs[0][...], xs[1][...])
        for j in range(1, nterm):
            tot = tot + _dot_nt(xs[2 * j][...], xs[2 * j + 1][...])
        acc[...] += tot

        @pl.when(k == nk - 1)
        def _():
            def rows_of(cidx, dgain):
                rows = pl.ds(pl.multiple_of(cidx * sub, sub), sub)
                dx, dgc = _rmsnorm_bwd(h_ref[rows, :], g_ref[...], acc[rows, :])
                o_ref[rows, :] = dh_ref[rows, :] + dx
                return dgain + dgc
            dg_ref[...] += lax.fori_loop(0, tm // sub, rows_of, jnp.zeros((1, d), F32))

    in_specs, args = [], []
    for x, xs_, w, ws_ in terms:
        in_specs += [xs_, ws_]
        args += [x, w]
    row = _bs((tm, d), lambda i, k: (i, 0))
    vec = _bs((1, d), lambda i, k: (0, 0))
    return _call(
        host, body, name=name, grid=(t // tm, nk),
        in_specs=in_specs + [row, vec, row],
        out_specs=[row, vec],
        out_shape=[S((t, d), F32), S((1, d), F32)],
        scratch_shapes=[pltpu.VMEM((tm, d), F32)],
        compiler_params=_cp("arbitrary", "arbitrary"),
        args=(*args, h, gain, dh))


def ffn_bwd_in(dg, du, wg, wu, layer, h, gain, dh, name, host=None):
    nk, t, f = dg.shape
    d = h.shape[1]
    tm = _row_tile(t, BWD_IN_ROWS)
    aspec = _bs((None, tm, f), lambda i, k: (k, i, 0))
    wspec = _bs((None, None, d, f), lambda i, k: (k, layer, 0, 0))
    return nt_acc_normbwd([(dg, aspec, wg, wspec), (du, aspec, wu, wspec)], nk, h, gain, dh, name, host)


def tn_mm(x, x_spec, y, y_spec, nblk, t, ka, nb, out_shape, out_spec, scale, prev, name, host=None):
    tk = _row_tile(t, TN_ROWS)

    def body(*refs):
        if prev is None:
            x_ref, y_ref, o_ref, acc = refs
        else:
            x_ref, y_ref, _, o_ref, acc = refs
        j = pl.program_id(1)

        @pl.when(j == 0)
        def _():
            acc[...] = jnp.zeros_like(acc)
        acc[...] += _dot_tn(x_ref[...].astype(BF16), y_ref[...].astype(BF16))

        @pl.when(j == t // tk - 1)
        def _():
            o_ref[...] = (scale * acc[...]).astype(o_ref.dtype)

    in_specs = [x_spec(tk), y_spec(tk)]
    args = [x, y]
    aliases = {}
    if prev is not None:
        in_specs.append(pl.BlockSpec(memory_space=pl.ANY))
        args.append(prev)
        aliases = {2: 0}
    return _call(
        host, body, name=name, grid=(nblk, t // tk),
        in_specs=in_specs, out_specs=out_spec, out_shape=out_shape,
        scratch_shapes=[pltpu.VMEM((ka, nb), F32)],
        aliases=aliases,
        compiler_params=_cp("parallel", "arbitrary"),
        args=tuple(args))


def ffn_wgrads(which, hn, dh, aa, dg, du, layer, grads, run):
    nk, t, f = aa.shape
    d = hn.shape[1]
    hn_spec = lambda tk: _bs((tk, d), lambda k, j: (j, 0))
    a_spec = lambda tk: _bs((None, tk, f), lambda k, j: (k, j, 0))
    shape_gu, spec_gu = S((nk, 2, d, f), BF16), _bs((None, None, d, f), lambda k, j: (k, layer, 0, 0))
    shape_d, spec_d = S((nk, 2, f, d), BF16), _bs((None, None, f, d), lambda k, j: (k, layer, 0, 0))
    for suffix, x, xs, y, ys, ka, nb, shp, spec, scale in (
            ("gate", hn, hn_spec, dg, a_spec, d, f, shape_gu, spec_gu, 1.0),
            ("up", hn, hn_spec, du, a_spec, d, f, shape_gu, spec_gu, 1.0),
            ("down", aa, a_spec, dh, hn_spec, f, d, shape_d, spec_d, 0.5)):
        key = f"{which}_w_{suffix}"
        grads[key] = run(tn_mm, x, xs, y, ys, nk, t, ka, nb, shp, spec, scale, grads.get(key), name=f"{which}_gw_{layer}_{suffix}")


def norm_mm(h, gain, w, name, host=None):
    t, d = h.shape
    nb, _, bw = w.shape
    tm = _row_tile(t, FFN_ROWS)

    def body(h_ref, g_ref, w_ref, hn_ref, z_ref, hn_scr):
        @pl.when(pl.program_id(1) == 0)
        def _():
            x = h_ref[...]
            hn = (x * _rstd(x) * g_ref[...]).astype(BF16)
            hn_scr[...] = hn
            hn_ref[...] = hn
        z_ref[...] = _dot(hn_scr[...], w_ref[...])

    return _call(
        host, body, name=name, grid=(t // tm, nb),
        in_specs=[_bs((tm, d), lambda i, k: (i, 0)), _bs((1, d), lambda i, k: (0, 0)), _bs((None, d, bw), lambda i, k: (k, 0, 0))],
        out_specs=[_bs((tm, d), lambda i, k: (i, 0)), _bs((tm, bw), lambda i, k: (i, k))],
        out_shape=[S((t, d), BF16), S((t, nb * bw), F32)],
        scratch_shapes=[pltpu.VMEM((tm, d), BF16)],
        compiler_params=_cp("parallel", "arbitrary"),
        args=(h, gain, w))


def nt_mm(a, w, name):
    t, k = a.shape
    n = w.shape[0]
    tm = _row_tile(t, 512)

    def body(a_ref, w_ref, o_ref):
        o_ref[...] = _dot_nt(a_ref[...].astype(BF16), w_ref[...])

    return pl.pallas_call(
        body, name=name, grid=(t // tm,),
        in_specs=[_bs((tm, k), lambda i: (i, 0)), _bs((n, k), lambda i: (0, 0))],
        out_specs=_bs((tm, n), lambda i: (i, 0)),
        out_shape=S((t, n), F32),
        compiler_params=_cp("parallel"),
    )(a, w)


def _head_mean_matrix():
    m = np.kron(np.eye(N_HEADS, dtype=np.float32), np.full((HEAD_DIM, HEAD_DIM), 1.0 / HEAD_DIM, np.float32))
    return jnp.asarray(m, BF16)


def _head_sum_matrix():
    m = np.kron(np.eye(N_HEADS, dtype=np.float32), np.ones((HEAD_DIM, HEAD_DIM), np.float32))
    return jnp.asarray(m, BF16)


def _rel_bucket_np(dist):
    max_exact = REL_BUCKETS // 2
    n = np.maximum(dist, 1).astype(np.float32)
    large = max_exact + (np.log(n / np.float32(max_exact)) / np.float32(math.log(REL_MAX_DIST / max_exact))
                         * np.float32(REL_BUCKETS - max_exact)).astype(np.int32)
    large = np.minimum(large, REL_BUCKETS - 1)
    return np.where(dist < max_exact, dist, large)


def _band_tables():
    qi = np.arange(BAND)[:, None]
    kj = np.arange(2 * BAND)[None, :]
    dist_q = qi + BAND - kj
    qq = np.arange(2 * BAND)[:, None]
    kk = np.arange(BAND)[None, :]
    dist_k = qq - kk
    out = []
    for dist in (dist_q, dist_k):
        valid = (dist >= 0) & (dist <= BAND)
        bucket = np.stack([_rel_bucket_np(np.clip(dist, 0, BAND) * d) for d in DILATIONS])
        out.append((bucket, valid))
    return out


def band_bias(rel_bias):
    out = []
    for bucket, valid in _band_tables():
        bucket = np.where(valid[None], bucket, -1)[:, None]
        tab = jnp.full((len(DILATIONS), N_HEADS) + bucket.shape[2:], NEG, F32)
        for b in range(REL_BUCKETS):
            if (bucket == b).any():
                tab = jnp.where(bucket == b, rel_bias[b][None, :, None, None], tab)
        out.append(tab.reshape(len(DILATIONS), N_HEADS // 2, 2 * tab.shape[2], tab.shape[3]))
    return out


LANE_TILE = 128
N_LANE_TILES = ATTN_W // LANE_TILE


def _view_shape(t, dil):
    return (t // dil, dil * ATTN_W)


def _view_spec(tm, dil):
    return _bs((tm // dil, dil * ATTN_W), lambda i: (i, 0))


def _cols_to(scr, val):
    for cc in range(N_LANE_TILES):
        scr[cc] = val[:, LANE_TILE * cc:LANE_TILE * (cc + 1)]


def _cols_from(scr):
    return jnp.concatenate([scr[cc] for cc in range(N_LANE_TILES)], axis=1)


def _write_view(scr, out_ref, dil):
    if dil == 1:
        out_ref[...] = _cols_from(scr).astype(out_ref.dtype)
        return
    rows = scr.shape[1] // dil
    for r in range(dil):
        for cc in range(N_LANE_TILES):
            c0 = r * ATTN_W + LANE_TILE * cc
            out_ref[:, c0:c0 + LANE_TILE] = scr[cc, pl.ds(r, rows, stride=dil), :].astype(out_ref.dtype)


def _read_view(scr, in_ref, dil):
    if dil == 1:
        return in_ref[...].astype(F32)
    rows = scr.shape[1] // dil
    for r in range(dil):
        for cc in range(N_LANE_TILES):
            c0 = r * ATTN_W + LANE_TILE * cc
            scr[cc, pl.ds(r, rows, stride=dil), :] = in_ref[:, c0:c0 + LANE_TILE].astype(F32)
    return _cols_from(scr)


def hyb_prep(z, q_gain, k_gain, name):
    t = z.shape[0]
    tm = _row_tile(t, 512)
    seg = _head_mean_matrix()
    nd = len(DILATIONS)

    def body(q_ref, k_ref, v_ref, qg_ref, kg_ref, seg_ref, *rest):
        outs, scr = rest[:3 * nd], rest[3 * nd]
        q = q_ref[...]
        k = k_ref[...]
        vals = (q * lax.rsqrt(_seg_dot(q * q, seg_ref[...]) + EPS) * qg_ref[...],
                k * lax.rsqrt(_seg_dot(k * k, seg_ref[...]) + EPS) * kg_ref[...],
                v_ref[...])
        for j, val in enumerate(vals):
            _cols_to(scr, val)
            for g, dil in enumerate(DILATIONS):
                _write_view(scr, outs[3 * g + j], dil)

    col = lambda c: _bs((tm, ATTN_W), lambda i: (i, c))
    vec = _bs((1, ATTN_W), lambda i: (0, 0))
    res = pl.pallas_call(
        body, name=name, grid=(t // tm,),
        in_specs=[col(3), col(4), col(5), vec, vec, _bs((ATTN_W, ATTN_W), lambda i: (0, 0))],
        out_specs=[_view_spec(tm, dil) for dil in DILATIONS for _ in range(3)],
        out_shape=[S(_view_shape(t, dil), BF16) for dil in DILATIONS for _ in range(3)],
        scratch_shapes=[pltpu.VMEM((N_LANE_TILES, tm, LANE_TILE), F32)],
        compiler_params=_cp("parallel"),
    )(z, z, z, q_gain, k_gain, seg)
    return {dil: tuple(res[3 * g:3 * g + 3]) for g, dil in enumerate(DILATIONS)}


def _lane_lo(shape):
    return lax.broadcasted_iota(jnp.int32, shape, 1) < HEAD_DIM


def _stack_heads(pair):
    lo = _lane_lo(pair.shape)
    zero = jnp.zeros_like(pair)
    return jnp.concatenate([jnp.where(lo, pair, zero), jnp.where(lo, zero, pair)], axis=0)


def _unstack_heads(st):
    rows = st.shape[0] // 2
    return jnp.where(_lane_lo((rows, st.shape[1])), st[:rows], st[rows:])


def attn_fwd(q, k, v, bias, dil, name, host=None):
    qv, kv, vv = q, k, v
    sub = q.shape[0]
    nb = sub // BAND

    def body(q_ref, kp_ref, kc_ref, vp_ref, vc_ref, b_ref, o_ref, l_ref):
        first = pl.program_id(1) == 0
        colk = lax.broadcasted_iota(jnp.int32, (2 * BAND, 2 * BAND), 1)
        for j in range(N_HEADS // 2):
            sl = slice(2 * HEAD_DIM * j, 2 * HEAD_DIM * (j + 1))
            kk = jnp.concatenate([kp_ref[:, sl], kc_ref[:, sl]], axis=0)
            vv_ = jnp.concatenate([vp_ref[:, sl], vc_ref[:, sl]], axis=0)
            s = _dot_nt(_stack_heads(q_ref[:, sl]), kk) * (HEAD_DIM ** -0.5) + b_ref[j]
            s = jnp.where(jnp.logical_and(first, colk < BAND), NEG, s)
            m = jnp.max(s, axis=-1, keepdims=True)
            p = jnp.exp(s - m)
            l = jnp.sum(p, axis=-1, keepdims=True)
            o_ref[:, sl] = _unstack_heads(_dot(p.astype(BF16), vv_) / l).astype(o_ref.dtype)
            l_ref[:, sl] = _unstack_heads(jnp.broadcast_to(m + jnp.log(l), (2 * BAND, 2 * HEAD_DIM)))

    cur = _bs((BAND, ATTN_W), lambda r, n: (n, r))
    prv = _bs((BAND, ATTN_W), lambda r, n: (jnp.maximum(n - 1, 0), r))
    return _call(
        host, body, name=name, grid=(dil, nb),
        in_specs=[cur, prv, cur, prv, cur, _bs((N_HEADS // 2, 2 * BAND, 2 * BAND), lambda r, n: (0, 0, 0))],
        out_specs=[cur, cur],
        out_shape=[S((sub, dil * ATTN_W), BF16), S((sub, dil * ATTN_W), F32)],
        compiler_params=_cp("parallel", "arbitrary"),
        args=(qv, kv, kv, vv, vv, bias))


def hyb_post(z, conv_w, os_, lses, name):
    t = z.shape[0]
    tm = _row_tile(t, 512)
    nd = len(DILATIONS)

    def body(gb_ref, gc_ref, cx_ref, gch_ref, cxh_ref, w_ref, *rest):
        o_refs, l_refs = rest[:nd], rest[nd:2 * nd]
        y_ref, ya_ref = rest[2 * nd:2 * nd + 2]
        lt_refs, scr = rest[2 * nd + 2:3 * nd + 2], rest[3 * nd + 2]
        i = pl.program_id(0)
        m = gc_ref[...] * cx_ref[...]
        mh = jnp.where(i == 0, 0.0, gch_ref[...] * cxh_ref[...])
        conv = w_ref[0:1, :] * _shift_down(m, mh, 2) + w_ref[1:2, :] * _shift_down(m, mh, 1) + w_ref[2:3, :] * m
        y_ref[0] = (gb_ref[...] * conv).astype(BF16)
        ls = [_read_view(scr, l_refs[g], dil) for g, dil in enumerate(DILATIONS)]
        mx = functools.reduce(jnp.maximum, ls)
        es = [jnp.exp(l - mx) for l in ls]
        den = functools.reduce(lambda a, b: a + b, es)
        num = es[0] * _read_view(scr, o_refs[0], DILATIONS[0])
        for g in range(1, nd):
            num = num + es[g] * _read_view(scr, o_refs[g], DILATIONS[g])
        ya = num / den
        y_ref[1] = ya.astype(BF16)
        ya_ref[...] = ya
        _cols_to(scr, mx + jnp.log(den))
        for g, dil in enumerate(DILATIONS):
            _write_view(scr, lt_refs[g], dil)

    hb = tm // 8
    col = lambda c: _bs((tm, CONV_W), lambda i: (i, c))
    halo = lambda c: _bs((8, CONV_W), lambda i: (jnp.maximum(i * hb - 1, 0), c))
    row = _bs((tm, ATTN_W), lambda i: (i, 0))
    views = [_view_spec(tm, dil) for dil in DILATIONS]
    res = pl.pallas_call(
        body, name=name, grid=(t // tm,),
        in_specs=[col(0), col(1), col(2), halo(1), halo(2), _bs((3, CONV_W), lambda i: (0, 0))] + views * 2,
        out_specs=[_bs((2, tm, ATTN_W), lambda i: (0, i, 0)), row] + views,
        out_shape=[S((2, t, ATTN_W), BF16), S((t, ATTN_W), F32)] + [S(_view_shape(t, dil), F32) for dil in DILATIONS],
        scratch_shapes=[pltpu.VMEM((N_LANE_TILES, tm, LANE_TILE), F32)],
        compiler_params=_cp("parallel"),
    )(z, z, z, z, z, conv_w, *os_, *lses)
    return res[0], res[1], dict(zip(DILATIONS, res[2:]))


def attn_delta(dy, ya, name):
    t = ya.shape[0]
    tm = _row_tile(t, 512)
    seg = _head_sum_matrix()
    nd = len(DILATIONS)

    def body(dy_ref, ya_ref, seg_ref, *rest):
        dl_refs, db_refs, scr = rest[:nd], rest[nd:2 * nd], rest[2 * nd]
        dya = dy_ref[...]
        _cols_to(scr, _seg_dot(dya * ya_ref[...], seg_ref[...]))
        for g, dil in enumerate(DILATIONS):
            _write_view(scr, dl_refs[g], dil)
        _cols_to(scr, dya)
        for g, dil in enumerate(DILATIONS):
            _write_view(scr, db_refs[g], dil)

    row = _bs((tm, ATTN_W), lambda i: (i, 0))
    views = [_view_spec(tm, dil) for dil in DILATIONS]
    res = pl.pallas_call(
        body, name=name, grid=(t // tm,),
        in_specs=[_bs((tm, ATTN_W), lambda i: (i, 1)), row, _bs((ATTN_W, ATTN_W), lambda i: (0, 0))],
        out_specs=views * 2,
        out_shape=[S(_view_shape(t, dil), F32) for dil in DILATIONS] + [S(_view_shape(t, dil), BF16) for dil in DILATIONS],
        scratch_shapes=[pltpu.VMEM((N_LANE_TILES, tm, LANE_TILE), F32)],
        compiler_params=_cp("parallel"),
    )(dy, ya, seg)
    return dict(zip(DILATIONS, res[:nd])), dict(zip(DILATIONS, res[nd:]))


def attn_bwd_dq(q, k, v, dya, lt, delta, bias, dil, name, host=None):
    qv, kv, vv, dv_, lv, ev = q, k, v, dya, lt, delta
    sub = q.shape[0]
    nb = sub // BAND

    def body(q_ref, kp_ref, kc_ref, vp_ref, vc_ref, do_ref, l_ref, e_ref, b_ref, dq_ref, db_ref):
        r, n = pl.program_id(0), pl.program_id(1)

        @pl.when(jnp.logical_and(r == 0, n == 0))
        def _():
            db_ref[...] = jnp.zeros_like(db_ref)
        first = n == 0
        colk = lax.broadcasted_iota(jnp.int32, (2 * BAND, 2 * BAND), 1)
        for j in range(N_HEADS // 2):
            c0 = 2 * HEAD_DIM * j
            sl = slice(c0, c0 + 2 * HEAD_DIM)
            kk = jnp.concatenate([kp_ref[:, sl], kc_ref[:, sl]], axis=0)
            vv_ = jnp.concatenate([vp_ref[:, sl], vc_ref[:, sl]], axis=0)
            lse = jnp.concatenate([l_ref[:, c0:c0 + 1], l_ref[:, c0 + HEAD_DIM:c0 + HEAD_DIM + 1]], axis=0)
            dlt = jnp.concatenate([e_ref[:, c0:c0 + 1], e_ref[:, c0 + HEAD_DIM:c0 + HEAD_DIM + 1]], axis=0)
            s = _dot_nt(_stack_heads(q_ref[:, sl]), kk) * (HEAD_DIM ** -0.5) + b_ref[j]
            s = jnp.where(jnp.logical_and(first, colk < BAND), NEG, s)
            p = jnp.exp(s - lse)
            ds = p * (_dot_nt(_stack_heads(do_ref[:, sl]), vv_) - dlt)
            db_ref[j] += ds
            dq_ref[:, sl] = (_unstack_heads(_dot(ds.astype(BF16), kk)) * (HEAD_DIM ** -0.5)).astype(dq_ref.dtype)

    cur = _bs((BAND, ATTN_W), lambda r, n: (n, r))
    prv = _bs((BAND, ATTN_W), lambda r, n: (jnp.maximum(n - 1, 0), r))
    tab = _bs((N_HEADS // 2, 2 * BAND, 2 * BAND), lambda r, n: (0, 0, 0))
    dq, db = _call(
        host, body, name=name, grid=(dil, nb),
        in_specs=[cur, prv, cur, prv, cur, cur, cur, cur, tab],
        out_specs=[cur, tab],
        out_shape=[S((sub, dil * ATTN_W), BF16), S((N_HEADS // 2, 2 * BAND, 2 * BAND), F32)],
        compiler_params=_cp("arbitrary", "arbitrary"),
        args=(qv, kv, kv, vv, vv, dv_, lv, ev, bias))
    return dq, db.reshape(N_HEADS, BAND, 2 * BAND)


def attn_bwd_dkv(q, k, v, dya, lt, delta, bias_k, dil, name):
    qv, kv, vv, dv_, lv, ev = q, k, v, dya, lt, delta
    sub = q.shape[0]
    nb = sub // BAND

    def body(k_ref, v_ref, qc_ref, qn_ref, dc_ref, dn_ref, lc_ref, ln_ref, ec_ref, en_ref, b_ref, dk_ref, dv_ref):
        last = pl.program_id(1) == nb - 1
        rowq = lax.broadcasted_iota(jnp.int32, (4 * BAND, BAND), 0)
        from_next = (rowq & BAND) != 0
        for j in range(N_HEADS // 2):
            c0 = 2 * HEAD_DIM * j
            sl = slice(c0, c0 + 2 * HEAD_DIM)
            kp, vp = k_ref[:, sl], v_ref[:, sl]
            q4 = _stack_heads(jnp.concatenate([qc_ref[:, sl], qn_ref[:, sl]], axis=0))
            do4 = _stack_heads(jnp.concatenate([dc_ref[:, sl], dn_ref[:, sl]], axis=0))
            lse = jnp.concatenate([ref[:, c:c + 1] for c in (c0, c0 + HEAD_DIM) for ref in (lc_ref, ln_ref)], axis=0)
            dlt = jnp.concatenate([ref[:, c:c + 1] for c in (c0, c0 + HEAD_DIM) for ref in (ec_ref, en_ref)], axis=0)
            s = _dot_nt(q4, kp) * (HEAD_DIM ** -0.5) + b_ref[j]
            s = jnp.where(jnp.logical_and(last, from_next), NEG, s)
            p = jnp.exp(s - lse)
            ds = p * (_dot_nt(do4, vp) - dlt)
            dv_ref[:, sl] = _dot_tn(p.astype(BF16), do4).astype(dv_ref.dtype)
            dk_ref[:, sl] = (_dot_tn(ds.astype(BF16), q4) * (HEAD_DIM ** -0.5)).astype(dk_ref.dtype)

    cur = _bs((BAND, ATTN_W), lambda r, n: (n, r))
    nxt = _bs((BAND, ATTN_W), lambda r, n: (jnp.minimum(n + 1, nb - 1), r))
    tab = _bs((N_HEADS // 2, 4 * BAND, BAND), lambda r, n: (0, 0, 0))
    dk, dv = pl.pallas_call(
        body, name=name, grid=(dil, nb),
        in_specs=[cur, cur, cur, nxt, cur, nxt, cur, nxt, cur, nxt, tab],
        out_specs=[cur, cur],
        out_shape=[S((sub, dil * ATTN_W), BF16)] * 2,
        compiler_params=_cp("parallel", "arbitrary"),
    )(kv, vv, qv, qv, dv_, dv_, lv, lv, ev, ev, bias_k)
    return dk, dv


def hyb_dz(z, dy, conv_w, q_gain, k_gain, dqs, dks, dvs, name):
    t = z.shape[0]
    tm = _row_tile(t, 512)
    nt = t // tm
    seg = _head_mean_matrix()

    def body(gb_ref, gc_ref, cx_ref, q_ref, k_ref, gch_ref, cxh_ref, gbn_ref, dyc_ref, dyn_ref, w_ref, qg_ref, kg_ref, seg_ref,
             dq1, dq2, dq3, dk1, dk2, dk3, dv1, dv2, dv3, dz_ref, dw_ref, dqg_ref, dkg_ref, scr):
        i = pl.program_id(0)

        def total(parts):
            acc = _read_view(scr, parts[0], DILATIONS[0])
            for g in range(1, len(DILATIONS)):
                acc = acc + _read_view(scr, parts[g], DILATIONS[g])
            return acc

        @pl.when(i == 0)
        def _():
            dw_ref[...] = jnp.zeros_like(dw_ref)
            dqg_ref[...] = jnp.zeros_like(dqg_ref)
            dkg_ref[...] = jnp.zeros_like(dkg_ref)
        gb, gc, cx, dyc = gb_ref[...], gc_ref[...], cx_ref[...], dyc_ref[...]
        m = gc * cx
        mh = jnp.where(i == 0, 0.0, gch_ref[...] * cxh_ref[...])
        m1, m2 = _shift_down(m, mh, 1), _shift_down(m, mh, 2)
        conv = w_ref[0:1, :] * m2 + w_ref[1:2, :] * m1 + w_ref[2:3, :] * m
        dconv = dyc * gb
        dcn = jnp.where(i == nt - 1, 0.0, dyn_ref[...] * gbn_ref[...])
        dm = w_ref[2:3, :] * dconv + w_ref[1:2, :] * _shift_up(dconv, dcn, 1) + w_ref[0:1, :] * _shift_up(dconv, dcn, 2)
        dz_ref[:, 0:CONV_W] = (dyc * conv).astype(BF16)
        dz_ref[:, CONV_W:2 * CONV_W] = (dm * cx).astype(BF16)
        dz_ref[:, 2 * CONV_W:3 * CONV_W] = (dm * gc).astype(BF16)
        dw_ref[0:1, :] += jnp.sum(dconv * m2, axis=0, keepdims=True)
        dw_ref[1:2, :] += jnp.sum(dconv * m1, axis=0, keepdims=True)
        dw_ref[2:3, :] += jnp.sum(dconv * m, axis=0, keepdims=True)
        base = 3 * CONV_W
        for idx, (x_ref, g_ref, parts, dgain_ref) in enumerate(((q_ref, qg_ref, (dq1, dq2, dq3), dqg_ref),
                                                                  (k_ref, kg_ref, (dk1, dk2, dk3), dkg_ref))):
            x = x_ref[...]
            dxh = total(parts)
            r = lax.rsqrt(_seg_dot(x * x, seg_ref[...]) + EPS)
            xhat = x * r
            tt = dxh * g_ref[...]
            dx = r * (tt - xhat * _seg_dot(tt * xhat, seg_ref[...]))
            dz_ref[:, base + idx * ATTN_W:base + (idx + 1) * ATTN_W] = dx.astype(BF16)
            dgain_ref[...] += jnp.sum(dxh * xhat, axis=0, keepdims=True)
        dz_ref[:, base + 2 * ATTN_W:base + 3 * ATTN_W] = total((dv1, dv2, dv3)).astype(BF16)

    hb = tm // 8
    col = lambda c: _bs((tm, CONV_W), lambda i: (i, c))
    prev = lambda c: _bs((8, CONV_W), lambda i: (jnp.maximum(i * hb - 1, 0), c))
    nxt = lambda c: _bs((8, CONV_W), lambda i: (jnp.minimum((i + 1) * hb, t // 8 - 1), c))
    row = _bs((tm, ATTN_W), lambda i: (i, 0))
    vec = _bs((1, ATTN_W), lambda i: (0, 0))
    return pl.pallas_call(
        body, name=name, grid=(nt,),
        in_specs=[col(0), col(1), col(2), col(3), col(4), prev(1), prev(2), nxt(0), col(0), nxt(0),
                  _bs((3, CONV_W), lambda i: (0, 0)), vec, vec, _bs((ATTN_W, ATTN_W), lambda i: (0, 0))]
                 + [_view_spec(tm, dil) for dil in DILATIONS] * 3,
        out_specs=[_bs((tm, 6 * CONV_W), lambda i: (i, 0)), _bs((3, CONV_W), lambda i: (0, 0)), vec, vec],
        out_shape=[S((t, 6 * CONV_W), BF16), S((3, CONV_W), F32), S((1, ATTN_W), F32), S((1, ATTN_W), F32)],
        scratch_shapes=[pltpu.VMEM((N_LANE_TILES, tm, LANE_TILE), F32)],
        compiler_params=_cp("arbitrary"),
    )(z, z, z, z, z, z, z, z, dy, dy, conv_w, q_gain, k_gain, seg, *dqs, *dks, *dvs)


def rel_bias_grad(dbs, name):
    (bq, vq), _ = _band_tables()
    onehot = np.zeros((len(DILATIONS), REL_BUCKETS, BAND * 2 * BAND), np.float32)
    for g in range(len(DILATIONS)):
        idx = bq[g].reshape(-1)
        ok = vq.reshape(-1)
        onehot[g, idx[ok], np.nonzero(ok)[0]] = 1.0
    onehot = jnp.asarray(onehot, BF16)
    flat = [d.reshape(N_HEADS, BAND * 2 * BAND) for d in dbs]

    def body(oh_ref, d1, d2, d3, o_ref):
        acc = jnp.zeros((REL_BUCKETS, N_HEADS), F32)
        for g, d in enumerate((d1, d2, d3)):
            x = d[...]
            hi = x.astype(BF16)
            lo = (x - hi.astype(F32)).astype(BF16)
            acc += _dot_nt(oh_ref[g], hi) + _dot_nt(oh_ref[g], lo)
        o_ref[...] = acc

    full = lambda shp: _bs(shp, lambda: tuple(0 for _ in shp))
    return pl.pallas_call(
        body, name=name,
        in_specs=[full(onehot.shape)] + [full(flat[0].shape)] * 3,
        out_specs=full((REL_BUCKETS, N_HEADS)),
        out_shape=S((REL_BUCKETS, N_HEADS), F32),
        compiler_params=pltpu.CompilerParams(vmem_limit_bytes=VMEM_LIMIT),
    )(onehot, *flat)


def _lru_gates(xb, wa_ref, wx_ref, ba, bx):
    xb16 = xb.astype(BF16)
    ga = jnp.concatenate([_dot(xb16[:, LRU_BLOCK * g:LRU_BLOCK * (g + 1)], wa_ref[g]) for g in range(LRU_BLOCKS)], axis=1) + ba
    gx = jnp.concatenate([_dot(xb16[:, LRU_BLOCK * g:LRU_BLOCK * (g + 1)], wx_ref[g]) for g in range(LRU_BLOCKS)], axis=1) + bx
    return ga, gx


def _lru_coeffs(ga, gx, lam):
    sga = _sigmoid(ga)
    sp = _softplus(-lam)
    log_a = -LRU_C * sga * sp
    a = jnp.exp(log_a)
    one_m_a2 = _neg_expm1(2.0 * log_a)
    return sga, sp, a, one_m_a2, jnp.sqrt(one_m_a2), _sigmoid(gx)


def rec_fwd(z, conv_w, conv_b, wa, wx, ba, bx, lam, name):
    t = z.shape[0]
    w = z.shape[1] // 2
    tm = _row_tile(t, 256)

    def body(xp_ref, xh_ref, yb_ref, cw_ref, cb_ref, wa_ref, wx_ref, ba_ref, bx_ref, lam_ref,
             xb_ref, ga_ref, gx_ref, hs_ref, out_ref, carry):
        i = pl.program_id(0)

        @pl.when(i == 0)
        def _():
            carry[...] = jnp.zeros_like(carry)
        xp = xp_ref[...]
        xh = jnp.where(i == 0, 0.0, xh_ref[...])
        xb = cb_ref[...] + cw_ref[3:4, :] * xp
        for j in range(3):
            xb = xb + cw_ref[j:j + 1, :] * _shift_down(xp, xh, 3 - j)
        ga, gx = _lru_gates(xb, wa_ref, wx_ref, ba_ref[...], bx_ref[...])
        _, _, a, _, sq, sgx = _lru_coeffs(ga, gx, lam_ref[...])
        aa, bb = a, sq * sgx * xb
        s = 1
        while s < tm:
            bb = aa * _roll_fill(bb, s, 0.0, False) + bb
            aa = aa * _roll_fill(aa, s, 1.0, False)
            s *= 2
        hs = aa * carry[0:1, :] + bb
        xb_ref[...] = xb
        ga_ref[...] = ga
        gx_ref[...] = gx
        hs_ref[...] = hs
        carry[0:1, :] = hs_ref[tm - 1:tm, :]
        gy, _ = _gelu_and_grad(yb_ref[...])
        out_ref[...] = (hs * gy).astype(BF16)

    hb = tm // 8
    row = _bs((tm, w), lambda i: (i, 0))
    vec = _bs((1, w), lambda i: (0, 0))
    wsp = _bs((LRU_BLOCKS, LRU_BLOCK, LRU_BLOCK), lambda i: (0, 0, 0))
    return pl.pallas_call(
        body, name=name, grid=(t // tm,),
        in_specs=[row, _bs((8, w), lambda i: (jnp.maximum(i * hb - 1, 0), 0)), _bs((tm, w), lambda i: (i, 1)),
                  _bs((4, w), lambda i: (0, 0)), vec, wsp, wsp, vec, vec, vec],
        out_specs=[row] * 5,
        out_shape=[S((t, w), F32)] * 4 + [S((t, w), BF16)],
        scratch_shapes=[pltpu.VMEM((8, w), F32)],
        compiler_params=_cp("arbitrary"),
    )(z, z, z, conv_w, conv_b, wa, wx, ba, bx, lam)


def rec_bwd(d_out, z, xb, ga, gx, hs, conv_w, wa, wx, lam, name, host=None):
    t = z.shape[0]
    w = z.shape[1] // 2
    tm = _row_tile(t, 256)
    nt = t // tm

    def body(do_ref, xp_ref, xph_ref, yb_ref, xb_ref, ga_ref, gx_ref, hs_ref, hsh_ref, cw_ref, wa_ref, wx_ref, lam_ref,
             dz_ref, dga_ref, dgx_ref, sm_ref, c_lam, c_a, c_dxb):
        i = pl.program_id(0)

        @pl.when(i == 0)
        def _():
            sm_ref[...] = jnp.zeros_like(sm_ref)
            c_lam[...] = jnp.zeros_like(c_lam)
            c_a[...] = jnp.zeros_like(c_a)
            c_dxb[...] = jnp.zeros_like(c_dxb)
        d_o, yb, xb, hs = do_ref[...], yb_ref[...], xb_ref[...], hs_ref[...]
        lam = lam_ref[...]
        gy, dgy = _gelu_and_grad(yb)
        dz_ref[:, w:2 * w] = (d_o * hs * dgy).astype(BF16)
        sga, sp, a, one_m_a2, sq, sgx = _lru_coeffs(ga_ref[...], gx_ref[...], lam)
        aa = _shift_up(a, c_a[...], 1)
        bb = d_o * gy
        s = 1
        while s < tm:
            bb = aa * _roll_fill(bb, s, 0.0, True) + bb
            aa = aa * _roll_fill(aa, s, 1.0, True)
            s *= 2
        lmb = aa * c_lam[0:1, :] + bb
        c_a[...] = a[0:8, :]
        c_lam[...] = lmb[0:8, :]
        hprev = _shift_down(hs, jnp.where(i == nt - 1, 0.0, hsh_ref[...]), 1)
        d_sq = lmb * sgx * xb
        d_sgx = lmb * sq * xb
        d_log_a = lmb * hprev * a - d_sq * (1.0 - one_m_a2) / sq
        dga = d_log_a * (-LRU_C * sp) * sga * (1.0 - sga)
        dgx = d_sgx * sgx * (1.0 - sgx)
        dga16, dgx16 = dga.astype(BF16), dgx.astype(BF16)
        dga_ref[...] = dga16
        dgx_ref[...] = dgx16
        dxb = lmb * sq * sgx + jnp.concatenate(
            [_dot_nt(dga16[:, LRU_BLOCK * g:LRU_BLOCK * (g + 1)], wa_ref[g]) + _dot_nt(dgx16[:, LRU_BLOCK * g:LRU_BLOCK * (g + 1)], wx_ref[g])
             for g in range(LRU_BLOCKS)], axis=1)
        nxt = c_dxb[...]
        dxp = cw_ref[3:4, :] * dxb
        for j in range(3):
            dxp = dxp + cw_ref[j:j + 1, :] * _shift_up(dxb, nxt, 3 - j)
        c_dxb[...] = dxb[0:8, :]
        dz_ref[:, 0:w] = dxp.astype(BF16)
        xp = xp_ref[...]
        xph = jnp.where(i == nt - 1, 0.0, xph_ref[...])
        sm_ref[0:1, :] += jnp.sum(dga, axis=0, keepdims=True)
        sm_ref[1:2, :] += jnp.sum(dgx, axis=0, keepdims=True)
        sm_ref[2:3, :] += jnp.sum(d_log_a * (-LRU_C * sga), axis=0, keepdims=True) * (-_sigmoid(-lam))
        sm_ref[3:4, :] += jnp.sum(dxb, axis=0, keepdims=True)
        for j in range(4):
            sm_ref[4 + j:5 + j, :] += jnp.sum(dxb * _shift_down(xp, xph, 3 - j), axis=0, keepdims=True)

    hb = tm // 8
    rev = lambda c: _bs((tm, w), lambda i: (nt - 1 - i, c))
    halo = lambda c: _bs((8, w), lambda i: (jnp.maximum((nt - 1 - i) * hb - 1, 0), c))
    vec = _bs((1, w), lambda i: (0, 0))
    wsp = _bs((LRU_BLOCKS, LRU_BLOCK, LRU_BLOCK), lambda i: (0, 0, 0))
    return _call(
        host, body, name=name, grid=(nt,),
        in_specs=[rev(0), rev(0), halo(0), rev(1), rev(0), rev(0), rev(0), rev(0), halo(0),
                  _bs((4, w), lambda i: (0, 0)), wsp, wsp, vec],
        out_specs=[_bs((tm, 2 * w), lambda i: (nt - 1 - i, 0)), rev(0), rev(0), _bs((8, w), lambda i: (0, 0))],
        out_shape=[S((t, 2 * w), BF16), S((t, w), BF16), S((t, w), BF16), S((8, w), F32)],
        scratch_shapes=[pltpu.VMEM((8, w), F32)] * 3,
        compiler_params=_cp("arbitrary"),
        args=(d_out, z, z, z, xb, ga, gx, hs, hs, conv_w, wa, wx, lam))


def ple_fwd(h, gain, wpg, layer, p, wpp, name, target=None):
    t, d = h.shape
    pd = p.shape[1]
    nk, _, rb, _ = wpg.shape
    cb = wpp.shape[3]
    tm = _row_tile(t, 512)
    row = _bs((tm, d), lambda i: (i, 0))
    in_specs = [row, _bs((1, d), lambda i: (0, 0)), _bs((nk, None, rb, d), lambda i: (0, layer, 0, 0)),
                _bs((tm, pd), lambda i: (i, 0)), _bs((nk, None, pd, cb), lambda i: (0, layer, 0, 0))]

    def forward(h_ref, g_ref, wg_ref, p_ref, wp_ref, hn_ref, gp_ref, pp_ref):
        x = h_ref[...]
        hn = (x * _rstd(x) * g_ref[...]).astype(BF16)
        gp = _dot(hn[:, 0:rb], wg_ref[0])
        for k in range(1, nk):
            gp = gp + _dot(hn[:, rb * k:rb * (k + 1)], wg_ref[k])
        pp = jnp.concatenate([_dot(p_ref[...].astype(BF16), wp_ref[k]) for k in range(nk)], axis=1)
        hn_ref[...] = hn
        gp_ref[...] = gp
        pp_ref[...] = pp
        return x + _sigmoid(gp) * pp

    if target is not None:
        def body_loss(h_ref, g_ref, wg_ref, p_ref, wp_ref, t_ref, l_ref, dy_ref, hn_ref, gp_ref, pp_ref):
            @pl.when(pl.program_id(0) == 0)
            def _():
                l_ref[...] = jnp.zeros_like(l_ref)
            err = forward(h_ref, g_ref, wg_ref, p_ref, wp_ref, hn_ref, gp_ref, pp_ref) - t_ref[...]
            dy_ref[...] = err * (1.0 / d)
            l_ref[...] += jnp.sum(jnp.sum(err * err, axis=1, keepdims=True), axis=0, keepdims=True) * (0.5 / d)

        return pl.pallas_call(
            body_loss, name=name, grid=(t // tm,),
            in_specs=in_specs + [row],
            out_specs=[_bs((1, 1), lambda i: (0, 0))] + [row] * 4,
            out_shape=[S((1, 1), F32), S((t, d), F32), S((t, d), BF16), S((t, d), F32), S((t, d), F32)],
            compiler_params=_cp("arbitrary"),
        )(h, gain, wpg, p, wpp, target)

    def body(h_ref, g_ref, wg_ref, p_ref, wp_ref, o_ref, hn_ref, gp_ref, pp_ref):
        o_ref[...] = forward(h_ref, g_ref, wg_ref, p_ref, wp_ref, hn_ref, gp_ref, pp_ref)

    return pl.pallas_call(
        body, name=name, grid=(t // tm,),
        in_specs=in_specs,
        out_specs=[row] * 4,
        out_shape=[S((t, d), F32), S((t, d), BF16), S((t, d), F32), S((t, d), F32)],
        compiler_params=_cp("parallel"),
    )(h, gain, wpg, p, wpp)


def ple_bwd(dh, h, gain, wpg, layer, gp, pp, name, host=None):
    t, d = h.shape
    nk, _, rb, _ = wpg.shape
    tm = _row_tile(t, 512)

    def body(dh_ref, h_ref, g_ref, wg_ref, gp_ref, pp_ref, o_ref, dgp_ref, dpp_ref, dg_ref):
        @pl.when(pl.program_id(0) == 0)
        def _():
            dg_ref[...] = jnp.zeros_like(dg_ref)
        d_h = dh_ref[...]
        gate = _sigmoid(gp_ref[...])
        dgp = (d_h * pp_ref[...] * gate * (1.0 - gate)).astype(BF16)
        dgp_ref[...] = dgp
        dpp_ref[...] = (d_h * gate).astype(BF16)
        dhn = jnp.concatenate([_dot_nt(dgp, wg_ref[k]) for k in range(nk)], axis=1)
        dx, dgain = _rmsnorm_bwd(h_ref[...], g_ref[...], dhn)
        o_ref[...] = d_h + dx
        dg_ref[...] += dgain

    row = _bs((tm, d), lambda i: (i, 0))
    vec = _bs((1, d), lambda i: (0, 0))
    return _call(
        host, body, name=name, grid=(t // tm,),
        in_specs=[row, row, vec, _bs((nk, None, rb, d), lambda i: (0, layer, 0, 0)), row, row],
        out_specs=[row, row, row, vec],
        out_shape=[S((t, d), F32), S((t, d), BF16), S((t, d), BF16), S((1, d), F32)],
        compiler_params=_cp("arbitrary"),
        args=(dh, h, gain, wpg, gp, pp))


def _vec(a, i):
    return a[i:i + 1]


def local_step(x, p, target, w, plan=None):
    t = x.shape[0]
    tm = _row_tile(t, 512)
    grads = {}
    if plan is not None:
        plan.grads = grads
    saved = []
    h = x
    bias_q, bias_k = band_bias(w["rel_bias"])
    qg = jnp.tile(w["hyb_q_gain"], (1, N_HEADS))
    kg = jnp.tile(w["hyb_k_gain"], (1, N_HEADS))

    def run(fn, *a, name):
        hst = plan.host(name) if plan is not None else None
        out = fn(*a, name, hst)
        if hst is not None:
            plan.done(hst)
        return out

    def lru_blocks(n):
        return jnp.transpose(w[n].reshape(N_SHARD, LRU_BLOCKS, 64, LRU_BLOCK), (1, 0, 2, 3)).reshape(LRU_BLOCKS, LRU_BLOCK, LRU_BLOCK)

    for i in range(2):
        s = {}
        s["h0"] = h
        s["hn1"], s["g1"], s["u1"], s["a1"] = run(ffn_up, h, _vec(w["ffn1_norm"], i), w[f"ffn1_w_gate/{i}"], w[f"ffn1_w_up/{i}"], 0, name=f"ffn1_up_{i}")
        h = run(ffn_down, s["a1"], w[f"ffn1_w_down/{i}"], 0, h, name=f"ffn1_down_{i}")
        s["h1"] = h
        if i == 0:
            w_hyb_in = w["hyb_w_in"].reshape(N_SHARD, D_MODEL, -1)
            w_hyb_out = w["hyb_w_out"].reshape(D_MODEL, D_MODEL)
            s["hnm"], s["z"] = run(norm_mm, h, _vec(w["mix_norm"], i), w_hyb_in, name="hyb_in")
            s["qkv"] = hyb_prep(s["z"], qg, kg, "hyb_prep")
            os_, lses = [], []
            for g, dil in enumerate(DILATIONS):
                o, l = run(attn_fwd, *s["qkv"][dil], bias_q[g], dil, name=f"attn_fwd_{dil}")
                os_.append(o)
                lses.append(l)
            s["y2"], s["ya"], s["lt"] = hyb_post(s["z"], w["hyb_conv_w"], os_, lses, "hyb_post")
            h = run(mm_acc, s["y2"], _bs((None, tm, ATTN_W), lambda r, k: (k, r, 0)),
                    w_hyb_out.reshape(2, ATTN_W, D_MODEL), _bs((None, ATTN_W, D_MODEL), lambda r, k: (k, 0, 0)),
                    h, 1.0, 2, t, D_MODEL, tm, name="hyb_out")
        else:
            w_rec_in = w["rec_w_in"].reshape(N_SHARD, D_MODEL, -1)
            s["hnm"], s["z"] = run(norm_mm, h, _vec(w["mix_norm"], i), w_rec_in, name="rec_in")
            w_rec_out = w["rec_w_out"].reshape(D_MODEL, D_MODEL)
            lru_wa, lru_wx = lru_blocks("lru_wa"), lru_blocks("lru_wx")
            s["xb"], s["ga"], s["gx"], s["hs"], s["ro"] = rec_fwd(
                s["z"], w["rec_conv_w"], w["rec_conv_b"], lru_wa, lru_wx, w["lru_ba"], w["lru_bx"], w["lru_lambda"], "rec_fwd")
            h = run(mm_acc, s["ro"], _bs((tm, D_MODEL), lambda r, k: (r, 0)), w_rec_out, _bs((D_MODEL, D_MODEL), lambda r, k: (0, 0)),
                    h, 1.0, 1, t, D_MODEL, tm, name="rec_out")
        s["h2"] = h
        s["hn2"], s["g2"], s["u2"], s["a2"] = run(ffn_up, h, _vec(w["ffn2_norm"], i), w[f"ffn2_w_gate/{i}"], w[f"ffn2_w_up/{i}"], 0, name=f"ffn2_up_{i}")
        h = run(ffn_down, s["a2"], w[f"ffn2_w_down/{i}"], 0, h, name=f"ffn2_down_{i}")
        s["h3"] = h
        if i == 0:
            h, s["hnp"], s["gp"], s["pp"] = ple_fwd(h, _vec(w["ple_norm"], i), w[f"ple_w_gate/{i}"], 0, p[i], w[f"ple_w_proj/{i}"], f"ple_fwd_{i}")
        else:
            loss, dh, s["hnp"], s["gp"], s["pp"] = ple_fwd(h, _vec(w["ple_norm"], i), w[f"ple_w_gate/{i}"], 0, p[i], w[f"ple_w_proj/{i}"],
                                                           f"ple_fwd_{i}", target)
        saved.append(s)

    norm_g = {n: [None, None] for n in ("ffn1_norm", "mix_norm", "ffn2_norm", "ple_norm")}
    for i in (1, 0):
        s = saved[i]
        dh_out = dh
        dh, dgp, dpp, norm_g["ple_norm"][i] = run(ple_bwd, dh_out, s["h3"], _vec(w["ple_norm"], i), w[f"ple_w_gate/{i}"], 0, s["gp"], s["pp"],
                                                  name=f"ple_bwd_{i}")
        grads["ple_w_gate"] = run(tn_mm, s["hnp"], lambda tk: _bs((tk, 256), lambda k, j: (j, k)), dgp, lambda tk: _bs((tk, D_MODEL), lambda k, j: (j, 0)),
                                  N_SHARD, t, 256, D_MODEL, S((N_SHARD, 2, 256, D_MODEL), BF16),
                                  _bs((None, None, 256, D_MODEL), lambda k, j, i=i: (k, i, 0, 0)), 1.0, grads.get("ple_w_gate"), name=f"ple_gw_gate_{i}")
        grads["ple_w_proj"] = run(tn_mm, p[i], lambda tk: _bs((tk, 256), lambda k, j: (j, 0)), dpp, lambda tk: _bs((tk, 256), lambda k, j: (j, k)),
                                  N_SHARD, t, 256, 256, S((N_SHARD, 2, 256, 256), BF16),
                                  _bs((None, None, 256, 256), lambda k, j, i=i: (k, i, 0, 0)), 1.0, grads.get("ple_w_proj"), name=f"ple_gw_proj_{i}")
        dh_out = dh
        dg, du = run(ffn_bwd_act, dh_out, w[f"ffn2_w_down/{i}"], 0, s["g2"], s["u2"], name=f"ffn2_bwd_act_{i}")
        ffn_wgrads("ffn2", s["hn2"], dh_out, s["a2"], dg, du, i, grads, run)
        dh, norm_g["ffn2_norm"][i] = run(ffn_bwd_in, dg, du, w[f"ffn2_w_gate/{i}"], w[f"ffn2_w_up/{i}"], 0, s["h2"], _vec(w["ffn2_norm"], i), dh_out,
                                         name=f"ffn2_bwd_in_{i}")
        dh_out = dh
        if i == 1:
            d_o = nt_mm(dh_out, w_rec_out, "rec_bwd_out")
            grads["rec_w_out"] = run(tn_mm, s["ro"], lambda tk: _bs((tk, 256), lambda k, j: (j, k)), dh_out, lambda tk: _bs((tk, D_MODEL), lambda k, j: (j, 0)),
                                     N_SHARD, t, 256, D_MODEL, S((N_SHARD, 256, D_MODEL), BF16), _bs((None, 256, D_MODEL), lambda k, j: (k, 0, 0)),
                                     1.0, None, name="rec_gw_out").reshape(N_SHARD, 1, 256, D_MODEL)
            dz, dga, dgx, small = run(rec_bwd, d_o, s["z"], s["xb"], s["ga"], s["gx"], s["hs"], w["rec_conv_w"], lru_wa, lru_wx, w["lru_lambda"],
                                      name="rec_bwd")
            blk = lambda tk: _bs((tk, LRU_BLOCK), lambda k, j: (j, k))
            for nm, dgt in (("lru_wa", dga), ("lru_wx", dgx)):
                gw = run(tn_mm, s["xb"], blk, dgt, blk, LRU_BLOCKS, t, LRU_BLOCK, LRU_BLOCK, S((LRU_BLOCKS, LRU_BLOCK, LRU_BLOCK), BF16),
                         _bs((None, LRU_BLOCK, LRU_BLOCK), lambda k, j: (k, 0, 0)), 1.0, None, name="rec_gw_" + nm)
                grads[nm] = jnp.transpose(gw.reshape(LRU_BLOCKS, N_SHARD, 64, LRU_BLOCK), (1, 0, 2, 3)).reshape(N_SHARD, 1, LRU_BLOCKS, 64, LRU_BLOCK)
            grads["lru_ba"], grads["lru_bx"], grads["lru_lambda"], grads["rec_conv_b"] = (small[r:r + 1] for r in range(4))
            grads["rec_conv_w"] = small[4:8]
            nb_, bw = N_SHARD, 512
            w_in, nm_in = w_rec_in, "rec_w_in"
        else:
            dy = nt_mm(dh_out, w_hyb_out, "hyb_bwd_out")
            grads["hyb_w_out"] = run(tn_mm, s["y2"], lambda tk: _bs((None, tk, 256), lambda k, j: (k // 2, j, k % 2)), dh_out,
                                     lambda tk: _bs((tk, D_MODEL), lambda k, j: (j, 0)),
                                     N_SHARD, t, 256, D_MODEL, S((N_SHARD, 256, D_MODEL), BF16), _bs((None, 256, D_MODEL), lambda k, j: (k, 0, 0)),
                                     1.0, None, name="hyb_gw_out").reshape(N_SHARD, 1, 256, D_MODEL)
            delta, dya = attn_delta(dy, s["ya"], "attn_delta")
            dqs, dks, dvs, dbs = [], [], [], []
            for g, dil in enumerate(DILATIONS):
                dq, db = run(attn_bwd_dq, *s["qkv"][dil], dya[dil], s["lt"][dil], delta[dil], bias_q[g], dil, name=f"attn_bwd_dq_{dil}")
                dk, dv = attn_bwd_dkv(*s["qkv"][dil], dya[dil], s["lt"][dil], delta[dil], bias_k[g], dil, f"attn_bwd_dkv_{dil}")
                dqs.append(dq); dks.append(dk); dvs.append(dv); dbs.append(db)
            grads["rel_bias"] = rel_bias_grad(dbs, "rel_bias_grad")
            dz, grads["hyb_conv_w"], dqg, dkg = hyb_dz(s["z"], dy, w["hyb_conv_w"], qg, kg, dqs, dks, dvs, "hyb_dz")
            grads["hyb_q_gain"] = jnp.sum(dqg.reshape(N_HEADS, HEAD_DIM), axis=0, keepdims=True)
            grads["hyb_k_gain"] = jnp.sum(dkg.reshape(N_HEADS, HEAD_DIM), axis=0, keepdims=True)
            nb_, bw = N_SHARD, 768
            w_in, nm_in = w_hyb_in, "hyb_w_in"
        grads[nm_in] = run(tn_mm, s["hnm"], lambda tk: _bs((tk, D_MODEL), lambda k, j: (j, 0)), dz, lambda tk, bw=bw: _bs((tk, bw), lambda k, j: (j, k)),
                           nb_, t, D_MODEL, bw, S((nb_, D_MODEL, bw), BF16), _bs((None, D_MODEL, bw), lambda k, j: (k, 0, 0)),
                           1.0, None, name=f"mix_gw_in_{i}").reshape(nb_, 1, D_MODEL, bw)
        dh, norm_g["mix_norm"][i] = run(
            nt_acc_normbwd, [(dz, _bs((_row_tile(t, BWD_IN_ROWS), bw), lambda r, k: (r, k)), w_in, _bs((None, D_MODEL, bw), lambda r, k: (k, 0, 0)))],
            nb_, s["h1"], _vec(w["mix_norm"], i), dh_out, name=f"mix_bwd_in_{i}")
        dh_out = dh
        dg, du = run(ffn_bwd_act, dh_out, w[f"ffn1_w_down/{i}"], 0, s["g1"], s["u1"], name=f"ffn1_bwd_act_{i}")
        ffn_wgrads("ffn1", s["hn1"], dh_out, s["a1"], dg, du, i, grads, run)
        dh, norm_g["ffn1_norm"][i] = run(ffn_bwd_in, dg, du, w[f"ffn1_w_gate/{i}"], w[f"ffn1_w_up/{i}"], 0, s["h0"], _vec(w["ffn1_norm"], i), dh_out,
                                         name=f"ffn1_bwd_in_{i}")
    for n, (g0, g1) in norm_g.items():
        grads[n] = jnp.concatenate([g0, g1], axis=0)
    return loss, dh, grads


def gather_weights(shards, name):
    n = len(shards)

    def body(*refs):
        ins, outs = refs[:n], refs[n:2 * n]
        send1, recv1, send2, recv2, lsem = refs[2 * n:]
        x, y, c, k, chips, kk = _place()
        sib = (x, y, 1 - c)

        def remote(src, dst, ssem, rsem, to):
            return pltpu.make_async_remote_copy(src_ref=src, dst_ref=dst, send_sem=ssem, recv_sem=rsem, device_id=to, device_id_type=MESH)

        local = [pltpu.make_async_copy(ins[a], outs[a].at[k], lsem.at[a]) for a in range(n)]
        for cp in local:
            cp.start()
        sends = []
        for a in range(n):
            for j, chip in enumerate(chips):
                cp = remote(ins[a].at[c], outs[a].at[k, c], send1.at[3 * a + j], recv1.at[3 * a + j], (*chip, c))
                cp.start()
                sends.append(cp)
        for a in range(n):
            for j, chip in enumerate(chips):
                remote(ins[a].at[c], outs[a].at[kk[j], c], send1.at[3 * a + j], recv1.at[3 * a + j], (*chip, c)).wait_recv()
                cp = remote(outs[a].at[kk[j], c], outs[a].at[kk[j], c], send2.at[3 * a + j], recv2.at[3 * a + j], sib)
                cp.start()
                sends.append(cp)
        for a in range(n):
            for j in range(3):
                remote(outs[a].at[kk[j], 1 - c], outs[a].at[kk[j], 1 - c], send2.at[3 * a + j], recv2.at[3 * a + j], sib).wait_recv()
        for cp in sends:
            cp.wait_send()
        for cp in local:
            cp.wait()

    return pl.pallas_call(
        body, name=name,
        in_specs=[_ANY] * n, out_specs=[_ANY] * n,
        out_shape=[S((N_SHARD,) + s.shape, s.dtype) for s in shards],
        scratch_shapes=[pltpu.SemaphoreType.DMA((3 * n,))] * 4 + [pltpu.SemaphoreType.DMA((n,))],
    )(*shards)


def exchange_cores(rs, name):
    n = len(rs)

    def body(*refs):
        outs = refs[n:2 * n]
        send, recv = refs[2 * n:]
        x, y, c = lax.axis_index("x"), lax.axis_index("y"), lax.axis_index("c")
        sends = []
        for a in range(n):
            for k in range(N_SHARD):
                slot = outs[a].at[2 * k + c]
                cp = _remote(slot, slot, send.at[N_SHARD * a + k], recv.at[N_SHARD * a + k], (x, y, 1 - c))
                cp.start()
                sends.append(cp)
        for a in range(n):
            for k in range(N_SHARD):
                slot = outs[a].at[2 * k + 1 - c]
                _remote(slot, slot, send.at[N_SHARD * a + k], recv.at[N_SHARD * a + k], (x, y, 1 - c)).wait_recv()
        for cp in sends:
            cp.wait_send()

    return pl.pallas_call(
        body, name=name,
        in_specs=[_ANY] * n, out_specs=[_ANY] * n,
        out_shape=[S(r.shape, r.dtype) for r in rs],
        input_output_aliases={a: a for a in range(n)},
        scratch_shapes=[pltpu.SemaphoreType.DMA((N_SHARD * n,))] * 2,
    )(*rs)


def allgather8(a, name):
    def body(a_ref, o_ref, send, recv, lsem):
        x, y, c = lax.axis_index("x"), lax.axis_index("y"), lax.axis_index("c")
        me = 4 * x + 2 * y + c
        local = pltpu.make_async_copy(a_ref, o_ref.at[me], lsem)
        local.start()
        cps = []
        for f in range(1, N_DEV):
            fx, fy, fc = (f >> 2) & 1, (f >> 1) & 1, f & 1
            peer = (1 - x if fx else x, 1 - y if fy else y, 1 - c if fc else c)
            cp = pltpu.make_async_remote_copy(src_ref=a_ref, dst_ref=o_ref.at[me], send_sem=send.at[f - 1], recv_sem=recv.at[f - 1],
                                              device_id=peer, device_id_type=MESH)
            cp.start()
            cps.append((cp, 4 * peer[0] + 2 * peer[1] + peer[2], f))
        for cp, pidx, f in cps:
            pltpu.make_async_remote_copy(src_ref=a_ref, dst_ref=o_ref.at[pidx], send_sem=send.at[f - 1], recv_sem=recv.at[f - 1],
                                         device_id=(x, y, c), device_id_type=MESH).wait_recv()
        for cp, _, _ in cps:
            cp.wait_send()
        local.wait()

    return pl.pallas_call(
        body, name=name, in_specs=[_ANY], out_specs=_ANY,
        out_shape=S((N_DEV,) + a.shape, a.dtype),
        scratch_shapes=[pltpu.SemaphoreType.DMA((N_DEV - 1,)), pltpu.SemaphoreType.DMA((N_DEV - 1,)), pltpu.SemaphoreType.DMA],
    )(a)


def sum8(a, name):
    _, r, c = a.shape

    def body(a_ref, o_ref):
        acc = a_ref[0]
        for j in range(1, N_DEV):
            acc = acc + a_ref[j]
        o_ref[...] = acc

    return pl.pallas_call(
        body, name=name, in_specs=[_bs((N_DEV, r, c), lambda: (0, 0, 0))], out_specs=_bs((r, c), lambda: (0, 0)),
        out_shape=S((r, c), F32),
    )(a)


def adamw(w, m, v, g, name):
    nl, r, c = w.shape
    tr = _row_tile(r, 256)
    summed = g.ndim == 4

    def body(w_ref, m_ref, v_ref, g_ref, go_ref, d_ref, mo_ref, vo_ref):
        if summed:
            gr = g_ref[0].astype(F32)
            for j in range(1, N_DEV):
                gr = gr + g_ref[j].astype(F32)
        else:
            gr = g_ref[...]
        m_new = ADAM_B1 * m_ref[...] + (1.0 - ADAM_B1) * gr
        v_new = ADAM_B2 * v_ref[...] + (1.0 - ADAM_B2) * (gr * gr)
        m_hat = m_new / (1.0 - ADAM_B1 ** ADAM_STEP)
        v_hat = v_new / (1.0 - ADAM_B2 ** ADAM_STEP)
        go_ref[...] = gr
        d_ref[...] = -ADAM_LR * (m_hat / (jnp.sqrt(v_hat) + ADAM_EPS) + ADAM_WD * w_ref[...])
        mo_ref[...] = m_new
        vo_ref[...] = v_new

    row = _bs((None, tr, c), lambda l, i: (l, i, 0))
    gspec = _bs((N_DEV, None, tr, c), lambda l, i: (0, l, i, 0)) if summed else row
    return pl.pallas_call(
        body, name=name, grid=(nl, r // tr),
        in_specs=[row, row, row, gspec], out_specs=[row] * 4, out_shape=[S((nl, r, c), F32)] * 4,
        compiler_params=_cp("parallel", "parallel"),
    )(w, m, v, g)


WEIGHTS = ["rel_bias", "ffn1_norm", "ffn1_w_gate", "ffn1_w_up", "ffn1_w_down", "mix_norm", "hyb_w_in", "hyb_conv_w", "hyb_q_gain",
           "hyb_k_gain", "hyb_w_out", "rec_w_in", "rec_conv_w", "rec_conv_b", "lru_wa", "lru_ba", "lru_wx", "lru_bx", "lru_lambda",
           "rec_w_out", "ffn2_norm", "ffn2_w_gate", "ffn2_w_up", "ffn2_w_down", "ple_norm", "ple_w_gate", "ple_w_proj"]
BIG = ["ffn1_w_gate", "ffn1_w_up", "ffn1_w_down", "hyb_w_in", "hyb_w_out", "rec_w_in", "lru_wa", "lru_wx", "rec_w_out",
       "ffn2_w_gate", "ffn2_w_up", "ffn2_w_down", "ple_w_gate", "ple_w_proj"]
SMALL_SHARDED = ["hyb_conv_w", "rec_conv_w", "rec_conv_b", "lru_ba", "lru_bx", "lru_lambda"]
SMALL = ["rel_bias", "ffn1_norm", "mix_norm", "ffn2_norm", "ple_norm", "hyb_q_gain", "hyb_k_gain"] + SMALL_SHARDED
PACK_W = 1024
PER_LAYER = ["ffn1_w_gate", "ffn1_w_up", "ffn1_w_down", "ffn2_w_gate", "ffn2_w_up", "ffn2_w_down", "ple_w_gate", "ple_w_proj"]
FIRST = ["ffn1_w_gate/0", "ffn1_w_up/0"]
LAST = ["ffn1_w_down"]
GATHER_PLAN = {
    "ffn1_up_0": ["ffn1_w_down/0", "hyb_w_in"],
    "ffn1_down_0": ["hyb_w_out", "ple_w_gate/0", "ple_w_proj/0"],
    "hyb_in": ["ffn2_w_gate/0"],
    "attn_fwd_1": ["ffn2_w_up/0"],
    "attn_fwd_4": ["ffn2_w_down/0"],
    "ffn2_up_0": ["ffn1_w_gate/1", "ffn1_w_up/1"],
    "ffn2_down_0": ["ffn1_w_down/1"],
    "ffn1_up_1": ["rec_w_in", "lru_wa", "lru_wx", "rec_w_out", "ffn2_w_gate/1"],
    "ffn1_down_1": ["ffn2_w_up/1"],
    "rec_in": ["ffn2_w_down/1", "ple_w_gate/1", "ple_w_proj/1"],
}
SCATTER_PLAN = {
    "ple_gw_proj_1": [("ple_w_gate", 1)],
    "ffn2_bwd_act_1": [("ple_w_proj", 1)],
    "ffn2_bwd_in_1": [("ffn2_w_gate", 1), ("ffn2_w_up", 1)],
    "rec_bwd": [("ffn2_w_down", 1)],
    "mix_bwd_in_1": [("rec_w_in", 0), ("rec_w_out", 0), ("lru_wa", 0), ("lru_wx", 0)],
    "ffn1_bwd_in_1": [("ffn1_w_gate", 1), ("ffn1_w_up", 1)],
    "ple_bwd_0": [("ffn1_w_down", 1)],
    "ple_gw_proj_0": [("ple_w_gate", 0)],
    "ffn2_bwd_act_0": [("ple_w_proj", 0)],
    "ffn2_bwd_in_0": [("ffn2_w_gate", 0), ("ffn2_w_up", 0)],
    "attn_bwd_dq_1": [("ffn2_w_down", 0)],
    "mix_bwd_in_0": [("hyb_w_in", 0), ("hyb_w_out", 0)],
    "ffn1_gw_0_up": [("ffn1_w_gate", 0)],
    "ffn1_gw_0_down": [("ffn1_w_up", 0)],
    "ffn1_bwd_in_0": [("ffn1_w_down", 0)],
}
FORWARD_PLAN = {
    "ffn1_bwd_in_1": ["rec_w_in", "rec_w_out", "lru_wa", "lru_wx"],
    "ffn2_bwd_in_0": ["ple_w_gate", "ple_w_proj"],
    "mix_bwd_in_0": ["ffn2_w_gate", "ffn2_w_up", "ffn2_w_down"],
    "ffn1_gw_0_up": ["hyb_w_in", "hyb_w_out"],
    "ffn1_bwd_in_0": ["ffn1_w_gate", "ffn1_w_up"],
}


class Plan:
    def __init__(self, shards, w):
        self.shards, self.w, self.grads, self.landed = shards, w, None, {}

    def host(self, kname):
        if kname in GATHER_PLAN:
            h = Host("gather", [self.shards[n] for n in GATHER_PLAN[kname]])
            h.names = GATHER_PLAN[kname]
            return h
        if kname in SCATTER_PLAN or kname in FORWARD_PLAN:
            items = SCATTER_PLAN.get(kname, [])
            fwd = FORWARD_PLAN.get(kname, [])
            h = Host("scatter", [(self.grads[n], lay, self.landed.get(n)) for n, lay in items], [self.landed[n] for n in fwd])
            h.names = [n for n, _ in items] + fwd
            return h
        return None

    def done(self, h):
        for n, o in zip(h.names, h.outs):
            if h.kind == "gather":
                self.w[n] = o
            else:
                self.landed[n] = o


def _halves(a):
    if a.shape[0] == 2:
        return a
    return a.reshape((2, a.shape[1] // 2) + a.shape[2:])


def _pack_rows(arrs, width):
    rows, offs, r0 = [], [], 0
    for a in arrs:
        if a.shape[1] > width:
            a = a.reshape(-1, width)
        rows.append(jnp.pad(a, ((0, 0), (0, width - a.shape[1]))))
        offs.append(r0)
        r0 += a.shape[0]
    pad = (-r0) % 8
    if pad:
        rows.append(jnp.zeros((pad, width), F32))
    return jnp.concatenate(rows, axis=0), offs


def kernel(x, p, rel_bias, ffn1_norm, ffn1_w_gate, ffn1_w_up, ffn1_w_down, mix_norm, hyb_w_in, hyb_conv_w, hyb_q_gain, hyb_k_gain, hyb_w_out, rec_w_in, rec_conv_w, rec_conv_b, lru_wa, lru_ba, lru_wx, lru_bx, lru_lambda, rec_w_out, ffn2_norm, ffn2_w_gate, ffn2_w_up, ffn2_w_down, ple_norm, ple_w_gate, ple_w_proj, loss_target, m_rel_bias, m_ffn1_norm, m_ffn1_w_gate, m_ffn1_w_up, m_ffn1_w_down, m_mix_norm, m_hyb_w_in, m_hyb_conv_w, m_hyb_q_gain, m_hyb_k_gain, m_hyb_w_out, m_rec_w_in, m_rec_conv_w, m_rec_conv_b, m_lru_wa, m_lru_ba, m_lru_wx, m_lru_bx, m_lru_lambda, m_rec_w_out, m_ffn2_norm, m_ffn2_w_gate, m_ffn2_w_up, m_ffn2_w_down, m_ple_norm, m_ple_w_gate, m_ple_w_proj, v_rel_bias, v_ffn1_norm, v_ffn1_w_gate, v_ffn1_w_up, v_ffn1_w_down, v_mix_norm, v_hyb_w_in, v_hyb_conv_w, v_hyb_q_gain, v_hyb_k_gain, v_hyb_w_out, v_rec_w_in, v_rec_conv_w, v_rec_conv_b, v_lru_wa, v_lru_ba, v_lru_wx, v_lru_bx, v_lru_lambda, v_rec_w_out, v_ffn2_norm, v_ffn2_w_gate, v_ffn2_w_up, v_ffn2_w_down, v_ple_norm, v_ple_w_gate, v_ple_w_proj):
    given = dict(locals())
    wts = {n: given[n] for n in WEIGHTS}
    k_chip = 2 * lax.axis_index("x") + lax.axis_index("y")

    shards = {}
    for n in BIG:
        b16 = wts[n].astype(BF16)
        if n in PER_LAYER:
            shards[n + "/0"], shards[n + "/1"] = b16[0:1], b16[1:2]
        else:
            shards[n] = b16
    first = gather_weights([_halves(shards[n]) for n in FIRST], "gather_first")
    w = {n: g.reshape((N_SHARD,) + shards[n].shape) for n, g in zip(FIRST, first)}
    plan = Plan(shards, w)
    sm2d = {n: wts[n].reshape(-1, wts[n].shape[-1]) for n in SMALL_SHARDED}
    slab, offs = _pack_rows([sm2d[n] for n in SMALL_SHARDED], 256)
    slabs = allgather8(slab, "gather_small")[0::2]
    for n, o in zip(SMALL_SHARDED, offs):
        r, cw = sm2d[n].shape
        w[n] = jnp.concatenate([slabs[kc, o:o + r, :cw] for kc in range(N_SHARD)], axis=1)
    for n in SMALL:
        if n not in SMALL_SHARDED:
            w[n] = wts[n]

    loss, dx, grads = local_step(x[0], p[:, 0], loss_target[0], w, plan)
    loss = lax.psum(loss[0, 0], ("x", "y", "c"))

    for n, r8 in zip(LAST, exchange_cores([plan.landed[n] for n in LAST], "exchange_cores")):
        plan.landed[n] = r8
    out = {}
    for n in BIG:
        r8 = plan.landed[n]
        shp = wts[n].shape
        shp3 = shp if len(shp) == 3 else (shp[0], -1, shp[-1])
        three = lambda a: a.reshape(shp3)
        res = adamw(three(wts[n]), three(given["m_" + n]), three(given["v_" + n]), r8.reshape((N_DEV,) + three(wts[n]).shape), "adamw_" + n)
        out[n] = [a.reshape(shp) for a in res]
    g2d = [grads[n].reshape(-1, grads[n].shape[-1]) if n != "rel_bias" else grads[n].reshape(1, -1) for n in SMALL]
    gslab, goffs = _pack_rows(g2d, PACK_W)
    gsum = sum8(allgather8(gslab, "gather_small_grads"), "sum_small_grads")
    for n, o, g in zip(SMALL, goffs, g2d):
        shp = wts[n].shape
        r, cw = g.shape
        gs = gsum[o:o + r, :cw]
        if n in SMALL_SHARDED:
            sw = shp[-1]
            gs = lax.dynamic_slice_in_dim(gs, k_chip * sw, sw, axis=1)
        three = lambda a: a.reshape((1, -1, shp[-1]))
        res = adamw(three(wts[n]), three(given["m_" + n]), three(given["v_" + n]), three(gs), "adamw_" + n)
        out[n] = [a.reshape(shp) for a in res]
    return (loss, dx[None], *[out[n][0] for n in WEIGHTS], *[out[n][1] for n in WEIGHTS],
            *[out[n][2] for n in WEIGHTS], *[out[n][3] for n in WEIGHTS])
```

```python
import functools
import math

import numpy as np
import jax
import jax.numpy as jnp
from jax import lax
from jax.experimental import pallas as pl
from jax.experimental.pallas import tpu as pltpu

F32, BF16 = jnp.float32, jnp.bfloat16
S = jax.ShapeDtypeStruct
MESH = pl.DeviceIdType.MESH

D_MODEL = 1024
N_SHARD = 4
N_DEV = 8
HEAD_DIM = 64
N_HEADS = 8
ATTN_W = N_HEADS * HEAD_DIM
CONV_W = 512
BAND = 128
DILATIONS = (1, 4, 16)
REL_BUCKETS = 32
REL_MAX_DIST = 2048
LRU_BLOCKS = 4
LRU_BLOCK = 256
LRU_C = 8.0
EPS = 1e-6
NEG = -1e30
VMEM_LIMIT = 56 * 1024 * 1024
FFN_ROWS = 1024
FFN_SHARDS_ROWS = 512
TN_ROWS = 2048
BWD_IN_ROWS = 512

ADAM_LR, ADAM_B1, ADAM_B2, ADAM_EPS, ADAM_WD, ADAM_STEP = 0.001, 0.9, 0.999, 1e-08, 0.01, 10


def _cp(*sem):
    return pltpu.CompilerParams(dimension_semantics=sem, vmem_limit_bytes=VMEM_LIMIT)


def _bs(shape, imap):
    return pl.BlockSpec(shape, imap)


def _row_tile(t, want):
    for cand in range(min(want, t) // 8 * 8, 0, -8):
        if t % cand == 0:
            return cand
    return t


_ANY = pl.BlockSpec(memory_space=pl.ANY)


def _place():
    x, y, c = lax.axis_index("x"), lax.axis_index("y"), lax.axis_index("c")
    chips = [(1 - x, y), (x, 1 - y), (1 - x, 1 - y)]
    return x, y, c, 2 * x + y, chips, [2 * cx + cy for cx, cy in chips]


def _remote(src, dst, ssem, rsem, to):
    return pltpu.make_async_remote_copy(src_ref=src, dst_ref=dst, send_sem=ssem, recv_sem=rsem, device_id=to, device_id_type=MESH)


class Host:
    def __init__(self, kind, items, forwards=()):
        self.kind, self.items, self.forwards, self.outs = kind, items, list(forwards), None

    def n_sems(self):
        return 3 * len(self.items) + N_SHARD * len(self.forwards), len(self.items)

    def operands(self):
        if self.kind == "gather":
            return list(self.items), [S((N_SHARD,) + s.shape, s.dtype) for s in self.items], {}
        xin, shapes, alias = [], [], {}
        for a, (g, _, r_prev) in enumerate(self.items):
            xin.append(g)
            if r_prev is not None:
                alias[len(xin)] = a
                xin.append(r_prev)
            shapes.append(S((N_DEV,) + g.shape[1:], g.dtype))
        for f, r in enumerate(self.forwards):
            alias[len(xin)] = len(self.items) + f
            xin.append(r)
            shapes.append(S(r.shape, r.dtype))
        return xin, shapes, alias

    def copies(self, xi, xo, send, recv, lsem):
        x, y, c, k, chips, kk = _place()
        starts, waits = [], []
        pos = 0
        for f in range(len(self.forwards)):
            arr = xo[len(self.items) + f]
            for kq in range(N_SHARD):
                sem = 3 * len(self.items) + N_SHARD * f + kq
                cp = _remote(arr.at[2 * kq + c], arr.at[2 * kq + c], send.at[sem], recv.at[sem], (x, y, 1 - c))
                starts.append((cp, "start"))
                waits.append((cp, "wait_send"))
                other = arr.at[2 * kq + 1 - c]
                waits.append((_remote(other, other, send.at[sem], recv.at[sem], (x, y, 1 - c)), "wait_recv"))
        for a, item in enumerate(self.items):
            if self.kind == "gather":
                src_of = lambda chip_idx, s=xi[a]: s
                dst_of = lambda chip_idx, o=xo[a]: o.at[chip_idx]
                mine, theirs = k, kk
            else:
                g_ref = xi[pos]
                pos += 1 if item[2] is None else 2
                lay = item[1]
                src_of = lambda chip_idx, g=g_ref, lay=lay: g.at[chip_idx, lay]
                dst_of = lambda slot, o=xo[a], lay=lay: o.at[slot, lay]
                mine, theirs = 2 * k + c, [2 * kj + c for kj in kk]
            own_src = src_of(k)
            local = pltpu.make_async_copy(own_src, dst_of(mine), lsem.at[a])
            starts.append((local, "start"))
            waits.append((local, "wait"))
            for j, chip in enumerate(chips):
                src = own_src if self.kind == "gather" else src_of(kk[j])
                cp = _remote(src, dst_of(mine), send.at[3 * a + j], recv.at[3 * a + j], (*chip, c))
                starts.append((cp, "start"))
                waits.append((cp, "wait_send"))
                waits.append((_remote(own_src, dst_of(theirs[j]), send.at[3 * a + j], recv.at[3 * a + j], (*chip, c)), "wait_recv"))
        return starts, waits


def _call(host, body, *, name, grid, in_specs, out_specs, out_shape, scratch_shapes=(), compiler_params=None, args, aliases=None):
    aliases = dict(aliases or {})
    if host is None:
        return pl.pallas_call(body, name=name, grid=grid, in_specs=in_specs, out_specs=out_specs, out_shape=out_shape,
                              scratch_shapes=list(scratch_shapes), input_output_aliases=aliases, compiler_params=compiler_params)(*args)
    single = not isinstance(out_shape, (list, tuple))
    out_specs_l = [out_specs] if single else list(out_specs)
    out_shape_l = [out_shape] if single else list(out_shape)
    n_in, n_out, n_scr = len(in_specs), len(out_shape_l), len(scratch_shapes)
    xin, xshapes, xalias = host.operands()
    n_items = len(xshapes)
    n_rsem, n_lsem = host.n_sems()
    for i_in, i_out in xalias.items():
        aliases[n_in + i_in] = n_out + i_out
    nd = len(grid)

    def hosted(*refs):
        ins, xi = refs[:n_in], refs[n_in:n_in + len(xin)]
        o0 = n_in + len(xin)
        outs, xo = refs[o0:o0 + n_out], refs[o0 + n_out:o0 + n_out + n_items]
        s0 = o0 + n_out + n_items
        scr = refs[s0:s0 + n_scr]
        send, recv, lsem = refs[s0 + n_scr:]
        first = functools.reduce(jnp.logical_and, [pl.program_id(d) == 0 for d in range(nd)])
        last = functools.reduce(jnp.logical_and, [pl.program_id(d) == grid[d] - 1 for d in range(nd)])
        starts, waits = host.copies(xi, xo, send, recv, lsem)

        @pl.when(first)
        def _():
            for cp, how in starts:
                getattr(cp, how)()
        body(*ins, *outs, *scr)

        @pl.when(last)
        def _():
            for cp, how in waits:
                getattr(cp, how)()

    res = pl.pallas_call(
        hosted, name=name, grid=grid,
        in_specs=list(in_specs) + [_ANY] * len(xin),
        out_specs=out_specs_l + [_ANY] * n_items,
        out_shape=out_shape_l + xshapes,
        scratch_shapes=list(scratch_shapes) + [pltpu.SemaphoreType.DMA((n_rsem,)), pltpu.SemaphoreType.DMA((n_rsem,)),
                                               pltpu.SemaphoreType.DMA((max(n_lsem, 1),))],
        input_output_aliases=aliases,
        compiler_params=pltpu.CompilerParams(dimension_semantics=("arbitrary",) * nd, vmem_limit_bytes=VMEM_LIMIT),
    )(*args, *xin)
    host.outs = list(res[n_out:])
    return res[0] if single else list(res[:n_out])


def _rstd(x):
    return lax.rsqrt(jnp.mean(x * x, axis=-1, keepdims=True) + EPS)


def _sigmoid(x):
    return 1.0 / (1.0 + jnp.exp(-x))


def _dot(a, b):
    return jnp.dot(a, b, preferred_element_type=F32)


def _dot_nt(a, b):
    return lax.dot_general(a, b, (((1,), (1,)), ((), ())), preferred_element_type=F32)


def _dot_tn(a, b):
    return lax.dot_general(a, b, (((0,), (0,)), ((), ())), preferred_element_type=F32)


def _seg_dot(x, seg_bf16):
    hi = x.astype(BF16)
    lo = (x - hi.astype(F32)).astype(BF16)
    return _dot(hi, seg_bf16) + _dot(lo, seg_bf16)


def _shift_down(x, prev8, s):
    if s == 0:
        return x
    tm = x.shape[0]
    row = lax.broadcasted_iota(jnp.int32, x.shape, 0)
    main = jnp.where(row >= s, pltpu.roll(x, s, axis=0), 0.0)
    row8 = lax.broadcasted_iota(jnp.int32, prev8.shape, 0)
    head = jnp.where(row8 < s, pltpu.roll(prev8, s, axis=0), 0.0)
    if tm == 8:
        return main + head
    return main + jnp.concatenate([head, jnp.zeros((tm - 8, x.shape[1]), x.dtype)], axis=0)


def _shift_up(x, next8, s):
    if s == 0:
        return x
    tm = x.shape[0]
    row = lax.broadcasted_iota(jnp.int32, x.shape, 0)
    main = jnp.where(row < tm - s, pltpu.roll(x, tm - s, axis=0), 0.0)
    row8 = lax.broadcasted_iota(jnp.int32, next8.shape, 0)
    tail = jnp.where(row8 >= 8 - s, pltpu.roll(next8, 8 - s, axis=0), 0.0)
    if tm == 8:
        return main + tail
    return main + jnp.concatenate([jnp.zeros((tm - 8, x.shape[1]), x.dtype), tail], axis=0)


def _roll_fill(x, s, fill, up):
    tm = x.shape[0]
    row = lax.broadcasted_iota(jnp.int32, x.shape, 0)
    if up:
        return jnp.where(row < tm - s, pltpu.roll(x, tm - s, axis=0), fill)
    return jnp.where(row >= s, pltpu.roll(x, s, axis=0), fill)


def _log1p(y):
    u = 1.0 + y
    return jnp.where(u == 1.0, y, jnp.log(u) * (y / jnp.where(u == 1.0, 1.0, u - 1.0)))


def _softplus(x):
    return jnp.maximum(x, 0.0) + _log1p(jnp.exp(-jnp.abs(x)))


def _neg_expm1(y):
    series = -y * (1.0 + y * (0.5 + y * (1.0 / 6.0 + y * (1.0 / 24.0 + y * (1.0 / 120.0)))))
    return jnp.where(jnp.abs(y) < 0.03, series, 1.0 - jnp.exp(y))


_GELU_C = math.sqrt(2.0 / math.pi)


def _gelu_and_grad(x):
    inner = _GELU_C * (x + 0.044715 * x * x * x)
    t = jnp.tanh(inner)
    g = 0.5 * x * (1.0 + t)
    dg = 0.5 * (1.0 + t) + 0.5 * x * (1.0 - t * t) * _GELU_C * (1.0 + 3.0 * 0.044715 * x * x)
    return g, dg


def _rmsnorm_bwd(x, gain, dy):
    r = _rstd(x)
    xhat = x * r
    dxhat = dy * gain
    dx = r * (dxhat - xhat * jnp.mean(dxhat * xhat, axis=-1, keepdims=True))
    return dx, jnp.sum(dy * xhat, axis=0, keepdims=True)


def ffn_up(h, gain, wg, wu, layer, name, host=None):
    t, d = h.shape
    nk, _, _, f = wg.shape
    tm = _row_tile(t, FFN_SHARDS_ROWS)

    def body(h_ref, g_ref, wg_ref, wu_ref, hn_ref, gg_ref, uu_ref, aa_ref):
        x = h_ref[...]
        hn = (x * _rstd(x) * g_ref[...]).astype(BF16)
        hn_ref[...] = hn
        for k in range(nk):
            g = _dot(hn, wg_ref[k])
            u = _dot(hn, wu_ref[k])
            s = _sigmoid(g)
            silu = g * s
            gg_ref[k] = (u * (s * (1.0 + g * (1.0 - s)))).astype(BF16)
            uu_ref[k] = silu.astype(BF16)
            aa_ref[k] = (silu * u).astype(BF16)

    wspec = _bs((nk, None, d, f), lambda i: (0, layer, 0, 0))
    aspec = _bs((nk, tm, f), lambda i: (0, i, 0))
    return _call(
        host, body, name=name, grid=(t // tm,),
        in_specs=[_bs((tm, d), lambda i: (i, 0)), _bs((1, d), lambda i: (0, 0)), wspec, wspec],
        out_specs=[_bs((tm, d), lambda i: (i, 0)), aspec, aspec, aspec],
        out_shape=[S((t, d), BF16), S((nk, t, f), BF16), S((nk, t, f), BF16), S((nk, t, f), BF16)],
        compiler_params=_cp("parallel"),
        args=(h, gain, wg, wu))


def mm_acc(a, a_spec, b, b_spec, res, scale, nk, t, n, tm, name, host=None):
    def body(a_ref, b_ref, r_ref, o_ref, acc):
        k = pl.program_id(1)

        @pl.when(k == 0)
        def _():
            acc[...] = jnp.zeros_like(acc)
        acc[...] += _dot(a_ref[...].astype(BF16), b_ref[...])

        @pl.when(k == nk - 1)
        def _():
            o_ref[...] = r_ref[...] + scale * acc[...]

    return _call(
        host, body, name=name, grid=(t // tm, nk),
        in_specs=[a_spec, b_spec, _bs((tm, n), lambda i, k: (i, 0))],
        out_specs=_bs((tm, n), lambda i, k: (i, 0)),
        out_shape=S((t, n), F32),
        scratch_shapes=[pltpu.VMEM((tm, n), F32)],
        compiler_params=_cp("parallel", "arbitrary"),
        args=(a, b, res))


def ffn_down(a, wd, layer, h, name, host=None, scale=0.5):
    nk, t, f = a.shape
    d = h.shape[1]
    tm = _row_tile(t, FFN_ROWS)

    def body(a_ref, w_ref, r_ref, o_ref):
        acc = _dot(a_ref[0], w_ref[0])
        for k in range(1, nk):
            acc = acc + _dot(a_ref[k], w_ref[k])
        o_ref[...] = r_ref[...] + scale * acc

    row = _bs((tm, d), lambda i: (i, 0))
    return _call(
        host, body, name=name, grid=(t // tm,),
        in_specs=[_bs((nk, tm, f), lambda i: (0, i, 0)), _bs((nk, None, f, d), lambda i: (0, layer, 0, 0)), row],
        out_specs=row, out_shape=S((t, d), F32),
        compiler_params=_cp("parallel"),
        args=(a, wd, h))


def ffn_bwd_act(dh, wd, layer, gg, uu, name, host=None):
    nk, t, f = gg.shape
    d = dh.shape[1]
    tm = _row_tile(t, FFN_SHARDS_ROWS)

    def body(dh_ref, wd_ref, g_ref, u_ref, dg_ref, du_ref):
        dh16 = dh_ref[...].astype(BF16)
        for k in range(nk):
            da = 0.5 * _dot_nt(dh16, wd_ref[k])
            dg_ref[k] = (da * g_ref[k].astype(F32)).astype(BF16)
            du_ref[k] = (da * u_ref[k].astype(F32)).astype(BF16)

    aspec = _bs((nk, tm, f), lambda i: (0, i, 0))
    return _call(
        host, body, name=name, grid=(t // tm,),
        in_specs=[_bs((tm, d), lambda i: (i, 0)), _bs((nk, None, f, d), lambda i: (0, layer, 0, 0)), aspec, aspec],
        out_specs=[aspec, aspec],
        out_shape=[S((nk, t, f), BF16), S((nk, t, f), BF16)],
        compiler_params=_cp("parallel"),
        args=(dh, wd, gg, uu))


def nt_acc_normbwd(terms, nk, h, gain, dh, name, host=None):
    t, d = h.shape
    tm = _row_tile(t, BWD_IN_ROWS)
    sub = _row_tile(tm, 256)
    nterm = len(terms)
    picks = [term[4] for term in terms]

    def body(*refs):
        xs = refs[:2 * nterm]
        h_ref, g_ref, dh_ref, o_ref, dg_ref, acc = refs[2 * nterm:]

        @pl.when(pl.program_id(0) == 0)
        def _():
            dg_ref[...] = jnp.zeros_like(dg_ref)
        tot = None
        for j in range(nterm):
            for k in range(nk):
                part = _dot_nt(picks[j](xs[2 * j], k), xs[2 * j + 1][k])
                tot = part if tot is None else tot + part
        acc[...] = tot

        def rows_of(cidx, dgain):
            rows = pl.ds(pl.multiple_of(cidx * sub, sub), sub)
            dx, dgc = _rmsnorm_bwd(h_ref[rows, :], g_ref[...], acc[rows, :])
            o_ref[rows, :] = dh_ref[rows, :] + dx
            return dgain + dgc
        dg_ref[...] += lax.fori_loop(0, tm // sub, rows_of, jnp.zeros((1, d), F32))

    in_specs, args = [], []
    for x, xs_, w, ws_, _ in terms:
        in_specs += [xs_, ws_]
        args += [x, w]
    row = _bs((tm, d), lambda i: (i, 0))
    vec = _bs((1, d), lambda i: (0, 0))
    return _call(
        host, body, name=name, grid=(t // tm,),
        in_specs=in_specs + [row, vec, row],
        out_specs=[row, vec],
        out_shape=[S((t, d), F32), S((1, d), F32)],
        scratch_shapes=[pltpu.VMEM((tm, d), F32)],
        compiler_params=_cp("arbitrary"),
        args=(*args, h, gain, dh))


def ffn_bwd_in(dg, du, wg, wu, layer, h, gain, dh, name, host=None):
    nk, t, f = dg.shape
    d = h.shape[1]
    tm = _row_tile(t, BWD_IN_ROWS)
    aspec = _bs((nk, tm, f), lambda i: (0, i, 0))
    wspec = _bs((nk, None, d, f), lambda i: (0, layer, 0, 0))
    pick = lambda x_ref, k: x_ref[k]
    return nt_acc_normbwd([(dg, aspec, wg, wspec, pick), (du, aspec, wu, wspec, pick)], nk, h, gain, dh, name, host)


def tn_mm(x, x_spec, y, y_spec, nblk, t, ka, nb, out_shape, out_spec, scale, prev, name, host=None):
    tk = _row_tile(t, TN_ROWS)

    def body(*refs):
        if prev is None:
            x_ref, y_ref, o_ref, acc = refs
        else:
            x_ref, y_ref, _, o_ref, acc = refs
        j = pl.program_id(1)

        @pl.when(j == 0)
        def _():
            acc[...] = jnp.zeros_like(acc)
        acc[...] += _dot_tn(x_ref[...].astype(BF16), y_ref[...].astype(BF16))

        @pl.when(j == t // tk - 1)
        def _():
            o_ref[...] = (scale * acc[...]).astype(o_ref.dtype)

    in_specs = [x_spec(tk), y_spec(tk)]
    args = [x, y]
    aliases = {}
    if prev is not None:
        in_specs.append(pl.BlockSpec(memory_space=pl.ANY))
        args.append(prev)
        aliases = {2: 0}
    return _call(
        host, body, name=name, grid=(nblk, t // tk),
        in_specs=in_specs, out_specs=out_spec, out_shape=out_shape,
        scratch_shapes=[pltpu.VMEM((ka, nb), F32)],
        aliases=aliases,
        compiler_params=_cp("parallel", "arbitrary"),
        args=tuple(args))


def ffn_wgrads(which, hn, dh, aa, dg, du, layer, grads, run):
    nk, t, f = aa.shape
    d = hn.shape[1]
    hn_spec = lambda tk: _bs((tk, d), lambda k, j: (j, 0))
    a_spec = lambda tk: _bs((None, tk, f), lambda k, j: (k, j, 0))
    shape_gu, spec_gu = S((nk, 2, d, f), BF16), _bs((None, None, d, f), lambda k, j: (k, layer, 0, 0))
    shape_d, spec_d = S((nk, 2, f, d), BF16), _bs((None, None, f, d), lambda k, j: (k, layer, 0, 0))
    for suffix, x, xs, y, ys, ka, nb, shp, spec, scale in (
            ("gate", hn, hn_spec, dg, a_spec, d, f, shape_gu, spec_gu, 1.0),
            ("up", hn, hn_spec, du, a_spec, d, f, shape_gu, spec_gu, 1.0),
            ("down", aa, a_spec, dh, hn_spec, f, d, shape_d, spec_d, 0.5)):
        key = f"{which}_w_{suffix}"
        grads[key] = run(tn_mm, x, xs, y, ys, nk, t, ka, nb, shp, spec, scale, grads.get(key), name=f"{which}_gw_{layer}_{suffix}")


def norm_mm(h, gain, w, name, host=None):
    t, d = h.shape
    nb, _, bw = w.shape
    tm = _row_tile(t, FFN_ROWS)

    def body(h_ref, g_ref, w_ref, hn_ref, z_ref, hn_scr):
        @pl.when(pl.program_id(1) == 0)
        def _():
            x = h_ref[...]
            hn = (x * _rstd(x) * g_ref[...]).astype(BF16)
            hn_scr[...] = hn
            hn_ref[...] = hn
        z_ref[...] = _dot(hn_scr[...], w_ref[...])

    return _call(
        host, body, name=name, grid=(t // tm, nb),
        in_specs=[_bs((tm, d), lambda i, k: (i, 0)), _bs((1, d), lambda i, k: (0, 0)), _bs((None, d, bw), lambda i, k: (k, 0, 0))],
        out_specs=[_bs((tm, d), lambda i, k: (i, 0)), _bs((tm, bw), lambda i, k: (i, k))],
        out_shape=[S((t, d), BF16), S((t, nb * bw), F32)],
        scratch_shapes=[pltpu.VMEM((tm, d), BF16)],
        compiler_params=_cp("parallel", "arbitrary"),
        args=(h, gain, w))


def nt_mm(a, w, name):
    t, k = a.shape
    n = w.shape[0]
    tm = _row_tile(t, 512)

    def body(a_ref, w_ref, o_ref):
        o_ref[...] = _dot_nt(a_ref[...].astype(BF16), w_ref[...])

    return pl.pallas_call(
        body, name=name, grid=(t // tm,),
        in_specs=[_bs((tm, k), lambda i: (i, 0)), _bs((n, k), lambda i: (0, 0))],
        out_specs=_bs((tm, n), lambda i: (i, 0)),
        out_shape=S((t, n), F32),
        compiler_params=_cp("parallel"),
    )(a, w)


def _head_mean_matrix():
    m = np.kron(np.eye(N_HEADS, dtype=np.float32), np.full((HEAD_DIM, HEAD_DIM), 1.0 / HEAD_DIM, np.float32))
    return jnp.asarray(m, BF16)


def _head_sum_matrix():
    m = np.kron(np.eye(N_HEADS, dtype=np.float32), np.ones((HEAD_DIM, HEAD_DIM), np.float32))
    return jnp.asarray(m, BF16)


def _rel_bucket_np(dist):
    max_exact = REL_BUCKETS // 2
    n = np.maximum(dist, 1).astype(np.float32)
    large = max_exact + (np.log(n / np.float32(max_exact)) / np.float32(math.log(REL_MAX_DIST / max_exact))
                         * np.float32(REL_BUCKETS - max_exact)).astype(np.int32)
    large = np.minimum(large, REL_BUCKETS - 1)
    return np.where(dist < max_exact, dist, large)


def _band_tables():
    qi = np.arange(BAND)[:, None]
    kj = np.arange(2 * BAND)[None, :]
    dist_q = qi + BAND - kj
    qq = np.arange(2 * BAND)[:, None]
    kk = np.arange(BAND)[None, :]
    dist_k = qq - kk
    out = []
    for dist in (dist_q, dist_k):
        valid = (dist >= 0) & (dist <= BAND)
        bucket = np.stack([_rel_bucket_np(np.clip(dist, 0, BAND) * d) for d in DILATIONS])
        out.append((bucket, valid))
    return out


def band_bias(rel_bias):
    out = []
    for bucket, valid in _band_tables():
        bucket = np.where(valid[None], bucket, -1)[:, None]
        tab = jnp.full((len(DILATIONS), N_HEADS) + bucket.shape[2:], NEG, F32)
        for b in range(REL_BUCKETS):
            if (bucket == b).any():
                tab = jnp.where(bucket == b, rel_bias[b][None, :, None, None], tab)
        out.append(tab.reshape(len(DILATIONS), N_HEADS // 2, 2 * tab.shape[2], tab.shape[3]))
    return out


LANE_TILE = 128
N_LANE_TILES = ATTN_W // LANE_TILE


def _view_shape(t, dil):
    return (t // dil, dil * ATTN_W)


def _view_spec(tm, dil):
    return _bs((tm // dil, dil * ATTN_W), lambda i: (i, 0))


def _cols_to(scr, val):
    for cc in range(N_LANE_TILES):
        scr[cc] = val[:, LANE_TILE * cc:LANE_TILE * (cc + 1)]


def _cols_from(scr):
    return jnp.concatenate([scr[cc] for cc in range(N_LANE_TILES)], axis=1)


def _write_view(scr, out_ref, dil):
    if dil == 1:
        out_ref[...] = _cols_from(scr).astype(out_ref.dtype)
        return
    rows = scr.shape[1] // dil
    for r in range(dil):
        for cc in range(N_LANE_TILES):
            c0 = r * ATTN_W + LANE_TILE * cc
            out_ref[:, c0:c0 + LANE_TILE] = scr[cc, pl.ds(r, rows, stride=dil), :].astype(out_ref.dtype)


def _read_view(scr, in_ref, dil):
    if dil == 1:
        return in_ref[...].astype(F32)
    rows = scr.shape[1] // dil
    for r in range(dil):
        for cc in range(N_LANE_TILES):
            c0 = r * ATTN_W + LANE_TILE * cc
            scr[cc, pl.ds(r, rows, stride=dil), :] = in_ref[:, c0:c0 + LANE_TILE].astype(F32)
    return _cols_from(scr)


def hyb_prep(z, q_gain, k_gain, name):
    t = z.shape[0]
    tm = _row_tile(t, 512)
    seg = _head_mean_matrix()
    nd = len(DILATIONS)

    def body(q_ref, k_ref, v_ref, qg_ref, kg_ref, seg_ref, *rest):
        outs, scr = rest[:3 * nd], rest[3 * nd]
        q = q_ref[...]
        k = k_ref[...]
        vals = (q * lax.rsqrt(_seg_dot(q * q, seg_ref[...]) + EPS) * qg_ref[...],
                k * lax.rsqrt(_seg_dot(k * k, seg_ref[...]) + EPS) * kg_ref[...],
                v_ref[...])
        for j, val in enumerate(vals):
            _cols_to(scr, val)
            for g, dil in enumerate(DILATIONS):
                _write_view(scr, outs[3 * g + j], dil)

    col = lambda c: _bs((tm, ATTN_W), lambda i: (i, c))
    vec = _bs((1, ATTN_W), lambda i: (0, 0))
    res = pl.pallas_call(
        body, name=name, grid=(t // tm,),
        in_specs=[col(3), col(4), col(5), vec, vec, _bs((ATTN_W, ATTN_W), lambda i: (0, 0))],
        out_specs=[_view_spec(tm, dil) for dil in DILATIONS for _ in range(3)],
        out_shape=[S(_view_shape(t, dil), BF16) for dil in DILATIONS for _ in range(3)],
        scratch_shapes=[pltpu.VMEM((N_LANE_TILES, tm, LANE_TILE), F32)],
        compiler_params=_cp("parallel"),
    )(z, z, z, q_gain, k_gain, seg)
    return {dil: tuple(res[3 * g:3 * g + 3]) for g, dil in enumerate(DILATIONS)}


def _lane_lo(shape):
    return lax.broadcasted_iota(jnp.int32, shape, 1) < HEAD_DIM


def _stack_heads(pair):
    lo = _lane_lo(pair.shape)
    zero = jnp.zeros_like(pair)
    return jnp.concatenate([jnp.where(lo, pair, zero), jnp.where(lo, zero, pair)], axis=0)


def _unstack_heads(st):
    rows = st.shape[0] // 2
    return jnp.where(_lane_lo((rows, st.shape[1])), st[:rows], st[rows:])


def attn_fwd(q, k, v, bias, dil, name, host=None):
    qv, kv, vv = q, k, v
    sub = q.shape[0]
    nb = sub // BAND

    def body(q_ref, kp_ref, kc_ref, vp_ref, vc_ref, b_ref, o_ref, l_ref):
        first = pl.program_id(1) == 0
        colk = lax.broadcasted_iota(jnp.int32, (2 * BAND, 2 * BAND), 1)
        for j in range(N_HEADS // 2):
            sl = slice(2 * HEAD_DIM * j, 2 * HEAD_DIM * (j + 1))
            kk = jnp.concatenate([kp_ref[:, sl], kc_ref[:, sl]], axis=0)
            vv_ = jnp.concatenate([vp_ref[:, sl], vc_ref[:, sl]], axis=0)
            s = _dot_nt(_stack_heads(q_ref[:, sl]), kk) * (HEAD_DIM ** -0.5) + b_ref[j]
            s = jnp.where(jnp.logical_and(first, colk < BAND), NEG, s)
            m = jnp.max(s, axis=-1, keepdims=True)
            p = jnp.exp(s - m)
            l = jnp.sum(p, axis=-1, keepdims=True)
            o_ref[:, sl] = _unstack_heads(_dot(p.astype(BF16), vv_) / l).astype(o_ref.dtype)
            l_ref[:, sl] = _unstack_heads(jnp.broadcast_to(m + jnp.log(l), (2 * BAND, 2 * HEAD_DIM)))

    cur = _bs((BAND, ATTN_W), lambda r, n: (n, r))
    prv = _bs((BAND, ATTN_W), lambda r, n: (jnp.maximum(n - 1, 0), r))
    return _call(
        host, body, name=name, grid=(dil, nb),
        in_specs=[cur, prv, cur, prv, cur, _bs((N_HEADS // 2, 2 * BAND, 2 * BAND), lambda r, n: (0, 0, 0))],
        out_specs=[cur, cur],
        out_shape=[S((sub, dil * ATTN_W), BF16), S((sub, dil * ATTN_W), F32)],
        compiler_params=_cp("parallel", "arbitrary"),
        args=(qv, kv, kv, vv, vv, bias))


def hyb_post(z, conv_w, os_, lses, name):
    t = z.shape[0]
    tm = _row_tile(t, 512)
    nd = len(DILATIONS)

    def body(gb_ref, gc_ref, cx_ref, gch_ref, cxh_ref, w_ref, *rest):
        o_refs, l_refs = rest[:nd], rest[nd:2 * nd]
        y_ref, ya_ref = rest[2 * nd:2 * nd + 2]
        lt_refs, scr = rest[2 * nd + 2:3 * nd + 2], rest[3 * nd + 2]
        i = pl.program_id(0)
        m = gc_ref[...] * cx_ref[...]
        mh = jnp.where(i == 0, 0.0, gch_ref[...] * cxh_ref[...])
        conv = w_ref[0:1, :] * _shift_down(m, mh, 2) + w_ref[1:2, :] * _shift_down(m, mh, 1) + w_ref[2:3, :] * m
        y_ref[0] = (gb_ref[...] * conv).astype(BF16)
        ls = [_read_view(scr, l_refs[g], dil) for g, dil in enumerate(DILATIONS)]
        mx = functools.reduce(jnp.maximum, ls)
        es = [jnp.exp(l - mx) for l in ls]
        den = functools.reduce(lambda a, b: a + b, es)
        num = es[0] * _read_view(scr, o_refs[0], DILATIONS[0])
        for g in range(1, nd):
            num = num + es[g] * _read_view(scr, o_refs[g], DILATIONS[g])
        ya = num / den
        y_ref[1] = ya.astype(BF16)
        ya_ref[...] = ya
        _cols_to(scr, mx + jnp.log(den))
        for g, dil in enumerate(DILATIONS):
            _write_view(scr, lt_refs[g], dil)

    hb = tm // 8
    col = lambda c: _bs((tm, CONV_W), lambda i: (i, c))
    halo = lambda c: _bs((8, CONV_W), lambda i: (jnp.maximum(i * hb - 1, 0), c))
    row = _bs((tm, ATTN_W), lambda i: (i, 0))
    views = [_view_spec(tm, dil) for dil in DILATIONS]
    res = pl.pallas_call(
        body, name=name, grid=(t // tm,),
        in_specs=[col(0), col(1), col(2), halo(1), halo(2), _bs((3, CONV_W), lambda i: (0, 0))] + views * 2,
        out_specs=[_bs((2, tm, ATTN_W), lambda i: (0, i, 0)), row] + views,
        out_shape=[S((2, t, ATTN_W), BF16), S((t, ATTN_W), F32)] + [S(_view_shape(t, dil), F32) for dil in DILATIONS],
        scratch_shapes=[pltpu.VMEM((N_LANE_TILES, tm, LANE_TILE), F32)],
        compiler_params=_cp("parallel"),
    )(z, z, z, z, z, conv_w, *os_, *lses)
    return res[0], res[1], dict(zip(DILATIONS, res[2:]))


def attn_delta(dy, ya, name):
    t = ya.shape[0]
    tm = _row_tile(t, 512)
    seg = _head_sum_matrix()
    nd = len(DILATIONS)

    def body(dy_ref, ya_ref, seg_ref, *rest):
        dl_refs, db_refs, scr = rest[:nd], rest[nd:2 * nd], rest[2 * nd]
        dya = dy_ref[...]
        _cols_to(scr, _seg_dot(dya * ya_ref[...], seg_ref[...]))
        for g, dil in enumerate(DILATIONS):
            _write_view(scr, dl_refs[g], dil)
        _cols_to(scr, dya)
        for g, dil in enumerate(DILATIONS):
            _write_view(scr, db_refs[g], dil)

    row = _bs((tm, ATTN_W), lambda i: (i, 0))
    views = [_view_spec(tm, dil) for dil in DILATIONS]
    res = pl.pallas_call(
        body, name=name, grid=(t // tm,),
        in_specs=[_bs((tm, ATTN_W), lambda i: (i, 1)), row, _bs((ATTN_W, ATTN_W), lambda i: (0, 0))],
        out_specs=views * 2,
        out_shape=[S(_view_shape(t, dil), F32) for dil in DILATIONS] + [S(_view_shape(t, dil), BF16) for dil in DILATIONS],
        scratch_shapes=[pltpu.VMEM((N_LANE_TILES, tm, LANE_TILE), F32)],
        compiler_params=_cp("parallel"),
    )(dy, ya, seg)
    return dict(zip(DILATIONS, res[:nd])), dict(zip(DILATIONS, res[nd:]))


def attn_bwd_dq(q, k, v, dya, lt, delta, bias, dil, name, host=None):
    qv, kv, vv, dv_, lv, ev = q, k, v, dya, lt, delta
    sub = q.shape[0]
    nb = sub // BAND

    def body(q_ref, kp_ref, kc_ref, vp_ref, vc_ref, do_ref, l_ref, e_ref, b_ref, dq_ref, db_ref):
        r, n = pl.program_id(0), pl.program_id(1)

        @pl.when(jnp.logical_and(r == 0, n == 0))
        def _():
            db_ref[...] = jnp.zeros_like(db_ref)
        first = n == 0
        colk = lax.broadcasted_iota(jnp.int32, (2 * BAND, 2 * BAND), 1)
        for j in range(N_HEADS // 2):
            c0 = 2 * HEAD_DIM * j
            sl = slice(c0, c0 + 2 * HEAD_DIM)
            kk = jnp.concatenate([kp_ref[:, sl], kc_ref[:, sl]], axis=0)
            vv_ = jnp.concatenate([vp_ref[:, sl], vc_ref[:, sl]], axis=0)
            lse = jnp.concatenate([l_ref[:, c0:c0 + 1], l_ref[:, c0 + HEAD_DIM:c0 + HEAD_DIM + 1]], axis=0)
            dlt = jnp.concatenate([e_ref[:, c0:c0 + 1], e_ref[:, c0 + HEAD_DIM:c0 + HEAD_DIM + 1]], axis=0)
            s = _dot_nt(_stack_heads(q_ref[:, sl]), kk) * (HEAD_DIM ** -0.5) + b_ref[j]
            s = jnp.where(jnp.logical_and(first, colk < BAND), NEG, s)
            p = jnp.exp(s - lse)
            ds = p * (_dot_nt(_stack_heads(do_ref[:, sl]), vv_) - dlt)
            db_ref[j] += ds
            dq_ref[:, sl] = (_unstack_heads(_dot(ds.astype(BF16), kk)) * (HEAD_DIM ** -0.5)).astype(dq_ref.dtype)

    cur = _bs((BAND, ATTN_W), lambda r, n: (n, r))
    prv = _bs((BAND, ATTN_W), lambda r, n: (jnp.maximum(n - 1, 0), r))
    tab = _bs((N_HEADS // 2, 2 * BAND, 2 * BAND), lambda r, n: (0, 0, 0))
    dq, db = _call(
        host, body, name=name, grid=(dil, nb),
        in_specs=[cur, prv, cur, prv, cur, cur, cur, cur, tab],
        out_specs=[cur, tab],
        out_shape=[S((sub, dil * ATTN_W), BF16), S((N_HEADS // 2, 2 * BAND, 2 * BAND), F32)],
        compiler_params=_cp("arbitrary", "arbitrary"),
        args=(qv, kv, kv, vv, vv, dv_, lv, ev, bias))
    return dq, db.reshape(N_HEADS, BAND, 2 * BAND)


def attn_bwd_dkv(q, k, v, dya, lt, delta, bias_k, dil, name, host=None):
    qv, kv, vv, dv_, lv, ev = q, k, v, dya, lt, delta
    sub = q.shape[0]
    nb = sub // BAND

    def body(k_ref, v_ref, qc_ref, qn_ref, dc_ref, dn_ref, lc_ref, ln_ref, ec_ref, en_ref, b_ref, dk_ref, dv_ref):
        last = pl.program_id(1) == nb - 1
        rowq = lax.broadcasted_iota(jnp.int32, (4 * BAND, BAND), 0)
        from_next = (rowq & BAND) != 0
        for j in range(N_HEADS // 2):
            c0 = 2 * HEAD_DIM * j
            sl = slice(c0, c0 + 2 * HEAD_DIM)
            kp, vp = k_ref[:, sl], v_ref[:, sl]
            q4 = _stack_heads(jnp.concatenate([qc_ref[:, sl], qn_ref[:, sl]], axis=0))
            do4 = _stack_heads(jnp.concatenate([dc_ref[:, sl], dn_ref[:, sl]], axis=0))
            lse = jnp.concatenate([ref[:, c:c + 1] for c in (c0, c0 + HEAD_DIM) for ref in (lc_ref, ln_ref)], axis=0)
            dlt = jnp.concatenate([ref[:, c:c + 1] for c in (c0, c0 + HEAD_DIM) for ref in (ec_ref, en_ref)], axis=0)
            s = _dot_nt(q4, kp) * (HEAD_DIM ** -0.5) + b_ref[j]
            s = jnp.where(jnp.logical_and(last, from_next), NEG, s)
            p = jnp.exp(s - lse)
            ds = p * (_dot_nt(do4, vp) - dlt)
            dv_ref[:, sl] = _dot_tn(p.astype(BF16), do4).astype(dv_ref.dtype)
            dk_ref[:, sl] = (_dot_tn(ds.astype(BF16), q4) * (HEAD_DIM ** -0.5)).astype(dk_ref.dtype)

    cur = _bs((BAND, ATTN_W), lambda r, n: (n, r))
    nxt = _bs((BAND, ATTN_W), lambda r, n: (jnp.minimum(n + 1, nb - 1), r))
    tab = _bs((N_HEADS // 2, 4 * BAND, BAND), lambda r, n: (0, 0, 0))
    return _call(
        host, body, name=name, grid=(dil, nb),
        in_specs=[cur, cur, cur, nxt, cur, nxt, cur, nxt, cur, nxt, tab],
        out_specs=[cur, cur],
        out_shape=[S((sub, dil * ATTN_W), BF16)] * 2,
        compiler_params=_cp("parallel", "arbitrary"),
        args=(kv, vv, qv, qv, dv_, dv_, lv, lv, ev, ev, bias_k))


def hyb_dz(z, dy, conv_w, q_gain, k_gain, dqs, dks, dvs, name):
    t = z.shape[0]
    tm = _row_tile(t, 512)
    nt = t // tm
    seg = _head_mean_matrix()

    def body(gb_ref, gc_ref, cx_ref, q_ref, k_ref, gch_ref, cxh_ref, gbn_ref, dyc_ref, dyn_ref, w_ref, qg_ref, kg_ref, seg_ref,
             dq1, dq2, dq3, dk1, dk2, dk3, dv1, dv2, dv3, dz_ref, dw_ref, dqg_ref, dkg_ref, scr):
        i = pl.program_id(0)

        def total(parts):
            acc = _read_view(scr, parts[0], DILATIONS[0])
            for g in range(1, len(DILATIONS)):
                acc = acc + _read_view(scr, parts[g], DILATIONS[g])
            return acc

        @pl.when(i == 0)
        def _():
            dw_ref[...] = jnp.zeros_like(dw_ref)
            dqg_ref[...] = jnp.zeros_like(dqg_ref)
            dkg_ref[...] = jnp.zeros_like(dkg_ref)
        gb, gc, cx, dyc = gb_ref[...], gc_ref[...], cx_ref[...], dyc_ref[...]
        m = gc * cx
        mh = jnp.where(i == 0, 0.0, gch_ref[...] * cxh_ref[...])
        m1, m2 = _shift_down(m, mh, 1), _shift_down(m, mh, 2)
        conv = w_ref[0:1, :] * m2 + w_ref[1:2, :] * m1 + w_ref[2:3, :] * m
        dconv = dyc * gb
        dcn = jnp.where(i == nt - 1, 0.0, dyn_ref[...] * gbn_ref[...])
        dm = w_ref[2:3, :] * dconv + w_ref[1:2, :] * _shift_up(dconv, dcn, 1) + w_ref[0:1, :] * _shift_up(dconv, dcn, 2)
        dz_ref[:, 0:CONV_W] = (dyc * conv).astype(BF16)
        dz_ref[:, CONV_W:2 * CONV_W] = (dm * cx).astype(BF16)
        dz_ref[:, 2 * CONV_W:3 * CONV_W] = (dm * gc).astype(BF16)
        dw_ref[0:1, :] += jnp.sum(dconv * m2, axis=0, keepdims=True)
        dw_ref[1:2, :] += jnp.sum(dconv * m1, axis=0, keepdims=True)
        dw_ref[2:3, :] += jnp.sum(dconv * m, axis=0, keepdims=True)
        base = 3 * CONV_W
        for idx, (x_ref, g_ref, parts, dgain_ref) in enumerate(((q_ref, qg_ref, (dq1, dq2, dq3), dqg_ref),
                                                                  (k_ref, kg_ref, (dk1, dk2, dk3), dkg_ref))):
            x = x_ref[...]
            dxh = total(parts)
            r = lax.rsqrt(_seg_dot(x * x, seg_ref[...]) + EPS)
            xhat = x * r
            tt = dxh * g_ref[...]
            dx = r * (tt - xhat * _seg_dot(tt * xhat, seg_ref[...]))
            dz_ref[:, base + idx * ATTN_W:base + (idx + 1) * ATTN_W] = dx.astype(BF16)
            dgain_ref[...] += jnp.sum(dxh * xhat, axis=0, keepdims=True)
        dz_ref[:, base + 2 * ATTN_W:base + 3 * ATTN_W] = total((dv1, dv2, dv3)).astype(BF16)

    hb = tm // 8
    col = lambda c: _bs((tm, CONV_W), lambda i: (i, c))
    prev = lambda c: _bs((8, CONV_W), lambda i: (jnp.maximum(i * hb - 1, 0), c))
    nxt = lambda c: _bs((8, CONV_W), lambda i: (jnp.minimum((i + 1) * hb, t // 8 - 1), c))
    row = _bs((tm, ATTN_W), lambda i: (i, 0))
    vec = _bs((1, ATTN_W), lambda i: (0, 0))
    return pl.pallas_call(
        body, name=name, grid=(nt,),
        in_specs=[col(0), col(1), col(2), col(3), col(4), prev(1), prev(2), nxt(0), col(0), nxt(0),
                  _bs((3, CONV_W), lambda i: (0, 0)), vec, vec, _bs((ATTN_W, ATTN_W), lambda i: (0, 0))]
                 + [_view_spec(tm, dil) for dil in DILATIONS] * 3,
        out_specs=[_bs((tm, 6 * CONV_W), lambda i: (i, 0)), _bs((3, CONV_W), lambda i: (0, 0)), vec, vec],
        out_shape=[S((t, 6 * CONV_W), BF16), S((3, CONV_W), F32), S((1, ATTN_W), F32), S((1, ATTN_W), F32)],
        scratch_shapes=[pltpu.VMEM((N_LANE_TILES, tm, LANE_TILE), F32)],
        compiler_params=_cp("arbitrary"),
    )(z, z, z, z, z, z, z, z, dy, dy, conv_w, q_gain, k_gain, seg, *dqs, *dks, *dvs)


def rel_bias_grad(dbs, name):
    (bq, vq), _ = _band_tables()
    onehot = np.zeros((len(DILATIONS), REL_BUCKETS, BAND * 2 * BAND), np.float32)
    for g in range(len(DILATIONS)):
        idx = bq[g].reshape(-1)
        ok = vq.reshape(-1)
        onehot[g, idx[ok], np.nonzero(ok)[0]] = 1.0
    onehot = jnp.asarray(onehot, BF16)
    flat = [d.reshape(N_HEADS, BAND * 2 * BAND) for d in dbs]

    def body(oh_ref, d1, d2, d3, o_ref):
        acc = jnp.zeros((REL_BUCKETS, N_HEADS), F32)
        for g, d in enumerate((d1, d2, d3)):
            x = d[...]
            hi = x.astype(BF16)
            lo = (x - hi.astype(F32)).astype(BF16)
            acc += _dot_nt(oh_ref[g], hi) + _dot_nt(oh_ref[g], lo)
        o_ref[...] = acc

    full = lambda shp: _bs(shp, lambda: tuple(0 for _ in shp))
    return pl.pallas_call(
        body, name=name,
        in_specs=[full(onehot.shape)] + [full(flat[0].shape)] * 3,
        out_specs=full((REL_BUCKETS, N_HEADS)),
        out_shape=S((REL_BUCKETS, N_HEADS), F32),
        compiler_params=pltpu.CompilerParams(vmem_limit_bytes=VMEM_LIMIT),
    )(onehot, *flat)


def _lru_gates(xb, wa_ref, wx_ref, ba, bx):
    xb16 = xb.astype(BF16)
    ga = jnp.concatenate([_dot(xb16[:, LRU_BLOCK * g:LRU_BLOCK * (g + 1)], wa_ref[g]) for g in range(LRU_BLOCKS)], axis=1) + ba
    gx = jnp.concatenate([_dot(xb16[:, LRU_BLOCK * g:LRU_BLOCK * (g + 1)], wx_ref[g]) for g in range(LRU_BLOCKS)], axis=1) + bx
    return ga, gx


def _lru_coeffs(ga, gx, lam):
    sga = _sigmoid(ga)
    sp = _softplus(-lam)
    log_a = -LRU_C * sga * sp
    a = jnp.exp(log_a)
    one_m_a2 = _neg_expm1(2.0 * log_a)
    return sga, sp, a, one_m_a2, jnp.sqrt(one_m_a2), _sigmoid(gx)


def rec_fwd(z, conv_w, conv_b, wa, wx, ba, bx, lam, name, host=None):
    t = z.shape[0]
    w = z.shape[1] // 2
    tm = _row_tile(t, 256)

    def body(xp_ref, xh_ref, yb_ref, cw_ref, cb_ref, wa_ref, wx_ref, ba_ref, bx_ref, lam_ref,
             xb_ref, ga_ref, gx_ref, hs_ref, out_ref, carry):
        i = pl.program_id(0)

        @pl.when(i == 0)
        def _():
            carry[...] = jnp.zeros_like(carry)
        xp = xp_ref[...]
        xh = jnp.where(i == 0, 0.0, xh_ref[...])
        xb = cb_ref[...] + cw_ref[3:4, :] * xp
        for j in range(3):
            xb = xb + cw_ref[j:j + 1, :] * _shift_down(xp, xh, 3 - j)
        ga, gx = _lru_gates(xb, wa_ref, wx_ref, ba_ref[...], bx_ref[...])
        _, _, a, _, sq, sgx = _lru_coeffs(ga, gx, lam_ref[...])
        aa, bb = a, sq * sgx * xb
        s = 1
        while s < tm:
            bb = aa * _roll_fill(bb, s, 0.0, False) + bb
            aa = aa * _roll_fill(aa, s, 1.0, False)
            s *= 2
        hs = aa * carry[0:1, :] + bb
        xb_ref[...] = xb
        ga_ref[...] = ga
        gx_ref[...] = gx
        hs_ref[...] = hs
        carry[0:1, :] = hs_ref[tm - 1:tm, :]
        gy, _ = _gelu_and_grad(yb_ref[...])
        out_ref[...] = (hs * gy).astype(BF16)

    hb = tm // 8
    row = _bs((tm, w), lambda i: (i, 0))
    vec = _bs((1, w), lambda i: (0, 0))
    wsp = _bs((LRU_BLOCKS, LRU_BLOCK, LRU_BLOCK), lambda i: (0, 0, 0))
    return _call(
        host, body, name=name, grid=(t // tm,),
        in_specs=[row, _bs((8, w), lambda i: (jnp.maximum(i * hb - 1, 0), 0)), _bs((tm, w), lambda i: (i, 1)),
                  _bs((4, w), lambda i: (0, 0)), vec, wsp, wsp, vec, vec, vec],
        out_specs=[row] * 5,
        out_shape=[S((t, w), F32)] * 4 + [S((t, w), BF16)],
        scratch_shapes=[pltpu.VMEM((8, w), F32)],
        compiler_params=_cp("arbitrary"),
        args=(z, z, z, conv_w, conv_b, wa, wx, ba, bx, lam))


def rec_bwd(d_out, z, xb, ga, gx, hs, conv_w, wa, wx, lam, name, host=None):
    t = z.shape[0]
    w = z.shape[1] // 2
    tm = _row_tile(t, 256)
    nt = t // tm

    def body(do_ref, xp_ref, xph_ref, yb_ref, xb_ref, ga_ref, gx_ref, hs_ref, hsh_ref, cw_ref, wa_ref, wx_ref, lam_ref,
             dz_ref, dga_ref, dgx_ref, sm_ref, c_lam, c_a, c_dxb):
        i = pl.program_id(0)

        @pl.when(i == 0)
        def _():
            sm_ref[...] = jnp.zeros_like(sm_ref)
            c_lam[...] = jnp.zeros_like(c_lam)
            c_a[...] = jnp.zeros_like(c_a)
            c_dxb[...] = jnp.zeros_like(c_dxb)
        d_o, yb, xb, hs = do_ref[...], yb_ref[...], xb_ref[...], hs_ref[...]
        lam = lam_ref[...]
        gy, dgy = _gelu_and_grad(yb)
        dz_ref[:, w:2 * w] = (d_o * hs * dgy).astype(BF16)
        sga, sp, a, one_m_a2, sq, sgx = _lru_coeffs(ga_ref[...], gx_ref[...], lam)
        aa = _shift_up(a, c_a[...], 1)
        bb = d_o * gy
        s = 1
        while s < tm:
            bb = aa * _roll_fill(bb, s, 0.0, True) + bb
            aa = aa * _roll_fill(aa, s, 1.0, True)
            s *= 2
        lmb = aa * c_lam[0:1, :] + bb
        c_a[...] = a[0:8, :]
        c_lam[...] = lmb[0:8, :]
        hprev = _shift_down(hs, jnp.where(i == nt - 1, 0.0, hsh_ref[...]), 1)
        d_sq = lmb * sgx * xb
        d_sgx = lmb * sq * xb
        d_log_a = lmb * hprev * a - d_sq * (1.0 - one_m_a2) / sq
        dga = d_log_a * (-LRU_C * sp) * sga * (1.0 - sga)
        dgx = d_sgx * sgx * (1.0 - sgx)
        dga16, dgx16 = dga.astype(BF16), dgx.astype(BF16)
        dga_ref[...] = dga16
        dgx_ref[...] = dgx16
        dxb = lmb * sq * sgx + jnp.concatenate(
            [_dot_nt(dga16[:, LRU_BLOCK * g:LRU_BLOCK * (g + 1)], wa_ref[g]) + _dot_nt(dgx16[:, LRU_BLOCK * g:LRU_BLOCK * (g + 1)], wx_ref[g])
             for g in range(LRU_BLOCKS)], axis=1)
        nxt = c_dxb[...]
        dxp = cw_ref[3:4, :] * dxb
        for j in range(3):
            dxp = dxp + cw_ref[j:j + 1, :] * _shift_up(dxb, nxt, 3 - j)
        c_dxb[...] = dxb[0:8, :]
        dz_ref[:, 0:w] = dxp.astype(BF16)
        xp = xp_ref[...]
        xph = jnp.where(i == nt - 1, 0.0, xph_ref[...])
        sm_ref[0:1, :] += jnp.sum(dga, axis=0, keepdims=True)
        sm_ref[1:2, :] += jnp.sum(dgx, axis=0, keepdims=True)
        sm_ref[2:3, :] += jnp.sum(d_log_a * (-LRU_C * sga), axis=0, keepdims=True) * (-_sigmoid(-lam))
        sm_ref[3:4, :] += jnp.sum(dxb, axis=0, keepdims=True)
        for j in range(4):
            sm_ref[4 + j:5 + j, :] += jnp.sum(dxb * _shift_down(xp, xph, 3 - j), axis=0, keepdims=True)

    hb = tm // 8
    rev = lambda c: _bs((tm, w), lambda i: (nt - 1 - i, c))
    halo = lambda c: _bs((8, w), lambda i: (jnp.maximum((nt - 1 - i) * hb - 1, 0), c))
    vec = _bs((1, w), lambda i: (0, 0))
    wsp = _bs((LRU_BLOCKS, LRU_BLOCK, LRU_BLOCK), lambda i: (0, 0, 0))
    return _call(
        host, body, name=name, grid=(nt,),
        in_specs=[rev(0), rev(0), halo(0), rev(1), rev(0), rev(0), rev(0), rev(0), halo(0),
                  _bs((4, w), lambda i: (0, 0)), wsp, wsp, vec],
        out_specs=[_bs((tm, 2 * w), lambda i: (nt - 1 - i, 0)), rev(0), rev(0), _bs((8, w), lambda i: (0, 0))],
        out_shape=[S((t, 2 * w), BF16), S((t, w), BF16), S((t, w), BF16), S((8, w), F32)],
        scratch_shapes=[pltpu.VMEM((8, w), F32)] * 3,
        compiler_params=_cp("arbitrary"),
        args=(d_out, z, z, z, xb, ga, gx, hs, hs, conv_w, wa, wx, lam))


def ple_fwd(h, gain, wpg, layer, p, wpp, name, target=None):
    t, d = h.shape
    pd = p.shape[1]
    nk, _, rb, _ = wpg.shape
    cb = wpp.shape[3]
    tm = _row_tile(t, 512)
    row = _bs((tm, d), lambda i: (i, 0))
    in_specs = [row, _bs((1, d), lambda i: (0, 0)), _bs((nk, None, rb, d), lambda i: (0, layer, 0, 0)),
                _bs((tm, pd), lambda i: (i, 0)), _bs((nk, None, pd, cb), lambda i: (0, layer, 0, 0))]

    def forward(h_ref, g_ref, wg_ref, p_ref, wp_ref, hn_ref, gp_ref, pp_ref):
        x = h_ref[...]
        hn = (x * _rstd(x) * g_ref[...]).astype(BF16)
        gp = _dot(hn[:, 0:rb], wg_ref[0])
        for k in range(1, nk):
            gp = gp + _dot(hn[:, rb * k:rb * (k + 1)], wg_ref[k])
        pp = jnp.concatenate([_dot(p_ref[...].astype(BF16), wp_ref[k]) for k in range(nk)], axis=1)
        hn_ref[...] = hn
        gp_ref[...] = gp
        pp_ref[...] = pp
        return x + _sigmoid(gp) * pp

    if target is not None:
        def body_loss(h_ref, g_ref, wg_ref, p_ref, wp_ref, t_ref, l_ref, dy_ref, hn_ref, gp_ref, pp_ref):
            @pl.when(pl.program_id(0) == 0)
            def _():
                l_ref[...] = jnp.zeros_like(l_ref)
            err = forward(h_ref, g_ref, wg_ref, p_ref, wp_ref, hn_ref, gp_ref, pp_ref) - t_ref[...]
            dy_ref[...] = err * (1.0 / d)
            l_ref[...] += jnp.sum(jnp.sum(err * err, axis=1, keepdims=True), axis=0, keepdims=True) * (0.5 / d)

        return pl.pallas_call(
            body_loss, name=name, grid=(t // tm,),
            in_specs=in_specs + [row],
            out_specs=[_bs((1, 1), lambda i: (0, 0))] + [row] * 4,
            out_shape=[S((1, 1), F32), S((t, d), F32), S((t, d), BF16), S((t, d), F32), S((t, d), F32)],
            compiler_params=_cp("arbitrary"),
        )(h, gain, wpg, p, wpp, target)

    def body(h_ref, g_ref, wg_ref, p_ref, wp_ref, o_ref, hn_ref, gp_ref, pp_ref):
        o_ref[...] = forward(h_ref, g_ref, wg_ref, p_ref, wp_ref, hn_ref, gp_ref, pp_ref)

    return pl.pallas_call(
        body, name=name, grid=(t // tm,),
        in_specs=in_specs,
        out_specs=[row] * 4,
        out_shape=[S((t, d), F32), S((t, d), BF16), S((t, d), F32), S((t, d), F32)],
        compiler_params=_cp("parallel"),
    )(h, gain, wpg, p, wpp)


def ple_bwd(dh, h, gain, wpg, layer, gp, pp, name, host=None):
    t, d = h.shape
    nk, _, rb, _ = wpg.shape
    tm = _row_tile(t, 512)

    def body(dh_ref, h_ref, g_ref, wg_ref, gp_ref, pp_ref, o_ref, dgp_ref, dpp_ref, dg_ref):
        @pl.when(pl.program_id(0) == 0)
        def _():
            dg_ref[...] = jnp.zeros_like(dg_ref)
        d_h = dh_ref[...]
        gate = _sigmoid(gp_ref[...])
        dgp = (d_h * pp_ref[...] * gate * (1.0 - gate)).astype(BF16)
        dgp_ref[...] = dgp
        dpp_ref[...] = (d_h * gate).astype(BF16)
        dhn = jnp.concatenate([_dot_nt(dgp, wg_ref[k]) for k in range(nk)], axis=1)
        dx, dgain = _rmsnorm_bwd(h_ref[...], g_ref[...], dhn)
        o_ref[...] = d_h + dx
        dg_ref[...] += dgain

    row = _bs((tm, d), lambda i: (i, 0))
    vec = _bs((1, d), lambda i: (0, 0))
    return _call(
        host, body, name=name, grid=(t // tm,),
        in_specs=[row, row, vec, _bs((nk, None, rb, d), lambda i: (0, layer, 0, 0)), row, row],
        out_specs=[row, row, row, vec],
        out_shape=[S((t, d), F32), S((t, d), BF16), S((t, d), BF16), S((1, d), F32)],
        compiler_params=_cp("arbitrary"),
        args=(dh, h, gain, wpg, gp, pp))


def _vec(a, i):
    return a[i:i + 1]


def local_step(x, p, target, w, plan=None):
    t = x.shape[0]
    tm = _row_tile(t, 512)
    grads = {}
    if plan is not None:
        plan.grads = grads
    saved = []
    h = x
    bias_q, bias_k = band_bias(w["rel_bias"])
    qg = jnp.tile(w["hyb_q_gain"], (1, N_HEADS))
    kg = jnp.tile(w["hyb_k_gain"], (1, N_HEADS))

    def run(fn, *a, name):
        hst = plan.host(name) if plan is not None else None
        out = fn(*a, name, hst)
        if hst is not None:
            plan.done(hst)
        return out

    def lru_blocks(n):
        return jnp.transpose(w[n].reshape(N_SHARD, LRU_BLOCKS, 64, LRU_BLOCK), (1, 0, 2, 3)).reshape(LRU_BLOCKS, LRU_BLOCK, LRU_BLOCK)

    for i in range(2):
        s = {}
        s["h0"] = h
        s["hn1"], s["g1"], s["u1"], s["a1"] = run(ffn_up, h, _vec(w["ffn1_norm"], i), w[f"ffn1_w_gate/{i}"], w[f"ffn1_w_up/{i}"], 0, name=f"ffn1_up_{i}")
        h = run(ffn_down, s["a1"], w[f"ffn1_w_down/{i}"], 0, h, name=f"ffn1_down_{i}")
        s["h1"] = h
        if i == 0:
            w_hyb_in = w["hyb_w_in"].reshape(N_SHARD, D_MODEL, -1)
            w_hyb_out = w["hyb_w_out"].reshape(D_MODEL, D_MODEL)
            s["hnm"], s["z"] = run(norm_mm, h, _vec(w["mix_norm"], i), w_hyb_in, name="hyb_in")
            s["qkv"] = hyb_prep(s["z"], qg, kg, "hyb_prep")
            os_, lses = [], []
            for g, dil in enumerate(DILATIONS):
                o, l = run(attn_fwd, *s["qkv"][dil], bias_q[g], dil, name=f"attn_fwd_{dil}")
                os_.append(o)
                lses.append(l)
            s["y2"], s["ya"], s["lt"] = hyb_post(s["z"], w["hyb_conv_w"], os_, lses, "hyb_post")
            h = run(functools.partial(ffn_down, scale=1.0), s["y2"], w_hyb_out.reshape(2, 1, ATTN_W, D_MODEL), 0, h, name="hyb_out")
        else:
            w_rec_in = w["rec_w_in"].reshape(N_SHARD, D_MODEL, -1)
            s["hnm"], s["z"] = run(norm_mm, h, _vec(w["mix_norm"], i), w_rec_in, name="rec_in")
            w_rec_out = w["rec_w_out"].reshape(D_MODEL, D_MODEL)
            lru_wa, lru_wx = lru_blocks("lru_wa"), lru_blocks("lru_wx")
            s["xb"], s["ga"], s["gx"], s["hs"], s["ro"] = run(
                rec_fwd, s["z"], w["rec_conv_w"], w["rec_conv_b"], lru_wa, lru_wx, w["lru_ba"], w["lru_bx"], w["lru_lambda"], name="rec_fwd")
            h = run(mm_acc, s["ro"], _bs((tm, D_MODEL), lambda r, k: (r, 0)), w_rec_out, _bs((D_MODEL, D_MODEL), lambda r, k: (0, 0)),
                    h, 1.0, 1, t, D_MODEL, tm, name="rec_out")
        s["h2"] = h
        s["hn2"], s["g2"], s["u2"], s["a2"] = run(ffn_up, h, _vec(w["ffn2_norm"], i), w[f"ffn2_w_gate/{i}"], w[f"ffn2_w_up/{i}"], 0, name=f"ffn2_up_{i}")
        h = run(ffn_down, s["a2"], w[f"ffn2_w_down/{i}"], 0, h, name=f"ffn2_down_{i}")
        s["h3"] = h
        if i == 0:
            h, s["hnp"], s["gp"], s["pp"] = ple_fwd(h, _vec(w["ple_norm"], i), w[f"ple_w_gate/{i}"], 0, p[i], w[f"ple_w_proj/{i}"], f"ple_fwd_{i}")
        else:
            loss, dh, s["hnp"], s["gp"], s["pp"] = ple_fwd(h, _vec(w["ple_norm"], i), w[f"ple_w_gate/{i}"], 0, p[i], w[f"ple_w_proj/{i}"],
                                                           f"ple_fwd_{i}", target)
        saved.append(s)

    norm_g = {n: [None, None] for n in ("ffn1_norm", "mix_norm", "ffn2_norm", "ple_norm")}
    for i in (1, 0):
        s = saved[i]
        dh_out = dh
        dh, dgp, dpp, norm_g["ple_norm"][i] = run(ple_bwd, dh_out, s["h3"], _vec(w["ple_norm"], i), w[f"ple_w_gate/{i}"], 0, s["gp"], s["pp"],
                                                  name=f"ple_bwd_{i}")
        grads["ple_w_gate"] = run(tn_mm, s["hnp"], lambda tk: _bs((tk, 256), lambda k, j: (j, k)), dgp, lambda tk: _bs((tk, D_MODEL), lambda k, j: (j, 0)),
                                  N_SHARD, t, 256, D_MODEL, S((N_SHARD, 2, 256, D_MODEL), BF16),
                                  _bs((None, None, 256, D_MODEL), lambda k, j, i=i: (k, i, 0, 0)), 1.0, grads.get("ple_w_gate"), name=f"ple_gw_gate_{i}")
        grads["ple_w_proj"] = run(tn_mm, p[i], lambda tk: _bs((tk, 256), lambda k, j: (j, 0)), dpp, lambda tk: _bs((tk, 256), lambda k, j: (j, k)),
                                  N_SHARD, t, 256, 256, S((N_SHARD, 2, 256, 256), BF16),
                                  _bs((None, None, 256, 256), lambda k, j, i=i: (k, i, 0, 0)), 1.0, grads.get("ple_w_proj"), name=f"ple_gw_proj_{i}")
        dh_out = dh
        dg, du = run(ffn_bwd_act, dh_out, w[f"ffn2_w_down/{i}"], 0, s["g2"], s["u2"], name=f"ffn2_bwd_act_{i}")
        ffn_wgrads("ffn2", s["hn2"], dh_out, s["a2"], dg, du, i, grads, run)
        dh, norm_g["ffn2_norm"][i] = run(ffn_bwd_in, dg, du, w[f"ffn2_w_gate/{i}"], w[f"ffn2_w_up/{i}"], 0, s["h2"], _vec(w["ffn2_norm"], i), dh_out,
                                         name=f"ffn2_bwd_in_{i}")
        dh_out = dh
        if i == 1:
            d_o = nt_mm(dh_out, w_rec_out, "rec_bwd_out")
            grads["rec_w_out"] = run(tn_mm, s["ro"], lambda tk: _bs((tk, 256), lambda k, j: (j, k)), dh_out, lambda tk: _bs((tk, D_MODEL), lambda k, j: (j, 0)),
                                     N_SHARD, t, 256, D_MODEL, S((N_SHARD, 256, D_MODEL), BF16), _bs((None, 256, D_MODEL), lambda k, j: (k, 0, 0)),
                                     1.0, None, name="rec_gw_out").reshape(N_SHARD, 1, 256, D_MODEL)
            dz, dga, dgx, small = run(rec_bwd, d_o, s["z"], s["xb"], s["ga"], s["gx"], s["hs"], w["rec_conv_w"], lru_wa, lru_wx, w["lru_lambda"],
                                      name="rec_bwd")
            blk = lambda tk: _bs((tk, LRU_BLOCK), lambda k, j: (j, k))
            for nm, dgt in (("lru_wa", dga), ("lru_wx", dgx)):
                gw = run(tn_mm, s["xb"], blk, dgt, blk, LRU_BLOCKS, t, LRU_BLOCK, LRU_BLOCK, S((LRU_BLOCKS, LRU_BLOCK, LRU_BLOCK), BF16),
                         _bs((None, LRU_BLOCK, LRU_BLOCK), lambda k, j: (k, 0, 0)), 1.0, None, name="rec_gw_" + nm)
                grads[nm] = jnp.transpose(gw.reshape(LRU_BLOCKS, N_SHARD, 64, LRU_BLOCK), (1, 0, 2, 3)).reshape(N_SHARD, 1, LRU_BLOCKS, 64, LRU_BLOCK)
            grads["lru_ba"], grads["lru_bx"], grads["lru_lambda"], grads["rec_conv_b"] = (small[r:r + 1] for r in range(4))
            grads["rec_conv_w"] = small[4:8]
            nb_, bw = N_SHARD, 512
            w_in, nm_in = w_rec_in, "rec_w_in"
        else:
            dy = nt_mm(dh_out, w_hyb_out, "hyb_bwd_out")
            grads["hyb_w_out"] = run(tn_mm, s["y2"], lambda tk: _bs((None, tk, 256), lambda k, j: (k // 2, j, k % 2)), dh_out,
                                     lambda tk: _bs((tk, D_MODEL), lambda k, j: (j, 0)),
                                     N_SHARD, t, 256, D_MODEL, S((N_SHARD, 256, D_MODEL), BF16), _bs((None, 256, D_MODEL), lambda k, j: (k, 0, 0)),
                                     1.0, None, name="hyb_gw_out").reshape(N_SHARD, 1, 256, D_MODEL)
            delta, dya = attn_delta(dy, s["ya"], "attn_delta")
            dqs, dks, dvs, dbs = [], [], [], []
            for g, dil in enumerate(DILATIONS):
                dq, db = run(attn_bwd_dq, *s["qkv"][dil], dya[dil], s["lt"][dil], delta[dil], bias_q[g], dil, name=f"attn_bwd_dq_{dil}")
                dk, dv = run(attn_bwd_dkv, *s["qkv"][dil], dya[dil], s["lt"][dil], delta[dil], bias_k[g], dil, name=f"attn_bwd_dkv_{dil}")
                dqs.append(dq); dks.append(dk); dvs.append(dv); dbs.append(db)
            grads["rel_bias"] = rel_bias_grad(dbs, "rel_bias_grad")
            dz, grads["hyb_conv_w"], dqg, dkg = hyb_dz(s["z"], dy, w["hyb_conv_w"], qg, kg, dqs, dks, dvs, "hyb_dz")
            grads["hyb_q_gain"] = jnp.sum(dqg.reshape(N_HEADS, HEAD_DIM), axis=0, keepdims=True)
            grads["hyb_k_gain"] = jnp.sum(dkg.reshape(N_HEADS, HEAD_DIM), axis=0, keepdims=True)
            nb_, bw = N_SHARD, 768
            w_in, nm_in = w_hyb_in, "hyb_w_in"
        grads[nm_in] = run(tn_mm, s["hnm"], lambda tk: _bs((tk, D_MODEL), lambda k, j: (j, 0)), dz, lambda tk, bw=bw: _bs((tk, bw), lambda k, j: (j, k)),
                           nb_, t, D_MODEL, bw, S((nb_, D_MODEL, bw), BF16), _bs((None, D_MODEL, bw), lambda k, j: (k, 0, 0)),
                           1.0, None, name=f"mix_gw_in_{i}").reshape(nb_, 1, D_MODEL, bw)
        dh, norm_g["mix_norm"][i] = run(
            nt_acc_normbwd, [(dz, _bs((_row_tile(t, BWD_IN_ROWS), nb_ * bw), lambda r: (r, 0)), w_in, _bs((nb_, D_MODEL, bw), lambda r: (0, 0, 0)),
                              lambda x_ref, k, bw=bw: x_ref[:, k * bw:(k + 1) * bw])],
            nb_, s["h1"], _vec(w["mix_norm"], i), dh_out, name=f"mix_bwd_in_{i}")
        dh_out = dh
        dg, du = run(ffn_bwd_act, dh_out, w[f"ffn1_w_down/{i}"], 0, s["g1"], s["u1"], name=f"ffn1_bwd_act_{i}")
        ffn_wgrads("ffn1", s["hn1"], dh_out, s["a1"], dg, du, i, grads, run)
        dh, norm_g["ffn1_norm"][i] = run(ffn_bwd_in, dg, du, w[f"ffn1_w_gate/{i}"], w[f"ffn1_w_up/{i}"], 0, s["h0"], _vec(w["ffn1_norm"], i), dh_out,
                                         name=f"ffn1_bwd_in_{i}")
    for n, (g0, g1) in norm_g.items():
        grads[n] = jnp.concatenate([g0, g1], axis=0)
    return loss, dh, grads


def gather_weights(shards, name):
    n = len(shards)

    def body(*refs):
        ins, outs = refs[:n], refs[n:2 * n]
        send1, recv1, send2, recv2, lsem = refs[2 * n:]
        x, y, c, k, chips, kk = _place()
        sib = (x, y, 1 - c)

        def remote(src, dst, ssem, rsem, to):
            return pltpu.make_async_remote_copy(src_ref=src, dst_ref=dst, send_sem=ssem, recv_sem=rsem, device_id=to, device_id_type=MESH)

        local = [pltpu.make_async_copy(ins[a], outs[a].at[k], lsem.at[a]) for a in range(n)]
        for cp in local:
            cp.start()
        sends = []
        for a in range(n):
            for j, chip in enumerate(chips):
                cp = remote(ins[a].at[c], outs[a].at[k, c], send1.at[3 * a + j], recv1.at[3 * a + j], (*chip, c))
                cp.start()
                sends.append(cp)
        for a in range(n):
            for j, chip in enumerate(chips):
                remote(ins[a].at[c], outs[a].at[kk[j], c], send1.at[3 * a + j], recv1.at[3 * a + j], (*chip, c)).wait_recv()
                cp = remote(outs[a].at[kk[j], c], outs[a].at[kk[j], c], send2.at[3 * a + j], recv2.at[3 * a + j], sib)
                cp.start()
                sends.append(cp)
        for a in range(n):
            for j in range(3):
                remote(outs[a].at[kk[j], 1 - c], outs[a].at[kk[j], 1 - c], send2.at[3 * a + j], recv2.at[3 * a + j], sib).wait_recv()
        for cp in sends:
            cp.wait_send()
        for cp in local:
            cp.wait()

    return pl.pallas_call(
        body, name=name,
        in_specs=[_ANY] * n, out_specs=[_ANY] * n,
        out_shape=[S((N_SHARD,) + s.shape, s.dtype) for s in shards],
        scratch_shapes=[pltpu.SemaphoreType.DMA((3 * n,))] * 4 + [pltpu.SemaphoreType.DMA((n,))],
    )(*shards)


def exchange_cores(rs, name):
    n = len(rs)

    def body(*refs):
        outs = refs[n:2 * n]
        send, recv = refs[2 * n:]
        x, y, c = lax.axis_index("x"), lax.axis_index("y"), lax.axis_index("c")
        sends = []
        for a in range(n):
            for k in range(N_SHARD):
                slot = outs[a].at[2 * k + c]
                cp = _remote(slot, slot, send.at[N_SHARD * a + k], recv.at[N_SHARD * a + k], (x, y, 1 - c))
                cp.start()
                sends.append(cp)
        for a in range(n):
            for k in range(N_SHARD):
                slot = outs[a].at[2 * k + 1 - c]
                _remote(slot, slot, send.at[N_SHARD * a + k], recv.at[N_SHARD * a + k], (x, y, 1 - c)).wait_recv()
        for cp in sends:
            cp.wait_send()

    return pl.pallas_call(
        body, name=name,
        in_specs=[_ANY] * n, out_specs=[_ANY] * n,
        out_shape=[S(r.shape, r.dtype) for r in rs],
        input_output_aliases={a: a for a in range(n)},
        scratch_shapes=[pltpu.SemaphoreType.DMA((N_SHARD * n,))] * 2,
    )(*rs)


def allgather8(a, name):
    def body(a_ref, o_ref, send, recv, lsem):
        x, y, c = lax.axis_index("x"), lax.axis_index("y"), lax.axis_index("c")
        me = 4 * x + 2 * y + c
        local = pltpu.make_async_copy(a_ref, o_ref.at[me], lsem)
        local.start()
        cps = []
        for f in range(1, N_DEV):
            fx, fy, fc = (f >> 2) & 1, (f >> 1) & 1, f & 1
            peer = (1 - x if fx else x, 1 - y if fy else y, 1 - c if fc else c)
            cp = pltpu.make_async_remote_copy(src_ref=a_ref, dst_ref=o_ref.at[me], send_sem=send.at[f - 1], recv_sem=recv.at[f - 1],
                                              device_id=peer, device_id_type=MESH)
            cp.start()
            cps.append((cp, 4 * peer[0] + 2 * peer[1] + peer[2], f))
        for cp, pidx, f in cps:
            pltpu.make_async_remote_copy(src_ref=a_ref, dst_ref=o_ref.at[pidx], send_sem=send.at[f - 1], recv_sem=recv.at[f - 1],
                                         device_id=(x, y, c), device_id_type=MESH).wait_recv()
        for cp, _, _ in cps:
            cp.wait_send()
        local.wait()

    return pl.pallas_call(
        body, name=name, in_specs=[_ANY], out_specs=_ANY,
        out_shape=S((N_DEV,) + a.shape, a.dtype),
        scratch_shapes=[pltpu.SemaphoreType.DMA((N_DEV - 1,)), pltpu.SemaphoreType.DMA((N_DEV - 1,)), pltpu.SemaphoreType.DMA],
    )(a)


def sum8(a, name):
    _, r, c = a.shape

    def body(a_ref, o_ref):
        acc = a_ref[0]
        for j in range(1, N_DEV):
            acc = acc + a_ref[j]
        o_ref[...] = acc

    return pl.pallas_call(
        body, name=name, in_specs=[_bs((N_DEV, r, c), lambda: (0, 0, 0))], out_specs=_bs((r, c), lambda: (0, 0)),
        out_shape=S((r, c), F32),
    )(a)


def adamw(w, m, v, g, name):
    nl, r, c = w.shape
    tr = _row_tile(r, 256)
    summed = g.ndim == 4

    def body(w_ref, m_ref, v_ref, g_ref, go_ref, d_ref, mo_ref, vo_ref):
        if summed:
            gr = g_ref[0].astype(F32)
            for j in range(1, N_DEV):
                gr = gr + g_ref[j].astype(F32)
        else:
            gr = g_ref[...]
        m_new = ADAM_B1 * m_ref[...] + (1.0 - ADAM_B1) * gr
        v_new = ADAM_B2 * v_ref[...] + (1.0 - ADAM_B2) * (gr * gr)
        m_hat = m_new / (1.0 - ADAM_B1 ** ADAM_STEP)
        v_hat = v_new / (1.0 - ADAM_B2 ** ADAM_STEP)
        go_ref[...] = gr
        d_ref[...] = -ADAM_LR * (m_hat / (jnp.sqrt(v_hat) + ADAM_EPS) + ADAM_WD * w_ref[...])
        mo_ref[...] = m_new
        vo_ref[...] = v_new

    row = _bs((None, tr, c), lambda l, i: (l, i, 0))
    gspec = _bs((N_DEV, None, tr, c), lambda l, i: (0, l, i, 0)) if summed else row
    return pl.pallas_call(
        body, name=name, grid=(nl, r // tr),
        in_specs=[row, row, row, gspec], out_specs=[row] * 4, out_shape=[S((nl, r, c), F32)] * 4,
        compiler_params=_cp("parallel", "parallel"),
    )(w, m, v, g)


WEIGHTS = ["rel_bias", "ffn1_norm", "ffn1_w_gate", "ffn1_w_up", "ffn1_w_down", "mix_norm", "hyb_w_in", "hyb_conv_w", "hyb_q_gain",
           "hyb_k_gain", "hyb_w_out", "rec_w_in", "rec_conv_w", "rec_conv_b", "lru_wa", "lru_ba", "lru_wx", "lru_bx", "lru_lambda",
           "rec_w_out", "ffn2_norm", "ffn2_w_gate", "ffn2_w_up", "ffn2_w_down", "ple_norm", "ple_w_gate", "ple_w_proj"]
BIG = ["ffn1_w_gate", "ffn1_w_up", "ffn1_w_down", "hyb_w_in", "hyb_w_out", "rec_w_in", "lru_wa", "lru_wx", "rec_w_out",
       "ffn2_w_gate", "ffn2_w_up", "ffn2_w_down", "ple_w_gate", "ple_w_proj"]
SMALL_SHARDED = ["hyb_conv_w", "rec_conv_w", "rec_conv_b", "lru_ba", "lru_bx", "lru_lambda"]
SMALL = ["rel_bias", "ffn1_norm", "mix_norm", "ffn2_norm", "ple_norm", "hyb_q_gain", "hyb_k_gain"] + SMALL_SHARDED
PACK_W = 1024
PER_LAYER = ["ffn1_w_gate", "ffn1_w_up", "ffn1_w_down", "ffn2_w_gate", "ffn2_w_up", "ffn2_w_down", "ple_w_gate", "ple_w_proj"]
FIRST = ["ffn1_w_gate/0", "ffn1_w_up/0"]
LAST = ["ffn1_w_down"]
GATHER_PLAN = {
    "ffn1_up_0": ["ffn1_w_down/0", "hyb_w_in"],
    "ffn1_down_0": ["hyb_w_out", "ple_w_gate/0", "ple_w_proj/0"],
    "hyb_in": ["ffn2_w_gate/0"],
    "attn_fwd_1": ["ffn2_w_up/0"],
    "attn_fwd_4": ["ffn2_w_down/0"],
    "ffn2_up_0": ["ffn1_w_gate/1", "ffn1_w_up/1"],
    "ffn2_down_0": ["lru_wa", "lru_wx", "rec_w_out"],
    "ffn1_up_1": ["ffn1_w_down/1", "rec_w_in"],
    "rec_in": ["ffn2_w_down/1", "ple_w_gate/1", "ple_w_proj/1"],
    "rec_fwd": ["ffn2_w_gate/1", "ffn2_w_up/1"],
}
SCATTER_PLAN = {
    "ple_gw_proj_1": [("ple_w_gate", 1)],
    "ffn2_bwd_act_1": [("ple_w_proj", 1)],
    "ffn2_bwd_in_1": [("ffn2_w_gate", 1)],
    "rec_bwd": [("ffn2_w_up", 1), ("ffn2_w_down", 1)],
    "mix_bwd_in_1": [("rec_w_in", 0), ("rec_w_out", 0), ("lru_wa", 0), ("lru_wx", 0)],
    "ffn1_bwd_in_1": [("ffn1_w_gate", 1)],
    "ple_bwd_0": [("ffn1_w_up", 1)],
    "ple_gw_proj_0": [("ple_w_gate", 0)],
    "ffn2_bwd_act_0": [("ple_w_proj", 0)],
    "ffn2_gw_0_gate": [("ffn1_w_down", 1)],
    "ffn2_bwd_in_0": [("ffn2_w_gate", 0)],
    "attn_bwd_dq_1": [("ffn2_w_down", 0)],
    "attn_bwd_dkv_1": [("ffn2_w_up", 0)],
    "mix_bwd_in_0": [("hyb_w_in", 0), ("hyb_w_out", 0)],
    "ffn1_gw_0_up": [("ffn1_w_gate", 0)],
    "ffn1_gw_0_down": [("ffn1_w_up", 0)],
    "ffn1_bwd_in_0": [("ffn1_w_down", 0)],
}
FORWARD_PLAN = {
    "ffn1_bwd_in_1": ["rec_w_in", "rec_w_out", "lru_wa", "lru_wx"],
    "ffn2_bwd_in_0": ["ple_w_gate", "ple_w_proj"],
    "mix_bwd_in_0": ["ffn2_w_gate", "ffn2_w_up", "ffn2_w_down"],
    "ffn1_gw_0_up": ["hyb_w_in", "hyb_w_out"],
    "ffn1_bwd_in_0": ["ffn1_w_gate", "ffn1_w_up"],
}


class Plan:
    def __init__(self, shards, w):
        self.shards, self.w, self.grads, self.landed = shards, w, None, {}

    def host(self, kname):
        if kname in GATHER_PLAN:
            h = Host("gather", [self.shards[n] for n in GATHER_PLAN[kname]])
            h.names = GATHER_PLAN[kname]
            return h
        if kname in SCATTER_PLAN or kname in FORWARD_PLAN:
            items = SCATTER_PLAN.get(kname, [])
            fwd = FORWARD_PLAN.get(kname, [])
            h = Host("scatter", [(self.grads[n], lay, self.landed.get(n)) for n, lay in items], [self.landed[n] for n in fwd])
            h.names = [n for n, _ in items] + fwd
            return h
        return None

    def done(self, h):
        for n, o in zip(h.names, h.outs):
            if h.kind == "gather":
                self.w[n] = o
            else:
                self.landed[n] = o


def _halves(a):
    if a.shape[0] == 2:
        return a
    return a.reshape((2, a.shape[1] // 2) + a.shape[2:])


def _pack_rows(arrs, width):
    rows, offs, r0 = [], [], 0
    for a in arrs:
        if a.shape[1] > width:
            a = a.reshape(-1, width)
        rows.append(jnp.pad(a, ((0, 0), (0, width - a.shape[1]))))
        offs.append(r0)
        r0 += a.shape[0]
    pad = (-r0) % 8
    if pad:
        rows.append(jnp.zeros((pad, width), F32))
    return jnp.concatenate(rows, axis=0), offs


def kernel(x, p, rel_bias, ffn1_norm, ffn1_w_gate, ffn1_w_up, ffn1_w_down, mix_norm, hyb_w_in, hyb_conv_w, hyb_q_gain, hyb_k_gain, hyb_w_out, rec_w_in, rec_conv_w, rec_conv_b, lru_wa, lru_ba, lru_wx, lru_bx, lru_lambda, rec_w_out, ffn2_norm, ffn2_w_gate, ffn2_w_up, ffn2_w_down, ple_norm, ple_w_gate, ple_w_proj, loss_target, m_rel_bias, m_ffn1_norm, m_ffn1_w_gate, m_ffn1_w_up, m_ffn1_w_down, m_mix_norm, m_hyb_w_in, m_hyb_conv_w, m_hyb_q_gain, m_hyb_k_gain, m_hyb_w_out, m_rec_w_in, m_rec_conv_w, m_rec_conv_b, m_lru_wa, m_lru_ba, m_lru_wx, m_lru_bx, m_lru_lambda, m_rec_w_out, m_ffn2_norm, m_ffn2_w_gate, m_ffn2_w_up, m_ffn2_w_down, m_ple_norm, m_ple_w_gate, m_ple_w_proj, v_rel_bias, v_ffn1_norm, v_ffn1_w_gate, v_ffn1_w_up, v_ffn1_w_down, v_mix_norm, v_hyb_w_in, v_hyb_conv_w, v_hyb_q_gain, v_hyb_k_gain, v_hyb_w_out, v_rec_w_in, v_rec_conv_w, v_rec_conv_b, v_lru_wa, v_lru_ba, v_lru_wx, v_lru_bx, v_lru_lambda, v_rec_w_out, v_ffn2_norm, v_ffn2_w_gate, v_ffn2_w_up, v_ffn2_w_down, v_ple_norm, v_ple_w_gate, v_ple_w_proj):
    given = dict(locals())
    wts = {n: given[n] for n in WEIGHTS}
    k_chip = 2 * lax.axis_index("x") + lax.axis_index("y")

    shards = {}
    for n in BIG:
        b16 = wts[n].astype(BF16)
        if n in PER_LAYER:
            shards[n + "/0"], shards[n + "/1"] = b16[0:1], b16[1:2]
        else:
            shards[n] = b16
    first = gather_weights([_halves(shards[n]) for n in FIRST], "gather_first")
    w = {n: g.reshape((N_SHARD,) + shards[n].shape) for n, g in zip(FIRST, first)}
    plan = Plan(shards, w)
    sm2d = {n: wts[n].reshape(-1, wts[n].shape[-1]) for n in SMALL_SHARDED}
    slab, offs = _pack_rows([sm2d[n] for n in SMALL_SHARDED], 256)
    slabs = allgather8(slab, "gather_small")[0::2]
    for n, o in zip(SMALL_SHARDED, offs):
        r, cw = sm2d[n].shape
        w[n] = jnp.concatenate([slabs[kc, o:o + r, :cw] for kc in range(N_SHARD)], axis=1)
    for n in SMALL:
        if n not in SMALL_SHARDED:
            w[n] = wts[n]

    loss, dx, grads = local_step(x[0], p[:, 0], loss_target[0], w, plan)
    loss = lax.psum(loss[0, 0], ("x", "y", "c"))

    for n, r8 in zip(LAST, exchange_cores([plan.landed[n] for n in LAST], "exchange_cores")):
        plan.landed[n] = r8
    out = {}
    for n in BIG:
        r8 = plan.landed[n]
        shp = wts[n].shape
        shp3 = shp if len(shp) == 3 else (shp[0], -1, shp[-1])
        three = lambda a: a.reshape(shp3)
        res = adamw(three(wts[n]), three(given["m_" + n]), three(given["v_" + n]), r8.reshape((N_DEV,) + three(wts[n]).shape), "adamw_" + n)
        out[n] = [a.reshape(shp) for a in res]
    g2d = [grads[n].reshape(-1, grads[n].shape[-1]) if n != "rel_bias" else grads[n].reshape(1, -1) for n in SMALL]
    gslab, goffs = _pack_rows(g2d, PACK_W)
    gsum = sum8(allgather8(gslab, "gather_small_grads"), "sum_small_grads")
    for n, o, g in zip(SMALL, goffs, g2d):
        shp = wts[n].shape
        r, cw = g.shape
        gs = gsum[o:o + r, :cw]
        if n in SMALL_SHARDED:
            sw = shp[-1]
            gs = lax.dynamic_slice_in_dim(gs, k_chip * sw, sw, axis=1)
        three = lambda a: a.reshape((1, -1, shp[-1]))
        res = adamw(three(wts[n]), three(given["m_" + n]), three(given["v_" + n]), three(gs), "adamw_" + n)
        out[n] = [a.reshape(shp) for a in res]
    return (loss, dx[None], *[out[n][0] for n in WEIGHTS], *[out[n][1] for n in WEIGHTS],
            *[out[n][2] for n in WEIGHTS], *[out[n][3] for n in WEIGHTS])
```

```python
import functools
import math

import numpy as np
import jax
import jax.numpy as jnp
from jax import lax
from jax.experimental import pallas as pl
from jax.experimental.pallas import tpu as pltpu

F32, BF16 = jnp.float32, jnp.bfloat16
S = jax.ShapeDtypeStruct
MESH = pl.DeviceIdType.MESH

D_MODEL = 1024
N_SHARD = 4
N_DEV = 8
HEAD_DIM = 64
N_HEADS = 8
ATTN_W = N_HEADS * HEAD_DIM
CONV_W = 512
BAND = 128
DILATIONS = (1, 4, 16)
REL_BUCKETS = 32
REL_MAX_DIST = 2048
LRU_BLOCKS = 4
LRU_BLOCK = 256
LRU_C = 8.0
EPS = 1e-6
NEG = -1e30
VMEM_LIMIT = 56 * 1024 * 1024
FFN_ROWS = 1024
FFN_SHARDS_ROWS = 512
TN_ROWS = 2048
BWD_IN_ROWS = 512

ADAM_LR, ADAM_B1, ADAM_B2, ADAM_EPS, ADAM_WD, ADAM_STEP = 0.001, 0.9, 0.999, 1e-08, 0.01, 10


def _cp(*sem):
    return pltpu.CompilerParams(dimension_semantics=sem, vmem_limit_bytes=VMEM_LIMIT)


def _bs(shape, imap):
    return pl.BlockSpec(shape, imap)


def _row_tile(t, want):
    for cand in range(min(want, t) // 8 * 8, 0, -8):
        if t % cand == 0:
            return cand
    return t


_ANY = pl.BlockSpec(memory_space=pl.ANY)


def _place():
    x, y, c = lax.axis_index("x"), lax.axis_index("y"), lax.axis_index("c")
    chips = [(1 - x, y), (x, 1 - y), (1 - x, 1 - y)]
    return x, y, c, 2 * x + y, chips, [2 * cx + cy for cx, cy in chips]


def _remote(src, dst, ssem, rsem, to):
    return pltpu.make_async_remote_copy(src_ref=src, dst_ref=dst, send_sem=ssem, recv_sem=rsem, device_id=to, device_id_type=MESH)


class Host:
    def __init__(self, kind, items, forwards=()):
        self.kind, self.items, self.forwards, self.outs = kind, items, list(forwards), None

    def n_sems(self):
        return 3 * len(self.items) + N_SHARD * len(self.forwards), len(self.items)

    def operands(self):
        if self.kind == "gather":
            return list(self.items), [S((N_SHARD,) + s.shape, s.dtype) for s in self.items], {}
        xin, shapes, alias = [], [], {}
        for a, (g, _, r_prev) in enumerate(self.items):
            xin.append(g)
            if r_prev is not None:
                alias[len(xin)] = a
                xin.append(r_prev)
            shapes.append(S((N_DEV,) + g.shape[1:], g.dtype))
        for f, r in enumerate(self.forwards):
            alias[len(xin)] = len(self.items) + f
            xin.append(r)
            shapes.append(S(r.shape, r.dtype))
        return xin, shapes, alias

    def copies(self, xi, xo, send, recv, lsem):
        x, y, c, k, chips, kk = _place()
        starts, waits = [], []
        pos = 0
        for f in range(len(self.forwards)):
            arr = xo[len(self.items) + f]
            for kq in range(N_SHARD):
                sem = 3 * len(self.items) + N_SHARD * f + kq
                cp = _remote(arr.at[2 * kq + c], arr.at[2 * kq + c], send.at[sem], recv.at[sem], (x, y, 1 - c))
                starts.append((cp, "start"))
                waits.append((cp, "wait_send"))
                other = arr.at[2 * kq + 1 - c]
                waits.append((_remote(other, other, send.at[sem], recv.at[sem], (x, y, 1 - c)), "wait_recv"))
        for a, item in enumerate(self.items):
            if self.kind == "gather":
                src_of = lambda chip_idx, s=xi[a]: s
                dst_of = lambda chip_idx, o=xo[a]: o.at[chip_idx]
                mine, theirs = k, kk
            else:
                g_ref = xi[pos]
                pos += 1 if item[2] is None else 2
                lay = item[1]
                src_of = lambda chip_idx, g=g_ref, lay=lay: g.at[chip_idx, lay]
                dst_of = lambda slot, o=xo[a], lay=lay: o.at[slot, lay]
                mine, theirs = 2 * k + c, [2 * kj + c for kj in kk]
            own_src = src_of(k)
            local = pltpu.make_async_copy(own_src, dst_of(mine), lsem.at[a])
            starts.append((local, "start"))
            waits.append((local, "wait"))
            for j, chip in enumerate(chips):
                src = own_src if self.kind == "gather" else src_of(kk[j])
                cp = _remote(src, dst_of(mine), send.at[3 * a + j], recv.at[3 * a + j], (*chip, c))
                starts.append((cp, "start"))
                waits.append((cp, "wait_send"))
                waits.append((_remote(own_src, dst_of(theirs[j]), send.at[3 * a + j], recv.at[3 * a + j], (*chip, c)), "wait_recv"))
        return starts, waits


def _call(host, body, *, name, grid, in_specs, out_specs, out_shape, scratch_shapes=(), compiler_params=None, args, aliases=None):
    aliases = dict(aliases or {})
    if host is None:
        return pl.pallas_call(body, name=name, grid=grid, in_specs=in_specs, out_specs=out_specs, out_shape=out_shape,
                              scratch_shapes=list(scratch_shapes), input_output_aliases=aliases, compiler_params=compiler_params)(*args)
    single = not isinstance(out_shape, (list, tuple))
    out_specs_l = [out_specs] if single else list(out_specs)
    out_shape_l = [out_shape] if single else list(out_shape)
    n_in, n_out, n_scr = len(in_specs), len(out_shape_l), len(scratch_shapes)
    xin, xshapes, xalias = host.operands()
    n_items = len(xshapes)
    n_rsem, n_lsem = host.n_sems()
    for i_in, i_out in xalias.items():
        aliases[n_in + i_in] = n_out + i_out
    nd = len(grid)

    def hosted(*refs):
        ins, xi = refs[:n_in], refs[n_in:n_in + len(xin)]
        o0 = n_in + len(xin)
        outs, xo = refs[o0:o0 + n_out], refs[o0 + n_out:o0 + n_out + n_items]
        s0 = o0 + n_out + n_items
        scr = refs[s0:s0 + n_scr]
        send, recv, lsem = refs[s0 + n_scr:]
        first = functools.reduce(jnp.logical_and, [pl.program_id(d) == 0 for d in range(nd)])
        last = functools.reduce(jnp.logical_and, [pl.program_id(d) == grid[d] - 1 for d in range(nd)])
        starts, waits = host.copies(xi, xo, send, recv, lsem)

        @pl.when(first)
        def _():
            for cp, how in starts:
                getattr(cp, how)()
        body(*ins, *outs, *scr)

        @pl.when(last)
        def _():
            for cp, how in waits:
                getattr(cp, how)()

    res = pl.pallas_call(
        hosted, name=name, grid=grid,
        in_specs=list(in_specs) + [_ANY] * len(xin),
        out_specs=out_specs_l + [_ANY] * n_items,
        out_shape=out_shape_l + xshapes,
        scratch_shapes=list(scratch_shapes) + [pltpu.SemaphoreType.DMA((n_rsem,)), pltpu.SemaphoreType.DMA((n_rsem,)),
                                               pltpu.SemaphoreType.DMA((max(n_lsem, 1),))],
        input_output_aliases=aliases,
        compiler_params=pltpu.CompilerParams(dimension_semantics=("arbitrary",) * nd, vmem_limit_bytes=VMEM_LIMIT),
    )(*args, *xin)
    host.outs = list(res[n_out:])
    return res[0] if single else list(res[:n_out])


def _rstd(x):
    return lax.rsqrt(jnp.mean(x * x, axis=-1, keepdims=True) + EPS)


def _sigmoid(x):
    return 1.0 / (1.0 + jnp.exp(-x))


def _dot(a, b):
    return jnp.dot(a, b, preferred_element_type=F32)


def _dot_nt(a, b):
    return lax.dot_general(a, b, (((1,), (1,)), ((), ())), preferred_element_type=F32)


def _dot_tn(a, b):
    return lax.dot_general(a, b, (((0,), (0,)), ((), ())), preferred_element_type=F32)


def _seg_dot(x, seg_bf16):
    hi = x.astype(BF16)
    lo = (x - hi.astype(F32)).astype(BF16)
    return _dot(hi, seg_bf16) + _dot(lo, seg_bf16)


def _shift_down(x, prev8, s):
    if s == 0:
        return x
    tm = x.shape[0]
    row = lax.broadcasted_iota(jnp.int32, x.shape, 0)
    main = jnp.where(row >= s, pltpu.roll(x, s, axis=0), 0.0)
    row8 = lax.broadcasted_iota(jnp.int32, prev8.shape, 0)
    head = jnp.where(row8 < s, pltpu.roll(prev8, s, axis=0), 0.0)
    if tm == 8:
        return main + head
    return main + jnp.concatenate([head, jnp.zeros((tm - 8, x.shape[1]), x.dtype)], axis=0)


def _shift_up(x, next8, s):
    if s == 0:
        return x
    tm = x.shape[0]
    row = lax.broadcasted_iota(jnp.int32, x.shape, 0)
    main = jnp.where(row < tm - s, pltpu.roll(x, tm - s, axis=0), 0.0)
    row8 = lax.broadcasted_iota(jnp.int32, next8.shape, 0)
    tail = jnp.where(row8 >= 8 - s, pltpu.roll(next8, 8 - s, axis=0), 0.0)
    if tm == 8:
        return main + tail
    return main + jnp.concatenate([jnp.zeros((tm - 8, x.shape[1]), x.dtype), tail], axis=0)


def _roll_fill(x, s, fill, up):
    tm = x.shape[0]
    row = lax.broadcasted_iota(jnp.int32, x.shape, 0)
    if up:
        return jnp.where(row < tm - s, pltpu.roll(x, tm - s, axis=0), fill)
    return jnp.where(row >= s, pltpu.roll(x, s, axis=0), fill)


def _log1p(y):
    u = 1.0 + y
    return jnp.where(u == 1.0, y, jnp.log(u) * (y / jnp.where(u == 1.0, 1.0, u - 1.0)))


def _softplus(x):
    return jnp.maximum(x, 0.0) + _log1p(jnp.exp(-jnp.abs(x)))


def _neg_expm1(y):
    series = -y * (1.0 + y * (0.5 + y * (1.0 / 6.0 + y * (1.0 / 24.0 + y * (1.0 / 120.0)))))
    return jnp.where(jnp.abs(y) < 0.03, series, 1.0 - jnp.exp(y))


_GELU_C = math.sqrt(2.0 / math.pi)


def _gelu_and_grad(x):
    inner = _GELU_C * (x + 0.044715 * x * x * x)
    t = jnp.tanh(inner)
    g = 0.5 * x * (1.0 + t)
    dg = 0.5 * (1.0 + t) + 0.5 * x * (1.0 - t * t) * _GELU_C * (1.0 + 3.0 * 0.044715 * x * x)
    return g, dg


def _rmsnorm_bwd(x, gain, dy):
    r = _rstd(x)
    xhat = x * r
    dxhat = dy * gain
    dx = r * (dxhat - xhat * jnp.mean(dxhat * xhat, axis=-1, keepdims=True))
    return dx, jnp.sum(dy * xhat, axis=0, keepdims=True)


def ffn_up(h, gain, wg, wu, layer, name, host=None):
    t, d = h.shape
    nk, _, _, f = wg.shape
    tm = _row_tile(t, FFN_SHARDS_ROWS)

    def body(h_ref, g_ref, wg_ref, wu_ref, hn_ref, gg_ref, uu_ref, aa_ref):
        x = h_ref[...]
        hn = (x * _rstd(x) * g_ref[...]).astype(BF16)
        hn_ref[...] = hn
        for k in range(nk):
            g = _dot(hn, wg_ref[k])
            u = _dot(hn, wu_ref[k])
            s = _sigmoid(g)
            silu = g * s
            gg_ref[k] = (u * (s * (1.0 + g * (1.0 - s)))).astype(BF16)
            uu_ref[k] = silu.astype(BF16)
            aa_ref[k] = (silu * u).astype(BF16)

    wspec = _bs((nk, None, d, f), lambda i: (0, layer, 0, 0))
    aspec = _bs((nk, tm, f), lambda i: (0, i, 0))
    return _call(
        host, body, name=name, grid=(t // tm,),
        in_specs=[_bs((tm, d), lambda i: (i, 0)), _bs((1, d), lambda i: (0, 0)), wspec, wspec],
        out_specs=[_bs((tm, d), lambda i: (i, 0)), aspec, aspec, aspec],
        out_shape=[S((t, d), BF16), S((nk, t, f), BF16), S((nk, t, f), BF16), S((nk, t, f), BF16)],
        compiler_params=_cp("parallel"),
        args=(h, gain, wg, wu))


def mm_acc(a, a_spec, b, b_spec, res, scale, nk, t, n, tm, name, host=None):
    def body(a_ref, b_ref, r_ref, o_ref, acc):
        k = pl.program_id(1)

        @pl.when(k == 0)
        def _():
            acc[...] = jnp.zeros_like(acc)
        acc[...] += _dot(a_ref[...].astype(BF16), b_ref[...])

        @pl.when(k == nk - 1)
        def _():
            o_ref[...] = r_ref[...] + scale * acc[...]

    return _call(
        host, body, name=name, grid=(t // tm, nk),
        in_specs=[a_spec, b_spec, _bs((tm, n), lambda i, k: (i, 0))],
        out_specs=_bs((tm, n), lambda i, k: (i, 0)),
        out_shape=S((t, n), F32),
        scratch_shapes=[pltpu.VMEM((tm, n), F32)],
        compiler_params=_cp("parallel", "arbitrary"),
        args=(a, b, res))


def ffn_down(a, wd, layer, h, name, host=None, scale=0.5):
    nk, t, f = a.shape
    d = h.shape[1]
    tm = _row_tile(t, FFN_ROWS)

    def body(a_ref, w_ref, r_ref, o_ref):
        acc = _dot(a_ref[0], w_ref[0])
        for k in range(1, nk):
            acc = acc + _dot(a_ref[k], w_ref[k])
        o_ref[...] = r_ref[...] + scale * acc

    row = _bs((tm, d), lambda i: (i, 0))
    return _call(
        host, body, name=name, grid=(t // tm,),
        in_specs=[_bs((nk, tm, f), lambda i: (0, i, 0)), _bs((nk, None, f, d), lambda i: (0, layer, 0, 0)), row],
        out_specs=row, out_shape=S((t, d), F32),
        compiler_params=_cp("parallel"),
        args=(a, wd, h))


def ffn_bwd_act(dh, wd, layer, gg, uu, name, host=None):
    nk, t, f = gg.shape
    d = dh.shape[1]
    tm = _row_tile(t, FFN_SHARDS_ROWS)

    def body(dh_ref, wd_ref, g_ref, u_ref, dg_ref, du_ref):
        dh16 = dh_ref[...].astype(BF16)
        for k in range(nk):
            da = 0.5 * _dot_nt(dh16, wd_ref[k])
            dg_ref[k] = (da * g_ref[k].astype(F32)).astype(BF16)
            du_ref[k] = (da * u_ref[k].astype(F32)).astype(BF16)

    aspec = _bs((nk, tm, f), lambda i: (0, i, 0))
    return _call(
        host, body, name=name, grid=(t // tm,),
        in_specs=[_bs((tm, d), lambda i: (i, 0)), _bs((nk, None, f, d), lambda i: (0, layer, 0, 0)), aspec, aspec],
        out_specs=[aspec, aspec],
        out_shape=[S((nk, t, f), BF16), S((nk, t, f), BF16)],
        compiler_params=_cp("parallel"),
        args=(dh, wd, gg, uu))


def nt_acc_normbwd(terms, nk, h, gain, dh, name, host=None):
    t, d = h.shape
    tm = _row_tile(t, BWD_IN_ROWS)
    sub = _row_tile(tm, 256)
    nterm = len(terms)
    picks = [term[4] for term in terms]

    def body(*refs):
        xs = refs[:2 * nterm]
        h_ref, g_ref, dh_ref, o_ref, dg_ref, acc = refs[2 * nterm:]

        @pl.when(pl.program_id(0) == 0)
        def _():
            dg_ref[...] = jnp.zeros_like(dg_ref)
        tot = None
        for j in range(nterm):
            for k in range(nk):
                part = _dot_nt(picks[j](xs[2 * j], k), xs[2 * j + 1][k])
                tot = part if tot is None else tot + part
        acc[...] = tot

        def rows_of(cidx, dgain):
            rows = pl.ds(pl.multiple_of(cidx * sub, sub), sub)
            dx, dgc = _rmsnorm_bwd(h_ref[rows, :], g_ref[...], acc[rows, :])
            o_ref[rows, :] = dh_ref[rows, :] + dx
            return dgain + dgc
        dg_ref[...] += lax.fori_loop(0, tm // sub, rows_of, jnp.zeros((1, d), F32))

    in_specs, args = [], []
    for x, xs_, w, ws_, _ in terms:
        in_specs += [xs_, ws_]
        args += [x, w]
    row = _bs((tm, d), lambda i: (i, 0))
    vec = _bs((1, d), lambda i: (0, 0))
    return _call(
        host, body, name=name, grid=(t // tm,),
        in_specs=in_specs + [row, vec, row],
        out_specs=[row, vec],
        out_shape=[S((t, d), F32), S((1, d), F32)],
        scratch_shapes=[pltpu.VMEM((tm, d), F32)],
        compiler_params=_cp("arbitrary"),
        args=(*args, h, gain, dh))


def ffn_bwd_in(dg, du, wg, wu, layer, h, gain, dh, name, host=None):
    nk, t, f = dg.shape
    d = h.shape[1]
    tm = _row_tile(t, BWD_IN_ROWS)
    aspec = _bs((nk, tm, f), lambda i: (0, i, 0))
    wspec = _bs((nk, None, d, f), lambda i: (0, layer, 0, 0))
    pick = lambda x_ref, k: x_ref[k]
    return nt_acc_normbwd([(dg, aspec, wg, wspec, pick), (du, aspec, wu, wspec, pick)], nk, h, gain, dh, name, host)


def tn_mm(x, x_spec, y, y_spec, nblk, t, ka, nb, out_shape, out_spec, scale, prev, name, host=None):
    tk = _row_tile(t, TN_ROWS if F32 in (x.dtype, y.dtype) else 2 * TN_ROWS)

    def body(*refs):
        if prev is None:
            x_ref, y_ref, o_ref, acc = refs
        else:
            x_ref, y_ref, _, o_ref, acc = refs
        j = pl.program_id(1)

        @pl.when(j == 0)
        def _():
            acc[...] = jnp.zeros_like(acc)
        acc[...] += _dot_tn(x_ref[...].astype(BF16), y_ref[...].astype(BF16))

        @pl.when(j == t // tk - 1)
        def _():
            o_ref[...] = (scale * acc[...]).astype(o_ref.dtype)

    in_specs = [x_spec(tk), y_spec(tk)]
    args = [x, y]
    aliases = {}
    if prev is not None:
        in_specs.append(pl.BlockSpec(memory_space=pl.ANY))
        args.append(prev)
        aliases = {2: 0}
    return _call(
        host, body, name=name, grid=(nblk, t // tk),
        in_specs=in_specs, out_specs=out_spec, out_shape=out_shape,
        scratch_shapes=[pltpu.VMEM((ka, nb), F32)],
        aliases=aliases,
        compiler_params=_cp("parallel", "arbitrary"),
        args=tuple(args))


def ffn_wgrads(which, hn, dh, aa, dg, du, layer, grads, run):
    nk, t, f = aa.shape
    d = hn.shape[1]
    hn_spec = lambda tk: _bs((tk, d), lambda k, j: (j, 0))
    a_spec = lambda tk: _bs((None, tk, f), lambda k, j: (k, j, 0))
    shape_gu, spec_gu = S((nk, 2, d, f), BF16), _bs((None, None, d, f), lambda k, j: (k, layer, 0, 0))
    shape_d, spec_d = S((nk, 2, f, d), BF16), _bs((None, None, f, d), lambda k, j: (k, layer, 0, 0))
    for suffix, x, xs, y, ys, ka, nb, shp, spec, scale in (
            ("gate", hn, hn_spec, dg, a_spec, d, f, shape_gu, spec_gu, 1.0),
            ("up", hn, hn_spec, du, a_spec, d, f, shape_gu, spec_gu, 1.0),
            ("down", aa, a_spec, dh, hn_spec, f, d, shape_d, spec_d, 0.5)):
        key = f"{which}_w_{suffix}"
        grads[key] = run(tn_mm, x, xs, y, ys, nk, t, ka, nb, shp, spec, scale, grads.get(key), name=f"{which}_gw_{layer}_{suffix}")


def norm_mm(h, gain, w, name, host=None):
    t, d = h.shape
    nb, _, bw = w.shape
    tm = _row_tile(t, FFN_ROWS)

    def body(h_ref, g_ref, w_ref, hn_ref, z_ref, hn_scr):
        @pl.when(pl.program_id(1) == 0)
        def _():
            x = h_ref[...]
            hn = (x * _rstd(x) * g_ref[...]).astype(BF16)
            hn_scr[...] = hn
            hn_ref[...] = hn
        z_ref[...] = _dot(hn_scr[...], w_ref[...])

    return _call(
        host, body, name=name, grid=(t // tm, nb),
        in_specs=[_bs((tm, d), lambda i, k: (i, 0)), _bs((1, d), lambda i, k: (0, 0)), _bs((None, d, bw), lambda i, k: (k, 0, 0))],
        out_specs=[_bs((tm, d), lambda i, k: (i, 0)), _bs((tm, bw), lambda i, k: (i, k))],
        out_shape=[S((t, d), BF16), S((t, nb * bw), F32)],
        scratch_shapes=[pltpu.VMEM((tm, d), BF16)],
        compiler_params=_cp("parallel", "arbitrary"),
        args=(h, gain, w))


def nt_mm(a, w, name):
    t, k = a.shape
    n = w.shape[0]
    tm = _row_tile(t, 512)

    def body(a_ref, w_ref, o_ref):
        o_ref[...] = _dot_nt(a_ref[...].astype(BF16), w_ref[...])

    return pl.pallas_call(
        body, name=name, grid=(t // tm,),
        in_specs=[_bs((tm, k), lambda i: (i, 0)), _bs((n, k), lambda i: (0, 0))],
        out_specs=_bs((tm, n), lambda i: (i, 0)),
        out_shape=S((t, n), F32),
        compiler_params=_cp("parallel"),
    )(a, w)


def _head_mean_matrix():
    m = np.kron(np.eye(N_HEADS, dtype=np.float32), np.full((HEAD_DIM, HEAD_DIM), 1.0 / HEAD_DIM, np.float32))
    return jnp.asarray(m, BF16)


def _head_sum_matrix():
    m = np.kron(np.eye(N_HEADS, dtype=np.float32), np.ones((HEAD_DIM, HEAD_DIM), np.float32))
    return jnp.asarray(m, BF16)


def _rel_bucket_np(dist):
    max_exact = REL_BUCKETS // 2
    n = np.maximum(dist, 1).astype(np.float32)
    large = max_exact + (np.log(n / np.float32(max_exact)) / np.float32(math.log(REL_MAX_DIST / max_exact))
                         * np.float32(REL_BUCKETS - max_exact)).astype(np.int32)
    large = np.minimum(large, REL_BUCKETS - 1)
    return np.where(dist < max_exact, dist, large)


def _band_tables():
    qi = np.arange(BAND)[:, None]
    kj = np.arange(2 * BAND)[None, :]
    dist_q = qi + BAND - kj
    qq = np.arange(2 * BAND)[:, None]
    kk = np.arange(BAND)[None, :]
    dist_k = qq - kk
    out = []
    for dist in (dist_q, dist_k):
        valid = (dist >= 0) & (dist <= BAND)
        bucket = np.stack([_rel_bucket_np(np.clip(dist, 0, BAND) * d) for d in DILATIONS])
        out.append((bucket, valid))
    return out


def band_bias(rel_bias):
    out = []
    for bucket, valid in _band_tables():
        bucket = np.where(valid[None], bucket, -1)[:, None]
        bucket_dev = jnp.asarray(bucket, jnp.int32)
        tab = jnp.full((len(DILATIONS), N_HEADS) + bucket.shape[2:], NEG, F32)
        for b in range(REL_BUCKETS):
            if (bucket == b).any():
                tab = jnp.where(bucket_dev == b, rel_bias[b][None, :, None, None], tab)
        out.append(tab.reshape(len(DILATIONS), N_HEADS // 2, 2 * tab.shape[2], tab.shape[3]))
    return out


LANE_TILE = 128
N_LANE_TILES = ATTN_W // LANE_TILE


def _view_shape(t, dil):
    return (t // dil, dil * ATTN_W)


def _view_spec(tm, dil):
    return _bs((tm // dil, dil * ATTN_W), lambda i: (i, 0))


def _cols_to(scr, val):
    for cc in range(N_LANE_TILES):
        scr[cc] = val[:, LANE_TILE * cc:LANE_TILE * (cc + 1)]


def _cols_from(scr):
    return jnp.concatenate([scr[cc] for cc in range(N_LANE_TILES)], axis=1)


def _write_view(scr, out_ref, dil):
    if dil == 1:
        out_ref[...] = _cols_from(scr).astype(out_ref.dtype)
        return
    rows = scr.shape[1] // dil
    for r in range(dil):
        for cc in range(N_LANE_TILES):
            c0 = r * ATTN_W + LANE_TILE * cc
            out_ref[:, c0:c0 + LANE_TILE] = scr[cc, pl.ds(r, rows, stride=dil), :].astype(out_ref.dtype)


def _read_view(scr, in_ref, dil):
    if dil == 1:
        return in_ref[...].astype(F32)
    rows = scr.shape[1] // dil
    for r in range(dil):
        for cc in range(N_LANE_TILES):
            c0 = r * ATTN_W + LANE_TILE * cc
            scr[cc, pl.ds(r, rows, stride=dil), :] = in_ref[:, c0:c0 + LANE_TILE].astype(F32)
    return _cols_from(scr)


def hyb_prep(z, q_gain, k_gain, name):
    t = z.shape[0]
    tm = _row_tile(t, 512)
    seg = _head_mean_matrix()
    nd = len(DILATIONS)

    def body(q_ref, k_ref, v_ref, qg_ref, kg_ref, seg_ref, *rest):
        outs, scr = rest[:3 * nd], rest[3 * nd]
        q = q_ref[...]
        k = k_ref[...]
        vals = (q * lax.rsqrt(_seg_dot(q * q, seg_ref[...]) + EPS) * qg_ref[...],
                k * lax.rsqrt(_seg_dot(k * k, seg_ref[...]) + EPS) * kg_ref[...],
                v_ref[...])
        for j, val in enumerate(vals):
            _cols_to(scr, val)
            for g, dil in enumerate(DILATIONS):
                _write_view(scr, outs[3 * g + j], dil)

    col = lambda c: _bs((tm, ATTN_W), lambda i: (i, c))
    vec = _bs((1, ATTN_W), lambda i: (0, 0))
    res = pl.pallas_call(
        body, name=name, grid=(t // tm,),
        in_specs=[col(3), col(4), col(5), vec, vec, _bs((ATTN_W, ATTN_W), lambda i: (0, 0))],
        out_specs=[_view_spec(tm, dil) for dil in DILATIONS for _ in range(3)],
        out_shape=[S(_view_shape(t, dil), BF16) for dil in DILATIONS for _ in range(3)],
        scratch_shapes=[pltpu.VMEM((N_LANE_TILES, tm, LANE_TILE), F32)],
        compiler_params=_cp("parallel"),
    )(z, z, z, q_gain, k_gain, seg)
    return {dil: tuple(res[3 * g:3 * g + 3]) for g, dil in enumerate(DILATIONS)}


def _lane_lo(shape):
    return lax.broadcasted_iota(jnp.int32, shape, 1) < HEAD_DIM


def _stack_heads(pair):
    lo = _lane_lo(pair.shape)
    zero = jnp.zeros_like(pair)
    return jnp.concatenate([jnp.where(lo, pair, zero), jnp.where(lo, zero, pair)], axis=0)


def _unstack_heads(st):
    rows = st.shape[0] // 2
    return jnp.where(_lane_lo((rows, st.shape[1])), st[:rows], st[rows:])


def attn_fwd(q, k, v, bias, dil, name, host=None):
    qv, kv, vv = q, k, v
    sub = q.shape[0]
    nb = sub // BAND

    def body(q_ref, kp_ref, kc_ref, vp_ref, vc_ref, b_ref, o_ref, l_ref):
        first = pl.program_id(1) == 0
        colk = lax.broadcasted_iota(jnp.int32, (2 * BAND, 2 * BAND), 1)
        for j in range(N_HEADS // 2):
            sl = slice(2 * HEAD_DIM * j, 2 * HEAD_DIM * (j + 1))
            kk = jnp.concatenate([kp_ref[:, sl], kc_ref[:, sl]], axis=0)
            vv_ = jnp.concatenate([vp_ref[:, sl], vc_ref[:, sl]], axis=0)
            s = _dot_nt(_stack_heads(q_ref[:, sl]), kk) * (HEAD_DIM ** -0.5) + b_ref[j]
            s = jnp.where(jnp.logical_and(first, colk < BAND), NEG, s)
            m = jnp.max(s, axis=-1, keepdims=True)
            p = jnp.exp(s - m)
            l = jnp.sum(p, axis=-1, keepdims=True)
            o_ref[:, sl] = _unstack_heads(_dot(p.astype(BF16), vv_) / l).astype(o_ref.dtype)
            l_ref[:, sl] = _unstack_heads(jnp.broadcast_to(m + jnp.log(l), (2 * BAND, 2 * HEAD_DIM)))

    cur = _bs((BAND, ATTN_W), lambda r, n: (n, r))
    prv = _bs((BAND, ATTN_W), lambda r, n: (jnp.maximum(n - 1, 0), r))
    return _call(
        host, body, name=name, grid=(dil, nb),
        in_specs=[cur, prv, cur, prv, cur, _bs((N_HEADS // 2, 2 * BAND, 2 * BAND), lambda r, n: (0, 0, 0))],
        out_specs=[cur, cur],
        out_shape=[S((sub, dil * ATTN_W), BF16), S((sub, dil * ATTN_W), F32)],
        compiler_params=_cp("parallel", "arbitrary"),
        args=(qv, kv, kv, vv, vv, bias))


def hyb_post(z, conv_w, os_, lses, name):
    t = z.shape[0]
    tm = _row_tile(t, 512)
    nd = len(DILATIONS)

    def body(gb_ref, gc_ref, cx_ref, gch_ref, cxh_ref, w_ref, *rest):
        o_refs, l_refs = rest[:nd], rest[nd:2 * nd]
        y_ref, ya_ref = rest[2 * nd:2 * nd + 2]
        lt_refs, scr = rest[2 * nd + 2:3 * nd + 2], rest[3 * nd + 2]
        i = pl.program_id(0)
        m = gc_ref[...] * cx_ref[...]
        mh = jnp.where(i == 0, 0.0, gch_ref[...] * cxh_ref[...])
        conv = w_ref[0:1, :] * _shift_down(m, mh, 2) + w_ref[1:2, :] * _shift_down(m, mh, 1) + w_ref[2:3, :] * m
        y_ref[0] = (gb_ref[...] * conv).astype(BF16)
        ls = [_read_view(scr, l_refs[g], dil) for g, dil in enumerate(DILATIONS)]
        mx = functools.reduce(jnp.maximum, ls)
        es = [jnp.exp(l - mx) for l in ls]
        den = functools.reduce(lambda a, b: a + b, es)
        num = es[0] * _read_view(scr, o_refs[0], DILATIONS[0])
        for g in range(1, nd):
            num = num + es[g] * _read_view(scr, o_refs[g], DILATIONS[g])
        ya = num / den
        y_ref[1] = ya.astype(BF16)
        ya_ref[...] = ya
        _cols_to(scr, mx + jnp.log(den))
        for g, dil in enumerate(DILATIONS):
            _write_view(scr, lt_refs[g], dil)

    hb = tm // 8
    col = lambda c: _bs((tm, CONV_W), lambda i: (i, c))
    halo = lambda c: _bs((8, CONV_W), lambda i: (jnp.maximum(i * hb - 1, 0), c))
    row = _bs((tm, ATTN_W), lambda i: (i, 0))
    views = [_view_spec(tm, dil) for dil in DILATIONS]
    res = pl.pallas_call(
        body, name=name, grid=(t // tm,),
        in_specs=[col(0), col(1), col(2), halo(1), halo(2), _bs((3, CONV_W), lambda i: (0, 0))] + views * 2,
        out_specs=[_bs((2, tm, ATTN_W), lambda i: (0, i, 0)), row] + views,
        out_shape=[S((2, t, ATTN_W), BF16), S((t, ATTN_W), F32)] + [S(_view_shape(t, dil), F32) for dil in DILATIONS],
        scratch_shapes=[pltpu.VMEM((N_LANE_TILES, tm, LANE_TILE), F32)],
        compiler_params=_cp("parallel"),
    )(z, z, z, z, z, conv_w, *os_, *lses)
    return res[0], res[1], dict(zip(DILATIONS, res[2:]))


def attn_delta(dy, ya, name):
    t = ya.shape[0]
    tm = _row_tile(t, 512)
    seg = _head_sum_matrix()
    nd = len(DILATIONS)

    def body(dy_ref, ya_ref, seg_ref, *rest):
        dl_refs, db_refs, scr = rest[:nd], rest[nd:2 * nd], rest[2 * nd]
        dya = dy_ref[...]
        _cols_to(scr, _seg_dot(dya * ya_ref[...], seg_ref[...]))
        for g, dil in enumerate(DILATIONS):
            _write_view(scr, dl_refs[g], dil)
        _cols_to(scr, dya)
        for g, dil in enumerate(DILATIONS):
            _write_view(scr, db_refs[g], dil)

    row = _bs((tm, ATTN_W), lambda i: (i, 0))
    views = [_view_spec(tm, dil) for dil in DILATIONS]
    res = pl.pallas_call(
        body, name=name, grid=(t // tm,),
        in_specs=[_bs((tm, ATTN_W), lambda i: (i, 1)), row, _bs((ATTN_W, ATTN_W), lambda i: (0, 0))],
        out_specs=views * 2,
        out_shape=[S(_view_shape(t, dil), F32) for dil in DILATIONS] + [S(_view_shape(t, dil), BF16) for dil in DILATIONS],
        scratch_shapes=[pltpu.VMEM((N_LANE_TILES, tm, LANE_TILE), F32)],
        compiler_params=_cp("parallel"),
    )(dy, ya, seg)
    return dict(zip(DILATIONS, res[:nd])), dict(zip(DILATIONS, res[nd:]))


def attn_bwd_dq(q, k, v, dya, lt, delta, bias, dil, name, host=None):
    qv, kv, vv, dv_, lv, ev = q, k, v, dya, lt, delta
    sub = q.shape[0]
    nb = sub // BAND

    def body(q_ref, kp_ref, kc_ref, vp_ref, vc_ref, do_ref, l_ref, e_ref, b_ref, dq_ref, db_ref):
        r, n = pl.program_id(0), pl.program_id(1)

        @pl.when(jnp.logical_and(r == 0, n == 0))
        def _():
            db_ref[...] = jnp.zeros_like(db_ref)
        first = n == 0
        colk = lax.broadcasted_iota(jnp.int32, (2 * BAND, 2 * BAND), 1)
        for j in range(N_HEADS // 2):
            c0 = 2 * HEAD_DIM * j
            sl = slice(c0, c0 + 2 * HEAD_DIM)
            kk = jnp.concatenate([kp_ref[:, sl], kc_ref[:, sl]], axis=0)
            vv_ = jnp.concatenate([vp_ref[:, sl], vc_ref[:, sl]], axis=0)
            lse = jnp.concatenate([l_ref[:, c0:c0 + 1], l_ref[:, c0 + HEAD_DIM:c0 + HEAD_DIM + 1]], axis=0)
            dlt = jnp.concatenate([e_ref[:, c0:c0 + 1], e_ref[:, c0 + HEAD_DIM:c0 + HEAD_DIM + 1]], axis=0)
            s = _dot_nt(_stack_heads(q_ref[:, sl]), kk) * (HEAD_DIM ** -0.5) + b_ref[j]
            s = jnp.where(jnp.logical_and(first, colk < BAND), NEG, s)
            p = jnp.exp(s - lse)
            ds = p * (_dot_nt(_stack_heads(do_ref[:, sl]), vv_) - dlt)
            db_ref[j] += ds
            dq_ref[:, sl] = (_unstack_heads(_dot(ds.astype(BF16), kk)) * (HEAD_DIM ** -0.5)).astype(dq_ref.dtype)

    cur = _bs((BAND, ATTN_W), lambda r, n: (n, r))
    prv = _bs((BAND, ATTN_W), lambda r, n: (jnp.maximum(n - 1, 0), r))
    tab = _bs((N_HEADS // 2, 2 * BAND, 2 * BAND), lambda r, n: (0, 0, 0))
    dq, db = _call(
        host, body, name=name, grid=(dil, nb),
        in_specs=[cur, prv, cur, prv, cur, cur, cur, cur, tab],
        out_specs=[cur, tab],
        out_shape=[S((sub, dil * ATTN_W), BF16), S((N_HEADS // 2, 2 * BAND, 2 * BAND), F32)],
        compiler_params=_cp("arbitrary", "arbitrary"),
        args=(qv, kv, kv, vv, vv, dv_, lv, ev, bias))
    return dq, db.reshape(N_HEADS, BAND, 2 * BAND)


def attn_bwd_dkv(q, k, v, dya, lt, delta, bias_k, dil, name, host=None):
    qv, kv, vv, dv_, lv, ev = q, k, v, dya, lt, delta
    sub = q.shape[0]
    nb = sub // BAND

    def body(k_ref, v_ref, qc_ref, qn_ref, dc_ref, dn_ref, lc_ref, ln_ref, ec_ref, en_ref, b_ref, dk_ref, dv_ref):
        last = pl.program_id(1) == nb - 1
        rowq = lax.broadcasted_iota(jnp.int32, (4 * BAND, BAND), 0)
        from_next = (rowq & BAND) != 0
        for j in range(N_HEADS // 2):
            c0 = 2 * HEAD_DIM * j
            sl = slice(c0, c0 + 2 * HEAD_DIM)
            kp, vp = k_ref[:, sl], v_ref[:, sl]
            q4 = _stack_heads(jnp.concatenate([qc_ref[:, sl], qn_ref[:, sl]], axis=0))
            do4 = _stack_heads(jnp.concatenate([dc_ref[:, sl], dn_ref[:, sl]], axis=0))
            lse = jnp.concatenate([ref[:, c:c + 1] for c in (c0, c0 + HEAD_DIM) for ref in (lc_ref, ln_ref)], axis=0)
            dlt = jnp.concatenate([ref[:, c:c + 1] for c in (c0, c0 + HEAD_DIM) for ref in (ec_ref, en_ref)], axis=0)
            s = _dot_nt(q4, kp) * (HEAD_DIM ** -0.5) + b_ref[j]
            s = jnp.where(jnp.logical_and(last, from_next), NEG, s)
            p = jnp.exp(s - lse)
            ds = p * (_dot_nt(do4, vp) - dlt)
            dv_ref[:, sl] = _dot_tn(p.astype(BF16), do4).astype(dv_ref.dtype)
            dk_ref[:, sl] = (_dot_tn(ds.astype(BF16), q4) * (HEAD_DIM ** -0.5)).astype(dk_ref.dtype)

    cur = _bs((BAND, ATTN_W), lambda r, n: (n, r))
    nxt = _bs((BAND, ATTN_W), lambda r, n: (jnp.minimum(n + 1, nb - 1), r))
    tab = _bs((N_HEADS // 2, 4 * BAND, BAND), lambda r, n: (0, 0, 0))
    return _call(
        host, body, name=name, grid=(dil, nb),
        in_specs=[cur, cur, cur, nxt, cur, nxt, cur, nxt, cur, nxt, tab],
        out_specs=[cur, cur],
        out_shape=[S((sub, dil * ATTN_W), BF16)] * 2,
        compiler_params=_cp("parallel", "arbitrary"),
        args=(kv, vv, qv, qv, dv_, dv_, lv, lv, ev, ev, bias_k))


def hyb_dz(z, dy, conv_w, q_gain, k_gain, dqs, dks, dvs, name):
    t = z.shape[0]
    tm = _row_tile(t, 512)
    nt = t // tm
    seg = _head_mean_matrix()

    def body(gb_ref, gc_ref, cx_ref, q_ref, k_ref, gch_ref, cxh_ref, gbn_ref, dyc_ref, dyn_ref, w_ref, qg_ref, kg_ref, seg_ref,
             dq1, dq2, dq3, dk1, dk2, dk3, dv1, dv2, dv3, dz_ref, dw_ref, dqg_ref, dkg_ref, scr):
        i = pl.program_id(0)

        def total(parts):
            acc = _read_view(scr, parts[0], DILATIONS[0])
            for g in range(1, len(DILATIONS)):
                acc = acc + _read_view(scr, parts[g], DILATIONS[g])
            return acc

        @pl.when(i == 0)
        def _():
            dw_ref[...] = jnp.zeros_like(dw_ref)
            dqg_ref[...] = jnp.zeros_like(dqg_ref)
            dkg_ref[...] = jnp.zeros_like(dkg_ref)
        gb, gc, cx, dyc = gb_ref[...], gc_ref[...], cx_ref[...], dyc_ref[...]
        m = gc * cx
        mh = jnp.where(i == 0, 0.0, gch_ref[...] * cxh_ref[...])
        m1, m2 = _shift_down(m, mh, 1), _shift_down(m, mh, 2)
        conv = w_ref[0:1, :] * m2 + w_ref[1:2, :] * m1 + w_ref[2:3, :] * m
        dconv = dyc * gb
        dcn = jnp.where(i == nt - 1, 0.0, dyn_ref[...] * gbn_ref[...])
        dm = w_ref[2:3, :] * dconv + w_ref[1:2, :] * _shift_up(dconv, dcn, 1) + w_ref[0:1, :] * _shift_up(dconv, dcn, 2)
        dz_ref[:, 0:CONV_W] = (dyc * conv).astype(BF16)
        dz_ref[:, CONV_W:2 * CONV_W] = (dm * cx).astype(BF16)
        dz_ref[:, 2 * CONV_W:3 * CONV_W] = (dm * gc).astype(BF16)
        dw_ref[0:1, :] += jnp.sum(dconv * m2, axis=0, keepdims=True)
        dw_ref[1:2, :] += jnp.sum(dconv * m1, axis=0, keepdims=True)
        dw_ref[2:3, :] += jnp.sum(dconv * m, axis=0, keepdims=True)
        base = 3 * CONV_W
        for idx, (x_ref, g_ref, parts, dgain_ref) in enumerate(((q_ref, qg_ref, (dq1, dq2, dq3), dqg_ref),
                                                                  (k_ref, kg_ref, (dk1, dk2, dk3), dkg_ref))):
            x = x_ref[...]
            dxh = total(parts)
            r = lax.rsqrt(_seg_dot(x * x, seg_ref[...]) + EPS)
            xhat = x * r
            tt = dxh * g_ref[...]
            dx = r * (tt - xhat * _seg_dot(tt * xhat, seg_ref[...]))
            dz_ref[:, base + idx * ATTN_W:base + (idx + 1) * ATTN_W] = dx.astype(BF16)
            dgain_ref[...] += jnp.sum(dxh * xhat, axis=0, keepdims=True)
        dz_ref[:, base + 2 * ATTN_W:base + 3 * ATTN_W] = total((dv1, dv2, dv3)).astype(BF16)

    hb = tm // 8
    col = lambda c: _bs((tm, CONV_W), lambda i: (i, c))
    prev = lambda c: _bs((8, CONV_W), lambda i: (jnp.maximum(i * hb - 1, 0), c))
    nxt = lambda c: _bs((8, CONV_W), lambda i: (jnp.minimum((i + 1) * hb, t // 8 - 1), c))
    row = _bs((tm, ATTN_W), lambda i: (i, 0))
    vec = _bs((1, ATTN_W), lambda i: (0, 0))
    return pl.pallas_call(
        body, name=name, grid=(nt,),
        in_specs=[col(0), col(1), col(2), col(3), col(4), prev(1), prev(2), nxt(0), col(0), nxt(0),
                  _bs((3, CONV_W), lambda i: (0, 0)), vec, vec, _bs((ATTN_W, ATTN_W), lambda i: (0, 0))]
                 + [_view_spec(tm, dil) for dil in DILATIONS] * 3,
        out_specs=[_bs((tm, 6 * CONV_W), lambda i: (i, 0)), _bs((3, CONV_W), lambda i: (0, 0)), vec, vec],
        out_shape=[S((t, 6 * CONV_W), BF16), S((3, CONV_W), F32), S((1, ATTN_W), F32), S((1, ATTN_W), F32)],
        scratch_shapes=[pltpu.VMEM((N_LANE_TILES, tm, LANE_TILE), F32)],
        compiler_params=_cp("arbitrary"),
    )(z, z, z, z, z, z, z, z, dy, dy, conv_w, q_gain, k_gain, seg, *dqs, *dks, *dvs)


def rel_bias_grad(dbs, name):
    (bq, vq), _ = _band_tables()
    onehot = np.zeros((len(DILATIONS), REL_BUCKETS, BAND * 2 * BAND), np.float32)
    for g in range(len(DILATIONS)):
        idx = bq[g].reshape(-1)
        ok = vq.reshape(-1)
        onehot[g, idx[ok], np.nonzero(ok)[0]] = 1.0
    onehot = jnp.asarray(onehot, BF16)
    flat = [d.reshape(N_HEADS, BAND * 2 * BAND) for d in dbs]

    def body(oh_ref, d1, d2, d3, o_ref):
        acc = jnp.zeros((REL_BUCKETS, N_HEADS), F32)
        for g, d in enumerate((d1, d2, d3)):
            x = d[...]
            hi = x.astype(BF16)
            lo = (x - hi.astype(F32)).astype(BF16)
            acc += _dot_nt(oh_ref[g], hi) + _dot_nt(oh_ref[g], lo)
        o_ref[...] = acc

    full = lambda shp: _bs(shp, lambda: tuple(0 for _ in shp))
    return pl.pallas_call(
        body, name=name,
        in_specs=[full(onehot.shape)] + [full(flat[0].shape)] * 3,
        out_specs=full((REL_BUCKETS, N_HEADS)),
        out_shape=S((REL_BUCKETS, N_HEADS), F32),
        compiler_params=pltpu.CompilerParams(vmem_limit_bytes=VMEM_LIMIT),
    )(onehot, *flat)


def _lru_gates(xb, wa_ref, wx_ref, ba, bx):
    xb16 = xb.astype(BF16)
    ga = jnp.concatenate([_dot(xb16[:, LRU_BLOCK * g:LRU_BLOCK * (g + 1)], wa_ref[g]) for g in range(LRU_BLOCKS)], axis=1) + ba
    gx = jnp.concatenate([_dot(xb16[:, LRU_BLOCK * g:LRU_BLOCK * (g + 1)], wx_ref[g]) for g in range(LRU_BLOCKS)], axis=1) + bx
    return ga, gx


def _lru_coeffs(ga, gx, lam):
    sga = _sigmoid(ga)
    sp = _softplus(-lam)
    log_a = -LRU_C * sga * sp
    a = jnp.exp(log_a)
    one_m_a2 = _neg_expm1(2.0 * log_a)
    return sga, sp, a, one_m_a2, jnp.sqrt(one_m_a2), _sigmoid(gx)


def rec_fwd(z, conv_w, conv_b, wa, wx, ba, bx, lam, name, host=None):
    t = z.shape[0]
    w = z.shape[1] // 2
    tm = _row_tile(t, 256)

    def body(xp_ref, xh_ref, yb_ref, cw_ref, cb_ref, wa_ref, wx_ref, ba_ref, bx_ref, lam_ref,
             xb_ref, ga_ref, gx_ref, hs_ref, out_ref, carry):
        i = pl.program_id(0)

        @pl.when(i == 0)
        def _():
            carry[...] = jnp.zeros_like(carry)
        xp = xp_ref[...]
        xh = jnp.where(i == 0, 0.0, xh_ref[...])
        xb = cb_ref[...] + cw_ref[3:4, :] * xp
        for j in range(3):
            xb = xb + cw_ref[j:j + 1, :] * _shift_down(xp, xh, 3 - j)
        ga, gx = _lru_gates(xb, wa_ref, wx_ref, ba_ref[...], bx_ref[...])
        _, _, a, _, sq, sgx = _lru_coeffs(ga, gx, lam_ref[...])
        aa, bb = a, sq * sgx * xb
        s = 1
        while s < tm:
            bb = aa * _roll_fill(bb, s, 0.0, False) + bb
            aa = aa * _roll_fill(aa, s, 1.0, False)
            s *= 2
        hs = aa * carry[0:1, :] + bb
        xb_ref[...] = xb
        ga_ref[...] = ga
        gx_ref[...] = gx
        hs_ref[...] = hs
        carry[0:1, :] = hs_ref[tm - 1:tm, :]
        gy, _ = _gelu_and_grad(yb_ref[...])
        out_ref[...] = (hs * gy).astype(BF16)

    hb = tm // 8
    row = _bs((tm, w), lambda i: (i, 0))
    vec = _bs((1, w), lambda i: (0, 0))
    wsp = _bs((LRU_BLOCKS, LRU_BLOCK, LRU_BLOCK), lambda i: (0, 0, 0))
    return _call(
        host, body, name=name, grid=(t // tm,),
        in_specs=[row, _bs((8, w), lambda i: (jnp.maximum(i * hb - 1, 0), 0)), _bs((tm, w), lambda i: (i, 1)),
                  _bs((4, w), lambda i: (0, 0)), vec, wsp, wsp, vec, vec, vec],
        out_specs=[row] * 5,
        out_shape=[S((t, w), F32)] * 4 + [S((t, w), BF16)],
        scratch_shapes=[pltpu.VMEM((8, w), F32)],
        compiler_params=_cp("arbitrary"),
        args=(z, z, z, conv_w, conv_b, wa, wx, ba, bx, lam))


def rec_bwd(d_out, z, xb, ga, gx, hs, conv_w, wa, wx, lam, name, host=None):
    t = z.shape[0]
    w = z.shape[1] // 2
    tm = _row_tile(t, 256)
    nt = t // tm

    def body(do_ref, xp_ref, xph_ref, yb_ref, xb_ref, ga_ref, gx_ref, hs_ref, hsh_ref, cw_ref, wa_ref, wx_ref, lam_ref,
             dz_ref, dga_ref, dgx_ref, sm_ref, c_lam, c_a, c_dxb):
        i = pl.program_id(0)

        @pl.when(i == 0)
        def _():
            sm_ref[...] = jnp.zeros_like(sm_ref)
            c_lam[...] = jnp.zeros_like(c_lam)
            c_a[...] = jnp.zeros_like(c_a)
            c_dxb[...] = jnp.zeros_like(c_dxb)
        d_o, yb, xb, hs = do_ref[...], yb_ref[...], xb_ref[...], hs_ref[...]
        lam = lam_ref[...]
        gy, dgy = _gelu_and_grad(yb)
        dz_ref[:, w:2 * w] = (d_o * hs * dgy).astype(BF16)
        sga, sp, a, one_m_a2, sq, sgx = _lru_coeffs(ga_ref[...], gx_ref[...], lam)
        aa = _shift_up(a, c_a[...], 1)
        bb = d_o * gy
        s = 1
        while s < tm:
            bb = aa * _roll_fill(bb, s, 0.0, True) + bb
            aa = aa * _roll_fill(aa, s, 1.0, True)
            s *= 2
        lmb = aa * c_lam[0:1, :] + bb
        c_a[...] = a[0:8, :]
        c_lam[...] = lmb[0:8, :]
        hprev = _shift_down(hs, jnp.where(i == nt - 1, 0.0, hsh_ref[...]), 1)
        d_sq = lmb * sgx * xb
        d_sgx = lmb * sq * xb
        d_log_a = lmb * hprev * a - d_sq * (1.0 - one_m_a2) / sq
        dga = d_log_a * (-LRU_C * sp) * sga * (1.0 - sga)
        dgx = d_sgx * sgx * (1.0 - sgx)
        dga16, dgx16 = dga.astype(BF16), dgx.astype(BF16)
        dga_ref[...] = dga16
        dgx_ref[...] = dgx16
        dxb = lmb * sq * sgx + jnp.concatenate(
            [_dot_nt(dga16[:, LRU_BLOCK * g:LRU_BLOCK * (g + 1)], wa_ref[g]) + _dot_nt(dgx16[:, LRU_BLOCK * g:LRU_BLOCK * (g + 1)], wx_ref[g])
             for g in range(LRU_BLOCKS)], axis=1)
        nxt = c_dxb[...]
        dxp = cw_ref[3:4, :] * dxb
        for j in range(3):
            dxp = dxp + cw_ref[j:j + 1, :] * _shift_up(dxb, nxt, 3 - j)
        c_dxb[...] = dxb[0:8, :]
        dz_ref[:, 0:w] = dxp.astype(BF16)
        xp = xp_ref[...]
        xph = jnp.where(i == nt - 1, 0.0, xph_ref[...])
        sm_ref[0:1, :] += jnp.sum(dga, axis=0, keepdims=True)
        sm_ref[1:2, :] += jnp.sum(dgx, axis=0, keepdims=True)
        sm_ref[2:3, :] += jnp.sum(d_log_a * (-LRU_C * sga), axis=0, keepdims=True) * (-_sigmoid(-lam))
        sm_ref[3:4, :] += jnp.sum(dxb, axis=0, keepdims=True)
        for j in range(4):
            sm_ref[4 + j:5 + j, :] += jnp.sum(dxb * _shift_down(xp, xph, 3 - j), axis=0, keepdims=True)

    hb = tm // 8
    rev = lambda c: _bs((tm, w), lambda i: (nt - 1 - i, c))
    halo = lambda c: _bs((8, w), lambda i: (jnp.maximum((nt - 1 - i) * hb - 1, 0), c))
    vec = _bs((1, w), lambda i: (0, 0))
    wsp = _bs((LRU_BLOCKS, LRU_BLOCK, LRU_BLOCK), lambda i: (0, 0, 0))
    return _call(
        host, body, name=name, grid=(nt,),
        in_specs=[rev(0), rev(0), halo(0), rev(1), rev(0), rev(0), rev(0), rev(0), halo(0),
                  _bs((4, w), lambda i: (0, 0)), wsp, wsp, vec],
        out_specs=[_bs((tm, 2 * w), lambda i: (nt - 1 - i, 0)), rev(0), rev(0), _bs((8, w), lambda i: (0, 0))],
        out_shape=[S((t, 2 * w), BF16), S((t, w), BF16), S((t, w), BF16), S((8, w), F32)],
        scratch_shapes=[pltpu.VMEM((8, w), F32)] * 3,
        compiler_params=_cp("arbitrary"),
        args=(d_out, z, z, z, xb, ga, gx, hs, hs, conv_w, wa, wx, lam))


def ple_fwd(h, gain, wpg, layer, p, wpp, name, target=None):
    t, d = h.shape
    pd = p.shape[1]
    nk, _, rb, _ = wpg.shape
    cb = wpp.shape[3]
    tm = _row_tile(t, 512)
    row = _bs((tm, d), lambda i: (i, 0))
    in_specs = [row, _bs((1, d), lambda i: (0, 0)), _bs((nk, None, rb, d), lambda i: (0, layer, 0, 0)),
                _bs((tm, pd), lambda i: (i, 0)), _bs((nk, None, pd, cb), lambda i: (0, layer, 0, 0))]

    def forward(h_ref, g_ref, wg_ref, p_ref, wp_ref, hn_ref, gp_ref, pp_ref):
        x = h_ref[...]
        hn = (x * _rstd(x) * g_ref[...]).astype(BF16)
        gp = _dot(hn[:, 0:rb], wg_ref[0])
        for k in range(1, nk):
            gp = gp + _dot(hn[:, rb * k:rb * (k + 1)], wg_ref[k])
        pp = jnp.concatenate([_dot(p_ref[...].astype(BF16), wp_ref[k]) for k in range(nk)], axis=1)
        hn_ref[...] = hn
        gp_ref[...] = gp
        pp_ref[...] = pp
        return x + _sigmoid(gp) * pp

    if target is not None:
        def body_loss(h_ref, g_ref, wg_ref, p_ref, wp_ref, t_ref, l_ref, dy_ref, hn_ref, gp_ref, pp_ref):
            @pl.when(pl.program_id(0) == 0)
            def _():
                l_ref[...] = jnp.zeros_like(l_ref)
            err = forward(h_ref, g_ref, wg_ref, p_ref, wp_ref, hn_ref, gp_ref, pp_ref) - t_ref[...]
            dy_ref[...] = err * (1.0 / d)
            l_ref[...] += jnp.sum(jnp.sum(err * err, axis=1, keepdims=True), axis=0, keepdims=True) * (0.5 / d)

        return pl.pallas_call(
            body_loss, name=name, grid=(t // tm,),
            in_specs=in_specs + [row],
            out_specs=[_bs((1, 1), lambda i: (0, 0))] + [row] * 4,
            out_shape=[S((1, 1), F32), S((t, d), F32), S((t, d), BF16), S((t, d), F32), S((t, d), F32)],
            compiler_params=_cp("arbitrary"),
        )(h, gain, wpg, p, wpp, target)

    def body(h_ref, g_ref, wg_ref, p_ref, wp_ref, o_ref, hn_ref, gp_ref, pp_ref):
        o_ref[...] = forward(h_ref, g_ref, wg_ref, p_ref, wp_ref, hn_ref, gp_ref, pp_ref)

    return pl.pallas_call(
        body, name=name, grid=(t // tm,),
        in_specs=in_specs,
        out_specs=[row] * 4,
        out_shape=[S((t, d), F32), S((t, d), BF16), S((t, d), F32), S((t, d), F32)],
        compiler_params=_cp("parallel"),
    )(h, gain, wpg, p, wpp)


def ple_bwd(dh, h, gain, wpg, layer, gp, pp, name, host=None):
    t, d = h.shape
    nk, _, rb, _ = wpg.shape
    tm = _row_tile(t, 512)

    def body(dh_ref, h_ref, g_ref, wg_ref, gp_ref, pp_ref, o_ref, dgp_ref, dpp_ref, dg_ref):
        @pl.when(pl.program_id(0) == 0)
        def _():
            dg_ref[...] = jnp.zeros_like(dg_ref)
        d_h = dh_ref[...]
        gate = _sigmoid(gp_ref[...])
        dgp = (d_h * pp_ref[...] * gate * (1.0 - gate)).astype(BF16)
        dgp_ref[...] = dgp
        dpp_ref[...] = (d_h * gate).astype(BF16)
        dhn = jnp.concatenate([_dot_nt(dgp, wg_ref[k]) for k in range(nk)], axis=1)
        dx, dgain = _rmsnorm_bwd(h_ref[...], g_ref[...], dhn)
        o_ref[...] = d_h + dx
        dg_ref[...] += dgain

    row = _bs((tm, d), lambda i: (i, 0))
    vec = _bs((1, d), lambda i: (0, 0))
    return _call(
        host, body, name=name, grid=(t // tm,),
        in_specs=[row, row, vec, _bs((nk, None, rb, d), lambda i: (0, layer, 0, 0)), row, row],
        out_specs=[row, row, row, vec],
        out_shape=[S((t, d), F32), S((t, d), BF16), S((t, d), BF16), S((1, d), F32)],
        compiler_params=_cp("arbitrary"),
        args=(dh, h, gain, wpg, gp, pp))


def _vec(a, i):
    return a[i:i + 1]


def local_step(x, p, target, w, plan=None):
    t = x.shape[0]
    tm = _row_tile(t, 512)
    grads = {}
    if plan is not None:
        plan.grads = grads
    saved = []
    h = x
    bias_q, bias_k = band_bias(w["rel_bias"])
    qg = jnp.tile(w["hyb_q_gain"], (1, N_HEADS))
    kg = jnp.tile(w["hyb_k_gain"], (1, N_HEADS))

    def run(fn, *a, name):
        hst = plan.host(name) if plan is not None else None
        out = fn(*a, name, hst)
        if hst is not None:
            plan.done(hst)
        return out

    def lru_blocks(n):
        return jnp.transpose(w[n].reshape(N_SHARD, LRU_BLOCKS, 64, LRU_BLOCK), (1, 0, 2, 3)).reshape(LRU_BLOCKS, LRU_BLOCK, LRU_BLOCK)

    for i in range(2):
        s = {}
        s["h0"] = h
        s["hn1"], s["g1"], s["u1"], s["a1"] = run(ffn_up, h, _vec(w["ffn1_norm"], i), w[f"ffn1_w_gate/{i}"], w[f"ffn1_w_up/{i}"], 0, name=f"ffn1_up_{i}")
        h = run(ffn_down, s["a1"], w[f"ffn1_w_down/{i}"], 0, h, name=f"ffn1_down_{i}")
        s["h1"] = h
        if i == 0:
            w_hyb_in = w["hyb_w_in"].reshape(N_SHARD, D_MODEL, -1)
            w_hyb_out = w["hyb_w_out"].reshape(D_MODEL, D_MODEL)
            s["hnm"], s["z"] = run(norm_mm, h, _vec(w["mix_norm"], i), w_hyb_in, name="hyb_in")
            s["qkv"] = hyb_prep(s["z"], qg, kg, "hyb_prep")
            os_, lses = [], []
            for g, dil in enumerate(DILATIONS):
                o, l = run(attn_fwd, *s["qkv"][dil], bias_q[g], dil, name=f"attn_fwd_{dil}")
                os_.append(o)
                lses.append(l)
            s["y2"], s["ya"], s["lt"] = hyb_post(s["z"], w["hyb_conv_w"], os_, lses, "hyb_post")
            h = run(functools.partial(ffn_down, scale=1.0), s["y2"], w_hyb_out.reshape(2, 1, ATTN_W, D_MODEL), 0, h, name="hyb_out")
        else:
            w_rec_in = w["rec_w_in"].reshape(N_SHARD, D_MODEL, -1)
            s["hnm"], s["z"] = run(norm_mm, h, _vec(w["mix_norm"], i), w_rec_in, name="rec_in")
            w_rec_out = w["rec_w_out"].reshape(D_MODEL, D_MODEL)
            lru_wa, lru_wx = lru_blocks("lru_wa"), lru_blocks("lru_wx")
            s["xb"], s["ga"], s["gx"], s["hs"], s["ro"] = run(
                rec_fwd, s["z"], w["rec_conv_w"], w["rec_conv_b"], lru_wa, lru_wx, w["lru_ba"], w["lru_bx"], w["lru_lambda"], name="rec_fwd")
            h = run(mm_acc, s["ro"], _bs((tm, D_MODEL), lambda r, k: (r, 0)), w_rec_out, _bs((D_MODEL, D_MODEL), lambda r, k: (0, 0)),
                    h, 1.0, 1, t, D_MODEL, tm, name="rec_out")
        s["h2"] = h
        s["hn2"], s["g2"], s["u2"], s["a2"] = run(ffn_up, h, _vec(w["ffn2_norm"], i), w[f"ffn2_w_gate/{i}"], w[f"ffn2_w_up/{i}"], 0, name=f"ffn2_up_{i}")
        h = run(ffn_down, s["a2"], w[f"ffn2_w_down/{i}"], 0, h, name=f"ffn2_down_{i}")
        s["h3"] = h
        if i == 0:
            h, s["hnp"], s["gp"], s["pp"] = ple_fwd(h, _vec(w["ple_norm"], i), w[f"ple_w_gate/{i}"], 0, p[i], w[f"ple_w_proj/{i}"], f"ple_fwd_{i}")
        else:
            loss, dh, s["hnp"], s["gp"], s["pp"] = ple_fwd(h, _vec(w["ple_norm"], i), w[f"ple_w_gate/{i}"], 0, p[i], w[f"ple_w_proj/{i}"],
                                                           f"ple_fwd_{i}", target)
        saved.append(s)

    norm_g = {n: [None, None] for n in ("ffn1_norm", "mix_norm", "ffn2_norm", "ple_norm")}
    for i in (1, 0):
        s = saved[i]
        dh_out = dh
        dh, dgp, dpp, norm_g["ple_norm"][i] = run(ple_bwd, dh_out, s["h3"], _vec(w["ple_norm"], i), w[f"ple_w_gate/{i}"], 0, s["gp"], s["pp"],
                                                  name=f"ple_bwd_{i}")
        grads["ple_w_gate"] = run(tn_mm, s["hnp"], lambda tk: _bs((tk, 256), lambda k, j: (j, k)), dgp, lambda tk: _bs((tk, D_MODEL), lambda k, j: (j, 0)),
                                  N_SHARD, t, 256, D_MODEL, S((N_SHARD, 2, 256, D_MODEL), BF16),
                                  _bs((None, None, 256, D_MODEL), lambda k, j, i=i: (k, i, 0, 0)), 1.0, grads.get("ple_w_gate"), name=f"ple_gw_gate_{i}")
        grads["ple_w_proj"] = run(tn_mm, p[i], lambda tk: _bs((tk, 256), lambda k, j: (j, 0)), dpp, lambda tk: _bs((tk, 256), lambda k, j: (j, k)),
                                  N_SHARD, t, 256, 256, S((N_SHARD, 2, 256, 256), BF16),
                                  _bs((None, None, 256, 256), lambda k, j, i=i: (k, i, 0, 0)), 1.0, grads.get("ple_w_proj"), name=f"ple_gw_proj_{i}")
        dh_out = dh
        dg, du = run(ffn_bwd_act, dh_out, w[f"ffn2_w_down/{i}"], 0, s["g2"], s["u2"], name=f"ffn2_bwd_act_{i}")
        ffn_wgrads("ffn2", s["hn2"], dh_out, s["a2"], dg, du, i, grads, run)
        dh, norm_g["ffn2_norm"][i] = run(ffn_bwd_in, dg, du, w[f"ffn2_w_gate/{i}"], w[f"ffn2_w_up/{i}"], 0, s["h2"], _vec(w["ffn2_norm"], i), dh_out,
                                         name=f"ffn2_bwd_in_{i}")
        dh_out = dh
        if i == 1:
            d_o = nt_mm(dh_out, w_rec_out, "rec_bwd_out")
            grads["rec_w_out"] = run(tn_mm, s["ro"], lambda tk: _bs((tk, 256), lambda k, j: (j, k)), dh_out, lambda tk: _bs((tk, D_MODEL), lambda k, j: (j, 0)),
                                     N_SHARD, t, 256, D_MODEL, S((N_SHARD, 256, D_MODEL), BF16), _bs((None, 256, D_MODEL), lambda k, j: (k, 0, 0)),
                                     1.0, None, name="rec_gw_out").reshape(N_SHARD, 1, 256, D_MODEL)
            dz, dga, dgx, small = run(rec_bwd, d_o, s["z"], s["xb"], s["ga"], s["gx"], s["hs"], w["rec_conv_w"], lru_wa, lru_wx, w["lru_lambda"],
                                      name="rec_bwd")
            blk = lambda tk: _bs((tk, LRU_BLOCK), lambda k, j: (j, k))
            for nm, dgt in (("lru_wa", dga), ("lru_wx", dgx)):
                gw = run(tn_mm, s["xb"], blk, dgt, blk, LRU_BLOCKS, t, LRU_BLOCK, LRU_BLOCK, S((LRU_BLOCKS, LRU_BLOCK, LRU_BLOCK), BF16),
                         _bs((None, LRU_BLOCK, LRU_BLOCK), lambda k, j: (k, 0, 0)), 1.0, None, name="rec_gw_" + nm)
                grads[nm] = jnp.transpose(gw.reshape(LRU_BLOCKS, N_SHARD, 64, LRU_BLOCK), (1, 0, 2, 3)).reshape(N_SHARD, 1, LRU_BLOCKS, 64, LRU_BLOCK)
            grads["lru_ba"], grads["lru_bx"], grads["lru_lambda"], grads["rec_conv_b"] = (small[r:r + 1] for r in range(4))
            grads["rec_conv_w"] = small[4:8]
            nb_, bw = N_SHARD, 512
            w_in, nm_in = w_rec_in, "rec_w_in"
        else:
            dy = nt_mm(dh_out, w_hyb_out, "hyb_bwd_out")
            grads["hyb_w_out"] = run(tn_mm, s["y2"], lambda tk: _bs((None, tk, 256), lambda k, j: (k // 2, j, k % 2)), dh_out,
                                     lambda tk: _bs((tk, D_MODEL), lambda k, j: (j, 0)),
                                     N_SHARD, t, 256, D_MODEL, S((N_SHARD, 256, D_MODEL), BF16), _bs((None, 256, D_MODEL), lambda k, j: (k, 0, 0)),
                                     1.0, None, name="hyb_gw_out").reshape(N_SHARD, 1, 256, D_MODEL)
            delta, dya = attn_delta(dy, s["ya"], "attn_delta")
            dqs, dks, dvs, dbs = [], [], [], []
            for g, dil in enumerate(DILATIONS):
                dq, db = run(attn_bwd_dq, *s["qkv"][dil], dya[dil], s["lt"][dil], delta[dil], bias_q[g], dil, name=f"attn_bwd_dq_{dil}")
                dk, dv = run(attn_bwd_dkv, *s["qkv"][dil], dya[dil], s["lt"][dil], delta[dil], bias_k[g], dil, name=f"attn_bwd_dkv_{dil}")
                dqs.append(dq); dks.append(dk); dvs.append(dv); dbs.append(db)
            grads["rel_bias"] = rel_bias_grad(dbs, "rel_bias_grad")
            dz, grads["hyb_conv_w"], dqg, dkg = hyb_dz(s["z"], dy, w["hyb_conv_w"], qg, kg, dqs, dks, dvs, "hyb_dz")
            grads["hyb_q_gain"] = jnp.sum(dqg.reshape(N_HEADS, HEAD_DIM), axis=0, keepdims=True)
            grads["hyb_k_gain"] = jnp.sum(dkg.reshape(N_HEADS, HEAD_DIM), axis=0, keepdims=True)
            nb_, bw = N_SHARD, 768
            w_in, nm_in = w_hyb_in, "hyb_w_in"
        grads[nm_in] = run(tn_mm, s["hnm"], lambda tk: _bs((tk, D_MODEL), lambda k, j: (j, 0)), dz, lambda tk, bw=bw: _bs((tk, bw), lambda k, j: (j, k)),
                           nb_, t, D_MODEL, bw, S((nb_, D_MODEL, bw), BF16), _bs((None, D_MODEL, bw), lambda k, j: (k, 0, 0)),
                           1.0, None, name=f"mix_gw_in_{i}").reshape(nb_, 1, D_MODEL, bw)
        dh, norm_g["mix_norm"][i] = run(
            nt_acc_normbwd, [(dz, _bs((_row_tile(t, BWD_IN_ROWS), nb_ * bw), lambda r: (r, 0)), w_in, _bs((nb_, D_MODEL, bw), lambda r: (0, 0, 0)),
                              lambda x_ref, k, bw=bw: x_ref[:, k * bw:(k + 1) * bw])],
            nb_, s["h1"], _vec(w["mix_norm"], i), dh_out, name=f"mix_bwd_in_{i}")
        dh_out = dh
        dg, du = run(ffn_bwd_act, dh_out, w[f"ffn1_w_down/{i}"], 0, s["g1"], s["u1"], name=f"ffn1_bwd_act_{i}")
        ffn_wgrads("ffn1", s["hn1"], dh_out, s["a1"], dg, du, i, grads, run)
        dh, norm_g["ffn1_norm"][i] = run(ffn_bwd_in, dg, du, w[f"ffn1_w_gate/{i}"], w[f"ffn1_w_up/{i}"], 0, s["h0"], _vec(w["ffn1_norm"], i), dh_out,
                                         name=f"ffn1_bwd_in_{i}")
    for n, (g0, g1) in norm_g.items():
        grads[n] = jnp.concatenate([g0, g1], axis=0)
    return loss, dh, grads


def gather_weights(shards, name):
    n = len(shards)

    def body(*refs):
        ins, outs = refs[:n], refs[n:2 * n]
        send1, recv1, send2, recv2, lsem = refs[2 * n:]
        x, y, c, k, chips, kk = _place()
        sib = (x, y, 1 - c)

        def remote(src, dst, ssem, rsem, to):
            return pltpu.make_async_remote_copy(src_ref=src, dst_ref=dst, send_sem=ssem, recv_sem=rsem, device_id=to, device_id_type=MESH)

        local = [pltpu.make_async_copy(ins[a], outs[a].at[k], lsem.at[a]) for a in range(n)]
        for cp in local:
            cp.start()
        sends = []
        for a in range(n):
            for j, chip in enumerate(chips):
                cp = remote(ins[a].at[c], outs[a].at[k, c], send1.at[3 * a + j], recv1.at[3 * a + j], (*chip, c))
                cp.start()
                sends.append(cp)
        for a in range(n):
            for j, chip in enumerate(chips):
                remote(ins[a].at[c], outs[a].at[kk[j], c], send1.at[3 * a + j], recv1.at[3 * a + j], (*chip, c)).wait_recv()
                cp = remote(outs[a].at[kk[j], c], outs[a].at[kk[j], c], send2.at[3 * a + j], recv2.at[3 * a + j], sib)
                cp.start()
                sends.append(cp)
        for a in range(n):
            for j in range(3):
                remote(outs[a].at[kk[j], 1 - c], outs[a].at[kk[j], 1 - c], send2.at[3 * a + j], recv2.at[3 * a + j], sib).wait_recv()
        for cp in sends:
            cp.wait_send()
        for cp in local:
            cp.wait()

    return pl.pallas_call(
        body, name=name,
        in_specs=[_ANY] * n, out_specs=[_ANY] * n,
        out_shape=[S((N_SHARD,) + s.shape, s.dtype) for s in shards],
        scratch_shapes=[pltpu.SemaphoreType.DMA((3 * n,))] * 4 + [pltpu.SemaphoreType.DMA((n,))],
    )(*shards)


def exchange_cores(rs, name):
    n = len(rs)

    def body(*refs):
        outs = refs[n:2 * n]
        send, recv = refs[2 * n:]
        x, y, c = lax.axis_index("x"), lax.axis_index("y"), lax.axis_index("c")
        sends = []
        for a in range(n):
            for k in range(N_SHARD):
                slot = outs[a].at[2 * k + c]
                cp = _remote(slot, slot, send.at[N_SHARD * a + k], recv.at[N_SHARD * a + k], (x, y, 1 - c))
                cp.start()
                sends.append(cp)
        for a in range(n):
            for k in range(N_SHARD):
                slot = outs[a].at[2 * k + 1 - c]
                _remote(slot, slot, send.at[N_SHARD * a + k], recv.at[N_SHARD * a + k], (x, y, 1 - c)).wait_recv()
        for cp in sends:
            cp.wait_send()

    return pl.pallas_call(
        body, name=name,
        in_specs=[_ANY] * n, out_specs=[_ANY] * n,
        out_shape=[S(r.shape, r.dtype) for r in rs],
        input_output_aliases={a: a for a in range(n)},
        scratch_shapes=[pltpu.SemaphoreType.DMA((N_SHARD * n,))] * 2,
    )(*rs)


def allgather8(a, name):
    def body(a_ref, o_ref, send, recv, lsem):
        x, y, c = lax.axis_index("x"), lax.axis_index("y"), lax.axis_index("c")
        me = 4 * x + 2 * y + c
        local = pltpu.make_async_copy(a_ref, o_ref.at[me], lsem)
        local.start()
        cps = []
        for f in range(1, N_DEV):
            fx, fy, fc = (f >> 2) & 1, (f >> 1) & 1, f & 1
            peer = (1 - x if fx else x, 1 - y if fy else y, 1 - c if fc else c)
            cp = pltpu.make_async_remote_copy(src_ref=a_ref, dst_ref=o_ref.at[me], send_sem=send.at[f - 1], recv_sem=recv.at[f - 1],
                                              device_id=peer, device_id_type=MESH)
            cp.start()
            cps.append((cp, 4 * peer[0] + 2 * peer[1] + peer[2], f))
        for cp, pidx, f in cps:
            pltpu.make_async_remote_copy(src_ref=a_ref, dst_ref=o_ref.at[pidx], send_sem=send.at[f - 1], recv_sem=recv.at[f - 1],
                                         device_id=(x, y, c), device_id_type=MESH).wait_recv()
        for cp, _, _ in cps:
            cp.wait_send()
        local.wait()

    return pl.pallas_call(
        body, name=name, in_specs=[_ANY], out_specs=_ANY,
        out_shape=S((N_DEV,) + a.shape, a.dtype),
        scratch_shapes=[pltpu.SemaphoreType.DMA((N_DEV - 1,)), pltpu.SemaphoreType.DMA((N_DEV - 1,)), pltpu.SemaphoreType.DMA],
    )(a)


def sum8(a, name):
    _, r, c = a.shape

    def body(a_ref, o_ref):
        acc = a_ref[0]
        for j in range(1, N_DEV):
            acc = acc + a_ref[j]
        o_ref[...] = acc

    return pl.pallas_call(
        body, name=name, in_specs=[_bs((N_DEV, r, c), lambda: (0, 0, 0))], out_specs=_bs((r, c), lambda: (0, 0)),
        out_shape=S((r, c), F32),
    )(a)


def adamw(w, m, v, g, name):
    nl, r, c = w.shape
    tr = _row_tile(r, 256)
    summed = g.ndim == 4

    def body(w_ref, m_ref, v_ref, g_ref, go_ref, d_ref, mo_ref, vo_ref):
        if summed:
            gr = g_ref[0].astype(F32)
            for j in range(1, N_DEV):
                gr = gr + g_ref[j].astype(F32)
        else:
            gr = g_ref[...]
        m_new = ADAM_B1 * m_ref[...] + (1.0 - ADAM_B1) * gr
        v_new = ADAM_B2 * v_ref[...] + (1.0 - ADAM_B2) * (gr * gr)
        m_hat = m_new / (1.0 - ADAM_B1 ** ADAM_STEP)
        v_hat = v_new / (1.0 - ADAM_B2 ** ADAM_STEP)
        go_ref[...] = gr
        d_ref[...] = -ADAM_LR * (m_hat / (jnp.sqrt(v_hat) + ADAM_EPS) + ADAM_WD * w_ref[...])
        mo_ref[...] = m_new
        vo_ref[...] = v_new

    row = _bs((None, tr, c), lambda l, i: (l, i, 0))
    gspec = _bs((N_DEV, None, tr, c), lambda l, i: (0, l, i, 0)) if summed else row
    return pl.pallas_call(
        body, name=name, grid=(nl, r // tr),
        in_specs=[row, row, row, gspec], out_specs=[row] * 4, out_shape=[S((nl, r, c), F32)] * 4,
        compiler_params=_cp("parallel", "parallel"),
    )(w, m, v, g)


WEIGHTS = ["rel_bias", "ffn1_norm", "ffn1_w_gate", "ffn1_w_up", "ffn1_w_down", "mix_norm", "hyb_w_in", "hyb_conv_w", "hyb_q_gain",
           "hyb_k_gain", "hyb_w_out", "rec_w_in", "rec_conv_w", "rec_conv_b", "lru_wa", "lru_ba", "lru_wx", "lru_bx", "lru_lambda",
           "rec_w_out", "ffn2_norm", "ffn2_w_gate", "ffn2_w_up", "ffn2_w_down", "ple_norm", "ple_w_gate", "ple_w_proj"]
BIG = ["ffn1_w_gate", "ffn1_w_up", "ffn1_w_down", "hyb_w_in", "hyb_w_out", "rec_w_in", "lru_wa", "lru_wx", "rec_w_out",
       "ffn2_w_gate", "ffn2_w_up", "ffn2_w_down", "ple_w_gate", "ple_w_proj"]
SMALL_SHARDED = ["hyb_conv_w", "rec_conv_w", "rec_conv_b", "lru_ba", "lru_bx", "lru_lambda"]
SMALL = ["rel_bias", "ffn1_norm", "mix_norm", "ffn2_norm", "ple_norm", "hyb_q_gain", "hyb_k_gain"] + SMALL_SHARDED
PACK_W = 1024
PER_LAYER = ["ffn1_w_gate", "ffn1_w_up", "ffn1_w_down", "ffn2_w_gate", "ffn2_w_up", "ffn2_w_down", "ple_w_gate", "ple_w_proj"]
FIRST = ["ffn1_w_gate/0", "ffn1_w_up/0"]
LAST = ["ffn1_w_down"]
GATHER_PLAN = {
    "ffn1_up_0": ["ffn1_w_down/0", "hyb_w_in"],
    "ffn1_down_0": ["hyb_w_out", "ple_w_gate/0", "ple_w_proj/0"],
    "hyb_in": ["ffn2_w_gate/0"],
    "attn_fwd_1": ["ffn2_w_up/0"],
    "attn_fwd_4": ["ffn2_w_down/0"],
    "ffn2_up_0": ["ffn1_w_gate/1", "ffn1_w_up/1"],
    "ffn2_down_0": ["lru_wa", "lru_wx", "rec_w_out"],
    "ffn1_up_1": ["ffn1_w_down/1", "rec_w_in"],
    "rec_in": ["ffn2_w_down/1", "ple_w_gate/1", "ple_w_proj/1"],
    "rec_fwd": ["ffn2_w_gate/1", "ffn2_w_up/1"],
}
SCATTER_PLAN = {
    "ple_gw_proj_1": [("ple_w_gate", 1)],
    "ffn2_bwd_act_1": [("ple_w_proj", 1)],
    "ffn2_bwd_in_1": [("ffn2_w_gate", 1)],
    "rec_bwd": [("ffn2_w_up", 1), ("ffn2_w_down", 1)],
    "mix_bwd_in_1": [("rec_w_in", 0), ("rec_w_out", 0), ("lru_wa", 0), ("lru_wx", 0)],
    "ffn1_bwd_in_1": [("ffn1_w_gate", 1)],
    "ple_bwd_0": [("ffn1_w_up", 1)],
    "ple_gw_proj_0": [("ple_w_gate", 0)],
    "ffn2_bwd_act_0": [("ple_w_proj", 0)],
    "ffn2_gw_0_gate": [("ffn1_w_down", 1)],
    "ffn2_bwd_in_0": [("ffn2_w_gate", 0)],
    "attn_bwd_dq_1": [("ffn2_w_down", 0)],
    "attn_bwd_dkv_1": [("ffn2_w_up", 0)],
    "mix_bwd_in_0": [("hyb_w_in", 0), ("hyb_w_out", 0)],
    "ffn1_gw_0_up": [("ffn1_w_gate", 0)],
    "ffn1_gw_0_down": [("ffn1_w_up", 0)],
    "ffn1_bwd_in_0": [("ffn1_w_down", 0)],
}
FORWARD_PLAN = {
    "ffn1_bwd_in_1": ["rec_w_in", "rec_w_out", "lru_wa", "lru_wx"],
    "ffn2_bwd_in_0": ["ple_w_gate", "ple_w_proj"],
    "mix_bwd_in_0": ["ffn2_w_gate", "ffn2_w_up", "ffn2_w_down"],
    "ffn1_gw_0_up": ["hyb_w_in", "hyb_w_out"],
    "ffn1_bwd_in_0": ["ffn1_w_gate", "ffn1_w_up"],
}


class Plan:
    def __init__(self, shards, w):
        self.shards, self.w, self.grads, self.landed = shards, w, None, {}

    def host(self, kname):
        if kname in GATHER_PLAN:
            h = Host("gather", [self.shards[n] for n in GATHER_PLAN[kname]])
            h.names = GATHER_PLAN[kname]
            return h
        if kname in SCATTER_PLAN or kname in FORWARD_PLAN:
            items = SCATTER_PLAN.get(kname, [])
            fwd = FORWARD_PLAN.get(kname, [])
            h = Host("scatter", [(self.grads[n], lay, self.landed.get(n)) for n, lay in items], [self.landed[n] for n in fwd])
            h.names = [n for n, _ in items] + fwd
            return h
        return None

    def done(self, h):
        for n, o in zip(h.names, h.outs):
            if h.kind == "gather":
                self.w[n] = o
            else:
                self.landed[n] = o


def _halves(a):
    if a.shape[0] == 2:
        return a
    return a.reshape((2, a.shape[1] // 2) + a.shape[2:])


def _pack_rows(arrs, width):
    rows, offs, r0 = [], [], 0
    for a in arrs:
        if a.shape[1] > width:
            a = a.reshape(-1, width)
        rows.append(jnp.pad(a, ((0, 0), (0, width - a.shape[1]))))
        offs.append(r0)
        r0 += a.shape[0]
    pad = (-r0) % 8
    if pad:
        rows.append(jnp.zeros((pad, width), F32))
    return jnp.concatenate(rows, axis=0), offs


def kernel(x, p, rel_bias, ffn1_norm, ffn1_w_gate, ffn1_w_up, ffn1_w_down, mix_norm, hyb_w_in, hyb_conv_w, hyb_q_gain, hyb_k_gain, hyb_w_out, rec_w_in, rec_conv_w, rec_conv_b, lru_wa, lru_ba, lru_wx, lru_bx, lru_lambda, rec_w_out, ffn2_norm, ffn2_w_gate, ffn2_w_up, ffn2_w_down, ple_norm, ple_w_gate, ple_w_proj, loss_target, m_rel_bias, m_ffn1_norm, m_ffn1_w_gate, m_ffn1_w_up, m_ffn1_w_down, m_mix_norm, m_hyb_w_in, m_hyb_conv_w, m_hyb_q_gain, m_hyb_k_gain, m_hyb_w_out, m_rec_w_in, m_rec_conv_w, m_rec_conv_b, m_lru_wa, m_lru_ba, m_lru_wx, m_lru_bx, m_lru_lambda, m_rec_w_out, m_ffn2_norm, m_ffn2_w_gate, m_ffn2_w_up, m_ffn2_w_down, m_ple_norm, m_ple_w_gate, m_ple_w_proj, v_rel_bias, v_ffn1_norm, v_ffn1_w_gate, v_ffn1_w_up, v_ffn1_w_down, v_mix_norm, v_hyb_w_in, v_hyb_conv_w, v_hyb_q_gain, v_hyb_k_gain, v_hyb_w_out, v_rec_w_in, v_rec_conv_w, v_rec_conv_b, v_lru_wa, v_lru_ba, v_lru_wx, v_lru_bx, v_lru_lambda, v_rec_w_out, v_ffn2_norm, v_ffn2_w_gate, v_ffn2_w_up, v_ffn2_w_down, v_ple_norm, v_ple_w_gate, v_ple_w_proj):
    given = dict(locals())
    wts = {n: given[n] for n in WEIGHTS}
    k_chip = 2 * lax.axis_index("x") + lax.axis_index("y")

    shards = {}
    for n in BIG:
        b16 = wts[n].astype(BF16)
        if n in PER_LAYER:
            shards[n + "/0"], shards[n + "/1"] = b16[0:1], b16[1:2]
        else:
            shards[n] = b16
    first = gather_weights([_halves(shards[n]) for n in FIRST], "gather_first")
    w = {n: g.reshape((N_SHARD,) + shards[n].shape) for n, g in zip(FIRST, first)}
    plan = Plan(shards, w)
    sm2d = {n: wts[n].reshape(-1, wts[n].shape[-1]) for n in SMALL_SHARDED}
    slab, offs = _pack_rows([sm2d[n] for n in SMALL_SHARDED], 256)
    slabs = allgather8(slab, "gather_small")[0::2]
    for n, o in zip(SMALL_SHARDED, offs):
        r, cw = sm2d[n].shape
        w[n] = jnp.concatenate([slabs[kc, o:o + r, :cw] for kc in range(N_SHARD)], axis=1)
    for n in SMALL:
        if n not in SMALL_SHARDED:
            w[n] = wts[n]

    loss, dx, grads = local_step(x[0], p[:, 0], loss_target[0], w, plan)
    loss = lax.psum(loss[0, 0], ("x", "y", "c"))

    for n, r8 in zip(LAST, exchange_cores([plan.landed[n] for n in LAST], "exchange_cores")):
        plan.landed[n] = r8
    out = {}
    for n in BIG:
        r8 = plan.landed[n]
        shp = wts[n].shape
        shp3 = shp if len(shp) == 3 else (shp[0], -1, shp[-1])
        three = lambda a: a.reshape(shp3)
        res = adamw(three(wts[n]), three(given["m_" + n]), three(given["v_" + n]), r8.reshape((N_DEV,) + three(wts[n]).shape), "adamw_" + n)
        out[n] = [a.reshape(shp) for a in res]
    g2d = [grads[n].reshape(-1, grads[n].shape[-1]) if n != "rel_bias" else grads[n].reshape(1, -1) for n in SMALL]
    gslab, goffs = _pack_rows(g2d, PACK_W)
    gsum = sum8(allgather8(gslab, "gather_small_grads"), "sum_small_grads")
    for n, o, g in zip(SMALL, goffs, g2d):
        shp = wts[n].shape
        r, cw = g.shape
        gs = gsum[o:o + r, :cw]
        if n in SMALL_SHARDED:
            sw = shp[-1]
            gs = lax.dynamic_slice_in_dim(gs, k_chip * sw, sw, axis=1)
        three = lambda a: a.reshape((1, -1, shp[-1]))
        res = adamw(three(wts[n]), three(given["m_" + n]), three(given["v_" + n]), three(gs), "adamw_" + n)
        out[n] = [a.reshape(shp) for a in res]
    return (loss, dx[None], *[out[n][0] for n in WEIGHTS], *[out[n][1] for n in WEIGHTS],
            *[out[n][2] for n in WEIGHTS], *[out[n][3] for n in WEIGHTS])
```

```python
import functools
import math

import numpy as np
import jax
import jax.numpy as jnp
from jax import lax
from jax.experimental import pallas as pl
from jax.experimental.pallas import tpu as pltpu

F32, BF16 = jnp.float32, jnp.bfloat16
S = jax.ShapeDtypeStruct
MESH = pl.DeviceIdType.MESH

D_MODEL = 1024
N_SHARD = 4
N_DEV = 8
HEAD_DIM = 64
N_HEADS = 8
ATTN_W = N_HEADS * HEAD_DIM
CONV_W = 512
BAND = 128
DILATIONS = (1, 4, 16)
REL_BUCKETS = 32
REL_MAX_DIST = 2048
LRU_BLOCKS = 4
LRU_BLOCK = 256
LRU_C = 8.0
EPS = 1e-6
NEG = -1e30
VMEM_LIMIT = 56 * 1024 * 1024
FFN_ROWS = 1024
FFN_SHARDS_ROWS = 512
TN_ROWS = 2048
BWD_IN_ROWS = 512

ADAM_LR, ADAM_B1, ADAM_B2, ADAM_EPS, ADAM_WD, ADAM_STEP = 0.001, 0.9, 0.999, 1e-08, 0.01, 10


def _cp(*sem):
    return pltpu.CompilerParams(dimension_semantics=sem, vmem_limit_bytes=VMEM_LIMIT)


def _bs(shape, imap):
    return pl.BlockSpec(shape, imap)


def _row_tile(t, want):
    for cand in range(min(want, t) // 8 * 8, 0, -8):
        if t % cand == 0:
            return cand
    return t


_ANY = pl.BlockSpec(memory_space=pl.ANY)


def _place():
    x, y, c = lax.axis_index("x"), lax.axis_index("y"), lax.axis_index("c")
    chips = [(1 - x, y), (x, 1 - y), (1 - x, 1 - y)]
    return x, y, c, 2 * x + y, chips, [2 * cx + cy for cx, cy in chips]


def _remote(src, dst, ssem, rsem, to):
    return pltpu.make_async_remote_copy(src_ref=src, dst_ref=dst, send_sem=ssem, recv_sem=rsem, device_id=to, device_id_type=MESH)


class Host:
    def __init__(self, kind, items, forwards=()):
        self.kind, self.items, self.forwards, self.outs = kind, items, list(forwards), None

    def n_sems(self):
        return 3 * len(self.items) + N_SHARD * len(self.forwards), len(self.items)

    def operands(self):
        if self.kind == "gather":
            return list(self.items), [S((N_SHARD,) + s.shape, s.dtype) for s in self.items], {}
        xin, shapes, alias = [], [], {}
        for a, (g, _, r_prev) in enumerate(self.items):
            xin.append(g)
            if r_prev is not None:
                alias[len(xin)] = a
                xin.append(r_prev)
            shapes.append(S((N_DEV,) + g.shape[1:], g.dtype))
        for f, r in enumerate(self.forwards):
            alias[len(xin)] = len(self.items) + f
            xin.append(r)
            shapes.append(S(r.shape, r.dtype))
        return xin, shapes, alias

    def copies(self, xi, xo, send, recv, lsem):
        x, y, c, k, chips, kk = _place()
        starts, waits = [], []
        pos = 0
        for f in range(len(self.forwards)):
            arr = xo[len(self.items) + f]
            for kq in range(N_SHARD):
                sem = 3 * len(self.items) + N_SHARD * f + kq
                cp = _remote(arr.at[2 * kq + c], arr.at[2 * kq + c], send.at[sem], recv.at[sem], (x, y, 1 - c))
                starts.append((cp, "start"))
                waits.append((cp, "wait_send"))
                other = arr.at[2 * kq + 1 - c]
                waits.append((_remote(other, other, send.at[sem], recv.at[sem], (x, y, 1 - c)), "wait_recv"))
        for a, item in enumerate(self.items):
            if self.kind == "gather":
                src_of = lambda chip_idx, s=xi[a]: s
                dst_of = lambda chip_idx, o=xo[a]: o.at[chip_idx]
                mine, theirs = k, kk
            else:
                g_ref = xi[pos]
                pos += 1 if item[2] is None else 2
                lay = item[1]
                src_of = lambda chip_idx, g=g_ref, lay=lay: g.at[chip_idx, lay]
                dst_of = lambda slot, o=xo[a], lay=lay: o.at[slot, lay]
                mine, theirs = 2 * k + c, [2 * kj + c for kj in kk]
            own_src = src_of(k)
            local = pltpu.make_async_copy(own_src, dst_of(mine), lsem.at[a])
            starts.append((local, "start"))
            waits.append((local, "wait"))
            for j, chip in enumerate(chips):
                src = own_src if self.kind == "gather" else src_of(kk[j])
                cp = _remote(src, dst_of(mine), send.at[3 * a + j], recv.at[3 * a + j], (*chip, c))
                starts.append((cp, "start"))
                waits.append((cp, "wait_send"))
                waits.append((_remote(own_src, dst_of(theirs[j]), send.at[3 * a + j], recv.at[3 * a + j], (*chip, c)), "wait_recv"))
        return starts, waits


def _call(host, body, *, name, grid, in_specs, out_specs, out_shape, scratch_shapes=(), compiler_params=None, args, aliases=None):
    aliases = dict(aliases or {})
    if host is None:
        return pl.pallas_call(body, name=name, grid=grid, in_specs=in_specs, out_specs=out_specs, out_shape=out_shape,
                              scratch_shapes=list(scratch_shapes), input_output_aliases=aliases, compiler_params=compiler_params)(*args)
    single = not isinstance(out_shape, (list, tuple))
    out_specs_l = [out_specs] if single else list(out_specs)
    out_shape_l = [out_shape] if single else list(out_shape)
    n_in, n_out, n_scr = len(in_specs), len(out_shape_l), len(scratch_shapes)
    xin, xshapes, xalias = host.operands()
    n_items = len(xshapes)
    n_rsem, n_lsem = host.n_sems()
    for i_in, i_out in xalias.items():
        aliases[n_in + i_in] = n_out + i_out
    nd = len(grid)

    def hosted(*refs):
        ins, xi = refs[:n_in], refs[n_in:n_in + len(xin)]
        o0 = n_in + len(xin)
        outs, xo = refs[o0:o0 + n_out], refs[o0 + n_out:o0 + n_out + n_items]
        s0 = o0 + n_out + n_items
        scr = refs[s0:s0 + n_scr]
        send, recv, lsem = refs[s0 + n_scr:]
        first = functools.reduce(jnp.logical_and, [pl.program_id(d) == 0 for d in range(nd)])
        last = functools.reduce(jnp.logical_and, [pl.program_id(d) == grid[d] - 1 for d in range(nd)])
        starts, waits = host.copies(xi, xo, send, recv, lsem)

        @pl.when(first)
        def _():
            for cp, how in starts:
                getattr(cp, how)()
        body(*ins, *outs, *scr)

        @pl.when(last)
        def _():
            for cp, how in waits:
                getattr(cp, how)()

    res = pl.pallas_call(
        hosted, name=name, grid=grid,
        in_specs=list(in_specs) + [_ANY] * len(xin),
        out_specs=out_specs_l + [_ANY] * n_items,
        out_shape=out_shape_l + xshapes,
        scratch_shapes=list(scratch_shapes) + [pltpu.SemaphoreType.DMA((n_rsem,)), pltpu.SemaphoreType.DMA((n_rsem,)),
                                               pltpu.SemaphoreType.DMA((max(n_lsem, 1),))],
        input_output_aliases=aliases,
        compiler_params=pltpu.CompilerParams(dimension_semantics=("arbitrary",) * nd, vmem_limit_bytes=VMEM_LIMIT),
    )(*args, *xin)
    host.outs = list(res[n_out:])
    return res[0] if single else list(res[:n_out])


def _rstd(x):
    return lax.rsqrt(jnp.mean(x * x, axis=-1, keepdims=True) + EPS)


def _sigmoid(x):
    return 1.0 / (1.0 + jnp.exp(-x))


def _dot(a, b):
    return jnp.dot(a, b, preferred_element_type=F32)


def _dot_nt(a, b):
    return lax.dot_general(a, b, (((1,), (1,)), ((), ())), preferred_element_type=F32)


def _dot_tn(a, b):
    return lax.dot_general(a, b, (((0,), (0,)), ((), ())), preferred_element_type=F32)


def _seg_dot(x, seg_bf16):
    hi = x.astype(BF16)
    lo = (x - hi.astype(F32)).astype(BF16)
    return _dot(hi, seg_bf16) + _dot(lo, seg_bf16)


def _shift_down(x, prev8, s):
    if s == 0:
        return x
    tm = x.shape[0]
    row = lax.broadcasted_iota(jnp.int32, x.shape, 0)
    main = jnp.where(row >= s, pltpu.roll(x, s, axis=0), 0.0)
    row8 = lax.broadcasted_iota(jnp.int32, prev8.shape, 0)
    head = jnp.where(row8 < s, pltpu.roll(prev8, s, axis=0), 0.0)
    if tm == 8:
        return main + head
    return main + jnp.concatenate([head, jnp.zeros((tm - 8, x.shape[1]), x.dtype)], axis=0)


def _shift_up(x, next8, s):
    if s == 0:
        return x
    tm = x.shape[0]
    row = lax.broadcasted_iota(jnp.int32, x.shape, 0)
    main = jnp.where(row < tm - s, pltpu.roll(x, tm - s, axis=0), 0.0)
    row8 = lax.broadcasted_iota(jnp.int32, next8.shape, 0)
    tail = jnp.where(row8 >= 8 - s, pltpu.roll(next8, 8 - s, axis=0), 0.0)
    if tm == 8:
        return main + tail
    return main + jnp.concatenate([jnp.zeros((tm - 8, x.shape[1]), x.dtype), tail], axis=0)


def _roll_fill(x, s, fill, up):
    tm = x.shape[0]
    row = lax.broadcasted_iota(jnp.int32, x.shape, 0)
    if up:
        return jnp.where(row < tm - s, pltpu.roll(x, tm - s, axis=0), fill)
    return jnp.where(row >= s, pltpu.roll(x, s, axis=0), fill)


def _log1p(y):
    u = 1.0 + y
    return jnp.where(u == 1.0, y, jnp.log(u) * (y / jnp.where(u == 1.0, 1.0, u - 1.0)))


def _softplus(x):
    return jnp.maximum(x, 0.0) + _log1p(jnp.exp(-jnp.abs(x)))


def _neg_expm1(y):
    series = -y * (1.0 + y * (0.5 + y * (1.0 / 6.0 + y * (1.0 / 24.0 + y * (1.0 / 120.0)))))
    return jnp.where(jnp.abs(y) < 0.03, series, 1.0 - jnp.exp(y))


_GELU_C = math.sqrt(2.0 / math.pi)


def _gelu_and_grad(x):
    inner = _GELU_C * (x + 0.044715 * x * x * x)
    t = jnp.tanh(inner)
    g = 0.5 * x * (1.0 + t)
    dg = 0.5 * (1.0 + t) + 0.5 * x * (1.0 - t * t) * _GELU_C * (1.0 + 3.0 * 0.044715 * x * x)
    return g, dg


def _rmsnorm_bwd(x, gain, dy):
    r = _rstd(x)
    xhat = x * r
    dxhat = dy * gain
    dx = r * (dxhat - xhat * jnp.mean(dxhat * xhat, axis=-1, keepdims=True))
    return dx, jnp.sum(dy * xhat, axis=0, keepdims=True)


def ffn_up(h, gain, wg, wu, layer, name, host=None):
    t, d = h.shape
    nk, _, _, f = wg.shape
    tm = _row_tile(t, FFN_SHARDS_ROWS)

    def body(h_ref, g_ref, wg_ref, wu_ref, hn_ref, gg_ref, uu_ref, aa_ref):
        x = h_ref[...]
        hn = (x * _rstd(x) * g_ref[...]).astype(BF16)
        hn_ref[...] = hn
        for k in range(nk):
            g = _dot(hn, wg_ref[k])
            u = _dot(hn, wu_ref[k])
            s = _sigmoid(g)
            silu = g * s
            gg_ref[k] = (u * (s * (1.0 + g * (1.0 - s)))).astype(BF16)
            uu_ref[k] = silu.astype(BF16)
            aa_ref[k] = (silu * u).astype(BF16)

    wspec = _bs((nk, None, d, f), lambda i: (0, layer, 0, 0))
    aspec = _bs((nk, tm, f), lambda i: (0, i, 0))
    return _call(
        host, body, name=name, grid=(t // tm,),
        in_specs=[_bs((tm, d), lambda i: (i, 0)), _bs((1, d), lambda i: (0, 0)), wspec, wspec],
        out_specs=[_bs((tm, d), lambda i: (i, 0)), aspec, aspec, aspec],
        out_shape=[S((t, d), BF16), S((nk, t, f), BF16), S((nk, t, f), BF16), S((nk, t, f), BF16)],
        compiler_params=_cp("parallel"),
        args=(h, gain, wg, wu))


def mm_acc(a, a_spec, b, b_spec, res, scale, nk, t, n, tm, name, host=None):
    def body(a_ref, b_ref, r_ref, o_ref, acc):
        k = pl.program_id(1)

        @pl.when(k == 0)
        def _():
            acc[...] = jnp.zeros_like(acc)
        acc[...] += _dot(a_ref[...].astype(BF16), b_ref[...])

        @pl.when(k == nk - 1)
        def _():
            o_ref[...] = r_ref[...] + scale * acc[...]

    return _call(
        host, body, name=name, grid=(t // tm, nk),
        in_specs=[a_spec, b_spec, _bs((tm, n), lambda i, k: (i, 0))],
        out_specs=_bs((tm, n), lambda i, k: (i, 0)),
        out_shape=S((t, n), F32),
        scratch_shapes=[pltpu.VMEM((tm, n), F32)],
        compiler_params=_cp("parallel", "arbitrary"),
        args=(a, b, res))


def ffn_down(a, wd, layer, h, name, host=None, scale=0.5):
    nk, t, f = a.shape
    d = h.shape[1]
    tm = _row_tile(t, FFN_ROWS)

    def body(a_ref, w_ref, r_ref, o_ref):
        acc = _dot(a_ref[0], w_ref[0])
        for k in range(1, nk):
            acc = acc + _dot(a_ref[k], w_ref[k])
        o_ref[...] = r_ref[...] + scale * acc

    row = _bs((tm, d), lambda i: (i, 0))
    return _call(
        host, body, name=name, grid=(t // tm,),
        in_specs=[_bs((nk, tm, f), lambda i: (0, i, 0)), _bs((nk, None, f, d), lambda i: (0, layer, 0, 0)), row],
        out_specs=row, out_shape=S((t, d), F32),
        compiler_params=_cp("parallel"),
        args=(a, wd, h))


def ffn_bwd_act(dh, wd, layer, gg, uu, name, host=None):
    nk, t, f = gg.shape
    d = dh.shape[1]
    tm = _row_tile(t, FFN_SHARDS_ROWS)

    def body(dh_ref, wd_ref, g_ref, u_ref, dg_ref, du_ref, dh16_ref):
        dh16 = dh_ref[...].astype(BF16)
        dh16_ref[...] = dh16
        for k in range(nk):
            da = 0.5 * _dot_nt(dh16, wd_ref[k])
            dg_ref[k] = (da * g_ref[k].astype(F32)).astype(BF16)
            du_ref[k] = (da * u_ref[k].astype(F32)).astype(BF16)

    aspec = _bs((nk, tm, f), lambda i: (0, i, 0))
    row = _bs((tm, d), lambda i: (i, 0))
    return _call(
        host, body, name=name, grid=(t // tm,),
        in_specs=[row, _bs((nk, None, f, d), lambda i: (0, layer, 0, 0)), aspec, aspec],
        out_specs=[aspec, aspec, row],
        out_shape=[S((nk, t, f), BF16), S((nk, t, f), BF16), S((t, d), BF16)],
        compiler_params=_cp("parallel"),
        args=(dh, wd, gg, uu))


def nt_acc_normbwd(terms, nk, h, gain, dh, name, host=None):
    t, d = h.shape
    tm = _row_tile(t, BWD_IN_ROWS)
    sub = _row_tile(tm, 256)
    nterm = len(terms)
    picks = [term[4] for term in terms]

    def body(*refs):
        xs = refs[:2 * nterm]
        h_ref, g_ref, dh_ref, o_ref, dg_ref, acc = refs[2 * nterm:]

        @pl.when(pl.program_id(0) == 0)
        def _():
            dg_ref[...] = jnp.zeros_like(dg_ref)
        tot = None
        for j in range(nterm):
            for k in range(nk):
                part = _dot_nt(picks[j](xs[2 * j], k), xs[2 * j + 1][k])
                tot = part if tot is None else tot + part
        acc[...] = tot

        def rows_of(cidx, dgain):
            rows = pl.ds(pl.multiple_of(cidx * sub, sub), sub)
            dx, dgc = _rmsnorm_bwd(h_ref[rows, :], g_ref[...], acc[rows, :])
            o_ref[rows, :] = dh_ref[rows, :] + dx
            return dgain + dgc
        dg_ref[...] += lax.fori_loop(0, tm // sub, rows_of, jnp.zeros((1, d), F32))

    in_specs, args = [], []
    for x, xs_, w, ws_, _ in terms:
        in_specs += [xs_, ws_]
        args += [x, w]
    row = _bs((tm, d), lambda i: (i, 0))
    vec = _bs((1, d), lambda i: (0, 0))
    return _call(
        host, body, name=name, grid=(t // tm,),
        in_specs=in_specs + [row, vec, row],
        out_specs=[row, vec],
        out_shape=[S((t, d), F32), S((1, d), F32)],
        scratch_shapes=[pltpu.VMEM((tm, d), F32)],
        compiler_params=_cp("arbitrary"),
        args=(*args, h, gain, dh))


def ffn_bwd_in(dg, du, wg, wu, layer, h, gain, dh, name, host=None):
    nk, t, f = dg.shape
    d = h.shape[1]
    tm = _row_tile(t, BWD_IN_ROWS)
    aspec = _bs((nk, tm, f), lambda i: (0, i, 0))
    wspec = _bs((nk, None, d, f), lambda i: (0, layer, 0, 0))
    pick = lambda x_ref, k: x_ref[k]
    return nt_acc_normbwd([(dg, aspec, wg, wspec, pick), (du, aspec, wu, wspec, pick)], nk, h, gain, dh, name, host)


def tn_mm(x, x_spec, y, y_spec, nblk, t, ka, nb, out_shape, out_spec, scale, prev, name, host=None):
    tk = _row_tile(t, TN_ROWS if F32 in (x.dtype, y.dtype) else 2 * TN_ROWS)

    def body(*refs):
        if prev is None:
            x_ref, y_ref, o_ref, acc = refs
        else:
            x_ref, y_ref, _, o_ref, acc = refs
        j = pl.program_id(1)

        @pl.when(j == 0)
        def _():
            acc[...] = jnp.zeros_like(acc)
        acc[...] += _dot_tn(x_ref[...].astype(BF16), y_ref[...].astype(BF16))

        @pl.when(j == t // tk - 1)
        def _():
            o_ref[...] = (scale * acc[...]).astype(o_ref.dtype)

    in_specs = [x_spec(tk), y_spec(tk)]
    args = [x, y]
    aliases = {}
    if prev is not None:
        in_specs.append(pl.BlockSpec(memory_space=pl.ANY))
        args.append(prev)
        aliases = {2: 0}
    return _call(
        host, body, name=name, grid=(nblk, t // tk),
        in_specs=in_specs, out_specs=out_spec, out_shape=out_shape,
        scratch_shapes=[pltpu.VMEM((ka, nb), F32)],
        aliases=aliases,
        compiler_params=_cp("parallel", "arbitrary"),
        args=tuple(args))


def ffn_wgrads(which, hn, dh, aa, dg, du, layer, grads, run):
    nk, t, f = aa.shape
    d = hn.shape[1]
    hn_spec = lambda tk: _bs((tk, d), lambda k, j: (j, 0))
    a_spec = lambda tk: _bs((None, tk, f), lambda k, j: (k, j, 0))
    shape_gu, spec_gu = S((nk, 2, d, f), BF16), _bs((None, None, d, f), lambda k, j: (k, layer, 0, 0))
    shape_d, spec_d = S((nk, 2, f, d), BF16), _bs((None, None, f, d), lambda k, j: (k, layer, 0, 0))
    for suffix, x, xs, y, ys, ka, nb, shp, spec, scale in (
            ("gate", hn, hn_spec, dg, a_spec, d, f, shape_gu, spec_gu, 1.0),
            ("up", hn, hn_spec, du, a_spec, d, f, shape_gu, spec_gu, 1.0),
            ("down", aa, a_spec, dh, hn_spec, f, d, shape_d, spec_d, 0.5)):
        key = f"{which}_w_{suffix}"
        grads[key] = run(tn_mm, x, xs, y, ys, nk, t, ka, nb, shp, spec, scale, grads.get(key), name=f"{which}_gw_{layer}_{suffix}")


def norm_mm(h, gain, w, name, host=None):
    t, d = h.shape
    nb, _, bw = w.shape
    tm = _row_tile(t, FFN_ROWS)

    def body(h_ref, g_ref, w_ref, hn_ref, z_ref, hn_scr):
        @pl.when(pl.program_id(1) == 0)
        def _():
            x = h_ref[...]
            hn = (x * _rstd(x) * g_ref[...]).astype(BF16)
            hn_scr[...] = hn
            hn_ref[...] = hn
        z_ref[...] = _dot(hn_scr[...], w_ref[...])

    return _call(
        host, body, name=name, grid=(t // tm, nb),
        in_specs=[_bs((tm, d), lambda i, k: (i, 0)), _bs((1, d), lambda i, k: (0, 0)), _bs((None, d, bw), lambda i, k: (k, 0, 0))],
        out_specs=[_bs((tm, d), lambda i, k: (i, 0)), _bs((tm, bw), lambda i, k: (i, k))],
        out_shape=[S((t, d), BF16), S((t, nb * bw), F32)],
        scratch_shapes=[pltpu.VMEM((tm, d), BF16)],
        compiler_params=_cp("parallel", "arbitrary"),
        args=(h, gain, w))


def nt_mm(a, w, name):
    t, k = a.shape
    n = w.shape[0]
    tm = _row_tile(t, 512)

    def body(a_ref, w_ref, o_ref):
        o_ref[...] = _dot_nt(a_ref[...].astype(BF16), w_ref[...])

    return pl.pallas_call(
        body, name=name, grid=(t // tm,),
        in_specs=[_bs((tm, k), lambda i: (i, 0)), _bs((n, k), lambda i: (0, 0))],
        out_specs=_bs((tm, n), lambda i: (i, 0)),
        out_shape=S((t, n), F32),
        compiler_params=_cp("parallel"),
    )(a, w)


def _head_mean_matrix():
    m = np.kron(np.eye(N_HEADS, dtype=np.float32), np.full((HEAD_DIM, HEAD_DIM), 1.0 / HEAD_DIM, np.float32))
    return jnp.asarray(m, BF16)


def _head_sum_matrix():
    m = np.kron(np.eye(N_HEADS, dtype=np.float32), np.ones((HEAD_DIM, HEAD_DIM), np.float32))
    return jnp.asarray(m, BF16)


def _rel_bucket_np(dist):
    max_exact = REL_BUCKETS // 2
    n = np.maximum(dist, 1).astype(np.float32)
    large = max_exact + (np.log(n / np.float32(max_exact)) / np.float32(math.log(REL_MAX_DIST / max_exact))
                         * np.float32(REL_BUCKETS - max_exact)).astype(np.int32)
    large = np.minimum(large, REL_BUCKETS - 1)
    return np.where(dist < max_exact, dist, large)


def _band_tables():
    qi = np.arange(BAND)[:, None]
    kj = np.arange(2 * BAND)[None, :]
    dist_q = qi + BAND - kj
    qq = np.arange(2 * BAND)[:, None]
    kk = np.arange(BAND)[None, :]
    dist_k = qq - kk
    out = []
    for dist in (dist_q, dist_k):
        valid = (dist >= 0) & (dist <= BAND)
        bucket = np.stack([_rel_bucket_np(np.clip(dist, 0, BAND) * d) for d in DILATIONS])
        out.append((bucket, valid))
    return out


def band_bias(rel_bias):
    out = []
    for bucket, valid in _band_tables():
        bucket = np.where(valid[None], bucket, -1)[:, None]
        bucket_dev = lax.optimization_barrier(jnp.asarray(bucket, jnp.int32))
        tab = jnp.full((len(DILATIONS), N_HEADS) + bucket.shape[2:], NEG, F32)
        for b in range(REL_BUCKETS):
            if (bucket == b).any():
                tab = jnp.where(bucket_dev == b, rel_bias[b][None, :, None, None], tab)
        out.append(tab.reshape(len(DILATIONS), N_HEADS // 2, 2 * tab.shape[2], tab.shape[3]))
    return out


LANE_TILE = 128
N_LANE_TILES = ATTN_W // LANE_TILE


def _view_shape(t, dil):
    return (t // dil, dil * ATTN_W)


def _view_spec(tm, dil):
    return _bs((tm // dil, dil * ATTN_W), lambda i: (i, 0))


def _cols_to(scr, val):
    for cc in range(N_LANE_TILES):
        scr[cc] = val[:, LANE_TILE * cc:LANE_TILE * (cc + 1)]


def _cols_from(scr):
    return jnp.concatenate([scr[cc] for cc in range(N_LANE_TILES)], axis=1)


def _write_view(scr, out_ref, dil):
    if dil == 1:
        out_ref[...] = _cols_from(scr).astype(out_ref.dtype)
        return
    rows = scr.shape[1] // dil
    for r in range(dil):
        for cc in range(N_LANE_TILES):
            c0 = r * ATTN_W + LANE_TILE * cc
            out_ref[:, c0:c0 + LANE_TILE] = scr[cc, pl.ds(r, rows, stride=dil), :].astype(out_ref.dtype)


def _read_view(scr, in_ref, dil):
    if dil == 1:
        return in_ref[...].astype(F32)
    rows = scr.shape[1] // dil
    for r in range(dil):
        for cc in range(N_LANE_TILES):
            c0 = r * ATTN_W + LANE_TILE * cc
            scr[cc, pl.ds(r, rows, stride=dil), :] = in_ref[:, c0:c0 + LANE_TILE].astype(F32)
    return _cols_from(scr)


def hyb_prep(z, q_gain, k_gain, name):
    t = z.shape[0]
    tm = _row_tile(t, 512)
    seg = _head_mean_matrix()
    nd = len(DILATIONS)

    def body(q_ref, k_ref, v_ref, qg_ref, kg_ref, seg_ref, *rest):
        outs, scr = rest[:3 * nd], rest[3 * nd]
        q = q_ref[...]
        k = k_ref[...]
        vals = (q * lax.rsqrt(_seg_dot(q * q, seg_ref[...]) + EPS) * qg_ref[...],
                k * lax.rsqrt(_seg_dot(k * k, seg_ref[...]) + EPS) * kg_ref[...],
                v_ref[...])
        for j, val in enumerate(vals):
            _cols_to(scr, val)
            for g, dil in enumerate(DILATIONS):
                _write_view(scr, outs[3 * g + j], dil)

    col = lambda c: _bs((tm, ATTN_W), lambda i: (i, c))
    vec = _bs((1, ATTN_W), lambda i: (0, 0))
    res = pl.pallas_call(
        body, name=name, grid=(t // tm,),
        in_specs=[col(3), col(4), col(5), vec, vec, _bs((ATTN_W, ATTN_W), lambda i: (0, 0))],
        out_specs=[_view_spec(tm, dil) for dil in DILATIONS for _ in range(3)],
        out_shape=[S(_view_shape(t, dil), BF16) for dil in DILATIONS for _ in range(3)],
        scratch_shapes=[pltpu.VMEM((N_LANE_TILES, tm, LANE_TILE), F32)],
        compiler_params=_cp("parallel"),
    )(z, z, z, q_gain, k_gain, seg)
    return {dil: tuple(res[3 * g:3 * g + 3]) for g, dil in enumerate(DILATIONS)}


def _lane_lo(shape):
    return lax.broadcasted_iota(jnp.int32, shape, 1) < HEAD_DIM


def _stack_heads(pair):
    lo = _lane_lo(pair.shape)
    zero = jnp.zeros_like(pair)
    return jnp.concatenate([jnp.where(lo, pair, zero), jnp.where(lo, zero, pair)], axis=0)


def _unstack_heads(st):
    rows = st.shape[0] // 2
    return jnp.where(_lane_lo((rows, st.shape[1])), st[:rows], st[rows:])


def attn_fwd(q, k, v, bias, dil, name, host=None):
    qv, kv, vv = q, k, v
    sub = q.shape[0]
    nb = sub // BAND

    def body(q_ref, kp_ref, kc_ref, vp_ref, vc_ref, b_ref, o_ref, l_ref):
        first = pl.program_id(1) == 0
        colk = lax.broadcasted_iota(jnp.int32, (2 * BAND, 2 * BAND), 1)
        for j in range(N_HEADS // 2):
            sl = slice(2 * HEAD_DIM * j, 2 * HEAD_DIM * (j + 1))
            kk = jnp.concatenate([kp_ref[:, sl], kc_ref[:, sl]], axis=0)
            vv_ = jnp.concatenate([vp_ref[:, sl], vc_ref[:, sl]], axis=0)
            s = _dot_nt(_stack_heads(q_ref[:, sl]), kk) * (HEAD_DIM ** -0.5) + b_ref[j]
            s = jnp.where(jnp.logical_and(first, colk < BAND), NEG, s)
            m = jnp.max(s, axis=-1, keepdims=True)
            p = jnp.exp(s - m)
            l = jnp.sum(p, axis=-1, keepdims=True)
            o_ref[:, sl] = _unstack_heads(_dot(p.astype(BF16), vv_) / l).astype(o_ref.dtype)
            l_ref[:, sl] = _unstack_heads(jnp.broadcast_to(m + jnp.log(l), (2 * BAND, 2 * HEAD_DIM)))

    cur = _bs((BAND, ATTN_W), lambda r, n: (n, r))
    prv = _bs((BAND, ATTN_W), lambda r, n: (jnp.maximum(n - 1, 0), r))
    return _call(
        host, body, name=name, grid=(dil, nb),
        in_specs=[cur, prv, cur, prv, cur, _bs((N_HEADS // 2, 2 * BAND, 2 * BAND), lambda r, n: (0, 0, 0))],
        out_specs=[cur, cur],
        out_shape=[S((sub, dil * ATTN_W), BF16), S((sub, dil * ATTN_W), F32)],
        compiler_params=_cp("parallel", "arbitrary"),
        args=(qv, kv, kv, vv, vv, bias))


def hyb_post(z, conv_w, os_, lses, name):
    t = z.shape[0]
    tm = _row_tile(t, 512)
    nd = len(DILATIONS)

    def body(gb_ref, gc_ref, cx_ref, gch_ref, cxh_ref, w_ref, *rest):
        o_refs, l_refs = rest[:nd], rest[nd:2 * nd]
        y_ref, ya_ref = rest[2 * nd:2 * nd + 2]
        lt_refs, scr = rest[2 * nd + 2:3 * nd + 2], rest[3 * nd + 2]
        i = pl.program_id(0)
        m = gc_ref[...] * cx_ref[...]
        mh = jnp.where(i == 0, 0.0, gch_ref[...] * cxh_ref[...])
        conv = w_ref[0:1, :] * _shift_down(m, mh, 2) + w_ref[1:2, :] * _shift_down(m, mh, 1) + w_ref[2:3, :] * m
        y_ref[0] = (gb_ref[...] * conv).astype(BF16)
        ls = [_read_view(scr, l_refs[g], dil) for g, dil in enumerate(DILATIONS)]
        mx = functools.reduce(jnp.maximum, ls)
        es = [jnp.exp(l - mx) for l in ls]
        den = functools.reduce(lambda a, b: a + b, es)
        num = es[0] * _read_view(scr, o_refs[0], DILATIONS[0])
        for g in range(1, nd):
            num = num + es[g] * _read_view(scr, o_refs[g], DILATIONS[g])
        ya = num / den
        y_ref[1] = ya.astype(BF16)
        ya_ref[...] = ya
        _cols_to(scr, mx + jnp.log(den))
        for g, dil in enumerate(DILATIONS):
            _write_view(scr, lt_refs[g], dil)

    hb = tm // 8
    col = lambda c: _bs((tm, CONV_W), lambda i: (i, c))
    halo = lambda c: _bs((8, CONV_W), lambda i: (jnp.maximum(i * hb - 1, 0), c))
    row = _bs((tm, ATTN_W), lambda i: (i, 0))
    views = [_view_spec(tm, dil) for dil in DILATIONS]
    res = pl.pallas_call(
        body, name=name, grid=(t // tm,),
        in_specs=[col(0), col(1), col(2), halo(1), halo(2), _bs((3, CONV_W), lambda i: (0, 0))] + views * 2,
        out_specs=[_bs((2, tm, ATTN_W), lambda i: (0, i, 0)), row] + views,
        out_shape=[S((2, t, ATTN_W), BF16), S((t, ATTN_W), F32)] + [S(_view_shape(t, dil), F32) for dil in DILATIONS],
        scratch_shapes=[pltpu.VMEM((N_LANE_TILES, tm, LANE_TILE), F32)],
        compiler_params=_cp("parallel"),
    )(z, z, z, z, z, conv_w, *os_, *lses)
    return res[0], res[1], dict(zip(DILATIONS, res[2:]))


def attn_delta(dy, ya, name):
    t = ya.shape[0]
    tm = _row_tile(t, 512)
    seg = _head_sum_matrix()
    nd = len(DILATIONS)

    def body(dy_ref, ya_ref, seg_ref, *rest):
        dl_refs, db_refs, scr = rest[:nd], rest[nd:2 * nd], rest[2 * nd]
        dya = dy_ref[...]
        _cols_to(scr, _seg_dot(dya * ya_ref[...], seg_ref[...]))
        for g, dil in enumerate(DILATIONS):
            _write_view(scr, dl_refs[g], dil)
        _cols_to(scr, dya)
        for g, dil in enumerate(DILATIONS):
            _write_view(scr, db_refs[g], dil)

    row = _bs((tm, ATTN_W), lambda i: (i, 0))
    views = [_view_spec(tm, dil) for dil in DILATIONS]
    res = pl.pallas_call(
        body, name=name, grid=(t // tm,),
        in_specs=[_bs((tm, ATTN_W), lambda i: (i, 1)), row, _bs((ATTN_W, ATTN_W), lambda i: (0, 0))],
        out_specs=views * 2,
        out_shape=[S(_view_shape(t, dil), F32) for dil in DILATIONS] + [S(_view_shape(t, dil), BF16) for dil in DILATIONS],
        scratch_shapes=[pltpu.VMEM((N_LANE_TILES, tm, LANE_TILE), F32)],
        compiler_params=_cp("parallel"),
    )(dy, ya, seg)
    return dict(zip(DILATIONS, res[:nd])), dict(zip(DILATIONS, res[nd:]))


def attn_bwd_dq(q, k, v, dya, lt, delta, bias, dil, name, host=None):
    qv, kv, vv, dv_, lv, ev = q, k, v, dya, lt, delta
    sub = q.shape[0]
    nb = sub // BAND

    def body(q_ref, kp_ref, kc_ref, vp_ref, vc_ref, do_ref, l_ref, e_ref, b_ref, dq_ref, db_ref):
        r, n = pl.program_id(0), pl.program_id(1)

        @pl.when(jnp.logical_and(r == 0, n == 0))
        def _():
            db_ref[...] = jnp.zeros_like(db_ref)
        first = n == 0
        colk = lax.broadcasted_iota(jnp.int32, (2 * BAND, 2 * BAND), 1)
        for j in range(N_HEADS // 2):
            c0 = 2 * HEAD_DIM * j
            sl = slice(c0, c0 + 2 * HEAD_DIM)
            kk = jnp.concatenate([kp_ref[:, sl], kc_ref[:, sl]], axis=0)
            vv_ = jnp.concatenate([vp_ref[:, sl], vc_ref[:, sl]], axis=0)
            lse = jnp.concatenate([l_ref[:, c0:c0 + 1], l_ref[:, c0 + HEAD_DIM:c0 + HEAD_DIM + 1]], axis=0)
            dlt = jnp.concatenate([e_ref[:, c0:c0 + 1], e_ref[:, c0 + HEAD_DIM:c0 + HEAD_DIM + 1]], axis=0)
            s = _dot_nt(_stack_heads(q_ref[:, sl]), kk) * (HEAD_DIM ** -0.5) + b_ref[j]
            s = jnp.where(jnp.logical_and(first, colk < BAND), NEG, s)
            p = jnp.exp(s - lse)
            ds = p * (_dot_nt(_stack_heads(do_ref[:, sl]), vv_) - dlt)
            db_ref[j] += ds
            dq_ref[:, sl] = (_unstack_heads(_dot(ds.astype(BF16), kk)) * (HEAD_DIM ** -0.5)).astype(dq_ref.dtype)

    cur = _bs((BAND, ATTN_W), lambda r, n: (n, r))
    prv = _bs((BAND, ATTN_W), lambda r, n: (jnp.maximum(n - 1, 0), r))
    tab = _bs((N_HEADS // 2, 2 * BAND, 2 * BAND), lambda r, n: (0, 0, 0))
    dq, db = _call(
        host, body, name=name, grid=(dil, nb),
        in_specs=[cur, prv, cur, prv, cur, cur, cur, cur, tab],
        out_specs=[cur, tab],
        out_shape=[S((sub, dil * ATTN_W), BF16), S((N_HEADS // 2, 2 * BAND, 2 * BAND), F32)],
        compiler_params=_cp("arbitrary", "arbitrary"),
        args=(qv, kv, kv, vv, vv, dv_, lv, ev, bias))
    return dq, db.reshape(N_HEADS, BAND, 2 * BAND)


def attn_bwd_dkv(q, k, v, dya, lt, delta, bias_k, dil, name, host=None):
    qv, kv, vv, dv_, lv, ev = q, k, v, dya, lt, delta
    sub = q.shape[0]
    nb = sub // BAND

    def body(k_ref, v_ref, qc_ref, qn_ref, dc_ref, dn_ref, lc_ref, ln_ref, ec_ref, en_ref, b_ref, dk_ref, dv_ref):
        last = pl.program_id(1) == nb - 1
        rowq = lax.broadcasted_iota(jnp.int32, (4 * BAND, BAND), 0)
        from_next = (rowq & BAND) != 0
        for j in range(N_HEADS // 2):
            c0 = 2 * HEAD_DIM * j
            sl = slice(c0, c0 + 2 * HEAD_DIM)
            kp, vp = k_ref[:, sl], v_ref[:, sl]
            q4 = _stack_heads(jnp.concatenate([qc_ref[:, sl], qn_ref[:, sl]], axis=0))
            do4 = _stack_heads(jnp.concatenate([dc_ref[:, sl], dn_ref[:, sl]], axis=0))
            lse = jnp.concatenate([ref[:, c:c + 1] for c in (c0, c0 + HEAD_DIM) for ref in (lc_ref, ln_ref)], axis=0)
            dlt = jnp.concatenate([ref[:, c:c + 1] for c in (c0, c0 + HEAD_DIM) for ref in (ec_ref, en_ref)], axis=0)
            s = _dot_nt(q4, kp) * (HEAD_DIM ** -0.5) + b_ref[j]
            s = jnp.where(jnp.logical_and(last, from_next), NEG, s)
            p = jnp.exp(s - lse)
            ds = p * (_dot_nt(do4, vp) - dlt)
            dv_ref[:, sl] = _dot_tn(p.astype(BF16), do4).astype(dv_ref.dtype)
            dk_ref[:, sl] = (_dot_tn(ds.astype(BF16), q4) * (HEAD_DIM ** -0.5)).astype(dk_ref.dtype)

    cur = _bs((BAND, ATTN_W), lambda r, n: (n, r))
    nxt = _bs((BAND, ATTN_W), lambda r, n: (jnp.minimum(n + 1, nb - 1), r))
    tab = _bs((N_HEADS // 2, 4 * BAND, BAND), lambda r, n: (0, 0, 0))
    return _call(
        host, body, name=name, grid=(dil, nb),
        in_specs=[cur, cur, cur, nxt, cur, nxt, cur, nxt, cur, nxt, tab],
        out_specs=[cur, cur],
        out_shape=[S((sub, dil * ATTN_W), BF16)] * 2,
        compiler_params=_cp("parallel", "arbitrary"),
        args=(kv, vv, qv, qv, dv_, dv_, lv, lv, ev, ev, bias_k))


def hyb_dz(z, dy, conv_w, q_gain, k_gain, dqs, dks, dvs, name):
    t = z.shape[0]
    tm = _row_tile(t, 512)
    nt = t // tm
    seg = _head_mean_matrix()

    def body(gb_ref, gc_ref, cx_ref, q_ref, k_ref, gch_ref, cxh_ref, gbn_ref, dyc_ref, dyn_ref, w_ref, qg_ref, kg_ref, seg_ref,
             dq1, dq2, dq3, dk1, dk2, dk3, dv1, dv2, dv3, dz_ref, dw_ref, dqg_ref, dkg_ref, scr):
        i = pl.program_id(0)

        def total(parts):
            acc = _read_view(scr, parts[0], DILATIONS[0])
            for g in range(1, len(DILATIONS)):
                acc = acc + _read_view(scr, parts[g], DILATIONS[g])
            return acc

        @pl.when(i == 0)
        def _():
            dw_ref[...] = jnp.zeros_like(dw_ref)
            dqg_ref[...] = jnp.zeros_like(dqg_ref)
            dkg_ref[...] = jnp.zeros_like(dkg_ref)
        gb, gc, cx, dyc = gb_ref[...], gc_ref[...], cx_ref[...], dyc_ref[...]
        m = gc * cx
        mh = jnp.where(i == 0, 0.0, gch_ref[...] * cxh_ref[...])
        m1, m2 = _shift_down(m, mh, 1), _shift_down(m, mh, 2)
        conv = w_ref[0:1, :] * m2 + w_ref[1:2, :] * m1 + w_ref[2:3, :] * m
        dconv = dyc * gb
        dcn = jnp.where(i == nt - 1, 0.0, dyn_ref[...] * gbn_ref[...])
        dm = w_ref[2:3, :] * dconv + w_ref[1:2, :] * _shift_up(dconv, dcn, 1) + w_ref[0:1, :] * _shift_up(dconv, dcn, 2)
        dz_ref[:, 0:CONV_W] = (dyc * conv).astype(BF16)
        dz_ref[:, CONV_W:2 * CONV_W] = (dm * cx).astype(BF16)
        dz_ref[:, 2 * CONV_W:3 * CONV_W] = (dm * gc).astype(BF16)
        dw_ref[0:1, :] += jnp.sum(dconv * m2, axis=0, keepdims=True)
        dw_ref[1:2, :] += jnp.sum(dconv * m1, axis=0, keepdims=True)
        dw_ref[2:3, :] += jnp.sum(dconv * m, axis=0, keepdims=True)
        base = 3 * CONV_W
        for idx, (x_ref, g_ref, parts, dgain_ref) in enumerate(((q_ref, qg_ref, (dq1, dq2, dq3), dqg_ref),
                                                                  (k_ref, kg_ref, (dk1, dk2, dk3), dkg_ref))):
            x = x_ref[...]
            dxh = total(parts)
            r = lax.rsqrt(_seg_dot(x * x, seg_ref[...]) + EPS)
            xhat = x * r
            tt = dxh * g_ref[...]
            dx = r * (tt - xhat * _seg_dot(tt * xhat, seg_ref[...]))
            dz_ref[:, base + idx * ATTN_W:base + (idx + 1) * ATTN_W] = dx.astype(BF16)
            dgain_ref[...] += jnp.sum(dxh * xhat, axis=0, keepdims=True)
        dz_ref[:, base + 2 * ATTN_W:base + 3 * ATTN_W] = total((dv1, dv2, dv3)).astype(BF16)

    hb = tm // 8
    col = lambda c: _bs((tm, CONV_W), lambda i: (i, c))
    prev = lambda c: _bs((8, CONV_W), lambda i: (jnp.maximum(i * hb - 1, 0), c))
    nxt = lambda c: _bs((8, CONV_W), lambda i: (jnp.minimum((i + 1) * hb, t // 8 - 1), c))
    row = _bs((tm, ATTN_W), lambda i: (i, 0))
    vec = _bs((1, ATTN_W), lambda i: (0, 0))
    return pl.pallas_call(
        body, name=name, grid=(nt,),
        in_specs=[col(0), col(1), col(2), col(3), col(4), prev(1), prev(2), nxt(0), col(0), nxt(0),
                  _bs((3, CONV_W), lambda i: (0, 0)), vec, vec, _bs((ATTN_W, ATTN_W), lambda i: (0, 0))]
                 + [_view_spec(tm, dil) for dil in DILATIONS] * 3,
        out_specs=[_bs((tm, 6 * CONV_W), lambda i: (i, 0)), _bs((3, CONV_W), lambda i: (0, 0)), vec, vec],
        out_shape=[S((t, 6 * CONV_W), BF16), S((3, CONV_W), F32), S((1, ATTN_W), F32), S((1, ATTN_W), F32)],
        scratch_shapes=[pltpu.VMEM((N_LANE_TILES, tm, LANE_TILE), F32)],
        compiler_params=_cp("arbitrary"),
    )(z, z, z, z, z, z, z, z, dy, dy, conv_w, q_gain, k_gain, seg, *dqs, *dks, *dvs)


def rel_bias_grad(dbs, name):
    (bq, vq), _ = _band_tables()
    onehot = np.zeros((len(DILATIONS), REL_BUCKETS, BAND * 2 * BAND), np.float32)
    for g in range(len(DILATIONS)):
        idx = bq[g].reshape(-1)
        ok = vq.reshape(-1)
        onehot[g, idx[ok], np.nonzero(ok)[0]] = 1.0
    onehot = jnp.asarray(onehot, BF16)
    flat = [d.reshape(N_HEADS, BAND * 2 * BAND) for d in dbs]

    def body(oh_ref, d1, d2, d3, o_ref):
        acc = jnp.zeros((REL_BUCKETS, N_HEADS), F32)
        for g, d in enumerate((d1, d2, d3)):
            x = d[...]
            hi = x.astype(BF16)
            lo = (x - hi.astype(F32)).astype(BF16)
            acc += _dot_nt(oh_ref[g], hi) + _dot_nt(oh_ref[g], lo)
        o_ref[...] = acc

    full = lambda shp: _bs(shp, lambda: tuple(0 for _ in shp))
    return pl.pallas_call(
        body, name=name,
        in_specs=[full(onehot.shape)] + [full(flat[0].shape)] * 3,
        out_specs=full((REL_BUCKETS, N_HEADS)),
        out_shape=S((REL_BUCKETS, N_HEADS), F32),
        compiler_params=pltpu.CompilerParams(vmem_limit_bytes=VMEM_LIMIT),
    )(onehot, *flat)


def _lru_gates(xb, wa_ref, wx_ref, ba, bx):
    xb16 = xb.astype(BF16)
    ga = jnp.concatenate([_dot(xb16[:, LRU_BLOCK * g:LRU_BLOCK * (g + 1)], wa_ref[g]) for g in range(LRU_BLOCKS)], axis=1) + ba
    gx = jnp.concatenate([_dot(xb16[:, LRU_BLOCK * g:LRU_BLOCK * (g + 1)], wx_ref[g]) for g in range(LRU_BLOCKS)], axis=1) + bx
    return ga, gx


def _lru_coeffs(ga, gx, lam):
    sga = _sigmoid(ga)
    sp = _softplus(-lam)
    log_a = -LRU_C * sga * sp
    a = jnp.exp(log_a)
    one_m_a2 = _neg_expm1(2.0 * log_a)
    return sga, sp, a, one_m_a2, jnp.sqrt(one_m_a2), _sigmoid(gx)


def rec_fwd(z, conv_w, conv_b, wa, wx, ba, bx, lam, name, host=None):
    t = z.shape[0]
    w = z.shape[1] // 2
    tm = _row_tile(t, 256)

    def body(xp_ref, xh_ref, yb_ref, cw_ref, cb_ref, wa_ref, wx_ref, ba_ref, bx_ref, lam_ref,
             xb_ref, ga_ref, gx_ref, hs_ref, out_ref, carry):
        i = pl.program_id(0)

        @pl.when(i == 0)
        def _():
            carry[...] = jnp.zeros_like(carry)
        xp = xp_ref[...]
        xh = jnp.where(i == 0, 0.0, xh_ref[...])
        xb = cb_ref[...] + cw_ref[3:4, :] * xp
        for j in range(3):
            xb = xb + cw_ref[j:j + 1, :] * _shift_down(xp, xh, 3 - j)
        ga, gx = _lru_gates(xb, wa_ref, wx_ref, ba_ref[...], bx_ref[...])
        _, _, a, _, sq, sgx = _lru_coeffs(ga, gx, lam_ref[...])
        aa, bb = a, sq * sgx * xb
        s = 1
        while s < tm:
            bb = aa * _roll_fill(bb, s, 0.0, False) + bb
            aa = aa * _roll_fill(aa, s, 1.0, False)
            s *= 2
        hs = aa * carry[0:1, :] + bb
        xb_ref[...] = xb
        ga_ref[...] = ga
        gx_ref[...] = gx
        hs_ref[...] = hs
        carry[0:1, :] = hs_ref[tm - 1:tm, :]
        gy, _ = _gelu_and_grad(yb_ref[...])
        out_ref[...] = (hs * gy).astype(BF16)

    hb = tm // 8
    row = _bs((tm, w), lambda i: (i, 0))
    vec = _bs((1, w), lambda i: (0, 0))
    wsp = _bs((LRU_BLOCKS, LRU_BLOCK, LRU_BLOCK), lambda i: (0, 0, 0))
    return _call(
        host, body, name=name, grid=(t // tm,),
        in_specs=[row, _bs((8, w), lambda i: (jnp.maximum(i * hb - 1, 0), 0)), _bs((tm, w), lambda i: (i, 1)),
                  _bs((4, w), lambda i: (0, 0)), vec, wsp, wsp, vec, vec, vec],
        out_specs=[row] * 5,
        out_shape=[S((t, w), F32)] * 4 + [S((t, w), BF16)],
        scratch_shapes=[pltpu.VMEM((8, w), F32)],
        compiler_params=_cp("arbitrary"),
        args=(z, z, z, conv_w, conv_b, wa, wx, ba, bx, lam))


def rec_bwd(d_out, z, xb, ga, gx, hs, conv_w, wa, wx, lam, name, host=None):
    t = z.shape[0]
    w = z.shape[1] // 2
    tm = _row_tile(t, 256)
    nt = t // tm

    def body(do_ref, xp_ref, xph_ref, yb_ref, xb_ref, ga_ref, gx_ref, hs_ref, hsh_ref, cw_ref, wa_ref, wx_ref, lam_ref,
             dz_ref, dga_ref, dgx_ref, sm_ref, c_lam, c_a, c_dxb):
        i = pl.program_id(0)

        @pl.when(i == 0)
        def _():
            sm_ref[...] = jnp.zeros_like(sm_ref)
            c_lam[...] = jnp.zeros_like(c_lam)
            c_a[...] = jnp.zeros_like(c_a)
            c_dxb[...] = jnp.zeros_like(c_dxb)
        d_o, yb, xb, hs = do_ref[...], yb_ref[...], xb_ref[...], hs_ref[...]
        lam = lam_ref[...]
        gy, dgy = _gelu_and_grad(yb)
        dz_ref[:, w:2 * w] = (d_o * hs * dgy).astype(BF16)
        sga, sp, a, one_m_a2, sq, sgx = _lru_coeffs(ga_ref[...], gx_ref[...], lam)
        aa = _shift_up(a, c_a[...], 1)
        bb = d_o * gy
        s = 1
        while s < tm:
            bb = aa * _roll_fill(bb, s, 0.0, True) + bb
            aa = aa * _roll_fill(aa, s, 1.0, True)
            s *= 2
        lmb = aa * c_lam[0:1, :] + bb
        c_a[...] = a[0:8, :]
        c_lam[...] = lmb[0:8, :]
        hprev = _shift_down(hs, jnp.where(i == nt - 1, 0.0, hsh_ref[...]), 1)
        d_sq = lmb * sgx * xb
        d_sgx = lmb * sq * xb
        d_log_a = lmb * hprev * a - d_sq * (1.0 - one_m_a2) / sq
        dga = d_log_a * (-LRU_C * sp) * sga * (1.0 - sga)
        dgx = d_sgx * sgx * (1.0 - sgx)
        dga16, dgx16 = dga.astype(BF16), dgx.astype(BF16)
        dga_ref[...] = dga16
        dgx_ref[...] = dgx16
        dxb = lmb * sq * sgx + jnp.concatenate(
            [_dot_nt(dga16[:, LRU_BLOCK * g:LRU_BLOCK * (g + 1)], wa_ref[g]) + _dot_nt(dgx16[:, LRU_BLOCK * g:LRU_BLOCK * (g + 1)], wx_ref[g])
             for g in range(LRU_BLOCKS)], axis=1)
        nxt = c_dxb[...]
        dxp = cw_ref[3:4, :] * dxb
        for j in range(3):
            dxp = dxp + cw_ref[j:j + 1, :] * _shift_up(dxb, nxt, 3 - j)
        c_dxb[...] = dxb[0:8, :]
        dz_ref[:, 0:w] = dxp.astype(BF16)
        xp = xp_ref[...]
        xph = jnp.where(i == nt - 1, 0.0, xph_ref[...])
        sm_ref[0:1, :] += jnp.sum(dga, axis=0, keepdims=True)
        sm_ref[1:2, :] += jnp.sum(dgx, axis=0, keepdims=True)
        sm_ref[2:3, :] += jnp.sum(d_log_a * (-LRU_C * sga), axis=0, keepdims=True) * (-_sigmoid(-lam))
        sm_ref[3:4, :] += jnp.sum(dxb, axis=0, keepdims=True)
        for j in range(4):
            sm_ref[4 + j:5 + j, :] += jnp.sum(dxb * _shift_down(xp, xph, 3 - j), axis=0, keepdims=True)

    hb = tm // 8
    rev = lambda c: _bs((tm, w), lambda i: (nt - 1 - i, c))
    halo = lambda c: _bs((8, w), lambda i: (jnp.maximum((nt - 1 - i) * hb - 1, 0), c))
    vec = _bs((1, w), lambda i: (0, 0))
    wsp = _bs((LRU_BLOCKS, LRU_BLOCK, LRU_BLOCK), lambda i: (0, 0, 0))
    return _call(
        host, body, name=name, grid=(nt,),
        in_specs=[rev(0), rev(0), halo(0), rev(1), rev(0), rev(0), rev(0), rev(0), halo(0),
                  _bs((4, w), lambda i: (0, 0)), wsp, wsp, vec],
        out_specs=[_bs((tm, 2 * w), lambda i: (nt - 1 - i, 0)), rev(0), rev(0), _bs((8, w), lambda i: (0, 0))],
        out_shape=[S((t, 2 * w), BF16), S((t, w), BF16), S((t, w), BF16), S((8, w), F32)],
        scratch_shapes=[pltpu.VMEM((8, w), F32)] * 3,
        compiler_params=_cp("arbitrary"),
        args=(d_out, z, z, z, xb, ga, gx, hs, hs, conv_w, wa, wx, lam))


def ple_fwd(h, gain, wpg, layer, p, p_layer, wpp, name, target=None):
    t, d = h.shape
    pd = p.shape[2]
    nk, _, rb, _ = wpg.shape
    cb = wpp.shape[3]
    tm = _row_tile(t, 512)
    row = _bs((tm, d), lambda i: (i, 0))
    in_specs = [row, _bs((1, d), lambda i: (0, 0)), _bs((nk, None, rb, d), lambda i: (0, layer, 0, 0)),
                _bs((None, tm, pd), lambda i: (p_layer, i, 0)), _bs((nk, None, pd, cb), lambda i: (0, layer, 0, 0))]

    def forward(h_ref, g_ref, wg_ref, p_ref, wp_ref, hn_ref, gp_ref, pp_ref):
        x = h_ref[...]
        hn = (x * _rstd(x) * g_ref[...]).astype(BF16)
        gp = _dot(hn[:, 0:rb], wg_ref[0])
        for k in range(1, nk):
            gp = gp + _dot(hn[:, rb * k:rb * (k + 1)], wg_ref[k])
        pp = jnp.concatenate([_dot(p_ref[...].astype(BF16), wp_ref[k]) for k in range(nk)], axis=1)
        hn_ref[...] = hn
        gp_ref[...] = gp
        pp_ref[...] = pp
        return x + _sigmoid(gp) * pp

    if target is not None:
        def body_loss(h_ref, g_ref, wg_ref, p_ref, wp_ref, t_ref, l_ref, dy_ref, hn_ref, gp_ref, pp_ref):
            @pl.when(pl.program_id(0) == 0)
            def _():
                l_ref[...] = jnp.zeros_like(l_ref)
            err = forward(h_ref, g_ref, wg_ref, p_ref, wp_ref, hn_ref, gp_ref, pp_ref) - t_ref[...]
            dy_ref[...] = err * (1.0 / d)
            l_ref[...] += jnp.sum(jnp.sum(err * err, axis=1, keepdims=True), axis=0, keepdims=True) * (0.5 / d)

        return pl.pallas_call(
            body_loss, name=name, grid=(t // tm,),
            in_specs=in_specs + [row],
            out_specs=[_bs((1, 1), lambda i: (0, 0))] + [row] * 4,
            out_shape=[S((1, 1), F32), S((t, d), F32), S((t, d), BF16), S((t, d), F32), S((t, d), F32)],
            compiler_params=_cp("arbitrary"),
        )(h, gain, wpg, p, wpp, target)

    def body(h_ref, g_ref, wg_ref, p_ref, wp_ref, o_ref, hn_ref, gp_ref, pp_ref):
        o_ref[...] = forward(h_ref, g_ref, wg_ref, p_ref, wp_ref, hn_ref, gp_ref, pp_ref)

    return pl.pallas_call(
        body, name=name, grid=(t // tm,),
        in_specs=in_specs,
        out_specs=[row] * 4,
        out_shape=[S((t, d), F32), S((t, d), BF16), S((t, d), F32), S((t, d), F32)],
        compiler_params=_cp("parallel"),
    )(h, gain, wpg, p, wpp)


def ple_bwd(dh, h, gain, wpg, layer, gp, pp, name, host=None):
    t, d = h.shape
    nk, _, rb, _ = wpg.shape
    tm = _row_tile(t, 512)

    def body(dh_ref, h_ref, g_ref, wg_ref, gp_ref, pp_ref, o_ref, dgp_ref, dpp_ref, dg_ref):
        @pl.when(pl.program_id(0) == 0)
        def _():
            dg_ref[...] = jnp.zeros_like(dg_ref)
        d_h = dh_ref[...]
        gate = _sigmoid(gp_ref[...])
        dgp = (d_h * pp_ref[...] * gate * (1.0 - gate)).astype(BF16)
        dgp_ref[...] = dgp
        dpp_ref[...] = (d_h * gate).astype(BF16)
        dhn = jnp.concatenate([_dot_nt(dgp, wg_ref[k]) for k in range(nk)], axis=1)
        dx, dgain = _rmsnorm_bwd(h_ref[...], g_ref[...], dhn)
        o_ref[...] = d_h + dx
        dg_ref[...] += dgain

    row = _bs((tm, d), lambda i: (i, 0))
    vec = _bs((1, d), lambda i: (0, 0))
    return _call(
        host, body, name=name, grid=(t // tm,),
        in_specs=[row, row, vec, _bs((nk, None, rb, d), lambda i: (0, layer, 0, 0)), row, row],
        out_specs=[row, row, row, vec],
        out_shape=[S((t, d), F32), S((t, d), BF16), S((t, d), BF16), S((1, d), F32)],
        compiler_params=_cp("arbitrary"),
        args=(dh, h, gain, wpg, gp, pp))


def _vec(a, i):
    return a[i:i + 1]


def local_step(x, p, target, w, plan=None):
    t = x.shape[0]
    tm = _row_tile(t, 512)
    grads = {}
    if plan is not None:
        plan.grads = grads
    saved = []
    h = x
    bias_q, bias_k = band_bias(w["rel_bias"])
    qg = jnp.tile(w["hyb_q_gain"], (1, N_HEADS))
    kg = jnp.tile(w["hyb_k_gain"], (1, N_HEADS))

    def run(fn, *a, name):
        hst = plan.host(name) if plan is not None else None
        out = fn(*a, name, hst)
        if hst is not None:
            plan.done(hst)
        return out

    def lru_blocks(n):
        return jnp.transpose(w[n].reshape(N_SHARD, LRU_BLOCKS, 64, LRU_BLOCK), (1, 0, 2, 3)).reshape(LRU_BLOCKS, LRU_BLOCK, LRU_BLOCK)

    for i in range(2):
        s = {}
        s["h0"] = h
        s["hn1"], s["g1"], s["u1"], s["a1"] = run(ffn_up, h, _vec(w["ffn1_norm"], i), w[f"ffn1_w_gate/{i}"], w[f"ffn1_w_up/{i}"], 0, name=f"ffn1_up_{i}")
        h = run(ffn_down, s["a1"], w[f"ffn1_w_down/{i}"], 0, h, name=f"ffn1_down_{i}")
        s["h1"] = h
        if i == 0:
            w_hyb_in = w["hyb_w_in"].reshape(N_SHARD, D_MODEL, -1)
            w_hyb_out = w["hyb_w_out"].reshape(D_MODEL, D_MODEL)
            s["hnm"], s["z"] = run(norm_mm, h, _vec(w["mix_norm"], i), w_hyb_in, name="hyb_in")
            s["qkv"] = hyb_prep(s["z"], qg, kg, "hyb_prep")
            os_, lses = [], []
            for g, dil in enumerate(DILATIONS):
                o, l = run(attn_fwd, *s["qkv"][dil], bias_q[g], dil, name=f"attn_fwd_{dil}")
                os_.append(o)
                lses.append(l)
            s["y2"], s["ya"], s["lt"] = hyb_post(s["z"], w["hyb_conv_w"], os_, lses, "hyb_post")
            h = run(functools.partial(ffn_down, scale=1.0), s["y2"], w_hyb_out.reshape(2, 1, ATTN_W, D_MODEL), 0, h, name="hyb_out")
        else:
            w_rec_in = w["rec_w_in"].reshape(N_SHARD, D_MODEL, -1)
            s["hnm"], s["z"] = run(norm_mm, h, _vec(w["mix_norm"], i), w_rec_in, name="rec_in")
            w_rec_out = w["rec_w_out"].reshape(D_MODEL, D_MODEL)
            lru_wa, lru_wx = lru_blocks("lru_wa"), lru_blocks("lru_wx")
            s["xb"], s["ga"], s["gx"], s["hs"], s["ro"] = run(
                rec_fwd, s["z"], w["rec_conv_w"], w["rec_conv_b"], lru_wa, lru_wx, w["lru_ba"], w["lru_bx"], w["lru_lambda"], name="rec_fwd")
            h = run(mm_acc, s["ro"], _bs((tm, D_MODEL), lambda r, k: (r, 0)), w_rec_out, _bs((D_MODEL, D_MODEL), lambda r, k: (0, 0)),
                    h, 1.0, 1, t, D_MODEL, tm, name="rec_out")
        s["h2"] = h
        s["hn2"], s["g2"], s["u2"], s["a2"] = run(ffn_up, h, _vec(w["ffn2_norm"], i), w[f"ffn2_w_gate/{i}"], w[f"ffn2_w_up/{i}"], 0, name=f"ffn2_up_{i}")
        h = run(ffn_down, s["a2"], w[f"ffn2_w_down/{i}"], 0, h, name=f"ffn2_down_{i}")
        s["h3"] = h
        if i == 0:
            h, s["hnp"], s["gp"], s["pp"] = ple_fwd(h, _vec(w["ple_norm"], i), w[f"ple_w_gate/{i}"], 0, p, i, w[f"ple_w_proj/{i}"], f"ple_fwd_{i}")
        else:
            loss, dh, s["hnp"], s["gp"], s["pp"] = ple_fwd(h, _vec(w["ple_norm"], i), w[f"ple_w_gate/{i}"], 0, p, i, w[f"ple_w_proj/{i}"],
                                                           f"ple_fwd_{i}", target)
        saved.append(s)

    norm_g = {n: [None, None] for n in ("ffn1_norm", "mix_norm", "ffn2_norm", "ple_norm")}
    for i in (1, 0):
        s = saved[i]
        dh_out = dh
        dh, dgp, dpp, norm_g["ple_norm"][i] = run(ple_bwd, dh_out, s["h3"], _vec(w["ple_norm"], i), w[f"ple_w_gate/{i}"], 0, s["gp"], s["pp"],
                                                  name=f"ple_bwd_{i}")
        grads["ple_w_gate"] = run(tn_mm, s["hnp"], lambda tk: _bs((tk, 256), lambda k, j: (j, k)), dgp, lambda tk: _bs((tk, D_MODEL), lambda k, j: (j, 0)),
                                  N_SHARD, t, 256, D_MODEL, S((N_SHARD, 2, 256, D_MODEL), BF16),
                                  _bs((None, None, 256, D_MODEL), lambda k, j, i=i: (k, i, 0, 0)), 1.0, grads.get("ple_w_gate"), name=f"ple_gw_gate_{i}")
        grads["ple_w_proj"] = run(tn_mm, p, lambda tk, i=i: _bs((None, tk, 256), lambda k, j: (i, j, 0)), dpp, lambda tk: _bs((tk, 256), lambda k, j: (j, k)),
                                  N_SHARD, t, 256, 256, S((N_SHARD, 2, 256, 256), BF16),
                                  _bs((None, None, 256, 256), lambda k, j, i=i: (k, i, 0, 0)), 1.0, grads.get("ple_w_proj"), name=f"ple_gw_proj_{i}")
        dh_out = dh
        dg, du, dh16 = run(ffn_bwd_act, dh_out, w[f"ffn2_w_down/{i}"], 0, s["g2"], s["u2"], name=f"ffn2_bwd_act_{i}")
        ffn_wgrads("ffn2", s["hn2"], dh16, s["a2"], dg, du, i, grads, run)
        dh, norm_g["ffn2_norm"][i] = run(ffn_bwd_in, dg, du, w[f"ffn2_w_gate/{i}"], w[f"ffn2_w_up/{i}"], 0, s["h2"], _vec(w["ffn2_norm"], i), dh_out,
                                         name=f"ffn2_bwd_in_{i}")
        dh_out = dh
        if i == 1:
            d_o = nt_mm(dh_out, w_rec_out, "rec_bwd_out")
            grads["rec_w_out"] = run(tn_mm, s["ro"], lambda tk: _bs((tk, 256), lambda k, j: (j, k)), dh_out, lambda tk: _bs((tk, D_MODEL), lambda k, j: (j, 0)),
                                     N_SHARD, t, 256, D_MODEL, S((N_SHARD, 256, D_MODEL), BF16), _bs((None, 256, D_MODEL), lambda k, j: (k, 0, 0)),
                                     1.0, None, name="rec_gw_out").reshape(N_SHARD, 1, 256, D_MODEL)
            dz, dga, dgx, small = run(rec_bwd, d_o, s["z"], s["xb"], s["ga"], s["gx"], s["hs"], w["rec_conv_w"], lru_wa, lru_wx, w["lru_lambda"],
                                      name="rec_bwd")
            blk = lambda tk: _bs((tk, LRU_BLOCK), lambda k, j: (j, k))
            for nm, dgt in (("lru_wa", dga), ("lru_wx", dgx)):
                gw = run(tn_mm, s["xb"], blk, dgt, blk, LRU_BLOCKS, t, LRU_BLOCK, LRU_BLOCK, S((LRU_BLOCKS, LRU_BLOCK, LRU_BLOCK), BF16),
                         _bs((None, LRU_BLOCK, LRU_BLOCK), lambda k, j: (k, 0, 0)), 1.0, None, name="rec_gw_" + nm)
                grads[nm] = jnp.transpose(gw.reshape(LRU_BLOCKS, N_SHARD, 64, LRU_BLOCK), (1, 0, 2, 3)).reshape(N_SHARD, 1, LRU_BLOCKS, 64, LRU_BLOCK)
            grads["lru_ba"], grads["lru_bx"], grads["lru_lambda"], grads["rec_conv_b"] = (small[r:r + 1] for r in range(4))
            grads["rec_conv_w"] = small[4:8]
            nb_, bw = N_SHARD, 512
            w_in, nm_in = w_rec_in, "rec_w_in"
        else:
            dy = nt_mm(dh_out, w_hyb_out, "hyb_bwd_out")
            grads["hyb_w_out"] = run(tn_mm, s["y2"], lambda tk: _bs((None, tk, 256), lambda k, j: (k // 2, j, k % 2)), dh_out,
                                     lambda tk: _bs((tk, D_MODEL), lambda k, j: (j, 0)),
                                     N_SHARD, t, 256, D_MODEL, S((N_SHARD, 256, D_MODEL), BF16), _bs((None, 256, D_MODEL), lambda k, j: (k, 0, 0)),
                                     1.0, None, name="hyb_gw_out").reshape(N_SHARD, 1, 256, D_MODEL)
            delta, dya = attn_delta(dy, s["ya"], "attn_delta")
            dqs, dks, dvs, dbs = [], [], [], []
            for g, dil in enumerate(DILATIONS):
                dq, db = run(attn_bwd_dq, *s["qkv"][dil], dya[dil], s["lt"][dil], delta[dil], bias_q[g], dil, name=f"attn_bwd_dq_{dil}")
                dk, dv = run(attn_bwd_dkv, *s["qkv"][dil], dya[dil], s["lt"][dil], delta[dil], bias_k[g], dil, name=f"attn_bwd_dkv_{dil}")
                dqs.append(dq); dks.append(dk); dvs.append(dv); dbs.append(db)
            grads["rel_bias"] = rel_bias_grad(dbs, "rel_bias_grad")
            dz, grads["hyb_conv_w"], dqg, dkg = hyb_dz(s["z"], dy, w["hyb_conv_w"], qg, kg, dqs, dks, dvs, "hyb_dz")
            grads["hyb_q_gain"] = jnp.sum(dqg.reshape(N_HEADS, HEAD_DIM), axis=0, keepdims=True)
            grads["hyb_k_gain"] = jnp.sum(dkg.reshape(N_HEADS, HEAD_DIM), axis=0, keepdims=True)
            nb_, bw = N_SHARD, 768
            w_in, nm_in = w_hyb_in, "hyb_w_in"
        grads[nm_in] = run(tn_mm, s["hnm"], lambda tk: _bs((tk, D_MODEL), lambda k, j: (j, 0)), dz, lambda tk, bw=bw: _bs((tk, bw), lambda k, j: (j, k)),
                           nb_, t, D_MODEL, bw, S((nb_, D_MODEL, bw), BF16), _bs((None, D_MODEL, bw), lambda k, j: (k, 0, 0)),
                           1.0, None, name=f"mix_gw_in_{i}").reshape(nb_, 1, D_MODEL, bw)
        dh, norm_g["mix_norm"][i] = run(
            nt_acc_normbwd, [(dz, _bs((_row_tile(t, BWD_IN_ROWS), nb_ * bw), lambda r: (r, 0)), w_in, _bs((nb_, D_MODEL, bw), lambda r: (0, 0, 0)),
                              lambda x_ref, k, bw=bw: x_ref[:, k * bw:(k + 1) * bw])],
            nb_, s["h1"], _vec(w["mix_norm"], i), dh_out, name=f"mix_bwd_in_{i}")
        dh_out = dh
        dg, du, dh16 = run(ffn_bwd_act, dh_out, w[f"ffn1_w_down/{i}"], 0, s["g1"], s["u1"], name=f"ffn1_bwd_act_{i}")
        ffn_wgrads("ffn1", s["hn1"], dh16, s["a1"], dg, du, i, grads, run)
        dh, norm_g["ffn1_norm"][i] = run(ffn_bwd_in, dg, du, w[f"ffn1_w_gate/{i}"], w[f"ffn1_w_up/{i}"], 0, s["h0"], _vec(w["ffn1_norm"], i), dh_out,
                                         name=f"ffn1_bwd_in_{i}")
    for n, (g0, g1) in norm_g.items():
        grads[n] = jnp.concatenate([g0, g1], axis=0)
    return loss, dh, grads


def gather_weights(shards, name):
    n = len(shards)

    def body(*refs):
        ins, outs = refs[:n], refs[n:2 * n]
        send1, recv1, send2, recv2, lsem = refs[2 * n:]
        x, y, c, k, chips, kk = _place()
        sib = (x, y, 1 - c)

        def remote(src, dst, ssem, rsem, to):
            return pltpu.make_async_remote_copy(src_ref=src, dst_ref=dst, send_sem=ssem, recv_sem=rsem, device_id=to, device_id_type=MESH)

        local = [pltpu.make_async_copy(ins[a], outs[a].at[k], lsem.at[a]) for a in range(n)]
        for cp in local:
            cp.start()
        sends = []
        for a in range(n):
            for j, chip in enumerate(chips):
                cp = remote(ins[a].at[c], outs[a].at[k, c], send1.at[3 * a + j], recv1.at[3 * a + j], (*chip, c))
                cp.start()
                sends.append(cp)
        for a in range(n):
            for j, chip in enumerate(chips):
                remote(ins[a].at[c], outs[a].at[kk[j], c], send1.at[3 * a + j], recv1.at[3 * a + j], (*chip, c)).wait_recv()
                cp = remote(outs[a].at[kk[j], c], outs[a].at[kk[j], c], send2.at[3 * a + j], recv2.at[3 * a + j], sib)
                cp.start()
                sends.append(cp)
        for a in range(n):
            for j in range(3):
                remote(outs[a].at[kk[j], 1 - c], outs[a].at[kk[j], 1 - c], send2.at[3 * a + j], recv2.at[3 * a + j], sib).wait_recv()
        for cp in sends:
            cp.wait_send()
        for cp in local:
            cp.wait()

    return pl.pallas_call(
        body, name=name,
        in_specs=[_ANY] * n, out_specs=[_ANY] * n,
        out_shape=[S((N_SHARD,) + s.shape, s.dtype) for s in shards],
        scratch_shapes=[pltpu.SemaphoreType.DMA((3 * n,))] * 4 + [pltpu.SemaphoreType.DMA((n,))],
    )(*shards)


def exchange_cores(rs, name):
    n = len(rs)

    def body(*refs):
        outs = refs[n:2 * n]
        send, recv = refs[2 * n:]
        x, y, c = lax.axis_index("x"), lax.axis_index("y"), lax.axis_index("c")
        sends = []
        for a in range(n):
            for k in range(N_SHARD):
                slot = outs[a].at[2 * k + c]
                cp = _remote(slot, slot, send.at[N_SHARD * a + k], recv.at[N_SHARD * a + k], (x, y, 1 - c))
                cp.start()
                sends.append(cp)
        for a in range(n):
            for k in range(N_SHARD):
                slot = outs[a].at[2 * k + 1 - c]
                _remote(slot, slot, send.at[N_SHARD * a + k], recv.at[N_SHARD * a + k], (x, y, 1 - c)).wait_recv()
        for cp in sends:
            cp.wait_send()

    return pl.pallas_call(
        body, name=name,
        in_specs=[_ANY] * n, out_specs=[_ANY] * n,
        out_shape=[S(r.shape, r.dtype) for r in rs],
        input_output_aliases={a: a for a in range(n)},
        scratch_shapes=[pltpu.SemaphoreType.DMA((N_SHARD * n,))] * 2,
    )(*rs)


def allgather8(a, name):
    def body(a_ref, o_ref, send, recv, lsem):
        x, y, c = lax.axis_index("x"), lax.axis_index("y"), lax.axis_index("c")
        me = 4 * x + 2 * y + c
        local = pltpu.make_async_copy(a_ref, o_ref.at[me], lsem)
        local.start()
        cps = []
        for f in range(1, N_DEV):
            fx, fy, fc = (f >> 2) & 1, (f >> 1) & 1, f & 1
            peer = (1 - x if fx else x, 1 - y if fy else y, 1 - c if fc else c)
            cp = pltpu.make_async_remote_copy(src_ref=a_ref, dst_ref=o_ref.at[me], send_sem=send.at[f - 1], recv_sem=recv.at[f - 1],
                                              device_id=peer, device_id_type=MESH)
            cp.start()
            cps.append((cp, 4 * peer[0] + 2 * peer[1] + peer[2], f))
        for cp, pidx, f in cps:
            pltpu.make_async_remote_copy(src_ref=a_ref, dst_ref=o_ref.at[pidx], send_sem=send.at[f - 1], recv_sem=recv.at[f - 1],
                                         device_id=(x, y, c), device_id_type=MESH).wait_recv()
        for cp, _, _ in cps:
            cp.wait_send()
        local.wait()

    return pl.pallas_call(
        body, name=name, in_specs=[_ANY], out_specs=_ANY,
        out_shape=S((N_DEV,) + a.shape, a.dtype),
        scratch_shapes=[pltpu.SemaphoreType.DMA((N_DEV - 1,)), pltpu.SemaphoreType.DMA((N_DEV - 1,)), pltpu.SemaphoreType.DMA],
    )(a)


def sum8(a, name):
    _, r, c = a.shape

    def body(a_ref, o_ref):
        acc = a_ref[0]
        for j in range(1, N_DEV):
            acc = acc + a_ref[j]
        o_ref[...] = acc

    return pl.pallas_call(
        body, name=name, in_specs=[_bs((N_DEV, r, c), lambda: (0, 0, 0))], out_specs=_bs((r, c), lambda: (0, 0)),
        out_shape=S((r, c), F32),
    )(a)


def adamw(w, m, v, g, name):
    nl, r, c = w.shape
    tr = _row_tile(r, 256)
    summed = g.ndim == 4

    def body(w_ref, m_ref, v_ref, g_ref, go_ref, d_ref, mo_ref, vo_ref):
        if summed:
            gr = g_ref[0].astype(F32)
            for j in range(1, N_DEV):
                gr = gr + g_ref[j].astype(F32)
        else:
            gr = g_ref[...]
        m_new = ADAM_B1 * m_ref[...] + (1.0 - ADAM_B1) * gr
        v_new = ADAM_B2 * v_ref[...] + (1.0 - ADAM_B2) * (gr * gr)
        m_hat = m_new / (1.0 - ADAM_B1 ** ADAM_STEP)
        v_hat = v_new / (1.0 - ADAM_B2 ** ADAM_STEP)
        go_ref[...] = gr
        d_ref[...] = -ADAM_LR * (m_hat / (jnp.sqrt(v_hat) + ADAM_EPS) + ADAM_WD * w_ref[...])
        mo_ref[...] = m_new
        vo_ref[...] = v_new

    row = _bs((None, tr, c), lambda l, i: (l, i, 0))
    gspec = _bs((N_DEV, None, tr, c), lambda l, i: (0, l, i, 0)) if summed else row
    return pl.pallas_call(
        body, name=name, grid=(nl, r // tr),
        in_specs=[row, row, row, gspec], out_specs=[row] * 4, out_shape=[S((nl, r, c), F32)] * 4,
        compiler_params=_cp("parallel", "parallel"),
    )(w, m, v, g)


WEIGHTS = ["rel_bias", "ffn1_norm", "ffn1_w_gate", "ffn1_w_up", "ffn1_w_down", "mix_norm", "hyb_w_in", "hyb_conv_w", "hyb_q_gain",
           "hyb_k_gain", "hyb_w_out", "rec_w_in", "rec_conv_w", "rec_conv_b", "lru_wa", "lru_ba", "lru_wx", "lru_bx", "lru_lambda",
           "rec_w_out", "ffn2_norm", "ffn2_w_gate", "ffn2_w_up", "ffn2_w_down", "ple_norm", "ple_w_gate", "ple_w_proj"]
BIG = ["ffn1_w_gate", "ffn1_w_up", "ffn1_w_down", "hyb_w_in", "hyb_w_out", "rec_w_in", "lru_wa", "lru_wx", "rec_w_out",
       "ffn2_w_gate", "ffn2_w_up", "ffn2_w_down", "ple_w_gate", "ple_w_proj"]
SMALL_SHARDED = ["hyb_conv_w", "rec_conv_w", "rec_conv_b", "lru_ba", "lru_bx", "lru_lambda"]
SMALL = ["rel_bias", "ffn1_norm", "mix_norm", "ffn2_norm", "ple_norm", "hyb_q_gain", "hyb_k_gain"] + SMALL_SHARDED
PACK_W = 1024
PER_LAYER = ["ffn1_w_gate", "ffn1_w_up", "ffn1_w_down", "ffn2_w_gate", "ffn2_w_up", "ffn2_w_down", "ple_w_gate", "ple_w_proj"]
FIRST = ["ffn1_w_gate/0", "ffn1_w_up/0"]
LAST = ["ffn1_w_down"]
GATHER_PLAN = {
    "ffn1_up_0": ["ffn1_w_down/0", "hyb_w_in"],
    "ffn1_down_0": ["hyb_w_out", "ple_w_gate/0", "ple_w_proj/0"],
    "hyb_in": ["ffn2_w_gate/0"],
    "attn_fwd_1": ["ffn2_w_up/0"],
    "attn_fwd_4": ["ffn2_w_down/0"],
    "ffn2_up_0": ["ffn1_w_gate/1", "ffn1_w_up/1"],
    "ffn2_down_0": ["lru_wa", "lru_wx", "rec_w_out"],
    "ffn1_up_1": ["ffn1_w_down/1", "rec_w_in"],
    "rec_in": ["ffn2_w_down/1", "ple_w_gate/1", "ple_w_proj/1"],
    "rec_fwd": ["ffn2_w_gate/1", "ffn2_w_up/1"],
}
SCATTER_PLAN = {
    "ple_gw_proj_1": [("ple_w_gate", 1)],
    "ffn2_bwd_act_1": [("ple_w_proj", 1)],
    "ffn2_bwd_in_1": [("ffn2_w_gate", 1)],
    "rec_bwd": [("ffn2_w_up", 1), ("ffn2_w_down", 1)],
    "mix_bwd_in_1": [("rec_w_in", 0), ("rec_w_out", 0), ("lru_wa", 0), ("lru_wx", 0)],
    "ffn1_bwd_in_1": [("ffn1_w_gate", 1)],
    "ple_bwd_0": [("ffn1_w_up", 1)],
    "ple_gw_proj_0": [("ple_w_gate", 0)],
    "ffn2_bwd_act_0": [("ple_w_proj", 0)],
    "ffn2_gw_0_gate": [("ffn1_w_down", 1)],
    "ffn2_bwd_in_0": [("ffn2_w_gate", 0)],
    "attn_bwd_dq_1": [("ffn2_w_down", 0)],
    "attn_bwd_dkv_1": [("ffn2_w_up", 0)],
    "mix_bwd_in_0": [("hyb_w_in", 0), ("hyb_w_out", 0)],
    "ffn1_gw_0_up": [("ffn1_w_gate", 0)],
    "ffn1_gw_0_down": [("ffn1_w_up", 0)],
    "ffn1_bwd_in_0": [("ffn1_w_down", 0)],
}
FORWARD_PLAN = {
    "ffn1_bwd_in_1": ["rec_w_in", "rec_w_out", "lru_wa", "lru_wx"],
    "ffn2_bwd_in_0": ["ple_w_gate", "ple_w_proj"],
    "mix_bwd_in_0": ["ffn2_w_gate", "ffn2_w_up", "ffn2_w_down"],
    "ffn1_gw_0_up": ["hyb_w_in", "hyb_w_out"],
    "ffn1_bwd_in_0": ["ffn1_w_gate", "ffn1_w_up"],
}


class Plan:
    def __init__(self, shards, w):
        self.shards, self.w, self.grads, self.landed = shards, w, None, {}

    def host(self, kname):
        if kname in GATHER_PLAN:
            h = Host("gather", [self.shards[n] for n in GATHER_PLAN[kname]])
            h.names = GATHER_PLAN[kname]
            return h
        if kname in SCATTER_PLAN or kname in FORWARD_PLAN:
            items = SCATTER_PLAN.get(kname, [])
            fwd = FORWARD_PLAN.get(kname, [])
            h = Host("scatter", [(self.grads[n], lay, self.landed.get(n)) for n, lay in items], [self.landed[n] for n in fwd])
            h.names = [n for n, _ in items] + fwd
            return h
        return None

    def done(self, h):
        for n, o in zip(h.names, h.outs):
            if h.kind == "gather":
                self.w[n] = o
            else:
                self.landed[n] = o


def _halves(a):
    if a.shape[0] == 2:
        return a
    return a.reshape((2, a.shape[1] // 2) + a.shape[2:])


def _pack_rows(arrs, width):
    rows, offs, r0 = [], [], 0
    for a in arrs:
        if a.shape[1] > width:
            a = a.reshape(-1, width)
        rows.append(jnp.pad(a, ((0, 0), (0, width - a.shape[1]))))
        offs.append(r0)
        r0 += a.shape[0]
    pad = (-r0) % 8
    if pad:
        rows.append(jnp.zeros((pad, width), F32))
    return jnp.concatenate(rows, axis=0), offs


def kernel(x, p, rel_bias, ffn1_norm, ffn1_w_gate, ffn1_w_up, ffn1_w_down, mix_norm, hyb_w_in, hyb_conv_w, hyb_q_gain, hyb_k_gain, hyb_w_out, rec_w_in, rec_conv_w, rec_conv_b, lru_wa, lru_ba, lru_wx, lru_bx, lru_lambda, rec_w_out, ffn2_norm, ffn2_w_gate, ffn2_w_up, ffn2_w_down, ple_norm, ple_w_gate, ple_w_proj, loss_target, m_rel_bias, m_ffn1_norm, m_ffn1_w_gate, m_ffn1_w_up, m_ffn1_w_down, m_mix_norm, m_hyb_w_in, m_hyb_conv_w, m_hyb_q_gain, m_hyb_k_gain, m_hyb_w_out, m_rec_w_in, m_rec_conv_w, m_rec_conv_b, m_lru_wa, m_lru_ba, m_lru_wx, m_lru_bx, m_lru_lambda, m_rec_w_out, m_ffn2_norm, m_ffn2_w_gate, m_ffn2_w_up, m_ffn2_w_down, m_ple_norm, m_ple_w_gate, m_ple_w_proj, v_rel_bias, v_ffn1_norm, v_ffn1_w_gate, v_ffn1_w_up, v_ffn1_w_down, v_mix_norm, v_hyb_w_in, v_hyb_conv_w, v_hyb_q_gain, v_hyb_k_gain, v_hyb_w_out, v_rec_w_in, v_rec_conv_w, v_rec_conv_b, v_lru_wa, v_lru_ba, v_lru_wx, v_lru_bx, v_lru_lambda, v_rec_w_out, v_ffn2_norm, v_ffn2_w_gate, v_ffn2_w_up, v_ffn2_w_down, v_ple_norm, v_ple_w_gate, v_ple_w_proj):
    given = dict(locals())
    wts = {n: given[n] for n in WEIGHTS}
    k_chip = 2 * lax.axis_index("x") + lax.axis_index("y")

    shards = {}
    for n in BIG:
        b16 = wts[n].astype(BF16)
        if n in PER_LAYER:
            shards[n + "/0"], shards[n + "/1"] = b16[0:1], b16[1:2]
        else:
            shards[n] = b16
    first = gather_weights([_halves(shards[n]) for n in FIRST], "gather_first")
    w = {n: g.reshape((N_SHARD,) + shards[n].shape) for n, g in zip(FIRST, first)}
    plan = Plan(shards, w)
    sm2d = {n: wts[n].reshape(-1, wts[n].shape[-1]) for n in SMALL_SHARDED}
    slab, offs = _pack_rows([sm2d[n] for n in SMALL_SHARDED], 256)
    slabs = allgather8(slab, "gather_small")[0::2]
    for n, o in zip(SMALL_SHARDED, offs):
        r, cw = sm2d[n].shape
        w[n] = jnp.concatenate([slabs[kc, o:o + r, :cw] for kc in range(N_SHARD)], axis=1)
    for n in SMALL:
        if n not in SMALL_SHARDED:
            w[n] = wts[n]

    loss, dx, grads = local_step(x[0], p.reshape(p.shape[0], p.shape[2], p.shape[3]), loss_target[0], w, plan)
    loss = lax.psum(loss[0, 0], ("x", "y", "c"))

    for n, r8 in zip(LAST, exchange_cores([plan.landed[n] for n in LAST], "exchange_cores")):
        plan.landed[n] = r8
    out = {}
    for n in BIG:
        r8 = plan.landed[n]
        shp = wts[n].shape
        shp3 = shp if len(shp) == 3 else (shp[0], -1, shp[-1])
        three = lambda a: a.reshape(shp3)
        res = adamw(three(wts[n]), three(given["m_" + n]), three(given["v_" + n]), r8.reshape((N_DEV,) + three(wts[n]).shape), "adamw_" + n)
        out[n] = [a.reshape(shp) for a in res]
    g2d = [grads[n].reshape(-1, grads[n].shape[-1]) if n != "rel_bias" else grads[n].reshape(1, -1) for n in SMALL]
    gslab, goffs = _pack_rows(g2d, PACK_W)
    gsum = sum8(allgather8(gslab, "gather_small_grads"), "sum_small_grads")
    for n, o, g in zip(SMALL, goffs, g2d):
        shp = wts[n].shape
        r, cw = g.shape
        gs = gsum[o:o + r, :cw]
        if n in SMALL_SHARDED:
            sw = shp[-1]
            gs = lax.dynamic_slice_in_dim(gs, k_chip * sw, sw, axis=1)
        three = lambda a: a.reshape((1, -1, shp[-1]))
        res = adamw(three(wts[n]), three(given["m_" + n]), three(given["v_" + n]), three(gs), "adamw_" + n)
        out[n] = [a.reshape(shp) for a in res]
    return (loss, dx[None], *[out[n][0] for n in WEIGHTS], *[out[n][1] for n in WEIGHTS],
            *[out[n][2] for n in WEIGHTS], *[out[n][3] for n in WEIGHTS])
```

```python
import functools
import math

import numpy as np
import jax
import jax.numpy as jnp
from jax import lax
from jax.experimental import pallas as pl
from jax.experimental.pallas import tpu as pltpu

F32, BF16 = jnp.float32, jnp.bfloat16
S = jax.ShapeDtypeStruct
MESH = pl.DeviceIdType.MESH

D_MODEL = 1024
N_SHARD = 4
N_DEV = 8
HEAD_DIM = 64
N_HEADS = 8
ATTN_W = N_HEADS * HEAD_DIM
CONV_W = 512
BAND = 128
DILATIONS = (1, 4, 16)
REL_BUCKETS = 32
REL_MAX_DIST = 2048
LRU_BLOCKS = 4
LRU_BLOCK = 256
LRU_C = 8.0
EPS = 1e-6
NEG = -1e30
VMEM_LIMIT = 56 * 1024 * 1024
FFN_ROWS = 1024
FFN_SHARDS_ROWS = 512
TN_ROWS = 2048
BWD_IN_ROWS = 512

ADAM_LR, ADAM_B1, ADAM_B2, ADAM_EPS, ADAM_WD, ADAM_STEP = 0.001, 0.9, 0.999, 1e-08, 0.01, 10


def _cp(*sem):
    return pltpu.CompilerParams(dimension_semantics=sem, vmem_limit_bytes=VMEM_LIMIT)


def _bs(shape, imap):
    return pl.BlockSpec(shape, imap)


def _row_tile(t, want):
    for cand in range(min(want, t) // 8 * 8, 0, -8):
        if t % cand == 0:
            return cand
    return t


_ANY = pl.BlockSpec(memory_space=pl.ANY)


def _place():
    x, y, c = lax.axis_index("x"), lax.axis_index("y"), lax.axis_index("c")
    chips = [(1 - x, y), (x, 1 - y), (1 - x, 1 - y)]
    return x, y, c, 2 * x + y, chips, [2 * cx + cy for cx, cy in chips]


def _remote(src, dst, ssem, rsem, to):
    return pltpu.make_async_remote_copy(src_ref=src, dst_ref=dst, send_sem=ssem, recv_sem=rsem, device_id=to, device_id_type=MESH)


class Host:
    def __init__(self, kind, items, forwards=()):
        self.kind, self.items, self.forwards, self.outs = kind, items, list(forwards), None

    def n_sems(self):
        return 3 * len(self.items) + N_SHARD * len(self.forwards), len(self.items)

    def operands(self):
        if self.kind == "gather":
            return list(self.items), [S((N_SHARD,) + s.shape, s.dtype) for s in self.items], {}
        xin, shapes, alias = [], [], {}
        for a, (g, _, r_prev) in enumerate(self.items):
            xin.append(g)
            if r_prev is not None:
                alias[len(xin)] = a
                xin.append(r_prev)
            shapes.append(S((N_DEV,) + g.shape[1:], g.dtype))
        for f, r in enumerate(self.forwards):
            alias[len(xin)] = len(self.items) + f
            xin.append(r)
            shapes.append(S(r.shape, r.dtype))
        return xin, shapes, alias

    def copies(self, xi, xo, send, recv, lsem):
        x, y, c, k, chips, kk = _place()
        starts, waits = [], []
        pos = 0
        for f in range(len(self.forwards)):
            arr = xo[len(self.items) + f]
            for kq in range(N_SHARD):
                sem = 3 * len(self.items) + N_SHARD * f + kq
                cp = _remote(arr.at[2 * kq + c], arr.at[2 * kq + c], send.at[sem], recv.at[sem], (x, y, 1 - c))
                starts.append((cp, "start"))
                waits.append((cp, "wait_send"))
                other = arr.at[2 * kq + 1 - c]
                waits.append((_remote(other, other, send.at[sem], recv.at[sem], (x, y, 1 - c)), "wait_recv"))
        for a, item in enumerate(self.items):
            if self.kind == "gather":
                src_of = lambda chip_idx, s=xi[a]: s
                dst_of = lambda chip_idx, o=xo[a]: o.at[chip_idx]
                mine, theirs = k, kk
            else:
                g_ref = xi[pos]
                pos += 1 if item[2] is None else 2
                lay = item[1]
                src_of = lambda chip_idx, g=g_ref, lay=lay: g.at[chip_idx, lay]
                dst_of = lambda slot, o=xo[a], lay=lay: o.at[slot, lay]
                mine, theirs = 2 * k + c, [2 * kj + c for kj in kk]
            own_src = src_of(k)
            local = pltpu.make_async_copy(own_src, dst_of(mine), lsem.at[a])
            starts.append((local, "start"))
            waits.append((local, "wait"))
            for j, chip in enumerate(chips):
                src = own_src if self.kind == "gather" else src_of(kk[j])
                cp = _remote(src, dst_of(mine), send.at[3 * a + j], recv.at[3 * a + j], (*chip, c))
                starts.append((cp, "start"))
                waits.append((cp, "wait_send"))
                waits.append((_remote(own_src, dst_of(theirs[j]), send.at[3 * a + j], recv.at[3 * a + j], (*chip, c)), "wait_recv"))
        return starts, waits


def _call(host, body, *, name, grid, in_specs, out_specs, out_shape, scratch_shapes=(), compiler_params=None, args, aliases=None):
    aliases = dict(aliases or {})
    if host is None:
        return pl.pallas_call(body, name=name, grid=grid, in_specs=in_specs, out_specs=out_specs, out_shape=out_shape,
                              scratch_shapes=list(scratch_shapes), input_output_aliases=aliases, compiler_params=compiler_params)(*args)
    single = not isinstance(out_shape, (list, tuple))
    out_specs_l = [out_specs] if single else list(out_specs)
    out_shape_l = [out_shape] if single else list(out_shape)
    n_in, n_out, n_scr = len(in_specs), len(out_shape_l), len(scratch_shapes)
    xin, xshapes, xalias = host.operands()
    n_items = len(xshapes)
    n_rsem, n_lsem = host.n_sems()
    for i_in, i_out in xalias.items():
        aliases[n_in + i_in] = n_out + i_out
    nd = len(grid)

    def hosted(*refs):
        ins, xi = refs[:n_in], refs[n_in:n_in + len(xin)]
        o0 = n_in + len(xin)
        outs, xo = refs[o0:o0 + n_out], refs[o0 + n_out:o0 + n_out + n_items]
        s0 = o0 + n_out + n_items
        scr = refs[s0:s0 + n_scr]
        send, recv, lsem = refs[s0 + n_scr:]
        first = functools.reduce(jnp.logical_and, [pl.program_id(d) == 0 for d in range(nd)])
        last = functools.reduce(jnp.logical_and, [pl.program_id(d) == grid[d] - 1 for d in range(nd)])
        starts, waits = host.copies(xi, xo, send, recv, lsem)

        @pl.when(first)
        def _():
            for cp, how in starts:
                getattr(cp, how)()
        body(*ins, *outs, *scr)

        @pl.when(last)
        def _():
            for cp, how in waits:
                getattr(cp, how)()

    res = pl.pallas_call(
        hosted, name=name, grid=grid,
        in_specs=list(in_specs) + [_ANY] * len(xin),
        out_specs=out_specs_l + [_ANY] * n_items,
        out_shape=out_shape_l + xshapes,
        scratch_shapes=list(scratch_shapes) + [pltpu.SemaphoreType.DMA((n_rsem,)), pltpu.SemaphoreType.DMA((n_rsem,)),
                                               pltpu.SemaphoreType.DMA((max(n_lsem, 1),))],
        input_output_aliases=aliases,
        compiler_params=pltpu.CompilerParams(dimension_semantics=("arbitrary",) * nd, vmem_limit_bytes=VMEM_LIMIT),
    )(*args, *xin)
    host.outs = list(res[n_out:])
    return res[0] if single else list(res[:n_out])


def _rstd(x):
    return lax.rsqrt(jnp.mean(x * x, axis=-1, keepdims=True) + EPS)


def _sigmoid(x):
    return 1.0 / (1.0 + jnp.exp(-x))


def _dot(a, b):
    return jnp.dot(a, b, preferred_element_type=F32)


def _dot_nt(a, b):
    return lax.dot_general(a, b, (((1,), (1,)), ((), ())), preferred_element_type=F32)


def _dot_tn(a, b):
    return lax.dot_general(a, b, (((0,), (0,)), ((), ())), preferred_element_type=F32)


def _seg_dot(x, seg_bf16):
    hi = x.astype(BF16)
    lo = (x - hi.astype(F32)).astype(BF16)
    return _dot(hi, seg_bf16) + _dot(lo, seg_bf16)


def _shift_down(x, prev8, s):
    if s == 0:
        return x
    tm = x.shape[0]
    row = lax.broadcasted_iota(jnp.int32, x.shape, 0)
    main = jnp.where(row >= s, pltpu.roll(x, s, axis=0), 0.0)
    row8 = lax.broadcasted_iota(jnp.int32, prev8.shape, 0)
    head = jnp.where(row8 < s, pltpu.roll(prev8, s, axis=0), 0.0)
    if tm == 8:
        return main + head
    return main + jnp.concatenate([head, jnp.zeros((tm - 8, x.shape[1]), x.dtype)], axis=0)


def _shift_up(x, next8, s):
    if s == 0:
        return x
    tm = x.shape[0]
    row = lax.broadcasted_iota(jnp.int32, x.shape, 0)
    main = jnp.where(row < tm - s, pltpu.roll(x, tm - s, axis=0), 0.0)
    row8 = lax.broadcasted_iota(jnp.int32, next8.shape, 0)
    tail = jnp.where(row8 >= 8 - s, pltpu.roll(next8, 8 - s, axis=0), 0.0)
    if tm == 8:
        return main + tail
    return main + jnp.concatenate([jnp.zeros((tm - 8, x.shape[1]), x.dtype), tail], axis=0)


def _roll_fill(x, s, fill, up):
    tm = x.shape[0]
    row = lax.broadcasted_iota(jnp.int32, x.shape, 0)
    if up:
        return jnp.where(row < tm - s, pltpu.roll(x, tm - s, axis=0), fill)
    return jnp.where(row >= s, pltpu.roll(x, s, axis=0), fill)


def _log1p(y):
    u = 1.0 + y
    return jnp.where(u == 1.0, y, jnp.log(u) * (y / jnp.where(u == 1.0, 1.0, u - 1.0)))


def _softplus(x):
    return jnp.maximum(x, 0.0) + _log1p(jnp.exp(-jnp.abs(x)))


def _neg_expm1(y):
    series = -y * (1.0 + y * (0.5 + y * (1.0 / 6.0 + y * (1.0 / 24.0 + y * (1.0 / 120.0)))))
    return jnp.where(jnp.abs(y) < 0.03, series, 1.0 - jnp.exp(y))


_GELU_C = math.sqrt(2.0 / math.pi)


def _gelu_and_grad(x):
    inner = _GELU_C * (x + 0.044715 * x * x * x)
    t = jnp.tanh(inner)
    g = 0.5 * x * (1.0 + t)
    dg = 0.5 * (1.0 + t) + 0.5 * x * (1.0 - t * t) * _GELU_C * (1.0 + 3.0 * 0.044715 * x * x)
    return g, dg


def _rmsnorm_bwd(x, gain, dy):
    r = _rstd(x)
    xhat = x * r
    dxhat = dy * gain
    dx = r * (dxhat - xhat * jnp.mean(dxhat * xhat, axis=-1, keepdims=True))
    return dx, jnp.sum(dy * xhat, axis=0, keepdims=True)


def ffn_up(h, gain, wg, wu, layer, name, host=None):
    t, d = h.shape
    nk, _, _, f = wg.shape
    tm = _row_tile(t, FFN_SHARDS_ROWS)

    def body(h_ref, g_ref, wg_ref, wu_ref, hn_ref, gg_ref, uu_ref, aa_ref):
        x = h_ref[...]
        hn = (x * _rstd(x) * g_ref[...]).astype(BF16)
        hn_ref[...] = hn
        for k in range(nk):
            g = _dot(hn, wg_ref[k])
            u = _dot(hn, wu_ref[k])
            s = _sigmoid(g)
            silu = g * s
            gg_ref[k] = (u * (s * (1.0 + g * (1.0 - s)))).astype(BF16)
            uu_ref[k] = silu.astype(BF16)
            aa_ref[k] = (silu * u).astype(BF16)

    wspec = _bs((nk, None, d, f), lambda i: (0, layer, 0, 0))
    aspec = _bs((nk, tm, f), lambda i: (0, i, 0))
    return _call(
        host, body, name=name, grid=(t // tm,),
        in_specs=[_bs((tm, d), lambda i: (i, 0)), _bs((1, d), lambda i: (0, 0)), wspec, wspec],
        out_specs=[_bs((tm, d), lambda i: (i, 0)), aspec, aspec, aspec],
        out_shape=[S((t, d), BF16), S((nk, t, f), BF16), S((nk, t, f), BF16), S((nk, t, f), BF16)],
        compiler_params=_cp("parallel"),
        args=(h, gain, wg, wu))


def mm_acc(a, a_spec, b, b_spec, res, scale, nk, t, n, tm, name, host=None):
    def body(a_ref, b_ref, r_ref, o_ref, acc):
        k = pl.program_id(1)

        @pl.when(k == 0)
        def _():
            acc[...] = jnp.zeros_like(acc)
        acc[...] += _dot(a_ref[...].astype(BF16), b_ref[...])

        @pl.when(k == nk - 1)
        def _():
            o_ref[...] = r_ref[...] + scale * acc[...]

    return _call(
        host, body, name=name, grid=(t // tm, nk),
        in_specs=[a_spec, b_spec, _bs((tm, n), lambda i, k: (i, 0))],
        out_specs=_bs((tm, n), lambda i, k: (i, 0)),
        out_shape=S((t, n), F32),
        scratch_shapes=[pltpu.VMEM((tm, n), F32)],
        compiler_params=_cp("parallel", "arbitrary"),
        args=(a, b, res))


def ffn_down(a, wd, layer, h, name, host=None, scale=0.5):
    nk, t, f = a.shape
    d = h.shape[1]
    tm = _row_tile(t, FFN_ROWS)

    def body(a_ref, w_ref, r_ref, o_ref):
        acc = _dot(a_ref[0], w_ref[0])
        for k in range(1, nk):
            acc = acc + _dot(a_ref[k], w_ref[k])
        o_ref[...] = r_ref[...] + scale * acc

    row = _bs((tm, d), lambda i: (i, 0))
    return _call(
        host, body, name=name, grid=(t // tm,),
        in_specs=[_bs((nk, tm, f), lambda i: (0, i, 0)), _bs((nk, None, f, d), lambda i: (0, layer, 0, 0)), row],
        out_specs=row, out_shape=S((t, d), F32),
        compiler_params=_cp("parallel"),
        args=(a, wd, h))


def ffn_bwd_act(dh, wd, layer, gg, uu, name, host=None):
    nk, t, f = gg.shape
    d = dh.shape[1]
    tm = _row_tile(t, FFN_SHARDS_ROWS)

    def body(dh_ref, wd_ref, g_ref, u_ref, dg_ref, du_ref, dh16_ref):
        dh16 = dh_ref[...].astype(BF16)
        dh16_ref[...] = dh16
        for k in range(nk):
            da = 0.5 * _dot_nt(dh16, wd_ref[k])
            dg_ref[k] = (da * g_ref[k].astype(F32)).astype(BF16)
            du_ref[k] = (da * u_ref[k].astype(F32)).astype(BF16)

    aspec = _bs((nk, tm, f), lambda i: (0, i, 0))
    row = _bs((tm, d), lambda i: (i, 0))
    return _call(
        host, body, name=name, grid=(t // tm,),
        in_specs=[row, _bs((nk, None, f, d), lambda i: (0, layer, 0, 0)), aspec, aspec],
        out_specs=[aspec, aspec, row],
        out_shape=[S((nk, t, f), BF16), S((nk, t, f), BF16), S((t, d), BF16)],
        compiler_params=_cp("parallel"),
        args=(dh, wd, gg, uu))


def nt_acc_normbwd(terms, nk, h, gain, dh, name, host=None):
    t, d = h.shape
    tm = _row_tile(t, BWD_IN_ROWS)
    sub = _row_tile(tm, 256)
    nterm = len(terms)
    picks = [term[4] for term in terms]

    def body(*refs):
        xs = refs[:2 * nterm]
        h_ref, g_ref, dh_ref, o_ref, dg_ref, acc = refs[2 * nterm:]

        @pl.when(pl.program_id(0) == 0)
        def _():
            dg_ref[...] = jnp.zeros_like(dg_ref)
        tot = None
        for j in range(nterm):
            for k in range(nk):
                part = _dot_nt(picks[j](xs[2 * j], k), xs[2 * j + 1][k])
                tot = part if tot is None else tot + part
        acc[...] = tot

        def rows_of(cidx, dgain):
            rows = pl.ds(pl.multiple_of(cidx * sub, sub), sub)
            dx, dgc = _rmsnorm_bwd(h_ref[rows, :], g_ref[...], acc[rows, :])
            o_ref[rows, :] = dh_ref[rows, :] + dx
            return dgain + dgc
        dg_ref[...] += lax.fori_loop(0, tm // sub, rows_of, jnp.zeros((1, d), F32))

    in_specs, args = [], []
    for x, xs_, w, ws_, _ in terms:
        in_specs += [xs_, ws_]
        args += [x, w]
    row = _bs((tm, d), lambda i: (i, 0))
    vec = _bs((1, d), lambda i: (0, 0))
    return _call(
        host, body, name=name, grid=(t // tm,),
        in_specs=in_specs + [row, vec, row],
        out_specs=[row, vec],
        out_shape=[S((t, d), F32), S((1, d), F32)],
        scratch_shapes=[pltpu.VMEM((tm, d), F32)],
        compiler_params=_cp("arbitrary"),
        args=(*args, h, gain, dh))


def ffn_bwd_in(dg, du, wg, wu, layer, h, gain, dh, name, host=None):
    nk, t, f = dg.shape
    d = h.shape[1]
    tm = _row_tile(t, BWD_IN_ROWS)
    aspec = _bs((nk, tm, f), lambda i: (0, i, 0))
    wspec = _bs((nk, None, d, f), lambda i: (0, layer, 0, 0))
    pick = lambda x_ref, k: x_ref[k]
    return nt_acc_normbwd([(dg, aspec, wg, wspec, pick), (du, aspec, wu, wspec, pick)], nk, h, gain, dh, name, host)


def tn_mm(x, x_spec, y, y_spec, nblk, t, ka, nb, out_shape, out_spec, scale, prev, name, host=None):
    tk = _row_tile(t, TN_ROWS if F32 in (x.dtype, y.dtype) else 2 * TN_ROWS)

    def body(*refs):
        if prev is None:
            x_ref, y_ref, o_ref, acc = refs
        else:
            x_ref, y_ref, _, o_ref, acc = refs
        j = pl.program_id(1)

        @pl.when(j == 0)
        def _():
            acc[...] = jnp.zeros_like(acc)
        acc[...] += _dot_tn(x_ref[...].astype(BF16), y_ref[...].astype(BF16))

        @pl.when(j == t // tk - 1)
        def _():
            o_ref[...] = (scale * acc[...]).astype(o_ref.dtype)

    in_specs = [x_spec(tk), y_spec(tk)]
    args = [x, y]
    aliases = {}
    if prev is not None:
        in_specs.append(pl.BlockSpec(memory_space=pl.ANY))
        args.append(prev)
        aliases = {2: 0}
    return _call(
        host, body, name=name, grid=(nblk, t // tk),
        in_specs=in_specs, out_specs=out_spec, out_shape=out_shape,
        scratch_shapes=[pltpu.VMEM((ka, nb), F32)],
        aliases=aliases,
        compiler_params=_cp("parallel", "arbitrary"),
        args=tuple(args))


def ffn_wgrads(which, hn, dh, aa, dg, du, layer, grads, run):
    nk, t, f = aa.shape
    d = hn.shape[1]
    hn_spec = lambda tk: _bs((tk, d), lambda k, j: (j, 0))
    a_spec = lambda tk: _bs((None, tk, f), lambda k, j: (k, j, 0))
    shape_gu, spec_gu = S((nk, 2, d, f), BF16), _bs((None, None, d, f), lambda k, j: (k, layer, 0, 0))
    shape_d, spec_d = S((nk, 2, f, d), BF16), _bs((None, None, f, d), lambda k, j: (k, layer, 0, 0))
    for suffix, x, xs, y, ys, ka, nb, shp, spec, scale in (
            ("gate", hn, hn_spec, dg, a_spec, d, f, shape_gu, spec_gu, 1.0),
            ("up", hn, hn_spec, du, a_spec, d, f, shape_gu, spec_gu, 1.0),
            ("down", aa, a_spec, dh, hn_spec, f, d, shape_d, spec_d, 0.5)):
        key = f"{which}_w_{suffix}"
        grads[key] = run(tn_mm, x, xs, y, ys, nk, t, ka, nb, shp, spec, scale, grads.get(key), name=f"{which}_gw_{layer}_{suffix}")


def norm_mm(h, gain, w, name, host=None):
    t, d = h.shape
    nb, _, bw = w.shape
    tm = _row_tile(t, FFN_SHARDS_ROWS)

    def body(h_ref, g_ref, w_ref, hn_ref, z_ref):
        x = h_ref[...]
        hn = (x * _rstd(x) * g_ref[...]).astype(BF16)
        hn_ref[...] = hn
        for k in range(nb):
            z_ref[:, k * bw:(k + 1) * bw] = _dot(hn, w_ref[k])

    return _call(
        host, body, name=name, grid=(t // tm,),
        in_specs=[_bs((tm, d), lambda i: (i, 0)), _bs((1, d), lambda i: (0, 0)), _bs((nb, d, bw), lambda i: (0, 0, 0))],
        out_specs=[_bs((tm, d), lambda i: (i, 0)), _bs((tm, nb * bw), lambda i: (i, 0))],
        out_shape=[S((t, d), BF16), S((t, nb * bw), F32)],
        compiler_params=_cp("parallel"),
        args=(h, gain, w))


def nt_mm(a, w, name):
    t, k = a.shape
    n = w.shape[0]
    tm = _row_tile(t, 512)

    def body(a_ref, w_ref, o_ref):
        o_ref[...] = _dot_nt(a_ref[...].astype(BF16), w_ref[...])

    return pl.pallas_call(
        body, name=name, grid=(t // tm,),
        in_specs=[_bs((tm, k), lambda i: (i, 0)), _bs((n, k), lambda i: (0, 0))],
        out_specs=_bs((tm, n), lambda i: (i, 0)),
        out_shape=S((t, n), F32),
        compiler_params=_cp("parallel"),
    )(a, w)


def _head_mean_matrix():
    m = np.kron(np.eye(N_HEADS, dtype=np.float32), np.full((HEAD_DIM, HEAD_DIM), 1.0 / HEAD_DIM, np.float32))
    return jnp.asarray(m, BF16)


def _head_sum_matrix():
    m = np.kron(np.eye(N_HEADS, dtype=np.float32), np.ones((HEAD_DIM, HEAD_DIM), np.float32))
    return jnp.asarray(m, BF16)


def _rel_bucket_np(dist):
    max_exact = REL_BUCKETS // 2
    n = np.maximum(dist, 1).astype(np.float32)
    large = max_exact + (np.log(n / np.float32(max_exact)) / np.float32(math.log(REL_MAX_DIST / max_exact))
                         * np.float32(REL_BUCKETS - max_exact)).astype(np.int32)
    large = np.minimum(large, REL_BUCKETS - 1)
    return np.where(dist < max_exact, dist, large)


def _band_tables():
    qi = np.arange(BAND)[:, None]
    kj = np.arange(2 * BAND)[None, :]
    dist_q = qi + BAND - kj
    qq = np.arange(2 * BAND)[:, None]
    kk = np.arange(BAND)[None, :]
    dist_k = qq - kk
    out = []
    for dist in (dist_q, dist_k):
        valid = (dist >= 0) & (dist <= BAND)
        bucket = np.stack([_rel_bucket_np(np.clip(dist, 0, BAND) * d) for d in DILATIONS])
        out.append((bucket, valid))
    return out


def band_bias(rel_bias):
    out = []
    for bucket, valid in _band_tables():
        bucket = np.where(valid[None], bucket, -1)[:, None]
        bucket_dev = lax.optimization_barrier(jnp.asarray(bucket, jnp.int32))
        tab = jnp.full((len(DILATIONS), N_HEADS) + bucket.shape[2:], NEG, F32)
        for b in range(REL_BUCKETS):
            if (bucket == b).any():
                tab = jnp.where(bucket_dev == b, rel_bias[b][None, :, None, None], tab)
        out.append(tab.reshape(len(DILATIONS), N_HEADS // 2, 2 * tab.shape[2], tab.shape[3]))
    return out


LANE_TILE = 128
N_LANE_TILES = ATTN_W // LANE_TILE


def _view_shape(t, dil):
    return (t // dil, dil * ATTN_W)


def _view_spec(tm, dil):
    return _bs((tm // dil, dil * ATTN_W), lambda i: (i, 0))


def _cols_to(scr, val):
    for cc in range(N_LANE_TILES):
        scr[cc] = val[:, LANE_TILE * cc:LANE_TILE * (cc + 1)]


def _cols_from(scr):
    return jnp.concatenate([scr[cc] for cc in range(N_LANE_TILES)], axis=1)


def _write_view(scr, out_ref, dil):
    if dil == 1:
        out_ref[...] = _cols_from(scr).astype(out_ref.dtype)
        return
    rows = scr.shape[1] // dil
    for r in range(dil):
        for cc in range(N_LANE_TILES):
            c0 = r * ATTN_W + LANE_TILE * cc
            out_ref[:, c0:c0 + LANE_TILE] = scr[cc, pl.ds(r, rows, stride=dil), :].astype(out_ref.dtype)


def _read_view(scr, in_ref, dil):
    if dil == 1:
        return in_ref[...].astype(F32)
    rows = scr.shape[1] // dil
    for r in range(dil):
        for cc in range(N_LANE_TILES):
            c0 = r * ATTN_W + LANE_TILE * cc
            scr[cc, pl.ds(r, rows, stride=dil), :] = in_ref[:, c0:c0 + LANE_TILE].astype(F32)
    return _cols_from(scr)


def hyb_prep(z, q_gain, k_gain, name):
    t = z.shape[0]
    tm = _row_tile(t, 512)
    seg = _head_mean_matrix()
    nd = len(DILATIONS)

    def body(q_ref, k_ref, v_ref, qg_ref, kg_ref, seg_ref, *rest):
        outs, scr = rest[:3 * nd], rest[3 * nd]
        q = q_ref[...]
        k = k_ref[...]
        vals = (q * lax.rsqrt(_seg_dot(q * q, seg_ref[...]) + EPS) * qg_ref[...],
                k * lax.rsqrt(_seg_dot(k * k, seg_ref[...]) + EPS) * kg_ref[...],
                v_ref[...])
        for j, val in enumerate(vals):
            _cols_to(scr, val)
            for g, dil in enumerate(DILATIONS):
                _write_view(scr, outs[3 * g + j], dil)

    col = lambda c: _bs((tm, ATTN_W), lambda i: (i, c))
    vec = _bs((1, ATTN_W), lambda i: (0, 0))
    res = pl.pallas_call(
        body, name=name, grid=(t // tm,),
        in_specs=[col(3), col(4), col(5), vec, vec, _bs((ATTN_W, ATTN_W), lambda i: (0, 0))],
        out_specs=[_view_spec(tm, dil) for dil in DILATIONS for _ in range(3)],
        out_shape=[S(_view_shape(t, dil), BF16) for dil in DILATIONS for _ in range(3)],
        scratch_shapes=[pltpu.VMEM((N_LANE_TILES, tm, LANE_TILE), F32)],
        compiler_params=_cp("parallel"),
    )(z, z, z, q_gain, k_gain, seg)
    return {dil: tuple(res[3 * g:3 * g + 3]) for g, dil in enumerate(DILATIONS)}


def _lane_lo(shape):
    return lax.broadcasted_iota(jnp.int32, shape, 1) < HEAD_DIM


def _stack_heads(pair):
    lo = _lane_lo(pair.shape)
    zero = jnp.zeros_like(pair)
    return jnp.concatenate([jnp.where(lo, pair, zero), jnp.where(lo, zero, pair)], axis=0)


def _unstack_heads(st):
    rows = st.shape[0] // 2
    return jnp.where(_lane_lo((rows, st.shape[1])), st[:rows], st[rows:])


def attn_fwd(q, k, v, bias, dil, name, host=None):
    qv, kv, vv = q, k, v
    sub = q.shape[0]
    nb = sub // BAND

    def body(q_ref, kp_ref, kc_ref, vp_ref, vc_ref, b_ref, o_ref, l_ref):
        first = pl.program_id(1) == 0
        colk = lax.broadcasted_iota(jnp.int32, (2 * BAND, 2 * BAND), 1)
        for j in range(N_HEADS // 2):
            sl = slice(2 * HEAD_DIM * j, 2 * HEAD_DIM * (j + 1))
            kk = jnp.concatenate([kp_ref[:, sl], kc_ref[:, sl]], axis=0)
            vv_ = jnp.concatenate([vp_ref[:, sl], vc_ref[:, sl]], axis=0)
            s = _dot_nt(_stack_heads(q_ref[:, sl]), kk) * (HEAD_DIM ** -0.5) + b_ref[j]
            s = jnp.where(jnp.logical_and(first, colk < BAND), NEG, s)
            m = jnp.max(s, axis=-1, keepdims=True)
            p = jnp.exp(s - m)
            l = jnp.sum(p, axis=-1, keepdims=True)
            o_ref[:, sl] = _unstack_heads(_dot(p.astype(BF16), vv_) / l).astype(o_ref.dtype)
            l_ref[:, sl] = _unstack_heads(jnp.broadcast_to(m + jnp.log(l), (2 * BAND, 2 * HEAD_DIM)))

    cur = _bs((BAND, ATTN_W), lambda r, n: (n, r))
    prv = _bs((BAND, ATTN_W), lambda r, n: (jnp.maximum(n - 1, 0), r))
    return _call(
        host, body, name=name, grid=(dil, nb),
        in_specs=[cur, prv, cur, prv, cur, _bs((N_HEADS // 2, 2 * BAND, 2 * BAND), lambda r, n: (0, 0, 0))],
        out_specs=[cur, cur],
        out_shape=[S((sub, dil * ATTN_W), BF16), S((sub, dil * ATTN_W), F32)],
        compiler_params=_cp("parallel", "arbitrary"),
        args=(qv, kv, kv, vv, vv, bias))


def hyb_post(z, conv_w, os_, lses, name):
    t = z.shape[0]
    tm = _row_tile(t, 512)
    nd = len(DILATIONS)

    def body(gb_ref, gc_ref, cx_ref, gch_ref, cxh_ref, w_ref, *rest):
        o_refs, l_refs = rest[:nd], rest[nd:2 * nd]
        y_ref, ya_ref = rest[2 * nd:2 * nd + 2]
        lt_refs, scr = rest[2 * nd + 2:3 * nd + 2], rest[3 * nd + 2]
        i = pl.program_id(0)
        m = gc_ref[...] * cx_ref[...]
        mh = jnp.where(i == 0, 0.0, gch_ref[...] * cxh_ref[...])
        conv = w_ref[0:1, :] * _shift_down(m, mh, 2) + w_ref[1:2, :] * _shift_down(m, mh, 1) + w_ref[2:3, :] * m
        y_ref[0] = (gb_ref[...] * conv).astype(BF16)
        ls = [_read_view(scr, l_refs[g], dil) for g, dil in enumerate(DILATIONS)]
        mx = functools.reduce(jnp.maximum, ls)
        es = [jnp.exp(l - mx) for l in ls]
        den = functools.reduce(lambda a, b: a + b, es)
        num = es[0] * _read_view(scr, o_refs[0], DILATIONS[0])
        for g in range(1, nd):
            num = num + es[g] * _read_view(scr, o_refs[g], DILATIONS[g])
        ya = num / den
        y_ref[1] = ya.astype(BF16)
        ya_ref[...] = ya
        _cols_to(scr, mx + jnp.log(den))
        for g, dil in enumerate(DILATIONS):
            _write_view(scr, lt_refs[g], dil)

    hb = tm // 8
    col = lambda c: _bs((tm, CONV_W), lambda i: (i, c))
    halo = lambda c: _bs((8, CONV_W), lambda i: (jnp.maximum(i * hb - 1, 0), c))
    row = _bs((tm, ATTN_W), lambda i: (i, 0))
    views = [_view_spec(tm, dil) for dil in DILATIONS]
    res = pl.pallas_call(
        body, name=name, grid=(t // tm,),
        in_specs=[col(0), col(1), col(2), halo(1), halo(2), _bs((3, CONV_W), lambda i: (0, 0))] + views * 2,
        out_specs=[_bs((2, tm, ATTN_W), lambda i: (0, i, 0)), row] + views,
        out_shape=[S((2, t, ATTN_W), BF16), S((t, ATTN_W), F32)] + [S(_view_shape(t, dil), F32) for dil in DILATIONS],
        scratch_shapes=[pltpu.VMEM((N_LANE_TILES, tm, LANE_TILE), F32)],
        compiler_params=_cp("parallel"),
    )(z, z, z, z, z, conv_w, *os_, *lses)
    return res[0], res[1], dict(zip(DILATIONS, res[2:]))


def attn_delta(dy, ya, name):
    t = ya.shape[0]
    tm = _row_tile(t, 512)
    seg = _head_sum_matrix()
    nd = len(DILATIONS)

    def body(dy_ref, ya_ref, seg_ref, *rest):
        dl_refs, db_refs, scr = rest[:nd], rest[nd:2 * nd], rest[2 * nd]
        dya = dy_ref[...]
        _cols_to(scr, _seg_dot(dya * ya_ref[...], seg_ref[...]))
        for g, dil in enumerate(DILATIONS):
            _write_view(scr, dl_refs[g], dil)
        _cols_to(scr, dya)
        for g, dil in enumerate(DILATIONS):
            _write_view(scr, db_refs[g], dil)

    row = _bs((tm, ATTN_W), lambda i: (i, 0))
    views = [_view_spec(tm, dil) for dil in DILATIONS]
    res = pl.pallas_call(
        body, name=name, grid=(t // tm,),
        in_specs=[_bs((tm, ATTN_W), lambda i: (i, 1)), row, _bs((ATTN_W, ATTN_W), lambda i: (0, 0))],
        out_specs=views * 2,
        out_shape=[S(_view_shape(t, dil), F32) for dil in DILATIONS] + [S(_view_shape(t, dil), BF16) for dil in DILATIONS],
        scratch_shapes=[pltpu.VMEM((N_LANE_TILES, tm, LANE_TILE), F32)],
        compiler_params=_cp("parallel"),
    )(dy, ya, seg)
    return dict(zip(DILATIONS, res[:nd])), dict(zip(DILATIONS, res[nd:]))


def attn_bwd_dq(q, k, v, dya, lt, delta, bias, dil, name, host=None):
    qv, kv, vv, dv_, lv, ev = q, k, v, dya, lt, delta
    sub = q.shape[0]
    nb = sub // BAND

    def body(q_ref, kp_ref, kc_ref, vp_ref, vc_ref, do_ref, l_ref, e_ref, b_ref, dq_ref, db_ref):
        r, n = pl.program_id(0), pl.program_id(1)

        @pl.when(jnp.logical_and(r == 0, n == 0))
        def _():
            db_ref[...] = jnp.zeros_like(db_ref)
        first = n == 0
        colk = lax.broadcasted_iota(jnp.int32, (2 * BAND, 2 * BAND), 1)
        for j in range(N_HEADS // 2):
            c0 = 2 * HEAD_DIM * j
            sl = slice(c0, c0 + 2 * HEAD_DIM)
            kk = jnp.concatenate([kp_ref[:, sl], kc_ref[:, sl]], axis=0)
            vv_ = jnp.concatenate([vp_ref[:, sl], vc_ref[:, sl]], axis=0)
            lse = jnp.concatenate([l_ref[:, c0:c0 + 1], l_ref[:, c0 + HEAD_DIM:c0 + HEAD_DIM + 1]], axis=0)
            dlt = jnp.concatenate([e_ref[:, c0:c0 + 1], e_ref[:, c0 + HEAD_DIM:c0 + HEAD_DIM + 1]], axis=0)
            s = _dot_nt(_stack_heads(q_ref[:, sl]), kk) * (HEAD_DIM ** -0.5) + b_ref[j]
            s = jnp.where(jnp.logical_and(first, colk < BAND), NEG, s)
            p = jnp.exp(s - lse)
            ds = p * (_dot_nt(_stack_heads(do_ref[:, sl]), vv_) - dlt)
            db_ref[j] += ds
            dq_ref[:, sl] = (_unstack_heads(_dot(ds.astype(BF16), kk)) * (HEAD_DIM ** -0.5)).astype(dq_ref.dtype)

    cur = _bs((BAND, ATTN_W), lambda r, n: (n, r))
    prv = _bs((BAND, ATTN_W), lambda r, n: (jnp.maximum(n - 1, 0), r))
    tab = _bs((N_HEADS // 2, 2 * BAND, 2 * BAND), lambda r, n: (0, 0, 0))
    dq, db = _call(
        host, body, name=name, grid=(dil, nb),
        in_specs=[cur, prv, cur, prv, cur, cur, cur, cur, tab],
        out_specs=[cur, tab],
        out_shape=[S((sub, dil * ATTN_W), BF16), S((N_HEADS // 2, 2 * BAND, 2 * BAND), F32)],
        compiler_params=_cp("arbitrary", "arbitrary"),
        args=(qv, kv, kv, vv, vv, dv_, lv, ev, bias))
    return dq, db.reshape(N_HEADS, BAND, 2 * BAND)


def attn_bwd_dkv(q, k, v, dya, lt, delta, bias_k, dil, name, host=None):
    qv, kv, vv, dv_, lv, ev = q, k, v, dya, lt, delta
    sub = q.shape[0]
    nb = sub // BAND

    def body(k_ref, v_ref, qc_ref, qn_ref, dc_ref, dn_ref, lc_ref, ln_ref, ec_ref, en_ref, b_ref, dk_ref, dv_ref):
        last = pl.program_id(1) == nb - 1
        rowq = lax.broadcasted_iota(jnp.int32, (4 * BAND, BAND), 0)
        from_next = (rowq & BAND) != 0
        for j in range(N_HEADS // 2):
            c0 = 2 * HEAD_DIM * j
            sl = slice(c0, c0 + 2 * HEAD_DIM)
            kp, vp = k_ref[:, sl], v_ref[:, sl]
            q4 = _stack_heads(jnp.concatenate([qc_ref[:, sl], qn_ref[:, sl]], axis=0))
            do4 = _stack_heads(jnp.concatenate([dc_ref[:, sl], dn_ref[:, sl]], axis=0))
            lse = jnp.concatenate([ref[:, c:c + 1] for c in (c0, c0 + HEAD_DIM) for ref in (lc_ref, ln_ref)], axis=0)
            dlt = jnp.concatenate([ref[:, c:c + 1] for c in (c0, c0 + HEAD_DIM) for ref in (ec_ref, en_ref)], axis=0)
            s = _dot_nt(q4, kp) * (HEAD_DIM ** -0.5) + b_ref[j]
            s = jnp.where(jnp.logical_and(last, from_next), NEG, s)
            p = jnp.exp(s - lse)
            ds = p * (_dot_nt(do4, vp) - dlt)
            dv_ref[:, sl] = _dot_tn(p.astype(BF16), do4).astype(dv_ref.dtype)
            dk_ref[:, sl] = (_dot_tn(ds.astype(BF16), q4) * (HEAD_DIM ** -0.5)).astype(dk_ref.dtype)

    cur = _bs((BAND, ATTN_W), lambda r, n: (n, r))
    nxt = _bs((BAND, ATTN_W), lambda r, n: (jnp.minimum(n + 1, nb - 1), r))
    tab = _bs((N_HEADS // 2, 4 * BAND, BAND), lambda r, n: (0, 0, 0))
    return _call(
        host, body, name=name, grid=(dil, nb),
        in_specs=[cur, cur, cur, nxt, cur, nxt, cur, nxt, cur, nxt, tab],
        out_specs=[cur, cur],
        out_shape=[S((sub, dil * ATTN_W), BF16)] * 2,
        compiler_params=_cp("parallel", "arbitrary"),
        args=(kv, vv, qv, qv, dv_, dv_, lv, lv, ev, ev, bias_k))


def hyb_dz(z, dy, conv_w, q_gain, k_gain, dqs, dks, dvs, name):
    t = z.shape[0]
    tm = _row_tile(t, 512)
    nt = t // tm
    seg = _head_mean_matrix()

    def body(gb_ref, gc_ref, cx_ref, q_ref, k_ref, gch_ref, cxh_ref, gbn_ref, dyc_ref, dyn_ref, w_ref, qg_ref, kg_ref, seg_ref,
             dq1, dq2, dq3, dk1, dk2, dk3, dv1, dv2, dv3, dz_ref, dw_ref, dqg_ref, dkg_ref, scr):
        i = pl.program_id(0)

        def total(parts):
            acc = _read_view(scr, parts[0], DILATIONS[0])
            for g in range(1, len(DILATIONS)):
                acc = acc + _read_view(scr, parts[g], DILATIONS[g])
            return acc

        @pl.when(i == 0)
        def _():
            dw_ref[...] = jnp.zeros_like(dw_ref)
            dqg_ref[...] = jnp.zeros_like(dqg_ref)
            dkg_ref[...] = jnp.zeros_like(dkg_ref)
        gb, gc, cx, dyc = gb_ref[...], gc_ref[...], cx_ref[...], dyc_ref[...]
        m = gc * cx
        mh = jnp.where(i == 0, 0.0, gch_ref[...] * cxh_ref[...])
        m1, m2 = _shift_down(m, mh, 1), _shift_down(m, mh, 2)
        conv = w_ref[0:1, :] * m2 + w_ref[1:2, :] * m1 + w_ref[2:3, :] * m
        dconv = dyc * gb
        dcn = jnp.where(i == nt - 1, 0.0, dyn_ref[...] * gbn_ref[...])
        dm = w_ref[2:3, :] * dconv + w_ref[1:2, :] * _shift_up(dconv, dcn, 1) + w_ref[0:1, :] * _shift_up(dconv, dcn, 2)
        dz_ref[:, 0:CONV_W] = (dyc * conv).astype(BF16)
        dz_ref[:, CONV_W:2 * CONV_W] = (dm * cx).astype(BF16)
        dz_ref[:, 2 * CONV_W:3 * CONV_W] = (dm * gc).astype(BF16)
        dw_ref[0:1, :] += jnp.sum(dconv * m2, axis=0, keepdims=True)
        dw_ref[1:2, :] += jnp.sum(dconv * m1, axis=0, keepdims=True)
        dw_ref[2:3, :] += jnp.sum(dconv * m, axis=0, keepdims=True)
        base = 3 * CONV_W
        for idx, (x_ref, g_ref, parts, dgain_ref) in enumerate(((q_ref, qg_ref, (dq1, dq2, dq3), dqg_ref),
                                                                  (k_ref, kg_ref, (dk1, dk2, dk3), dkg_ref))):
            x = x_ref[...]
            dxh = total(parts)
            r = lax.rsqrt(_seg_dot(x * x, seg_ref[...]) + EPS)
            xhat = x * r
            tt = dxh * g_ref[...]
            dx = r * (tt - xhat * _seg_dot(tt * xhat, seg_ref[...]))
            dz_ref[:, base + idx * ATTN_W:base + (idx + 1) * ATTN_W] = dx.astype(BF16)
            dgain_ref[...] += jnp.sum(dxh * xhat, axis=0, keepdims=True)
        dz_ref[:, base + 2 * ATTN_W:base + 3 * ATTN_W] = total((dv1, dv2, dv3)).astype(BF16)

    hb = tm // 8
    col = lambda c: _bs((tm, CONV_W), lambda i: (i, c))
    prev = lambda c: _bs((8, CONV_W), lambda i: (jnp.maximum(i * hb - 1, 0), c))
    nxt = lambda c: _bs((8, CONV_W), lambda i: (jnp.minimum((i + 1) * hb, t // 8 - 1), c))
    row = _bs((tm, ATTN_W), lambda i: (i, 0))
    vec = _bs((1, ATTN_W), lambda i: (0, 0))
    return pl.pallas_call(
        body, name=name, grid=(nt,),
        in_specs=[col(0), col(1), col(2), col(3), col(4), prev(1), prev(2), nxt(0), col(0), nxt(0),
                  _bs((3, CONV_W), lambda i: (0, 0)), vec, vec, _bs((ATTN_W, ATTN_W), lambda i: (0, 0))]
                 + [_view_spec(tm, dil) for dil in DILATIONS] * 3,
        out_specs=[_bs((tm, 6 * CONV_W), lambda i: (i, 0)), _bs((3, CONV_W), lambda i: (0, 0)), vec, vec],
        out_shape=[S((t, 6 * CONV_W), BF16), S((3, CONV_W), F32), S((1, ATTN_W), F32), S((1, ATTN_W), F32)],
        scratch_shapes=[pltpu.VMEM((N_LANE_TILES, tm, LANE_TILE), F32)],
        compiler_params=_cp("arbitrary"),
    )(z, z, z, z, z, z, z, z, dy, dy, conv_w, q_gain, k_gain, seg, *dqs, *dks, *dvs)


def rel_bias_grad(dbs, name):
    (bq, vq), _ = _band_tables()
    onehot = np.zeros((len(DILATIONS), REL_BUCKETS, BAND * 2 * BAND), np.float32)
    for g in range(len(DILATIONS)):
        idx = bq[g].reshape(-1)
        ok = vq.reshape(-1)
        onehot[g, idx[ok], np.nonzero(ok)[0]] = 1.0
    onehot = jnp.asarray(onehot, BF16)
    flat = [d.reshape(N_HEADS, BAND * 2 * BAND) for d in dbs]

    def body(oh_ref, d1, d2, d3, o_ref):
        acc = jnp.zeros((REL_BUCKETS, N_HEADS), F32)
        for g, d in enumerate((d1, d2, d3)):
            x = d[...]
            hi = x.astype(BF16)
            lo = (x - hi.astype(F32)).astype(BF16)
            acc += _dot_nt(oh_ref[g], hi) + _dot_nt(oh_ref[g], lo)
        o_ref[...] = acc

    full = lambda shp: _bs(shp, lambda: tuple(0 for _ in shp))
    return pl.pallas_call(
        body, name=name,
        in_specs=[full(onehot.shape)] + [full(flat[0].shape)] * 3,
        out_specs=full((REL_BUCKETS, N_HEADS)),
        out_shape=S((REL_BUCKETS, N_HEADS), F32),
        compiler_params=pltpu.CompilerParams(vmem_limit_bytes=VMEM_LIMIT),
    )(onehot, *flat)


def _lru_gates(xb, wa_ref, wx_ref, ba, bx):
    xb16 = xb.astype(BF16)
    ga = jnp.concatenate([_dot(xb16[:, LRU_BLOCK * g:LRU_BLOCK * (g + 1)], wa_ref[g]) for g in range(LRU_BLOCKS)], axis=1) + ba
    gx = jnp.concatenate([_dot(xb16[:, LRU_BLOCK * g:LRU_BLOCK * (g + 1)], wx_ref[g]) for g in range(LRU_BLOCKS)], axis=1) + bx
    return ga, gx


def _lru_coeffs(ga, gx, lam):
    sga = _sigmoid(ga)
    sp = _softplus(-lam)
    log_a = -LRU_C * sga * sp
    a = jnp.exp(log_a)
    one_m_a2 = _neg_expm1(2.0 * log_a)
    return sga, sp, a, one_m_a2, jnp.sqrt(one_m_a2), _sigmoid(gx)


def rec_fwd(z, conv_w, conv_b, wa, wx, ba, bx, lam, name, host=None):
    t = z.shape[0]
    w = z.shape[1] // 2
    tm = _row_tile(t, 256)

    def body(xp_ref, xh_ref, yb_ref, cw_ref, cb_ref, wa_ref, wx_ref, ba_ref, bx_ref, lam_ref,
             xb_ref, ga_ref, gx_ref, hs_ref, out_ref, carry):
        i = pl.program_id(0)

        @pl.when(i == 0)
        def _():
            carry[...] = jnp.zeros_like(carry)
        xp = xp_ref[...]
        xh = jnp.where(i == 0, 0.0, xh_ref[...])
        xb = cb_ref[...] + cw_ref[3:4, :] * xp
        for j in range(3):
            xb = xb + cw_ref[j:j + 1, :] * _shift_down(xp, xh, 3 - j)
        ga, gx = _lru_gates(xb, wa_ref, wx_ref, ba_ref[...], bx_ref[...])
        _, _, a, _, sq, sgx = _lru_coeffs(ga, gx, lam_ref[...])
        aa, bb = a, sq * sgx * xb
        s = 1
        while s < tm:
            bb = aa * _roll_fill(bb, s, 0.0, False) + bb
            aa = aa * _roll_fill(aa, s, 1.0, False)
            s *= 2
        hs = aa * carry[0:1, :] + bb
        xb_ref[...] = xb
        ga_ref[...] = ga
        gx_ref[...] = gx
        hs_ref[...] = hs
        carry[0:1, :] = hs_ref[tm - 1:tm, :]
        gy, _ = _gelu_and_grad(yb_ref[...])
        out_ref[...] = (hs * gy).astype(BF16)

    hb = tm // 8
    row = _bs((tm, w), lambda i: (i, 0))
    vec = _bs((1, w), lambda i: (0, 0))
    wsp = _bs((LRU_BLOCKS, LRU_BLOCK, LRU_BLOCK), lambda i: (0, 0, 0))
    return _call(
        host, body, name=name, grid=(t // tm,),
        in_specs=[row, _bs((8, w), lambda i: (jnp.maximum(i * hb - 1, 0), 0)), _bs((tm, w), lambda i: (i, 1)),
                  _bs((4, w), lambda i: (0, 0)), vec, wsp, wsp, vec, vec, vec],
        out_specs=[row] * 5,
        out_shape=[S((t, w), F32)] * 4 + [S((t, w), BF16)],
        scratch_shapes=[pltpu.VMEM((8, w), F32)],
        compiler_params=_cp("arbitrary"),
        args=(z, z, z, conv_w, conv_b, wa, wx, ba, bx, lam))


def rec_bwd(d_out, z, xb, ga, gx, hs, conv_w, wa, wx, lam, name, host=None):
    t = z.shape[0]
    w = z.shape[1] // 2
    tm = _row_tile(t, 256)
    nt = t // tm

    def body(do_ref, xp_ref, xph_ref, yb_ref, xb_ref, ga_ref, gx_ref, hs_ref, hsh_ref, cw_ref, wa_ref, wx_ref, lam_ref,
             dz_ref, dga_ref, dgx_ref, sm_ref, c_lam, c_a, c_dxb):
        i = pl.program_id(0)

        @pl.when(i == 0)
        def _():
            sm_ref[...] = jnp.zeros_like(sm_ref)
            c_lam[...] = jnp.zeros_like(c_lam)
            c_a[...] = jnp.zeros_like(c_a)
            c_dxb[...] = jnp.zeros_like(c_dxb)
        d_o, yb, xb, hs = do_ref[...], yb_ref[...], xb_ref[...], hs_ref[...]
        lam = lam_ref[...]
        gy, dgy = _gelu_and_grad(yb)
        dz_ref[:, w:2 * w] = (d_o * hs * dgy).astype(BF16)
        sga, sp, a, one_m_a2, sq, sgx = _lru_coeffs(ga_ref[...], gx_ref[...], lam)
        aa = _shift_up(a, c_a[...], 1)
        bb = d_o * gy
        s = 1
        while s < tm:
            bb = aa * _roll_fill(bb, s, 0.0, True) + bb
            aa = aa * _roll_fill(aa, s, 1.0, True)
            s *= 2
        lmb = aa * c_lam[0:1, :] + bb
        c_a[...] = a[0:8, :]
        c_lam[...] = lmb[0:8, :]
        hprev = _shift_down(hs, jnp.where(i == nt - 1, 0.0, hsh_ref[...]), 1)
        d_sq = lmb * sgx * xb
        d_sgx = lmb * sq * xb
        d_log_a = lmb * hprev * a - d_sq * (1.0 - one_m_a2) / sq
        dga = d_log_a * (-LRU_C * sp) * sga * (1.0 - sga)
        dgx = d_sgx * sgx * (1.0 - sgx)
        dga16, dgx16 = dga.astype(BF16), dgx.astype(BF16)
        dga_ref[...] = dga16
        dgx_ref[...] = dgx16
        dxb = lmb * sq * sgx + jnp.concatenate(
            [_dot_nt(dga16[:, LRU_BLOCK * g:LRU_BLOCK * (g + 1)], wa_ref[g]) + _dot_nt(dgx16[:, LRU_BLOCK * g:LRU_BLOCK * (g + 1)], wx_ref[g])
             for g in range(LRU_BLOCKS)], axis=1)
        nxt = c_dxb[...]
        dxp = cw_ref[3:4, :] * dxb
        for j in range(3):
            dxp = dxp + cw_ref[j:j + 1, :] * _shift_up(dxb, nxt, 3 - j)
        c_dxb[...] = dxb[0:8, :]
        dz_ref[:, 0:w] = dxp.astype(BF16)
        xp = xp_ref[...]
        xph = jnp.where(i == nt - 1, 0.0, xph_ref[...])
        sm_ref[0:1, :] += jnp.sum(dga, axis=0, keepdims=True)
        sm_ref[1:2, :] += jnp.sum(dgx, axis=0, keepdims=True)
        sm_ref[2:3, :] += jnp.sum(d_log_a * (-LRU_C * sga), axis=0, keepdims=True) * (-_sigmoid(-lam))
        sm_ref[3:4, :] += jnp.sum(dxb, axis=0, keepdims=True)
        for j in range(4):
            sm_ref[4 + j:5 + j, :] += jnp.sum(dxb * _shift_down(xp, xph, 3 - j), axis=0, keepdims=True)

    hb = tm // 8
    rev = lambda c: _bs((tm, w), lambda i: (nt - 1 - i, c))
    halo = lambda c: _bs((8, w), lambda i: (jnp.maximum((nt - 1 - i) * hb - 1, 0), c))
    vec = _bs((1, w), lambda i: (0, 0))
    wsp = _bs((LRU_BLOCKS, LRU_BLOCK, LRU_BLOCK), lambda i: (0, 0, 0))
    return _call(
        host, body, name=name, grid=(nt,),
        in_specs=[rev(0), rev(0), halo(0), rev(1), rev(0), rev(0), rev(0), rev(0), halo(0),
                  _bs((4, w), lambda i: (0, 0)), wsp, wsp, vec],
        out_specs=[_bs((tm, 2 * w), lambda i: (nt - 1 - i, 0)), rev(0), rev(0), _bs((8, w), lambda i: (0, 0))],
        out_shape=[S((t, 2 * w), BF16), S((t, w), BF16), S((t, w), BF16), S((8, w), F32)],
        scratch_shapes=[pltpu.VMEM((8, w), F32)] * 3,
        compiler_params=_cp("arbitrary"),
        args=(d_out, z, z, z, xb, ga, gx, hs, hs, conv_w, wa, wx, lam))


def ple_fwd(h, gain, wpg, layer, p, p_layer, wpp, name, target=None):
    t, d = h.shape
    pd = p.shape[2]
    nk, _, rb, _ = wpg.shape
    cb = wpp.shape[3]
    tm = _row_tile(t, 512)
    row = _bs((tm, d), lambda i: (i, 0))
    in_specs = [row, _bs((1, d), lambda i: (0, 0)), _bs((nk, None, rb, d), lambda i: (0, layer, 0, 0)),
                _bs((None, tm, pd), lambda i: (p_layer, i, 0)), _bs((nk, None, pd, cb), lambda i: (0, layer, 0, 0))]

    def forward(h_ref, g_ref, wg_ref, p_ref, wp_ref, hn_ref, gp_ref, pp_ref):
        x = h_ref[...]
        hn = (x * _rstd(x) * g_ref[...]).astype(BF16)
        gp = _dot(hn[:, 0:rb], wg_ref[0])
        for k in range(1, nk):
            gp = gp + _dot(hn[:, rb * k:rb * (k + 1)], wg_ref[k])
        pp = jnp.concatenate([_dot(p_ref[...].astype(BF16), wp_ref[k]) for k in range(nk)], axis=1)
        hn_ref[...] = hn
        gp_ref[...] = gp
        pp_ref[...] = pp
        return x + _sigmoid(gp) * pp

    if target is not None:
        def body_loss(h_ref, g_ref, wg_ref, p_ref, wp_ref, t_ref, l_ref, dy_ref, hn_ref, gp_ref, pp_ref):
            @pl.when(pl.program_id(0) == 0)
            def _():
                l_ref[...] = jnp.zeros_like(l_ref)
            err = forward(h_ref, g_ref, wg_ref, p_ref, wp_ref, hn_ref, gp_ref, pp_ref) - t_ref[...]
            dy_ref[...] = err * (1.0 / d)
            l_ref[...] += jnp.sum(jnp.sum(err * err, axis=1, keepdims=True), axis=0, keepdims=True) * (0.5 / d)

        return pl.pallas_call(
            body_loss, name=name, grid=(t // tm,),
            in_specs=in_specs + [row],
            out_specs=[_bs((1, 1), lambda i: (0, 0))] + [row] * 4,
            out_shape=[S((1, 1), F32), S((t, d), F32), S((t, d), BF16), S((t, d), F32), S((t, d), F32)],
            compiler_params=_cp("arbitrary"),
        )(h, gain, wpg, p, wpp, target)

    def body(h_ref, g_ref, wg_ref, p_ref, wp_ref, o_ref, hn_ref, gp_ref, pp_ref):
        o_ref[...] = forward(h_ref, g_ref, wg_ref, p_ref, wp_ref, hn_ref, gp_ref, pp_ref)

    return pl.pallas_call(
        body, name=name, grid=(t // tm,),
        in_specs=in_specs,
        out_specs=[row] * 4,
        out_shape=[S((t, d), F32), S((t, d), BF16), S((t, d), F32), S((t, d), F32)],
        compiler_params=_cp("parallel"),
    )(h, gain, wpg, p, wpp)


def ple_bwd(dh, h, gain, wpg, layer, gp, pp, name, host=None):
    t, d = h.shape
    nk, _, rb, _ = wpg.shape
    tm = _row_tile(t, 512)

    def body(dh_ref, h_ref, g_ref, wg_ref, gp_ref, pp_ref, o_ref, dgp_ref, dpp_ref, dg_ref):
        @pl.when(pl.program_id(0) == 0)
        def _():
            dg_ref[...] = jnp.zeros_like(dg_ref)
        d_h = dh_ref[...]
        gate = _sigmoid(gp_ref[...])
        dgp = (d_h * pp_ref[...] * gate * (1.0 - gate)).astype(BF16)
        dgp_ref[...] = dgp
        dpp_ref[...] = (d_h * gate).astype(BF16)
        dhn = jnp.concatenate([_dot_nt(dgp, wg_ref[k]) for k in range(nk)], axis=1)
        dx, dgain = _rmsnorm_bwd(h_ref[...], g_ref[...], dhn)
        o_ref[...] = d_h + dx
        dg_ref[...] += dgain

    row = _bs((tm, d), lambda i: (i, 0))
    vec = _bs((1, d), lambda i: (0, 0))
    return _call(
        host, body, name=name, grid=(t // tm,),
        in_specs=[row, row, vec, _bs((nk, None, rb, d), lambda i: (0, layer, 0, 0)), row, row],
        out_specs=[row, row, row, vec],
        out_shape=[S((t, d), F32), S((t, d), BF16), S((t, d), BF16), S((1, d), F32)],
        compiler_params=_cp("arbitrary"),
        args=(dh, h, gain, wpg, gp, pp))


def _vec(a, i):
    return a[i:i + 1]


def local_step(x, p, target, w, plan=None):
    t = x.shape[0]
    tm = _row_tile(t, 512)
    grads = {}
    if plan is not None:
        plan.grads = grads
    saved = []
    h = x
    bias_q, bias_k = band_bias(w["rel_bias"])
    qg = jnp.tile(w["hyb_q_gain"], (1, N_HEADS))
    kg = jnp.tile(w["hyb_k_gain"], (1, N_HEADS))

    def run(fn, *a, name):
        hst = plan.host(name) if plan is not None else None
        out = fn(*a, name, hst)
        if hst is not None:
            plan.done(hst)
        return out

    def lru_blocks(n):
        return jnp.transpose(w[n].reshape(N_SHARD, LRU_BLOCKS, 64, LRU_BLOCK), (1, 0, 2, 3)).reshape(LRU_BLOCKS, LRU_BLOCK, LRU_BLOCK)

    for i in range(2):
        s = {}
        s["h0"] = h
        s["hn1"], s["g1"], s["u1"], s["a1"] = run(ffn_up, h, _vec(w["ffn1_norm"], i), w[f"ffn1_w_gate/{i}"], w[f"ffn1_w_up/{i}"], 0, name=f"ffn1_up_{i}")
        h = run(ffn_down, s["a1"], w[f"ffn1_w_down/{i}"], 0, h, name=f"ffn1_down_{i}")
        s["h1"] = h
        if i == 0:
            w_hyb_in = w["hyb_w_in"].reshape(N_SHARD, D_MODEL, -1)
            w_hyb_out = w["hyb_w_out"].reshape(D_MODEL, D_MODEL)
            s["hnm"], s["z"] = run(norm_mm, h, _vec(w["mix_norm"], i), w_hyb_in, name="hyb_in")
            s["qkv"] = hyb_prep(s["z"], qg, kg, "hyb_prep")
            os_, lses = [], []
            for g, dil in enumerate(DILATIONS):
                o, l = run(attn_fwd, *s["qkv"][dil], bias_q[g], dil, name=f"attn_fwd_{dil}")
                os_.append(o)
                lses.append(l)
            s["y2"], s["ya"], s["lt"] = hyb_post(s["z"], w["hyb_conv_w"], os_, lses, "hyb_post")
            h = run(functools.partial(ffn_down, scale=1.0), s["y2"], w_hyb_out.reshape(2, 1, ATTN_W, D_MODEL), 0, h, name="hyb_out")
        else:
            w_rec_in = w["rec_w_in"].reshape(N_SHARD, D_MODEL, -1)
            s["hnm"], s["z"] = run(norm_mm, h, _vec(w["mix_norm"], i), w_rec_in, name="rec_in")
            w_rec_out = w["rec_w_out"].reshape(D_MODEL, D_MODEL)
            lru_wa, lru_wx = lru_blocks("lru_wa"), lru_blocks("lru_wx")
            s["xb"], s["ga"], s["gx"], s["hs"], s["ro"] = run(
                rec_fwd, s["z"], w["rec_conv_w"], w["rec_conv_b"], lru_wa, lru_wx, w["lru_ba"], w["lru_bx"], w["lru_lambda"], name="rec_fwd")
            h = run(mm_acc, s["ro"], _bs((tm, D_MODEL), lambda r, k: (r, 0)), w_rec_out, _bs((D_MODEL, D_MODEL), lambda r, k: (0, 0)),
                    h, 1.0, 1, t, D_MODEL, tm, name="rec_out")
        s["h2"] = h
        s["hn2"], s["g2"], s["u2"], s["a2"] = run(ffn_up, h, _vec(w["ffn2_norm"], i), w[f"ffn2_w_gate/{i}"], w[f"ffn2_w_up/{i}"], 0, name=f"ffn2_up_{i}")
        h = run(ffn_down, s["a2"], w[f"ffn2_w_down/{i}"], 0, h, name=f"ffn2_down_{i}")
        s["h3"] = h
        if i == 0:
            h, s["hnp"], s["gp"], s["pp"] = ple_fwd(h, _vec(w["ple_norm"], i), w[f"ple_w_gate/{i}"], 0, p, i, w[f"ple_w_proj/{i}"], f"ple_fwd_{i}")
        else:
            loss, dh, s["hnp"], s["gp"], s["pp"] = ple_fwd(h, _vec(w["ple_norm"], i), w[f"ple_w_gate/{i}"], 0, p, i, w[f"ple_w_proj/{i}"],
                                                           f"ple_fwd_{i}", target)
        saved.append(s)

    norm_g = {n: [None, None] for n in ("ffn1_norm", "mix_norm", "ffn2_norm", "ple_norm")}
    for i in (1, 0):
        s = saved[i]
        dh_out = dh
        dh, dgp, dpp, norm_g["ple_norm"][i] = run(ple_bwd, dh_out, s["h3"], _vec(w["ple_norm"], i), w[f"ple_w_gate/{i}"], 0, s["gp"], s["pp"],
                                                  name=f"ple_bwd_{i}")
        grads["ple_w_gate"] = run(tn_mm, s["hnp"], lambda tk: _bs((tk, 256), lambda k, j: (j, k)), dgp, lambda tk: _bs((tk, D_MODEL), lambda k, j: (j, 0)),
                                  N_SHARD, t, 256, D_MODEL, S((N_SHARD, 2, 256, D_MODEL), BF16),
                                  _bs((None, None, 256, D_MODEL), lambda k, j, i=i: (k, i, 0, 0)), 1.0, grads.get("ple_w_gate"), name=f"ple_gw_gate_{i}")
        grads["ple_w_proj"] = run(tn_mm, p, lambda tk, i=i: _bs((None, tk, 256), lambda k, j: (i, j, 0)), dpp, lambda tk: _bs((tk, 256), lambda k, j: (j, k)),
                                  N_SHARD, t, 256, 256, S((N_SHARD, 2, 256, 256), BF16),
                                  _bs((None, None, 256, 256), lambda k, j, i=i: (k, i, 0, 0)), 1.0, grads.get("ple_w_proj"), name=f"ple_gw_proj_{i}")
        dh_out = dh
        dg, du, dh16 = run(ffn_bwd_act, dh_out, w[f"ffn2_w_down/{i}"], 0, s["g2"], s["u2"], name=f"ffn2_bwd_act_{i}")
        ffn_wgrads("ffn2", s["hn2"], dh16, s["a2"], dg, du, i, grads, run)
        dh, norm_g["ffn2_norm"][i] = run(ffn_bwd_in, dg, du, w[f"ffn2_w_gate/{i}"], w[f"ffn2_w_up/{i}"], 0, s["h2"], _vec(w["ffn2_norm"], i), dh_out,
                                         name=f"ffn2_bwd_in_{i}")
        dh_out = dh
        if i == 1:
            d_o = nt_mm(dh_out, w_rec_out, "rec_bwd_out")
            grads["rec_w_out"] = run(tn_mm, s["ro"], lambda tk: _bs((tk, 256), lambda k, j: (j, k)), dh_out, lambda tk: _bs((tk, D_MODEL), lambda k, j: (j, 0)),
                                     N_SHARD, t, 256, D_MODEL, S((N_SHARD, 256, D_MODEL), BF16), _bs((None, 256, D_MODEL), lambda k, j: (k, 0, 0)),
                                     1.0, None, name="rec_gw_out").reshape(N_SHARD, 1, 256, D_MODEL)
            dz, dga, dgx, small = run(rec_bwd, d_o, s["z"], s["xb"], s["ga"], s["gx"], s["hs"], w["rec_conv_w"], lru_wa, lru_wx, w["lru_lambda"],
                                      name="rec_bwd")
            blk = lambda tk: _bs((tk, LRU_BLOCK), lambda k, j: (j, k))
            for nm, dgt in (("lru_wa", dga), ("lru_wx", dgx)):
                gw = run(tn_mm, s["xb"], blk, dgt, blk, LRU_BLOCKS, t, LRU_BLOCK, LRU_BLOCK, S((LRU_BLOCKS, LRU_BLOCK, LRU_BLOCK), BF16),
                         _bs((None, LRU_BLOCK, LRU_BLOCK), lambda k, j: (k, 0, 0)), 1.0, None, name="rec_gw_" + nm)
                grads[nm] = jnp.transpose(gw.reshape(LRU_BLOCKS, N_SHARD, 64, LRU_BLOCK), (1, 0, 2, 3)).reshape(N_SHARD, 1, LRU_BLOCKS, 64, LRU_BLOCK)
            grads["lru_ba"], grads["lru_bx"], grads["lru_lambda"], grads["rec_conv_b"] = (small[r:r + 1] for r in range(4))
            grads["rec_conv_w"] = small[4:8]
            nb_, bw = N_SHARD, 512
            w_in, nm_in = w_rec_in, "rec_w_in"
        else:
            dy = nt_mm(dh_out, w_hyb_out, "hyb_bwd_out")
            grads["hyb_w_out"] = run(tn_mm, s["y2"], lambda tk: _bs((None, tk, 256), lambda k, j: (k // 2, j, k % 2)), dh_out,
                                     lambda tk: _bs((tk, D_MODEL), lambda k, j: (j, 0)),
                                     N_SHARD, t, 256, D_MODEL, S((N_SHARD, 256, D_MODEL), BF16), _bs((None, 256, D_MODEL), lambda k, j: (k, 0, 0)),
                                     1.0, None, name="hyb_gw_out").reshape(N_SHARD, 1, 256, D_MODEL)
            delta, dya = attn_delta(dy, s["ya"], "attn_delta")
            dqs, dks, dvs, dbs = [], [], [], []
            for g, dil in enumerate(DILATIONS):
                dq, db = run(attn_bwd_dq, *s["qkv"][dil], dya[dil], s["lt"][dil], delta[dil], bias_q[g], dil, name=f"attn_bwd_dq_{dil}")
                dk, dv = run(attn_bwd_dkv, *s["qkv"][dil], dya[dil], s["lt"][dil], delta[dil], bias_k[g], dil, name=f"attn_bwd_dkv_{dil}")
                dqs.append(dq); dks.append(dk); dvs.append(dv); dbs.append(db)
            grads["rel_bias"] = rel_bias_grad(dbs, "rel_bias_grad")
            dz, grads["hyb_conv_w"], dqg, dkg = hyb_dz(s["z"], dy, w["hyb_conv_w"], qg, kg, dqs, dks, dvs, "hyb_dz")
            grads["hyb_q_gain"] = jnp.sum(dqg.reshape(N_HEADS, HEAD_DIM), axis=0, keepdims=True)
            grads["hyb_k_gain"] = jnp.sum(dkg.reshape(N_HEADS, HEAD_DIM), axis=0, keepdims=True)
            nb_, bw = N_SHARD, 768
            w_in, nm_in = w_hyb_in, "hyb_w_in"
        grads[nm_in] = run(tn_mm, s["hnm"], lambda tk: _bs((tk, D_MODEL), lambda k, j: (j, 0)), dz, lambda tk, bw=bw: _bs((tk, bw), lambda k, j: (j, k)),
                           nb_, t, D_MODEL, bw, S((nb_, D_MODEL, bw), BF16), _bs((None, D_MODEL, bw), lambda k, j: (k, 0, 0)),
                           1.0, None, name=f"mix_gw_in_{i}").reshape(nb_, 1, D_MODEL, bw)
        dh, norm_g["mix_norm"][i] = run(
            nt_acc_normbwd, [(dz, _bs((_row_tile(t, BWD_IN_ROWS), nb_ * bw), lambda r: (r, 0)), w_in, _bs((nb_, D_MODEL, bw), lambda r: (0, 0, 0)),
                              lambda x_ref, k, bw=bw: x_ref[:, k * bw:(k + 1) * bw])],
            nb_, s["h1"], _vec(w["mix_norm"], i), dh_out, name=f"mix_bwd_in_{i}")
        dh_out = dh
        dg, du, dh16 = run(ffn_bwd_act, dh_out, w[f"ffn1_w_down/{i}"], 0, s["g1"], s["u1"], name=f"ffn1_bwd_act_{i}")
        ffn_wgrads("ffn1", s["hn1"], dh16, s["a1"], dg, du, i, grads, run)
        dh, norm_g["ffn1_norm"][i] = run(ffn_bwd_in, dg, du, w[f"ffn1_w_gate/{i}"], w[f"ffn1_w_up/{i}"], 0, s["h0"], _vec(w["ffn1_norm"], i), dh_out,
                                         name=f"ffn1_bwd_in_{i}")
    for n, (g0, g1) in norm_g.items():
        grads[n] = jnp.concatenate([g0, g1], axis=0)
    return loss, dh, grads


def gather_weights(shards, name):
    n = len(shards)

    def body(*refs):
        ins, outs = refs[:n], refs[n:2 * n]
        send1, recv1, send2, recv2, lsem = refs[2 * n:]
        x, y, c, k, chips, kk = _place()
        sib = (x, y, 1 - c)

        def remote(src, dst, ssem, rsem, to):
            return pltpu.make_async_remote_copy(src_ref=src, dst_ref=dst, send_sem=ssem, recv_sem=rsem, device_id=to, device_id_type=MESH)

        local = [pltpu.make_async_copy(ins[a], outs[a].at[k], lsem.at[a]) for a in range(n)]
        for cp in local:
            cp.start()
        sends = []
        for a in range(n):
            for j, chip in enumerate(chips):
                cp = remote(ins[a].at[c], outs[a].at[k, c], send1.at[3 * a + j], recv1.at[3 * a + j], (*chip, c))
                cp.start()
                sends.append(cp)
        for a in range(n):
            for j, chip in enumerate(chips):
                remote(ins[a].at[c], outs[a].at[kk[j], c], send1.at[3 * a + j], recv1.at[3 * a + j], (*chip, c)).wait_recv()
                cp = remote(outs[a].at[kk[j], c], outs[a].at[kk[j], c], send2.at[3 * a + j], recv2.at[3 * a + j], sib)
                cp.start()
                sends.append(cp)
        for a in range(n):
            for j in range(3):
                remote(outs[a].at[kk[j], 1 - c], outs[a].at[kk[j], 1 - c], send2.at[3 * a + j], recv2.at[3 * a + j], sib).wait_recv()
        for cp in sends:
            cp.wait_send()
        for cp in local:
            cp.wait()

    return pl.pallas_call(
        body, name=name,
        in_specs=[_ANY] * n, out_specs=[_ANY] * n,
        out_shape=[S((N_SHARD,) + s.shape, s.dtype) for s in shards],
        scratch_shapes=[pltpu.SemaphoreType.DMA((3 * n,))] * 4 + [pltpu.SemaphoreType.DMA((n,))],
    )(*shards)


def exchange_cores(rs, name):
    n = len(rs)

    def body(*refs):
        outs = refs[n:2 * n]
        send, recv = refs[2 * n:]
        x, y, c = lax.axis_index("x"), lax.axis_index("y"), lax.axis_index("c")
        sends = []
        for a in range(n):
            for k in range(N_SHARD):
                slot = outs[a].at[2 * k + c]
                cp = _remote(slot, slot, send.at[N_SHARD * a + k], recv.at[N_SHARD * a + k], (x, y, 1 - c))
                cp.start()
                sends.append(cp)
        for a in range(n):
            for k in range(N_SHARD):
                slot = outs[a].at[2 * k + 1 - c]
                _remote(slot, slot, send.at[N_SHARD * a + k], recv.at[N_SHARD * a + k], (x, y, 1 - c)).wait_recv()
        for cp in sends:
            cp.wait_send()

    return pl.pallas_call(
        body, name=name,
        in_specs=[_ANY] * n, out_specs=[_ANY] * n,
        out_shape=[S(r.shape, r.dtype) for r in rs],
        input_output_aliases={a: a for a in range(n)},
        scratch_shapes=[pltpu.SemaphoreType.DMA((N_SHARD * n,))] * 2,
    )(*rs)


def allgather8(a, name):
    def body(a_ref, o_ref, send, recv, lsem):
        x, y, c = lax.axis_index("x"), lax.axis_index("y"), lax.axis_index("c")
        me = 4 * x + 2 * y + c
        local = pltpu.make_async_copy(a_ref, o_ref.at[me], lsem)
        local.start()
        cps = []
        for f in range(1, N_DEV):
            fx, fy, fc = (f >> 2) & 1, (f >> 1) & 1, f & 1
            peer = (1 - x if fx else x, 1 - y if fy else y, 1 - c if fc else c)
            cp = pltpu.make_async_remote_copy(src_ref=a_ref, dst_ref=o_ref.at[me], send_sem=send.at[f - 1], recv_sem=recv.at[f - 1],
                                              device_id=peer, device_id_type=MESH)
            cp.start()
            cps.append((cp, 4 * peer[0] + 2 * peer[1] + peer[2], f))
        for cp, pidx, f in cps:
            pltpu.make_async_remote_copy(src_ref=a_ref, dst_ref=o_ref.at[pidx], send_sem=send.at[f - 1], recv_sem=recv.at[f - 1],
                                         device_id=(x, y, c), device_id_type=MESH).wait_recv()
        for cp, _, _ in cps:
            cp.wait_send()
        local.wait()

    return pl.pallas_call(
        body, name=name, in_specs=[_ANY], out_specs=_ANY,
        out_shape=S((N_DEV,) + a.shape, a.dtype),
        scratch_shapes=[pltpu.SemaphoreType.DMA((N_DEV - 1,)), pltpu.SemaphoreType.DMA((N_DEV - 1,)), pltpu.SemaphoreType.DMA],
    )(a)


def sum8(a, name):
    _, r, c = a.shape

    def body(a_ref, o_ref):
        acc = a_ref[0]
        for j in range(1, N_DEV):
            acc = acc + a_ref[j]
        o_ref[...] = acc

    return pl.pallas_call(
        body, name=name, in_specs=[_bs((N_DEV, r, c), lambda: (0, 0, 0))], out_specs=_bs((r, c), lambda: (0, 0)),
        out_shape=S((r, c), F32),
    )(a)


def adamw(w, m, v, g, name):
    nl, r, c = w.shape
    tr = _row_tile(r, 256)
    summed = g.ndim == 4

    def body(w_ref, m_ref, v_ref, g_ref, go_ref, d_ref, mo_ref, vo_ref):
        if summed:
            gr = g_ref[0].astype(F32)
            for j in range(1, N_DEV):
                gr = gr + g_ref[j].astype(F32)
        else:
            gr = g_ref[...]
        m_new = ADAM_B1 * m_ref[...] + (1.0 - ADAM_B1) * gr
        v_new = ADAM_B2 * v_ref[...] + (1.0 - ADAM_B2) * (gr * gr)
        m_hat = m_new / (1.0 - ADAM_B1 ** ADAM_STEP)
        v_hat = v_new / (1.0 - ADAM_B2 ** ADAM_STEP)
        go_ref[...] = gr
        d_ref[...] = -ADAM_LR * (m_hat / (jnp.sqrt(v_hat) + ADAM_EPS) + ADAM_WD * w_ref[...])
        mo_ref[...] = m_new
        vo_ref[...] = v_new

    row = _bs((None, tr, c), lambda l, i: (l, i, 0))
    gspec = _bs((N_DEV, None, tr, c), lambda l, i: (0, l, i, 0)) if summed else row
    return pl.pallas_call(
        body, name=name, grid=(nl, r // tr),
        in_specs=[row, row, row, gspec], out_specs=[row] * 4, out_shape=[S((nl, r, c), F32)] * 4,
        compiler_params=_cp("parallel", "parallel"),
    )(w, m, v, g)


WEIGHTS = ["rel_bias", "ffn1_norm", "ffn1_w_gate", "ffn1_w_up", "ffn1_w_down", "mix_norm", "hyb_w_in", "hyb_conv_w", "hyb_q_gain",
           "hyb_k_gain", "hyb_w_out", "rec_w_in", "rec_conv_w", "rec_conv_b", "lru_wa", "lru_ba", "lru_wx", "lru_bx", "lru_lambda",
           "rec_w_out", "ffn2_norm", "ffn2_w_gate", "ffn2_w_up", "ffn2_w_down", "ple_norm", "ple_w_gate", "ple_w_proj"]
BIG = ["ffn1_w_gate", "ffn1_w_up", "ffn1_w_down", "hyb_w_in", "hyb_w_out", "rec_w_in", "lru_wa", "lru_wx", "rec_w_out",
       "ffn2_w_gate", "ffn2_w_up", "ffn2_w_down", "ple_w_gate", "ple_w_proj"]
SMALL_SHARDED = ["hyb_conv_w", "rec_conv_w", "rec_conv_b", "lru_ba", "lru_bx", "lru_lambda"]
SMALL = ["rel_bias", "ffn1_norm", "mix_norm", "ffn2_norm", "ple_norm", "hyb_q_gain", "hyb_k_gain"] + SMALL_SHARDED
PACK_W = 1024
PER_LAYER = ["ffn1_w_gate", "ffn1_w_up", "ffn1_w_down", "ffn2_w_gate", "ffn2_w_up", "ffn2_w_down", "ple_w_gate", "ple_w_proj"]
FIRST = ["ffn1_w_gate/0", "ffn1_w_up/0"]
LAST = ["ffn1_w_down"]
GATHER_PLAN = {
    "ffn1_up_0": ["ffn1_w_down/0", "hyb_w_in"],
    "ffn1_down_0": ["hyb_w_out", "ple_w_gate/0", "ple_w_proj/0"],
    "hyb_in": ["ffn2_w_gate/0"],
    "attn_fwd_1": ["ffn2_w_up/0"],
    "attn_fwd_4": ["ffn2_w_down/0"],
    "attn_fwd_16": ["ffn1_w_gate/1"],
    "ffn2_up_0": ["ffn1_w_up/1"],
    "ffn2_down_0": ["lru_wa", "lru_wx", "rec_w_out"],
    "ffn1_up_1": ["ffn1_w_down/1", "rec_w_in"],
    "rec_in": ["ffn2_w_down/1", "ple_w_gate/1", "ple_w_proj/1"],
    "rec_fwd": ["ffn2_w_gate/1", "ffn2_w_up/1"],
}
SCATTER_PLAN = {
    "ple_gw_proj_1": [("ple_w_gate", 1)],
    "ffn2_bwd_act_1": [("ple_w_proj", 1)],
    "ffn2_bwd_in_1": [("ffn2_w_gate", 1)],
    "rec_bwd": [("ffn2_w_up", 1), ("ffn2_w_down", 1)],
    "mix_bwd_in_1": [("rec_w_in", 0), ("rec_w_out", 0), ("lru_wa", 0), ("lru_wx", 0)],
    "ffn1_bwd_in_1": [("ffn1_w_gate", 1)],
    "ple_bwd_0": [("ffn1_w_up", 1)],
    "ple_gw_proj_0": [("ple_w_gate", 0)],
    "ffn2_bwd_act_0": [("ple_w_proj", 0)],
    "ffn2_gw_0_gate": [("ffn1_w_down", 1)],
    "ffn2_bwd_in_0": [("ffn2_w_gate", 0)],
    "attn_bwd_dq_1": [("ffn2_w_down", 0)],
    "attn_bwd_dkv_1": [("ffn2_w_up", 0)],
    "mix_bwd_in_0": [("hyb_w_in", 0), ("hyb_w_out", 0)],
    "ffn1_gw_0_up": [("ffn1_w_gate", 0)],
    "ffn1_gw_0_down": [("ffn1_w_up", 0)],
    "ffn1_bwd_in_0": [("ffn1_w_down", 0)],
}
FORWARD_PLAN = {
    "ffn1_bwd_in_1": ["rec_w_in", "rec_w_out", "lru_wa", "lru_wx"],
    "ffn2_bwd_in_0": ["ple_w_gate", "ple_w_proj"],
    "mix_bwd_in_0": ["ffn2_w_gate", "ffn2_w_up", "ffn2_w_down"],
    "ffn1_gw_0_up": ["hyb_w_in", "hyb_w_out"],
    "ffn1_bwd_in_0": ["ffn1_w_gate", "ffn1_w_up"],
}


class Plan:
    def __init__(self, shards, w):
        self.shards, self.w, self.grads, self.landed = shards, w, None, {}

    def host(self, kname):
        if kname in GATHER_PLAN:
            h = Host("gather", [self.shards[n] for n in GATHER_PLAN[kname]])
            h.names = GATHER_PLAN[kname]
            return h
        if kname in SCATTER_PLAN or kname in FORWARD_PLAN:
            items = SCATTER_PLAN.get(kname, [])
            fwd = FORWARD_PLAN.get(kname, [])
            h = Host("scatter", [(self.grads[n], lay, self.landed.get(n)) for n, lay in items], [self.landed[n] for n in fwd])
            h.names = [n for n, _ in items] + fwd
            return h
        return None

    def done(self, h):
        for n, o in zip(h.names, h.outs):
            if h.kind == "gather":
                self.w[n] = o
            else:
                self.landed[n] = o


def _halves(a):
    if a.shape[0] == 2:
        return a
    return a.reshape((2, a.shape[1] // 2) + a.shape[2:])


def _pack_rows(arrs, width):
    rows, offs, r0 = [], [], 0
    for a in arrs:
        if a.shape[1] > width:
            a = a.reshape(-1, width)
        rows.append(jnp.pad(a, ((0, 0), (0, width - a.shape[1]))))
        offs.append(r0)
        r0 += a.shape[0]
    pad = (-r0) % 8
    if pad:
        rows.append(jnp.zeros((pad, width), F32))
    return jnp.concatenate(rows, axis=0), offs


def kernel(x, p, rel_bias, ffn1_norm, ffn1_w_gate, ffn1_w_up, ffn1_w_down, mix_norm, hyb_w_in, hyb_conv_w, hyb_q_gain, hyb_k_gain, hyb_w_out, rec_w_in, rec_conv_w, rec_conv_b, lru_wa, lru_ba, lru_wx, lru_bx, lru_lambda, rec_w_out, ffn2_norm, ffn2_w_gate, ffn2_w_up, ffn2_w_down, ple_norm, ple_w_gate, ple_w_proj, loss_target, m_rel_bias, m_ffn1_norm, m_ffn1_w_gate, m_ffn1_w_up, m_ffn1_w_down, m_mix_norm, m_hyb_w_in, m_hyb_conv_w, m_hyb_q_gain, m_hyb_k_gain, m_hyb_w_out, m_rec_w_in, m_rec_conv_w, m_rec_conv_b, m_lru_wa, m_lru_ba, m_lru_wx, m_lru_bx, m_lru_lambda, m_rec_w_out, m_ffn2_norm, m_ffn2_w_gate, m_ffn2_w_up, m_ffn2_w_down, m_ple_norm, m_ple_w_gate, m_ple_w_proj, v_rel_bias, v_ffn1_norm, v_ffn1_w_gate, v_ffn1_w_up, v_ffn1_w_down, v_mix_norm, v_hyb_w_in, v_hyb_conv_w, v_hyb_q_gain, v_hyb_k_gain, v_hyb_w_out, v_rec_w_in, v_rec_conv_w, v_rec_conv_b, v_lru_wa, v_lru_ba, v_lru_wx, v_lru_bx, v_lru_lambda, v_rec_w_out, v_ffn2_norm, v_ffn2_w_gate, v_ffn2_w_up, v_ffn2_w_down, v_ple_norm, v_ple_w_gate, v_ple_w_proj):
    given = dict(locals())
    wts = {n: given[n] for n in WEIGHTS}
    k_chip = 2 * lax.axis_index("x") + lax.axis_index("y")

    shards = {}
    for n in BIG:
        b16 = wts[n].astype(BF16)
        if n in PER_LAYER:
            shards[n + "/0"], shards[n + "/1"] = b16[0:1], b16[1:2]
        else:
            shards[n] = b16
    first = gather_weights([_halves(shards[n]) for n in FIRST], "gather_first")
    w = {n: g.reshape((N_SHARD,) + shards[n].shape) for n, g in zip(FIRST, first)}
    plan = Plan(shards, w)
    sm2d = {n: wts[n].reshape(-1, wts[n].shape[-1]) for n in SMALL_SHARDED}
    slab, offs = _pack_rows([sm2d[n] for n in SMALL_SHARDED], 256)
    slabs = allgather8(slab, "gather_small")[0::2]
    for n, o in zip(SMALL_SHARDED, offs):
        r, cw = sm2d[n].shape
        w[n] = jnp.concatenate([slabs[kc, o:o + r, :cw] for kc in range(N_SHARD)], axis=1)
    for n in SMALL:
        if n not in SMALL_SHARDED:
            w[n] = wts[n]

    loss, dx, grads = local_step(x[0], p.reshape(p.shape[0], p.shape[2], p.shape[3]), loss_target[0], w, plan)
    loss = lax.psum(loss[0, 0], ("x", "y", "c"))

    for n, r8 in zip(LAST, exchange_cores([plan.landed[n] for n in LAST], "exchange_cores")):
        plan.landed[n] = r8
    out = {}
    for n in BIG:
        r8 = plan.landed[n]
        shp = wts[n].shape
        shp3 = shp if len(shp) == 3 else (shp[0], -1, shp[-1])
        three = lambda a: a.reshape(shp3)
        res = adamw(three(wts[n]), three(given["m_" + n]), three(given["v_" + n]), r8.reshape((N_DEV,) + three(wts[n]).shape), "adamw_" + n)
        out[n] = [a.reshape(shp) for a in res]
    g2d = [grads[n].reshape(-1, grads[n].shape[-1]) if n != "rel_bias" else grads[n].reshape(1, -1) for n in SMALL]
    gslab, goffs = _pack_rows(g2d, PACK_W)
    gsum = sum8(allgather8(gslab, "gather_small_grads"), "sum_small_grads")
    for n, o, g in zip(SMALL, goffs, g2d):
        shp = wts[n].shape
        r, cw = g.shape
        gs = gsum[o:o + r, :cw]
        if n in SMALL_SHARDED:
            sw = shp[-1]
            gs = lax.dynamic_slice_in_dim(gs, k_chip * sw, sw, axis=1)
        three = lambda a: a.reshape((1, -1, shp[-1]))
        res = adamw(three(wts[n]), three(given["m_" + n]), three(given["v_" + n]), three(gs), "adamw_" + n)
        out[n] = [a.reshape(shp) for a in res]
    return (loss, dx[None], *[out[n][0] for n in WEIGHTS], *[out[n][1] for n in WEIGHTS],
            *[out[n][2] for n in WEIGHTS], *[out[n][3] for n in WEIGHTS])
```

```python
import functools
import math

import numpy as np
import jax
import jax.numpy as jnp
from jax import lax
from jax.experimental import pallas as pl
from jax.experimental.pallas import tpu as pltpu

F32, BF16 = jnp.float32, jnp.bfloat16
S = jax.ShapeDtypeStruct
MESH = pl.DeviceIdType.MESH

D_MODEL = 1024
N_SHARD = 4
N_DEV = 8
HEAD_DIM = 64
N_HEADS = 8
ATTN_W = N_HEADS * HEAD_DIM
CONV_W = 512
BAND = 128
DILATIONS = (1, 4, 16)
REL_BUCKETS = 32
REL_MAX_DIST = 2048
LRU_BLOCKS = 4
LRU_BLOCK = 256
LRU_C = 8.0
EPS = 1e-6
NEG = -1e30
VMEM_LIMIT = 56 * 1024 * 1024
FFN_ROWS = 1024
FFN_SHARDS_ROWS = 512
TN_ROWS = 2048
BWD_IN_ROWS = 512

ADAM_LR, ADAM_B1, ADAM_B2, ADAM_EPS, ADAM_WD, ADAM_STEP = 0.001, 0.9, 0.999, 1e-08, 0.01, 10


def _cp(*sem):
    return pltpu.CompilerParams(dimension_semantics=sem, vmem_limit_bytes=VMEM_LIMIT)


def _bs(shape, imap):
    return pl.BlockSpec(shape, imap)


def _row_tile(t, want):
    for cand in range(min(want, t) // 8 * 8, 0, -8):
        if t % cand == 0:
            return cand
    return t


_ANY = pl.BlockSpec(memory_space=pl.ANY)


def _place():
    x, y, c = lax.axis_index("x"), lax.axis_index("y"), lax.axis_index("c")
    chips = [(1 - x, y), (x, 1 - y), (1 - x, 1 - y)]
    return x, y, c, 2 * x + y, chips, [2 * cx + cy for cx, cy in chips]


def _remote(src, dst, ssem, rsem, to):
    return pltpu.make_async_remote_copy(src_ref=src, dst_ref=dst, send_sem=ssem, recv_sem=rsem, device_id=to, device_id_type=MESH)


class Host:
    def __init__(self, kind, items, forwards=()):
        self.kind, self.items, self.forwards, self.outs = kind, items, list(forwards), None

    def n_sems(self):
        return 3 * len(self.items) + N_SHARD * len(self.forwards), len(self.items)

    def operands(self):
        if self.kind == "gather":
            return list(self.items), [S((N_SHARD,) + s.shape, s.dtype) for s in self.items], {}
        xin, shapes, alias = [], [], {}
        for a, (g, _, r_prev) in enumerate(self.items):
            xin.append(g)
            if r_prev is not None:
                alias[len(xin)] = a
                xin.append(r_prev)
            shapes.append(S((N_DEV,) + g.shape[1:], g.dtype))
        for f, r in enumerate(self.forwards):
            alias[len(xin)] = len(self.items) + f
            xin.append(r)
            shapes.append(S(r.shape, r.dtype))
        return xin, shapes, alias

    def copies(self, xi, xo, send, recv, lsem):
        x, y, c, k, chips, kk = _place()
        starts, waits = [], []
        pos = 0
        for f in range(len(self.forwards)):
            arr = xo[len(self.items) + f]
            for kq in range(N_SHARD):
                sem = 3 * len(self.items) + N_SHARD * f + kq
                cp = _remote(arr.at[2 * kq + c], arr.at[2 * kq + c], send.at[sem], recv.at[sem], (x, y, 1 - c))
                starts.append((cp, "start"))
                waits.append((cp, "wait_send"))
                other = arr.at[2 * kq + 1 - c]
                waits.append((_remote(other, other, send.at[sem], recv.at[sem], (x, y, 1 - c)), "wait_recv"))
        for a, item in enumerate(self.items):
            if self.kind == "gather":
                src_of = lambda chip_idx, s=xi[a]: s
                dst_of = lambda chip_idx, o=xo[a]: o.at[chip_idx]
                mine, theirs = k, kk
            else:
                g_ref = xi[pos]
                pos += 1 if item[2] is None else 2
                lay = item[1]
                src_of = lambda chip_idx, g=g_ref, lay=lay: g.at[chip_idx, lay]
                dst_of = lambda slot, o=xo[a], lay=lay: o.at[slot, lay]
                mine, theirs = 2 * k + c, [2 * kj + c for kj in kk]
            own_src = src_of(k)
            local = pltpu.make_async_copy(own_src, dst_of(mine), lsem.at[a])
            starts.append((local, "start"))
            waits.append((local, "wait"))
            for j, chip in enumerate(chips):
                src = own_src if self.kind == "gather" else src_of(kk[j])
                cp = _remote(src, dst_of(mine), send.at[3 * a + j], recv.at[3 * a + j], (*chip, c))
                starts.append((cp, "start"))
                waits.append((cp, "wait_send"))
                waits.append((_remote(own_src, dst_of(theirs[j]), send.at[3 * a + j], recv.at[3 * a + j], (*chip, c)), "wait_recv"))
        return starts, waits


def _call(host, body, *, name, grid, in_specs, out_specs, out_shape, scratch_shapes=(), compiler_params=None, args, aliases=None):
    aliases = dict(aliases or {})
    if host is None:
        return pl.pallas_call(body, name=name, grid=grid, in_specs=in_specs, out_specs=out_specs, out_shape=out_shape,
                              scratch_shapes=list(scratch_shapes), input_output_aliases=aliases, compiler_params=compiler_params)(*args)
    single = not isinstance(out_shape, (list, tuple))
    out_specs_l = [out_specs] if single else list(out_specs)
    out_shape_l = [out_shape] if single else list(out_shape)
    n_in, n_out, n_scr = len(in_specs), len(out_shape_l), len(scratch_shapes)
    xin, xshapes, xalias = host.operands()
    n_items = len(xshapes)
    n_rsem, n_lsem = host.n_sems()
    for i_in, i_out in xalias.items():
        aliases[n_in + i_in] = n_out + i_out
    nd = len(grid)

    def hosted(*refs):
        ins, xi = refs[:n_in], refs[n_in:n_in + len(xin)]
        o0 = n_in + len(xin)
        outs, xo = refs[o0:o0 + n_out], refs[o0 + n_out:o0 + n_out + n_items]
        s0 = o0 + n_out + n_items
        scr = refs[s0:s0 + n_scr]
        send, recv, lsem = refs[s0 + n_scr:]
        first = functools.reduce(jnp.logical_and, [pl.program_id(d) == 0 for d in range(nd)])
        last = functools.reduce(jnp.logical_and, [pl.program_id(d) == grid[d] - 1 for d in range(nd)])
        starts, waits = host.copies(xi, xo, send, recv, lsem)

        @pl.when(first)
        def _():
            for cp, how in starts:
                getattr(cp, how)()
        body(*ins, *outs, *scr)

        @pl.when(last)
        def _():
            for cp, how in waits:
                getattr(cp, how)()

    res = pl.pallas_call(
        hosted, name=name, grid=grid,
        in_specs=list(in_specs) + [_ANY] * len(xin),
        out_specs=out_specs_l + [_ANY] * n_items,
        out_shape=out_shape_l + xshapes,
        scratch_shapes=list(scratch_shapes) + [pltpu.SemaphoreType.DMA((n_rsem,)), pltpu.SemaphoreType.DMA((n_rsem,)),
                                               pltpu.SemaphoreType.DMA((max(n_lsem, 1),))],
        input_output_aliases=aliases,
        compiler_params=pltpu.CompilerParams(dimension_semantics=("arbitrary",) * nd, vmem_limit_bytes=VMEM_LIMIT),
    )(*args, *xin)
    host.outs = list(res[n_out:])
    return res[0] if single else list(res[:n_out])


def _rstd(x):
    return lax.rsqrt(jnp.mean(x * x, axis=-1, keepdims=True) + EPS)


def _sigmoid(x):
    return 1.0 / (1.0 + jnp.exp(-x))


def _dot(a, b):
    return jnp.dot(a, b, preferred_element_type=F32)


def _dot_nt(a, b):
    return lax.dot_general(a, b, (((1,), (1,)), ((), ())), preferred_element_type=F32)


def _dot_tn(a, b):
    return lax.dot_general(a, b, (((0,), (0,)), ((), ())), preferred_element_type=F32)


def _seg_dot(x, seg_bf16):
    hi = x.astype(BF16)
    lo = (x - hi.astype(F32)).astype(BF16)
    return _dot(hi, seg_bf16) + _dot(lo, seg_bf16)


def _shift_down(x, prev8, s):
    if s == 0:
        return x
    tm = x.shape[0]
    row = lax.broadcasted_iota(jnp.int32, x.shape, 0)
    main = jnp.where(row >= s, pltpu.roll(x, s, axis=0), 0.0)
    row8 = lax.broadcasted_iota(jnp.int32, prev8.shape, 0)
    head = jnp.where(row8 < s, pltpu.roll(prev8, s, axis=0), 0.0)
    if tm == 8:
        return main + head
    return main + jnp.concatenate([head, jnp.zeros((tm - 8, x.shape[1]), x.dtype)], axis=0)


def _shift_up(x, next8, s):
    if s == 0:
        return x
    tm = x.shape[0]
    row = lax.broadcasted_iota(jnp.int32, x.shape, 0)
    main = jnp.where(row < tm - s, pltpu.roll(x, tm - s, axis=0), 0.0)
    row8 = lax.broadcasted_iota(jnp.int32, next8.shape, 0)
    tail = jnp.where(row8 >= 8 - s, pltpu.roll(next8, 8 - s, axis=0), 0.0)
    if tm == 8:
        return main + tail
    return main + jnp.concatenate([jnp.zeros((tm - 8, x.shape[1]), x.dtype), tail], axis=0)


def _roll_fill(x, s, fill, up):
    tm = x.shape[0]
    row = lax.broadcasted_iota(jnp.int32, x.shape, 0)
    if up:
        return jnp.where(row < tm - s, pltpu.roll(x, tm - s, axis=0), fill)
    return jnp.where(row >= s, pltpu.roll(x, s, axis=0), fill)


def _log1p(y):
    u = 1.0 + y
    return jnp.where(u == 1.0, y, jnp.log(u) * (y / jnp.where(u == 1.0, 1.0, u - 1.0)))


def _softplus(x):
    return jnp.maximum(x, 0.0) + _log1p(jnp.exp(-jnp.abs(x)))


def _neg_expm1(y):
    series = -y * (1.0 + y * (0.5 + y * (1.0 / 6.0 + y * (1.0 / 24.0 + y * (1.0 / 120.0)))))
    return jnp.where(jnp.abs(y) < 0.03, series, 1.0 - jnp.exp(y))


_GELU_C = math.sqrt(2.0 / math.pi)


def _gelu_and_grad(x):
    inner = _GELU_C * (x + 0.044715 * x * x * x)
    t = jnp.tanh(inner)
    g = 0.5 * x * (1.0 + t)
    dg = 0.5 * (1.0 + t) + 0.5 * x * (1.0 - t * t) * _GELU_C * (1.0 + 3.0 * 0.044715 * x * x)
    return g, dg


def _rmsnorm_bwd(x, gain, dy):
    r = _rstd(x)
    xhat = x * r
    dxhat = dy * gain
    dx = r * (dxhat - xhat * jnp.mean(dxhat * xhat, axis=-1, keepdims=True))
    return dx, jnp.sum(dy * xhat, axis=0, keepdims=True)


def ffn_up(h, gain, wg, wu, layer, name, host=None):
    t, d = h.shape
    nk, _, _, f = wg.shape
    tm = _row_tile(t, FFN_SHARDS_ROWS)

    def body(h_ref, g_ref, wg_ref, wu_ref, hn_ref, gg_ref, uu_ref, aa_ref):
        x = h_ref[...]
        hn = (x * _rstd(x) * g_ref[...]).astype(BF16)
        hn_ref[...] = hn
        for k in range(nk):
            g = _dot(hn, wg_ref[k])
            u = _dot(hn, wu_ref[k])
            s = _sigmoid(g)
            silu = g * s
            gg_ref[k] = (u * (s * (1.0 + g * (1.0 - s)))).astype(BF16)
            uu_ref[k] = silu.astype(BF16)
            aa_ref[k] = (silu * u).astype(BF16)

    wspec = _bs((nk, None, d, f), lambda i: (0, layer, 0, 0))
    aspec = _bs((nk, tm, f), lambda i: (0, i, 0))
    return _call(
        host, body, name=name, grid=(t // tm,),
        in_specs=[_bs((tm, d), lambda i: (i, 0)), _bs((1, d), lambda i: (0, 0)), wspec, wspec],
        out_specs=[_bs((tm, d), lambda i: (i, 0)), aspec, aspec, aspec],
        out_shape=[S((t, d), BF16), S((nk, t, f), BF16), S((nk, t, f), BF16), S((nk, t, f), BF16)],
        compiler_params=_cp("parallel"),
        args=(h, gain, wg, wu))


def mm_acc(a, a_spec, b, b_spec, res, scale, nk, t, n, tm, name, host=None):
    def body(a_ref, b_ref, r_ref, o_ref, acc):
        k = pl.program_id(1)

        @pl.when(k == 0)
        def _():
            acc[...] = jnp.zeros_like(acc)
        acc[...] += _dot(a_ref[...].astype(BF16), b_ref[...])

        @pl.when(k == nk - 1)
        def _():
            o_ref[...] = r_ref[...] + scale * acc[...]

    return _call(
        host, body, name=name, grid=(t // tm, nk),
        in_specs=[a_spec, b_spec, _bs((tm, n), lambda i, k: (i, 0))],
        out_specs=_bs((tm, n), lambda i, k: (i, 0)),
        out_shape=S((t, n), F32),
        scratch_shapes=[pltpu.VMEM((tm, n), F32)],
        compiler_params=_cp("parallel", "arbitrary"),
        args=(a, b, res))


def ffn_down(a, wd, layer, h, name, host=None, scale=0.5):
    nk, t, f = a.shape
    d = h.shape[1]
    tm = _row_tile(t, FFN_ROWS)

    def body(a_ref, w_ref, r_ref, o_ref):
        acc = _dot(a_ref[0], w_ref[0])
        for k in range(1, nk):
            acc = acc + _dot(a_ref[k], w_ref[k])
        o_ref[...] = r_ref[...] + scale * acc

    row = _bs((tm, d), lambda i: (i, 0))
    return _call(
        host, body, name=name, grid=(t // tm,),
        in_specs=[_bs((nk, tm, f), lambda i: (0, i, 0)), _bs((nk, None, f, d), lambda i: (0, layer, 0, 0)), row],
        out_specs=row, out_shape=S((t, d), F32),
        compiler_params=_cp("parallel"),
        args=(a, wd, h))


def ffn_bwd_act(dh, wd, layer, gg, uu, name, host=None):
    nk, t, f = gg.shape
    d = dh.shape[1]
    tm = _row_tile(t, FFN_SHARDS_ROWS)

    def body(dh_ref, wd_ref, g_ref, u_ref, dg_ref, du_ref, dh16_ref):
        dh16 = dh_ref[...].astype(BF16)
        dh16_ref[...] = dh16
        for k in range(nk):
            da = 0.5 * _dot_nt(dh16, wd_ref[k])
            dg_ref[k] = (da * g_ref[k].astype(F32)).astype(BF16)
            du_ref[k] = (da * u_ref[k].astype(F32)).astype(BF16)

    aspec = _bs((nk, tm, f), lambda i: (0, i, 0))
    row = _bs((tm, d), lambda i: (i, 0))
    return _call(
        host, body, name=name, grid=(t // tm,),
        in_specs=[row, _bs((nk, None, f, d), lambda i: (0, layer, 0, 0)), aspec, aspec],
        out_specs=[aspec, aspec, row],
        out_shape=[S((nk, t, f), BF16), S((nk, t, f), BF16), S((t, d), BF16)],
        compiler_params=_cp("parallel"),
        args=(dh, wd, gg, uu))


def nt_acc_normbwd(terms, nk, h, gain, dh, name, host=None):
    t, d = h.shape
    tm = _row_tile(t, BWD_IN_ROWS)
    sub = _row_tile(tm, 256)
    nterm = len(terms)
    picks = [term[4] for term in terms]

    def body(*refs):
        xs = refs[:2 * nterm]
        h_ref, g_ref, dh_ref, o_ref, dg_ref, acc = refs[2 * nterm:]

        @pl.when(pl.program_id(0) == 0)
        def _():
            dg_ref[...] = jnp.zeros_like(dg_ref)
        tot = None
        for j in range(nterm):
            for k in range(nk):
                part = _dot_nt(picks[j](xs[2 * j], k), xs[2 * j + 1][k])
                tot = part if tot is None else tot + part
        acc[...] = tot

        def rows_of(cidx, dgain):
            rows = pl.ds(pl.multiple_of(cidx * sub, sub), sub)
            dx, dgc = _rmsnorm_bwd(h_ref[rows, :], g_ref[...], acc[rows, :])
            o_ref[rows, :] = dh_ref[rows, :] + dx
            return dgain + dgc
        dg_ref[...] += lax.fori_loop(0, tm // sub, rows_of, jnp.zeros((1, d), F32))

    in_specs, args = [], []
    for x, xs_, w, ws_, _ in terms:
        in_specs += [xs_, ws_]
        args += [x, w]
    row = _bs((tm, d), lambda i: (i, 0))
    vec = _bs((1, d), lambda i: (0, 0))
    return _call(
        host, body, name=name, grid=(t // tm,),
        in_specs=in_specs + [row, vec, row],
        out_specs=[row, vec],
        out_shape=[S((t, d), F32), S((1, d), F32)],
        scratch_shapes=[pltpu.VMEM((tm, d), F32)],
        compiler_params=_cp("arbitrary"),
        args=(*args, h, gain, dh))


def ffn_bwd_in(dg, du, wg, wu, layer, h, gain, dh, name, host=None):
    nk, t, f = dg.shape
    d = h.shape[1]
    tm = _row_tile(t, BWD_IN_ROWS)
    aspec = _bs((nk, tm, f), lambda i: (0, i, 0))
    wspec = _bs((nk, None, d, f), lambda i: (0, layer, 0, 0))
    pick = lambda x_ref, k: x_ref[k]
    return nt_acc_normbwd([(dg, aspec, wg, wspec, pick), (du, aspec, wu, wspec, pick)], nk, h, gain, dh, name, host)


def tn_mm(x, x_spec, y, y_spec, nblk, t, ka, nb, out_shape, out_spec, scale, prev, name, host=None):
    tk = _row_tile(t, TN_ROWS if F32 in (x.dtype, y.dtype) else 2 * TN_ROWS)

    def body(*refs):
        if prev is None:
            x_ref, y_ref, o_ref, acc = refs
        else:
            x_ref, y_ref, _, o_ref, acc = refs
        j = pl.program_id(1)

        @pl.when(j == 0)
        def _():
            acc[...] = jnp.zeros_like(acc)
        acc[...] += _dot_tn(x_ref[...].astype(BF16), y_ref[...].astype(BF16))

        @pl.when(j == t // tk - 1)
        def _():
            o_ref[...] = (scale * acc[...]).astype(o_ref.dtype)

    in_specs = [x_spec(tk), y_spec(tk)]
    args = [x, y]
    aliases = {}
    if prev is not None:
        in_specs.append(pl.BlockSpec(memory_space=pl.ANY))
        args.append(prev)
        aliases = {2: 0}
    return _call(
        host, body, name=name, grid=(nblk, t // tk),
        in_specs=in_specs, out_specs=out_spec, out_shape=out_shape,
        scratch_shapes=[pltpu.VMEM((ka, nb), F32)],
        aliases=aliases,
        compiler_params=_cp("parallel", "arbitrary"),
        args=tuple(args))


def ffn_wgrads(which, hn, dh, aa, dg, du, layer, grads, run):
    nk, t, f = aa.shape
    d = hn.shape[1]
    hn_spec = lambda tk: _bs((tk, d), lambda k, j: (j, 0))
    a_spec = lambda tk: _bs((None, tk, f), lambda k, j: (k, j, 0))
    shape_gu, spec_gu = S((nk, 2, d, f), BF16), _bs((None, None, d, f), lambda k, j: (k, layer, 0, 0))
    shape_d, spec_d = S((nk, 2, f, d), BF16), _bs((None, None, f, d), lambda k, j: (k, layer, 0, 0))
    for suffix, x, xs, y, ys, ka, nb, shp, spec, scale in (
            ("gate", hn, hn_spec, dg, a_spec, d, f, shape_gu, spec_gu, 1.0),
            ("up", hn, hn_spec, du, a_spec, d, f, shape_gu, spec_gu, 1.0),
            ("down", aa, a_spec, dh, hn_spec, f, d, shape_d, spec_d, 0.5)):
        key = f"{which}_w_{suffix}"
        grads[key] = run(tn_mm, x, xs, y, ys, nk, t, ka, nb, shp, spec, scale, grads.get(key), name=f"{which}_gw_{layer}_{suffix}")


def norm_mm(h, gain, w, name, host=None):
    t, d = h.shape
    nb, _, bw = w.shape
    tm = _row_tile(t, FFN_SHARDS_ROWS)

    def body(h_ref, g_ref, w_ref, hn_ref, z_ref):
        x = h_ref[...]
        hn = (x * _rstd(x) * g_ref[...]).astype(BF16)
        hn_ref[...] = hn
        for k in range(nb):
            z_ref[:, k * bw:(k + 1) * bw] = _dot(hn, w_ref[k])

    return _call(
        host, body, name=name, grid=(t // tm,),
        in_specs=[_bs((tm, d), lambda i: (i, 0)), _bs((1, d), lambda i: (0, 0)), _bs((nb, d, bw), lambda i: (0, 0, 0))],
        out_specs=[_bs((tm, d), lambda i: (i, 0)), _bs((tm, nb * bw), lambda i: (i, 0))],
        out_shape=[S((t, d), BF16), S((t, nb * bw), F32)],
        compiler_params=_cp("parallel"),
        args=(h, gain, w))


def nt_mm(a, w, name):
    t, k = a.shape
    n = w.shape[0]
    tm = _row_tile(t, 512)

    def body(a_ref, w_ref, o_ref):
        o_ref[...] = _dot_nt(a_ref[...].astype(BF16), w_ref[...])

    return pl.pallas_call(
        body, name=name, grid=(t // tm,),
        in_specs=[_bs((tm, k), lambda i: (i, 0)), _bs((n, k), lambda i: (0, 0))],
        out_specs=_bs((tm, n), lambda i: (i, 0)),
        out_shape=S((t, n), F32),
        compiler_params=_cp("parallel"),
    )(a, w)


def _head_mean_matrix():
    m = np.kron(np.eye(N_HEADS, dtype=np.float32), np.full((HEAD_DIM, HEAD_DIM), 1.0 / HEAD_DIM, np.float32))
    return jnp.asarray(m, BF16)


def _head_sum_matrix():
    m = np.kron(np.eye(N_HEADS, dtype=np.float32), np.ones((HEAD_DIM, HEAD_DIM), np.float32))
    return jnp.asarray(m, BF16)


def _rel_bucket_np(dist):
    max_exact = REL_BUCKETS // 2
    n = np.maximum(dist, 1).astype(np.float32)
    large = max_exact + (np.log(n / np.float32(max_exact)) / np.float32(math.log(REL_MAX_DIST / max_exact))
                         * np.float32(REL_BUCKETS - max_exact)).astype(np.int32)
    large = np.minimum(large, REL_BUCKETS - 1)
    return np.where(dist < max_exact, dist, large)


def _band_tables():
    qi = np.arange(BAND)[:, None]
    kj = np.arange(2 * BAND)[None, :]
    dist_q = qi + BAND - kj
    qq = np.arange(2 * BAND)[:, None]
    kk = np.arange(BAND)[None, :]
    dist_k = qq - kk
    out = []
    for dist in (dist_q, dist_k):
        valid = (dist >= 0) & (dist <= BAND)
        bucket = np.stack([_rel_bucket_np(np.clip(dist, 0, BAND) * d) for d in DILATIONS])
        out.append((bucket, valid))
    return out


def band_bias(rel_bias):
    out = []
    for bucket, valid in _band_tables():
        bucket = np.where(valid[None], bucket, -1)[:, None]
        bucket_dev = lax.optimization_barrier(jnp.asarray(bucket, jnp.int32))
        tab = jnp.full((len(DILATIONS), N_HEADS) + bucket.shape[2:], NEG, F32)
        for b in range(REL_BUCKETS):
            if (bucket == b).any():
                tab = jnp.where(bucket_dev == b, rel_bias[b][None, :, None, None], tab)
        out.append(tab.reshape(len(DILATIONS), N_HEADS // 2, 2 * tab.shape[2], tab.shape[3]))
    return out


LANE_TILE = 128
N_LANE_TILES = ATTN_W // LANE_TILE


def _view_shape(t, dil):
    return (t // dil, dil * ATTN_W)


def _view_spec(tm, dil):
    return _bs((tm // dil, dil * ATTN_W), lambda i: (i, 0))


def _cols_to(scr, val):
    for cc in range(N_LANE_TILES):
        scr[cc] = val[:, LANE_TILE * cc:LANE_TILE * (cc + 1)]


def _cols_from(scr):
    return jnp.concatenate([scr[cc] for cc in range(N_LANE_TILES)], axis=1)


def _write_view(scr, out_ref, dil):
    if dil == 1:
        out_ref[...] = _cols_from(scr).astype(out_ref.dtype)
        return
    rows = scr.shape[1] // dil
    for r in range(dil):
        for cc in range(N_LANE_TILES):
            c0 = r * ATTN_W + LANE_TILE * cc
            out_ref[:, c0:c0 + LANE_TILE] = scr[cc, pl.ds(r, rows, stride=dil), :].astype(out_ref.dtype)


def _read_view(scr, in_ref, dil):
    if dil == 1:
        return in_ref[...].astype(F32)
    rows = scr.shape[1] // dil
    for r in range(dil):
        for cc in range(N_LANE_TILES):
            c0 = r * ATTN_W + LANE_TILE * cc
            scr[cc, pl.ds(r, rows, stride=dil), :] = in_ref[:, c0:c0 + LANE_TILE].astype(F32)
    return _cols_from(scr)


def hyb_prep(z, q_gain, k_gain, name):
    t = z.shape[0]
    tm = _row_tile(t, 512)
    seg = _head_mean_matrix()
    nd = len(DILATIONS)

    def body(q_ref, k_ref, v_ref, qg_ref, kg_ref, seg_ref, *rest):
        outs, scr = rest[:3 * nd], rest[3 * nd]
        q = q_ref[...]
        k = k_ref[...]
        vals = (q * lax.rsqrt(_seg_dot(q * q, seg_ref[...]) + EPS) * qg_ref[...],
                k * lax.rsqrt(_seg_dot(k * k, seg_ref[...]) + EPS) * kg_ref[...],
                v_ref[...])
        for j, val in enumerate(vals):
            _cols_to(scr, val)
            for g, dil in enumerate(DILATIONS):
                _write_view(scr, outs[3 * g + j], dil)

    col = lambda c: _bs((tm, ATTN_W), lambda i: (i, c))
    vec = _bs((1, ATTN_W), lambda i: (0, 0))
    res = pl.pallas_call(
        body, name=name, grid=(t // tm,),
        in_specs=[col(3), col(4), col(5), vec, vec, _bs((ATTN_W, ATTN_W), lambda i: (0, 0))],
        out_specs=[_view_spec(tm, dil) for dil in DILATIONS for _ in range(3)],
        out_shape=[S(_view_shape(t, dil), BF16) for dil in DILATIONS for _ in range(3)],
        scratch_shapes=[pltpu.VMEM((N_LANE_TILES, tm, LANE_TILE), F32)],
        compiler_params=_cp("parallel"),
    )(z, z, z, q_gain, k_gain, seg)
    return {dil: tuple(res[3 * g:3 * g + 3]) for g, dil in enumerate(DILATIONS)}


def _lane_lo(shape):
    return lax.broadcasted_iota(jnp.int32, shape, 1) < HEAD_DIM


def _stack_heads(pair):
    lo = _lane_lo(pair.shape)
    zero = jnp.zeros_like(pair)
    return jnp.concatenate([jnp.where(lo, pair, zero), jnp.where(lo, zero, pair)], axis=0)


def _unstack_heads(st):
    rows = st.shape[0] // 2
    return jnp.where(_lane_lo((rows, st.shape[1])), st[:rows], st[rows:])


def attn_fwd(q, k, v, bias, dil, name, host=None):
    qv, kv, vv = q, k, v
    sub = q.shape[0]
    nb = sub // BAND

    def body(q_ref, kp_ref, kc_ref, vp_ref, vc_ref, b_ref, o_ref, l_ref):
        first = pl.program_id(1) == 0
        colk = lax.broadcasted_iota(jnp.int32, (2 * BAND, 2 * BAND), 1)
        for j in range(N_HEADS // 2):
            sl = slice(2 * HEAD_DIM * j, 2 * HEAD_DIM * (j + 1))
            kk = jnp.concatenate([kp_ref[:, sl], kc_ref[:, sl]], axis=0)
            vv_ = jnp.concatenate([vp_ref[:, sl], vc_ref[:, sl]], axis=0)
            s = _dot_nt(_stack_heads(q_ref[:, sl]), kk) * (HEAD_DIM ** -0.5) + b_ref[j]
            s = jnp.where(jnp.logical_and(first, colk < BAND), NEG, s)
            m = jnp.max(s, axis=-1, keepdims=True)
            p = jnp.exp(s - m)
            l = jnp.sum(p, axis=-1, keepdims=True)
            o_ref[:, sl] = _unstack_heads(_dot(p.astype(BF16), vv_) / l).astype(o_ref.dtype)
            l_ref[:, sl] = _unstack_heads(jnp.broadcast_to(m + jnp.log(l), (2 * BAND, 2 * HEAD_DIM)))

    cur = _bs((BAND, ATTN_W), lambda r, n: (n, r))
    prv = _bs((BAND, ATTN_W), lambda r, n: (jnp.maximum(n - 1, 0), r))
    return _call(
        host, body, name=name, grid=(dil, nb),
        in_specs=[cur, prv, cur, prv, cur, _bs((N_HEADS // 2, 2 * BAND, 2 * BAND), lambda r, n: (0, 0, 0))],
        out_specs=[cur, cur],
        out_shape=[S((sub, dil * ATTN_W), BF16), S((sub, dil * ATTN_W), F32)],
        compiler_params=_cp("parallel", "arbitrary"),
        args=(qv, kv, kv, vv, vv, bias))


def hyb_post(z, conv_w, os_, lses, name, host=None):
    t = z.shape[0]
    tm = _row_tile(t, 512)
    nd = len(DILATIONS)

    def body(gb_ref, gc_ref, cx_ref, gch_ref, cxh_ref, w_ref, *rest):
        o_refs, l_refs = rest[:nd], rest[nd:2 * nd]
        y_ref, ya_ref = rest[2 * nd:2 * nd + 2]
        lt_refs, scr = rest[2 * nd + 2:3 * nd + 2], rest[3 * nd + 2]
        i = pl.program_id(0)
        m = gc_ref[...] * cx_ref[...]
        mh = jnp.where(i == 0, 0.0, gch_ref[...] * cxh_ref[...])
        conv = w_ref[0:1, :] * _shift_down(m, mh, 2) + w_ref[1:2, :] * _shift_down(m, mh, 1) + w_ref[2:3, :] * m
        y_ref[0] = (gb_ref[...] * conv).astype(BF16)
        ls = [_read_view(scr, l_refs[g], dil) for g, dil in enumerate(DILATIONS)]
        mx = functools.reduce(jnp.maximum, ls)
        es = [jnp.exp(l - mx) for l in ls]
        den = functools.reduce(lambda a, b: a + b, es)
        num = es[0] * _read_view(scr, o_refs[0], DILATIONS[0])
        for g in range(1, nd):
            num = num + es[g] * _read_view(scr, o_refs[g], DILATIONS[g])
        ya = num / den
        y_ref[1] = ya.astype(BF16)
        ya_ref[...] = ya
        _cols_to(scr, mx + jnp.log(den))
        for g, dil in enumerate(DILATIONS):
            _write_view(scr, lt_refs[g], dil)

    hb = tm // 8
    col = lambda c: _bs((tm, CONV_W), lambda i: (i, c))
    halo = lambda c: _bs((8, CONV_W), lambda i: (jnp.maximum(i * hb - 1, 0), c))
    row = _bs((tm, ATTN_W), lambda i: (i, 0))
    views = [_view_spec(tm, dil) for dil in DILATIONS]
    res = _call(
        host, body, name=name, grid=(t // tm,),
        in_specs=[col(0), col(1), col(2), halo(1), halo(2), _bs((3, CONV_W), lambda i: (0, 0))] + views * 2,
        out_specs=[_bs((2, tm, ATTN_W), lambda i: (0, i, 0)), row] + views,
        out_shape=[S((2, t, ATTN_W), BF16), S((t, ATTN_W), F32)] + [S(_view_shape(t, dil), F32) for dil in DILATIONS],
        scratch_shapes=[pltpu.VMEM((N_LANE_TILES, tm, LANE_TILE), F32)],
        compiler_params=_cp("parallel"),
        args=(z, z, z, z, z, conv_w, *os_, *lses))
    return res[0], res[1], dict(zip(DILATIONS, res[2:]))


def attn_delta(dy, ya, name):
    t = ya.shape[0]
    tm = _row_tile(t, 512)
    seg = _head_sum_matrix()
    nd = len(DILATIONS)

    def body(dy_ref, ya_ref, seg_ref, *rest):
        dl_refs, db_refs, scr = rest[:nd], rest[nd:2 * nd], rest[2 * nd]
        dya = dy_ref[...]
        _cols_to(scr, _seg_dot(dya * ya_ref[...], seg_ref[...]))
        for g, dil in enumerate(DILATIONS):
            _write_view(scr, dl_refs[g], dil)
        _cols_to(scr, dya)
        for g, dil in enumerate(DILATIONS):
            _write_view(scr, db_refs[g], dil)

    row = _bs((tm, ATTN_W), lambda i: (i, 0))
    views = [_view_spec(tm, dil) for dil in DILATIONS]
    res = pl.pallas_call(
        body, name=name, grid=(t // tm,),
        in_specs=[_bs((tm, ATTN_W), lambda i: (i, 1)), row, _bs((ATTN_W, ATTN_W), lambda i: (0, 0))],
        out_specs=views * 2,
        out_shape=[S(_view_shape(t, dil), F32) for dil in DILATIONS] + [S(_view_shape(t, dil), BF16) for dil in DILATIONS],
        scratch_shapes=[pltpu.VMEM((N_LANE_TILES, tm, LANE_TILE), F32)],
        compiler_params=_cp("parallel"),
    )(dy, ya, seg)
    return dict(zip(DILATIONS, res[:nd])), dict(zip(DILATIONS, res[nd:]))


def attn_bwd_dq(q, k, v, dya, lt, delta, bias, dil, name, host=None):
    qv, kv, vv, dv_, lv, ev = q, k, v, dya, lt, delta
    sub = q.shape[0]
    nb = sub // BAND

    def body(q_ref, kp_ref, kc_ref, vp_ref, vc_ref, do_ref, l_ref, e_ref, b_ref, dq_ref, db_ref):
        r, n = pl.program_id(0), pl.program_id(1)

        @pl.when(jnp.logical_and(r == 0, n == 0))
        def _():
            db_ref[...] = jnp.zeros_like(db_ref)
        first = n == 0
        colk = lax.broadcasted_iota(jnp.int32, (2 * BAND, 2 * BAND), 1)
        for j in range(N_HEADS // 2):
            c0 = 2 * HEAD_DIM * j
            sl = slice(c0, c0 + 2 * HEAD_DIM)
            kk = jnp.concatenate([kp_ref[:, sl], kc_ref[:, sl]], axis=0)
            vv_ = jnp.concatenate([vp_ref[:, sl], vc_ref[:, sl]], axis=0)
            lse = jnp.concatenate([l_ref[:, c0:c0 + 1], l_ref[:, c0 + HEAD_DIM:c0 + HEAD_DIM + 1]], axis=0)
            dlt = jnp.concatenate([e_ref[:, c0:c0 + 1], e_ref[:, c0 + HEAD_DIM:c0 + HEAD_DIM + 1]], axis=0)
            s = _dot_nt(_stack_heads(q_ref[:, sl]), kk) * (HEAD_DIM ** -0.5) + b_ref[j]
            s = jnp.where(jnp.logical_and(first, colk < BAND), NEG, s)
            p = jnp.exp(s - lse)
            ds = p * (_dot_nt(_stack_heads(do_ref[:, sl]), vv_) - dlt)
            db_ref[j] += ds
            dq_ref[:, sl] = (_unstack_heads(_dot(ds.astype(BF16), kk)) * (HEAD_DIM ** -0.5)).astype(dq_ref.dtype)

    cur = _bs((BAND, ATTN_W), lambda r, n: (n, r))
    prv = _bs((BAND, ATTN_W), lambda r, n: (jnp.maximum(n - 1, 0), r))
    tab = _bs((N_HEADS // 2, 2 * BAND, 2 * BAND), lambda r, n: (0, 0, 0))
    dq, db = _call(
        host, body, name=name, grid=(dil, nb),
        in_specs=[cur, prv, cur, prv, cur, cur, cur, cur, tab],
        out_specs=[cur, tab],
        out_shape=[S((sub, dil * ATTN_W), BF16), S((N_HEADS // 2, 2 * BAND, 2 * BAND), F32)],
        compiler_params=_cp("arbitrary", "arbitrary"),
        args=(qv, kv, kv, vv, vv, dv_, lv, ev, bias))
    return dq, db.reshape(N_HEADS, BAND, 2 * BAND)


def attn_bwd_dkv(q, k, v, dya, lt, delta, bias_k, dil, name, host=None):
    qv, kv, vv, dv_, lv, ev = q, k, v, dya, lt, delta
    sub = q.shape[0]
    nb = sub // BAND

    def body(k_ref, v_ref, qc_ref, qn_ref, dc_ref, dn_ref, lc_ref, ln_ref, ec_ref, en_ref, b_ref, dk_ref, dv_ref):
        last = pl.program_id(1) == nb - 1
        rowq = lax.broadcasted_iota(jnp.int32, (4 * BAND, BAND), 0)
        from_next = (rowq & BAND) != 0
        for j in range(N_HEADS // 2):
            c0 = 2 * HEAD_DIM * j
            sl = slice(c0, c0 + 2 * HEAD_DIM)
            kp, vp = k_ref[:, sl], v_ref[:, sl]
            q4 = _stack_heads(jnp.concatenate([qc_ref[:, sl], qn_ref[:, sl]], axis=0))
            do4 = _stack_heads(jnp.concatenate([dc_ref[:, sl], dn_ref[:, sl]], axis=0))
            lse = jnp.concatenate([ref[:, c:c + 1] for c in (c0, c0 + HEAD_DIM) for ref in (lc_ref, ln_ref)], axis=0)
            dlt = jnp.concatenate([ref[:, c:c + 1] for c in (c0, c0 + HEAD_DIM) for ref in (ec_ref, en_ref)], axis=0)
            s = _dot_nt(q4, kp) * (HEAD_DIM ** -0.5) + b_ref[j]
            s = jnp.where(jnp.logical_and(last, from_next), NEG, s)
            p = jnp.exp(s - lse)
            ds = p * (_dot_nt(do4, vp) - dlt)
            dv_ref[:, sl] = _dot_tn(p.astype(BF16), do4).astype(dv_ref.dtype)
            dk_ref[:, sl] = (_dot_tn(ds.astype(BF16), q4) * (HEAD_DIM ** -0.5)).astype(dk_ref.dtype)

    cur = _bs((BAND, ATTN_W), lambda r, n: (n, r))
    nxt = _bs((BAND, ATTN_W), lambda r, n: (jnp.minimum(n + 1, nb - 1), r))
    tab = _bs((N_HEADS // 2, 4 * BAND, BAND), lambda r, n: (0, 0, 0))
    return _call(
        host, body, name=name, grid=(dil, nb),
        in_specs=[cur, cur, cur, nxt, cur, nxt, cur, nxt, cur, nxt, tab],
        out_specs=[cur, cur],
        out_shape=[S((sub, dil * ATTN_W), BF16)] * 2,
        compiler_params=_cp("parallel", "arbitrary"),
        args=(kv, vv, qv, qv, dv_, dv_, lv, lv, ev, ev, bias_k))


def hyb_dz(z, dy, conv_w, q_gain, k_gain, dqs, dks, dvs, name):
    t = z.shape[0]
    tm = _row_tile(t, 512)
    nt = t // tm
    seg = _head_mean_matrix()

    def body(gb_ref, gc_ref, cx_ref, q_ref, k_ref, gch_ref, cxh_ref, gbn_ref, dyc_ref, dyn_ref, w_ref, qg_ref, kg_ref, seg_ref,
             dq1, dq2, dq3, dk1, dk2, dk3, dv1, dv2, dv3, dz_ref, dw_ref, dqg_ref, dkg_ref, scr):
        i = pl.program_id(0)

        def total(parts):
            acc = _read_view(scr, parts[0], DILATIONS[0])
            for g in range(1, len(DILATIONS)):
                acc = acc + _read_view(scr, parts[g], DILATIONS[g])
            return acc

        @pl.when(i == 0)
        def _():
            dw_ref[...] = jnp.zeros_like(dw_ref)
            dqg_ref[...] = jnp.zeros_like(dqg_ref)
            dkg_ref[...] = jnp.zeros_like(dkg_ref)
        gb, gc, cx, dyc = gb_ref[...], gc_ref[...], cx_ref[...], dyc_ref[...]
        m = gc * cx
        mh = jnp.where(i == 0, 0.0, gch_ref[...] * cxh_ref[...])
        m1, m2 = _shift_down(m, mh, 1), _shift_down(m, mh, 2)
        conv = w_ref[0:1, :] * m2 + w_ref[1:2, :] * m1 + w_ref[2:3, :] * m
        dconv = dyc * gb
        dcn = jnp.where(i == nt - 1, 0.0, dyn_ref[...] * gbn_ref[...])
        dm = w_ref[2:3, :] * dconv + w_ref[1:2, :] * _shift_up(dconv, dcn, 1) + w_ref[0:1, :] * _shift_up(dconv, dcn, 2)
        dz_ref[:, 0:CONV_W] = (dyc * conv).astype(BF16)
        dz_ref[:, CONV_W:2 * CONV_W] = (dm * cx).astype(BF16)
        dz_ref[:, 2 * CONV_W:3 * CONV_W] = (dm * gc).astype(BF16)
        dw_ref[0:1, :] += jnp.sum(dconv * m2, axis=0, keepdims=True)
        dw_ref[1:2, :] += jnp.sum(dconv * m1, axis=0, keepdims=True)
        dw_ref[2:3, :] += jnp.sum(dconv * m, axis=0, keepdims=True)
        base = 3 * CONV_W
        for idx, (x_ref, g_ref, parts, dgain_ref) in enumerate(((q_ref, qg_ref, (dq1, dq2, dq3), dqg_ref),
                                                                  (k_ref, kg_ref, (dk1, dk2, dk3), dkg_ref))):
            x = x_ref[...]
            dxh = total(parts)
            r = lax.rsqrt(_seg_dot(x * x, seg_ref[...]) + EPS)
            xhat = x * r
            tt = dxh * g_ref[...]
            dx = r * (tt - xhat * _seg_dot(tt * xhat, seg_ref[...]))
            dz_ref[:, base + idx * ATTN_W:base + (idx + 1) * ATTN_W] = dx.astype(BF16)
            dgain_ref[...] += jnp.sum(dxh * xhat, axis=0, keepdims=True)
        dz_ref[:, base + 2 * ATTN_W:base + 3 * ATTN_W] = total((dv1, dv2, dv3)).astype(BF16)

    hb = tm // 8
    col = lambda c: _bs((tm, CONV_W), lambda i: (i, c))
    prev = lambda c: _bs((8, CONV_W), lambda i: (jnp.maximum(i * hb - 1, 0), c))
    nxt = lambda c: _bs((8, CONV_W), lambda i: (jnp.minimum((i + 1) * hb, t // 8 - 1), c))
    row = _bs((tm, ATTN_W), lambda i: (i, 0))
    vec = _bs((1, ATTN_W), lambda i: (0, 0))
    return pl.pallas_call(
        body, name=name, grid=(nt,),
        in_specs=[col(0), col(1), col(2), col(3), col(4), prev(1), prev(2), nxt(0), col(0), nxt(0),
                  _bs((3, CONV_W), lambda i: (0, 0)), vec, vec, _bs((ATTN_W, ATTN_W), lambda i: (0, 0))]
                 + [_view_spec(tm, dil) for dil in DILATIONS] * 3,
        out_specs=[_bs((tm, 6 * CONV_W), lambda i: (i, 0)), _bs((3, CONV_W), lambda i: (0, 0)), vec, vec],
        out_shape=[S((t, 6 * CONV_W), BF16), S((3, CONV_W), F32), S((1, ATTN_W), F32), S((1, ATTN_W), F32)],
        scratch_shapes=[pltpu.VMEM((N_LANE_TILES, tm, LANE_TILE), F32)],
        compiler_params=_cp("arbitrary"),
    )(z, z, z, z, z, z, z, z, dy, dy, conv_w, q_gain, k_gain, seg, *dqs, *dks, *dvs)


def rel_bias_grad(dbs, name):
    (bq, vq), _ = _band_tables()
    onehot = np.zeros((len(DILATIONS), REL_BUCKETS, BAND * 2 * BAND), np.float32)
    for g in range(len(DILATIONS)):
        idx = bq[g].reshape(-1)
        ok = vq.reshape(-1)
        onehot[g, idx[ok], np.nonzero(ok)[0]] = 1.0
    onehot = jnp.asarray(onehot, BF16)
    flat = [d.reshape(N_HEADS, BAND * 2 * BAND) for d in dbs]

    def body(oh_ref, d1, d2, d3, o_ref):
        acc = jnp.zeros((REL_BUCKETS, N_HEADS), F32)
        for g, d in enumerate((d1, d2, d3)):
            x = d[...]
            hi = x.astype(BF16)
            lo = (x - hi.astype(F32)).astype(BF16)
            acc += _dot_nt(oh_ref[g], hi) + _dot_nt(oh_ref[g], lo)
        o_ref[...] = acc

    full = lambda shp: _bs(shp, lambda: tuple(0 for _ in shp))
    return pl.pallas_call(
        body, name=name,
        in_specs=[full(onehot.shape)] + [full(flat[0].shape)] * 3,
        out_specs=full((REL_BUCKETS, N_HEADS)),
        out_shape=S((REL_BUCKETS, N_HEADS), F32),
        compiler_params=pltpu.CompilerParams(vmem_limit_bytes=VMEM_LIMIT),
    )(onehot, *flat)


def _lru_gates(xb, wa_ref, wx_ref, ba, bx):
    xb16 = xb.astype(BF16)
    ga = jnp.concatenate([_dot(xb16[:, LRU_BLOCK * g:LRU_BLOCK * (g + 1)], wa_ref[g]) for g in range(LRU_BLOCKS)], axis=1) + ba
    gx = jnp.concatenate([_dot(xb16[:, LRU_BLOCK * g:LRU_BLOCK * (g + 1)], wx_ref[g]) for g in range(LRU_BLOCKS)], axis=1) + bx
    return ga, gx


def _lru_coeffs(ga, gx, lam):
    sga = _sigmoid(ga)
    sp = _softplus(-lam)
    log_a = -LRU_C * sga * sp
    a = jnp.exp(log_a)
    one_m_a2 = _neg_expm1(2.0 * log_a)
    return sga, sp, a, one_m_a2, jnp.sqrt(one_m_a2), _sigmoid(gx)


def rec_fwd(z, conv_w, conv_b, wa, wx, ba, bx, lam, name, host=None):
    t = z.shape[0]
    w = z.shape[1] // 2
    tm = _row_tile(t, 256)

    def body(xp_ref, xh_ref, yb_ref, cw_ref, cb_ref, wa_ref, wx_ref, ba_ref, bx_ref, lam_ref,
             xb_ref, ga_ref, gx_ref, hs_ref, out_ref, carry):
        i = pl.program_id(0)

        @pl.when(i == 0)
        def _():
            carry[...] = jnp.zeros_like(carry)
        xp = xp_ref[...]
        xh = jnp.where(i == 0, 0.0, xh_ref[...])
        xb = cb_ref[...] + cw_ref[3:4, :] * xp
        for j in range(3):
            xb = xb + cw_ref[j:j + 1, :] * _shift_down(xp, xh, 3 - j)
        ga, gx = _lru_gates(xb, wa_ref, wx_ref, ba_ref[...], bx_ref[...])
        _, _, a, _, sq, sgx = _lru_coeffs(ga, gx, lam_ref[...])
        aa, bb = a, sq * sgx * xb
        s = 1
        while s < tm:
            bb = aa * _roll_fill(bb, s, 0.0, False) + bb
            aa = aa * _roll_fill(aa, s, 1.0, False)
            s *= 2
        hs = aa * carry[0:1, :] + bb
        xb_ref[...] = xb
        ga_ref[...] = ga
        gx_ref[...] = gx
        hs_ref[...] = hs
        carry[0:1, :] = hs_ref[tm - 1:tm, :]
        gy, _ = _gelu_and_grad(yb_ref[...])
        out_ref[...] = (hs * gy).astype(BF16)

    hb = tm // 8
    row = _bs((tm, w), lambda i: (i, 0))
    vec = _bs((1, w), lambda i: (0, 0))
    wsp = _bs((LRU_BLOCKS, LRU_BLOCK, LRU_BLOCK), lambda i: (0, 0, 0))
    return _call(
        host, body, name=name, grid=(t // tm,),
        in_specs=[row, _bs((8, w), lambda i: (jnp.maximum(i * hb - 1, 0), 0)), _bs((tm, w), lambda i: (i, 1)),
                  _bs((4, w), lambda i: (0, 0)), vec, wsp, wsp, vec, vec, vec],
        out_specs=[row] * 5,
        out_shape=[S((t, w), F32)] * 4 + [S((t, w), BF16)],
        scratch_shapes=[pltpu.VMEM((8, w), F32)],
        compiler_params=_cp("arbitrary"),
        args=(z, z, z, conv_w, conv_b, wa, wx, ba, bx, lam))


def rec_bwd(d_out, z, xb, ga, gx, hs, conv_w, wa, wx, lam, name, host=None):
    t = z.shape[0]
    w = z.shape[1] // 2
    tm = _row_tile(t, 256)
    nt = t // tm

    def body(do_ref, xp_ref, xph_ref, yb_ref, xb_ref, ga_ref, gx_ref, hs_ref, hsh_ref, cw_ref, wa_ref, wx_ref, lam_ref,
             dz_ref, dga_ref, dgx_ref, sm_ref, c_lam, c_a, c_dxb):
        i = pl.program_id(0)

        @pl.when(i == 0)
        def _():
            sm_ref[...] = jnp.zeros_like(sm_ref)
            c_lam[...] = jnp.zeros_like(c_lam)
            c_a[...] = jnp.zeros_like(c_a)
            c_dxb[...] = jnp.zeros_like(c_dxb)
        d_o, yb, xb, hs = do_ref[...], yb_ref[...], xb_ref[...], hs_ref[...]
        lam = lam_ref[...]
        gy, dgy = _gelu_and_grad(yb)
        dz_ref[:, w:2 * w] = (d_o * hs * dgy).astype(BF16)
        sga, sp, a, one_m_a2, sq, sgx = _lru_coeffs(ga_ref[...], gx_ref[...], lam)
        aa = _shift_up(a, c_a[...], 1)
        bb = d_o * gy
        s = 1
        while s < tm:
            bb = aa * _roll_fill(bb, s, 0.0, True) + bb
            aa = aa * _roll_fill(aa, s, 1.0, True)
            s *= 2
        lmb = aa * c_lam[0:1, :] + bb
        c_a[...] = a[0:8, :]
        c_lam[...] = lmb[0:8, :]
        hprev = _shift_down(hs, jnp.where(i == nt - 1, 0.0, hsh_ref[...]), 1)
        d_sq = lmb * sgx * xb
        d_sgx = lmb * sq * xb
        d_log_a = lmb * hprev * a - d_sq * (1.0 - one_m_a2) / sq
        dga = d_log_a * (-LRU_C * sp) * sga * (1.0 - sga)
        dgx = d_sgx * sgx * (1.0 - sgx)
        dga16, dgx16 = dga.astype(BF16), dgx.astype(BF16)
        dga_ref[...] = dga16
        dgx_ref[...] = dgx16
        dxb = lmb * sq * sgx + jnp.concatenate(
            [_dot_nt(dga16[:, LRU_BLOCK * g:LRU_BLOCK * (g + 1)], wa_ref[g]) + _dot_nt(dgx16[:, LRU_BLOCK * g:LRU_BLOCK * (g + 1)], wx_ref[g])
             for g in range(LRU_BLOCKS)], axis=1)
        nxt = c_dxb[...]
        dxp = cw_ref[3:4, :] * dxb
        for j in range(3):
            dxp = dxp + cw_ref[j:j + 1, :] * _shift_up(dxb, nxt, 3 - j)
        c_dxb[...] = dxb[0:8, :]
        dz_ref[:, 0:w] = dxp.astype(BF16)
        xp = xp_ref[...]
        xph = jnp.where(i == nt - 1, 0.0, xph_ref[...])
        sm_ref[0:1, :] += jnp.sum(dga, axis=0, keepdims=True)
        sm_ref[1:2, :] += jnp.sum(dgx, axis=0, keepdims=True)
        sm_ref[2:3, :] += jnp.sum(d_log_a * (-LRU_C * sga), axis=0, keepdims=True) * (-_sigmoid(-lam))
        sm_ref[3:4, :] += jnp.sum(dxb, axis=0, keepdims=True)
        for j in range(4):
            sm_ref[4 + j:5 + j, :] += jnp.sum(dxb * _shift_down(xp, xph, 3 - j), axis=0, keepdims=True)

    hb = tm // 8
    rev = lambda c: _bs((tm, w), lambda i: (nt - 1 - i, c))
    halo = lambda c: _bs((8, w), lambda i: (jnp.maximum((nt - 1 - i) * hb - 1, 0), c))
    vec = _bs((1, w), lambda i: (0, 0))
    wsp = _bs((LRU_BLOCKS, LRU_BLOCK, LRU_BLOCK), lambda i: (0, 0, 0))
    return _call(
        host, body, name=name, grid=(nt,),
        in_specs=[rev(0), rev(0), halo(0), rev(1), rev(0), rev(0), rev(0), rev(0), halo(0),
                  _bs((4, w), lambda i: (0, 0)), wsp, wsp, vec],
        out_specs=[_bs((tm, 2 * w), lambda i: (nt - 1 - i, 0)), rev(0), rev(0), _bs((8, w), lambda i: (0, 0))],
        out_shape=[S((t, 2 * w), BF16), S((t, w), BF16), S((t, w), BF16), S((8, w), F32)],
        scratch_shapes=[pltpu.VMEM((8, w), F32)] * 3,
        compiler_params=_cp("arbitrary"),
        args=(d_out, z, z, z, xb, ga, gx, hs, hs, conv_w, wa, wx, lam))


def ple_fwd(h, gain, wpg, layer, p, p_layer, wpp, name, target=None):
    t, d = h.shape
    pd = p.shape[2]
    nk, _, rb, _ = wpg.shape
    cb = wpp.shape[3]
    tm = _row_tile(t, 512)
    row = _bs((tm, d), lambda i: (i, 0))
    in_specs = [row, _bs((1, d), lambda i: (0, 0)), _bs((nk, None, rb, d), lambda i: (0, layer, 0, 0)),
                _bs((None, tm, pd), lambda i: (p_layer, i, 0)), _bs((nk, None, pd, cb), lambda i: (0, layer, 0, 0))]

    def forward(h_ref, g_ref, wg_ref, p_ref, wp_ref, hn_ref, gp_ref, pp_ref):
        x = h_ref[...]
        hn = (x * _rstd(x) * g_ref[...]).astype(BF16)
        gp = _dot(hn[:, 0:rb], wg_ref[0])
        for k in range(1, nk):
            gp = gp + _dot(hn[:, rb * k:rb * (k + 1)], wg_ref[k])
        pp = jnp.concatenate([_dot(p_ref[...].astype(BF16), wp_ref[k]) for k in range(nk)], axis=1)
        hn_ref[...] = hn
        gp_ref[...] = gp
        pp_ref[...] = pp
        return x + _sigmoid(gp) * pp

    if target is not None:
        def body_loss(h_ref, g_ref, wg_ref, p_ref, wp_ref, t_ref, l_ref, dy_ref, hn_ref, gp_ref, pp_ref):
            @pl.when(pl.program_id(0) == 0)
            def _():
                l_ref[...] = jnp.zeros_like(l_ref)
            err = forward(h_ref, g_ref, wg_ref, p_ref, wp_ref, hn_ref, gp_ref, pp_ref) - t_ref[...]
            dy_ref[...] = err * (1.0 / d)
            l_ref[...] += jnp.sum(jnp.sum(err * err, axis=1, keepdims=True), axis=0, keepdims=True) * (0.5 / d)

        return pl.pallas_call(
            body_loss, name=name, grid=(t // tm,),
            in_specs=in_specs + [row],
            out_specs=[_bs((1, 1), lambda i: (0, 0))] + [row] * 4,
            out_shape=[S((1, 1), F32), S((t, d), F32), S((t, d), BF16), S((t, d), F32), S((t, d), F32)],
            compiler_params=_cp("arbitrary"),
        )(h, gain, wpg, p, wpp, target)

    def body(h_ref, g_ref, wg_ref, p_ref, wp_ref, o_ref, hn_ref, gp_ref, pp_ref):
        o_ref[...] = forward(h_ref, g_ref, wg_ref, p_ref, wp_ref, hn_ref, gp_ref, pp_ref)

    return pl.pallas_call(
        body, name=name, grid=(t // tm,),
        in_specs=in_specs,
        out_specs=[row] * 4,
        out_shape=[S((t, d), F32), S((t, d), BF16), S((t, d), F32), S((t, d), F32)],
        compiler_params=_cp("parallel"),
    )(h, gain, wpg, p, wpp)


def ple_bwd(dh, h, gain, wpg, layer, gp, pp, name, host=None):
    t, d = h.shape
    nk, _, rb, _ = wpg.shape
    tm = _row_tile(t, 512)

    def body(dh_ref, h_ref, g_ref, wg_ref, gp_ref, pp_ref, o_ref, dgp_ref, dpp_ref, dg_ref):
        @pl.when(pl.program_id(0) == 0)
        def _():
            dg_ref[...] = jnp.zeros_like(dg_ref)
        d_h = dh_ref[...]
        gate = _sigmoid(gp_ref[...])
        dgp = (d_h * pp_ref[...] * gate * (1.0 - gate)).astype(BF16)
        dgp_ref[...] = dgp
        dpp_ref[...] = (d_h * gate).astype(BF16)
        dhn = jnp.concatenate([_dot_nt(dgp, wg_ref[k]) for k in range(nk)], axis=1)
        dx, dgain = _rmsnorm_bwd(h_ref[...], g_ref[...], dhn)
        o_ref[...] = d_h + dx
        dg_ref[...] += dgain

    row = _bs((tm, d), lambda i: (i, 0))
    vec = _bs((1, d), lambda i: (0, 0))
    return _call(
        host, body, name=name, grid=(t // tm,),
        in_specs=[row, row, vec, _bs((nk, None, rb, d), lambda i: (0, layer, 0, 0)), row, row],
        out_specs=[row, row, row, vec],
        out_shape=[S((t, d), F32), S((t, d), BF16), S((t, d), BF16), S((1, d), F32)],
        compiler_params=_cp("arbitrary"),
        args=(dh, h, gain, wpg, gp, pp))


def _vec(a, i):
    return a[i:i + 1]


def local_step(x, p, target, w, plan=None):
    t = x.shape[0]
    tm = _row_tile(t, 512)
    grads = {}
    if plan is not None:
        plan.grads = grads
    saved = []
    h = x
    bias_q, bias_k = band_bias(w["rel_bias"])
    qg = jnp.tile(w["hyb_q_gain"], (1, N_HEADS))
    kg = jnp.tile(w["hyb_k_gain"], (1, N_HEADS))

    def run(fn, *a, name):
        hst = plan.host(name) if plan is not None else None
        out = fn(*a, name, hst)
        if hst is not None:
            plan.done(hst)
        return out

    def lru_blocks(n):
        return jnp.transpose(w[n].reshape(N_SHARD, LRU_BLOCKS, 64, LRU_BLOCK), (1, 0, 2, 3)).reshape(LRU_BLOCKS, LRU_BLOCK, LRU_BLOCK)

    for i in range(2):
        s = {}
        s["h0"] = h
        s["hn1"], s["g1"], s["u1"], s["a1"] = run(ffn_up, h, _vec(w["ffn1_norm"], i), w[f"ffn1_w_gate/{i}"], w[f"ffn1_w_up/{i}"], 0, name=f"ffn1_up_{i}")
        h = run(ffn_down, s["a1"], w[f"ffn1_w_down/{i}"], 0, h, name=f"ffn1_down_{i}")
        s["h1"] = h
        if i == 0:
            w_hyb_in = w["hyb_w_in"].reshape(N_SHARD, D_MODEL, -1)
            w_hyb_out = w["hyb_w_out"].reshape(D_MODEL, D_MODEL)
            s["hnm"], s["z"] = run(norm_mm, h, _vec(w["mix_norm"], i), w_hyb_in, name="hyb_in")
            s["qkv"] = hyb_prep(s["z"], qg, kg, "hyb_prep")
            os_, lses = [], []
            for g, dil in enumerate(DILATIONS):
                o, l = run(attn_fwd, *s["qkv"][dil], bias_q[g], dil, name=f"attn_fwd_{dil}")
                os_.append(o)
                lses.append(l)
            s["y2"], s["ya"], s["lt"] = run(hyb_post, s["z"], w["hyb_conv_w"], os_, lses, name="hyb_post")
            h = run(functools.partial(ffn_down, scale=1.0), s["y2"], w_hyb_out.reshape(2, 1, ATTN_W, D_MODEL), 0, h, name="hyb_out")
        else:
            w_rec_in = w["rec_w_in"].reshape(N_SHARD, D_MODEL, -1)
            s["hnm"], s["z"] = run(norm_mm, h, _vec(w["mix_norm"], i), w_rec_in, name="rec_in")
            w_rec_out = w["rec_w_out"].reshape(D_MODEL, D_MODEL)
            lru_wa, lru_wx = lru_blocks("lru_wa"), lru_blocks("lru_wx")
            s["xb"], s["ga"], s["gx"], s["hs"], s["ro"] = run(
                rec_fwd, s["z"], w["rec_conv_w"], w["rec_conv_b"], lru_wa, lru_wx, w["lru_ba"], w["lru_bx"], w["lru_lambda"], name="rec_fwd")
            h = run(mm_acc, s["ro"], _bs((tm, D_MODEL), lambda r, k: (r, 0)), w_rec_out, _bs((D_MODEL, D_MODEL), lambda r, k: (0, 0)),
                    h, 1.0, 1, t, D_MODEL, tm, name="rec_out")
        s["h2"] = h
        s["hn2"], s["g2"], s["u2"], s["a2"] = run(ffn_up, h, _vec(w["ffn2_norm"], i), w[f"ffn2_w_gate/{i}"], w[f"ffn2_w_up/{i}"], 0, name=f"ffn2_up_{i}")
        h = run(ffn_down, s["a2"], w[f"ffn2_w_down/{i}"], 0, h, name=f"ffn2_down_{i}")
        s["h3"] = h
        if i == 0:
            h, s["hnp"], s["gp"], s["pp"] = ple_fwd(h, _vec(w["ple_norm"], i), w[f"ple_w_gate/{i}"], 0, p, i, w[f"ple_w_proj/{i}"], f"ple_fwd_{i}")
        else:
            loss, dh, s["hnp"], s["gp"], s["pp"] = ple_fwd(h, _vec(w["ple_norm"], i), w[f"ple_w_gate/{i}"], 0, p, i, w[f"ple_w_proj/{i}"],
                                                           f"ple_fwd_{i}", target)
        saved.append(s)

    norm_g = {n: [None, None] for n in ("ffn1_norm", "mix_norm", "ffn2_norm", "ple_norm")}
    for i in (1, 0):
        s = saved[i]
        dh_out = dh
        dh, dgp, dpp, norm_g["ple_norm"][i] = run(ple_bwd, dh_out, s["h3"], _vec(w["ple_norm"], i), w[f"ple_w_gate/{i}"], 0, s["gp"], s["pp"],
                                                  name=f"ple_bwd_{i}")
        grads["ple_w_gate"] = run(tn_mm, s["hnp"], lambda tk: _bs((tk, 256), lambda k, j: (j, k)), dgp, lambda tk: _bs((tk, D_MODEL), lambda k, j: (j, 0)),
                                  N_SHARD, t, 256, D_MODEL, S((N_SHARD, 2, 256, D_MODEL), BF16),
                                  _bs((None, None, 256, D_MODEL), lambda k, j, i=i: (k, i, 0, 0)), 1.0, grads.get("ple_w_gate"), name=f"ple_gw_gate_{i}")
        grads["ple_w_proj"] = run(tn_mm, p, lambda tk, i=i: _bs((None, tk, 256), lambda k, j: (i, j, 0)), dpp, lambda tk: _bs((tk, 256), lambda k, j: (j, k)),
                                  N_SHARD, t, 256, 256, S((N_SHARD, 2, 256, 256), BF16),
                                  _bs((None, None, 256, 256), lambda k, j, i=i: (k, i, 0, 0)), 1.0, grads.get("ple_w_proj"), name=f"ple_gw_proj_{i}")
        dh_out = dh
        dg, du, dh16 = run(ffn_bwd_act, dh_out, w[f"ffn2_w_down/{i}"], 0, s["g2"], s["u2"], name=f"ffn2_bwd_act_{i}")
        ffn_wgrads("ffn2", s["hn2"], dh16, s["a2"], dg, du, i, grads, run)
        dh, norm_g["ffn2_norm"][i] = run(ffn_bwd_in, dg, du, w[f"ffn2_w_gate/{i}"], w[f"ffn2_w_up/{i}"], 0, s["h2"], _vec(w["ffn2_norm"], i), dh_out,
                                         name=f"ffn2_bwd_in_{i}")
        dh_out = dh
        if i == 1:
            d_o = nt_mm(dh_out, w_rec_out, "rec_bwd_out")
            grads["rec_w_out"] = run(tn_mm, s["ro"], lambda tk: _bs((tk, 256), lambda k, j: (j, k)), dh_out, lambda tk: _bs((tk, D_MODEL), lambda k, j: (j, 0)),
                                     N_SHARD, t, 256, D_MODEL, S((N_SHARD, 256, D_MODEL), BF16), _bs((None, 256, D_MODEL), lambda k, j: (k, 0, 0)),
                                     1.0, None, name="rec_gw_out").reshape(N_SHARD, 1, 256, D_MODEL)
            dz, dga, dgx, small = run(rec_bwd, d_o, s["z"], s["xb"], s["ga"], s["gx"], s["hs"], w["rec_conv_w"], lru_wa, lru_wx, w["lru_lambda"],
                                      name="rec_bwd")
            blk = lambda tk: _bs((tk, LRU_BLOCK), lambda k, j: (j, k))
            for nm, dgt in (("lru_wa", dga), ("lru_wx", dgx)):
                gw = run(tn_mm, s["xb"], blk, dgt, blk, LRU_BLOCKS, t, LRU_BLOCK, LRU_BLOCK, S((LRU_BLOCKS, LRU_BLOCK, LRU_BLOCK), BF16),
                         _bs((None, LRU_BLOCK, LRU_BLOCK), lambda k, j: (k, 0, 0)), 1.0, None, name="rec_gw_" + nm)
                grads[nm] = jnp.transpose(gw.reshape(LRU_BLOCKS, N_SHARD, 64, LRU_BLOCK), (1, 0, 2, 3)).reshape(N_SHARD, 1, LRU_BLOCKS, 64, LRU_BLOCK)
            grads["lru_ba"], grads["lru_bx"], grads["lru_lambda"], grads["rec_conv_b"] = (small[r:r + 1] for r in range(4))
            grads["rec_conv_w"] = small[4:8]
            nb_, bw = N_SHARD, 512
            w_in, nm_in = w_rec_in, "rec_w_in"
        else:
            dy = nt_mm(dh_out, w_hyb_out, "hyb_bwd_out")
            grads["hyb_w_out"] = run(tn_mm, s["y2"], lambda tk: _bs((None, tk, 256), lambda k, j: (k // 2, j, k % 2)), dh_out,
                                     lambda tk: _bs((tk, D_MODEL), lambda k, j: (j, 0)),
                                     N_SHARD, t, 256, D_MODEL, S((N_SHARD, 256, D_MODEL), BF16), _bs((None, 256, D_MODEL), lambda k, j: (k, 0, 0)),
                                     1.0, None, name="hyb_gw_out").reshape(N_SHARD, 1, 256, D_MODEL)
            delta, dya = attn_delta(dy, s["ya"], "attn_delta")
            dqs, dks, dvs, dbs = [], [], [], []
            for g, dil in enumerate(DILATIONS):
                dq, db = run(attn_bwd_dq, *s["qkv"][dil], dya[dil], s["lt"][dil], delta[dil], bias_q[g], dil, name=f"attn_bwd_dq_{dil}")
                dk, dv = run(attn_bwd_dkv, *s["qkv"][dil], dya[dil], s["lt"][dil], delta[dil], bias_k[g], dil, name=f"attn_bwd_dkv_{dil}")
                dqs.append(dq); dks.append(dk); dvs.append(dv); dbs.append(db)
            grads["rel_bias"] = rel_bias_grad(dbs, "rel_bias_grad")
            dz, grads["hyb_conv_w"], dqg, dkg = hyb_dz(s["z"], dy, w["hyb_conv_w"], qg, kg, dqs, dks, dvs, "hyb_dz")
            grads["hyb_q_gain"] = jnp.sum(dqg.reshape(N_HEADS, HEAD_DIM), axis=0, keepdims=True)
            grads["hyb_k_gain"] = jnp.sum(dkg.reshape(N_HEADS, HEAD_DIM), axis=0, keepdims=True)
            nb_, bw = N_SHARD, 768
            w_in, nm_in = w_hyb_in, "hyb_w_in"
        grads[nm_in] = run(tn_mm, s["hnm"], lambda tk: _bs((tk, D_MODEL), lambda k, j: (j, 0)), dz, lambda tk, bw=bw: _bs((tk, bw), lambda k, j: (j, k)),
                           nb_, t, D_MODEL, bw, S((nb_, D_MODEL, bw), BF16), _bs((None, D_MODEL, bw), lambda k, j: (k, 0, 0)),
                           1.0, None, name=f"mix_gw_in_{i}").reshape(nb_, 1, D_MODEL, bw)
        dh, norm_g["mix_norm"][i] = run(
            nt_acc_normbwd, [(dz, _bs((_row_tile(t, BWD_IN_ROWS), nb_ * bw), lambda r: (r, 0)), w_in, _bs((nb_, D_MODEL, bw), lambda r: (0, 0, 0)),
                              lambda x_ref, k, bw=bw: x_ref[:, k * bw:(k + 1) * bw])],
            nb_, s["h1"], _vec(w["mix_norm"], i), dh_out, name=f"mix_bwd_in_{i}")
        dh_out = dh
        dg, du, dh16 = run(ffn_bwd_act, dh_out, w[f"ffn1_w_down/{i}"], 0, s["g1"], s["u1"], name=f"ffn1_bwd_act_{i}")
        ffn_wgrads("ffn1", s["hn1"], dh16, s["a1"], dg, du, i, grads, run)
        dh, norm_g["ffn1_norm"][i] = run(ffn_bwd_in, dg, du, w[f"ffn1_w_gate/{i}"], w[f"ffn1_w_up/{i}"], 0, s["h0"], _vec(w["ffn1_norm"], i), dh_out,
                                         name=f"ffn1_bwd_in_{i}")
    for n, (g0, g1) in norm_g.items():
        grads[n] = jnp.concatenate([g0, g1], axis=0)
    return loss, dh, grads


def gather_weights(shards, name):
    n = len(shards)

    def body(*refs):
        ins, outs = refs[:n], refs[n:2 * n]
        send1, recv1, send2, recv2, lsem = refs[2 * n:]
        x, y, c, k, chips, kk = _place()
        sib = (x, y, 1 - c)

        def remote(src, dst, ssem, rsem, to):
            return pltpu.make_async_remote_copy(src_ref=src, dst_ref=dst, send_sem=ssem, recv_sem=rsem, device_id=to, device_id_type=MESH)

        local = [pltpu.make_async_copy(ins[a], outs[a].at[k], lsem.at[a]) for a in range(n)]
        for cp in local:
            cp.start()
        sends = []
        for a in range(n):
            for j, chip in enumerate(chips):
                cp = remote(ins[a].at[c], outs[a].at[k, c], send1.at[3 * a + j], recv1.at[3 * a + j], (*chip, c))
                cp.start()
                sends.append(cp)
        for a in range(n):
            for j, chip in enumerate(chips):
                remote(ins[a].at[c], outs[a].at[kk[j], c], send1.at[3 * a + j], recv1.at[3 * a + j], (*chip, c)).wait_recv()
                cp = remote(outs[a].at[kk[j], c], outs[a].at[kk[j], c], send2.at[3 * a + j], recv2.at[3 * a + j], sib)
                cp.start()
                sends.append(cp)
        for a in range(n):
            for j in range(3):
                remote(outs[a].at[kk[j], 1 - c], outs[a].at[kk[j], 1 - c], send2.at[3 * a + j], recv2.at[3 * a + j], sib).wait_recv()
        for cp in sends:
            cp.wait_send()
        for cp in local:
            cp.wait()

    return pl.pallas_call(
        body, name=name,
        in_specs=[_ANY] * n, out_specs=[_ANY] * n,
        out_shape=[S((N_SHARD,) + s.shape, s.dtype) for s in shards],
        scratch_shapes=[pltpu.SemaphoreType.DMA((3 * n,))] * 4 + [pltpu.SemaphoreType.DMA((n,))],
    )(*shards)


def exchange_cores(rs, name):
    n = len(rs)

    def body(*refs):
        outs = refs[n:2 * n]
        send, recv = refs[2 * n:]
        x, y, c = lax.axis_index("x"), lax.axis_index("y"), lax.axis_index("c")
        sends = []
        for a in range(n):
            for k in range(N_SHARD):
                slot = outs[a].at[2 * k + c]
                cp = _remote(slot, slot, send.at[N_SHARD * a + k], recv.at[N_SHARD * a + k], (x, y, 1 - c))
                cp.start()
                sends.append(cp)
        for a in range(n):
            for k in range(N_SHARD):
                slot = outs[a].at[2 * k + 1 - c]
                _remote(slot, slot, send.at[N_SHARD * a + k], recv.at[N_SHARD * a + k], (x, y, 1 - c)).wait_recv()
        for cp in sends:
            cp.wait_send()

    return pl.pallas_call(
        body, name=name,
        in_specs=[_ANY] * n, out_specs=[_ANY] * n,
        out_shape=[S(r.shape, r.dtype) for r in rs],
        input_output_aliases={a: a for a in range(n)},
        scratch_shapes=[pltpu.SemaphoreType.DMA((N_SHARD * n,))] * 2,
    )(*rs)


def allgather8(a, name):
    def body(a_ref, o_ref, send, recv, lsem):
        x, y, c = lax.axis_index("x"), lax.axis_index("y"), lax.axis_index("c")
        me = 4 * x + 2 * y + c
        local = pltpu.make_async_copy(a_ref, o_ref.at[me], lsem)
        local.start()
        cps = []
        for f in range(1, N_DEV):
            fx, fy, fc = (f >> 2) & 1, (f >> 1) & 1, f & 1
            peer = (1 - x if fx else x, 1 - y if fy else y, 1 - c if fc else c)
            cp = pltpu.make_async_remote_copy(src_ref=a_ref, dst_ref=o_ref.at[me], send_sem=send.at[f - 1], recv_sem=recv.at[f - 1],
                                              device_id=peer, device_id_type=MESH)
            cp.start()
            cps.append((cp, 4 * peer[0] + 2 * peer[1] + peer[2], f))
        for cp, pidx, f in cps:
            pltpu.make_async_remote_copy(src_ref=a_ref, dst_ref=o_ref.at[pidx], send_sem=send.at[f - 1], recv_sem=recv.at[f - 1],
                                         device_id=(x, y, c), device_id_type=MESH).wait_recv()
        for cp, _, _ in cps:
            cp.wait_send()
        local.wait()

    return pl.pallas_call(
        body, name=name, in_specs=[_ANY], out_specs=_ANY,
        out_shape=S((N_DEV,) + a.shape, a.dtype),
        scratch_shapes=[pltpu.SemaphoreType.DMA((N_DEV - 1,)), pltpu.SemaphoreType.DMA((N_DEV - 1,)), pltpu.SemaphoreType.DMA],
    )(a)


def sum8(a, name):
    _, r, c = a.shape

    def body(a_ref, o_ref):
        acc = a_ref[0]
        for j in range(1, N_DEV):
            acc = acc + a_ref[j]
        o_ref[...] = acc

    return pl.pallas_call(
        body, name=name, in_specs=[_bs((N_DEV, r, c), lambda: (0, 0, 0))], out_specs=_bs((r, c), lambda: (0, 0)),
        out_shape=S((r, c), F32),
    )(a)


def adamw(w, m, v, g, name):
    nl, r, c = w.shape
    tr = _row_tile(r, 256)
    summed = g.ndim == 4

    def body(w_ref, m_ref, v_ref, g_ref, go_ref, d_ref, mo_ref, vo_ref):
        if summed:
            gr = g_ref[0].astype(F32)
            for j in range(1, N_DEV):
                gr = gr + g_ref[j].astype(F32)
        else:
            gr = g_ref[...]
        m_new = ADAM_B1 * m_ref[...] + (1.0 - ADAM_B1) * gr
        v_new = ADAM_B2 * v_ref[...] + (1.0 - ADAM_B2) * (gr * gr)
        m_hat = m_new / (1.0 - ADAM_B1 ** ADAM_STEP)
        v_hat = v_new / (1.0 - ADAM_B2 ** ADAM_STEP)
        go_ref[...] = gr
        d_ref[...] = -ADAM_LR * (m_hat / (jnp.sqrt(v_hat) + ADAM_EPS) + ADAM_WD * w_ref[...])
        mo_ref[...] = m_new
        vo_ref[...] = v_new

    row = _bs((None, tr, c), lambda l, i: (l, i, 0))
    gspec = _bs((N_DEV, None, tr, c), lambda l, i: (0, l, i, 0)) if summed else row
    return pl.pallas_call(
        body, name=name, grid=(nl, r // tr),
        in_specs=[row, row, row, gspec], out_specs=[row] * 4, out_shape=[S((nl, r, c), F32)] * 4,
        compiler_params=_cp("parallel", "parallel"),
    )(w, m, v, g)


WEIGHTS = ["rel_bias", "ffn1_norm", "ffn1_w_gate", "ffn1_w_up", "ffn1_w_down", "mix_norm", "hyb_w_in", "hyb_conv_w", "hyb_q_gain",
           "hyb_k_gain", "hyb_w_out", "rec_w_in", "rec_conv_w", "rec_conv_b", "lru_wa", "lru_ba", "lru_wx", "lru_bx", "lru_lambda",
           "rec_w_out", "ffn2_norm", "ffn2_w_gate", "ffn2_w_up", "ffn2_w_down", "ple_norm", "ple_w_gate", "ple_w_proj"]
BIG = ["ffn1_w_gate", "ffn1_w_up", "ffn1_w_down", "hyb_w_in", "hyb_w_out", "rec_w_in", "lru_wa", "lru_wx", "rec_w_out",
       "ffn2_w_gate", "ffn2_w_up", "ffn2_w_down", "ple_w_gate", "ple_w_proj"]
SMALL_SHARDED = ["hyb_conv_w", "rec_conv_w", "rec_conv_b", "lru_ba", "lru_bx", "lru_lambda"]
SMALL = ["rel_bias", "ffn1_norm", "mix_norm", "ffn2_norm", "ple_norm", "hyb_q_gain", "hyb_k_gain"] + SMALL_SHARDED
PACK_W = 1024
PER_LAYER = ["ffn1_w_gate", "ffn1_w_up", "ffn1_w_down", "ffn2_w_gate", "ffn2_w_up", "ffn2_w_down", "ple_w_gate", "ple_w_proj"]
FIRST = ["ffn1_w_gate/0", "ffn1_w_up/0"]
LAST = ["ffn1_w_down"]
GATHER_PLAN = {
    "ffn1_up_0": ["ffn1_w_down/0", "hyb_w_in"],
    "ffn1_down_0": ["hyb_w_out", "ple_w_gate/0", "ple_w_proj/0"],
    "hyb_post": ["ffn2_w_gate/0"],
    "attn_fwd_1": ["ffn2_w_up/0"],
    "attn_fwd_4": ["ffn2_w_down/0"],
    "attn_fwd_16": ["ffn1_w_gate/1"],
    "ffn2_up_0": ["ffn1_w_up/1"],
    "ffn2_down_0": ["lru_wa", "lru_wx", "rec_w_out"],
    "ffn1_up_1": ["ffn1_w_down/1", "rec_w_in"],
    "ffn1_down_1": ["ffn2_w_down/1"],
    "rec_in": ["ple_w_gate/1", "ple_w_proj/1"],
    "rec_fwd": ["ffn2_w_gate/1", "ffn2_w_up/1"],
}
SCATTER_PLAN = {
    "ple_gw_proj_1": [("ple_w_gate", 1)],
    "ffn2_bwd_act_1": [("ple_w_proj", 1)],
    "ffn2_bwd_in_1": [("ffn2_w_gate", 1)],
    "rec_bwd": [("ffn2_w_up", 1), ("ffn2_w_down", 1)],
    "mix_bwd_in_1": [("rec_w_in", 0), ("rec_w_out", 0), ("lru_wa", 0), ("lru_wx", 0)],
    "ffn1_bwd_in_1": [("ffn1_w_gate", 1)],
    "ple_bwd_0": [("ffn1_w_up", 1)],
    "ple_gw_proj_0": [("ple_w_gate", 0)],
    "ffn2_bwd_act_0": [("ple_w_proj", 0)],
    "ffn2_gw_0_gate": [("ffn1_w_down", 1)],
    "ffn2_bwd_in_0": [("ffn2_w_gate", 0)],
    "attn_bwd_dq_1": [("ffn2_w_down", 0)],
    "attn_bwd_dkv_1": [("ffn2_w_up", 0)],
    "mix_bwd_in_0": [("hyb_w_in", 0), ("hyb_w_out", 0)],
    "ffn1_gw_0_up": [("ffn1_w_gate", 0)],
    "ffn1_gw_0_down": [("ffn1_w_up", 0)],
    "ffn1_bwd_in_0": [("ffn1_w_down", 0)],
}
FORWARD_PLAN = {
    "ffn1_bwd_in_1": ["rec_w_in", "rec_w_out", "lru_wa", "lru_wx"],
    "ffn2_bwd_in_0": ["ple_w_gate", "ple_w_proj"],
    "mix_bwd_in_0": ["ffn2_w_gate", "ffn2_w_up", "ffn2_w_down"],
    "ffn1_gw_0_up": ["hyb_w_in", "hyb_w_out"],
    "ffn1_bwd_in_0": ["ffn1_w_gate", "ffn1_w_up"],
}


class Plan:
    def __init__(self, shards, w):
        self.shards, self.w, self.grads, self.landed = shards, w, None, {}

    def host(self, kname):
        if kname in GATHER_PLAN:
            h = Host("gather", [self.shards[n] for n in GATHER_PLAN[kname]])
            h.names = GATHER_PLAN[kname]
            return h
        if kname in SCATTER_PLAN or kname in FORWARD_PLAN:
            items = SCATTER_PLAN.get(kname, [])
            fwd = FORWARD_PLAN.get(kname, [])
            h = Host("scatter", [(self.grads[n], lay, self.landed.get(n)) for n, lay in items], [self.landed[n] for n in fwd])
            h.names = [n for n, _ in items] + fwd
            return h
        return None

    def done(self, h):
        for n, o in zip(h.names, h.outs):
            if h.kind == "gather":
                self.w[n] = o
            else:
                self.landed[n] = o


def _halves(a):
    if a.shape[0] == 2:
        return a
    return a.reshape((2, a.shape[1] // 2) + a.shape[2:])


def _pack_rows(arrs, width):
    rows, offs, r0 = [], [], 0
    for a in arrs:
        if a.shape[1] > width:
            a = a.reshape(-1, width)
        rows.append(jnp.pad(a, ((0, 0), (0, width - a.shape[1]))))
        offs.append(r0)
        r0 += a.shape[0]
    pad = (-r0) % 8
    if pad:
        rows.append(jnp.zeros((pad, width), F32))
    return jnp.concatenate(rows, axis=0), offs


def kernel(x, p, rel_bias, ffn1_norm, ffn1_w_gate, ffn1_w_up, ffn1_w_down, mix_norm, hyb_w_in, hyb_conv_w, hyb_q_gain, hyb_k_gain, hyb_w_out, rec_w_in, rec_conv_w, rec_conv_b, lru_wa, lru_ba, lru_wx, lru_bx, lru_lambda, rec_w_out, ffn2_norm, ffn2_w_gate, ffn2_w_up, ffn2_w_down, ple_norm, ple_w_gate, ple_w_proj, loss_target, m_rel_bias, m_ffn1_norm, m_ffn1_w_gate, m_ffn1_w_up, m_ffn1_w_down, m_mix_norm, m_hyb_w_in, m_hyb_conv_w, m_hyb_q_gain, m_hyb_k_gain, m_hyb_w_out, m_rec_w_in, m_rec_conv_w, m_rec_conv_b, m_lru_wa, m_lru_ba, m_lru_wx, m_lru_bx, m_lru_lambda, m_rec_w_out, m_ffn2_norm, m_ffn2_w_gate, m_ffn2_w_up, m_ffn2_w_down, m_ple_norm, m_ple_w_gate, m_ple_w_proj, v_rel_bias, v_ffn1_norm, v_ffn1_w_gate, v_ffn1_w_up, v_ffn1_w_down, v_mix_norm, v_hyb_w_in, v_hyb_conv_w, v_hyb_q_gain, v_hyb_k_gain, v_hyb_w_out, v_rec_w_in, v_rec_conv_w, v_rec_conv_b, v_lru_wa, v_lru_ba, v_lru_wx, v_lru_bx, v_lru_lambda, v_rec_w_out, v_ffn2_norm, v_ffn2_w_gate, v_ffn2_w_up, v_ffn2_w_down, v_ple_norm, v_ple_w_gate, v_ple_w_proj):
    given = dict(locals())
    wts = {n: given[n] for n in WEIGHTS}
    k_chip = 2 * lax.axis_index("x") + lax.axis_index("y")

    shards = {}
    for n in BIG:
        b16 = wts[n].astype(BF16)
        if n in PER_LAYER:
            shards[n + "/0"], shards[n + "/1"] = b16[0:1], b16[1:2]
        else:
            shards[n] = b16
    first = gather_weights([_halves(shards[n]) for n in FIRST], "gather_first")
    w = {n: g.reshape((N_SHARD,) + shards[n].shape) for n, g in zip(FIRST, first)}
    plan = Plan(shards, w)
    sm2d = {n: wts[n].reshape(-1, wts[n].shape[-1]) for n in SMALL_SHARDED}
    slab, offs = _pack_rows([sm2d[n] for n in SMALL_SHARDED], 256)
    slabs = allgather8(slab, "gather_small")[0::2]
    for n, o in zip(SMALL_SHARDED, offs):
        r, cw = sm2d[n].shape
        w[n] = jnp.concatenate([slabs[kc, o:o + r, :cw] for kc in range(N_SHARD)], axis=1)
    for n in SMALL:
        if n not in SMALL_SHARDED:
            w[n] = wts[n]

    loss, dx, grads = local_step(x[0], p.reshape(p.shape[0], p.shape[2], p.shape[3]), loss_target[0], w, plan)
    loss = lax.psum(loss[0, 0], ("x", "y", "c"))

    for n, r8 in zip(LAST, exchange_cores([plan.landed[n] for n in LAST], "exchange_cores")):
        plan.landed[n] = r8
    out = {}
    for n in BIG:
        r8 = plan.landed[n]
        shp = wts[n].shape
        shp3 = shp if len(shp) == 3 else (shp[0], -1, shp[-1])
        three = lambda a: a.reshape(shp3)
        res = adamw(three(wts[n]), three(given["m_" + n]), three(given["v_" + n]), r8.reshape((N_DEV,) + three(wts[n]).shape), "adamw_" + n)
        out[n] = [a.reshape(shp) for a in res]
    g2d = [grads[n].reshape(-1, grads[n].shape[-1]) if n != "rel_bias" else grads[n].reshape(1, -1) for n in SMALL]
    gslab, goffs = _pack_rows(g2d, PACK_W)
    gsum = sum8(allgather8(gslab, "gather_small_grads"), "sum_small_grads")
    for n, o, g in zip(SMALL, goffs, g2d):
        shp = wts[n].shape
        r, cw = g.shape
        gs = gsum[o:o + r, :cw]
        if n in SMALL_SHARDED:
            sw = shp[-1]
            gs = lax.dynamic_slice_in_dim(gs, k_chip * sw, sw, axis=1)
        three = lambda a: a.reshape((1, -1, shp[-1]))
        res = adamw(three(wts[n]), three(given["m_" + n]), three(given["v_" + n]), three(gs), "adamw_" + n)
        out[n] = [a.reshape(shp) for a in res]
    return (loss, dx[None], *[out[n][0] for n in WEIGHTS], *[out[n][1] for n in WEIGHTS],
            *[out[n][2] for n in WEIGHTS], *[out[n][3] for n in WEIGHTS])
```

```python
import functools
import math

import numpy as np
import jax
import jax.numpy as jnp
from jax import lax
from jax.experimental import pallas as pl
from jax.experimental.pallas import tpu as pltpu

F32, BF16 = jnp.float32, jnp.bfloat16
S = jax.ShapeDtypeStruct
MESH = pl.DeviceIdType.MESH

D_MODEL = 1024
N_SHARD = 4
N_DEV = 8
HEAD_DIM = 64
N_HEADS = 8
ATTN_W = N_HEADS * HEAD_DIM
CONV_W = 512
BAND = 128
DILATIONS = (1, 4, 16)
REL_BUCKETS = 32
REL_MAX_DIST = 2048
LRU_BLOCKS = 4
LRU_BLOCK = 256
LRU_C = 8.0
EPS = 1e-6
NEG = -1e30
VMEM_LIMIT = 56 * 1024 * 1024
FFN_ROWS = 1024
FFN_SHARDS_ROWS = 512
TN_ROWS = 2048
SCAN_ROWS = 128
BWD_IN_ROWS = 512

ADAM_LR, ADAM_B1, ADAM_B2, ADAM_EPS, ADAM_WD, ADAM_STEP = 0.001, 0.9, 0.999, 1e-08, 0.01, 10


def _cp(*sem):
    return pltpu.CompilerParams(dimension_semantics=sem, vmem_limit_bytes=VMEM_LIMIT)


def _bs(shape, imap):
    return pl.BlockSpec(shape, imap)


def _row_tile(t, want):
    for cand in range(min(want, t) // 8 * 8, 0, -8):
        if t % cand == 0:
            return cand
    return t


_ANY = pl.BlockSpec(memory_space=pl.ANY)


def _place():
    x, y, c = lax.axis_index("x"), lax.axis_index("y"), lax.axis_index("c")
    chips = [(1 - x, y), (x, 1 - y), (1 - x, 1 - y)]
    return x, y, c, 2 * x + y, chips, [2 * cx + cy for cx, cy in chips]


def _remote(src, dst, ssem, rsem, to):
    return pltpu.make_async_remote_copy(src_ref=src, dst_ref=dst, send_sem=ssem, recv_sem=rsem, device_id=to, device_id_type=MESH)


class Host:
    def __init__(self, kind, items, forwards=()):
        self.kind, self.items, self.forwards, self.outs = kind, items, list(forwards), None

    def n_sems(self):
        return 3 * len(self.items) + N_SHARD * len(self.forwards), len(self.items)

    def operands(self):
        if self.kind == "gather":
            return list(self.items), [S((N_SHARD,) + s.shape, s.dtype) for s in self.items], {}
        xin, shapes, alias = [], [], {}
        for a, (g, _, r_prev) in enumerate(self.items):
            xin.append(g)
            if r_prev is not None:
                alias[len(xin)] = a
                xin.append(r_prev)
            shapes.append(S((N_DEV,) + g.shape[1:], g.dtype))
        for f, r in enumerate(self.forwards):
            alias[len(xin)] = len(self.items) + f
            xin.append(r)
            shapes.append(S(r.shape, r.dtype))
        return xin, shapes, alias

    def copies(self, xi, xo, send, recv, lsem):
        x, y, c, k, chips, kk = _place()
        starts, waits = [], []
        pos = 0
        for f in range(len(self.forwards)):
            arr = xo[len(self.items) + f]
            for kq in range(N_SHARD):
                sem = 3 * len(self.items) + N_SHARD * f + kq
                cp = _remote(arr.at[2 * kq + c], arr.at[2 * kq + c], send.at[sem], recv.at[sem], (x, y, 1 - c))
                starts.append((cp, "start"))
                waits.append((cp, "wait_send"))
                other = arr.at[2 * kq + 1 - c]
                waits.append((_remote(other, other, send.at[sem], recv.at[sem], (x, y, 1 - c)), "wait_recv"))
        for a, item in enumerate(self.items):
            if self.kind == "gather":
                src_of = lambda chip_idx, s=xi[a]: s
                dst_of = lambda chip_idx, o=xo[a]: o.at[chip_idx]
                mine, theirs = k, kk
            else:
                g_ref = xi[pos]
                pos += 1 if item[2] is None else 2
                lay = item[1]
                src_of = lambda chip_idx, g=g_ref, lay=lay: g.at[chip_idx, lay]
                dst_of = lambda slot, o=xo[a], lay=lay: o.at[slot, lay]
                mine, theirs = 2 * k + c, [2 * kj + c for kj in kk]
            own_src = src_of(k)
            local = pltpu.make_async_copy(own_src, dst_of(mine), lsem.at[a])
            starts.append((local, "start"))
            waits.append((local, "wait"))
            for j, chip in enumerate(chips):
                src = own_src if self.kind == "gather" else src_of(kk[j])
                cp = _remote(src, dst_of(mine), send.at[3 * a + j], recv.at[3 * a + j], (*chip, c))
                starts.append((cp, "start"))
                waits.append((cp, "wait_send"))
                waits.append((_remote(own_src, dst_of(theirs[j]), send.at[3 * a + j], recv.at[3 * a + j], (*chip, c)), "wait_recv"))
        return starts, waits


def _call(host, body, *, name, grid, in_specs, out_specs, out_shape, scratch_shapes=(), compiler_params=None, args, aliases=None):
    aliases = dict(aliases or {})
    if host is None:
        return pl.pallas_call(body, name=name, grid=grid, in_specs=in_specs, out_specs=out_specs, out_shape=out_shape,
                              scratch_shapes=list(scratch_shapes), input_output_aliases=aliases, compiler_params=compiler_params)(*args)
    single = not isinstance(out_shape, (list, tuple))
    out_specs_l = [out_specs] if single else list(out_specs)
    out_shape_l = [out_shape] if single else list(out_shape)
    n_in, n_out, n_scr = len(in_specs), len(out_shape_l), len(scratch_shapes)
    xin, xshapes, xalias = host.operands()
    n_items = len(xshapes)
    n_rsem, n_lsem = host.n_sems()
    for i_in, i_out in xalias.items():
        aliases[n_in + i_in] = n_out + i_out
    nd = len(grid)

    def hosted(*refs):
        ins, xi = refs[:n_in], refs[n_in:n_in + len(xin)]
        o0 = n_in + len(xin)
        outs, xo = refs[o0:o0 + n_out], refs[o0 + n_out:o0 + n_out + n_items]
        s0 = o0 + n_out + n_items
        scr = refs[s0:s0 + n_scr]
        send, recv, lsem = refs[s0 + n_scr:]
        first = functools.reduce(jnp.logical_and, [pl.program_id(d) == 0 for d in range(nd)])
        last = functools.reduce(jnp.logical_and, [pl.program_id(d) == grid[d] - 1 for d in range(nd)])
        starts, waits = host.copies(xi, xo, send, recv, lsem)

        @pl.when(first)
        def _():
            for cp, how in starts:
                getattr(cp, how)()
        body(*ins, *outs, *scr)

        @pl.when(last)
        def _():
            for cp, how in waits:
                getattr(cp, how)()

    res = pl.pallas_call(
        hosted, name=name, grid=grid,
        in_specs=list(in_specs) + [_ANY] * len(xin),
        out_specs=out_specs_l + [_ANY] * n_items,
        out_shape=out_shape_l + xshapes,
        scratch_shapes=list(scratch_shapes) + [pltpu.SemaphoreType.DMA((n_rsem,)), pltpu.SemaphoreType.DMA((n_rsem,)),
                                               pltpu.SemaphoreType.DMA((max(n_lsem, 1),))],
        input_output_aliases=aliases,
        compiler_params=pltpu.CompilerParams(dimension_semantics=("arbitrary",) * nd, vmem_limit_bytes=VMEM_LIMIT),
    )(*args, *xin)
    host.outs = list(res[n_out:])
    return res[0] if single else list(res[:n_out])


def _rstd(x):
    return lax.rsqrt(jnp.mean(x * x, axis=-1, keepdims=True) + EPS)


def _sigmoid(x):
    return 1.0 / (1.0 + jnp.exp(-x))


def _dot(a, b):
    return jnp.dot(a, b, preferred_element_type=F32)


def _dot_nt(a, b):
    return lax.dot_general(a, b, (((1,), (1,)), ((), ())), preferred_element_type=F32)


def _dot_tn(a, b):
    return lax.dot_general(a, b, (((0,), (0,)), ((), ())), preferred_element_type=F32)


def _seg_dot(x, seg_bf16):
    hi = x.astype(BF16)
    lo = (x - hi.astype(F32)).astype(BF16)
    return _dot(hi, seg_bf16) + _dot(lo, seg_bf16)


def _shift_down(x, prev8, s):
    if s == 0:
        return x
    tm = x.shape[0]
    row = lax.broadcasted_iota(jnp.int32, x.shape, 0)
    main = jnp.where(row >= s, pltpu.roll(x, s, axis=0), 0.0)
    row8 = lax.broadcasted_iota(jnp.int32, prev8.shape, 0)
    head = jnp.where(row8 < s, pltpu.roll(prev8, s, axis=0), 0.0)
    if tm == 8:
        return main + head
    return main + jnp.concatenate([head, jnp.zeros((tm - 8, x.shape[1]), x.dtype)], axis=0)


def _shift_up(x, next8, s):
    if s == 0:
        return x
    tm = x.shape[0]
    row = lax.broadcasted_iota(jnp.int32, x.shape, 0)
    main = jnp.where(row < tm - s, pltpu.roll(x, tm - s, axis=0), 0.0)
    row8 = lax.broadcasted_iota(jnp.int32, next8.shape, 0)
    tail = jnp.where(row8 >= 8 - s, pltpu.roll(next8, 8 - s, axis=0), 0.0)
    if tm == 8:
        return main + tail
    return main + jnp.concatenate([jnp.zeros((tm - 8, x.shape[1]), x.dtype), tail], axis=0)


def _roll_fill(x, s, fill, up):
    tm = x.shape[0]
    row = lax.broadcasted_iota(jnp.int32, x.shape, 0)
    if up:
        return jnp.where(row < tm - s, pltpu.roll(x, tm - s, axis=0), fill)
    return jnp.where(row >= s, pltpu.roll(x, s, axis=0), fill)


def _log1p(y):
    u = 1.0 + y
    return jnp.where(u == 1.0, y, jnp.log(u) * (y / jnp.where(u == 1.0, 1.0, u - 1.0)))


def _softplus(x):
    return jnp.maximum(x, 0.0) + _log1p(jnp.exp(-jnp.abs(x)))


def _neg_expm1(y):
    series = -y * (1.0 + y * (0.5 + y * (1.0 / 6.0 + y * (1.0 / 24.0 + y * (1.0 / 120.0)))))
    return jnp.where(jnp.abs(y) < 0.03, series, 1.0 - jnp.exp(y))


_GELU_C = math.sqrt(2.0 / math.pi)


def _gelu_and_grad(x):
    inner = _GELU_C * (x + 0.044715 * x * x * x)
    t = jnp.tanh(inner)
    g = 0.5 * x * (1.0 + t)
    dg = 0.5 * (1.0 + t) + 0.5 * x * (1.0 - t * t) * _GELU_C * (1.0 + 3.0 * 0.044715 * x * x)
    return g, dg


def _rmsnorm_bwd(x, gain, dy):
    r = _rstd(x)
    xhat = x * r
    dxhat = dy * gain
    dx = r * (dxhat - xhat * jnp.mean(dxhat * xhat, axis=-1, keepdims=True))
    return dx, jnp.sum(dy * xhat, axis=0, keepdims=True)


def ffn_up(h, gain, wg, wu, layer, name, host=None):
    t, d = h.shape
    nk, _, _, f = wg.shape
    tm = _row_tile(t, FFN_SHARDS_ROWS)

    def body(h_ref, g_ref, wg_ref, wu_ref, hn_ref, gg_ref, uu_ref, aa_ref):
        x = h_ref[...]
        hn = (x * _rstd(x) * g_ref[...]).astype(BF16)
        hn_ref[...] = hn
        for k in range(nk):
            g = _dot(hn, wg_ref[k])
            u = _dot(hn, wu_ref[k])
            s = _sigmoid(g)
            silu = g * s
            gg_ref[k] = (u * (s * (1.0 + g * (1.0 - s)))).astype(BF16)
            uu_ref[k] = silu.astype(BF16)
            aa_ref[k] = (silu * u).astype(BF16)

    wspec = _bs((nk, None, d, f), lambda i: (0, layer, 0, 0))
    aspec = _bs((nk, tm, f), lambda i: (0, i, 0))
    return _call(
        host, body, name=name, grid=(t // tm,),
        in_specs=[_bs((tm, d), lambda i: (i, 0)), _bs((1, d), lambda i: (0, 0)), wspec, wspec],
        out_specs=[_bs((tm, d), lambda i: (i, 0)), aspec, aspec, aspec],
        out_shape=[S((t, d), BF16), S((nk, t, f), BF16), S((nk, t, f), BF16), S((nk, t, f), BF16)],
        compiler_params=_cp("parallel"),
        args=(h, gain, wg, wu))


def mm_acc(a, a_spec, b, b_spec, res, scale, nk, t, n, tm, name, host=None):
    def body(a_ref, b_ref, r_ref, o_ref, acc):
        k = pl.program_id(1)

        @pl.when(k == 0)
        def _():
            acc[...] = jnp.zeros_like(acc)
        acc[...] += _dot(a_ref[...].astype(BF16), b_ref[...])

        @pl.when(k == nk - 1)
        def _():
            o_ref[...] = r_ref[...] + scale * acc[...]

    return _call(
        host, body, name=name, grid=(t // tm, nk),
        in_specs=[a_spec, b_spec, _bs((tm, n), lambda i, k: (i, 0))],
        out_specs=_bs((tm, n), lambda i, k: (i, 0)),
        out_shape=S((t, n), F32),
        scratch_shapes=[pltpu.VMEM((tm, n), F32)],
        compiler_params=_cp("parallel", "arbitrary"),
        args=(a, b, res))


def ffn_down(a, wd, layer, h, name, host=None, scale=0.5):
    nk, t, f = a.shape
    d = h.shape[1]
    tm = _row_tile(t, FFN_ROWS)

    def body(a_ref, w_ref, r_ref, o_ref):
        acc = _dot(a_ref[0], w_ref[0])
        for k in range(1, nk):
            acc = acc + _dot(a_ref[k], w_ref[k])
        o_ref[...] = r_ref[...] + scale * acc

    row = _bs((tm, d), lambda i: (i, 0))
    return _call(
        host, body, name=name, grid=(t // tm,),
        in_specs=[_bs((nk, tm, f), lambda i: (0, i, 0)), _bs((nk, None, f, d), lambda i: (0, layer, 0, 0)), row],
        out_specs=row, out_shape=S((t, d), F32),
        compiler_params=_cp("parallel"),
        args=(a, wd, h))


def ffn_bwd_act(dh, wd, layer, gg, uu, name, host=None):
    nk, t, f = gg.shape
    d = dh.shape[1]
    tm = _row_tile(t, FFN_SHARDS_ROWS)

    def body(dh_ref, wd_ref, g_ref, u_ref, dg_ref, du_ref, dh16_ref):
        dh16 = dh_ref[...].astype(BF16)
        dh16_ref[...] = dh16
        for k in range(nk):
            da = 0.5 * _dot_nt(dh16, wd_ref[k])
            dg_ref[k] = (da * g_ref[k].astype(F32)).astype(BF16)
            du_ref[k] = (da * u_ref[k].astype(F32)).astype(BF16)

    aspec = _bs((nk, tm, f), lambda i: (0, i, 0))
    row = _bs((tm, d), lambda i: (i, 0))
    return _call(
        host, body, name=name, grid=(t // tm,),
        in_specs=[row, _bs((nk, None, f, d), lambda i: (0, layer, 0, 0)), aspec, aspec],
        out_specs=[aspec, aspec, row],
        out_shape=[S((nk, t, f), BF16), S((nk, t, f), BF16), S((t, d), BF16)],
        compiler_params=_cp("parallel"),
        args=(dh, wd, gg, uu))


def nt_acc_normbwd(terms, nk, h, gain, dh, name, host=None):
    t, d = h.shape
    tm = _row_tile(t, BWD_IN_ROWS)
    sub = _row_tile(tm, 256)
    nterm = len(terms)
    picks = [term[4] for term in terms]

    def body(*refs):
        xs = refs[:2 * nterm]
        h_ref, g_ref, dh_ref, o_ref, dg_ref, acc = refs[2 * nterm:]

        @pl.when(pl.program_id(0) == 0)
        def _():
            dg_ref[...] = jnp.zeros_like(dg_ref)
        tot = None
        for j in range(nterm):
            for k in range(nk):
                part = _dot_nt(picks[j](xs[2 * j], k), xs[2 * j + 1][k])
                tot = part if tot is None else tot + part
        acc[...] = tot

        def rows_of(cidx, dgain):
            rows = pl.ds(pl.multiple_of(cidx * sub, sub), sub)
            dx, dgc = _rmsnorm_bwd(h_ref[rows, :], g_ref[...], acc[rows, :])
            o_ref[rows, :] = dh_ref[rows, :] + dx
            return dgain + dgc
        dg_ref[...] += lax.fori_loop(0, tm // sub, rows_of, jnp.zeros((1, d), F32))

    in_specs, args = [], []
    for x, xs_, w, ws_, _ in terms:
        in_specs += [xs_, ws_]
        args += [x, w]
    row = _bs((tm, d), lambda i: (i, 0))
    vec = _bs((1, d), lambda i: (0, 0))
    return _call(
        host, body, name=name, grid=(t // tm,),
        in_specs=in_specs + [row, vec, row],
        out_specs=[row, vec],
        out_shape=[S((t, d), F32), S((1, d), F32)],
        scratch_shapes=[pltpu.VMEM((tm, d), F32)],
        compiler_params=_cp("arbitrary"),
        args=(*args, h, gain, dh))


def ffn_bwd_in(dg, du, wg, wu, layer, h, gain, dh, name, host=None):
    nk, t, f = dg.shape
    d = h.shape[1]
    tm = _row_tile(t, BWD_IN_ROWS)
    aspec = _bs((nk, tm, f), lambda i: (0, i, 0))
    wspec = _bs((nk, None, d, f), lambda i: (0, layer, 0, 0))
    pick = lambda x_ref, k: x_ref[k]
    return nt_acc_normbwd([(dg, aspec, wg, wspec, pick), (du, aspec, wu, wspec, pick)], nk, h, gain, dh, name, host)


def tn_mm(x, x_spec, y, y_spec, nblk, t, ka, nb, out_shape, out_spec, scale, prev, name, host=None):
    tk = _row_tile(t, TN_ROWS if F32 in (x.dtype, y.dtype) else 2 * TN_ROWS)

    def body(*refs):
        if prev is None:
            x_ref, y_ref, o_ref, acc = refs
        else:
            x_ref, y_ref, _, o_ref, acc = refs
        j = pl.program_id(1)

        @pl.when(j == 0)
        def _():
            acc[...] = jnp.zeros_like(acc)
        acc[...] += _dot_tn(x_ref[...].astype(BF16), y_ref[...].astype(BF16))

        @pl.when(j == t // tk - 1)
        def _():
            o_ref[...] = (scale * acc[...]).astype(o_ref.dtype)

    in_specs = [x_spec(tk), y_spec(tk)]
    args = [x, y]
    aliases = {}
    if prev is not None:
        in_specs.append(pl.BlockSpec(memory_space=pl.ANY))
        args.append(prev)
        aliases = {2: 0}
    return _call(
        host, body, name=name, grid=(nblk, t // tk),
        in_specs=in_specs, out_specs=out_spec, out_shape=out_shape,
        scratch_shapes=[pltpu.VMEM((ka, nb), F32)],
        aliases=aliases,
        compiler_params=_cp("parallel", "arbitrary"),
        args=tuple(args))


def ffn_wgrads(which, hn, dh, aa, dg, du, layer, grads, run):
    nk, t, f = aa.shape
    d = hn.shape[1]
    hn_spec = lambda tk: _bs((tk, d), lambda k, j: (j, 0))
    a_spec = lambda tk: _bs((None, tk, f), lambda k, j: (k, j, 0))
    shape_gu, spec_gu = S((nk, 2, d, f), BF16), _bs((None, None, d, f), lambda k, j: (k, layer, 0, 0))
    shape_d, spec_d = S((nk, 2, f, d), BF16), _bs((None, None, f, d), lambda k, j: (k, layer, 0, 0))
    for suffix, x, xs, y, ys, ka, nb, shp, spec, scale in (
            ("gate", hn, hn_spec, dg, a_spec, d, f, shape_gu, spec_gu, 1.0),
            ("up", hn, hn_spec, du, a_spec, d, f, shape_gu, spec_gu, 1.0),
            ("down", aa, a_spec, dh, hn_spec, f, d, shape_d, spec_d, 0.5)):
        key = f"{which}_w_{suffix}"
        grads[key] = run(tn_mm, x, xs, y, ys, nk, t, ka, nb, shp, spec, scale, grads.get(key), name=f"{which}_gw_{layer}_{suffix}")


def norm_mm(h, gain, w, name, host=None):
    t, d = h.shape
    nb, _, bw = w.shape
    tm = _row_tile(t, FFN_SHARDS_ROWS)

    def body(h_ref, g_ref, w_ref, hn_ref, z_ref):
        x = h_ref[...]
        hn = (x * _rstd(x) * g_ref[...]).astype(BF16)
        hn_ref[...] = hn
        for k in range(nb):
            z_ref[:, k * bw:(k + 1) * bw] = _dot(hn, w_ref[k])

    return _call(
        host, body, name=name, grid=(t // tm,),
        in_specs=[_bs((tm, d), lambda i: (i, 0)), _bs((1, d), lambda i: (0, 0)), _bs((nb, d, bw), lambda i: (0, 0, 0))],
        out_specs=[_bs((tm, d), lambda i: (i, 0)), _bs((tm, nb * bw), lambda i: (i, 0))],
        out_shape=[S((t, d), BF16), S((t, nb * bw), F32)],
        compiler_params=_cp("parallel"),
        args=(h, gain, w))


def nt_mm(a, w, name):
    t, k = a.shape
    n = w.shape[0]
    tm = _row_tile(t, 512)

    def body(a_ref, w_ref, o_ref):
        o_ref[...] = _dot_nt(a_ref[...].astype(BF16), w_ref[...])

    return pl.pallas_call(
        body, name=name, grid=(t // tm,),
        in_specs=[_bs((tm, k), lambda i: (i, 0)), _bs((n, k), lambda i: (0, 0))],
        out_specs=_bs((tm, n), lambda i: (i, 0)),
        out_shape=S((t, n), F32),
        compiler_params=_cp("parallel"),
    )(a, w)


def _head_mean_matrix():
    m = np.kron(np.eye(N_HEADS, dtype=np.float32), np.full((HEAD_DIM, HEAD_DIM), 1.0 / HEAD_DIM, np.float32))
    return jnp.asarray(m, BF16)


def _head_sum_matrix():
    m = np.kron(np.eye(N_HEADS, dtype=np.float32), np.ones((HEAD_DIM, HEAD_DIM), np.float32))
    return jnp.asarray(m, BF16)


def _rel_bucket_np(dist):
    max_exact = REL_BUCKETS // 2
    n = np.maximum(dist, 1).astype(np.float32)
    large = max_exact + (np.log(n / np.float32(max_exact)) / np.float32(math.log(REL_MAX_DIST / max_exact))
                         * np.float32(REL_BUCKETS - max_exact)).astype(np.int32)
    large = np.minimum(large, REL_BUCKETS - 1)
    return np.where(dist < max_exact, dist, large)


def _band_tables():
    qi = np.arange(BAND)[:, None]
    kj = np.arange(2 * BAND)[None, :]
    dist_q = qi + BAND - kj
    qq = np.arange(2 * BAND)[:, None]
    kk = np.arange(BAND)[None, :]
    dist_k = qq - kk
    out = []
    for dist in (dist_q, dist_k):
        valid = (dist >= 0) & (dist <= BAND)
        bucket = np.stack([_rel_bucket_np(np.clip(dist, 0, BAND) * d) for d in DILATIONS])
        out.append((bucket, valid))
    return out


def band_bias(rel_bias):
    out = []
    for bucket, valid in _band_tables():
        bucket = np.where(valid[None], bucket, -1)[:, None]
        bucket_dev = lax.optimization_barrier(jnp.asarray(bucket, jnp.int32))
        tab = jnp.full((len(DILATIONS), N_HEADS) + bucket.shape[2:], NEG, F32)
        for b in range(REL_BUCKETS):
            if (bucket == b).any():
                tab = jnp.where(bucket_dev == b, rel_bias[b][None, :, None, None], tab)
        out.append(tab.reshape(len(DILATIONS), N_HEADS // 2, 2 * tab.shape[2], tab.shape[3]))
    return out


LANE_TILE = 128
N_LANE_TILES = ATTN_W // LANE_TILE


def _view_shape(t, dil):
    return (t // dil, dil * ATTN_W)


def _view_spec(tm, dil):
    return _bs((tm // dil, dil * ATTN_W), lambda i: (i, 0))


def _cols_to(scr, val):
    for cc in range(N_LANE_TILES):
        scr[cc] = val[:, LANE_TILE * cc:LANE_TILE * (cc + 1)]


def _cols_from(scr):
    return jnp.concatenate([scr[cc] for cc in range(N_LANE_TILES)], axis=1)


def _write_view(scr, out_ref, dil):
    if dil == 1:
        out_ref[...] = _cols_from(scr).astype(out_ref.dtype)
        return
    rows = scr.shape[1] // dil
    for r in range(dil):
        for cc in range(N_LANE_TILES):
            c0 = r * ATTN_W + LANE_TILE * cc
            out_ref[:, c0:c0 + LANE_TILE] = scr[cc, pl.ds(r, rows, stride=dil), :].astype(out_ref.dtype)


def _read_view(scr, in_ref, dil):
    if dil == 1:
        return in_ref[...].astype(F32)
    rows = scr.shape[1] // dil
    for r in range(dil):
        for cc in range(N_LANE_TILES):
            c0 = r * ATTN_W + LANE_TILE * cc
            scr[cc, pl.ds(r, rows, stride=dil), :] = in_ref[:, c0:c0 + LANE_TILE].astype(F32)
    return _cols_from(scr)


def hyb_prep(z, q_gain, k_gain, name):
    t = z.shape[0]
    tm = _row_tile(t, 512)
    seg = _head_mean_matrix()
    nd = len(DILATIONS)

    def body(q_ref, k_ref, v_ref, qg_ref, kg_ref, seg_ref, *rest):
        outs, scr = rest[:3 * nd], rest[3 * nd]
        q = q_ref[...]
        k = k_ref[...]
        vals = (q * lax.rsqrt(_seg_dot(q * q, seg_ref[...]) + EPS) * qg_ref[...],
                k * lax.rsqrt(_seg_dot(k * k, seg_ref[...]) + EPS) * kg_ref[...],
                v_ref[...])
        for j, val in enumerate(vals):
            _cols_to(scr, val)
            for g, dil in enumerate(DILATIONS):
                _write_view(scr, outs[3 * g + j], dil)

    col = lambda c: _bs((tm, ATTN_W), lambda i: (i, c))
    vec = _bs((1, ATTN_W), lambda i: (0, 0))
    res = pl.pallas_call(
        body, name=name, grid=(t // tm,),
        in_specs=[col(3), col(4), col(5), vec, vec, _bs((ATTN_W, ATTN_W), lambda i: (0, 0))],
        out_specs=[_view_spec(tm, dil) for dil in DILATIONS for _ in range(3)],
        out_shape=[S(_view_shape(t, dil), BF16) for dil in DILATIONS for _ in range(3)],
        scratch_shapes=[pltpu.VMEM((N_LANE_TILES, tm, LANE_TILE), F32)],
        compiler_params=_cp("parallel"),
    )(z, z, z, q_gain, k_gain, seg)
    return {dil: tuple(res[3 * g:3 * g + 3]) for g, dil in enumerate(DILATIONS)}


def _lane_lo(shape):
    return lax.broadcasted_iota(jnp.int32, shape, 1) < HEAD_DIM


def _stack_heads(pair):
    lo = _lane_lo(pair.shape)
    zero = jnp.zeros_like(pair)
    return jnp.concatenate([jnp.where(lo, pair, zero), jnp.where(lo, zero, pair)], axis=0)


def _unstack_heads(st):
    rows = st.shape[0] // 2
    return jnp.where(_lane_lo((rows, st.shape[1])), st[:rows], st[rows:])


def attn_fwd(q, k, v, bias, dil, name, host=None):
    qv, kv, vv = q, k, v
    sub = q.shape[0]
    nb = sub // BAND

    def body(q_ref, kp_ref, kc_ref, vp_ref, vc_ref, b_ref, o_ref, l_ref):
        first = pl.program_id(1) == 0
        colk = lax.broadcasted_iota(jnp.int32, (2 * BAND, 2 * BAND), 1)
        for j in range(N_HEADS // 2):
            sl = slice(2 * HEAD_DIM * j, 2 * HEAD_DIM * (j + 1))
            kk = jnp.concatenate([kp_ref[:, sl], kc_ref[:, sl]], axis=0)
            vv_ = jnp.concatenate([vp_ref[:, sl], vc_ref[:, sl]], axis=0)
            s = _dot_nt(_stack_heads(q_ref[:, sl]), kk) * (HEAD_DIM ** -0.5) + b_ref[j]
            s = jnp.where(jnp.logical_and(first, colk < BAND), NEG, s)
            m = jnp.max(s, axis=-1, keepdims=True)
            p = jnp.exp(s - m)
            l = jnp.sum(p, axis=-1, keepdims=True)
            o_ref[:, sl] = _unstack_heads(_dot(p.astype(BF16), vv_) / l).astype(o_ref.dtype)
            l_ref[:, sl] = _unstack_heads(jnp.broadcast_to(m + jnp.log(l), (2 * BAND, 2 * HEAD_DIM)))

    cur = _bs((BAND, ATTN_W), lambda r, n: (n, r))
    prv = _bs((BAND, ATTN_W), lambda r, n: (jnp.maximum(n - 1, 0), r))
    return _call(
        host, body, name=name, grid=(dil, nb),
        in_specs=[cur, prv, cur, prv, cur, _bs((N_HEADS // 2, 2 * BAND, 2 * BAND), lambda r, n: (0, 0, 0))],
        out_specs=[cur, cur],
        out_shape=[S((sub, dil * ATTN_W), BF16), S((sub, dil * ATTN_W), F32)],
        compiler_params=_cp("parallel", "arbitrary"),
        args=(qv, kv, kv, vv, vv, bias))


def hyb_post(z, conv_w, os_, lses, name, host=None):
    t = z.shape[0]
    tm = _row_tile(t, 512)
    nd = len(DILATIONS)

    def body(gb_ref, gc_ref, cx_ref, gch_ref, cxh_ref, w_ref, *rest):
        o_refs, l_refs = rest[:nd], rest[nd:2 * nd]
        y_ref, ya_ref = rest[2 * nd:2 * nd + 2]
        lt_refs, scr = rest[2 * nd + 2:3 * nd + 2], rest[3 * nd + 2]
        i = pl.program_id(0)
        m = gc_ref[...] * cx_ref[...]
        mh = jnp.where(i == 0, 0.0, gch_ref[...] * cxh_ref[...])
        conv = w_ref[0:1, :] * _shift_down(m, mh, 2) + w_ref[1:2, :] * _shift_down(m, mh, 1) + w_ref[2:3, :] * m
        y_ref[0] = (gb_ref[...] * conv).astype(BF16)
        ls = [_read_view(scr, l_refs[g], dil) for g, dil in enumerate(DILATIONS)]
        mx = functools.reduce(jnp.maximum, ls)
        es = [jnp.exp(l - mx) for l in ls]
        den = functools.reduce(lambda a, b: a + b, es)
        num = es[0] * _read_view(scr, o_refs[0], DILATIONS[0])
        for g in range(1, nd):
            num = num + es[g] * _read_view(scr, o_refs[g], DILATIONS[g])
        ya = num / den
        y_ref[1] = ya.astype(BF16)
        ya_ref[...] = ya
        _cols_to(scr, mx + jnp.log(den))
        for g, dil in enumerate(DILATIONS):
            _write_view(scr, lt_refs[g], dil)

    hb = tm // 8
    col = lambda c: _bs((tm, CONV_W), lambda i: (i, c))
    halo = lambda c: _bs((8, CONV_W), lambda i: (jnp.maximum(i * hb - 1, 0), c))
    row = _bs((tm, ATTN_W), lambda i: (i, 0))
    views = [_view_spec(tm, dil) for dil in DILATIONS]
    res = _call(
        host, body, name=name, grid=(t // tm,),
        in_specs=[col(0), col(1), col(2), halo(1), halo(2), _bs((3, CONV_W), lambda i: (0, 0))] + views * 2,
        out_specs=[_bs((2, tm, ATTN_W), lambda i: (0, i, 0)), row] + views,
        out_shape=[S((2, t, ATTN_W), BF16), S((t, ATTN_W), F32)] + [S(_view_shape(t, dil), F32) for dil in DILATIONS],
        scratch_shapes=[pltpu.VMEM((N_LANE_TILES, tm, LANE_TILE), F32)],
        compiler_params=_cp("parallel"),
        args=(z, z, z, z, z, conv_w, *os_, *lses))
    return res[0], res[1], dict(zip(DILATIONS, res[2:]))


def attn_delta(dy, ya, name):
    t = ya.shape[0]
    tm = _row_tile(t, 512)
    seg = _head_sum_matrix()
    nd = len(DILATIONS)

    def body(dy_ref, ya_ref, seg_ref, *rest):
        dl_refs, db_refs, scr = rest[:nd], rest[nd:2 * nd], rest[2 * nd]
        dya = dy_ref[...]
        _cols_to(scr, _seg_dot(dya * ya_ref[...], seg_ref[...]))
        for g, dil in enumerate(DILATIONS):
            _write_view(scr, dl_refs[g], dil)
        _cols_to(scr, dya)
        for g, dil in enumerate(DILATIONS):
            _write_view(scr, db_refs[g], dil)

    row = _bs((tm, ATTN_W), lambda i: (i, 0))
    views = [_view_spec(tm, dil) for dil in DILATIONS]
    res = pl.pallas_call(
        body, name=name, grid=(t // tm,),
        in_specs=[_bs((tm, ATTN_W), lambda i: (i, 1)), row, _bs((ATTN_W, ATTN_W), lambda i: (0, 0))],
        out_specs=views * 2,
        out_shape=[S(_view_shape(t, dil), F32) for dil in DILATIONS] + [S(_view_shape(t, dil), BF16) for dil in DILATIONS],
        scratch_shapes=[pltpu.VMEM((N_LANE_TILES, tm, LANE_TILE), F32)],
        compiler_params=_cp("parallel"),
    )(dy, ya, seg)
    return dict(zip(DILATIONS, res[:nd])), dict(zip(DILATIONS, res[nd:]))


def attn_bwd_dq(q, k, v, dya, lt, delta, bias, dil, name, host=None):
    qv, kv, vv, dv_, lv, ev = q, k, v, dya, lt, delta
    sub = q.shape[0]
    nb = sub // BAND

    def body(q_ref, kp_ref, kc_ref, vp_ref, vc_ref, do_ref, l_ref, e_ref, b_ref, dq_ref, db_ref):
        r, n = pl.program_id(0), pl.program_id(1)

        @pl.when(jnp.logical_and(r == 0, n == 0))
        def _():
            db_ref[...] = jnp.zeros_like(db_ref)
        first = n == 0
        colk = lax.broadcasted_iota(jnp.int32, (2 * BAND, 2 * BAND), 1)
        for j in range(N_HEADS // 2):
            c0 = 2 * HEAD_DIM * j
            sl = slice(c0, c0 + 2 * HEAD_DIM)
            kk = jnp.concatenate([kp_ref[:, sl], kc_ref[:, sl]], axis=0)
            vv_ = jnp.concatenate([vp_ref[:, sl], vc_ref[:, sl]], axis=0)
            lse = jnp.concatenate([l_ref[:, c0:c0 + 1], l_ref[:, c0 + HEAD_DIM:c0 + HEAD_DIM + 1]], axis=0)
            dlt = jnp.concatenate([e_ref[:, c0:c0 + 1], e_ref[:, c0 + HEAD_DIM:c0 + HEAD_DIM + 1]], axis=0)
            s = _dot_nt(_stack_heads(q_ref[:, sl]), kk) * (HEAD_DIM ** -0.5) + b_ref[j]
            s = jnp.where(jnp.logical_and(first, colk < BAND), NEG, s)
            p = jnp.exp(s - lse)
            ds = p * (_dot_nt(_stack_heads(do_ref[:, sl]), vv_) - dlt)
            db_ref[j] += ds
            dq_ref[:, sl] = (_unstack_heads(_dot(ds.astype(BF16), kk)) * (HEAD_DIM ** -0.5)).astype(dq_ref.dtype)

    cur = _bs((BAND, ATTN_W), lambda r, n: (n, r))
    prv = _bs((BAND, ATTN_W), lambda r, n: (jnp.maximum(n - 1, 0), r))
    tab = _bs((N_HEADS // 2, 2 * BAND, 2 * BAND), lambda r, n: (0, 0, 0))
    dq, db = _call(
        host, body, name=name, grid=(dil, nb),
        in_specs=[cur, prv, cur, prv, cur, cur, cur, cur, tab],
        out_specs=[cur, tab],
        out_shape=[S((sub, dil * ATTN_W), BF16), S((N_HEADS // 2, 2 * BAND, 2 * BAND), F32)],
        compiler_params=_cp("arbitrary", "arbitrary"),
        args=(qv, kv, kv, vv, vv, dv_, lv, ev, bias))
    return dq, db.reshape(N_HEADS, BAND, 2 * BAND)


def attn_bwd_dkv(q, k, v, dya, lt, delta, bias_k, dil, name, host=None):
    qv, kv, vv, dv_, lv, ev = q, k, v, dya, lt, delta
    sub = q.shape[0]
    nb = sub // BAND

    def body(k_ref, v_ref, qc_ref, qn_ref, dc_ref, dn_ref, lc_ref, ln_ref, ec_ref, en_ref, b_ref, dk_ref, dv_ref):
        last = pl.program_id(1) == nb - 1
        rowq = lax.broadcasted_iota(jnp.int32, (4 * BAND, BAND), 0)
        from_next = (rowq & BAND) != 0
        for j in range(N_HEADS // 2):
            c0 = 2 * HEAD_DIM * j
            sl = slice(c0, c0 + 2 * HEAD_DIM)
            kp, vp = k_ref[:, sl], v_ref[:, sl]
            q4 = _stack_heads(jnp.concatenate([qc_ref[:, sl], qn_ref[:, sl]], axis=0))
            do4 = _stack_heads(jnp.concatenate([dc_ref[:, sl], dn_ref[:, sl]], axis=0))
            lse = jnp.concatenate([ref[:, c:c + 1] for c in (c0, c0 + HEAD_DIM) for ref in (lc_ref, ln_ref)], axis=0)
            dlt = jnp.concatenate([ref[:, c:c + 1] for c in (c0, c0 + HEAD_DIM) for ref in (ec_ref, en_ref)], axis=0)
            s = _dot_nt(q4, kp) * (HEAD_DIM ** -0.5) + b_ref[j]
            s = jnp.where(jnp.logical_and(last, from_next), NEG, s)
            p = jnp.exp(s - lse)
            ds = p * (_dot_nt(do4, vp) - dlt)
            dv_ref[:, sl] = _dot_tn(p.astype(BF16), do4).astype(dv_ref.dtype)
            dk_ref[:, sl] = (_dot_tn(ds.astype(BF16), q4) * (HEAD_DIM ** -0.5)).astype(dk_ref.dtype)

    cur = _bs((BAND, ATTN_W), lambda r, n: (n, r))
    nxt = _bs((BAND, ATTN_W), lambda r, n: (jnp.minimum(n + 1, nb - 1), r))
    tab = _bs((N_HEADS // 2, 4 * BAND, BAND), lambda r, n: (0, 0, 0))
    return _call(
        host, body, name=name, grid=(dil, nb),
        in_specs=[cur, cur, cur, nxt, cur, nxt, cur, nxt, cur, nxt, tab],
        out_specs=[cur, cur],
        out_shape=[S((sub, dil * ATTN_W), BF16)] * 2,
        compiler_params=_cp("parallel", "arbitrary"),
        args=(kv, vv, qv, qv, dv_, dv_, lv, lv, ev, ev, bias_k))


def hyb_dz(z, dy, conv_w, q_gain, k_gain, dqs, dks, dvs, name):
    t = z.shape[0]
    tm = _row_tile(t, 512)
    nt = t // tm
    seg = _head_mean_matrix()

    def body(gb_ref, gc_ref, cx_ref, q_ref, k_ref, gch_ref, cxh_ref, gbn_ref, dyc_ref, dyn_ref, w_ref, qg_ref, kg_ref, seg_ref,
             dq1, dq2, dq3, dk1, dk2, dk3, dv1, dv2, dv3, dz_ref, dw_ref, dqg_ref, dkg_ref, scr):
        i = pl.program_id(0)

        def total(parts):
            acc = _read_view(scr, parts[0], DILATIONS[0])
            for g in range(1, len(DILATIONS)):
                acc = acc + _read_view(scr, parts[g], DILATIONS[g])
            return acc

        @pl.when(i == 0)
        def _():
            dw_ref[...] = jnp.zeros_like(dw_ref)
            dqg_ref[...] = jnp.zeros_like(dqg_ref)
            dkg_ref[...] = jnp.zeros_like(dkg_ref)
        gb, gc, cx, dyc = gb_ref[...], gc_ref[...], cx_ref[...], dyc_ref[...]
        m = gc * cx
        mh = jnp.where(i == 0, 0.0, gch_ref[...] * cxh_ref[...])
        m1, m2 = _shift_down(m, mh, 1), _shift_down(m, mh, 2)
        conv = w_ref[0:1, :] * m2 + w_ref[1:2, :] * m1 + w_ref[2:3, :] * m
        dconv = dyc * gb
        dcn = jnp.where(i == nt - 1, 0.0, dyn_ref[...] * gbn_ref[...])
        dm = w_ref[2:3, :] * dconv + w_ref[1:2, :] * _shift_up(dconv, dcn, 1) + w_ref[0:1, :] * _shift_up(dconv, dcn, 2)
        dz_ref[:, 0:CONV_W] = (dyc * conv).astype(BF16)
        dz_ref[:, CONV_W:2 * CONV_W] = (dm * cx).astype(BF16)
        dz_ref[:, 2 * CONV_W:3 * CONV_W] = (dm * gc).astype(BF16)
        dw_ref[0:1, :] += jnp.sum(dconv * m2, axis=0, keepdims=True)
        dw_ref[1:2, :] += jnp.sum(dconv * m1, axis=0, keepdims=True)
        dw_ref[2:3, :] += jnp.sum(dconv * m, axis=0, keepdims=True)
        base = 3 * CONV_W
        for idx, (x_ref, g_ref, parts, dgain_ref) in enumerate(((q_ref, qg_ref, (dq1, dq2, dq3), dqg_ref),
                                                                  (k_ref, kg_ref, (dk1, dk2, dk3), dkg_ref))):
            x = x_ref[...]
            dxh = total(parts)
            r = lax.rsqrt(_seg_dot(x * x, seg_ref[...]) + EPS)
            xhat = x * r
            tt = dxh * g_ref[...]
            dx = r * (tt - xhat * _seg_dot(tt * xhat, seg_ref[...]))
            dz_ref[:, base + idx * ATTN_W:base + (idx + 1) * ATTN_W] = dx.astype(BF16)
            dgain_ref[...] += jnp.sum(dxh * xhat, axis=0, keepdims=True)
        dz_ref[:, base + 2 * ATTN_W:base + 3 * ATTN_W] = total((dv1, dv2, dv3)).astype(BF16)

    hb = tm // 8
    col = lambda c: _bs((tm, CONV_W), lambda i: (i, c))
    prev = lambda c: _bs((8, CONV_W), lambda i: (jnp.maximum(i * hb - 1, 0), c))
    nxt = lambda c: _bs((8, CONV_W), lambda i: (jnp.minimum((i + 1) * hb, t // 8 - 1), c))
    row = _bs((tm, ATTN_W), lambda i: (i, 0))
    vec = _bs((1, ATTN_W), lambda i: (0, 0))
    return pl.pallas_call(
        body, name=name, grid=(nt,),
        in_specs=[col(0), col(1), col(2), col(3), col(4), prev(1), prev(2), nxt(0), col(0), nxt(0),
                  _bs((3, CONV_W), lambda i: (0, 0)), vec, vec, _bs((ATTN_W, ATTN_W), lambda i: (0, 0))]
                 + [_view_spec(tm, dil) for dil in DILATIONS] * 3,
        out_specs=[_bs((tm, 6 * CONV_W), lambda i: (i, 0)), _bs((3, CONV_W), lambda i: (0, 0)), vec, vec],
        out_shape=[S((t, 6 * CONV_W), BF16), S((3, CONV_W), F32), S((1, ATTN_W), F32), S((1, ATTN_W), F32)],
        scratch_shapes=[pltpu.VMEM((N_LANE_TILES, tm, LANE_TILE), F32)],
        compiler_params=_cp("arbitrary"),
    )(z, z, z, z, z, z, z, z, dy, dy, conv_w, q_gain, k_gain, seg, *dqs, *dks, *dvs)


def rel_bias_grad(dbs, name):
    (bq, vq), _ = _band_tables()
    onehot = np.zeros((len(DILATIONS), REL_BUCKETS, BAND * 2 * BAND), np.float32)
    for g in range(len(DILATIONS)):
        idx = bq[g].reshape(-1)
        ok = vq.reshape(-1)
        onehot[g, idx[ok], np.nonzero(ok)[0]] = 1.0
    onehot = jnp.asarray(onehot, BF16)
    flat = [d.reshape(N_HEADS, BAND * 2 * BAND) for d in dbs]

    def body(oh_ref, d1, d2, d3, o_ref):
        acc = jnp.zeros((REL_BUCKETS, N_HEADS), F32)
        for g, d in enumerate((d1, d2, d3)):
            x = d[...]
            hi = x.astype(BF16)
            lo = (x - hi.astype(F32)).astype(BF16)
            acc += _dot_nt(oh_ref[g], hi) + _dot_nt(oh_ref[g], lo)
        o_ref[...] = acc

    full = lambda shp: _bs(shp, lambda: tuple(0 for _ in shp))
    return pl.pallas_call(
        body, name=name,
        in_specs=[full(onehot.shape)] + [full(flat[0].shape)] * 3,
        out_specs=full((REL_BUCKETS, N_HEADS)),
        out_shape=S((REL_BUCKETS, N_HEADS), F32),
        compiler_params=pltpu.CompilerParams(vmem_limit_bytes=VMEM_LIMIT),
    )(onehot, *flat)


def _lru_gates(xb, wa_ref, wx_ref, ba, bx):
    xb16 = xb.astype(BF16)
    ga = jnp.concatenate([_dot(xb16[:, LRU_BLOCK * g:LRU_BLOCK * (g + 1)], wa_ref[g]) for g in range(LRU_BLOCKS)], axis=1) + ba
    gx = jnp.concatenate([_dot(xb16[:, LRU_BLOCK * g:LRU_BLOCK * (g + 1)], wx_ref[g]) for g in range(LRU_BLOCKS)], axis=1) + bx
    return ga, gx


def _lru_coeffs(ga, gx, lam):
    sga = _sigmoid(ga)
    sp = _softplus(-lam)
    log_a = -LRU_C * sga * sp
    a = jnp.exp(log_a)
    one_m_a2 = _neg_expm1(2.0 * log_a)
    return sga, sp, a, one_m_a2, jnp.sqrt(one_m_a2), _sigmoid(gx)


def rec_fwd(z, conv_w, conv_b, wa, wx, ba, bx, lam, name, host=None):
    t = z.shape[0]
    w = z.shape[1] // 2
    tm = _row_tile(t, SCAN_ROWS)

    def body(xp_ref, xh_ref, yb_ref, cw_ref, cb_ref, wa_ref, wx_ref, ba_ref, bx_ref, lam_ref,
             xb_ref, ga_ref, gx_ref, hs_ref, out_ref, carry):
        i = pl.program_id(0)

        @pl.when(i == 0)
        def _():
            carry[...] = jnp.zeros_like(carry)
        xp = xp_ref[...]
        xh = jnp.where(i == 0, 0.0, xh_ref[...])
        xb = cb_ref[...] + cw_ref[3:4, :] * xp
        for j in range(3):
            xb = xb + cw_ref[j:j + 1, :] * _shift_down(xp, xh, 3 - j)
        ga, gx = _lru_gates(xb, wa_ref, wx_ref, ba_ref[...], bx_ref[...])
        _, _, a, _, sq, sgx = _lru_coeffs(ga, gx, lam_ref[...])
        aa, bb = a, sq * sgx * xb
        s = 1
        while s < tm:
            bb = aa * _roll_fill(bb, s, 0.0, False) + bb
            aa = aa * _roll_fill(aa, s, 1.0, False)
            s *= 2
        hs = aa * carry[0:1, :] + bb
        xb_ref[...] = xb
        ga_ref[...] = ga
        gx_ref[...] = gx
        hs_ref[...] = hs
        carry[0:1, :] = hs_ref[tm - 1:tm, :]
        gy, _ = _gelu_and_grad(yb_ref[...])
        out_ref[...] = (hs * gy).astype(BF16)

    hb = tm // 8
    row = _bs((tm, w), lambda i: (i, 0))
    vec = _bs((1, w), lambda i: (0, 0))
    wsp = _bs((LRU_BLOCKS, LRU_BLOCK, LRU_BLOCK), lambda i: (0, 0, 0))
    return _call(
        host, body, name=name, grid=(t // tm,),
        in_specs=[row, _bs((8, w), lambda i: (jnp.maximum(i * hb - 1, 0), 0)), _bs((tm, w), lambda i: (i, 1)),
                  _bs((4, w), lambda i: (0, 0)), vec, wsp, wsp, vec, vec, vec],
        out_specs=[row] * 5,
        out_shape=[S((t, w), F32)] * 4 + [S((t, w), BF16)],
        scratch_shapes=[pltpu.VMEM((8, w), F32)],
        compiler_params=_cp("arbitrary"),
        args=(z, z, z, conv_w, conv_b, wa, wx, ba, bx, lam))


def rec_bwd(d_out, z, xb, ga, gx, hs, conv_w, wa, wx, lam, name, host=None):
    t = z.shape[0]
    w = z.shape[1] // 2
    tm = _row_tile(t, SCAN_ROWS)
    nt = t // tm

    def body(do_ref, xp_ref, xph_ref, yb_ref, xb_ref, ga_ref, gx_ref, hs_ref, hsh_ref, cw_ref, wa_ref, wx_ref, lam_ref,
             dz_ref, dga_ref, dgx_ref, sm_ref, c_lam, c_a, c_dxb):
        i = pl.program_id(0)

        @pl.when(i == 0)
        def _():
            sm_ref[...] = jnp.zeros_like(sm_ref)
            c_lam[...] = jnp.zeros_like(c_lam)
            c_a[...] = jnp.zeros_like(c_a)
            c_dxb[...] = jnp.zeros_like(c_dxb)
        d_o, yb, xb, hs = do_ref[...], yb_ref[...], xb_ref[...], hs_ref[...]
        lam = lam_ref[...]
        gy, dgy = _gelu_and_grad(yb)
        dz_ref[:, w:2 * w] = (d_o * hs * dgy).astype(BF16)
        sga, sp, a, one_m_a2, sq, sgx = _lru_coeffs(ga_ref[...], gx_ref[...], lam)
        aa = _shift_up(a, c_a[...], 1)
        bb = d_o * gy
        s = 1
        while s < tm:
            bb = aa * _roll_fill(bb, s, 0.0, True) + bb
            aa = aa * _roll_fill(aa, s, 1.0, True)
            s *= 2
        lmb = aa * c_lam[0:1, :] + bb
        c_a[...] = a[0:8, :]
        c_lam[...] = lmb[0:8, :]
        hprev = _shift_down(hs, jnp.where(i == nt - 1, 0.0, hsh_ref[...]), 1)
        d_sq = lmb * sgx * xb
        d_sgx = lmb * sq * xb
        d_log_a = lmb * hprev * a - d_sq * (1.0 - one_m_a2) / sq
        dga = d_log_a * (-LRU_C * sp) * sga * (1.0 - sga)
        dgx = d_sgx * sgx * (1.0 - sgx)
        dga16, dgx16 = dga.astype(BF16), dgx.astype(BF16)
        dga_ref[...] = dga16
        dgx_ref[...] = dgx16
        dxb = lmb * sq * sgx + jnp.concatenate(
            [_dot_nt(dga16[:, LRU_BLOCK * g:LRU_BLOCK * (g + 1)], wa_ref[g]) + _dot_nt(dgx16[:, LRU_BLOCK * g:LRU_BLOCK * (g + 1)], wx_ref[g])
             for g in range(LRU_BLOCKS)], axis=1)
        nxt = c_dxb[...]
        dxp = cw_ref[3:4, :] * dxb
        for j in range(3):
            dxp = dxp + cw_ref[j:j + 1, :] * _shift_up(dxb, nxt, 3 - j)
        c_dxb[...] = dxb[0:8, :]
        dz_ref[:, 0:w] = dxp.astype(BF16)
        xp = xp_ref[...]
        xph = jnp.where(i == nt - 1, 0.0, xph_ref[...])
        sm_ref[0:1, :] += jnp.sum(dga, axis=0, keepdims=True)
        sm_ref[1:2, :] += jnp.sum(dgx, axis=0, keepdims=True)
        sm_ref[2:3, :] += jnp.sum(d_log_a * (-LRU_C * sga), axis=0, keepdims=True) * (-_sigmoid(-lam))
        sm_ref[3:4, :] += jnp.sum(dxb, axis=0, keepdims=True)
        for j in range(4):
            sm_ref[4 + j:5 + j, :] += jnp.sum(dxb * _shift_down(xp, xph, 3 - j), axis=0, keepdims=True)

    hb = tm // 8
    rev = lambda c: _bs((tm, w), lambda i: (nt - 1 - i, c))
    halo = lambda c: _bs((8, w), lambda i: (jnp.maximum((nt - 1 - i) * hb - 1, 0), c))
    vec = _bs((1, w), lambda i: (0, 0))
    wsp = _bs((LRU_BLOCKS, LRU_BLOCK, LRU_BLOCK), lambda i: (0, 0, 0))
    return _call(
        host, body, name=name, grid=(nt,),
        in_specs=[rev(0), rev(0), halo(0), rev(1), rev(0), rev(0), rev(0), rev(0), halo(0),
                  _bs((4, w), lambda i: (0, 0)), wsp, wsp, vec],
        out_specs=[_bs((tm, 2 * w), lambda i: (nt - 1 - i, 0)), rev(0), rev(0), _bs((8, w), lambda i: (0, 0))],
        out_shape=[S((t, 2 * w), BF16), S((t, w), BF16), S((t, w), BF16), S((8, w), F32)],
        scratch_shapes=[pltpu.VMEM((8, w), F32)] * 3,
        compiler_params=_cp("arbitrary"),
        args=(d_out, z, z, z, xb, ga, gx, hs, hs, conv_w, wa, wx, lam))


def ple_fwd(h, gain, wpg, layer, p, p_layer, wpp, name, target=None):
    t, d = h.shape
    pd = p.shape[2]
    nk, _, rb, _ = wpg.shape
    cb = wpp.shape[3]
    tm = _row_tile(t, 512)
    row = _bs((tm, d), lambda i: (i, 0))
    in_specs = [row, _bs((1, d), lambda i: (0, 0)), _bs((nk, None, rb, d), lambda i: (0, layer, 0, 0)),
                _bs((None, tm, pd), lambda i: (p_layer, i, 0)), _bs((nk, None, pd, cb), lambda i: (0, layer, 0, 0))]

    def forward(h_ref, g_ref, wg_ref, p_ref, wp_ref, hn_ref, gp_ref, pp_ref):
        x = h_ref[...]
        hn = (x * _rstd(x) * g_ref[...]).astype(BF16)
        gp = _dot(hn[:, 0:rb], wg_ref[0])
        for k in range(1, nk):
            gp = gp + _dot(hn[:, rb * k:rb * (k + 1)], wg_ref[k])
        pp = jnp.concatenate([_dot(p_ref[...].astype(BF16), wp_ref[k]) for k in range(nk)], axis=1)
        hn_ref[...] = hn
        gp_ref[...] = gp
        pp_ref[...] = pp
        return x + _sigmoid(gp) * pp

    if target is not None:
        def body_loss(h_ref, g_ref, wg_ref, p_ref, wp_ref, t_ref, l_ref, dy_ref, hn_ref, gp_ref, pp_ref):
            @pl.when(pl.program_id(0) == 0)
            def _():
                l_ref[...] = jnp.zeros_like(l_ref)
            err = forward(h_ref, g_ref, wg_ref, p_ref, wp_ref, hn_ref, gp_ref, pp_ref) - t_ref[...]
            dy_ref[...] = err * (1.0 / d)
            l_ref[...] += jnp.sum(jnp.sum(err * err, axis=1, keepdims=True), axis=0, keepdims=True) * (0.5 / d)

        return pl.pallas_call(
            body_loss, name=name, grid=(t // tm,),
            in_specs=in_specs + [row],
            out_specs=[_bs((1, 1), lambda i: (0, 0))] + [row] * 4,
            out_shape=[S((1, 1), F32), S((t, d), F32), S((t, d), BF16), S((t, d), F32), S((t, d), F32)],
            compiler_params=_cp("arbitrary"),
        )(h, gain, wpg, p, wpp, target)

    def body(h_ref, g_ref, wg_ref, p_ref, wp_ref, o_ref, hn_ref, gp_ref, pp_ref):
        o_ref[...] = forward(h_ref, g_ref, wg_ref, p_ref, wp_ref, hn_ref, gp_ref, pp_ref)

    return pl.pallas_call(
        body, name=name, grid=(t // tm,),
        in_specs=in_specs,
        out_specs=[row] * 4,
        out_shape=[S((t, d), F32), S((t, d), BF16), S((t, d), F32), S((t, d), F32)],
        compiler_params=_cp("parallel"),
    )(h, gain, wpg, p, wpp)


def ple_bwd(dh, h, gain, wpg, layer, gp, pp, name, host=None):
    t, d = h.shape
    nk, _, rb, _ = wpg.shape
    tm = _row_tile(t, 512)

    def body(dh_ref, h_ref, g_ref, wg_ref, gp_ref, pp_ref, o_ref, dgp_ref, dpp_ref, dg_ref):
        @pl.when(pl.program_id(0) == 0)
        def _():
            dg_ref[...] = jnp.zeros_like(dg_ref)
        d_h = dh_ref[...]
        gate = _sigmoid(gp_ref[...])
        dgp = (d_h * pp_ref[...] * gate * (1.0 - gate)).astype(BF16)
        dgp_ref[...] = dgp
        dpp_ref[...] = (d_h * gate).astype(BF16)
        dhn = jnp.concatenate([_dot_nt(dgp, wg_ref[k]) for k in range(nk)], axis=1)
        dx, dgain = _rmsnorm_bwd(h_ref[...], g_ref[...], dhn)
        o_ref[...] = d_h + dx
        dg_ref[...] += dgain

    row = _bs((tm, d), lambda i: (i, 0))
    vec = _bs((1, d), lambda i: (0, 0))
    return _call(
        host, body, name=name, grid=(t // tm,),
        in_specs=[row, row, vec, _bs((nk, None, rb, d), lambda i: (0, layer, 0, 0)), row, row],
        out_specs=[row, row, row, vec],
        out_shape=[S((t, d), F32), S((t, d), BF16), S((t, d), BF16), S((1, d), F32)],
        compiler_params=_cp("arbitrary"),
        args=(dh, h, gain, wpg, gp, pp))


def _vec(a, i):
    return a[i:i + 1]


def local_step(x, p, target, w, plan=None):
    t = x.shape[0]
    tm = _row_tile(t, 512)
    grads = {}
    if plan is not None:
        plan.grads = grads
    saved = []
    h = x
    bias_q, bias_k = band_bias(w["rel_bias"])
    qg = jnp.tile(w["hyb_q_gain"], (1, N_HEADS))
    kg = jnp.tile(w["hyb_k_gain"], (1, N_HEADS))

    def run(fn, *a, name):
        hst = plan.host(name) if plan is not None else None
        out = fn(*a, name, hst)
        if hst is not None:
            plan.done(hst)
        return out

    def lru_blocks(n):
        return jnp.transpose(w[n].reshape(N_SHARD, LRU_BLOCKS, 64, LRU_BLOCK), (1, 0, 2, 3)).reshape(LRU_BLOCKS, LRU_BLOCK, LRU_BLOCK)

    for i in range(2):
        s = {}
        s["h0"] = h
        s["hn1"], s["g1"], s["u1"], s["a1"] = run(ffn_up, h, _vec(w["ffn1_norm"], i), w[f"ffn1_w_gate/{i}"], w[f"ffn1_w_up/{i}"], 0, name=f"ffn1_up_{i}")
        h = run(ffn_down, s["a1"], w[f"ffn1_w_down/{i}"], 0, h, name=f"ffn1_down_{i}")
        s["h1"] = h
        if i == 0:
            w_hyb_in = w["hyb_w_in"].reshape(N_SHARD, D_MODEL, -1)
            w_hyb_out = w["hyb_w_out"].reshape(D_MODEL, D_MODEL)
            s["hnm"], s["z"] = run(norm_mm, h, _vec(w["mix_norm"], i), w_hyb_in, name="hyb_in")
            s["qkv"] = hyb_prep(s["z"], qg, kg, "hyb_prep")
            os_, lses = [], []
            for g, dil in enumerate(DILATIONS):
                o, l = run(attn_fwd, *s["qkv"][dil], bias_q[g], dil, name=f"attn_fwd_{dil}")
                os_.append(o)
                lses.append(l)
            s["y2"], s["ya"], s["lt"] = run(hyb_post, s["z"], w["hyb_conv_w"], os_, lses, name="hyb_post")
            h = run(functools.partial(ffn_down, scale=1.0), s["y2"], w_hyb_out.reshape(2, 1, ATTN_W, D_MODEL), 0, h, name="hyb_out")
        else:
            w_rec_in = w["rec_w_in"].reshape(N_SHARD, D_MODEL, -1)
            s["hnm"], s["z"] = run(norm_mm, h, _vec(w["mix_norm"], i), w_rec_in, name="rec_in")
            w_rec_out = w["rec_w_out"].reshape(D_MODEL, D_MODEL)
            lru_wa, lru_wx = lru_blocks("lru_wa"), lru_blocks("lru_wx")
            s["xb"], s["ga"], s["gx"], s["hs"], s["ro"] = run(
                rec_fwd, s["z"], w["rec_conv_w"], w["rec_conv_b"], lru_wa, lru_wx, w["lru_ba"], w["lru_bx"], w["lru_lambda"], name="rec_fwd")
            h = run(mm_acc, s["ro"], _bs((tm, D_MODEL), lambda r, k: (r, 0)), w_rec_out, _bs((D_MODEL, D_MODEL), lambda r, k: (0, 0)),
                    h, 1.0, 1, t, D_MODEL, tm, name="rec_out")
        s["h2"] = h
        s["hn2"], s["g2"], s["u2"], s["a2"] = run(ffn_up, h, _vec(w["ffn2_norm"], i), w[f"ffn2_w_gate/{i}"], w[f"ffn2_w_up/{i}"], 0, name=f"ffn2_up_{i}")
        h = run(ffn_down, s["a2"], w[f"ffn2_w_down/{i}"], 0, h, name=f"ffn2_down_{i}")
        s["h3"] = h
        if i == 0:
            h, s["hnp"], s["gp"], s["pp"] = ple_fwd(h, _vec(w["ple_norm"], i), w[f"ple_w_gate/{i}"], 0, p, i, w[f"ple_w_proj/{i}"], f"ple_fwd_{i}")
        else:
            loss, dh, s["hnp"], s["gp"], s["pp"] = ple_fwd(h, _vec(w["ple_norm"], i), w[f"ple_w_gate/{i}"], 0, p, i, w[f"ple_w_proj/{i}"],
                                                           f"ple_fwd_{i}", target)
        saved.append(s)

    norm_g = {n: [None, None] for n in ("ffn1_norm", "mix_norm", "ffn2_norm", "ple_norm")}
    for i in (1, 0):
        s = saved[i]
        dh_out = dh
        dh, dgp, dpp, norm_g["ple_norm"][i] = run(ple_bwd, dh_out, s["h3"], _vec(w["ple_norm"], i), w[f"ple_w_gate/{i}"], 0, s["gp"], s["pp"],
                                                  name=f"ple_bwd_{i}")
        grads["ple_w_gate"] = run(tn_mm, s["hnp"], lambda tk: _bs((tk, 256), lambda k, j: (j, k)), dgp, lambda tk: _bs((tk, D_MODEL), lambda k, j: (j, 0)),
                                  N_SHARD, t, 256, D_MODEL, S((N_SHARD, 2, 256, D_MODEL), BF16),
                                  _bs((None, None, 256, D_MODEL), lambda k, j, i=i: (k, i, 0, 0)), 1.0, grads.get("ple_w_gate"), name=f"ple_gw_gate_{i}")
        grads["ple_w_proj"] = run(tn_mm, p, lambda tk, i=i: _bs((None, tk, 256), lambda k, j: (i, j, 0)), dpp, lambda tk: _bs((tk, 256), lambda k, j: (j, k)),
                                  N_SHARD, t, 256, 256, S((N_SHARD, 2, 256, 256), BF16),
                                  _bs((None, None, 256, 256), lambda k, j, i=i: (k, i, 0, 0)), 1.0, grads.get("ple_w_proj"), name=f"ple_gw_proj_{i}")
        dh_out = dh
        dg, du, dh16 = run(ffn_bwd_act, dh_out, w[f"ffn2_w_down/{i}"], 0, s["g2"], s["u2"], name=f"ffn2_bwd_act_{i}")
        ffn_wgrads("ffn2", s["hn2"], dh16, s["a2"], dg, du, i, grads, run)
        dh, norm_g["ffn2_norm"][i] = run(ffn_bwd_in, dg, du, w[f"ffn2_w_gate/{i}"], w[f"ffn2_w_up/{i}"], 0, s["h2"], _vec(w["ffn2_norm"], i), dh_out,
                                         name=f"ffn2_bwd_in_{i}")
        dh_out = dh
        if i == 1:
            d_o = nt_mm(dh_out, w_rec_out, "rec_bwd_out")
            grads["rec_w_out"] = run(tn_mm, s["ro"], lambda tk: _bs((tk, 256), lambda k, j: (j, k)), dh_out, lambda tk: _bs((tk, D_MODEL), lambda k, j: (j, 0)),
                                     N_SHARD, t, 256, D_MODEL, S((N_SHARD, 256, D_MODEL), BF16), _bs((None, 256, D_MODEL), lambda k, j: (k, 0, 0)),
                                     1.0, None, name="rec_gw_out").reshape(N_SHARD, 1, 256, D_MODEL)
            dz, dga, dgx, small = run(rec_bwd, d_o, s["z"], s["xb"], s["ga"], s["gx"], s["hs"], w["rec_conv_w"], lru_wa, lru_wx, w["lru_lambda"],
                                      name="rec_bwd")
            blk = lambda tk: _bs((tk, LRU_BLOCK), lambda k, j: (j, k))
            for nm, dgt in (("lru_wa", dga), ("lru_wx", dgx)):
                gw = run(tn_mm, s["xb"], blk, dgt, blk, LRU_BLOCKS, t, LRU_BLOCK, LRU_BLOCK, S((LRU_BLOCKS, LRU_BLOCK, LRU_BLOCK), BF16),
                         _bs((None, LRU_BLOCK, LRU_BLOCK), lambda k, j: (k, 0, 0)), 1.0, None, name="rec_gw_" + nm)
                grads[nm] = jnp.transpose(gw.reshape(LRU_BLOCKS, N_SHARD, 64, LRU_BLOCK), (1, 0, 2, 3)).reshape(N_SHARD, 1, LRU_BLOCKS, 64, LRU_BLOCK)
            grads["lru_ba"], grads["lru_bx"], grads["lru_lambda"], grads["rec_conv_b"] = (small[r:r + 1] for r in range(4))
            grads["rec_conv_w"] = small[4:8]
            nb_, bw = N_SHARD, 512
            w_in, nm_in = w_rec_in, "rec_w_in"
        else:
            dy = nt_mm(dh_out, w_hyb_out, "hyb_bwd_out")
            grads["hyb_w_out"] = run(tn_mm, s["y2"], lambda tk: _bs((None, tk, 256), lambda k, j: (k // 2, j, k % 2)), dh_out,
                                     lambda tk: _bs((tk, D_MODEL), lambda k, j: (j, 0)),
                                     N_SHARD, t, 256, D_MODEL, S((N_SHARD, 256, D_MODEL), BF16), _bs((None, 256, D_MODEL), lambda k, j: (k, 0, 0)),
                                     1.0, None, name="hyb_gw_out").reshape(N_SHARD, 1, 256, D_MODEL)
            delta, dya = attn_delta(dy, s["ya"], "attn_delta")
            dqs, dks, dvs, dbs = [], [], [], []
            for g, dil in enumerate(DILATIONS):
                dq, db = run(attn_bwd_dq, *s["qkv"][dil], dya[dil], s["lt"][dil], delta[dil], bias_q[g], dil, name=f"attn_bwd_dq_{dil}")
                dk, dv = run(attn_bwd_dkv, *s["qkv"][dil], dya[dil], s["lt"][dil], delta[dil], bias_k[g], dil, name=f"attn_bwd_dkv_{dil}")
                dqs.append(dq); dks.append(dk); dvs.append(dv); dbs.append(db)
            grads["rel_bias"] = rel_bias_grad(dbs, "rel_bias_grad")
            dz, grads["hyb_conv_w"], dqg, dkg = hyb_dz(s["z"], dy, w["hyb_conv_w"], qg, kg, dqs, dks, dvs, "hyb_dz")
            grads["hyb_q_gain"] = jnp.sum(dqg.reshape(N_HEADS, HEAD_DIM), axis=0, keepdims=True)
            grads["hyb_k_gain"] = jnp.sum(dkg.reshape(N_HEADS, HEAD_DIM), axis=0, keepdims=True)
            nb_, bw = N_SHARD, 768
            w_in, nm_in = w_hyb_in, "hyb_w_in"
        grads[nm_in] = run(tn_mm, s["hnm"], lambda tk: _bs((tk, D_MODEL), lambda k, j: (j, 0)), dz, lambda tk, bw=bw: _bs((tk, bw), lambda k, j: (j, k)),
                           nb_, t, D_MODEL, bw, S((nb_, D_MODEL, bw), BF16), _bs((None, D_MODEL, bw), lambda k, j: (k, 0, 0)),
                           1.0, None, name=f"mix_gw_in_{i}").reshape(nb_, 1, D_MODEL, bw)
        dh, norm_g["mix_norm"][i] = run(
            nt_acc_normbwd, [(dz, _bs((_row_tile(t, BWD_IN_ROWS), nb_ * bw), lambda r: (r, 0)), w_in, _bs((nb_, D_MODEL, bw), lambda r: (0, 0, 0)),
                              lambda x_ref, k, bw=bw: x_ref[:, k * bw:(k + 1) * bw])],
            nb_, s["h1"], _vec(w["mix_norm"], i), dh_out, name=f"mix_bwd_in_{i}")
        dh_out = dh
        dg, du, dh16 = run(ffn_bwd_act, dh_out, w[f"ffn1_w_down/{i}"], 0, s["g1"], s["u1"], name=f"ffn1_bwd_act_{i}")
        ffn_wgrads("ffn1", s["hn1"], dh16, s["a1"], dg, du, i, grads, run)
        dh, norm_g["ffn1_norm"][i] = run(ffn_bwd_in, dg, du, w[f"ffn1_w_gate/{i}"], w[f"ffn1_w_up/{i}"], 0, s["h0"], _vec(w["ffn1_norm"], i), dh_out,
                                         name=f"ffn1_bwd_in_{i}")
    for n, (g0, g1) in norm_g.items():
        grads[n] = jnp.concatenate([g0, g1], axis=0)
    return loss, dh, grads


def gather_weights(shards, name):
    n = len(shards)

    def body(*refs):
        ins, outs = refs[:n], refs[n:2 * n]
        send1, recv1, send2, recv2, lsem = refs[2 * n:]
        x, y, c, k, chips, kk = _place()
        sib = (x, y, 1 - c)

        def remote(src, dst, ssem, rsem, to):
            return pltpu.make_async_remote_copy(src_ref=src, dst_ref=dst, send_sem=ssem, recv_sem=rsem, device_id=to, device_id_type=MESH)

        local = [pltpu.make_async_copy(ins[a], outs[a].at[k], lsem.at[a]) for a in range(n)]
        for cp in local:
            cp.start()
        sends = []
        for a in range(n):
            for j, chip in enumerate(chips):
                cp = remote(ins[a].at[c], outs[a].at[k, c], send1.at[3 * a + j], recv1.at[3 * a + j], (*chip, c))
                cp.start()
                sends.append(cp)
        for a in range(n):
            for j, chip in enumerate(chips):
                remote(ins[a].at[c], outs[a].at[kk[j], c], send1.at[3 * a + j], recv1.at[3 * a + j], (*chip, c)).wait_recv()
                cp = remote(outs[a].at[kk[j], c], outs[a].at[kk[j], c], send2.at[3 * a + j], recv2.at[3 * a + j], sib)
                cp.start()
                sends.append(cp)
        for a in range(n):
            for j in range(3):
                remote(outs[a].at[kk[j], 1 - c], outs[a].at[kk[j], 1 - c], send2.at[3 * a + j], recv2.at[3 * a + j], sib).wait_recv()
        for cp in sends:
            cp.wait_send()
        for cp in local:
            cp.wait()

    return pl.pallas_call(
        body, name=name,
        in_specs=[_ANY] * n, out_specs=[_ANY] * n,
        out_shape=[S((N_SHARD,) + s.shape, s.dtype) for s in shards],
        scratch_shapes=[pltpu.SemaphoreType.DMA((3 * n,))] * 4 + [pltpu.SemaphoreType.DMA((n,))],
    )(*shards)


def exchange_cores(rs, name):
    n = len(rs)

    def body(*refs):
        outs = refs[n:2 * n]
        send, recv = refs[2 * n:]
        x, y, c = lax.axis_index("x"), lax.axis_index("y"), lax.axis_index("c")
        sends = []
        for a in range(n):
            for k in range(N_SHARD):
                slot = outs[a].at[2 * k + c]
                cp = _remote(slot, slot, send.at[N_SHARD * a + k], recv.at[N_SHARD * a + k], (x, y, 1 - c))
                cp.start()
                sends.append(cp)
        for a in range(n):
            for k in range(N_SHARD):
                slot = outs[a].at[2 * k + 1 - c]
                _remote(slot, slot, send.at[N_SHARD * a + k], recv.at[N_SHARD * a + k], (x, y, 1 - c)).wait_recv()
        for cp in sends:
            cp.wait_send()

    return pl.pallas_call(
        body, name=name,
        in_specs=[_ANY] * n, out_specs=[_ANY] * n,
        out_shape=[S(r.shape, r.dtype) for r in rs],
        input_output_aliases={a: a for a in range(n)},
        scratch_shapes=[pltpu.SemaphoreType.DMA((N_SHARD * n,))] * 2,
    )(*rs)


def allgather8(a, name):
    def body(a_ref, o_ref, send, recv, lsem):
        x, y, c = lax.axis_index("x"), lax.axis_index("y"), lax.axis_index("c")
        me = 4 * x + 2 * y + c
        local = pltpu.make_async_copy(a_ref, o_ref.at[me], lsem)
        local.start()
        cps = []
        for f in range(1, N_DEV):
            fx, fy, fc = (f >> 2) & 1, (f >> 1) & 1, f & 1
            peer = (1 - x if fx else x, 1 - y if fy else y, 1 - c if fc else c)
            cp = pltpu.make_async_remote_copy(src_ref=a_ref, dst_ref=o_ref.at[me], send_sem=send.at[f - 1], recv_sem=recv.at[f - 1],
                                              device_id=peer, device_id_type=MESH)
            cp.start()
            cps.append((cp, 4 * peer[0] + 2 * peer[1] + peer[2], f))
        for cp, pidx, f in cps:
            pltpu.make_async_remote_copy(src_ref=a_ref, dst_ref=o_ref.at[pidx], send_sem=send.at[f - 1], recv_sem=recv.at[f - 1],
                                         device_id=(x, y, c), device_id_type=MESH).wait_recv()
        for cp, _, _ in cps:
            cp.wait_send()
        local.wait()

    return pl.pallas_call(
        body, name=name, in_specs=[_ANY], out_specs=_ANY,
        out_shape=S((N_DEV,) + a.shape, a.dtype),
        scratch_shapes=[pltpu.SemaphoreType.DMA((N_DEV - 1,)), pltpu.SemaphoreType.DMA((N_DEV - 1,)), pltpu.SemaphoreType.DMA],
    )(a)


def sum8(a, name):
    _, r, c = a.shape

    def body(a_ref, o_ref):
        acc = a_ref[0]
        for j in range(1, N_DEV):
            acc = acc + a_ref[j]
        o_ref[...] = acc

    return pl.pallas_call(
        body, name=name, in_specs=[_bs((N_DEV, r, c), lambda: (0, 0, 0))], out_specs=_bs((r, c), lambda: (0, 0)),
        out_shape=S((r, c), F32),
    )(a)


def adamw(w, m, v, g, name):
    nl, r, c = w.shape
    tr = _row_tile(r, 256)
    summed = g.ndim == 4

    def body(w_ref, m_ref, v_ref, g_ref, go_ref, d_ref, mo_ref, vo_ref):
        if summed:
            gr = g_ref[0].astype(F32)
            for j in range(1, N_DEV):
                gr = gr + g_ref[j].astype(F32)
        else:
            gr = g_ref[...]
        m_new = ADAM_B1 * m_ref[...] + (1.0 - ADAM_B1) * gr
        v_new = ADAM_B2 * v_ref[...] + (1.0 - ADAM_B2) * (gr * gr)
        m_hat = m_new / (1.0 - ADAM_B1 ** ADAM_STEP)
        v_hat = v_new / (1.0 - ADAM_B2 ** ADAM_STEP)
        go_ref[...] = gr
        d_ref[...] = -ADAM_LR * (m_hat / (jnp.sqrt(v_hat) + ADAM_EPS) + ADAM_WD * w_ref[...])
        mo_ref[...] = m_new
        vo_ref[...] = v_new

    row = _bs((None, tr, c), lambda l, i: (l, i, 0))
    gspec = _bs((N_DEV, None, tr, c), lambda l, i: (0, l, i, 0)) if summed else row
    return pl.pallas_call(
        body, name=name, grid=(nl, r // tr),
        in_specs=[row, row, row, gspec], out_specs=[row] * 4, out_shape=[S((nl, r, c), F32)] * 4,
        compiler_params=_cp("parallel", "parallel"),
    )(w, m, v, g)


WEIGHTS = ["rel_bias", "ffn1_norm", "ffn1_w_gate", "ffn1_w_up", "ffn1_w_down", "mix_norm", "hyb_w_in", "hyb_conv_w", "hyb_q_gain",
           "hyb_k_gain", "hyb_w_out", "rec_w_in", "rec_conv_w", "rec_conv_b", "lru_wa", "lru_ba", "lru_wx", "lru_bx", "lru_lambda",
           "rec_w_out", "ffn2_norm", "ffn2_w_gate", "ffn2_w_up", "ffn2_w_down", "ple_norm", "ple_w_gate", "ple_w_proj"]
BIG = ["ffn1_w_gate", "ffn1_w_up", "ffn1_w_down", "hyb_w_in", "hyb_w_out", "rec_w_in", "lru_wa", "lru_wx", "rec_w_out",
       "ffn2_w_gate", "ffn2_w_up", "ffn2_w_down", "ple_w_gate", "ple_w_proj"]
SMALL_SHARDED = ["hyb_conv_w", "rec_conv_w", "rec_conv_b", "lru_ba", "lru_bx", "lru_lambda"]
SMALL = ["rel_bias", "ffn1_norm", "mix_norm", "ffn2_norm", "ple_norm", "hyb_q_gain", "hyb_k_gain"] + SMALL_SHARDED
PACK_W = 1024
PER_LAYER = ["ffn1_w_gate", "ffn1_w_up", "ffn1_w_down", "ffn2_w_gate", "ffn2_w_up", "ffn2_w_down", "ple_w_gate", "ple_w_proj"]
FIRST = ["ffn1_w_gate/0", "ffn1_w_up/0"]
LAST = ["ffn1_w_down"]
GATHER_PLAN = {
    "ffn1_up_0": ["ffn1_w_down/0", "hyb_w_in"],
    "ffn1_down_0": ["hyb_w_out", "ple_w_gate/0", "ple_w_proj/0"],
    "hyb_post": ["ffn2_w_gate/0"],
    "attn_fwd_1": ["ffn2_w_up/0"],
    "attn_fwd_4": ["ffn2_w_down/0"],
    "attn_fwd_16": ["ffn1_w_gate/1"],
    "ffn2_up_0": ["ffn1_w_up/1"],
    "ffn2_down_0": ["lru_wa", "lru_wx", "rec_w_out"],
    "ffn1_up_1": ["ffn1_w_down/1", "rec_w_in"],
    "ffn1_down_1": ["ffn2_w_down/1"],
    "rec_in": ["ple_w_gate/1", "ple_w_proj/1"],
    "rec_fwd": ["ffn2_w_gate/1", "ffn2_w_up/1"],
}
SCATTER_PLAN = {
    "ple_gw_proj_1": [("ple_w_gate", 1)],
    "ffn2_bwd_act_1": [("ple_w_proj", 1)],
    "ffn2_bwd_in_1": [("ffn2_w_gate", 1)],
    "rec_bwd": [("ffn2_w_up", 1), ("ffn2_w_down", 1)],
    "mix_bwd_in_1": [("rec_w_in", 0), ("rec_w_out", 0), ("lru_wa", 0), ("lru_wx", 0)],
    "ffn1_bwd_in_1": [("ffn1_w_gate", 1)],
    "ple_bwd_0": [("ffn1_w_up", 1)],
    "ple_gw_proj_0": [("ple_w_gate", 0)],
    "ffn2_bwd_act_0": [("ple_w_proj", 0)],
    "ffn2_gw_0_gate": [("ffn1_w_down", 1)],
    "ffn2_bwd_in_0": [("ffn2_w_gate", 0)],
    "attn_bwd_dq_1": [("ffn2_w_down", 0)],
    "attn_bwd_dkv_1": [("ffn2_w_up", 0)],
    "mix_bwd_in_0": [("hyb_w_in", 0), ("hyb_w_out", 0)],
    "ffn1_gw_0_up": [("ffn1_w_gate", 0)],
    "ffn1_gw_0_down": [("ffn1_w_up", 0)],
    "ffn1_bwd_in_0": [("ffn1_w_down", 0)],
}
FORWARD_PLAN = {
    "ffn1_bwd_in_1": ["rec_w_in", "rec_w_out", "lru_wa", "lru_wx"],
    "ffn2_bwd_in_0": ["ple_w_gate", "ple_w_proj"],
    "mix_bwd_in_0": ["ffn2_w_gate", "ffn2_w_up", "ffn2_w_down"],
    "ffn1_gw_0_up": ["hyb_w_in", "hyb_w_out"],
    "ffn1_bwd_in_0": ["ffn1_w_gate", "ffn1_w_up"],
}


class Plan:
    def __init__(self, shards, w):
        self.shards, self.w, self.grads, self.landed = shards, w, None, {}

    def host(self, kname):
        if kname in GATHER_PLAN:
            h = Host("gather", [self.shards[n] for n in GATHER_PLAN[kname]])
            h.names = GATHER_PLAN[kname]
            return h
        if kname in SCATTER_PLAN or kname in FORWARD_PLAN:
            items = SCATTER_PLAN.get(kname, [])
            fwd = FORWARD_PLAN.get(kname, [])
            h = Host("scatter", [(self.grads[n], lay, self.landed.get(n)) for n, lay in items], [self.landed[n] for n in fwd])
            h.names = [n for n, _ in items] + fwd
            return h
        return None

    def done(self, h):
        for n, o in zip(h.names, h.outs):
            if h.kind == "gather":
                self.w[n] = o
            else:
                self.landed[n] = o


def _halves(a):
    if a.shape[0] == 2:
        return a
    return a.reshape((2, a.shape[1] // 2) + a.shape[2:])


def _pack_rows(arrs, width):
    rows, offs, r0 = [], [], 0
    for a in arrs:
        if a.shape[1] > width:
            a = a.reshape(-1, width)
        rows.append(jnp.pad(a, ((0, 0), (0, width - a.shape[1]))))
        offs.append(r0)
        r0 += a.shape[0]
    pad = (-r0) % 8
    if pad:
        rows.append(jnp.zeros((pad, width), F32))
    return jnp.concatenate(rows, axis=0), offs


def kernel(x, p, rel_bias, ffn1_norm, ffn1_w_gate, ffn1_w_up, ffn1_w_down, mix_norm, hyb_w_in, hyb_conv_w, hyb_q_gain, hyb_k_gain, hyb_w_out, rec_w_in, rec_conv_w, rec_conv_b, lru_wa, lru_ba, lru_wx, lru_bx, lru_lambda, rec_w_out, ffn2_norm, ffn2_w_gate, ffn2_w_up, ffn2_w_down, ple_norm, ple_w_gate, ple_w_proj, loss_target, m_rel_bias, m_ffn1_norm, m_ffn1_w_gate, m_ffn1_w_up, m_ffn1_w_down, m_mix_norm, m_hyb_w_in, m_hyb_conv_w, m_hyb_q_gain, m_hyb_k_gain, m_hyb_w_out, m_rec_w_in, m_rec_conv_w, m_rec_conv_b, m_lru_wa, m_lru_ba, m_lru_wx, m_lru_bx, m_lru_lambda, m_rec_w_out, m_ffn2_norm, m_ffn2_w_gate, m_ffn2_w_up, m_ffn2_w_down, m_ple_norm, m_ple_w_gate, m_ple_w_proj, v_rel_bias, v_ffn1_norm, v_ffn1_w_gate, v_ffn1_w_up, v_ffn1_w_down, v_mix_norm, v_hyb_w_in, v_hyb_conv_w, v_hyb_q_gain, v_hyb_k_gain, v_hyb_w_out, v_rec_w_in, v_rec_conv_w, v_rec_conv_b, v_lru_wa, v_lru_ba, v_lru_wx, v_lru_bx, v_lru_lambda, v_rec_w_out, v_ffn2_norm, v_ffn2_w_gate, v_ffn2_w_up, v_ffn2_w_down, v_ple_norm, v_ple_w_gate, v_ple_w_proj):
    given = dict(locals())
    wts = {n: given[n] for n in WEIGHTS}
    k_chip = 2 * lax.axis_index("x") + lax.axis_index("y")

    shards = {}
    for n in BIG:
        b16 = wts[n].astype(BF16)
        if n in PER_LAYER:
            shards[n + "/0"], shards[n + "/1"] = b16[0:1], b16[1:2]
        else:
            shards[n] = b16
    first = gather_weights([_halves(shards[n]) for n in FIRST], "gather_first")
    w = {n: g.reshape((N_SHARD,) + shards[n].shape) for n, g in zip(FIRST, first)}
    plan = Plan(shards, w)
    sm2d = {n: wts[n].reshape(-1, wts[n].shape[-1]) for n in SMALL_SHARDED}
    slab, offs = _pack_rows([sm2d[n] for n in SMALL_SHARDED], 256)
    slabs = allgather8(slab, "gather_small")[0::2]
    for n, o in zip(SMALL_SHARDED, offs):
        r, cw = sm2d[n].shape
        w[n] = jnp.concatenate([slabs[kc, o:o + r, :cw] for kc in range(N_SHARD)], axis=1)
    for n in SMALL:
        if n not in SMALL_SHARDED:
            w[n] = wts[n]

    loss, dx, grads = local_step(x[0], p.reshape(p.shape[0], p.shape[2], p.shape[3]), loss_target[0], w, plan)
    loss = lax.psum(loss[0, 0], ("x", "y", "c"))

    for n, r8 in zip(LAST, exchange_cores([plan.landed[n] for n in LAST], "exchange_cores")):
        plan.landed[n] = r8
    out = {}
    for n in BIG:
        r8 = plan.landed[n]
        shp = wts[n].shape
        shp3 = shp if len(shp) == 3 else (shp[0], -1, shp[-1])
        three = lambda a: a.reshape(shp3)
        res = adamw(three(wts[n]), three(given["m_" + n]), three(given["v_" + n]), r8.reshape((N_DEV,) + three(wts[n]).shape), "adamw_" + n)
        out[n] = [a.reshape(shp) for a in res]
    g2d = [grads[n].reshape(-1, grads[n].shape[-1]) if n != "rel_bias" else grads[n].reshape(1, -1) for n in SMALL]
    gslab, goffs = _pack_rows(g2d, PACK_W)
    gsum = sum8(allgather8(gslab, "gather_small_grads"), "sum_small_grads")
    for n, o, g in zip(SMALL, goffs, g2d):
        shp = wts[n].shape
        r, cw = g.shape
        gs = gsum[o:o + r, :cw]
        if n in SMALL_SHARDED:
            sw = shp[-1]
            gs = lax.dynamic_slice_in_dim(gs, k_chip * sw, sw, axis=1)
        three = lambda a: a.reshape((1, -1, shp[-1]))
        res = adamw(three(wts[n]), three(given["m_" + n]), three(given["v_" + n]), three(gs), "adamw_" + n)
        out[n] = [a.reshape(shp) for a in res]
    return (loss, dx[None], *[out[n][0] for n in WEIGHTS], *[out[n][1] for n in WEIGHTS],
            *[out[n][2] for n in WEIGHTS], *[out[n][3] for n in WEIGHTS])
```
